```python
import jax, jax.numpy as jnp
from jax import lax
import numpy as np

D_MODEL = 1024
BATCH = 16
SEQ = 256
DEPTH = 2
DEC_BATCH = 2
DEC_SEQ = 1024
PAST_LEN = 512

GRID_W = 64
MIX_W = D_MODEL
H_GLA = 4
DK_GLA = 64
DV_GLA = 128
GLA_RANK = 16
GATE_NORM = 16.0
GLA_CHUNK = 32
H_NAT = 8
HD_NAT = 64
WIN_H = 8
WIN_W = 16
Q_BLOCK_W = 16
BAND_W = Q_BLOCK_W + WIN_W
D_FF = 2816
N_MOD = 9
ROPE_BASE = 10000.0
EPS = 1e-6
NEG_INF = -1e30
ATTN_QBLOCK = 128
D_IN = 2 * H_GLA * DK_GLA + 2 * H_GLA * DV_GLA + 2 * GLA_RANK + 3 * H_NAT * HD_NAT

kernel_name = 'hymba_gla_natten_macaron_prefix_dit_step'


def rmsnorm(x, g):
    xf = x.astype(jnp.float32)
    y = xf * lax.rsqrt(jnp.mean(xf * xf, axis=-1, keepdims=True) + EPS)
    return (y * g.astype(jnp.float32)).astype(x.dtype)


def modulation(cvec, w_mod, b_mod):
    m = jax.nn.silu(cvec) @ w_mod + b_mod
    return m.reshape(cvec.shape[0], N_MOD, 1, D_MODEL)


def swiglu(h, w_in, w_out):
    gate, up = jnp.split(h @ w_in, 2, axis=-1)
    return (jax.nn.silu(gate) * up) @ w_out


def rope_axis(x, pos):
    half = x.shape[-1] // 2
    freqs = ROPE_BASE ** (-jnp.arange(half, dtype=jnp.float32) / half)
    ang = pos.astype(jnp.float32)[:, None] * freqs[None, :]
    cos = jnp.cos(ang)[:, None, :]
    sin = jnp.sin(ang)[:, None, :]
    xf = x.astype(jnp.float32)
    x1, x2 = xf[..., :half], xf[..., half:]
    return jnp.concatenate([x1 * cos - x2 * sin, x1 * sin + x2 * cos], axis=-1).astype(x.dtype)


def rope_2d(x):
    t = jnp.arange(x.shape[1])
    half = x.shape[-1] // 2
    return jnp.concatenate([rope_axis(x[..., :half], t // GRID_W),
                            rope_axis(x[..., half:], t % GRID_W)], axis=-1)


def gla_chunked(q, k, v, g, s0):
    f32 = jnp.float32
    b_, t_, h_, dk = q.shape
    dv = v.shape[-1]
    n = t_ // GLA_CHUNK
    qc = q.astype(f32).reshape(b_, n, GLA_CHUNK, h_, dk)
    kc = k.astype(f32).reshape(b_, n, GLA_CHUNK, h_, dk)
    vc = v.astype(f32).reshape(b_, n, GLA_CHUNK, h_, dv)
    cum = jnp.cumsum(g.astype(f32).reshape(b_, n, GLA_CHUNK, h_, dk), axis=2)
    last = cum[:, :, -1]
    causal = jnp.tril(jnp.ones((GLA_CHUNK, GLA_CHUNK), dtype=bool))[None, None, :, :, None, None]
    diff = cum[:, :, :, None] - cum[:, :, None, :]
    decay = jnp.where(causal, jnp.exp(jnp.where(causal, diff, 0.0)), 0.0)
    scores = jnp.einsum('bnthd,bnshd,bntshd->bnhts', qc, kc, decay)
    o_intra = jnp.einsum('bnhts,bnshv->bnthv', scores, vc)
    kv = jnp.einsum('bnshd,bnshv->bnhdv', kc * jnp.exp(last[:, :, None] - cum), vc)
    chunk_decay = jnp.exp(last)

    def step(state, inp):
        d, kv_c = inp
        return d[..., None] * state + kv_c, state

    s_final, s_enter = lax.scan(step, s0.astype(f32),
                                (chunk_decay.swapaxes(0, 1), kv.swapaxes(0, 1)))
    o_inter = jnp.einsum('bnthd,nbhdv->bnthv', qc * jnp.exp(cum), s_enter)
    return (o_intra + o_inter).reshape(b_, t_, h_, dv), s_final


def gla_bidir(q, k, v, g_f, g_b, s0_f, s0_b):
    flip = lambda a: jnp.flip(a, axis=1)
    o_f, s_f = gla_chunked(q, k, v, g_f, s0_f)
    o_b, s_b = gla_chunked(flip(q), flip(k), flip(v), flip(g_b), s0_b)
    return o_f + flip(o_b), s_f, s_b


def project(h, w_in, gla_wa2, gla_ba):
    b_, t_, _ = h.shape
    sizes = [H_GLA * DK_GLA, H_GLA * DK_GLA, H_GLA * DV_GLA, H_GLA * DV_GLA, GLA_RANK, GLA_RANK,
             H_NAT * HD_NAT, H_NAT * HD_NAT, H_NAT * HD_NAT]
    cuts = [int(c) for c in np.cumsum(sizes)[:-1]]
    q_g, k_g, v_g, r_g, lr_f, lr_b, q_n, k_n, v_n = jnp.split(h @ w_in, cuts, axis=-1)

    def gate(lr, i):
        z = (lr @ gla_wa2[i] + gla_ba[i]).astype(jnp.float32)
        return (jax.nn.log_sigmoid(z) / GATE_NORM).reshape(b_, t_, H_GLA, DK_GLA)

    heads = lambda a, nh: a.reshape(b_, t_, nh, -1)
    return (heads(q_g, H_GLA) * DK_GLA ** -0.5, heads(k_g, H_GLA), heads(v_g, H_GLA), r_g,
            gate(lr_f, 0), gate(lr_b, 1), heads(q_n, H_NAT), heads(k_n, H_NAT), heads(v_n, H_NAT))


def gla_merge(o, r, g_norm):
    b_, t_ = r.shape[:2]
    o = rmsnorm(o, g_norm).reshape(b_, t_, -1)
    return (o * jax.nn.silu(r.astype(jnp.float32))).astype(r.dtype)


def ctx_attention(q, k, v):
    b_, s_, h_, hd = q.shape
    qb = q.reshape(b_, s_ // ATTN_QBLOCK, ATTN_QBLOCK, h_, hd).swapaxes(0, 1)

    def block(qi):
        logits = jnp.einsum('bqhd,bkhd->bhqk', qi, k).astype(jnp.float32) * hd ** -0.5
        p = jax.nn.softmax(logits, axis=-1).astype(v.dtype)
        return jnp.einsum('bhqk,bkhd->bqhd', p, v)

    o = lax.map(block, qb)
    return o.swapaxes(0, 1).reshape(b_, s_, h_ * hd)


def natten_latent(q, k, v, ck, cv, rpb):
    f32 = jnp.float32
    b_, n_, h_, hd = q.shape
    rows = n_ // GRID_W
    wh = min(WIN_H, rows)
    n_cb = GRID_W // Q_BLOCK_W
    qr = jnp.arange(rows)
    key_rows = jnp.clip(qr - wh // 2, 0, rows - wh)[:, None] + jnp.arange(wh)[None, :]
    band_start = jnp.clip(jnp.arange(n_cb) * Q_BLOCK_W - WIN_W // 2, 0, GRID_W - BAND_W)
    key_cols = band_start[:, None] + jnp.arange(BAND_W)[None, :]
    qcol = jnp.arange(GRID_W).reshape(n_cb, Q_BLOCK_W)
    win_start = jnp.clip(qcol - WIN_W // 2, 0, GRID_W - WIN_W)
    rel = key_cols[:, None, :] - win_start[:, :, None]
    col_valid = (rel >= 0) & (rel < WIN_W)
    dr_idx = key_rows - qr[:, None] + (WIN_H - 1)
    dc_idx = jnp.clip(key_cols[:, None, :] - qcol[:, :, None] + (WIN_W - 1), 0, 2 * WIN_W - 2)
    bias = rpb[:, dr_idx[:, None, None, :, None], dc_idx[None, :, :, None, :]].astype(f32)
    kg = k.reshape(b_, rows, GRID_W, h_, hd)
    vg = v.reshape(b_, rows, GRID_W, h_, hd)
    k_band = kg[:, key_rows[:, :, None, None], key_cols[None, None, :, :]]
    v_band = vg[:, key_rows[:, :, None, None], key_cols[None, None, :, :]]
    q_blk = q.reshape(b_, rows, n_cb, Q_BLOCK_W, h_, hd)
    scale = hd ** -0.5
    s_win = jnp.einsum('brjihd,brajmhd->bhrjiam', q_blk, k_band).astype(f32) * scale + bias
    s_win = jnp.where(col_valid[:, :, None, :], s_win, NEG_INF)
    s_ctx = jnp.einsum('brjihd,bphd->bhrjip', q_blk, ck).astype(f32) * scale
    n_win = wh * BAND_W
    logits = jnp.concatenate([s_win.reshape(b_, h_, rows, n_cb, Q_BLOCK_W, n_win), s_ctx], axis=-1)
    p = jax.nn.softmax(logits, axis=-1).astype(v.dtype)
    p_win = p[..., :n_win].reshape(b_, h_, rows, n_cb, Q_BLOCK_W, wh, BAND_W)
    p_ctx = p[..., n_win:]
    o = (jnp.einsum('bhrjiam,brajmhd->brjihd', p_win, v_band)
         + jnp.einsum('bhrjip,bphd->brjihd', p_ctx, cv))
    return o.reshape(b_, n_, h_ * hd)


def mixer_context(h, w_in, gla_wa2, gla_ba, gla_norm_g, w_out):
    q_g, k_g, v_g, r_g, g_f, g_b, q_n, k_n, v_n = project(h, w_in, gla_wa2, gla_ba)
    zeros = jnp.zeros((h.shape[0], H_GLA, DK_GLA, DV_GLA), jnp.float32)
    o_g, s_f, s_b = gla_bidir(q_g, k_g, v_g, g_f, g_b, zeros, zeros)
    o_n = ctx_attention(q_n, k_n, v_n)
    y = jnp.concatenate([gla_merge(o_g, r_g, gla_norm_g), o_n], axis=-1) @ w_out
    state = jnp.stack([s_f, s_b], axis=1).astype(h.dtype)
    return y, (k_n, v_n, state)


def mixer_latent(h, cache_k, cache_v, state, w_in, gla_wa2, gla_ba, gla_norm_g, nat_rpb, w_out):
    q_g, k_g, v_g, r_g, g_f, g_b, q_n, k_n, v_n = project(h, w_in, gla_wa2, gla_ba)
    q_g = rope_2d(q_g)
    k_g = rope_2d(k_g)
    o_g, _, _ = gla_bidir(q_g, k_g, v_g, g_f, g_b, state[:, 0], state[:, 1])
    o_n = natten_latent(q_n, k_n, v_n, cache_k, cache_v, nat_rpb)
    y = jnp.concatenate([gla_merge(o_g, r_g, gla_norm_g), o_n], axis=-1) @ w_out
    return y, None


def macaron_layer(x, mod, norm_g, ffn_w_in, ffn_w_out, mixer):
    h = rmsnorm(x, norm_g[0]) * (1.0 + mod[:, 1]) + mod[:, 0]
    x = x + 0.5 * mod[:, 2] * rmsnorm(swiglu(h, ffn_w_in[0], ffn_w_out[0]), norm_g[1])
    h = rmsnorm(x, norm_g[2]) * (1.0 + mod[:, 4]) + mod[:, 3]
    y, extras = mixer(h)
    x = x + mod[:, 5] * rmsnorm(y, norm_g[3])
    h = rmsnorm(x, norm_g[4]) * (1.0 + mod[:, 7]) + mod[:, 6]
    x = x + 0.5 * mod[:, 8] * rmsnorm(swiglu(h, ffn_w_in[1], ffn_w_out[1]), norm_g[5])
    return x, extras


def setup_inputs(seed: int = 0) -> dict:
    key = jax.random.key(seed)
    ks = jax.random.split(key, 18)
    nrm = lambda k, shape, s=1.0: s * jax.random.normal(k, shape, jnp.float32)
    return {
        'x_prompt': nrm(ks[0], (BATCH, SEQ, D_MODEL)),
        'x_sample': nrm(ks[1], (DEC_BATCH, DEC_SEQ, D_MODEL)),
        'cache_k': nrm(ks[2], (DEC_BATCH, DEPTH, PAST_LEN, H_NAT, HD_NAT)),
        'cache_v': nrm(ks[3], (DEC_BATCH, DEPTH, PAST_LEN, H_NAT, HD_NAT)),
        'state_gla': nrm(ks[4], (DEC_BATCH, DEPTH, 2, H_GLA, DK_GLA, DV_GLA), 0.5),
        'c': nrm(ks[5], (DEC_BATCH, D_MODEL)),
        'c_ctx': nrm(ks[6], (D_MODEL,)),
        'w_mod': nrm(ks[7], (DEPTH, D_MODEL, N_MOD * D_MODEL), 0.5 * D_MODEL ** -0.5),
        'b_mod': nrm(ks[8], (DEPTH, N_MOD * D_MODEL), 0.02),
        'norm_g': 1.0 + nrm(ks[9], (DEPTH, 6, D_MODEL), 0.05),
        'ffn_w_in': nrm(ks[10], (DEPTH, 2, D_MODEL, 2 * D_FF), D_MODEL ** -0.5),
        'ffn_w_out': nrm(ks[11], (DEPTH, 2, D_FF, D_MODEL), D_FF ** -0.5),
        'w_in': nrm(ks[12], (DEPTH, D_MODEL, D_IN), D_MODEL ** -0.5),
        'gla_wa2': nrm(ks[13], (DEPTH, 2, GLA_RANK, H_GLA * DK_GLA), GLA_RANK ** -0.5),
        'gla_ba': nrm(ks[14], (DEPTH, 2, H_GLA * DK_GLA), 0.1),
        'gla_norm_g': 1.0 + nrm(ks[15], (DEPTH, DV_GLA), 0.05),
        'nat_rpb': nrm(ks[16], (DEPTH, H_NAT, 2 * WIN_H - 1, 2 * WIN_W - 1), 0.1),
        'w_out': nrm(ks[17], (DEPTH, MIX_W, D_MODEL), MIX_W ** -0.5),
    }


def reference(x_prompt, x_sample, cache_k, cache_v, state_gla, c, c_ctx, w_mod, b_mod, norm_g,
              ffn_w_in, ffn_w_out, w_in, gla_wa2, gla_ba, gla_norm_g, nat_rpb, w_out):
    xp = x_prompt
    k_list, v_list, s_list = [], [], []
    for l in range(DEPTH):
        mod = modulation(c_ctx[None, :], w_mod[l], b_mod[l])
        mixer = lambda h, l=l: mixer_context(h, w_in[l], gla_wa2[l], gla_ba[l], gla_norm_g[l], w_out[l])
        xp, (k_l, v_l, s_l) = macaron_layer(xp, mod, norm_g[l], ffn_w_in[l], ffn_w_out[l], mixer)
        k_list.append(k_l)
        v_list.append(v_l)
        s_list.append(s_l)
    new_cache_k = jnp.stack(k_list, axis=1)
    new_cache_v = jnp.stack(v_list, axis=1)
    new_state_gla = jnp.stack(s_list, axis=1)

    xs = x_sample
    for l in range(DEPTH):
        mod = modulation(c, w_mod[l], b_mod[l])
        mixer = lambda h, l=l: mixer_latent(h, cache_k[:, l], cache_v[:, l], state_gla[:, l], w_in[l],
                                            gla_wa2[l], gla_ba[l], gla_norm_g[l], nat_rpb[l], w_out[l])
        xs, _ = macaron_layer(xs, mod, norm_g[l], ffn_w_in[l], ffn_w_out[l], mixer)

    return (xp, xs, new_cache_k, new_cache_v, new_state_gla)
```

```python
import functools

import numpy as np
import jax
import jax.numpy as jnp
from jax import lax
from jax.experimental import pallas as pl
from jax.experimental.pallas import tpu as pltpu

D_MODEL = 1024
BATCH = 16
SEQ = 256
DEPTH = 2
DEC_BATCH = 2
DEC_SEQ = 1024
PAST_LEN = 512
GRID_W = 64
H_GLA = 4
DK_GLA = 64
DV_GLA = 128
GLA_RANK = 16
GATE_NORM = 16.0
H_NAT = 8
HD_NAT = 64
WIN_H = 8
WIN_W = 16
D_FF = 2816
N_MOD = 9
ROPE_BASE = 10000.0
EPS = 1e-6
NEG_INF = -1e30

F32 = jnp.float32
BF16 = jnp.bfloat16

N_CTX = BATCH * SEQ
N_LAT = DEC_BATCH * DEC_SEQ
N_ALL = N_CTX + N_LAT
N_GROUPS = 1 + DEC_BATCH
QK_W = H_GLA * DK_GLA
V_W = H_GLA * DV_GLA
NAT_W = H_NAT * HD_NAT
GRID_ROWS = DEC_SEQ // GRID_W

LANES = 128
SUBLANES = 8

TM = 512
TF = 1408
MOD_TN = 1152
GLA_T = 256
GLA_CB = 16
GLA_NB = GLA_T // GLA_CB
VMEM_LIMIT = 56 * 1024 * 1024

assert N_CTX % TM == 0 and DEC_SEQ % TM == 0 and D_FF % TF == 0 and TF % LANES == 0
assert SEQ == GLA_T and DEC_SEQ % GLA_T == 0


def _group_of_tile(i):
    return jnp.where(i < N_CTX // TM, 0, 1 + (i - N_CTX // TM) // (DEC_SEQ // TM))


def _dot(a, b):
    return jnp.dot(a, b, preferred_element_type=F32)


def _dot_nt(a, b):
    return lax.dot_general(a, b, (((1,), (1,)), ((), ())), preferred_element_type=F32)


def _dot_tn(a, b):
    return lax.dot_general(a, b, (((0,), (0,)), ((), ())), preferred_element_type=F32)


def _rms(x, g):
    ms = jnp.mean(x * x, axis=-1, keepdims=True)
    return x * lax.rsqrt(ms + EPS) * g


def _silu(x):
    return x * jax.nn.sigmoid(x)


def _mod_kernel(c_ref, w_ref, b_ref, o_ref):
    s = _silu(c_ref[...]).astype(BF16)
    o_ref[0] = _dot(s, w_ref[0].astype(BF16)) + b_ref[0]


def _modulation(cvecs, w_mod, b_mod):
    n_out = N_MOD * D_MODEL
    return pl.pallas_call(
        _mod_kernel,
        grid=(DEPTH, n_out // MOD_TN),
        in_specs=[
            pl.BlockSpec((SUBLANES, D_MODEL), lambda l, j: (0, 0)),
            pl.BlockSpec((1, D_MODEL, MOD_TN), lambda l, j: (l, 0, j)),
            pl.BlockSpec((1, 1, MOD_TN), lambda l, j: (l, 0, j)),
        ],
        out_specs=pl.BlockSpec((1, SUBLANES, MOD_TN), lambda l, j: (l, 0, j)),
        out_shape=jax.ShapeDtypeStruct((DEPTH, SUBLANES, n_out), F32),
        compiler_params=pltpu.CompilerParams(
            dimension_semantics=("arbitrary", "arbitrary"), vmem_limit_bytes=VMEM_LIMIT),
        name="modulation",
    )(cvecs, w_mod, b_mod.reshape(DEPTH, 1, n_out))


def _ffn_kernel(x_ref, mod_ref, ng_ref, wg_ref, wu_ref, wo_ref, o_ref, h_scr, acc_scr, *, m0, n0):
    j = pl.program_id(1)

    @pl.when(j == 0)
    def _():
        h = _rms(x_ref[...], ng_ref[n0:n0 + 1, :])
        h = h * (1.0 + mod_ref[0, m0 + 1:m0 + 2, :]) + mod_ref[0, m0:m0 + 1, :]
        h_scr[...] = h.astype(BF16)
        acc_scr[...] = jnp.zeros_like(acc_scr)

    h = h_scr[...]
    gate = _dot(h, wg_ref[...])
    up = _dot(h, wu_ref[...])
    act = (_silu(gate) * up).astype(BF16)
    acc_scr[...] += _dot(act, wo_ref[...])

    @pl.when(j == pl.num_programs(1) - 1)
    def _():
        y = _rms(acc_scr[...], ng_ref[n0 + 1:n0 + 2, :])
        o_ref[...] = x_ref[...] + 0.5 * mod_ref[0, m0 + 2:m0 + 3, :] * y


def _ffn(x, mod_l, ng_l, w_in_bf, w_out_bf, *, m0, n0):
    nf = D_FF // TF
    return pl.pallas_call(
        functools.partial(_ffn_kernel, m0=m0, n0=n0),
        grid=(N_ALL // TM, nf),
        in_specs=[
            pl.BlockSpec((TM, D_MODEL), lambda i, j: (i, 0)),
            pl.BlockSpec((1, N_MOD, D_MODEL), lambda i, j: (_group_of_tile(i), 0, 0)),
            pl.BlockSpec((6, D_MODEL), lambda i, j: (0, 0)),
            pl.BlockSpec((D_MODEL, TF), lambda i, j: (0, j)),
            pl.BlockSpec((D_MODEL, TF), lambda i, j: (0, j + nf)),
            pl.BlockSpec((TF, D_MODEL), lambda i, j: (j, 0)),
        ],
        out_specs=pl.BlockSpec((TM, D_MODEL), lambda i, j: (i, 0)),
        out_shape=jax.ShapeDtypeStruct((N_ALL, D_MODEL), F32),
        scratch_shapes=[pltpu.VMEM((TM, D_MODEL), BF16), pltpu.VMEM((TM, D_MODEL), F32)],
        compiler_params=pltpu.CompilerParams(
            dimension_semantics=("arbitrary", "arbitrary"), vmem_limit_bytes=VMEM_LIMIT),
        name="ffn",
    )(x, mod_l, ng_l, w_in_bf, w_in_bf, w_out_bf)


_P_QK = 0
_P_VR = 4 * QK_W
_P_NAT = _P_VR + 2 * V_W
_P_END = _P_NAT + 3 * NAT_W


def _proj_kernel(x_ref, mod_ref, ng_ref, w_ref, wlr_ref, wa_ref, ba_ref, cos_ref, sin_ref,
                 q_ref, k_ref, v_ref, r_ref, g_ref, qn_ref, kn_ref, vn_ref):
    h = _rms(x_ref[...], ng_ref[2:3, :])
    h = (h * (1.0 + mod_ref[0, 4:5, :]) + mod_ref[0, 3:4, :]).astype(BF16)

    qk = _dot(h, w_ref[:, _P_QK:_P_VR])
    cos = cos_ref[...]
    sin = sin_ref[...]
    q = qk[:, 0:QK_W] * cos + qk[:, 2 * QK_W:3 * QK_W] * sin
    q_ref[...] = q * (DK_GLA ** -0.5)
    k_ref[...] = qk[:, QK_W:2 * QK_W] * cos + qk[:, 3 * QK_W:4 * QK_W] * sin

    vr = _dot(h, w_ref[:, _P_VR:_P_NAT])
    v_ref[...] = vr[:, 0:V_W]
    r_ref[...] = vr[:, V_W:2 * V_W]

    nat = _dot(h, w_ref[:, _P_NAT:_P_END])
    qn_ref[...] = nat[:, 0:NAT_W]
    kn_ref[...] = nat[:, NAT_W:2 * NAT_W]
    vn_ref[...] = nat[:, 2 * NAT_W:3 * NAT_W]

    lr = _dot(h, wlr_ref[...]).astype(BF16)
    z = _dot(lr, wa_ref[...]) + ba_ref[...]
    g = (jnp.minimum(z, 0.0) - jnp.log1p(jnp.exp(-jnp.abs(z)))) * (1.0 / GATE_NORM)
    g_ref[0] = g[:, 0:QK_W]
    g_ref[1] = g[:, QK_W:2 * QK_W]


def _rope_table_block(i):
    lat_tiles = DEC_SEQ // TM
    return jnp.where(i < N_CTX // TM, lat_tiles, (i - N_CTX // TM) % lat_tiles)


def _project(x, mod_l, ng_l, w_big, w_lr, w_a, b_a, cos_t, sin_t):
    tile = lambda w: pl.BlockSpec((TM, w), lambda i: (i, 0))
    full = lambda a: pl.BlockSpec(a.shape, lambda i: (0,) * a.ndim)
    return pl.pallas_call(
        _proj_kernel,
        grid=(N_ALL // TM,),
        in_specs=[
            tile(D_MODEL),
            pl.BlockSpec((1, N_MOD, D_MODEL), lambda i: (_group_of_tile(i), 0, 0)),
            full(ng_l), full(w_big), full(w_lr), full(w_a), full(b_a),
            pl.BlockSpec((TM, QK_W), lambda i: (_rope_table_block(i), 0)),
            pl.BlockSpec((TM, QK_W), lambda i: (_rope_table_block(i), 0)),
        ],
        out_specs=[
            tile(QK_W), tile(QK_W), tile(V_W), tile(V_W),
            pl.BlockSpec((2, TM, QK_W), lambda i: (0, i, 0)),
            tile(NAT_W), tile(NAT_W), tile(NAT_W),
        ],
        out_shape=[
            jax.ShapeDtypeStruct((N_ALL, QK_W), F32), jax.ShapeDtypeStruct((N_ALL, QK_W), F32),
            jax.ShapeDtypeStruct((N_ALL, V_W), F32), jax.ShapeDtypeStruct((N_ALL, V_W), F32),
            jax.ShapeDtypeStruct((2, N_ALL, QK_W), F32),
            jax.ShapeDtypeStruct((N_ALL, NAT_W), F32), jax.ShapeDtypeStruct((N_ALL, NAT_W), F32),
            jax.ShapeDtypeStruct((N_ALL, NAT_W), F32),
        ],
        compiler_params=pltpu.CompilerParams(
            dimension_semantics=("arbitrary",), vmem_limit_bytes=VMEM_LIMIT),
        name="mixer_proj",
    )(x, mod_l, ng_l, w_big, w_lr, w_a, b_a, cos_t, sin_t)


def _gla_tables():
    rows = []
    seq_specs = [(b * (SEQ // GLA_T), SEQ // GLA_T) for b in range(BATCH)]
    seq_specs += [(N_CTX // GLA_T + b * (DEC_SEQ // GLA_T), DEC_SEQ // GLA_T) for b in range(DEC_BATCH)]
    for sid, (blk0, nchunk) in enumerate(seq_specs):
        for direction in (0, 1):
            order = range(nchunk) if direction == 0 else range(nchunk - 1, -1, -1)
            for n, c in enumerate(order):
                rows.append((blk0 + c, direction, int(n == 0), sid))
    return np.asarray(rows, dtype=np.int32).T.copy()


_GLA_TAB = _gla_tables()
_GLA_ITEMS = _GLA_TAB.shape[1]
_GLA_NSEQ = BATCH + DEC_BATCH
_PAD = GLA_CB
_GLA_DGROUP = 4


def _split3(x):
    hi = x.astype(BF16)
    r1 = x - hi.astype(F32)
    mid = r1.astype(BF16)
    lo = (r1 - mid.astype(F32)).astype(BF16)
    return hi, mid, lo


def _dot3(m, parts):
    return _dot(m, parts[0]) + _dot(m, parts[1]) + _dot(m, parts[2])


def _gla_item(direction, q_ref, k_ref, v_ref, g_ref, o_ref, s_scr, kpad, cpad, vpad, sbd, kv_scr):
    T, CB, NB = GLA_T, GLA_CB, GLA_NB
    fwd = direction == 0
    q = q_ref[...]
    k = k_ref[...]
    g = g_ref[0]

    row = lax.broadcasted_iota(jnp.int32, (T, T), 0)
    col = lax.broadcasted_iota(jnp.int32, (T, T), 1)
    same = (row >> 4) == (col >> 4)
    inc = (col <= row) if fwd else (col >= row)
    m_cum = jnp.where(same & inc, 1.0, 0.0).astype(BF16)
    m_rem = jnp.where(same & ~inc, 1.0, 0.0).astype(BF16)
    g3 = _split3(g)
    cum = _dot3(m_cum, g3)
    rem = _dot3(m_rem, g3)

    brow = lax.broadcasted_iota(jnp.int32, (LANES, T), 0)
    bcol = lax.broadcasted_iota(jnp.int32, (LANES, T), 1)
    m_tot = jnp.where(brow == (bcol >> 4), 1.0, 0.0).astype(BF16)
    dec_t = jnp.exp(_dot3(m_tot, g3)).T

    qt = (q * jnp.exp(cum)).astype(BF16)
    kh = (k * jnp.exp(rem)).astype(BF16)

    zpad_qk = jnp.zeros((_PAD, QK_W), F32)
    zpad_v = jnp.zeros((_PAD, V_W), F32)
    kpad[0:_PAD, :] = zpad_qk
    kpad[_PAD + T:, :] = zpad_qk
    cpad[0:_PAD, :] = zpad_qk
    cpad[_PAD + T:, :] = zpad_qk
    vpad[0:_PAD, :] = zpad_v
    vpad[_PAD + T:, :] = zpad_v
    kpad[_PAD:_PAD + T, :] = k
    cpad[_PAD:_PAD + T, :] = cum
    vpad[_PAD:_PAD + T, :] = v_ref[...]

    pos = lax.broadcasted_iota(jnp.int32, (T, QK_W), 0) & (CB - 1)
    erow = lax.broadcasted_iota(jnp.int32, (QK_W, V_W), 0)
    ecol = lax.broadcasted_iota(jnp.int32, (QK_W, V_W), 1)
    head_sum = jnp.where((erow >> 6) == (ecol >> 7), 1.0, 0.0).astype(BF16)

    offset = lambda d: _PAD - d if fwd else _PAD + d
    for d0 in range(0, CB, _GLA_DGROUP):
        ps = []
        for d in range(d0, d0 + _GLA_DGROUP):
            off = offset(d)
            valid = (pos >= d) if fwd else (pos <= CB - 1 - d)
            e = jnp.exp(jnp.where(valid, cum - cpad[off:off + T, :], NEG_INF))
            ps.append((q * kpad[off:off + T, :] * e).astype(BF16))
        w = _dot(jnp.concatenate(ps, axis=0), head_sum)
        part = None
        for n, d in enumerate(range(d0, d0 + _GLA_DGROUP)):
            off = offset(d)
            term = w[n * T:(n + 1) * T, :] * vpad[off:off + T, :]
            part = term if part is None else part + term
        if d0 == 0:
            o_ref[0] = part
        else:
            o_ref[0] += part

    v_bf = v_ref[...].astype(BF16)
    for i in range(NB):
        r0 = i * CB
        for pair in range(H_GLA // 2):
            kv = _dot_tn(kh[r0:r0 + CB, pair * LANES:(pair + 1) * LANES],
                         v_bf[r0:r0 + CB, pair * 2 * DV_GLA:(pair + 1) * 2 * DV_GLA])
            kv_scr[i, 2 * pair] = kv[0:DK_GLA, 0:DV_GLA]
            kv_scr[i, 2 * pair + 1] = kv[DK_GLA:2 * DK_GLA, DV_GLA:2 * DV_GLA]

    order = range(NB) if fwd else range(NB - 1, -1, -1)
    for h in range(H_GLA):
        s = s_scr[h]
        for i in order:
            half = h % 2
            sbd[i, h // 2, half * DK_GLA:(half + 1) * DK_GLA, half * DV_GLA:(half + 1) * DV_GLA] = s.astype(BF16)
            s = dec_t[h * DK_GLA:(h + 1) * DK_GLA, i:i + 1] * s + kv_scr[i, h]
        s_scr[h] = s

    for i in range(NB):
        r0 = i * CB
        for pair in range(H_GLA // 2):
            c0 = pair * 2 * DV_GLA
            o_ref[0, r0:r0 + CB, c0:c0 + 2 * DV_GLA] += _dot(qt[r0:r0 + CB, pair * LANES:(pair + 1) * LANES],
                                                            sbd[i, pair])


def _gla_kernel(tab_ref, q_ref, k_ref, v_ref, g_ref, s0_ref, o_ref, so_ref,
                s_scr, kpad, cpad, vpad, sbd, kv_scr):
    it = pl.program_id(0)

    @pl.when(it == 0)
    def _():
        sbd[...] = jnp.zeros_like(sbd)

    @pl.when(tab_ref[2, it] == 1)
    def _():
        s_scr[...] = s0_ref[0, 0]

    for direction in (0, 1):
        @pl.when(tab_ref[1, it] == direction)
        def _(direction=direction):
            _gla_item(direction, q_ref, k_ref, v_ref, g_ref, o_ref, s_scr, kpad, cpad, vpad, sbd, kv_scr)

    so_ref[0, 0] = s_scr[...]


def _gla(q, k, v, g, s0):
    tok = lambda w: pl.BlockSpec((GLA_T, w), lambda it, tab: (tab[0, it], 0))
    state = pl.BlockSpec((1, 1, H_GLA, DK_GLA, DV_GLA), lambda it, tab: (tab[3, it], tab[1, it], 0, 0, 0))
    grid_spec = pltpu.PrefetchScalarGridSpec(
        num_scalar_prefetch=1,
        grid=(_GLA_ITEMS,),
        in_specs=[
            tok(QK_W), tok(QK_W), tok(V_W),
            pl.BlockSpec((1, GLA_T, QK_W), lambda it, tab: (tab[1, it], tab[0, it], 0)),
            state,
        ],
        out_specs=[
            pl.BlockSpec((1, GLA_T, V_W), lambda it, tab: (tab[1, it], tab[0, it], 0)),
            state,
        ],
        scratch_shapes=[
            pltpu.VMEM((H_GLA, DK_GLA, DV_GLA), F32),
            pltpu.VMEM((GLA_T + 2 * _PAD, QK_W), F32),
            pltpu.VMEM((GLA_T + 2 * _PAD, QK_W), F32),
            pltpu.VMEM((GLA_T + 2 * _PAD, V_W), F32),
            pltpu.VMEM((GLA_NB, H_GLA // 2, 2 * DK_GLA, 2 * DV_GLA), BF16),
            pltpu.VMEM((GLA_NB, H_GLA, DK_GLA, DV_GLA), F32),
        ],
    )
    return pl.pallas_call(
        _gla_kernel,
        grid_spec=grid_spec,
        out_shape=[
            jax.ShapeDtypeStruct((2, N_ALL, V_W), F32),
            jax.ShapeDtypeStruct((_GLA_NSEQ, 2, H_GLA, DK_GLA, DV_GLA), F32),
        ],
        compiler_params=pltpu.CompilerParams(
            dimension_semantics=("arbitrary",), vmem_limit_bytes=VMEM_LIMIT),
        name="gla",
    )(jnp.asarray(_GLA_TAB), q, k, v, g, s0)


def _softmax_pv(s_list, v_list):
    m = s_list[0].max(axis=-1, keepdims=True)
    for s in s_list[1:]:
        m = jnp.maximum(m, s.max(axis=-1, keepdims=True))
    num = None
    den = None
    for s, vv in zip(s_list, v_list):
        e = jnp.exp(s - m)
        den = e.sum(axis=-1, keepdims=True) if den is None else den + e.sum(axis=-1, keepdims=True)
        pv = _dot(e.astype(BF16), vv)
        num = pv if num is None else num + pv
    return num / den


def _ctx_attn_kernel(q_ref, k_ref, v_ref, o_ref):
    scale = HD_NAT ** -0.5
    lane = lax.broadcasted_iota(jnp.int32, (SEQ, LANES), 1)
    for t in range(NAT_W // LANES):
        sl = slice(t * LANES, (t + 1) * LANES)
        qt = q_ref[:, sl]
        kt = k_ref[:, sl].astype(BF16)
        vt = v_ref[:, sl].astype(BF16)
        out = jnp.zeros((SEQ, LANES), F32)
        for half in range(LANES // HD_NAT):
            mine = (lane >> 6) == half
            s = _dot_nt(jnp.where(mine, qt, 0.0).astype(BF16), kt) * scale
            out = jnp.where(mine, _softmax_pv([s], [vt]), out)
        o_ref[:, sl] = out


def _ctx_attention(qn, kn, vn):
    spec = pl.BlockSpec((SEQ, NAT_W), lambda b: (b, 0))
    return pl.pallas_call(
        _ctx_attn_kernel,
        grid=(BATCH,),
        in_specs=[spec, spec, spec],
        out_specs=spec,
        out_shape=jax.ShapeDtypeStruct((N_CTX, NAT_W), F32),
        compiler_params=pltpu.CompilerParams(
            dimension_semantics=("arbitrary",), vmem_limit_bytes=VMEM_LIMIT),
        name="ctx_attention",
    )(qn, kn, vn)


_WIN_KEYS = WIN_H * GRID_W


def _nat_key_row0(r):
    return jnp.clip(r - WIN_H // 2, 0, GRID_ROWS - WIN_H)


def _nat_kernel(q_ref, k_ref, v_ref, ck_ref, cv_ref, bias_ref, o_ref):
    r = pl.program_id(1)
    scale = HD_NAT ** -0.5
    k0 = pl.multiple_of(_nat_key_row0(r) * GRID_W, GRID_W)
    lane = lax.broadcasted_iota(jnp.int32, (GRID_W, LANES), 1)
    for t in range(NAT_W // LANES):
        sl = slice(t * LANES, (t + 1) * LANES)
        qt = q_ref[:, sl]
        kw = k_ref[pl.ds(k0, _WIN_KEYS), sl].astype(BF16)
        vw = v_ref[pl.ds(k0, _WIN_KEYS), sl].astype(BF16)
        ck = ck_ref[0, 0, :, sl].astype(BF16)
        cv = cv_ref[0, 0, :, sl].astype(BF16)
        out = jnp.zeros((GRID_W, LANES), F32)
        for half in range(LANES // HD_NAT):
            mine = (lane >> 6) == half
            qm = jnp.where(mine, qt, 0.0).astype(BF16)
            s_win = _dot_nt(qm, kw) * scale + bias_ref[0, 2 * t + half]
            s_ctx = _dot_nt(qm, ck) * scale
            out = jnp.where(mine, _softmax_pv([s_win, s_ctx], [vw, cv]), out)
        o_ref[:, sl] = out


def _nat_bias_table(rpb):
    c = np.arange(GRID_W)
    win_start = np.clip(c - WIN_W // 2, 0, GRID_W - WIN_W)
    kc = np.arange(GRID_W)
    valid = (kc[None, :] >= win_start[:, None]) & (kc[None, :] < win_start[:, None] + WIN_W)
    dc = np.clip(kc[None, :] - c[:, None] + (WIN_W - 1), 0, 2 * WIN_W - 2)
    off = np.arange(WIN_H)
    a = np.arange(WIN_H)
    dr = a[None, :] - off[:, None] + (WIN_H - 1)
    tab = rpb[:, dr[:, :, None, None], dc[None, None, :, :]]
    tab = jnp.where(valid[None, None, None], tab, NEG_INF)
    tab = tab.transpose(1, 0, 3, 2, 4)
    return tab.reshape(WIN_H, H_NAT, GRID_W, _WIN_KEYS).astype(F32)


def _nat_attention(qn, kn, vn, cache_k, cache_v, bias, layer):
    lat0 = N_CTX // GRID_W
    return pl.pallas_call(
        _nat_kernel,
        grid=(DEC_BATCH, GRID_ROWS),
        in_specs=[
            pl.BlockSpec((GRID_W, NAT_W), lambda b, r: (lat0 + b * GRID_ROWS + r, 0)),
            pl.BlockSpec((DEC_SEQ, NAT_W), lambda b, r: (N_CTX // DEC_SEQ + b, 0)),
            pl.BlockSpec((DEC_SEQ, NAT_W), lambda b, r: (N_CTX // DEC_SEQ + b, 0)),
            pl.BlockSpec((1, 1, PAST_LEN, NAT_W), lambda b, r: (b, layer, 0, 0)),
            pl.BlockSpec((1, 1, PAST_LEN, NAT_W), lambda b, r: (b, layer, 0, 0)),
            pl.BlockSpec((1, H_NAT, GRID_W, _WIN_KEYS), lambda b, r: (r - _nat_key_row0(r), 0, 0, 0)),
        ],
        out_specs=pl.BlockSpec((GRID_W, NAT_W), lambda b, r: (b * GRID_ROWS + r, 0)),
        out_shape=jax.ShapeDtypeStruct((N_LAT, NAT_W), F32),
        compiler_params=pltpu.CompilerParams(
            dimension_semantics=("arbitrary", "arbitrary"), vmem_limit_bytes=VMEM_LIMIT),
        name="nat_attention",
    )(qn, kn, vn, cache_k, cache_v, bias)


def _out_kernel(x_ref, o_ref, r_ref, on_ref, mod_ref, ng_ref, gng_ref, w_ref, y_ref):
    og = o_ref[0] + o_ref[1]
    parts = []
    for h in range(H_GLA):
        parts.append(_rms(og[:, h * DV_GLA:(h + 1) * DV_GLA], gng_ref[...]))
    merged = jnp.concatenate(parts, axis=1) * _silu(r_ref[...])
    y = _dot(merged.astype(BF16), w_ref[0:V_W, :]) + _dot(on_ref[...].astype(BF16), w_ref[V_W:, :])
    y_ref[...] = x_ref[...] + mod_ref[0, 5:6, :] * _rms(y, ng_ref[3:4, :])


def _mixer_out(x, o_gla, r, o_nat, mod_l, ng_l, gng_l, w_out_bf):
    tile = lambda w: pl.BlockSpec((TM, w), lambda i: (i, 0))
    full = lambda a: pl.BlockSpec(a.shape, lambda i: (0,) * a.ndim)
    return pl.pallas_call(
        _out_kernel,
        grid=(N_ALL // TM,),
        in_specs=[
            tile(D_MODEL),
            pl.BlockSpec((2, TM, V_W), lambda i: (0, i, 0)),
            tile(V_W), tile(NAT_W),
            pl.BlockSpec((1, N_MOD, D_MODEL), lambda i: (_group_of_tile(i), 0, 0)),
            full(ng_l), full(gng_l), full(w_out_bf),
        ],
        out_specs=tile(D_MODEL),
        out_shape=jax.ShapeDtypeStruct((N_ALL, D_MODEL), F32),
        compiler_params=pltpu.CompilerParams(
            dimension_semantics=("arbitrary",), vmem_limit_bytes=VMEM_LIMIT),
        name="mixer_out",
    )(x, o_gla, r, o_nat, mod_l, ng_l, gng_l, w_out_bf)


def _rope_tables():
    quarter = DK_GLA // 4
    freqs = ROPE_BASE ** (-jnp.arange(quarter, dtype=F32) / quarter)
    t = jnp.arange(DEC_SEQ)
    ang_r = (t // GRID_W).astype(F32)[:, None] * freqs[None, :]
    ang_c = (t % GRID_W).astype(F32)[:, None] * freqs[None, :]
    cos_h = jnp.concatenate([jnp.cos(ang_r), jnp.cos(ang_r), jnp.cos(ang_c), jnp.cos(ang_c)], axis=1)
    sin_h = jnp.concatenate([-jnp.sin(ang_r), jnp.sin(ang_r), -jnp.sin(ang_c), jnp.sin(ang_c)], axis=1)
    cos_t = jnp.concatenate([jnp.tile(cos_h, (1, H_GLA)), jnp.ones((TM, QK_W), F32)], axis=0)
    sin_t = jnp.concatenate([jnp.tile(sin_h, (1, H_GLA)), jnp.zeros((TM, QK_W), F32)], axis=0)
    return cos_t, sin_t


def _swap_perm():
    quarter = DK_GLA // 4
    idx = np.arange(QK_W)
    blk = (idx // quarter) % 4
    return np.where(blk % 2 == 0, idx + quarter, idx - quarter)


def _pack_w_in(w_in_l, gla_wa2_l, gla_ba_l):
    sizes = [QK_W, QK_W, V_W, V_W, GLA_RANK, GLA_RANK, NAT_W, NAT_W, NAT_W]
    cuts = np.cumsum([0] + sizes)
    wq, wk, wv, wr, wlf, wlb, wqn, wkn, wvn = [w_in_l[:, cuts[i]:cuts[i + 1]] for i in range(9)]
    perm = _swap_perm()
    w_big = jnp.concatenate([wq, wk, wq[:, perm], wk[:, perm], wv, wr, wqn, wkn, wvn], axis=1).astype(BF16)
    w_lr = jnp.zeros((D_MODEL, LANES), F32).at[:, 0:GLA_RANK].set(wlf).at[:, GLA_RANK:2 * GLA_RANK].set(wlb)
    w_a = jnp.zeros((LANES, 2 * QK_W), F32)
    w_a = w_a.at[0:GLA_RANK, 0:QK_W].set(gla_wa2_l[0]).at[GLA_RANK:2 * GLA_RANK, QK_W:].set(gla_wa2_l[1])
    b_a = jnp.concatenate([gla_ba_l[0], gla_ba_l[1]])[None, :]
    return w_big, w_lr.astype(BF16), w_a.astype(BF16), b_a


def kernel(x_prompt, x_sample, cache_k, cache_v, state_gla, c, c_ctx, w_mod, b_mod, norm_g, ffn_w_in, ffn_w_out,
           w_in, gla_wa2, gla_ba, gla_norm_g, nat_rpb, w_out):
    cvecs = jnp.zeros((SUBLANES, D_MODEL), F32).at[0].set(c_ctx).at[1:1 + DEC_BATCH].set(c)
    mod = _modulation(cvecs, w_mod, b_mod)[:, :N_GROUPS].reshape(DEPTH, N_GROUPS, N_MOD, D_MODEL)

    ffn_w_in_bf = ffn_w_in.astype(BF16)
    ffn_w_out_bf = ffn_w_out.astype(BF16)
    w_out_bf = w_out.astype(BF16)
    cos_t, sin_t = _rope_tables()
    ck = cache_k.reshape(DEC_BATCH, DEPTH, PAST_LEN, NAT_W)
    cv = cache_v.reshape(DEC_BATCH, DEPTH, PAST_LEN, NAT_W)

    x = jnp.concatenate([x_prompt.reshape(N_CTX, D_MODEL), x_sample.reshape(N_LAT, D_MODEL)], axis=0)
    k_list, v_list, s_list = [], [], []
    for l in range(DEPTH):
        mod_l, ng_l = mod[l], norm_g[l]
        x = _ffn(x, mod_l, ng_l, ffn_w_in_bf[l, 0], ffn_w_out_bf[l, 0], m0=0, n0=0)

        w_big, w_lr, w_a, b_a = _pack_w_in(w_in[l], gla_wa2[l], gla_ba[l])
        q, k, v, r, g, qn, kn, vn = _project(x, mod_l, ng_l, w_big, w_lr, w_a, b_a, cos_t, sin_t)

        s0 = jnp.concatenate([jnp.zeros((BATCH, 2, H_GLA, DK_GLA, DV_GLA), F32), state_gla[:, l]], axis=0)
        o_gla, s_fin = _gla(q, k, v, g, s0)
        o_ctx = _ctx_attention(qn, kn, vn)
        o_lat = _nat_attention(qn, kn, vn, ck, cv, _nat_bias_table(nat_rpb[l]), l)
        o_nat = jnp.concatenate([o_ctx, o_lat], axis=0)

        x = _mixer_out(x, o_gla, r, o_nat, mod_l, ng_l, gla_norm_g[l][None, :], w_out_bf[l])
        x = _ffn(x, mod_l, ng_l, ffn_w_in_bf[l, 1], ffn_w_out_bf[l, 1], m0=6, n0=4)

        k_list.append(kn[:N_CTX].reshape(BATCH, SEQ, H_NAT, HD_NAT))
        v_list.append(vn[:N_CTX].reshape(BATCH, SEQ, H_NAT, HD_NAT))
        s_list.append(s_fin[:BATCH])

    y_prompt = x[:N_CTX].reshape(BATCH, SEQ, D_MODEL)
    y_sample = x[N_CTX:].reshape(DEC_BATCH, DEC_SEQ, D_MODEL)
    return (y_prompt, y_sample, jnp.stack(k_list, axis=1), jnp.stack(v_list, axis=1), jnp.stack(s_list, axis=1))
```

```python
import functools

import numpy as np
import jax
import jax.numpy as jnp
from jax import lax
from jax.experimental import pallas as pl
from jax.experimental.pallas import tpu as pltpu

D_MODEL = 1024
BATCH = 16
SEQ = 256
DEPTH = 2
DEC_BATCH = 2
DEC_SEQ = 1024
PAST_LEN = 512
GRID_W = 64
H_GLA = 4
DK_GLA = 64
DV_GLA = 128
GLA_RANK = 16
GATE_NORM = 16.0
H_NAT = 8
HD_NAT = 64
WIN_H = 8
WIN_W = 16
D_FF = 2816
N_MOD = 9
ROPE_BASE = 10000.0
EPS = 1e-6
NEG_INF = -1e30

F32 = jnp.float32
BF16 = jnp.bfloat16

N_CTX = BATCH * SEQ
N_LAT = DEC_BATCH * DEC_SEQ
N_ALL = N_CTX + N_LAT
N_GROUPS = 1 + DEC_BATCH
QK_W = H_GLA * DK_GLA
V_W = H_GLA * DV_GLA
NAT_W = H_NAT * HD_NAT
GRID_ROWS = DEC_SEQ // GRID_W

LANES = 128
SUBLANES = 8

TM = 512
TF = 1408
MOD_TN = 1152
GLA_T = 256
GLA_CB = 16
GLA_NB = GLA_T // GLA_CB
VMEM_LIMIT = 56 * 1024 * 1024

assert N_CTX % TM == 0 and DEC_SEQ % TM == 0 and D_FF % TF == 0 and TF % LANES == 0
assert SEQ == GLA_T and DEC_SEQ % GLA_T == 0


def _group_of_tile(i):
    return jnp.where(i < N_CTX // TM, 0, 1 + (i - N_CTX // TM) // (DEC_SEQ // TM))


def _dot(a, b):
    return jnp.dot(a, b, preferred_element_type=F32)


def _dot_nt(a, b):
    return lax.dot_general(a, b, (((1,), (1,)), ((), ())), preferred_element_type=F32)


def _dot_tn(a, b):
    return lax.dot_general(a, b, (((0,), (0,)), ((), ())), preferred_element_type=F32)


def _rms(x, g):
    ms = jnp.mean(x * x, axis=-1, keepdims=True)
    return x * lax.rsqrt(ms + EPS) * g


def _silu(x):
    return x * jax.nn.sigmoid(x)


def _mod_kernel(c_ref, w_ref, b_ref, o_ref):
    s = _silu(c_ref[...]).astype(BF16)
    o_ref[0] = _dot(s, w_ref[0].astype(BF16)) + b_ref[0]


def _modulation(cvecs, w_mod, b_mod):
    n_out = N_MOD * D_MODEL
    return pl.pallas_call(
        _mod_kernel,
        grid=(DEPTH, n_out // MOD_TN),
        in_specs=[
            pl.BlockSpec((SUBLANES, D_MODEL), lambda l, j: (0, 0)),
            pl.BlockSpec((1, D_MODEL, MOD_TN), lambda l, j: (l, 0, j)),
            pl.BlockSpec((1, 1, MOD_TN), lambda l, j: (l, 0, j)),
        ],
        out_specs=pl.BlockSpec((1, SUBLANES, MOD_TN), lambda l, j: (l, 0, j)),
        out_shape=jax.ShapeDtypeStruct((DEPTH, SUBLANES, n_out), F32),
        compiler_params=pltpu.CompilerParams(
            dimension_semantics=("arbitrary", "arbitrary"), vmem_limit_bytes=VMEM_LIMIT),
        name="modulation",
    )(cvecs, w_mod, b_mod.reshape(DEPTH, 1, n_out))


def _ffn_kernel(x_ref, mod_ref, ng_ref, wg_ref, wu_ref, wo_ref, o_ref, h_scr, acc_scr, *, m0, n0):
    j = pl.program_id(1)

    @pl.when(j == 0)
    def _():
        h = _rms(x_ref[...], ng_ref[n0:n0 + 1, :])
        h = h * (1.0 + mod_ref[0, m0 + 1:m0 + 2, :]) + mod_ref[0, m0:m0 + 1, :]
        h_scr[...] = h.astype(BF16)
        acc_scr[...] = jnp.zeros_like(acc_scr)

    h = h_scr[...]
    gate = _dot(h, wg_ref[0, 0])
    up = _dot(h, wu_ref[0, 0])
    act = (_silu(gate) * up).astype(BF16)
    acc_scr[...] += _dot(act, wo_ref[0, 0])

    @pl.when(j == pl.num_programs(1) - 1)
    def _():
        y = _rms(acc_scr[...], ng_ref[n0 + 1:n0 + 2, :])
        o_ref[...] = x_ref[...] + 0.5 * mod_ref[0, m0 + 2:m0 + 3, :] * y


def _ffn(x, mod_l, ng_l, w_in_bf, w_out_bf, *, layer, which):
    nf = D_FF // TF
    m0, n0 = (0, 0) if which == 0 else (6, 4)
    return pl.pallas_call(
        functools.partial(_ffn_kernel, m0=m0, n0=n0),
        grid=(N_ALL // TM, nf),
        in_specs=[
            pl.BlockSpec((TM, D_MODEL), lambda i, j: (i, 0)),
            pl.BlockSpec((1, N_MOD, D_MODEL), lambda i, j: (_group_of_tile(i), 0, 0)),
            pl.BlockSpec((6, D_MODEL), lambda i, j: (0, 0)),
            pl.BlockSpec((1, 1, D_MODEL, TF), lambda i, j: (layer, which, 0, j)),
            pl.BlockSpec((1, 1, D_MODEL, TF), lambda i, j: (layer, which, 0, j + nf)),
            pl.BlockSpec((1, 1, TF, D_MODEL), lambda i, j: (layer, which, j, 0)),
        ],
        out_specs=pl.BlockSpec((TM, D_MODEL), lambda i, j: (i, 0)),
        out_shape=jax.ShapeDtypeStruct((N_ALL, D_MODEL), F32),
        scratch_shapes=[pltpu.VMEM((TM, D_MODEL), BF16), pltpu.VMEM((TM, D_MODEL), F32)],
        compiler_params=pltpu.CompilerParams(
            dimension_semantics=("arbitrary", "arbitrary"), vmem_limit_bytes=VMEM_LIMIT),
        name="ffn",
    )(x, mod_l, ng_l, w_in_bf, w_in_bf, w_out_bf)


_P_QK = 0
_P_VR = 4 * QK_W
_P_NAT = _P_VR + 2 * V_W
_P_END = _P_NAT + 3 * NAT_W


def _proj_kernel(x_ref, mod_ref, ng_ref, w_ref, wlr_ref, wa_ref, ba_ref, cos_ref, sin_ref,
                 q_ref, k_ref, v_ref, r_ref, g_ref, qn_ref, kn_ref, vn_ref):
    h = _rms(x_ref[...], ng_ref[2:3, :])
    h = (h * (1.0 + mod_ref[0, 4:5, :]) + mod_ref[0, 3:4, :]).astype(BF16)

    qk = _dot(h, w_ref[:, _P_QK:_P_VR])
    cos = cos_ref[...]
    sin = sin_ref[...]
    q = qk[:, 0:QK_W] * cos + qk[:, 2 * QK_W:3 * QK_W] * sin
    q_ref[...] = q * (DK_GLA ** -0.5)
    k_ref[...] = qk[:, QK_W:2 * QK_W] * cos + qk[:, 3 * QK_W:4 * QK_W] * sin

    vr = _dot(h, w_ref[:, _P_VR:_P_NAT])
    v_ref[...] = vr[:, 0:V_W]
    r_ref[...] = vr[:, V_W:2 * V_W]

    nat = _dot(h, w_ref[:, _P_NAT:_P_END])
    qn_ref[...] = nat[:, 0:NAT_W]
    kn_ref[...] = nat[:, NAT_W:2 * NAT_W]
    vn_ref[...] = nat[:, 2 * NAT_W:3 * NAT_W]

    lr = _dot(h, wlr_ref[...]).astype(BF16)
    z = _dot(lr, wa_ref[...]) + ba_ref[...]
    g = (jnp.minimum(z, 0.0) - jnp.log1p(jnp.exp(-jnp.abs(z)))) * (1.0 / GATE_NORM)
    g_ref[0] = g[:, 0:QK_W]
    g_ref[1] = g[:, QK_W:2 * QK_W]


def _rope_table_block(i):
    lat_tiles = DEC_SEQ // TM
    return jnp.where(i < N_CTX // TM, lat_tiles, (i - N_CTX // TM) % lat_tiles)


def _project(x, mod_l, ng_l, w_big, w_lr, w_a, b_a, cos_t, sin_t):
    tile = lambda w: pl.BlockSpec((TM, w), lambda i: (i, 0))
    full = lambda a: pl.BlockSpec(a.shape, lambda i: (0,) * a.ndim)
    return pl.pallas_call(
        _proj_kernel,
        grid=(N_ALL // TM,),
        in_specs=[
            tile(D_MODEL),
            pl.BlockSpec((1, N_MOD, D_MODEL), lambda i: (_group_of_tile(i), 0, 0)),
            full(ng_l), full(w_big), full(w_lr), full(w_a), full(b_a),
            pl.BlockSpec((TM, QK_W), lambda i: (_rope_table_block(i), 0)),
            pl.BlockSpec((TM, QK_W), lambda i: (_rope_table_block(i), 0)),
        ],
        out_specs=[
            tile(QK_W), tile(QK_W), tile(V_W), tile(V_W),
            pl.BlockSpec((2, TM, QK_W), lambda i: (0, i, 0)),
            tile(NAT_W), tile(NAT_W), tile(NAT_W),
        ],
        out_shape=[
            jax.ShapeDtypeStruct((N_ALL, QK_W), F32), jax.ShapeDtypeStruct((N_ALL, QK_W), F32),
            jax.ShapeDtypeStruct((N_ALL, V_W), F32), jax.ShapeDtypeStruct((N_ALL, V_W), F32),
            jax.ShapeDtypeStruct((2, N_ALL, QK_W), F32),
            jax.ShapeDtypeStruct((N_ALL, NAT_W), F32), jax.ShapeDtypeStruct((N_ALL, NAT_W), F32),
            jax.ShapeDtypeStruct((N_ALL, NAT_W), F32),
        ],
        compiler_params=pltpu.CompilerParams(
            dimension_semantics=("arbitrary",), vmem_limit_bytes=VMEM_LIMIT),
        name="mixer_proj",
    )(x, mod_l, ng_l, w_big, w_lr, w_a, b_a, cos_t, sin_t)


def _gla_tables():
    rows = []
    seq_specs = [(b * (SEQ // GLA_T), SEQ // GLA_T) for b in range(BATCH)]
    seq_specs += [(N_CTX // GLA_T + b * (DEC_SEQ // GLA_T), DEC_SEQ // GLA_T) for b in range(DEC_BATCH)]
    for sid, (blk0, nchunk) in enumerate(seq_specs):
        for direction in (0, 1):
            order = range(nchunk) if direction == 0 else range(nchunk - 1, -1, -1)
            for n, c in enumerate(order):
                rows.append((blk0 + c, direction, int(n == 0), sid))
    return np.asarray(rows, dtype=np.int32).T.copy()


_GLA_TAB = _gla_tables()
_GLA_ITEMS = _GLA_TAB.shape[1]
_GLA_NSEQ = BATCH + DEC_BATCH
_PAD = GLA_CB
_GLA_DGROUP = 4


def _split3(x):
    hi = x.astype(BF16)
    r1 = x - hi.astype(F32)
    mid = r1.astype(BF16)
    lo = (r1 - mid.astype(F32)).astype(BF16)
    return hi, mid, lo


def _dot3(m, parts):
    return _dot(m, parts[0]) + _dot(m, parts[1]) + _dot(m, parts[2])


def _gla_item(direction, q_ref, k_ref, v_ref, g_ref, o_ref, s_scr, kpad, cpad, vpad, sbd, kv_scr):
    T, CB, NB = GLA_T, GLA_CB, GLA_NB
    fwd = direction == 0
    q = q_ref[...]
    k = k_ref[...]
    g = g_ref[0]

    row = lax.broadcasted_iota(jnp.int32, (T, T), 0)
    col = lax.broadcasted_iota(jnp.int32, (T, T), 1)
    same = (row >> 4) == (col >> 4)
    inc = (col <= row) if fwd else (col >= row)
    m_cum = jnp.where(same & inc, 1.0, 0.0).astype(BF16)
    m_rem = jnp.where(same & ~inc, 1.0, 0.0).astype(BF16)
    g3 = _split3(g)
    cum = _dot3(m_cum, g3)
    rem = _dot3(m_rem, g3)

    brow = lax.broadcasted_iota(jnp.int32, (LANES, T), 0)
    bcol = lax.broadcasted_iota(jnp.int32, (LANES, T), 1)
    m_tot = jnp.where(brow == (bcol >> 4), 1.0, 0.0).astype(BF16)
    dec_t = jnp.exp(_dot3(m_tot, g3)).T

    qt = (q * jnp.exp(cum)).astype(BF16)
    kh = (k * jnp.exp(rem)).astype(BF16)

    zpad_qk = jnp.zeros((_PAD, QK_W), F32)
    zpad_v = jnp.zeros((_PAD, V_W), F32)
    kpad[0:_PAD, :] = zpad_qk
    kpad[_PAD + T:, :] = zpad_qk
    cpad[0:_PAD, :] = zpad_qk
    cpad[_PAD + T:, :] = zpad_qk
    vpad[0:_PAD, :] = zpad_v
    vpad[_PAD + T:, :] = zpad_v
    kpad[_PAD:_PAD + T, :] = k
    cpad[_PAD:_PAD + T, :] = cum
    vpad[_PAD:_PAD + T, :] = v_ref[...]

    pos = lax.broadcasted_iota(jnp.int32, (T, QK_W), 0) & (CB - 1)
    erow = lax.broadcasted_iota(jnp.int32, (QK_W, V_W), 0)
    ecol = lax.broadcasted_iota(jnp.int32, (QK_W, V_W), 1)
    head_sum = jnp.where((erow >> 6) == (ecol >> 7), 1.0, 0.0).astype(BF16)

    offset = lambda d: _PAD - d if fwd else _PAD + d
    for d0 in range(0, CB, _GLA_DGROUP):
        ps = []
        for d in range(d0, d0 + _GLA_DGROUP):
            off = offset(d)
            valid = (pos >= d) if fwd else (pos <= CB - 1 - d)
            e = jnp.exp(jnp.where(valid, cum - cpad[off:off + T, :], NEG_INF))
            ps.append((q * kpad[off:off + T, :] * e).astype(BF16))
        w = _dot(jnp.concatenate(ps, axis=0), head_sum)
        part = None
        for n, d in enumerate(range(d0, d0 + _GLA_DGROUP)):
            off = offset(d)
            term = w[n * T:(n + 1) * T, :] * vpad[off:off + T, :]
            part = term if part is None else part + term
        if d0 == 0:
            o_ref[0] = part
        else:
            o_ref[0] += part

    v_bf = v_ref[...].astype(BF16)
    for i in range(NB):
        r0 = i * CB
        for pair in range(H_GLA // 2):
            kv = _dot_tn(kh[r0:r0 + CB, pair * LANES:(pair + 1) * LANES],
                         v_bf[r0:r0 + CB, pair * 2 * DV_GLA:(pair + 1) * 2 * DV_GLA])
            kv_scr[i, 2 * pair] = kv[0:DK_GLA, 0:DV_GLA]
            kv_scr[i, 2 * pair + 1] = kv[DK_GLA:2 * DK_GLA, DV_GLA:2 * DV_GLA]

    order = range(NB) if fwd else range(NB - 1, -1, -1)
    for h in range(H_GLA):
        s = s_scr[h]
        for i in order:
            half = h % 2
            sbd[i, h // 2, half * DK_GLA:(half + 1) * DK_GLA, half * DV_GLA:(half + 1) * DV_GLA] = s.astype(BF16)
            s = dec_t[h * DK_GLA:(h + 1) * DK_GLA, i:i + 1] * s + kv_scr[i, h]
        s_scr[h] = s

    for i in range(NB):
        r0 = i * CB
        for pair in range(H_GLA // 2):
            c0 = pair * 2 * DV_GLA
            o_ref[0, r0:r0 + CB, c0:c0 + 2 * DV_GLA] += _dot(qt[r0:r0 + CB, pair * LANES:(pair + 1) * LANES],
                                                            sbd[i, pair])


def _gla_kernel(tab_ref, q_ref, k_ref, v_ref, g_ref, s0_ref, o_ref, so_ref,
                s_scr, kpad, cpad, vpad, sbd, kv_scr):
    it = pl.program_id(0)

    @pl.when(it == 0)
    def _():
        sbd[...] = jnp.zeros_like(sbd)

    @pl.when(tab_ref[2, it] == 1)
    def _():
        s_scr[...] = s0_ref[0, 0]

    for direction in (0, 1):
        @pl.when(tab_ref[1, it] == direction)
        def _(direction=direction):
            _gla_item(direction, q_ref, k_ref, v_ref, g_ref, o_ref, s_scr, kpad, cpad, vpad, sbd, kv_scr)

    so_ref[0, 0] = s_scr[...]


def _gla(q, k, v, g, s0):
    tok = lambda w: pl.BlockSpec((GLA_T, w), lambda it, tab: (tab[0, it], 0))
    state = pl.BlockSpec((1, 1, H_GLA, DK_GLA, DV_GLA), lambda it, tab: (tab[3, it], tab[1, it], 0, 0, 0))
    grid_spec = pltpu.PrefetchScalarGridSpec(
        num_scalar_prefetch=1,
        grid=(_GLA_ITEMS,),
        in_specs=[
            tok(QK_W), tok(QK_W), tok(V_W),
            pl.BlockSpec((1, GLA_T, QK_W), lambda it, tab: (tab[1, it], tab[0, it], 0)),
            state,
        ],
        out_specs=[
            pl.BlockSpec((1, GLA_T, V_W), lambda it, tab: (tab[1, it], tab[0, it], 0)),
            state,
        ],
        scratch_shapes=[
            pltpu.VMEM((H_GLA, DK_GLA, DV_GLA), F32),
            pltpu.VMEM((GLA_T + 2 * _PAD, QK_W), F32),
            pltpu.VMEM((GLA_T + 2 * _PAD, QK_W), F32),
            pltpu.VMEM((GLA_T + 2 * _PAD, V_W), F32),
            pltpu.VMEM((GLA_NB, H_GLA // 2, 2 * DK_GLA, 2 * DV_GLA), BF16),
            pltpu.VMEM((GLA_NB, H_GLA, DK_GLA, DV_GLA), F32),
        ],
    )
    return pl.pallas_call(
        _gla_kernel,
        grid_spec=grid_spec,
        out_shape=[
            jax.ShapeDtypeStruct((2, N_ALL, V_W), F32),
            jax.ShapeDtypeStruct((_GLA_NSEQ, 2, H_GLA, DK_GLA, DV_GLA), F32),
        ],
        compiler_params=pltpu.CompilerParams(
            dimension_semantics=("arbitrary",), vmem_limit_bytes=VMEM_LIMIT),
        name="gla",
    )(jnp.asarray(_GLA_TAB), q, k, v, g, s0)


def _softmax_pv(s_list, v_list):
    m = s_list[0].max(axis=-1, keepdims=True)
    for s in s_list[1:]:
        m = jnp.maximum(m, s.max(axis=-1, keepdims=True))
    num = None
    den = None
    for s, vv in zip(s_list, v_list):
        e = jnp.exp(s - m)
        den = e.sum(axis=-1, keepdims=True) if den is None else den + e.sum(axis=-1, keepdims=True)
        pv = _dot(e.astype(BF16), vv)
        num = pv if num is None else num + pv
    return num / den


def _ctx_attn_kernel(q_ref, k_ref, v_ref, o_ref):
    scale = HD_NAT ** -0.5
    lane = lax.broadcasted_iota(jnp.int32, (SEQ, LANES), 1)
    for t in range(NAT_W // LANES):
        sl = slice(t * LANES, (t + 1) * LANES)
        qt = q_ref[:, sl]
        kt = k_ref[:, sl].astype(BF16)
        vt = v_ref[:, sl].astype(BF16)
        out = jnp.zeros((SEQ, LANES), F32)
        for half in range(LANES // HD_NAT):
            mine = (lane >> 6) == half
            s = _dot_nt(jnp.where(mine, qt, 0.0).astype(BF16), kt) * scale
            out = jnp.where(mine, _softmax_pv([s], [vt]), out)
        o_ref[:, sl] = out


def _ctx_attention(qn, kn, vn):
    spec = pl.BlockSpec((SEQ, NAT_W), lambda b: (b, 0))
    return pl.pallas_call(
        _ctx_attn_kernel,
        grid=(BATCH,),
        in_specs=[spec, spec, spec],
        out_specs=spec,
        out_shape=jax.ShapeDtypeStruct((N_CTX, NAT_W), F32),
        compiler_params=pltpu.CompilerParams(
            dimension_semantics=("arbitrary",), vmem_limit_bytes=VMEM_LIMIT),
        name="ctx_attention",
    )(qn, kn, vn)


_NAT_QROWS = 4
_NAT_GROUPS = GRID_ROWS // _NAT_QROWS
_NAT_KROWS = WIN_H + _NAT_QROWS
_NAT_Q = _NAT_QROWS * GRID_W
_NAT_KEYS = _NAT_KROWS * GRID_W
_NAT_DR = 2 * WIN_H - 1
_NAT_DC = 2 * WIN_W - 1


def _nat_key_row0(r):
    return jnp.clip(r - WIN_H // 2, 0, GRID_ROWS - WIN_H)


def _nat_build_bias(rpb_ref, tz_scr):
    c = lax.broadcasted_iota(jnp.int32, (GRID_W, LANES), 0)
    lane = lax.broadcasted_iota(jnp.int32, (GRID_W, LANES), 1)
    kc = lane & (GRID_W - 1)
    second = (lane >> 6) == 1
    rel = kc - c + (WIN_W - 1)
    win_start = jnp.clip(c - WIN_W // 2, 0, GRID_W - WIN_W)
    valid = (kc >= win_start) & (kc < win_start + WIN_W)

    def body(n, carry):
        dr = n >> 3
        h = n & (H_NAT - 1)
        base0 = (h * _NAT_DR + dr) * _NAT_DC
        base1 = (h * _NAT_DR + jnp.minimum(dr + 1, _NAT_DR - 1)) * _NAT_DC
        acc = jnp.full((GRID_W, LANES), NEG_INF, F32)
        for j in range(_NAT_DC):
            val = jnp.where(second, rpb_ref[base1 + j], rpb_ref[base0 + j])
            acc = jnp.where(valid & (rel == j), val, acc)
        tz_scr[dr, h] = acc
        return carry

    lax.fori_loop(0, _NAT_DR * H_NAT, body, 0)


def _nat_kernel(rpb_ref, q_ref, k_ref, v_ref, ck_ref, cv_ref, o_ref, tz_scr, bias_scr):
    grp = pl.program_id(1)

    @pl.when((pl.program_id(0) == 0) & (grp == 0))
    def _():
        _nat_build_bias(rpb_ref, tz_scr)

    scale = HD_NAT ** -0.5
    krow0 = jnp.clip(_NAT_QROWS * grp - WIN_H // 2, 0, GRID_ROWS - _NAT_KROWS)
    k0 = pl.multiple_of(krow0 * GRID_W, GRID_W)
    lane_q = lax.broadcasted_iota(jnp.int32, (_NAT_Q, LANES), 1)
    lane_b = lax.broadcasted_iota(jnp.int32, (GRID_W, LANES), 1)
    neg = jnp.full((GRID_W, LANES), NEG_INF, F32)
    for t in range(NAT_W // LANES):
        sl = slice(t * LANES, (t + 1) * LANES)
        qt = q_ref[:, sl]
        kw = k_ref[pl.ds(k0, _NAT_KEYS), sl].astype(BF16)
        vw = v_ref[pl.ds(k0, _NAT_KEYS), sl].astype(BF16)
        ck = ck_ref[0, 0, :, sl].astype(BF16)
        cv = cv_ref[0, 0, :, sl].astype(BF16)
        out = jnp.zeros((_NAT_Q, LANES), F32)
        for half in range(LANES // HD_NAT):
            h = 2 * t + half
            for qr in range(_NAT_QROWS):
                r = _NAT_QROWS * grp + qr
                lo = _nat_key_row0(r)
                for kp in range(_NAT_KROWS // 2):
                    kr = krow0 + 2 * kp
                    tile = tz_scr[jnp.clip(kr - r + (WIN_H - 1), 0, _NAT_DR - 1), h]
                    ok_a = ((kr >= lo) & (kr < lo + WIN_H)).astype(jnp.int32)
                    ok_b = ((kr + 1 >= lo) & (kr + 1 < lo + WIN_H)).astype(jnp.int32)
                    ok = jnp.where(lane_b < GRID_W, ok_a, ok_b) == 1
                    bias_scr[qr * GRID_W:(qr + 1) * GRID_W, kp * LANES:(kp + 1) * LANES] = jnp.where(ok, tile, neg)
            mine = (lane_q >> 6) == half
            qm = jnp.where(mine, qt, 0.0).astype(BF16)
            s_win = _dot_nt(qm, kw) * scale + bias_scr[...]
            s_ctx = _dot_nt(qm, ck) * scale
            out = jnp.where(mine, _softmax_pv([s_win, s_ctx], [vw, cv]), out)
        o_ref[:, sl] = out


def _nat_attention(qn, kn, vn, cache_k, cache_v, rpb_l, layer):
    lat0 = N_CTX // _NAT_Q
    return pl.pallas_call(
        _nat_kernel,
        grid=(DEC_BATCH, _NAT_GROUPS),
        in_specs=[
            pl.BlockSpec(memory_space=pltpu.SMEM),
            pl.BlockSpec((_NAT_Q, NAT_W), lambda b, g: (lat0 + b * _NAT_GROUPS + g, 0)),
            pl.BlockSpec((DEC_SEQ, NAT_W), lambda b, g: (N_CTX // DEC_SEQ + b, 0)),
            pl.BlockSpec((DEC_SEQ, NAT_W), lambda b, g: (N_CTX // DEC_SEQ + b, 0)),
            pl.BlockSpec((1, 1, PAST_LEN, NAT_W), lambda b, g: (b, layer, 0, 0)),
            pl.BlockSpec((1, 1, PAST_LEN, NAT_W), lambda b, g: (b, layer, 0, 0)),
        ],
        out_specs=pl.BlockSpec((_NAT_Q, NAT_W), lambda b, g: (b * _NAT_GROUPS + g, 0)),
        out_shape=jax.ShapeDtypeStruct((N_LAT, NAT_W), F32),
        scratch_shapes=[
            pltpu.VMEM((_NAT_DR, H_NAT, GRID_W, LANES), F32),
            pltpu.VMEM((_NAT_Q, _NAT_KEYS), F32),
        ],
        compiler_params=pltpu.CompilerParams(
            dimension_semantics=("arbitrary", "arbitrary"), vmem_limit_bytes=VMEM_LIMIT),
        name="nat_attention",
    )(rpb_l.reshape(-1), qn, kn, vn, cache_k, cache_v)


def _out_kernel(x_ref, o_ref, r_ref, on_ref, mod_ref, ng_ref, gng_ref, w_ref, y_ref):
    og = o_ref[0] + o_ref[1]
    parts = []
    for h in range(H_GLA):
        parts.append(_rms(og[:, h * DV_GLA:(h + 1) * DV_GLA], gng_ref[...]))
    merged = jnp.concatenate(parts, axis=1) * _silu(r_ref[...])
    y = _dot(merged.astype(BF16), w_ref[0:V_W, :]) + _dot(on_ref[...].astype(BF16), w_ref[V_W:, :])
    y_ref[...] = x_ref[...] + mod_ref[0, 5:6, :] * _rms(y, ng_ref[3:4, :])


def _mixer_out(x, o_gla, r, o_nat, mod_l, ng_l, gng_l, w_out_bf):
    tile = lambda w: pl.BlockSpec((TM, w), lambda i: (i, 0))
    full = lambda a: pl.BlockSpec(a.shape, lambda i: (0,) * a.ndim)
    return pl.pallas_call(
        _out_kernel,
        grid=(N_ALL // TM,),
        in_specs=[
            tile(D_MODEL),
            pl.BlockSpec((2, TM, V_W), lambda i: (0, i, 0)),
            tile(V_W), tile(NAT_W),
            pl.BlockSpec((1, N_MOD, D_MODEL), lambda i: (_group_of_tile(i), 0, 0)),
            full(ng_l), full(gng_l), full(w_out_bf),
        ],
        out_specs=tile(D_MODEL),
        out_shape=jax.ShapeDtypeStruct((N_ALL, D_MODEL), F32),
        compiler_params=pltpu.CompilerParams(
            dimension_semantics=("arbitrary",), vmem_limit_bytes=VMEM_LIMIT),
        name="mixer_out",
    )(x, o_gla, r, o_nat, mod_l, ng_l, gng_l, w_out_bf)


def _rope_tables():
    quarter = DK_GLA // 4
    freqs = ROPE_BASE ** (-jnp.arange(quarter, dtype=F32) / quarter)
    t = jnp.arange(DEC_SEQ)
    ang_r = (t // GRID_W).astype(F32)[:, None] * freqs[None, :]
    ang_c = (t % GRID_W).astype(F32)[:, None] * freqs[None, :]
    cos_h = jnp.concatenate([jnp.cos(ang_r), jnp.cos(ang_r), jnp.cos(ang_c), jnp.cos(ang_c)], axis=1)
    sin_h = jnp.concatenate([-jnp.sin(ang_r), jnp.sin(ang_r), -jnp.sin(ang_c), jnp.sin(ang_c)], axis=1)
    cos_t = jnp.concatenate([jnp.tile(cos_h, (1, H_GLA)), jnp.ones((TM, QK_W), F32)], axis=0)
    sin_t = jnp.concatenate([jnp.tile(sin_h, (1, H_GLA)), jnp.zeros((TM, QK_W), F32)], axis=0)
    return cos_t, sin_t


def _swap_perm():
    quarter = DK_GLA // 4
    idx = np.arange(QK_W)
    blk = (idx // quarter) % 4
    return np.where(blk % 2 == 0, idx + quarter, idx - quarter)


def _pack_w_in(w_in_l, gla_wa2_l, gla_ba_l):
    sizes = [QK_W, QK_W, V_W, V_W, GLA_RANK, GLA_RANK, NAT_W, NAT_W, NAT_W]
    cuts = np.cumsum([0] + sizes)
    wq, wk, wv, wr, wlf, wlb, wqn, wkn, wvn = [w_in_l[:, cuts[i]:cuts[i + 1]] for i in range(9)]
    perm = _swap_perm()
    w_big = jnp.concatenate([wq, wk, wq[:, perm], wk[:, perm], wv, wr, wqn, wkn, wvn], axis=1).astype(BF16)
    w_lr = jnp.zeros((D_MODEL, LANES), F32).at[:, 0:GLA_RANK].set(wlf).at[:, GLA_RANK:2 * GLA_RANK].set(wlb)
    w_a = jnp.zeros((LANES, 2 * QK_W), F32)
    w_a = w_a.at[0:GLA_RANK, 0:QK_W].set(gla_wa2_l[0]).at[GLA_RANK:2 * GLA_RANK, QK_W:].set(gla_wa2_l[1])
    b_a = jnp.concatenate([gla_ba_l[0], gla_ba_l[1]])[None, :]
    return w_big, w_lr.astype(BF16), w_a.astype(BF16), b_a


def kernel(x_prompt, x_sample, cache_k, cache_v, state_gla, c, c_ctx, w_mod, b_mod, norm_g, ffn_w_in, ffn_w_out,
           w_in, gla_wa2, gla_ba, gla_norm_g, nat_rpb, w_out):
    cvecs = jnp.zeros((SUBLANES, D_MODEL), F32).at[0].set(c_ctx).at[1:1 + DEC_BATCH].set(c)
    mod = _modulation(cvecs, w_mod, b_mod)[:, :N_GROUPS].reshape(DEPTH, N_GROUPS, N_MOD, D_MODEL)

    ffn_w_in_bf = ffn_w_in.astype(BF16)
    ffn_w_out_bf = ffn_w_out.astype(BF16)
    w_out_bf = w_out.astype(BF16)
    cos_t, sin_t = _rope_tables()
    ck = cache_k.reshape(DEC_BATCH, DEPTH, PAST_LEN, NAT_W)
    cv = cache_v.reshape(DEC_BATCH, DEPTH, PAST_LEN, NAT_W)

    x = jnp.concatenate([x_prompt.reshape(N_CTX, D_MODEL), x_sample.reshape(N_LAT, D_MODEL)], axis=0)
    k_list, v_list, s_list = [], [], []
    for l in range(DEPTH):
        mod_l, ng_l = mod[l], norm_g[l]
        x = _ffn(x, mod_l, ng_l, ffn_w_in_bf, ffn_w_out_bf, layer=l, which=0)

        w_big, w_lr, w_a, b_a = _pack_w_in(w_in[l], gla_wa2[l], gla_ba[l])
        q, k, v, r, g, qn, kn, vn = _project(x, mod_l, ng_l, w_big, w_lr, w_a, b_a, cos_t, sin_t)

        s0 = jnp.concatenate([jnp.zeros((BATCH, 2, H_GLA, DK_GLA, DV_GLA), F32), state_gla[:, l]], axis=0)
        o_gla, s_fin = _gla(q, k, v, g, s0)
        o_ctx = _ctx_attention(qn, kn, vn)
        o_lat = _nat_attention(qn, kn, vn, ck, cv, nat_rpb[l], l)
        o_nat = jnp.concatenate([o_ctx, o_lat], axis=0)

        x = _mixer_out(x, o_gla, r, o_nat, mod_l, ng_l, gla_norm_g[l][None, :], w_out_bf[l])
        x = _ffn(x, mod_l, ng_l, ffn_w_in_bf, ffn_w_out_bf, layer=l, which=1)

        k_list.append(kn[:N_CTX].reshape(BATCH, SEQ, H_NAT, HD_NAT))
        v_list.append(vn[:N_CTX].reshape(BATCH, SEQ, H_NAT, HD_NAT))
        s_list.append(s_fin[:BATCH])

    y_prompt = x[:N_CTX].reshape(BATCH, SEQ, D_MODEL)
    y_sample = x[N_CTX:].reshape(DEC_BATCH, DEC_SEQ, D_MODEL)
    return (y_prompt, y_sample, jnp.stack(k_list, axis=1), jnp.stack(v_list, axis=1), jnp.stack(s_list, axis=1))
```

```python
import functools

import numpy as np
import jax
import jax.numpy as jnp
from jax import lax
from jax.experimental import pallas as pl
from jax.experimental.pallas import tpu as pltpu

D_MODEL = 1024
BATCH = 16
SEQ = 256
DEPTH = 2
DEC_BATCH = 2
DEC_SEQ = 1024
PAST_LEN = 512
GRID_W = 64
H_GLA = 4
DK_GLA = 64
DV_GLA = 128
GLA_RANK = 16
GATE_NORM = 16.0
H_NAT = 8
HD_NAT = 64
WIN_H = 8
WIN_W = 16
D_FF = 2816
N_MOD = 9
ROPE_BASE = 10000.0
EPS = 1e-6
NEG_INF = -1e30

F32 = jnp.float32
BF16 = jnp.bfloat16

N_CTX = BATCH * SEQ
N_LAT = DEC_BATCH * DEC_SEQ
N_ALL = N_CTX + N_LAT
N_GROUPS = 1 + DEC_BATCH
QK_W = H_GLA * DK_GLA
V_W = H_GLA * DV_GLA
NAT_W = H_NAT * HD_NAT
GRID_ROWS = DEC_SEQ // GRID_W

LANES = 128
SUBLANES = 8

TM = 512
TF = 1408
MOD_TN = 1152
GLA_T = 256
GLA_CB = 16
GLA_NB = GLA_T // GLA_CB
VMEM_LIMIT = 56 * 1024 * 1024

assert N_CTX % TM == 0 and DEC_SEQ % TM == 0 and D_FF % TF == 0 and TF % LANES == 0
assert SEQ == GLA_T and DEC_SEQ % GLA_T == 0


def _group_of_tile(i):
    return jnp.where(i < N_CTX // TM, 0, 1 + (i - N_CTX // TM) // (DEC_SEQ // TM))


def _dot(a, b):
    return jnp.dot(a, b, preferred_element_type=F32)


def _dot_nt(a, b):
    return lax.dot_general(a, b, (((1,), (1,)), ((), ())), preferred_element_type=F32)


def _dot_tn(a, b):
    return lax.dot_general(a, b, (((0,), (0,)), ((), ())), preferred_element_type=F32)


def _rms(x, g):
    ms = jnp.mean(x * x, axis=-1, keepdims=True)
    return x * lax.rsqrt(ms + EPS) * g


def _silu(x):
    return x * jax.nn.sigmoid(x)


def _mod_kernel(c_ref, w_ref, b_ref, o_ref):
    s = _silu(c_ref[...]).astype(BF16)
    o_ref[0] = _dot(s, w_ref[0].astype(BF16)) + b_ref[0]


def _modulation(cvecs, w_mod, b_mod):
    n_out = N_MOD * D_MODEL
    return pl.pallas_call(
        _mod_kernel,
        grid=(DEPTH, n_out // MOD_TN),
        in_specs=[
            pl.BlockSpec((SUBLANES, D_MODEL), lambda l, j: (0, 0)),
            pl.BlockSpec((1, D_MODEL, MOD_TN), lambda l, j: (l, 0, j)),
            pl.BlockSpec((1, 1, MOD_TN), lambda l, j: (l, 0, j)),
        ],
        out_specs=pl.BlockSpec((1, SUBLANES, MOD_TN), lambda l, j: (l, 0, j)),
        out_shape=jax.ShapeDtypeStruct((DEPTH, SUBLANES, n_out), F32),
        compiler_params=pltpu.CompilerParams(
            dimension_semantics=("arbitrary", "arbitrary"), vmem_limit_bytes=VMEM_LIMIT),
        name="modulation",
    )(cvecs, w_mod, b_mod.reshape(DEPTH, 1, n_out))


def _ffn_kernel(x_ref, mod_ref, ng_ref, wg_ref, wu_ref, wo_ref, o_ref, h_scr, acc_scr, *, m0, n0):
    j = pl.program_id(1)

    @pl.when(j == 0)
    def _():
        h = _rms(x_ref[...], ng_ref[n0:n0 + 1, :])
        h = h * (1.0 + mod_ref[0, m0 + 1:m0 + 2, :]) + mod_ref[0, m0:m0 + 1, :]
        h_scr[...] = h.astype(BF16)
        acc_scr[...] = jnp.zeros_like(acc_scr)

    h = h_scr[...]
    gate = _dot(h, wg_ref[0, 0])
    up = _dot(h, wu_ref[0, 0])
    act = (_silu(gate) * up).astype(BF16)
    acc_scr[...] += _dot(act, wo_ref[0, 0])

    @pl.when(j == pl.num_programs(1) - 1)
    def _():
        y = _rms(acc_scr[...], ng_ref[n0 + 1:n0 + 2, :])
        o_ref[...] = x_ref[...] + 0.5 * mod_ref[0, m0 + 2:m0 + 3, :] * y


def _ffn(x, mod_l, ng_l, w_in_bf, w_out_bf, *, layer, which):
    nf = D_FF // TF
    m0, n0 = (0, 0) if which == 0 else (6, 4)
    return pl.pallas_call(
        functools.partial(_ffn_kernel, m0=m0, n0=n0),
        grid=(N_ALL // TM, nf),
        in_specs=[
            pl.BlockSpec((TM, D_MODEL), lambda i, j: (i, 0)),
            pl.BlockSpec((1, N_MOD, D_MODEL), lambda i, j: (_group_of_tile(i), 0, 0)),
            pl.BlockSpec((6, D_MODEL), lambda i, j: (0, 0)),
            pl.BlockSpec((1, 1, D_MODEL, TF), lambda i, j: (layer, which, 0, j)),
            pl.BlockSpec((1, 1, D_MODEL, TF), lambda i, j: (layer, which, 0, j + nf)),
            pl.BlockSpec((1, 1, TF, D_MODEL), lambda i, j: (layer, which, j, 0)),
        ],
        out_specs=pl.BlockSpec((TM, D_MODEL), lambda i, j: (i, 0)),
        out_shape=jax.ShapeDtypeStruct((N_ALL, D_MODEL), F32),
        scratch_shapes=[pltpu.VMEM((TM, D_MODEL), BF16), pltpu.VMEM((TM, D_MODEL), F32)],
        compiler_params=pltpu.CompilerParams(
            dimension_semantics=("arbitrary", "arbitrary"), vmem_limit_bytes=VMEM_LIMIT),
        name="ffn",
    )(x, mod_l, ng_l, w_in_bf, w_in_bf, w_out_bf)


_P_QK = 0
_P_VR = 4 * QK_W
_P_NAT = _P_VR + 2 * V_W
_P_END = _P_NAT + 3 * NAT_W


def _store_lane_tiles(ref, val):
    for c in range(ref.shape[0]):
        ref[c] = val[:, c * LANES:(c + 1) * LANES]


def _load_lane_tiles(ref, rows):
    return jnp.concatenate([ref[c, rows, :] for c in range(ref.shape[0])], axis=1)


def _proj_kernel(x_ref, mod_ref, ng_ref, w_ref, wlr_ref, wa_ref, ba_ref, cos_ref, sin_ref,
                 q_ref, k_ref, v_ref, r_ref, g_ref, qn_ref, kn_ref, vn_ref):
    h = _rms(x_ref[...], ng_ref[2:3, :])
    h = (h * (1.0 + mod_ref[0, 4:5, :]) + mod_ref[0, 3:4, :]).astype(BF16)

    qk = _dot(h, w_ref[:, _P_QK:_P_VR])
    cos = cos_ref[...]
    sin = sin_ref[...]
    q = qk[:, 0:QK_W] * cos + qk[:, 2 * QK_W:3 * QK_W] * sin
    _store_lane_tiles(q_ref, q * (DK_GLA ** -0.5))
    _store_lane_tiles(k_ref, qk[:, QK_W:2 * QK_W] * cos + qk[:, 3 * QK_W:4 * QK_W] * sin)

    vr = _dot(h, w_ref[:, _P_VR:_P_NAT])
    _store_lane_tiles(v_ref, vr[:, 0:V_W])
    r_ref[...] = vr[:, V_W:2 * V_W]

    nat = _dot(h, w_ref[:, _P_NAT:_P_END])
    qn_ref[...] = nat[:, 0:NAT_W]
    kn_ref[...] = nat[:, NAT_W:2 * NAT_W]
    vn_ref[...] = nat[:, 2 * NAT_W:3 * NAT_W]

    lr = _dot(h, wlr_ref[...]).astype(BF16)
    z = _dot(lr, wa_ref[...]) + ba_ref[...]
    g = (jnp.minimum(z, 0.0) - jnp.log1p(jnp.exp(-jnp.abs(z)))) * (1.0 / GATE_NORM)
    _store_lane_tiles(g_ref.at[0], g[:, 0:QK_W])
    _store_lane_tiles(g_ref.at[1], g[:, QK_W:2 * QK_W])


def _rope_table_block(i):
    lat_tiles = DEC_SEQ // TM
    return jnp.where(i < N_CTX // TM, lat_tiles, (i - N_CTX // TM) % lat_tiles)


def _project(x, mod_l, ng_l, w_big, w_lr, w_a, b_a, cos_t, sin_t):
    tile = lambda w: pl.BlockSpec((TM, w), lambda i: (i, 0))
    lane_tiled = lambda w: pl.BlockSpec((w // LANES, TM, LANES), lambda i: (0, i, 0))
    full = lambda a: pl.BlockSpec(a.shape, lambda i: (0,) * a.ndim)
    return pl.pallas_call(
        _proj_kernel,
        grid=(N_ALL // TM,),
        in_specs=[
            tile(D_MODEL),
            pl.BlockSpec((1, N_MOD, D_MODEL), lambda i: (_group_of_tile(i), 0, 0)),
            full(ng_l), full(w_big), full(w_lr), full(w_a), full(b_a),
            pl.BlockSpec((TM, QK_W), lambda i: (_rope_table_block(i), 0)),
            pl.BlockSpec((TM, QK_W), lambda i: (_rope_table_block(i), 0)),
        ],
        out_specs=[
            lane_tiled(QK_W), lane_tiled(QK_W), lane_tiled(V_W), tile(V_W),
            pl.BlockSpec((2, QK_W // LANES, TM, LANES), lambda i: (0, 0, i, 0)),
            tile(NAT_W), tile(NAT_W), tile(NAT_W),
        ],
        out_shape=[
            jax.ShapeDtypeStruct((QK_W // LANES, N_ALL, LANES), F32),
            jax.ShapeDtypeStruct((QK_W // LANES, N_ALL, LANES), F32),
            jax.ShapeDtypeStruct((V_W // LANES, N_ALL, LANES), F32), jax.ShapeDtypeStruct((N_ALL, V_W), F32),
            jax.ShapeDtypeStruct((2, QK_W // LANES, N_ALL, LANES), F32),
            jax.ShapeDtypeStruct((N_ALL, NAT_W), F32), jax.ShapeDtypeStruct((N_ALL, NAT_W), F32),
            jax.ShapeDtypeStruct((N_ALL, NAT_W), F32),
        ],
        compiler_params=pltpu.CompilerParams(
            dimension_semantics=("arbitrary",), vmem_limit_bytes=VMEM_LIMIT),
        name="mixer_proj",
    )(x, mod_l, ng_l, w_big, w_lr, w_a, b_a, cos_t, sin_t)


def _gla_tables():
    rows = []
    seq_specs = [(b * (SEQ // GLA_T), SEQ // GLA_T) for b in range(BATCH)]
    seq_specs += [(N_CTX // GLA_T + b * (DEC_SEQ // GLA_T), DEC_SEQ // GLA_T) for b in range(DEC_BATCH)]
    for sid, (blk0, nchunk) in enumerate(seq_specs):
        for direction in (0, 1):
            order = range(nchunk) if direction == 0 else range(nchunk - 1, -1, -1)
            for n, c in enumerate(order):
                rows.append((blk0 + c, direction, int(n == 0), sid))
    return np.asarray(rows, dtype=np.int32).T.copy()


_GLA_TAB = _gla_tables()
_GLA_ITEMS = _GLA_TAB.shape[1]
_GLA_NSEQ = BATCH + DEC_BATCH
_GLA_DGROUP = 4


def _gla_item(direction, q_ref, k_ref, v_ref, g_ref, o_ref, s_scr, qp, kp, cp, vp, op, qt_s, kh_s, sbd, kv_scr):
    T, CB, NB = GLA_T, GLA_CB, GLA_NB
    fwd = direction == 0
    slab = lambda i: slice(i * NB, (i + 1) * NB)

    cum = None
    for i in (range(CB) if fwd else range(CB - 1, -1, -1)):
        src = pl.ds(i, NB, stride=CB)
        gi = _load_lane_tiles(g_ref.at[0], src)
        cum = gi if cum is None else cum + gi
        cp[slab(i), :] = cum
        qi = _load_lane_tiles(q_ref, src)
        qp[slab(i), :] = qi
        kp[slab(i), :] = _load_lane_tiles(k_ref, src)
        vp[slab(i), :] = _load_lane_tiles(v_ref, src)
        _store_lane_tiles(qt_s.at[:, slab(i), :], qi * jnp.exp(cum))
    total = cum
    for i in range(CB):
        _store_lane_tiles(kh_s.at[:, slab(i), :], kp[slab(i), :] * jnp.exp(total - cp[slab(i), :]))
    dec_t = jnp.concatenate([jnp.exp(total), jnp.zeros((LANES - NB, QK_W), F32)], axis=0).T

    erow = lax.broadcasted_iota(jnp.int32, (QK_W, V_W), 0)
    ecol = lax.broadcasted_iota(jnp.int32, (QK_W, V_W), 1)
    head_sum = jnp.where((erow >> 6) == (ecol >> 7), 1.0, 0.0).astype(BF16)

    def rows(d):
        near, far = slice(0, T - NB * d), slice(NB * d, T)
        return (far, near) if fwd else (near, far)

    for d0 in range(0, CB, _GLA_DGROUP):
        ds_ = range(d0, d0 + _GLA_DGROUP)
        ps = []
        for d in ds_:
            tq, tk = rows(d)
            e = jnp.exp(cp[tq, :] - cp[tk, :])
            ps.append((qp[tq, :] * kp[tk, :] * e).astype(BF16))
        w = _dot(jnp.concatenate(ps, axis=0), head_sum)
        r0 = 0
        for d in ds_:
            tq, tk = rows(d)
            n = T - NB * d
            term = w[r0:r0 + n, :] * vp[tk, :]
            r0 += n
            for c in range(V_W // LANES):
                if d == 0:
                    op[c] = term[:, c * LANES:(c + 1) * LANES]
                else:
                    op[c, tq, :] += term[:, c * LANES:(c + 1) * LANES]

    for b in range(NB):
        blk = pl.ds(b, CB, stride=NB)
        khb = _load_lane_tiles(kh_s, blk).astype(BF16)
        vb = _load_lane_tiles(v_ref, slice(b * CB, (b + 1) * CB)).astype(BF16)
        for pair in range(H_GLA // 2):
            kv = _dot_tn(khb[:, pair * LANES:(pair + 1) * LANES],
                         vb[:, pair * 2 * DV_GLA:(pair + 1) * 2 * DV_GLA])
            kv_scr[b, 2 * pair] = kv[0:DK_GLA, 0:DV_GLA]
            kv_scr[b, 2 * pair + 1] = kv[DK_GLA:2 * DK_GLA, DV_GLA:2 * DV_GLA]

    order = range(NB) if fwd else range(NB - 1, -1, -1)
    for h in range(H_GLA):
        s = s_scr[h]
        for b in order:
            half = h % 2
            sbd[b, h // 2, half * DK_GLA:(half + 1) * DK_GLA, half * DV_GLA:(half + 1) * DV_GLA] = s.astype(BF16)
            s = dec_t[h * DK_GLA:(h + 1) * DK_GLA, b:b + 1] * s + kv_scr[b, h]
        s_scr[h] = s

    for b in range(NB):
        blk = pl.ds(b, CB, stride=NB)
        qtb = _load_lane_tiles(qt_s, blk).astype(BF16)
        inter = [_dot(qtb[:, pair * LANES:(pair + 1) * LANES], sbd[b, pair]) for pair in range(H_GLA // 2)]
        o_ref[0, b * CB:(b + 1) * CB, :] = _load_lane_tiles(op, blk) + jnp.concatenate(inter, axis=1)


def _gla_kernel(tab_ref, q_ref, k_ref, v_ref, g_ref, s0_ref, o_ref, so_ref,
                s_scr, qp, kp, cp, vp, op, qt_s, kh_s, sbd, kv_scr):
    it = pl.program_id(0)

    @pl.when(it == 0)
    def _():
        sbd[...] = jnp.zeros_like(sbd)

    @pl.when(tab_ref[2, it] == 1)
    def _():
        s_scr[...] = s0_ref[0, 0]

    for direction in (0, 1):
        @pl.when(tab_ref[1, it] == direction)
        def _(direction=direction):
            _gla_item(direction, q_ref, k_ref, v_ref, g_ref, o_ref, s_scr, qp, kp, cp, vp, op, qt_s, kh_s,
                      sbd, kv_scr)

    so_ref[0, 0] = s_scr[...]


def _gla(q, k, v, g, s0):
    tok = lambda w: pl.BlockSpec((w // LANES, GLA_T, LANES), lambda it, tab: (0, tab[0, it], 0))
    state = pl.BlockSpec((1, 1, H_GLA, DK_GLA, DV_GLA), lambda it, tab: (tab[3, it], tab[1, it], 0, 0, 0))
    grid_spec = pltpu.PrefetchScalarGridSpec(
        num_scalar_prefetch=1,
        grid=(_GLA_ITEMS,),
        in_specs=[
            tok(QK_W), tok(QK_W), tok(V_W),
            pl.BlockSpec((1, QK_W // LANES, GLA_T, LANES), lambda it, tab: (tab[1, it], 0, tab[0, it], 0)),
            state,
        ],
        out_specs=[
            pl.BlockSpec((1, GLA_T, V_W), lambda it, tab: (tab[1, it], tab[0, it], 0)),
            state,
        ],
        scratch_shapes=[
            pltpu.VMEM((H_GLA, DK_GLA, DV_GLA), F32),
            pltpu.VMEM((GLA_T, QK_W), F32),
            pltpu.VMEM((GLA_T, QK_W), F32),
            pltpu.VMEM((GLA_T, QK_W), F32),
            pltpu.VMEM((GLA_T, V_W), F32),
            pltpu.VMEM((V_W // LANES, GLA_T, LANES), F32),
            pltpu.VMEM((QK_W // LANES, GLA_T, LANES), F32),
            pltpu.VMEM((QK_W // LANES, GLA_T, LANES), F32),
            pltpu.VMEM((GLA_NB, H_GLA // 2, 2 * DK_GLA, 2 * DV_GLA), BF16),
            pltpu.VMEM((GLA_NB, H_GLA, DK_GLA, DV_GLA), F32),
        ],
    )
    return pl.pallas_call(
        _gla_kernel,
        grid_spec=grid_spec,
        out_shape=[
            jax.ShapeDtypeStruct((2, N_ALL, V_W), F32),
            jax.ShapeDtypeStruct((_GLA_NSEQ, 2, H_GLA, DK_GLA, DV_GLA), F32),
        ],
        compiler_params=pltpu.CompilerParams(
            dimension_semantics=("arbitrary",), vmem_limit_bytes=VMEM_LIMIT),
        name="gla",
    )(jnp.asarray(_GLA_TAB), q, k, v, g, s0)


def _softmax_pv(s_list, v_list):
    m = s_list[0].max(axis=-1, keepdims=True)
    for s in s_list[1:]:
        m = jnp.maximum(m, s.max(axis=-1, keepdims=True))
    num = None
    den = None
    for s, vv in zip(s_list, v_list):
        e = jnp.exp(s - m)
        den = e.sum(axis=-1, keepdims=True) if den is None else den + e.sum(axis=-1, keepdims=True)
        pv = _dot(e.astype(BF16), vv)
        num = pv if num is None else num + pv
    return num / den


def _ctx_attn_kernel(q_ref, k_ref, v_ref, o_ref):
    scale = HD_NAT ** -0.5
    lane = lax.broadcasted_iota(jnp.int32, (SEQ, LANES), 1)
    for t in range(NAT_W // LANES):
        sl = slice(t * LANES, (t + 1) * LANES)
        qt = q_ref[:, sl]
        kt = k_ref[:, sl].astype(BF16)
        vt = v_ref[:, sl].astype(BF16)
        out = jnp.zeros((SEQ, LANES), F32)
        for half in range(LANES // HD_NAT):
            mine = (lane >> 6) == half
            s = _dot_nt(jnp.where(mine, qt, 0.0).astype(BF16), kt) * scale
            out = jnp.where(mine, _softmax_pv([s], [vt]), out)
        o_ref[:, sl] = out


def _ctx_attention(qn, kn, vn):
    spec = pl.BlockSpec((SEQ, NAT_W), lambda b: (b, 0))
    return pl.pallas_call(
        _ctx_attn_kernel,
        grid=(BATCH,),
        in_specs=[spec, spec, spec],
        out_specs=spec,
        out_shape=jax.ShapeDtypeStruct((N_CTX, NAT_W), F32),
        compiler_params=pltpu.CompilerParams(
            dimension_semantics=("arbitrary",), vmem_limit_bytes=VMEM_LIMIT),
        name="ctx_attention",
    )(qn, kn, vn)


_NAT_QROWS = 4
_NAT_GROUPS = GRID_ROWS // _NAT_QROWS
_NAT_KROWS = WIN_H + _NAT_QROWS
_NAT_Q = _NAT_QROWS * GRID_W
_NAT_KEYS = _NAT_KROWS * GRID_W
_NAT_DR = 2 * WIN_H - 1
_NAT_DC = 2 * WIN_W - 1


def _nat_key_row0(r):
    return jnp.clip(r - WIN_H // 2, 0, GRID_ROWS - WIN_H)


def _nat_build_bias(rpb_ref, tz_scr):
    c = lax.broadcasted_iota(jnp.int32, (GRID_W, LANES), 0)
    lane = lax.broadcasted_iota(jnp.int32, (GRID_W, LANES), 1)
    kc = lane & (GRID_W - 1)
    second = (lane >> 6) == 1
    rel = kc - c + (WIN_W - 1)
    win_start = jnp.clip(c - WIN_W // 2, 0, GRID_W - WIN_W)
    valid = (kc >= win_start) & (kc < win_start + WIN_W)

    def body(n, carry):
        dr = n >> 3
        h = n & (H_NAT - 1)
        base0 = (h * _NAT_DR + dr) * _NAT_DC
        base1 = (h * _NAT_DR + jnp.minimum(dr + 1, _NAT_DR - 1)) * _NAT_DC
        acc = jnp.full((GRID_W, LANES), NEG_INF, F32)
        for j in range(_NAT_DC):
            val = jnp.where(second, rpb_ref[base1 + j], rpb_ref[base0 + j])
            acc = jnp.where(valid & (rel == j), val, acc)
        tz_scr[dr, h] = acc
        return carry

    lax.fori_loop(0, _NAT_DR * H_NAT, body, 0)


def _nat_kernel(rpb_ref, q_ref, k_ref, v_ref, ck_ref, cv_ref, o_ref, tz_scr, bias_scr):
    grp = pl.program_id(1)

    @pl.when((pl.program_id(0) == 0) & (grp == 0))
    def _():
        _nat_build_bias(rpb_ref, tz_scr)

    scale = HD_NAT ** -0.5
    krow0 = jnp.clip(_NAT_QROWS * grp - WIN_H // 2, 0, GRID_ROWS - _NAT_KROWS)
    k0 = pl.multiple_of(krow0 * GRID_W, GRID_W)
    lane_q = lax.broadcasted_iota(jnp.int32, (_NAT_Q, LANES), 1)
    lane_b = lax.broadcasted_iota(jnp.int32, (GRID_W, LANES), 1)
    neg = jnp.full((GRID_W, LANES), NEG_INF, F32)
    for t in range(NAT_W // LANES):
        sl = slice(t * LANES, (t + 1) * LANES)
        qt = q_ref[:, sl]
        kw = k_ref[pl.ds(k0, _NAT_KEYS), sl].astype(BF16)
        vw = v_ref[pl.ds(k0, _NAT_KEYS), sl].astype(BF16)
        ck = ck_ref[0, 0, :, sl].astype(BF16)
        cv = cv_ref[0, 0, :, sl].astype(BF16)
        out = jnp.zeros((_NAT_Q, LANES), F32)
        for half in range(LANES // HD_NAT):
            h = 2 * t + half
            for qr in range(_NAT_QROWS):
                r = _NAT_QROWS * grp + qr
                lo = _nat_key_row0(r)
                for kp in range(_NAT_KROWS // 2):
                    kr = krow0 + 2 * kp
                    tile = tz_scr[jnp.clip(kr - r + (WIN_H - 1), 0, _NAT_DR - 1), h]
                    ok_a = ((kr >= lo) & (kr < lo + WIN_H)).astype(jnp.int32)
                    ok_b = ((kr + 1 >= lo) & (kr + 1 < lo + WIN_H)).astype(jnp.int32)
                    ok = jnp.where(lane_b < GRID_W, ok_a, ok_b) == 1
                    bias_scr[qr * GRID_W:(qr + 1) * GRID_W, kp * LANES:(kp + 1) * LANES] = jnp.where(ok, tile, neg)
            mine = (lane_q >> 6) == half
            qm = jnp.where(mine, qt, 0.0).astype(BF16)
            s_win = _dot_nt(qm, kw) * scale + bias_scr[...]
            s_ctx = _dot_nt(qm, ck) * scale
            out = jnp.where(mine, _softmax_pv([s_win, s_ctx], [vw, cv]), out)
        o_ref[:, sl] = out


def _nat_attention(qn, kn, vn, cache_k, cache_v, rpb_l, layer):
    lat0 = N_CTX // _NAT_Q
    return pl.pallas_call(
        _nat_kernel,
        grid=(DEC_BATCH, _NAT_GROUPS),
        in_specs=[
            pl.BlockSpec(memory_space=pltpu.SMEM),
            pl.BlockSpec((_NAT_Q, NAT_W), lambda b, g: (lat0 + b * _NAT_GROUPS + g, 0)),
            pl.BlockSpec((DEC_SEQ, NAT_W), lambda b, g: (N_CTX // DEC_SEQ + b, 0)),
            pl.BlockSpec((DEC_SEQ, NAT_W), lambda b, g: (N_CTX // DEC_SEQ + b, 0)),
            pl.BlockSpec((1, 1, PAST_LEN, NAT_W), lambda b, g: (b, layer, 0, 0)),
            pl.BlockSpec((1, 1, PAST_LEN, NAT_W), lambda b, g: (b, layer, 0, 0)),
        ],
        out_specs=pl.BlockSpec((_NAT_Q, NAT_W), lambda b, g: (b * _NAT_GROUPS + g, 0)),
        out_shape=jax.ShapeDtypeStruct((N_LAT, NAT_W), F32),
        scratch_shapes=[
            pltpu.VMEM((_NAT_DR, H_NAT, GRID_W, LANES), F32),
            pltpu.VMEM((_NAT_Q, _NAT_KEYS), F32),
        ],
        compiler_params=pltpu.CompilerParams(
            dimension_semantics=("arbitrary", "arbitrary"), vmem_limit_bytes=VMEM_LIMIT),
        name="nat_attention",
    )(rpb_l.reshape(-1), qn, kn, vn, cache_k, cache_v)


def _out_kernel(x_ref, o_ref, r_ref, on_ref, mod_ref, ng_ref, gng_ref, w_ref, y_ref):
    og = o_ref[0] + o_ref[1]
    parts = []
    for h in range(H_GLA):
        parts.append(_rms(og[:, h * DV_GLA:(h + 1) * DV_GLA], gng_ref[...]))
    merged = jnp.concatenate(parts, axis=1) * _silu(r_ref[...])
    y = _dot(merged.astype(BF16), w_ref[0:V_W, :]) + _dot(on_ref[...].astype(BF16), w_ref[V_W:, :])
    y_ref[...] = x_ref[...] + mod_ref[0, 5:6, :] * _rms(y, ng_ref[3:4, :])


def _mixer_out(x, o_gla, r, o_nat, mod_l, ng_l, gng_l, w_out_bf):
    tile = lambda w: pl.BlockSpec((TM, w), lambda i: (i, 0))
    full = lambda a: pl.BlockSpec(a.shape, lambda i: (0,) * a.ndim)
    return pl.pallas_call(
        _out_kernel,
        grid=(N_ALL // TM,),
        in_specs=[
            tile(D_MODEL),
            pl.BlockSpec((2, TM, V_W), lambda i: (0, i, 0)),
            tile(V_W), tile(NAT_W),
            pl.BlockSpec((1, N_MOD, D_MODEL), lambda i: (_group_of_tile(i), 0, 0)),
            full(ng_l), full(gng_l), full(w_out_bf),
        ],
        out_specs=tile(D_MODEL),
        out_shape=jax.ShapeDtypeStruct((N_ALL, D_MODEL), F32),
        compiler_params=pltpu.CompilerParams(
            dimension_semantics=("arbitrary",), vmem_limit_bytes=VMEM_LIMIT),
        name="mixer_out",
    )(x, o_gla, r, o_nat, mod_l, ng_l, gng_l, w_out_bf)


def _rope_tables():
    quarter = DK_GLA // 4
    freqs = ROPE_BASE ** (-jnp.arange(quarter, dtype=F32) / quarter)
    t = jnp.arange(DEC_SEQ)
    ang_r = (t // GRID_W).astype(F32)[:, None] * freqs[None, :]
    ang_c = (t % GRID_W).astype(F32)[:, None] * freqs[None, :]
    cos_h = jnp.concatenate([jnp.cos(ang_r), jnp.cos(ang_r), jnp.cos(ang_c), jnp.cos(ang_c)], axis=1)
    sin_h = jnp.concatenate([-jnp.sin(ang_r), jnp.sin(ang_r), -jnp.sin(ang_c), jnp.sin(ang_c)], axis=1)
    cos_t = jnp.concatenate([jnp.tile(cos_h, (1, H_GLA)), jnp.ones((TM, QK_W), F32)], axis=0)
    sin_t = jnp.concatenate([jnp.tile(sin_h, (1, H_GLA)), jnp.zeros((TM, QK_W), F32)], axis=0)
    return cos_t, sin_t


def _swap_perm():
    quarter = DK_GLA // 4
    idx = np.arange(QK_W)
    blk = (idx // quarter) % 4
    return np.where(blk % 2 == 0, idx + quarter, idx - quarter)


def _pack_w_in(w_in_l, gla_wa2_l, gla_ba_l):
    sizes = [QK_W, QK_W, V_W, V_W, GLA_RANK, GLA_RANK, NAT_W, NAT_W, NAT_W]
    cuts = np.cumsum([0] + sizes)
    wq, wk, wv, wr, wlf, wlb, wqn, wkn, wvn = [w_in_l[:, cuts[i]:cuts[i + 1]] for i in range(9)]
    perm = _swap_perm()
    w_big = jnp.concatenate([wq, wk, wq[:, perm], wk[:, perm], wv, wr, wqn, wkn, wvn], axis=1).astype(BF16)
    w_lr = jnp.zeros((D_MODEL, LANES), F32).at[:, 0:GLA_RANK].set(wlf).at[:, GLA_RANK:2 * GLA_RANK].set(wlb)
    w_a = jnp.zeros((LANES, 2 * QK_W), F32)
    w_a = w_a.at[0:GLA_RANK, 0:QK_W].set(gla_wa2_l[0]).at[GLA_RANK:2 * GLA_RANK, QK_W:].set(gla_wa2_l[1])
    b_a = jnp.concatenate([gla_ba_l[0], gla_ba_l[1]])[None, :]
    return w_big, w_lr.astype(BF16), w_a.astype(BF16), b_a


def kernel(x_prompt, x_sample, cache_k, cache_v, state_gla, c, c_ctx, w_mod, b_mod, norm_g, ffn_w_in, ffn_w_out,
           w_in, gla_wa2, gla_ba, gla_norm_g, nat_rpb, w_out):
    cvecs = jnp.zeros((SUBLANES, D_MODEL), F32).at[0].set(c_ctx).at[1:1 + DEC_BATCH].set(c)
    mod = _modulation(cvecs, w_mod, b_mod)[:, :N_GROUPS].reshape(DEPTH, N_GROUPS, N_MOD, D_MODEL)

    ffn_w_in_bf = ffn_w_in.astype(BF16)
    ffn_w_out_bf = ffn_w_out.astype(BF16)
    w_out_bf = w_out.astype(BF16)
    cos_t, sin_t = _rope_tables()
    ck = cache_k.reshape(DEC_BATCH, DEPTH, PAST_LEN, NAT_W)
    cv = cache_v.reshape(DEC_BATCH, DEPTH, PAST_LEN, NAT_W)

    x = jnp.concatenate([x_prompt.reshape(N_CTX, D_MODEL), x_sample.reshape(N_LAT, D_MODEL)], axis=0)
    k_list, v_list, s_list = [], [], []
    for l in range(DEPTH):
        mod_l, ng_l = mod[l], norm_g[l]
        x = _ffn(x, mod_l, ng_l, ffn_w_in_bf, ffn_w_out_bf, layer=l, which=0)

        w_big, w_lr, w_a, b_a = _pack_w_in(w_in[l], gla_wa2[l], gla_ba[l])
        q, k, v, r, g, qn, kn, vn = _project(x, mod_l, ng_l, w_big, w_lr, w_a, b_a, cos_t, sin_t)

        s0 = jnp.concatenate([jnp.zeros((BATCH, 2, H_GLA, DK_GLA, DV_GLA), F32), state_gla[:, l]], axis=0)
        o_gla, s_fin = _gla(q, k, v, g, s0)
        o_ctx = _ctx_attention(qn, kn, vn)
        o_lat = _nat_attention(qn, kn, vn, ck, cv, nat_rpb[l], l)
        o_nat = jnp.concatenate([o_ctx, o_lat], axis=0)

        x = _mixer_out(x, o_gla, r, o_nat, mod_l, ng_l, gla_norm_g[l][None, :], w_out_bf[l])
        x = _ffn(x, mod_l, ng_l, ffn_w_in_bf, ffn_w_out_bf, layer=l, which=1)

        k_list.append(kn[:N_CTX].reshape(BATCH, SEQ, H_NAT, HD_NAT))
        v_list.append(vn[:N_CTX].reshape(BATCH, SEQ, H_NAT, HD_NAT))
        s_list.append(s_fin[:BATCH])

    y_prompt = x[:N_CTX].reshape(BATCH, SEQ, D_MODEL)
    y_sample = x[N_CTX:].reshape(DEC_BATCH, DEC_SEQ, D_MODEL)
    return (y_prompt, y_sample, jnp.stack(k_list, axis=1), jnp.stack(v_list, axis=1), jnp.stack(s_list, axis=1))
```

```python
import functools

import numpy as np
import jax
import jax.numpy as jnp
from jax import lax
from jax.experimental import pallas as pl
from jax.experimental.pallas import tpu as pltpu

D_MODEL = 1024
BATCH = 16
SEQ = 256
DEPTH = 2
DEC_BATCH = 2
DEC_SEQ = 1024
PAST_LEN = 512
GRID_W = 64
H_GLA = 4
DK_GLA = 64
DV_GLA = 128
GLA_RANK = 16
GATE_NORM = 16.0
H_NAT = 8
HD_NAT = 64
WIN_H = 8
WIN_W = 16
D_FF = 2816
N_MOD = 9
ROPE_BASE = 10000.0
EPS = 1e-6
NEG_INF = -1e30

F32 = jnp.float32
BF16 = jnp.bfloat16

N_CTX = BATCH * SEQ
N_LAT = DEC_BATCH * DEC_SEQ
N_ALL = N_CTX + N_LAT
N_GROUPS = 1 + DEC_BATCH
QK_W = H_GLA * DK_GLA
V_W = H_GLA * DV_GLA
NAT_W = H_NAT * HD_NAT
GRID_ROWS = DEC_SEQ // GRID_W

LANES = 128
SUBLANES = 8

TM = 512
TF = 1408
MOD_TN = 1152
GLA_T = 256
GLA_CB = 16
GLA_NB = GLA_T // GLA_CB
VMEM_LIMIT = 56 * 1024 * 1024

assert N_CTX % TM == 0 and DEC_SEQ % TM == 0 and D_FF % TF == 0 and TF % LANES == 0
assert SEQ == GLA_T and DEC_SEQ % GLA_T == 0


def _group_of_tile(i):
    return jnp.where(i < N_CTX // TM, 0, 1 + (i - N_CTX // TM) // (DEC_SEQ // TM))


def _dot(a, b):
    return jnp.dot(a, b, preferred_element_type=F32)


def _dot_nt(a, b):
    return lax.dot_general(a, b, (((1,), (1,)), ((), ())), preferred_element_type=F32)


def _dot_tn(a, b):
    return lax.dot_general(a, b, (((0,), (0,)), ((), ())), preferred_element_type=F32)


def _rms(x, g):
    ms = jnp.mean(x * x, axis=-1, keepdims=True)
    return x * lax.rsqrt(ms + EPS) * g


def _silu(x):
    return x * jax.nn.sigmoid(x)


def _mod_kernel(c_ref, w_ref, b_ref, o_ref):
    s = _silu(c_ref[...]).astype(BF16)
    o_ref[0] = _dot(s, w_ref[0].astype(BF16)) + b_ref[0]


def _modulation(cvecs, w_mod, b_mod):
    n_out = N_MOD * D_MODEL
    return pl.pallas_call(
        _mod_kernel,
        grid=(DEPTH, n_out // MOD_TN),
        in_specs=[
            pl.BlockSpec((SUBLANES, D_MODEL), lambda l, j: (0, 0)),
            pl.BlockSpec((1, D_MODEL, MOD_TN), lambda l, j: (l, 0, j)),
            pl.BlockSpec((1, 1, MOD_TN), lambda l, j: (l, 0, j)),
        ],
        out_specs=pl.BlockSpec((1, SUBLANES, MOD_TN), lambda l, j: (l, 0, j)),
        out_shape=jax.ShapeDtypeStruct((DEPTH, SUBLANES, n_out), F32),
        compiler_params=pltpu.CompilerParams(
            dimension_semantics=("arbitrary", "arbitrary"), vmem_limit_bytes=VMEM_LIMIT),
        name="modulation",
    )(cvecs, w_mod, b_mod.reshape(DEPTH, 1, n_out))


def _ffn_kernel(x_ref, mod_ref, ng_ref, wg_ref, wu_ref, wo_ref, o_ref, h_scr, acc_scr, *, m0, n0):
    j = pl.program_id(1)

    @pl.when(j == 0)
    def _():
        h = _rms(x_ref[...], ng_ref[n0:n0 + 1, :])
        h = h * (1.0 + mod_ref[0, m0 + 1:m0 + 2, :]) + mod_ref[0, m0:m0 + 1, :]
        h_scr[...] = h.astype(BF16)
        acc_scr[...] = jnp.zeros_like(acc_scr)

    h = h_scr[...]
    gate = _dot(h, wg_ref[0, 0])
    up = _dot(h, wu_ref[0, 0])
    act = (_silu(gate) * up).astype(BF16)
    acc_scr[...] += _dot(act, wo_ref[0, 0])

    @pl.when(j == pl.num_programs(1) - 1)
    def _():
        y = _rms(acc_scr[...], ng_ref[n0 + 1:n0 + 2, :])
        o_ref[...] = x_ref[...] + 0.5 * mod_ref[0, m0 + 2:m0 + 3, :] * y


def _ffn(x, mod_l, ng_l, w_in_bf, w_out_bf, *, layer, which):
    nf = D_FF // TF
    m0, n0 = (0, 0) if which == 0 else (6, 4)
    return pl.pallas_call(
        functools.partial(_ffn_kernel, m0=m0, n0=n0),
        grid=(N_ALL // TM, nf),
        in_specs=[
            pl.BlockSpec((TM, D_MODEL), lambda i, j: (i, 0)),
            pl.BlockSpec((1, N_MOD, D_MODEL), lambda i, j: (_group_of_tile(i), 0, 0)),
            pl.BlockSpec((6, D_MODEL), lambda i, j: (0, 0)),
            pl.BlockSpec((1, 1, D_MODEL, TF), lambda i, j: (layer, which, 0, j)),
            pl.BlockSpec((1, 1, D_MODEL, TF), lambda i, j: (layer, which, 0, j + nf)),
            pl.BlockSpec((1, 1, TF, D_MODEL), lambda i, j: (layer, which, j, 0)),
        ],
        out_specs=pl.BlockSpec((TM, D_MODEL), lambda i, j: (i, 0)),
        out_shape=jax.ShapeDtypeStruct((N_ALL, D_MODEL), F32),
        scratch_shapes=[pltpu.VMEM((TM, D_MODEL), BF16), pltpu.VMEM((TM, D_MODEL), F32)],
        compiler_params=pltpu.CompilerParams(
            dimension_semantics=("arbitrary", "arbitrary"), vmem_limit_bytes=VMEM_LIMIT),
        name="ffn",
    )(x, mod_l, ng_l, w_in_bf, w_in_bf, w_out_bf)


_P_QK = 0
_P_VR = 4 * QK_W
_P_NAT = _P_VR + 2 * V_W
_P_END = _P_NAT + 3 * NAT_W


def _block_transpose_perm():
    r = lax.broadcasted_iota(jnp.int32, (GLA_T, GLA_T), 0)
    c = lax.broadcasted_iota(jnp.int32, (GLA_T, GLA_T), 1)
    return jnp.where(c == (r & (GLA_CB - 1)) * GLA_NB + (r >> 4), 1.0, 0.0).astype(BF16)


def _permute_chunks(perm, a):
    return jnp.concatenate([_dot(perm, a[c * GLA_T:(c + 1) * GLA_T, :]).astype(BF16)
                            for c in range(a.shape[0] // GLA_T)], axis=0)


def _proj_kernel(x_ref, mod_ref, ng_ref, w_ref, wlr_ref, wa_ref, ba_ref, cos_ref, sin_ref,
                 q_ref, k_ref, v_ref, r_ref, g_ref, qn_ref, kn_ref, vn_ref):
    h = _rms(x_ref[...], ng_ref[2:3, :])
    h = (h * (1.0 + mod_ref[0, 4:5, :]) + mod_ref[0, 3:4, :]).astype(BF16)
    hp = _permute_chunks(_block_transpose_perm(), h)

    qk = _dot(hp, w_ref[:, _P_QK:_P_VR])
    cos = cos_ref[...]
    sin = sin_ref[...]
    q = qk[:, 0:QK_W] * cos + qk[:, 2 * QK_W:3 * QK_W] * sin
    q_ref[...] = q * (DK_GLA ** -0.5)
    k_ref[...] = qk[:, QK_W:2 * QK_W] * cos + qk[:, 3 * QK_W:4 * QK_W] * sin

    vr = _dot(hp, w_ref[:, _P_VR:_P_NAT])
    v_ref[...] = vr[:, 0:V_W]
    r_ref[...] = vr[:, V_W:2 * V_W]

    nat = _dot(h, w_ref[:, _P_NAT:_P_END])
    qn_ref[...] = nat[:, 0:NAT_W]
    kn_ref[...] = nat[:, NAT_W:2 * NAT_W]
    vn_ref[...] = nat[:, 2 * NAT_W:3 * NAT_W]

    lr = _dot(hp, wlr_ref[...]).astype(BF16)
    z = _dot(lr, wa_ref[...]) + ba_ref[...]
    g = (jnp.minimum(z, 0.0) - jnp.log1p(jnp.exp(-jnp.abs(z)))) * (1.0 / GATE_NORM)
    g_ref[0] = g[:, 0:QK_W]
    g_ref[1] = g[:, QK_W:2 * QK_W]


def _rope_table_block(i):
    lat_tiles = DEC_SEQ // TM
    return jnp.where(i < N_CTX // TM, lat_tiles, (i - N_CTX // TM) % lat_tiles)


def _project(x, mod_l, ng_l, w_big, w_lr, w_a, b_a, cos_t, sin_t):
    tile = lambda w: pl.BlockSpec((TM, w), lambda i: (i, 0))
    full = lambda a: pl.BlockSpec(a.shape, lambda i: (0,) * a.ndim)
    return pl.pallas_call(
        _proj_kernel,
        grid=(N_ALL // TM,),
        in_specs=[
            tile(D_MODEL),
            pl.BlockSpec((1, N_MOD, D_MODEL), lambda i: (_group_of_tile(i), 0, 0)),
            full(ng_l), full(w_big), full(w_lr), full(w_a), full(b_a),
            pl.BlockSpec((TM, QK_W), lambda i: (_rope_table_block(i), 0)),
            pl.BlockSpec((TM, QK_W), lambda i: (_rope_table_block(i), 0)),
        ],
        out_specs=[
            tile(QK_W), tile(QK_W), tile(V_W), tile(V_W),
            pl.BlockSpec((2, TM, QK_W), lambda i: (0, i, 0)),
            tile(NAT_W), tile(NAT_W), tile(NAT_W),
        ],
        out_shape=[
            jax.ShapeDtypeStruct((N_ALL, QK_W), F32), jax.ShapeDtypeStruct((N_ALL, QK_W), F32),
            jax.ShapeDtypeStruct((N_ALL, V_W), F32), jax.ShapeDtypeStruct((N_ALL, V_W), F32),
            jax.ShapeDtypeStruct((2, N_ALL, QK_W), F32),
            jax.ShapeDtypeStruct((N_ALL, NAT_W), F32), jax.ShapeDtypeStruct((N_ALL, NAT_W), F32),
            jax.ShapeDtypeStruct((N_ALL, NAT_W), F32),
        ],
        compiler_params=pltpu.CompilerParams(
            dimension_semantics=("arbitrary",), vmem_limit_bytes=VMEM_LIMIT),
        name="mixer_proj",
    )(x, mod_l, ng_l, w_big, w_lr, w_a, b_a, cos_t, sin_t)


def _gla_tables():
    rows = []
    seq_specs = [(b * (SEQ // GLA_T), SEQ // GLA_T) for b in range(BATCH)]
    seq_specs += [(N_CTX // GLA_T + b * (DEC_SEQ // GLA_T), DEC_SEQ // GLA_T) for b in range(DEC_BATCH)]
    for sid, (blk0, nchunk) in enumerate(seq_specs):
        for direction in (0, 1):
            order = range(nchunk) if direction == 0 else range(nchunk - 1, -1, -1)
            for n, c in enumerate(order):
                rows.append((blk0 + c, direction, int(n == 0), sid))
    return np.asarray(rows, dtype=np.int32).T.copy()


_GLA_TAB = _gla_tables()
_GLA_ITEMS = _GLA_TAB.shape[1]
_GLA_NSEQ = BATCH + DEC_BATCH
_GLA_DGROUP = 4


def _gla_item(direction, q_ref, k_ref, v_ref, g_ref, o_ref, s_scr, cp, sstk):
    T, CB, NB = GLA_T, GLA_CB, GLA_NB
    fwd = direction == 0
    slab = lambda i: slice(i * NB, (i + 1) * NB)

    cum = None
    for i in (range(CB) if fwd else range(CB - 1, -1, -1)):
        gi = g_ref[0, slab(i), :]
        cum = gi if cum is None else cum + gi
        cp[slab(i), :] = cum
    total = cum
    cum_all = cp[...]
    qt = q_ref[...] * jnp.exp(cum_all)
    kh = k_ref[...] * jnp.exp(jnp.concatenate([total] * CB, axis=0) - cum_all)
    dec_t = jnp.concatenate([jnp.exp(total), jnp.zeros((LANES - NB, QK_W), F32)], axis=0).T

    erow = lax.broadcasted_iota(jnp.int32, (QK_W, V_W), 0)
    ecol = lax.broadcasted_iota(jnp.int32, (QK_W, V_W), 1)
    head_sum = jnp.where((erow >> 6) == (ecol >> 7), 1.0, 0.0).astype(BF16)

    def rows(d):
        near, far = slice(0, T - NB * d), slice(NB * d, T)
        return (far, near) if fwd else (near, far)

    for d0 in range(0, CB, _GLA_DGROUP):
        ds_ = range(d0, d0 + _GLA_DGROUP)
        ps = []
        for d in ds_:
            tq, tk = rows(d)
            e = jnp.exp(cp[tq, :] - cp[tk, :])
            ps.append((q_ref[tq, :] * k_ref[tk, :] * e).astype(BF16))
        w = _dot(jnp.concatenate(ps, axis=0), head_sum)
        r0 = 0
        for d in ds_:
            tq, tk = rows(d)
            n = T - NB * d
            term = w[r0:r0 + n, :] * v_ref[tk, :]
            r0 += n
            if d == 0:
                o_ref[0] = term
            else:
                o_ref[0, tq, :] += term

    kht = kh.T.astype(BF16)
    v_bf = v_ref[...].astype(BF16)
    key_blk = lax.broadcasted_iota(jnp.int32, (DK_GLA, T), 1) & (NB - 1)
    row_blk = lax.broadcasted_iota(jnp.int32, (T, LANES), 0) & (NB - 1)
    lane_half = lax.broadcasted_iota(jnp.int32, (T, LANES), 1) >> 6
    order = range(NB) if fwd else range(NB - 1, -1, -1)
    for h in range(H_GLA):
        kh_h = kht[h * DK_GLA:(h + 1) * DK_GLA, :]
        kv = _dot(jnp.concatenate([jnp.where(key_blk == b, kh_h, 0) for b in range(NB)], axis=0),
                  v_bf[:, h * DV_GLA:(h + 1) * DV_GLA])
        s = s_scr[h]
        for b in order:
            sstk[h, b * DK_GLA:(b + 1) * DK_GLA, :] = s.astype(BF16)
            s = dec_t[h * DK_GLA:(h + 1) * DK_GLA, b:b + 1] * s + kv[b * DK_GLA:(b + 1) * DK_GLA, :]
        s_scr[h] = s

        pair_tile = qt[:, (h // 2) * LANES:(h // 2 + 1) * LANES]
        both = jnp.where(lane_half == h % 2, pair_tile, pltpu.roll(pair_tile, DK_GLA, axis=1))
        lhs = jnp.concatenate([jnp.where(row_blk == 2 * j + lane_half, both, 0.0).astype(BF16)
                               for j in range(NB // 2)], axis=1)
        o_ref[0, :, h * DV_GLA:(h + 1) * DV_GLA] += _dot(lhs, sstk[h])


def _gla_kernel(tab_ref, q_ref, k_ref, v_ref, g_ref, s0_ref, o_ref, so_ref, s_scr, cp, sstk):
    it = pl.program_id(0)

    @pl.when(tab_ref[2, it] == 1)
    def _():
        s_scr[...] = s0_ref[0, 0]

    for direction in (0, 1):
        @pl.when(tab_ref[1, it] == direction)
        def _(direction=direction):
            _gla_item(direction, q_ref, k_ref, v_ref, g_ref, o_ref, s_scr, cp, sstk)

    so_ref[0, 0] = s_scr[...]


def _gla(q, k, v, g, s0):
    tok = lambda w: pl.BlockSpec((GLA_T, w), lambda it, tab: (tab[0, it], 0))
    state = pl.BlockSpec((1, 1, H_GLA, DK_GLA, DV_GLA), lambda it, tab: (tab[3, it], tab[1, it], 0, 0, 0))
    grid_spec = pltpu.PrefetchScalarGridSpec(
        num_scalar_prefetch=1,
        grid=(_GLA_ITEMS,),
        in_specs=[
            tok(QK_W), tok(QK_W), tok(V_W),
            pl.BlockSpec((1, GLA_T, QK_W), lambda it, tab: (tab[1, it], tab[0, it], 0)),
            state,
        ],
        out_specs=[
            pl.BlockSpec((1, GLA_T, V_W), lambda it, tab: (tab[1, it], tab[0, it], 0)),
            state,
        ],
        scratch_shapes=[
            pltpu.VMEM((H_GLA, DK_GLA, DV_GLA), F32),
            pltpu.VMEM((GLA_T, QK_W), F32),
            pltpu.VMEM((H_GLA, GLA_NB * DK_GLA, DV_GLA), BF16),
        ],
    )
    return pl.pallas_call(
        _gla_kernel,
        grid_spec=grid_spec,
        out_shape=[
            jax.ShapeDtypeStruct((2, N_ALL, V_W), F32),
            jax.ShapeDtypeStruct((_GLA_NSEQ, 2, H_GLA, DK_GLA, DV_GLA), F32),
        ],
        compiler_params=pltpu.CompilerParams(
            dimension_semantics=("arbitrary",), vmem_limit_bytes=VMEM_LIMIT),
        name="gla",
    )(jnp.asarray(_GLA_TAB), q, k, v, g, s0)


def _softmax_pv(s_list, v_list):
    m = s_list[0].max(axis=-1, keepdims=True)
    for s in s_list[1:]:
        m = jnp.maximum(m, s.max(axis=-1, keepdims=True))
    num = None
    den = None
    for s, vv in zip(s_list, v_list):
        e = jnp.exp(s - m)
        den = e.sum(axis=-1, keepdims=True) if den is None else den + e.sum(axis=-1, keepdims=True)
        pv = _dot(e.astype(BF16), vv)
        num = pv if num is None else num + pv
    return num / den


def _ctx_attn_kernel(q_ref, k_ref, v_ref, o_ref):
    scale = HD_NAT ** -0.5
    lane = lax.broadcasted_iota(jnp.int32, (SEQ, LANES), 1)
    for t in range(NAT_W // LANES):
        sl = slice(t * LANES, (t + 1) * LANES)
        qt = q_ref[:, sl]
        kt = k_ref[:, sl].astype(BF16)
        vt = v_ref[:, sl].astype(BF16)
        out = jnp.zeros((SEQ, LANES), F32)
        for half in range(LANES // HD_NAT):
            mine = (lane >> 6) == half
            s = _dot_nt(jnp.where(mine, qt, 0.0).astype(BF16), kt) * scale
            out = jnp.where(mine, _softmax_pv([s], [vt]), out)
        o_ref[:, sl] = out


def _ctx_attention(qn, kn, vn):
    spec = pl.BlockSpec((SEQ, NAT_W), lambda b: (b, 0))
    return pl.pallas_call(
        _ctx_attn_kernel,
        grid=(BATCH,),
        in_specs=[spec, spec, spec],
        out_specs=spec,
        out_shape=jax.ShapeDtypeStruct((N_CTX, NAT_W), F32),
        compiler_params=pltpu.CompilerParams(
            dimension_semantics=("arbitrary",), vmem_limit_bytes=VMEM_LIMIT),
        name="ctx_attention",
    )(qn, kn, vn)


_NAT_QROWS = 4
_NAT_GROUPS = GRID_ROWS // _NAT_QROWS
_NAT_KROWS = WIN_H + _NAT_QROWS
_NAT_Q = _NAT_QROWS * GRID_W
_NAT_KEYS = _NAT_KROWS * GRID_W
_NAT_DR = 2 * WIN_H - 1
_NAT_DC = 2 * WIN_W - 1


def _nat_key_row0(r):
    return jnp.clip(r - WIN_H // 2, 0, GRID_ROWS - WIN_H)


def _nat_build_bias(rpb_ref, tz_scr):
    c = lax.broadcasted_iota(jnp.int32, (GRID_W, LANES), 0)
    lane = lax.broadcasted_iota(jnp.int32, (GRID_W, LANES), 1)
    kc = lane & (GRID_W - 1)
    second = (lane >> 6) == 1
    rel = kc - c + (WIN_W - 1)
    win_start = jnp.clip(c - WIN_W // 2, 0, GRID_W - WIN_W)
    valid = (kc >= win_start) & (kc < win_start + WIN_W)

    def body(n, carry):
        dr = n >> 3
        h = n & (H_NAT - 1)
        base0 = (h * _NAT_DR + dr) * _NAT_DC
        base1 = (h * _NAT_DR + jnp.minimum(dr + 1, _NAT_DR - 1)) * _NAT_DC
        acc = jnp.full((GRID_W, LANES), NEG_INF, F32)
        for j in range(_NAT_DC):
            val = jnp.where(second, rpb_ref[base1 + j], rpb_ref[base0 + j])
            acc = jnp.where(valid & (rel == j), val, acc)
        tz_scr[dr, h] = acc
        return carry

    lax.fori_loop(0, _NAT_DR * H_NAT, body, 0)


def _nat_kernel(rpb_ref, q_ref, k_ref, v_ref, ck_ref, cv_ref, o_ref, tz_scr, bias_scr):
    grp = pl.program_id(1)

    @pl.when((pl.program_id(0) == 0) & (grp == 0))
    def _():
        _nat_build_bias(rpb_ref, tz_scr)

    scale = HD_NAT ** -0.5
    krow0 = jnp.clip(_NAT_QROWS * grp - WIN_H // 2, 0, GRID_ROWS - _NAT_KROWS)
    k0 = pl.multiple_of(krow0 * GRID_W, GRID_W)
    lane_q = lax.broadcasted_iota(jnp.int32, (_NAT_Q, LANES), 1)
    lane_b = lax.broadcasted_iota(jnp.int32, (GRID_W, LANES), 1)
    neg = jnp.full((GRID_W, LANES), NEG_INF, F32)
    for t in range(NAT_W // LANES):
        sl = slice(t * LANES, (t + 1) * LANES)
        qt = q_ref[:, sl]
        kw = k_ref[pl.ds(k0, _NAT_KEYS), sl].astype(BF16)
        vw = v_ref[pl.ds(k0, _NAT_KEYS), sl].astype(BF16)
        ck = ck_ref[0, 0, :, sl].astype(BF16)
        cv = cv_ref[0, 0, :, sl].astype(BF16)
        out = jnp.zeros((_NAT_Q, LANES), F32)
        for half in range(LANES // HD_NAT):
            h = 2 * t + half
            for qr in range(_NAT_QROWS):
                r = _NAT_QROWS * grp + qr
                lo = _nat_key_row0(r)
                for kp in range(_NAT_KROWS // 2):
                    kr = krow0 + 2 * kp
                    tile = tz_scr[jnp.clip(kr - r + (WIN_H - 1), 0, _NAT_DR - 1), h]
                    ok_a = ((kr >= lo) & (kr < lo + WIN_H)).astype(jnp.int32)
                    ok_b = ((kr + 1 >= lo) & (kr + 1 < lo + WIN_H)).astype(jnp.int32)
                    ok = jnp.where(lane_b < GRID_W, ok_a, ok_b) == 1
                    bias_scr[qr * GRID_W:(qr + 1) * GRID_W, kp * LANES:(kp + 1) * LANES] = jnp.where(ok, tile, neg)
            mine = (lane_q >> 6) == half
            qm = jnp.where(mine, qt, 0.0).astype(BF16)
            s_win = _dot_nt(qm, kw) * scale + bias_scr[...]
            s_ctx = _dot_nt(qm, ck) * scale
            out = jnp.where(mine, _softmax_pv([s_win, s_ctx], [vw, cv]), out)
        o_ref[:, sl] = out


def _nat_attention(qn, kn, vn, cache_k, cache_v, rpb_l, layer):
    lat0 = N_CTX // _NAT_Q
    return pl.pallas_call(
        _nat_kernel,
        grid=(DEC_BATCH, _NAT_GROUPS),
        in_specs=[
            pl.BlockSpec(memory_space=pltpu.SMEM),
            pl.BlockSpec((_NAT_Q, NAT_W), lambda b, g: (lat0 + b * _NAT_GROUPS + g, 0)),
            pl.BlockSpec((DEC_SEQ, NAT_W), lambda b, g: (N_CTX // DEC_SEQ + b, 0)),
            pl.BlockSpec((DEC_SEQ, NAT_W), lambda b, g: (N_CTX // DEC_SEQ + b, 0)),
            pl.BlockSpec((1, 1, PAST_LEN, NAT_W), lambda b, g: (b, layer, 0, 0)),
            pl.BlockSpec((1, 1, PAST_LEN, NAT_W), lambda b, g: (b, layer, 0, 0)),
        ],
        out_specs=pl.BlockSpec((_NAT_Q, NAT_W), lambda b, g: (b * _NAT_GROUPS + g, 0)),
        out_shape=jax.ShapeDtypeStruct((N_LAT, NAT_W), F32),
        scratch_shapes=[
            pltpu.VMEM((_NAT_DR, H_NAT, GRID_W, LANES), F32),
            pltpu.VMEM((_NAT_Q, _NAT_KEYS), F32),
        ],
        compiler_params=pltpu.CompilerParams(
            dimension_semantics=("arbitrary", "arbitrary"), vmem_limit_bytes=VMEM_LIMIT),
        name="nat_attention",
    )(rpb_l.reshape(-1), qn, kn, vn, cache_k, cache_v)


def _out_kernel(x_ref, o_ref, r_ref, on_ref, mod_ref, ng_ref, gng_ref, w_ref, y_ref):
    og = o_ref[0] + o_ref[1]
    parts = []
    for h in range(H_GLA):
        parts.append(_rms(og[:, h * DV_GLA:(h + 1) * DV_GLA], gng_ref[...]))
    merged = (jnp.concatenate(parts, axis=1) * _silu(r_ref[...])).astype(BF16)
    merged = _permute_chunks(_block_transpose_perm(), merged)
    y = _dot(merged, w_ref[0:V_W, :]) + _dot(on_ref[...].astype(BF16), w_ref[V_W:, :])
    y_ref[...] = x_ref[...] + mod_ref[0, 5:6, :] * _rms(y, ng_ref[3:4, :])


def _mixer_out(x, o_gla, r, o_nat, mod_l, ng_l, gng_l, w_out_bf):
    tile = lambda w: pl.BlockSpec((TM, w), lambda i: (i, 0))
    full = lambda a: pl.BlockSpec(a.shape, lambda i: (0,) * a.ndim)
    return pl.pallas_call(
        _out_kernel,
        grid=(N_ALL // TM,),
        in_specs=[
            tile(D_MODEL),
            pl.BlockSpec((2, TM, V_W), lambda i: (0, i, 0)),
            tile(V_W), tile(NAT_W),
            pl.BlockSpec((1, N_MOD, D_MODEL), lambda i: (_group_of_tile(i), 0, 0)),
            full(ng_l), full(gng_l), full(w_out_bf),
        ],
        out_specs=tile(D_MODEL),
        out_shape=jax.ShapeDtypeStruct((N_ALL, D_MODEL), F32),
        compiler_params=pltpu.CompilerParams(
            dimension_semantics=("arbitrary",), vmem_limit_bytes=VMEM_LIMIT),
        name="mixer_out",
    )(x, o_gla, r, o_nat, mod_l, ng_l, gng_l, w_out_bf)


def _rope_tables():
    quarter = DK_GLA // 4
    freqs = ROPE_BASE ** (-jnp.arange(quarter, dtype=F32) / quarter)
    t = jnp.arange(DEC_SEQ)
    ang_r = (t // GRID_W).astype(F32)[:, None] * freqs[None, :]
    ang_c = (t % GRID_W).astype(F32)[:, None] * freqs[None, :]
    cos_h = jnp.concatenate([jnp.cos(ang_r), jnp.cos(ang_r), jnp.cos(ang_c), jnp.cos(ang_c)], axis=1)
    sin_h = jnp.concatenate([-jnp.sin(ang_r), jnp.sin(ang_r), -jnp.sin(ang_c), jnp.sin(ang_c)], axis=1)
    pos_major = lambda a: a.reshape(-1, GLA_NB, GLA_CB, QK_W).transpose(0, 2, 1, 3).reshape(a.shape)
    cos_t = jnp.concatenate([pos_major(jnp.tile(cos_h, (1, H_GLA))), jnp.ones((TM, QK_W), F32)], axis=0)
    sin_t = jnp.concatenate([pos_major(jnp.tile(sin_h, (1, H_GLA))), jnp.zeros((TM, QK_W), F32)], axis=0)
    return cos_t, sin_t


def _swap_perm():
    quarter = DK_GLA // 4
    idx = np.arange(QK_W)
    blk = (idx // quarter) % 4
    return np.where(blk % 2 == 0, idx + quarter, idx - quarter)


def _pack_w_in(w_in_l, gla_wa2_l, gla_ba_l):
    sizes = [QK_W, QK_W, V_W, V_W, GLA_RANK, GLA_RANK, NAT_W, NAT_W, NAT_W]
    cuts = np.cumsum([0] + sizes)
    wq, wk, wv, wr, wlf, wlb, wqn, wkn, wvn = [w_in_l[:, cuts[i]:cuts[i + 1]] for i in range(9)]
    perm = _swap_perm()
    w_big = jnp.concatenate([wq, wk, wq[:, perm], wk[:, perm], wv, wr, wqn, wkn, wvn], axis=1).astype(BF16)
    w_lr = jnp.zeros((D_MODEL, LANES), F32).at[:, 0:GLA_RANK].set(wlf).at[:, GLA_RANK:2 * GLA_RANK].set(wlb)
    w_a = jnp.zeros((LANES, 2 * QK_W), F32)
    w_a = w_a.at[0:GLA_RANK, 0:QK_W].set(gla_wa2_l[0]).at[GLA_RANK:2 * GLA_RANK, QK_W:].set(gla_wa2_l[1])
    b_a = jnp.concatenate([gla_ba_l[0], gla_ba_l[1]])[None, :]
    return w_big, w_lr.astype(BF16), w_a.astype(BF16), b_a


def kernel(x_prompt, x_sample, cache_k, cache_v, state_gla, c, c_ctx, w_mod, b_mod, norm_g, ffn_w_in, ffn_w_out,
           w_in, gla_wa2, gla_ba, gla_norm_g, nat_rpb, w_out):
    cvecs = jnp.zeros((SUBLANES, D_MODEL), F32).at[0].set(c_ctx).at[1:1 + DEC_BATCH].set(c)
    mod = _modulation(cvecs, w_mod, b_mod)[:, :N_GROUPS].reshape(DEPTH, N_GROUPS, N_MOD, D_MODEL)

    ffn_w_in_bf = ffn_w_in.astype(BF16)
    ffn_w_out_bf = ffn_w_out.astype(BF16)
    w_out_bf = w_out.astype(BF16)
    cos_t, sin_t = _rope_tables()
    ck = cache_k.reshape(DEC_BATCH, DEPTH, PAST_LEN, NAT_W)
    cv = cache_v.reshape(DEC_BATCH, DEPTH, PAST_LEN, NAT_W)

    x = jnp.concatenate([x_prompt.reshape(N_CTX, D_MODEL), x_sample.reshape(N_LAT, D_MODEL)], axis=0)
    k_list, v_list, s_list = [], [], []
    for l in range(DEPTH):
        mod_l, ng_l = mod[l], norm_g[l]
        x = _ffn(x, mod_l, ng_l, ffn_w_in_bf, ffn_w_out_bf, layer=l, which=0)

        w_big, w_lr, w_a, b_a = _pack_w_in(w_in[l], gla_wa2[l], gla_ba[l])
        q, k, v, r, g, qn, kn, vn = _project(x, mod_l, ng_l, w_big, w_lr, w_a, b_a, cos_t, sin_t)

        s0 = jnp.concatenate([jnp.zeros((BATCH, 2, H_GLA, DK_GLA, DV_GLA), F32), state_gla[:, l]], axis=0)
        o_gla, s_fin = _gla(q, k, v, g, s0)
        o_ctx = _ctx_attention(qn, kn, vn)
        o_lat = _nat_attention(qn, kn, vn, ck, cv, nat_rpb[l], l)
        o_nat = jnp.concatenate([o_ctx, o_lat], axis=0)

        x = _mixer_out(x, o_gla, r, o_nat, mod_l, ng_l, gla_norm_g[l][None, :], w_out_bf[l])
        x = _ffn(x, mod_l, ng_l, ffn_w_in_bf, ffn_w_out_bf, layer=l, which=1)

        k_list.append(kn[:N_CTX].reshape(BATCH, SEQ, H_NAT, HD_NAT))
        v_list.append(vn[:N_CTX].reshape(BATCH, SEQ, H_NAT, HD_NAT))
        s_list.append(s_fin[:BATCH])

    y_prompt = x[:N_CTX].reshape(BATCH, SEQ, D_MODEL)
    y_sample = x[N_CTX:].reshape(DEC_BATCH, DEC_SEQ, D_MODEL)
    return (y_prompt, y_sample, jnp.stack(k_list, axis=1), jnp.stack(v_list, axis=1), jnp.stack(s_list, axis=1))
```

```python
import functools

import numpy as np
import jax
import jax.numpy as jnp
from jax import lax
from jax.experimental import pallas as pl
from jax.experimental.pallas import tpu as pltpu

D_MODEL = 1024
BATCH = 16
SEQ = 256
DEPTH = 2
DEC_BATCH = 2
DEC_SEQ = 1024
PAST_LEN = 512
GRID_W = 64
H_GLA = 4
DK_GLA = 64
DV_GLA = 128
GLA_RANK = 16
GATE_NORM = 16.0
H_NAT = 8
HD_NAT = 64
WIN_H = 8
WIN_W = 16
D_FF = 2816
N_MOD = 9
ROPE_BASE = 10000.0
EPS = 1e-6
NEG_INF = -1e30

F32 = jnp.float32
BF16 = jnp.bfloat16

N_CTX = BATCH * SEQ
N_LAT = DEC_BATCH * DEC_SEQ
N_ALL = N_CTX + N_LAT
N_GROUPS = 1 + DEC_BATCH
QK_W = H_GLA * DK_GLA
V_W = H_GLA * DV_GLA
NAT_W = H_NAT * HD_NAT
GRID_ROWS = DEC_SEQ // GRID_W

LANES = 128
SUBLANES = 8

TM = 512
TF = 1408
MOD_TN = 1152
GLA_T = 256
GLA_CB = 16
GLA_NB = GLA_T // GLA_CB
VMEM_LIMIT = 56 * 1024 * 1024

assert N_CTX % TM == 0 and DEC_SEQ % TM == 0 and D_FF % TF == 0 and TF % LANES == 0
assert SEQ == GLA_T and DEC_SEQ % GLA_T == 0


def _group_of_tile(i):
    return jnp.where(i < N_CTX // TM, 0, 1 + (i - N_CTX // TM) // (DEC_SEQ // TM))


def _dot(a, b):
    return jnp.dot(a, b, preferred_element_type=F32)


def _dot_nt(a, b):
    return lax.dot_general(a, b, (((1,), (1,)), ((), ())), preferred_element_type=F32)


def _dot_tn(a, b):
    return lax.dot_general(a, b, (((0,), (0,)), ((), ())), preferred_element_type=F32)


def _rms(x, g):
    ms = jnp.mean(x * x, axis=-1, keepdims=True)
    return x * lax.rsqrt(ms + EPS) * g


def _silu(x):
    return x * jax.nn.sigmoid(x)


def _mod_kernel(c_ref, w_ref, b_ref, o_ref):
    s = _silu(c_ref[...]).astype(BF16)
    o_ref[0] = _dot(s, w_ref[0].astype(BF16)) + b_ref[0]


def _modulation(cvecs, w_mod, b_mod):
    n_out = N_MOD * D_MODEL
    return pl.pallas_call(
        _mod_kernel,
        grid=(DEPTH, n_out // MOD_TN),
        in_specs=[
            pl.BlockSpec((SUBLANES, D_MODEL), lambda l, j: (0, 0)),
            pl.BlockSpec((1, D_MODEL, MOD_TN), lambda l, j: (l, 0, j)),
            pl.BlockSpec((1, 1, MOD_TN), lambda l, j: (l, 0, j)),
        ],
        out_specs=pl.BlockSpec((1, SUBLANES, MOD_TN), lambda l, j: (l, 0, j)),
        out_shape=jax.ShapeDtypeStruct((DEPTH, SUBLANES, n_out), F32),
        compiler_params=pltpu.CompilerParams(
            dimension_semantics=("arbitrary", "arbitrary"), vmem_limit_bytes=VMEM_LIMIT),
        name="modulation",
    )(cvecs, w_mod, b_mod.reshape(DEPTH, 1, n_out))


_CTX_TILES = N_CTX // TM
_LAT_TILES = N_LAT // TM


def _ctx_tile(i):
    return jnp.minimum(i, _CTX_TILES - 1)


def _lat_tile(i):
    return jnp.maximum(i - _CTX_TILES, 0)


def _read_split(i, ctx_ref, lat_ref):
    return jnp.where(i < _CTX_TILES, ctx_ref[...], lat_ref[...])


def _ffn_kernel(*refs, m0, n0, split_in, split_out):
    n_x = 2 if split_in else 1
    x_refs, (mod_ref, ng_ref, wg_ref, wu_ref, wo_ref) = refs[:n_x], refs[n_x:n_x + 5]
    n_o = 2 if split_out else 1
    o_refs = refs[n_x + 5:n_x + 5 + n_o]
    h_scr, acc_scr, x_scr = refs[n_x + 5 + n_o:]
    i = pl.program_id(0)
    j = pl.program_id(1)

    @pl.when(j == 0)
    def _():
        x = _read_split(i, *x_refs) if split_in else x_refs[0][...]
        x_scr[...] = x
        h = _rms(x, ng_ref[n0:n0 + 1, :])
        h = h * (1.0 + mod_ref[0, m0 + 1:m0 + 2, :]) + mod_ref[0, m0:m0 + 1, :]
        h_scr[...] = h.astype(BF16)
        acc_scr[...] = jnp.zeros_like(acc_scr)

    h = h_scr[...]
    gate = _dot(h, wg_ref[0, 0])
    up = _dot(h, wu_ref[0, 0])
    act = (_silu(gate) * up).astype(BF16)
    acc_scr[...] += _dot(act, wo_ref[0, 0])

    @pl.when(j == pl.num_programs(1) - 1)
    def _():
        y = _rms(acc_scr[...], ng_ref[n0 + 1:n0 + 2, :])
        out = x_scr[...] + 0.5 * mod_ref[0, m0 + 2:m0 + 3, :] * y
        if split_out:
            @pl.when(i < _CTX_TILES)
            def _():
                o_refs[0][...] = out

            @pl.when(i >= _CTX_TILES)
            def _():
                o_refs[1][...] = out
        else:
            o_refs[0][...] = out


def _ffn(x, mod_l, ng_l, w_in_bf, w_out_bf, *, layer, which, split_out=False):
    nf = D_FF // TF
    m0, n0 = (0, 0) if which == 0 else (6, 4)
    split_in = isinstance(x, tuple)
    tile = pl.BlockSpec((TM, D_MODEL), lambda i, j: (i, 0))
    ctx_tile = pl.BlockSpec((TM, D_MODEL), lambda i, j: (_ctx_tile(i), 0))
    lat_tile = pl.BlockSpec((TM, D_MODEL), lambda i, j: (_lat_tile(i), 0))
    if split_out:
        out_specs = [ctx_tile, lat_tile]
        out_shape = [jax.ShapeDtypeStruct((N_CTX, D_MODEL), F32), jax.ShapeDtypeStruct((N_LAT, D_MODEL), F32)]
    else:
        out_specs = tile
        out_shape = jax.ShapeDtypeStruct((N_ALL, D_MODEL), F32)
    return pl.pallas_call(
        functools.partial(_ffn_kernel, m0=m0, n0=n0, split_in=split_in, split_out=split_out),
        grid=(N_ALL // TM, nf),
        in_specs=([ctx_tile, lat_tile] if split_in else [tile]) + [
            pl.BlockSpec((1, N_MOD, D_MODEL), lambda i, j: (_group_of_tile(i), 0, 0)),
            pl.BlockSpec((6, D_MODEL), lambda i, j: (0, 0)),
            pl.BlockSpec((1, 1, D_MODEL, TF), lambda i, j: (layer, which, 0, j)),
            pl.BlockSpec((1, 1, D_MODEL, TF), lambda i, j: (layer, which, 0, j + nf)),
            pl.BlockSpec((1, 1, TF, D_MODEL), lambda i, j: (layer, which, j, 0)),
        ],
        out_specs=out_specs,
        out_shape=out_shape,
        scratch_shapes=[pltpu.VMEM((TM, D_MODEL), BF16), pltpu.VMEM((TM, D_MODEL), F32),
                        pltpu.VMEM((TM, D_MODEL), F32)],
        compiler_params=pltpu.CompilerParams(
            dimension_semantics=("arbitrary", "arbitrary"), vmem_limit_bytes=VMEM_LIMIT),
        name="ffn",
    )(*(x if split_in else (x,)), mod_l, ng_l, w_in_bf, w_in_bf, w_out_bf)


_P_QK = 0
_P_VR = 4 * QK_W
_P_NAT = _P_VR + 2 * V_W
_P_END = _P_NAT + 3 * NAT_W


def _block_transpose_perm():
    r = lax.broadcasted_iota(jnp.int32, (GLA_T, GLA_T), 0)
    c = lax.broadcasted_iota(jnp.int32, (GLA_T, GLA_T), 1)
    return jnp.where(c == (r & (GLA_CB - 1)) * GLA_NB + (r >> 4), 1.0, 0.0).astype(BF16)


def _permute_chunks(perm, a):
    return jnp.concatenate([_dot(perm, a[c * GLA_T:(c + 1) * GLA_T, :]).astype(BF16)
                            for c in range(a.shape[0] // GLA_T)], axis=0)


def _proj_kernel(x_ref, mod_ref, ng_ref, w_ref, wlr_ref, wa_ref, ba_ref, cos_ref, sin_ref,
                 q_ref, k_ref, v_ref, r_ref, g_ref, qn_ref, kn_ref, vn_ref):
    h = _rms(x_ref[...], ng_ref[2:3, :])
    h = (h * (1.0 + mod_ref[0, 4:5, :]) + mod_ref[0, 3:4, :]).astype(BF16)
    hp = _permute_chunks(_block_transpose_perm(), h)

    qk = _dot(hp, w_ref[:, _P_QK:_P_VR])
    cos = cos_ref[...]
    sin = sin_ref[...]
    q = qk[:, 0:QK_W] * cos + qk[:, 2 * QK_W:3 * QK_W] * sin
    q_ref[...] = q * (DK_GLA ** -0.5)
    k_ref[...] = qk[:, QK_W:2 * QK_W] * cos + qk[:, 3 * QK_W:4 * QK_W] * sin

    vr = _dot(hp, w_ref[:, _P_VR:_P_NAT])
    v_ref[...] = vr[:, 0:V_W]
    r_ref[...] = vr[:, V_W:2 * V_W]

    nat = _dot(h, w_ref[:, _P_NAT:_P_END])
    qn_ref[...] = nat[:, 0:NAT_W]
    kn_ref[...] = nat[:, NAT_W:2 * NAT_W]
    vn_ref[...] = nat[:, 2 * NAT_W:3 * NAT_W]

    lr = _dot(hp, wlr_ref[...]).astype(BF16)
    z = _dot(lr, wa_ref[...]) + ba_ref[...]
    g = (jnp.minimum(z, 0.0) - jnp.log1p(jnp.exp(-jnp.abs(z)))) * (1.0 / GATE_NORM)
    g_ref[0] = g[:, 0:QK_W]
    g_ref[1] = g[:, QK_W:2 * QK_W]


def _rope_table_block(i):
    lat_tiles = DEC_SEQ // TM
    return jnp.where(i < N_CTX // TM, lat_tiles, (i - N_CTX // TM) % lat_tiles)


def _project(x, mod_l, ng_l, w_big, w_lr, w_a, b_a, cos_t, sin_t):
    tile = lambda w: pl.BlockSpec((TM, w), lambda i: (i, 0))
    full = lambda a: pl.BlockSpec(a.shape, lambda i: (0,) * a.ndim)
    return pl.pallas_call(
        _proj_kernel,
        grid=(N_ALL // TM,),
        in_specs=[
            tile(D_MODEL),
            pl.BlockSpec((1, N_MOD, D_MODEL), lambda i: (_group_of_tile(i), 0, 0)),
            full(ng_l), full(w_big), full(w_lr), full(w_a), full(b_a),
            pl.BlockSpec((TM, QK_W), lambda i: (_rope_table_block(i), 0)),
            pl.BlockSpec((TM, QK_W), lambda i: (_rope_table_block(i), 0)),
        ],
        out_specs=[
            tile(QK_W), tile(QK_W), tile(V_W), tile(V_W),
            pl.BlockSpec((2, TM, QK_W), lambda i: (0, i, 0)),
            tile(NAT_W), tile(NAT_W), tile(NAT_W),
        ],
        out_shape=[
            jax.ShapeDtypeStruct((N_ALL, QK_W), F32), jax.ShapeDtypeStruct((N_ALL, QK_W), F32),
            jax.ShapeDtypeStruct((N_ALL, V_W), F32), jax.ShapeDtypeStruct((N_ALL, V_W), F32),
            jax.ShapeDtypeStruct((2, N_ALL, QK_W), F32),
            jax.ShapeDtypeStruct((N_ALL, NAT_W), F32), jax.ShapeDtypeStruct((N_ALL, NAT_W), F32),
            jax.ShapeDtypeStruct((N_ALL, NAT_W), F32),
        ],
        compiler_params=pltpu.CompilerParams(
            dimension_semantics=("arbitrary",), vmem_limit_bytes=VMEM_LIMIT),
        name="mixer_proj",
    )(x, mod_l, ng_l, w_big, w_lr, w_a, b_a, cos_t, sin_t)


def _gla_tables():
    rows = []
    seq_specs = [(b * (SEQ // GLA_T), SEQ // GLA_T) for b in range(BATCH)]
    seq_specs += [(N_CTX // GLA_T + b * (DEC_SEQ // GLA_T), DEC_SEQ // GLA_T) for b in range(DEC_BATCH)]
    for sid, (blk0, nchunk) in enumerate(seq_specs):
        for direction in (0, 1):
            order = range(nchunk) if direction == 0 else range(nchunk - 1, -1, -1)
            for n, c in enumerate(order):
                rows.append((blk0 + c, direction, int(n == 0), sid))
    return np.asarray(rows, dtype=np.int32).T.copy()


_GLA_TAB = _gla_tables()
_GLA_ITEMS = _GLA_TAB.shape[1]
_GLA_NSEQ = BATCH + DEC_BATCH
_GLA_PAIR_ROWS = GLA_NB * GLA_CB * (GLA_CB + 1) // 2


def _gla_item(direction, q_ref, k_ref, v_ref, g_ref, o_ref, s_scr, cp, sstk, p_scr, w_scr):
    T, CB, NB = GLA_T, GLA_CB, GLA_NB
    fwd = direction == 0
    slab = lambda i: slice(i * NB, (i + 1) * NB)

    cum = None
    for i in (range(CB) if fwd else range(CB - 1, -1, -1)):
        gi = g_ref[0, slab(i), :]
        cum = gi if cum is None else cum + gi
        cp[slab(i), :] = cum
    total = cum
    cum_all = cp[...]
    qt = q_ref[...] * jnp.exp(cum_all)
    kh = k_ref[...] * jnp.exp(jnp.concatenate([total] * CB, axis=0) - cum_all)
    dec_t = jnp.concatenate([jnp.exp(total), jnp.zeros((LANES - NB, QK_W), F32)], axis=0).T

    erow = lax.broadcasted_iota(jnp.int32, (QK_W, QK_W), 0)
    ecol = lax.broadcasted_iota(jnp.int32, (QK_W, QK_W), 1)
    head_sum = jnp.where((erow >> 6) == (ecol >> 6), 1.0, 0.0).astype(BF16)

    key_positions = lambda i: range(i + 1) if fwd else range(i, CB)
    r0 = 0
    for i in range(CB):
        qi = q_ref[slab(i), :]
        ci = cp[slab(i), :]
        for j in key_positions(i):
            e = jnp.exp(ci - cp[slab(j), :])
            p_scr[r0:r0 + NB, :] = (qi * k_ref[slab(j), :] * e).astype(BF16)
            r0 += NB
    w_scr[...] = _dot(p_scr[...], head_sum)
    first_half = lax.broadcasted_iota(jnp.int32, (NB, LANES), 1) < DK_GLA
    r0 = 0
    for i in range(CB):
        acc = None
        for j in key_positions(i):
            spread = []
            for pair in range(H_GLA // 2):
                tile = w_scr[r0:r0 + NB, pair * LANES:(pair + 1) * LANES]
                other = pltpu.roll(tile, DK_GLA, axis=1)
                spread += [jnp.where(first_half, tile, other), jnp.where(first_half, other, tile)]
            term = jnp.concatenate(spread, axis=1) * v_ref[slab(j), :]
            acc = term if acc is None else acc + term
            r0 += NB
        o_ref[0, slab(i), :] = acc

    kht = kh.T.astype(BF16)
    v_bf = v_ref[...].astype(BF16)
    key_blk = lax.broadcasted_iota(jnp.int32, (DK_GLA, T), 1) & (NB - 1)
    row_blk = lax.broadcasted_iota(jnp.int32, (T, LANES), 0) & (NB - 1)
    lane_half = lax.broadcasted_iota(jnp.int32, (T, LANES), 1) >> 6
    order = range(NB) if fwd else range(NB - 1, -1, -1)
    for h in range(H_GLA):
        kh_h = kht[h * DK_GLA:(h + 1) * DK_GLA, :]
        kv = _dot(jnp.concatenate([jnp.where(key_blk == b, kh_h, 0) for b in range(NB)], axis=0),
                  v_bf[:, h * DV_GLA:(h + 1) * DV_GLA])
        s = s_scr[h]
        for b in order:
            sstk[h, b * DK_GLA:(b + 1) * DK_GLA, :] = s.astype(BF16)
            s = dec_t[h * DK_GLA:(h + 1) * DK_GLA, b:b + 1] * s + kv[b * DK_GLA:(b + 1) * DK_GLA, :]
        s_scr[h] = s

        pair_tile = qt[:, (h // 2) * LANES:(h // 2 + 1) * LANES]
        both = jnp.where(lane_half == h % 2, pair_tile, pltpu.roll(pair_tile, DK_GLA, axis=1))
        lhs = jnp.concatenate([jnp.where(row_blk == 2 * j + lane_half, both, 0.0).astype(BF16)
                               for j in range(NB // 2)], axis=1)
        o_ref[0, :, h * DV_GLA:(h + 1) * DV_GLA] += _dot(lhs, sstk[h])


def _gla_kernel(tab_ref, q_ref, k_ref, v_ref, g_ref, s0_ref, o_ref, so_ref, s_scr, cp, sstk, p_scr, w_scr):
    it = pl.program_id(0)

    @pl.when(tab_ref[2, it] == 1)
    def _():
        s_scr[...] = s0_ref[0, 0]

    for direction in (0, 1):
        @pl.when(tab_ref[1, it] == direction)
        def _(direction=direction):
            _gla_item(direction, q_ref, k_ref, v_ref, g_ref, o_ref, s_scr, cp, sstk, p_scr, w_scr)

    so_ref[0, 0] = s_scr[...]


def _gla(q, k, v, g, s0):
    tok = lambda w: pl.BlockSpec((GLA_T, w), lambda it, tab: (tab[0, it], 0))
    state = pl.BlockSpec((1, 1, H_GLA, DK_GLA, DV_GLA), lambda it, tab: (tab[3, it], tab[1, it], 0, 0, 0))
    grid_spec = pltpu.PrefetchScalarGridSpec(
        num_scalar_prefetch=1,
        grid=(_GLA_ITEMS,),
        in_specs=[
            tok(QK_W), tok(QK_W), tok(V_W),
            pl.BlockSpec((1, GLA_T, QK_W), lambda it, tab: (tab[1, it], tab[0, it], 0)),
            state,
        ],
        out_specs=[
            pl.BlockSpec((1, GLA_T, V_W), lambda it, tab: (tab[1, it], tab[0, it], 0)),
            state,
        ],
        scratch_shapes=[
            pltpu.VMEM((H_GLA, DK_GLA, DV_GLA), F32),
            pltpu.VMEM((GLA_T, QK_W), F32),
            pltpu.VMEM((H_GLA, GLA_NB * DK_GLA, DV_GLA), BF16),
            pltpu.VMEM((_GLA_PAIR_ROWS, QK_W), BF16),
            pltpu.VMEM((_GLA_PAIR_ROWS, QK_W), F32),
        ],
    )
    return pl.pallas_call(
        _gla_kernel,
        grid_spec=grid_spec,
        out_shape=[
            jax.ShapeDtypeStruct((2, N_ALL, V_W), F32),
            jax.ShapeDtypeStruct((_GLA_NSEQ, 2, H_GLA, DK_GLA, DV_GLA), F32),
        ],
        compiler_params=pltpu.CompilerParams(
            dimension_semantics=("arbitrary",), vmem_limit_bytes=VMEM_LIMIT),
        name="gla",
    )(jnp.asarray(_GLA_TAB), q, k, v, g, s0)


def _softmax_pv(s_list, v_list):
    m = s_list[0].max(axis=-1, keepdims=True)
    for s in s_list[1:]:
        m = jnp.maximum(m, s.max(axis=-1, keepdims=True))
    num = None
    den = None
    for s, vv in zip(s_list, v_list):
        e = jnp.exp(s - m)
        den = e.sum(axis=-1, keepdims=True) if den is None else den + e.sum(axis=-1, keepdims=True)
        pv = _dot(e.astype(BF16), vv)
        num = pv if num is None else num + pv
    return num / den


def _ctx_attn_kernel(q_ref, k_ref, v_ref, o_ref):
    scale = HD_NAT ** -0.5
    lane = lax.broadcasted_iota(jnp.int32, (SEQ, LANES), 1)
    for t in range(NAT_W // LANES):
        sl = slice(t * LANES, (t + 1) * LANES)
        qt = q_ref[:, sl]
        kt = k_ref[:, sl].astype(BF16)
        vt = v_ref[:, sl].astype(BF16)
        out = jnp.zeros((SEQ, LANES), F32)
        for half in range(LANES // HD_NAT):
            mine = (lane >> 6) == half
            s = _dot_nt(jnp.where(mine, qt, 0.0).astype(BF16), kt) * scale
            out = jnp.where(mine, _softmax_pv([s], [vt]), out)
        o_ref[:, sl] = out


def _ctx_attention(qn, kn, vn):
    spec = pl.BlockSpec((SEQ, NAT_W), lambda b: (b, 0))
    return pl.pallas_call(
        _ctx_attn_kernel,
        grid=(BATCH,),
        in_specs=[spec, spec, spec],
        out_specs=spec,
        out_shape=jax.ShapeDtypeStruct((N_CTX, NAT_W), F32),
        compiler_params=pltpu.CompilerParams(
            dimension_semantics=("arbitrary",), vmem_limit_bytes=VMEM_LIMIT),
        name="ctx_attention",
    )(qn, kn, vn)


_NAT_QROWS = 4
_NAT_GROUPS = GRID_ROWS // _NAT_QROWS
_NAT_KROWS = WIN_H + _NAT_QROWS
_NAT_Q = _NAT_QROWS * GRID_W
_NAT_KEYS = _NAT_KROWS * GRID_W
_NAT_DR = 2 * WIN_H - 1
_NAT_DC = 2 * WIN_W - 1


def _nat_key_row0(r):
    return jnp.clip(r - WIN_H // 2, 0, GRID_ROWS - WIN_H)


def _nat_build_bias(rpb_ref, tz_scr):
    c = lax.broadcasted_iota(jnp.int32, (GRID_W, LANES), 0)
    lane = lax.broadcasted_iota(jnp.int32, (GRID_W, LANES), 1)
    kc = lane & (GRID_W - 1)
    second = (lane >> 6) == 1
    rel = kc - c + (WIN_W - 1)
    win_start = jnp.clip(c - WIN_W // 2, 0, GRID_W - WIN_W)
    valid = (kc >= win_start) & (kc < win_start + WIN_W)

    def body(n, carry):
        dr = n >> 3
        h = n & (H_NAT - 1)
        base0 = (h * _NAT_DR + dr) * _NAT_DC
        base1 = (h * _NAT_DR + jnp.minimum(dr + 1, _NAT_DR - 1)) * _NAT_DC
        acc = jnp.full((GRID_W, LANES), NEG_INF, F32)
        for j in range(_NAT_DC):
            val = jnp.where(second, rpb_ref[base1 + j], rpb_ref[base0 + j])
            acc = jnp.where(valid & (rel == j), val, acc)
        tz_scr[dr, h] = acc
        return carry

    lax.fori_loop(0, _NAT_DR * H_NAT, body, 0)


def _nat_kernel(rpb_ref, q_ref, k_ref, v_ref, ck_ref, cv_ref, o_ref, tz_scr, bias_scr):
    grp = pl.program_id(1)

    @pl.when((pl.program_id(0) == 0) & (grp == 0))
    def _():
        _nat_build_bias(rpb_ref, tz_scr)

    scale = HD_NAT ** -0.5
    krow0 = jnp.clip(_NAT_QROWS * grp - WIN_H // 2, 0, GRID_ROWS - _NAT_KROWS)
    k0 = pl.multiple_of(krow0 * GRID_W, GRID_W)
    lane_q = lax.broadcasted_iota(jnp.int32, (_NAT_Q, LANES), 1)
    lane_b = lax.broadcasted_iota(jnp.int32, (GRID_W, LANES), 1)
    neg = jnp.full((GRID_W, LANES), NEG_INF, F32)
    for t in range(NAT_W // LANES):
        sl = slice(t * LANES, (t + 1) * LANES)
        qt = q_ref[:, sl]
        kw = k_ref[pl.ds(k0, _NAT_KEYS), sl].astype(BF16)
        vw = v_ref[pl.ds(k0, _NAT_KEYS), sl].astype(BF16)
        ck = ck_ref[0, 0, :, sl].astype(BF16)
        cv = cv_ref[0, 0, :, sl].astype(BF16)
        out = jnp.zeros((_NAT_Q, LANES), F32)
        for half in range(LANES // HD_NAT):
            h = 2 * t + half
            for qr in range(_NAT_QROWS):
                r = _NAT_QROWS * grp + qr
                lo = _nat_key_row0(r)
                for kp in range(_NAT_KROWS // 2):
                    kr = krow0 + 2 * kp
                    tile = tz_scr[jnp.clip(kr - r + (WIN_H - 1), 0, _NAT_DR - 1), h]
                    ok_a = ((kr >= lo) & (kr < lo + WIN_H)).astype(jnp.int32)
                    ok_b = ((kr + 1 >= lo) & (kr + 1 < lo + WIN_H)).astype(jnp.int32)
                    ok = jnp.where(lane_b < GRID_W, ok_a, ok_b) == 1
                    bias_scr[qr * GRID_W:(qr + 1) * GRID_W, kp * LANES:(kp + 1) * LANES] = jnp.where(ok, tile, neg)
            mine = (lane_q >> 6) == half
            qm = jnp.where(mine, qt, 0.0).astype(BF16)
            s_win = _dot_nt(qm, kw) * scale + bias_scr[...]
            s_ctx = _dot_nt(qm, ck) * scale
            out = jnp.where(mine, _softmax_pv([s_win, s_ctx], [vw, cv]), out)
        o_ref[:, sl] = out


def _nat_attention(qn, kn, vn, cache_k, cache_v, rpb_l, layer):
    lat0 = N_CTX // _NAT_Q
    return pl.pallas_call(
        _nat_kernel,
        grid=(DEC_BATCH, _NAT_GROUPS),
        in_specs=[
            pl.BlockSpec(memory_space=pltpu.SMEM),
            pl.BlockSpec((_NAT_Q, NAT_W), lambda b, g: (lat0 + b * _NAT_GROUPS + g, 0)),
            pl.BlockSpec((DEC_SEQ, NAT_W), lambda b, g: (N_CTX // DEC_SEQ + b, 0)),
            pl.BlockSpec((DEC_SEQ, NAT_W), lambda b, g: (N_CTX // DEC_SEQ + b, 0)),
            pl.BlockSpec((1, 1, PAST_LEN, NAT_W), lambda b, g: (b, layer, 0, 0)),
            pl.BlockSpec((1, 1, PAST_LEN, NAT_W), lambda b, g: (b, layer, 0, 0)),
        ],
        out_specs=pl.BlockSpec((_NAT_Q, NAT_W), lambda b, g: (b * _NAT_GROUPS + g, 0)),
        out_shape=jax.ShapeDtypeStruct((N_LAT, NAT_W), F32),
        scratch_shapes=[
            pltpu.VMEM((_NAT_DR, H_NAT, GRID_W, LANES), F32),
            pltpu.VMEM((_NAT_Q, _NAT_KEYS), F32),
        ],
        compiler_params=pltpu.CompilerParams(
            dimension_semantics=("arbitrary", "arbitrary"), vmem_limit_bytes=VMEM_LIMIT),
        name="nat_attention",
    )(rpb_l.reshape(-1), qn, kn, vn, cache_k, cache_v)


def _out_kernel(x_ref, o_ref, r_ref, on_ctx_ref, on_lat_ref, mod_ref, ng_ref, gng_ref, w_ref, y_ref):
    o_nat = _read_split(pl.program_id(0), on_ctx_ref, on_lat_ref)
    og = o_ref[0] + o_ref[1]
    parts = []
    for h in range(H_GLA):
        parts.append(_rms(og[:, h * DV_GLA:(h + 1) * DV_GLA], gng_ref[...]))
    merged = (jnp.concatenate(parts, axis=1) * _silu(r_ref[...])).astype(BF16)
    merged = _permute_chunks(_block_transpose_perm(), merged)
    y = _dot(merged, w_ref[0:V_W, :]) + _dot(o_nat.astype(BF16), w_ref[V_W:, :])
    y_ref[...] = x_ref[...] + mod_ref[0, 5:6, :] * _rms(y, ng_ref[3:4, :])


def _mixer_out(x, o_gla, r, o_ctx, o_lat, mod_l, ng_l, gng_l, w_out_bf):
    tile = lambda w: pl.BlockSpec((TM, w), lambda i: (i, 0))
    full = lambda a: pl.BlockSpec(a.shape, lambda i: (0,) * a.ndim)
    return pl.pallas_call(
        _out_kernel,
        grid=(N_ALL // TM,),
        in_specs=[
            tile(D_MODEL),
            pl.BlockSpec((2, TM, V_W), lambda i: (0, i, 0)),
            tile(V_W),
            pl.BlockSpec((TM, NAT_W), lambda i: (_ctx_tile(i), 0)),
            pl.BlockSpec((TM, NAT_W), lambda i: (_lat_tile(i), 0)),
            pl.BlockSpec((1, N_MOD, D_MODEL), lambda i: (_group_of_tile(i), 0, 0)),
            full(ng_l), full(gng_l), full(w_out_bf),
        ],
        out_specs=tile(D_MODEL),
        out_shape=jax.ShapeDtypeStruct((N_ALL, D_MODEL), F32),
        compiler_params=pltpu.CompilerParams(
            dimension_semantics=("arbitrary",), vmem_limit_bytes=VMEM_LIMIT),
        name="mixer_out",
    )(x, o_gla, r, o_ctx, o_lat, mod_l, ng_l, gng_l, w_out_bf)


def _rope_tables():
    quarter = DK_GLA // 4
    freqs = ROPE_BASE ** (-jnp.arange(quarter, dtype=F32) / quarter)
    t = jnp.arange(DEC_SEQ)
    ang_r = (t // GRID_W).astype(F32)[:, None] * freqs[None, :]
    ang_c = (t % GRID_W).astype(F32)[:, None] * freqs[None, :]
    cos_h = jnp.concatenate([jnp.cos(ang_r), jnp.cos(ang_r), jnp.cos(ang_c), jnp.cos(ang_c)], axis=1)
    sin_h = jnp.concatenate([-jnp.sin(ang_r), jnp.sin(ang_r), -jnp.sin(ang_c), jnp.sin(ang_c)], axis=1)
    pos_major = lambda a: a.reshape(-1, GLA_NB, GLA_CB, QK_W).transpose(0, 2, 1, 3).reshape(a.shape)
    cos_t = jnp.concatenate([pos_major(jnp.tile(cos_h, (1, H_GLA))), jnp.ones((TM, QK_W), F32)], axis=0)
    sin_t = jnp.concatenate([pos_major(jnp.tile(sin_h, (1, H_GLA))), jnp.zeros((TM, QK_W), F32)], axis=0)
    return cos_t, sin_t


def _swap_perm():
    quarter = DK_GLA // 4
    idx = np.arange(QK_W)
    blk = (idx // quarter) % 4
    return np.where(blk % 2 == 0, idx + quarter, idx - quarter)


def _pack_w_in(w_in_l, gla_wa2_l, gla_ba_l):
    sizes = [QK_W, QK_W, V_W, V_W, GLA_RANK, GLA_RANK, NAT_W, NAT_W, NAT_W]
    cuts = np.cumsum([0] + sizes)
    wq, wk, wv, wr, wlf, wlb, wqn, wkn, wvn = [w_in_l[:, cuts[i]:cuts[i + 1]] for i in range(9)]
    perm = _swap_perm()
    w_big = jnp.concatenate([wq, wk, wq[:, perm], wk[:, perm], wv, wr, wqn, wkn, wvn], axis=1).astype(BF16)
    w_lr = jnp.zeros((D_MODEL, LANES), F32).at[:, 0:GLA_RANK].set(wlf).at[:, GLA_RANK:2 * GLA_RANK].set(wlb)
    w_a = jnp.zeros((LANES, 2 * QK_W), F32)
    w_a = w_a.at[0:GLA_RANK, 0:QK_W].set(gla_wa2_l[0]).at[GLA_RANK:2 * GLA_RANK, QK_W:].set(gla_wa2_l[1])
    b_a = jnp.concatenate([gla_ba_l[0], gla_ba_l[1]])[None, :]
    return w_big, w_lr.astype(BF16), w_a.astype(BF16), b_a


def kernel(x_prompt, x_sample, cache_k, cache_v, state_gla, c, c_ctx, w_mod, b_mod, norm_g, ffn_w_in, ffn_w_out,
           w_in, gla_wa2, gla_ba, gla_norm_g, nat_rpb, w_out):
    cvecs = jnp.zeros((SUBLANES, D_MODEL), F32).at[0].set(c_ctx).at[1:1 + DEC_BATCH].set(c)
    mod = _modulation(cvecs, w_mod, b_mod)[:, :N_GROUPS].reshape(DEPTH, N_GROUPS, N_MOD, D_MODEL)

    ffn_w_in_bf = ffn_w_in.astype(BF16)
    ffn_w_out_bf = ffn_w_out.astype(BF16)
    w_out_bf = w_out.astype(BF16)
    cos_t, sin_t = _rope_tables()
    ck = cache_k.reshape(DEC_BATCH, DEPTH, PAST_LEN, NAT_W)
    cv = cache_v.reshape(DEC_BATCH, DEPTH, PAST_LEN, NAT_W)

    x = (x_prompt.reshape(N_CTX, D_MODEL), x_sample.reshape(N_LAT, D_MODEL))
    k_list, v_list, s_list = [], [], []
    for l in range(DEPTH):
        mod_l, ng_l = mod[l], norm_g[l]
        x = _ffn(x, mod_l, ng_l, ffn_w_in_bf, ffn_w_out_bf, layer=l, which=0)

        w_big, w_lr, w_a, b_a = _pack_w_in(w_in[l], gla_wa2[l], gla_ba[l])
        q, k, v, r, g, qn, kn, vn = _project(x, mod_l, ng_l, w_big, w_lr, w_a, b_a, cos_t, sin_t)

        s0 = jnp.concatenate([jnp.zeros((BATCH, 2, H_GLA, DK_GLA, DV_GLA), F32), state_gla[:, l]], axis=0)
        o_gla, s_fin = _gla(q, k, v, g, s0)
        o_ctx = _ctx_attention(qn, kn, vn)
        o_lat = _nat_attention(qn, kn, vn, ck, cv, nat_rpb[l], l)

        x = _mixer_out(x, o_gla, r, o_ctx, o_lat, mod_l, ng_l, gla_norm_g[l][None, :], w_out_bf[l])
        x = _ffn(x, mod_l, ng_l, ffn_w_in_bf, ffn_w_out_bf, layer=l, which=1, split_out=(l == DEPTH - 1))

        k_list.append(kn[:N_CTX].reshape(BATCH, SEQ, H_NAT, HD_NAT))
        v_list.append(vn[:N_CTX].reshape(BATCH, SEQ, H_NAT, HD_NAT))
        s_list.append(s_fin[:BATCH])

    y_prompt = x[0].reshape(BATCH, SEQ, D_MODEL)
    y_sample = x[1].reshape(DEC_BATCH, DEC_SEQ, D_MODEL)
    return (y_prompt, y_sample, jnp.stack(k_list, axis=1), jnp.stack(v_list, axis=1), jnp.stack(s_list, axis=1))
```

```python
import functools

import numpy as np
import jax
import jax.numpy as jnp
from jax import lax
from jax.experimental import pallas as pl
from jax.experimental.pallas import tpu as pltpu

D_MODEL = 1024
BATCH = 16
SEQ = 256
DEPTH = 2
DEC_BATCH = 2
DEC_SEQ = 1024
PAST_LEN = 512
GRID_W = 64
H_GLA = 4
DK_GLA = 64
DV_GLA = 128
GLA_RANK = 16
GATE_NORM = 16.0
H_NAT = 8
HD_NAT = 64
WIN_H = 8
WIN_W = 16
D_FF = 2816
N_MOD = 9
ROPE_BASE = 10000.0
EPS = 1e-6
NEG_INF = -1e30

F32 = jnp.float32
BF16 = jnp.bfloat16

N_CTX = BATCH * SEQ
N_LAT = DEC_BATCH * DEC_SEQ
N_ALL = N_CTX + N_LAT
N_GROUPS = 1 + DEC_BATCH
QK_W = H_GLA * DK_GLA
V_W = H_GLA * DV_GLA
NAT_W = H_NAT * HD_NAT
GRID_ROWS = DEC_SEQ // GRID_W

LANES = 128
SUBLANES = 8

TM = 512
TF = 1408
MOD_TN = 1152
GLA_T = 256
GLA_CB = 16
GLA_NB = GLA_T // GLA_CB
VMEM_LIMIT = 56 * 1024 * 1024

assert N_CTX % TM == 0 and DEC_SEQ % TM == 0 and D_FF % TF == 0 and TF % LANES == 0
assert SEQ == GLA_T and DEC_SEQ % GLA_T == 0


def _group_of_tile(i):
    return jnp.where(i < N_CTX // TM, 0, 1 + (i - N_CTX // TM) // (DEC_SEQ // TM))


def _dot(a, b):
    return jnp.dot(a, b, preferred_element_type=F32)


def _dot_nt(a, b):
    return lax.dot_general(a, b, (((1,), (1,)), ((), ())), preferred_element_type=F32)


def _dot_tn(a, b):
    return lax.dot_general(a, b, (((0,), (0,)), ((), ())), preferred_element_type=F32)


def _rms(x, g):
    ms = jnp.mean(x * x, axis=-1, keepdims=True)
    return x * lax.rsqrt(ms + EPS) * g


def _silu(x):
    return x * jax.nn.sigmoid(x)


def _mod_kernel(c_ref, w_ref, b_ref, o_ref):
    s = _silu(c_ref[...]).astype(BF16)
    o_ref[0] = _dot(s, w_ref[0].astype(BF16)) + b_ref[0]


def _modulation(cvecs, w_mod, b_mod):
    n_out = N_MOD * D_MODEL
    return pl.pallas_call(
        _mod_kernel,
        grid=(DEPTH, n_out // MOD_TN),
        in_specs=[
            pl.BlockSpec((SUBLANES, D_MODEL), lambda l, j: (0, 0)),
            pl.BlockSpec((1, D_MODEL, MOD_TN), lambda l, j: (l, 0, j)),
            pl.BlockSpec((1, 1, MOD_TN), lambda l, j: (l, 0, j)),
        ],
        out_specs=pl.BlockSpec((1, SUBLANES, MOD_TN), lambda l, j: (l, 0, j)),
        out_shape=jax.ShapeDtypeStruct((DEPTH, SUBLANES, n_out), F32),
        compiler_params=pltpu.CompilerParams(
            dimension_semantics=("arbitrary", "arbitrary"), vmem_limit_bytes=VMEM_LIMIT),
        name="modulation",
    )(cvecs, w_mod, b_mod.reshape(DEPTH, 1, n_out))


_CTX_TILES = N_CTX // TM
_LAT_TILES = N_LAT // TM


def _ctx_tile(i):
    return jnp.minimum(i, _CTX_TILES - 1)


def _lat_tile(i):
    return jnp.maximum(i - _CTX_TILES, 0)


def _read_split(i, ctx_ref, lat_ref):
    return jnp.where(i < _CTX_TILES, ctx_ref[...], lat_ref[...])


def _ffn_kernel(*refs, m0, n0, split_in, split_out):
    n_x = 2 if split_in else 1
    x_refs, (mod_ref, ng_ref, wg_ref, wu_ref, wo_ref) = refs[:n_x], refs[n_x:n_x + 5]
    n_o = 2 if split_out else 1
    o_refs = refs[n_x + 5:n_x + 5 + n_o]
    h_scr, acc_scr, x_scr = refs[n_x + 5 + n_o:]
    i = pl.program_id(0)
    j = pl.program_id(1)

    @pl.when(j == 0)
    def _():
        x = _read_split(i, *x_refs) if split_in else x_refs[0][...]
        x_scr[...] = x
        h = _rms(x, ng_ref[n0:n0 + 1, :])
        h = h * (1.0 + mod_ref[0, m0 + 1:m0 + 2, :]) + mod_ref[0, m0:m0 + 1, :]
        h_scr[...] = h.astype(BF16)
        acc_scr[...] = jnp.zeros_like(acc_scr)

    h = h_scr[...]
    gate = _dot(h, wg_ref[0, 0])
    up = _dot(h, wu_ref[0, 0])
    act = (_silu(gate) * up).astype(BF16)
    acc_scr[...] += _dot(act, wo_ref[0, 0])

    @pl.when(j == pl.num_programs(1) - 1)
    def _():
        y = _rms(acc_scr[...], ng_ref[n0 + 1:n0 + 2, :])
        out = x_scr[...] + 0.5 * mod_ref[0, m0 + 2:m0 + 3, :] * y
        if split_out:
            @pl.when(i < _CTX_TILES)
            def _():
                o_refs[0][...] = out

            @pl.when(i >= _CTX_TILES)
            def _():
                o_refs[1][...] = out
        else:
            o_refs[0][...] = out


def _ffn(x, mod_l, ng_l, w_in_bf, w_out_bf, *, layer, which, split_out=False):
    nf = D_FF // TF
    m0, n0 = (0, 0) if which == 0 else (6, 4)
    split_in = isinstance(x, tuple)
    tile = pl.BlockSpec((TM, D_MODEL), lambda i, j: (i, 0))
    ctx_tile = pl.BlockSpec((TM, D_MODEL), lambda i, j: (_ctx_tile(i), 0))
    lat_tile = pl.BlockSpec((TM, D_MODEL), lambda i, j: (_lat_tile(i), 0))
    if split_out:
        out_specs = [ctx_tile, lat_tile]
        out_shape = [jax.ShapeDtypeStruct((N_CTX, D_MODEL), F32), jax.ShapeDtypeStruct((N_LAT, D_MODEL), F32)]
    else:
        out_specs = tile
        out_shape = jax.ShapeDtypeStruct((N_ALL, D_MODEL), F32)
    return pl.pallas_call(
        functools.partial(_ffn_kernel, m0=m0, n0=n0, split_in=split_in, split_out=split_out),
        grid=(N_ALL // TM, nf),
        in_specs=([ctx_tile, lat_tile] if split_in else [tile]) + [
            pl.BlockSpec((1, N_MOD, D_MODEL), lambda i, j: (_group_of_tile(i), 0, 0)),
            pl.BlockSpec((6, D_MODEL), lambda i, j: (0, 0)),
            pl.BlockSpec((1, 1, D_MODEL, TF), lambda i, j: (layer, which, 0, j)),
            pl.BlockSpec((1, 1, D_MODEL, TF), lambda i, j: (layer, which, 0, j + nf)),
            pl.BlockSpec((1, 1, TF, D_MODEL), lambda i, j: (layer, which, j, 0)),
        ],
        out_specs=out_specs,
        out_shape=out_shape,
        scratch_shapes=[pltpu.VMEM((TM, D_MODEL), BF16), pltpu.VMEM((TM, D_MODEL), F32),
                        pltpu.VMEM((TM, D_MODEL), F32)],
        compiler_params=pltpu.CompilerParams(
            dimension_semantics=("arbitrary", "arbitrary"), vmem_limit_bytes=VMEM_LIMIT),
        name="ffn",
    )(*(x if split_in else (x,)), mod_l, ng_l, w_in_bf, w_in_bf, w_out_bf)


_P_QK = 0
_P_VR = 4 * QK_W
_P_NAT = _P_VR + 2 * V_W
_P_END = _P_NAT + 3 * NAT_W


def _block_transpose_perm():
    r = lax.broadcasted_iota(jnp.int32, (GLA_T, GLA_T), 0)
    c = lax.broadcasted_iota(jnp.int32, (GLA_T, GLA_T), 1)
    return jnp.where(c == (r & (GLA_CB - 1)) * GLA_NB + (r >> 4), 1.0, 0.0).astype(BF16)


def _permute_chunks(perm, a):
    return jnp.concatenate([_dot(perm, a[c * GLA_T:(c + 1) * GLA_T, :]).astype(BF16)
                            for c in range(a.shape[0] // GLA_T)], axis=0)


def _proj_kernel(x_ref, mod_ref, ng_ref, w_ref, wlr_ref, wa_ref, ba_ref, cos_ref, sin_ref,
                 q_ref, k_ref, v_ref, r_ref, g_ref, qn_ref, kn_ref, vn_ref):
    h = _rms(x_ref[...], ng_ref[2:3, :])
    h = (h * (1.0 + mod_ref[0, 4:5, :]) + mod_ref[0, 3:4, :]).astype(BF16)
    hp = _permute_chunks(_block_transpose_perm(), h)

    qk = _dot(hp, w_ref[:, _P_QK:_P_VR])
    cos = cos_ref[...]
    sin = sin_ref[...]
    q = qk[:, 0:QK_W] * cos + qk[:, 2 * QK_W:3 * QK_W] * sin
    q_ref[...] = q * (DK_GLA ** -0.5)
    k_ref[...] = qk[:, QK_W:2 * QK_W] * cos + qk[:, 3 * QK_W:4 * QK_W] * sin

    vr = _dot(hp, w_ref[:, _P_VR:_P_NAT])
    v_ref[...] = vr[:, 0:V_W]
    r_ref[...] = vr[:, V_W:2 * V_W]

    nat = _dot(h, w_ref[:, _P_NAT:_P_END])
    qn_ref[...] = nat[:, 0:NAT_W]
    kn_ref[...] = nat[:, NAT_W:2 * NAT_W]
    vn_ref[...] = nat[:, 2 * NAT_W:3 * NAT_W]

    lr = _dot(hp, wlr_ref[...]).astype(BF16)
    z = _dot(lr, wa_ref[...]) + ba_ref[...]
    g = (jnp.minimum(z, 0.0) - jnp.log1p(jnp.exp(-jnp.abs(z)))) * (1.0 / GATE_NORM)
    g_ref[0] = g[:, 0:QK_W]
    g_ref[1] = g[:, QK_W:2 * QK_W]


def _rope_table_block(i):
    lat_tiles = DEC_SEQ // TM
    return jnp.where(i < N_CTX // TM, lat_tiles, (i - N_CTX // TM) % lat_tiles)


def _project(x, mod_l, ng_l, w_big, w_lr, w_a, b_a, cos_t, sin_t):
    tile = lambda w: pl.BlockSpec((TM, w), lambda i: (i, 0))
    full = lambda a: pl.BlockSpec(a.shape, lambda i: (0,) * a.ndim)
    return pl.pallas_call(
        _proj_kernel,
        grid=(N_ALL // TM,),
        in_specs=[
            tile(D_MODEL),
            pl.BlockSpec((1, N_MOD, D_MODEL), lambda i: (_group_of_tile(i), 0, 0)),
            full(ng_l), full(w_big), full(w_lr), full(w_a), full(b_a),
            pl.BlockSpec((TM, QK_W), lambda i: (_rope_table_block(i), 0)),
            pl.BlockSpec((TM, QK_W), lambda i: (_rope_table_block(i), 0)),
        ],
        out_specs=[
            tile(QK_W), tile(QK_W), tile(V_W), tile(V_W),
            pl.BlockSpec((2, TM, QK_W), lambda i: (0, i, 0)),
            tile(NAT_W), tile(NAT_W), tile(NAT_W),
        ],
        out_shape=[
            jax.ShapeDtypeStruct((N_ALL, QK_W), F32), jax.ShapeDtypeStruct((N_ALL, QK_W), F32),
            jax.ShapeDtypeStruct((N_ALL, V_W), F32), jax.ShapeDtypeStruct((N_ALL, V_W), F32),
            jax.ShapeDtypeStruct((2, N_ALL, QK_W), F32),
            jax.ShapeDtypeStruct((N_ALL, NAT_W), F32), jax.ShapeDtypeStruct((N_ALL, NAT_W), F32),
            jax.ShapeDtypeStruct((N_ALL, NAT_W), F32),
        ],
        compiler_params=pltpu.CompilerParams(
            dimension_semantics=("arbitrary",), vmem_limit_bytes=VMEM_LIMIT),
        name="mixer_proj",
    )(x, mod_l, ng_l, w_big, w_lr, w_a, b_a, cos_t, sin_t)


def _gla_tables():
    rows = []
    seq_specs = [(b * (SEQ // GLA_T), SEQ // GLA_T) for b in range(BATCH)]
    seq_specs += [(N_CTX // GLA_T + b * (DEC_SEQ // GLA_T), DEC_SEQ // GLA_T) for b in range(DEC_BATCH)]
    for sid, (blk0, nchunk) in enumerate(seq_specs):
        for direction in (0, 1):
            order = range(nchunk) if direction == 0 else range(nchunk - 1, -1, -1)
            for n, c in enumerate(order):
                rows.append((blk0 + c, direction, int(n == 0), sid))
    return np.asarray(rows, dtype=np.int32).T.copy()


_GLA_TAB = _gla_tables()
_GLA_ITEMS = _GLA_TAB.shape[1]
_GLA_NSEQ = BATCH + DEC_BATCH
_GLA_PAIR_ROWS = GLA_NB * GLA_CB * (GLA_CB + 1) // 2


def _gla_item(direction, q_ref, k_ref, v_ref, g_ref, o_ref, s_scr, cp, sstk, p_scr, w_scr):
    T, CB, NB = GLA_T, GLA_CB, GLA_NB
    fwd = direction == 0
    slab = lambda i: slice(i * NB, (i + 1) * NB)

    cum = None
    for i in (range(CB) if fwd else range(CB - 1, -1, -1)):
        gi = g_ref[0, slab(i), :]
        cum = gi if cum is None else cum + gi
        cp[slab(i), :] = cum
    total = cum
    cum_all = cp[...]
    qt = q_ref[...] * jnp.exp(cum_all)
    kh = k_ref[...] * jnp.exp(jnp.concatenate([total] * CB, axis=0) - cum_all)
    dec_t = jnp.concatenate([jnp.exp(total), jnp.zeros((LANES - NB, QK_W), F32)], axis=0).T

    erow = lax.broadcasted_iota(jnp.int32, (QK_W, QK_W), 0)
    ecol = lax.broadcasted_iota(jnp.int32, (QK_W, QK_W), 1)
    head_sum = jnp.where((erow >> 6) == (ecol >> 6), 1.0, 0.0).astype(BF16)

    key_positions = lambda i: range(i + 1) if fwd else range(i, CB)
    r0 = 0
    for i in range(CB):
        qi = q_ref[slab(i), :]
        ci = cp[slab(i), :]
        for j in key_positions(i):
            e = jnp.exp(ci - cp[slab(j), :])
            p_scr[r0:r0 + NB, :] = (qi * k_ref[slab(j), :] * e).astype(BF16)
            r0 += NB
    w_scr[...] = _dot(p_scr[...], head_sum)
    first_half = lax.broadcasted_iota(jnp.int32, (NB, LANES), 1) < DK_GLA
    r0 = 0
    for i in range(CB):
        acc = None
        for j in key_positions(i):
            spread = []
            for pair in range(H_GLA // 2):
                tile = w_scr[r0:r0 + NB, pair * LANES:(pair + 1) * LANES]
                other = pltpu.roll(tile, DK_GLA, axis=1)
                spread += [jnp.where(first_half, tile, other), jnp.where(first_half, other, tile)]
            term = jnp.concatenate(spread, axis=1) * v_ref[slab(j), :]
            acc = term if acc is None else acc + term
            r0 += NB
        o_ref[0, slab(i), :] = acc

    kht = kh.T.astype(BF16)
    v_bf = v_ref[...].astype(BF16)
    key_blk = lax.broadcasted_iota(jnp.int32, (DK_GLA, T), 1) & (NB - 1)
    row_blk = lax.broadcasted_iota(jnp.int32, (T, LANES), 0) & (NB - 1)
    lane_half = lax.broadcasted_iota(jnp.int32, (T, LANES), 1) >> 6
    order = range(NB) if fwd else range(NB - 1, -1, -1)
    for h in range(H_GLA):
        kh_h = kht[h * DK_GLA:(h + 1) * DK_GLA, :]
        kv = _dot(jnp.concatenate([jnp.where(key_blk == b, kh_h, 0) for b in range(NB)], axis=0),
                  v_bf[:, h * DV_GLA:(h + 1) * DV_GLA])
        s = s_scr[h]
        for b in order:
            sstk[h, b * DK_GLA:(b + 1) * DK_GLA, :] = s.astype(BF16)
            s = dec_t[h * DK_GLA:(h + 1) * DK_GLA, b:b + 1] * s + kv[b * DK_GLA:(b + 1) * DK_GLA, :]
        s_scr[h] = s

        pair_tile = qt[:, (h // 2) * LANES:(h // 2 + 1) * LANES]
        both = jnp.where(lane_half == h % 2, pair_tile, pltpu.roll(pair_tile, DK_GLA, axis=1))
        lhs = jnp.concatenate([jnp.where(row_blk == 2 * j + lane_half, both, 0.0).astype(BF16)
                               for j in range(NB // 2)], axis=1)
        o_ref[0, :, h * DV_GLA:(h + 1) * DV_GLA] += _dot(lhs, sstk[h])


def _gla_kernel(tab_ref, q_ref, k_ref, v_ref, g_ref, s0_ref, o_ref, so_ref, s_scr, cp, sstk, p_scr, w_scr):
    it = pl.program_id(0)

    @pl.when(tab_ref[2, it] == 1)
    def _():
        s_scr[...] = s0_ref[0, 0]

    for direction in (0, 1):
        @pl.when(tab_ref[1, it] == direction)
        def _(direction=direction):
            _gla_item(direction, q_ref, k_ref, v_ref, g_ref, o_ref, s_scr, cp, sstk, p_scr, w_scr)

    so_ref[0, 0] = s_scr[...]


def _gla(q, k, v, g, s0):
    tok = lambda w: pl.BlockSpec((GLA_T, w), lambda it, tab: (tab[0, it], 0))
    state = pl.BlockSpec((1, 1, H_GLA, DK_GLA, DV_GLA), lambda it, tab: (tab[3, it], tab[1, it], 0, 0, 0))
    grid_spec = pltpu.PrefetchScalarGridSpec(
        num_scalar_prefetch=1,
        grid=(_GLA_ITEMS,),
        in_specs=[
            tok(QK_W), tok(QK_W), tok(V_W),
            pl.BlockSpec((1, GLA_T, QK_W), lambda it, tab: (tab[1, it], tab[0, it], 0)),
            state,
        ],
        out_specs=[
            pl.BlockSpec((1, GLA_T, V_W), lambda it, tab: (tab[1, it], tab[0, it], 0)),
            state,
        ],
        scratch_shapes=[
            pltpu.VMEM((H_GLA, DK_GLA, DV_GLA), F32),
            pltpu.VMEM((GLA_T, QK_W), F32),
            pltpu.VMEM((H_GLA, GLA_NB * DK_GLA, DV_GLA), BF16),
            pltpu.VMEM((_GLA_PAIR_ROWS, QK_W), BF16),
            pltpu.VMEM((_GLA_PAIR_ROWS, QK_W), F32),
        ],
    )
    return pl.pallas_call(
        _gla_kernel,
        grid_spec=grid_spec,
        out_shape=[
            jax.ShapeDtypeStruct((2, N_ALL, V_W), F32),
            jax.ShapeDtypeStruct((_GLA_NSEQ, 2, H_GLA, DK_GLA, DV_GLA), F32),
        ],
        compiler_params=pltpu.CompilerParams(
            dimension_semantics=("arbitrary",), vmem_limit_bytes=VMEM_LIMIT),
        name="gla",
    )(jnp.asarray(_GLA_TAB), q, k, v, g, s0)


def _softmax_pv(s_list, v_list):
    m = s_list[0].max(axis=-1, keepdims=True)
    for s in s_list[1:]:
        m = jnp.maximum(m, s.max(axis=-1, keepdims=True))
    num = None
    den = None
    for s, vv in zip(s_list, v_list):
        e = jnp.exp(s - m)
        den = e.sum(axis=-1, keepdims=True) if den is None else den + e.sum(axis=-1, keepdims=True)
        pv = _dot(e.astype(BF16), vv)
        num = pv if num is None else num + pv
    return num / den


def _ctx_attn_kernel(q_ref, k_ref, v_ref, o_ref):
    scale = HD_NAT ** -0.5
    lane = lax.broadcasted_iota(jnp.int32, (SEQ, LANES), 1)
    for t in range(NAT_W // LANES):
        sl = slice(t * LANES, (t + 1) * LANES)
        qt = q_ref[:, sl]
        kt = k_ref[:, sl].astype(BF16)
        vt = v_ref[:, sl].astype(BF16)
        out = jnp.zeros((SEQ, LANES), F32)
        for half in range(LANES // HD_NAT):
            mine = (lane >> 6) == half
            s = _dot_nt(jnp.where(mine, qt, 0.0).astype(BF16), kt) * scale
            out = jnp.where(mine, _softmax_pv([s], [vt]), out)
        o_ref[:, sl] = out


def _ctx_attention(qn, kn, vn):
    spec = pl.BlockSpec((SEQ, NAT_W), lambda b: (b, 0))
    return pl.pallas_call(
        _ctx_attn_kernel,
        grid=(BATCH,),
        in_specs=[spec, spec, spec],
        out_specs=spec,
        out_shape=jax.ShapeDtypeStruct((N_CTX, NAT_W), F32),
        compiler_params=pltpu.CompilerParams(
            dimension_semantics=("arbitrary",), vmem_limit_bytes=VMEM_LIMIT),
        name="ctx_attention",
    )(qn, kn, vn)


_NAT_QROWS = 4
_NAT_GROUPS = GRID_ROWS // _NAT_QROWS
_NAT_KROWS = WIN_H + _NAT_QROWS
_NAT_Q = _NAT_QROWS * GRID_W
_NAT_KEYS = _NAT_KROWS * GRID_W
_NAT_DR = 2 * WIN_H - 1
_NAT_DC = 2 * WIN_W - 1


def _nat_key_row0(r):
    return jnp.clip(r - WIN_H // 2, 0, GRID_ROWS - WIN_H)


def _nat_build_bias(rpb_ref, tz_scr):
    c = lax.broadcasted_iota(jnp.int32, (GRID_W, LANES), 0)
    lane = lax.broadcasted_iota(jnp.int32, (GRID_W, LANES), 1)
    kc = lane & (GRID_W - 1)
    second = (lane >> 6) == 1
    win_start = jnp.clip(c - WIN_W // 2, 0, GRID_W - WIN_W)
    valid = (kc >= win_start) & (kc < win_start + WIN_W)

    def one_row(n, carry):
        dr = n >> 3
        h = n & (H_NAT - 1)
        src = jnp.broadcast_to(rpb_ref[pl.ds(h * _NAT_DR + dr, 1), :], (GRID_W, LANES))
        rolled = pltpu.roll(src, LANES - (WIN_W - 1), axis=1, stride=1, stride_axis=0)
        tz_scr[dr, h] = jnp.where(valid, rolled, NEG_INF)
        return carry

    lax.fori_loop(0, _NAT_DR * H_NAT, one_row, 0)

    def pair_rows(n, carry):
        dr = n >> 3
        h = n & (H_NAT - 1)
        tz_scr[dr, h] = jnp.where(second, tz_scr[dr + 1, h], tz_scr[dr, h])
        return carry

    lax.fori_loop(0, (_NAT_DR - 1) * H_NAT, pair_rows, 0)


def _nat_kernel(rpb_ref, q_ref, k_ref, v_ref, ck_ref, cv_ref, o_ref, tz_scr, bias_scr):
    grp = pl.program_id(1)

    @pl.when((pl.program_id(0) == 0) & (grp == 0))
    def _():
        _nat_build_bias(rpb_ref, tz_scr)

    scale = HD_NAT ** -0.5
    krow0 = jnp.clip(_NAT_QROWS * grp - WIN_H // 2, 0, GRID_ROWS - _NAT_KROWS)
    k0 = pl.multiple_of(krow0 * GRID_W, GRID_W)
    lane_q = lax.broadcasted_iota(jnp.int32, (_NAT_Q, LANES), 1)
    lane_b = lax.broadcasted_iota(jnp.int32, (GRID_W, LANES), 1)
    neg = jnp.full((GRID_W, LANES), NEG_INF, F32)
    for t in range(NAT_W // LANES):
        sl = slice(t * LANES, (t + 1) * LANES)
        qt = q_ref[:, sl]
        kw = k_ref[pl.ds(k0, _NAT_KEYS), sl].astype(BF16)
        vw = v_ref[pl.ds(k0, _NAT_KEYS), sl].astype(BF16)
        ck = ck_ref[0, 0, :, sl].astype(BF16)
        cv = cv_ref[0, 0, :, sl].astype(BF16)
        out = jnp.zeros((_NAT_Q, LANES), F32)
        for half in range(LANES // HD_NAT):
            h = 2 * t + half
            for qr in range(_NAT_QROWS):
                r = _NAT_QROWS * grp + qr
                lo = _nat_key_row0(r)
                for kp in range(_NAT_KROWS // 2):
                    kr = krow0 + 2 * kp
                    tile = tz_scr[jnp.clip(kr - r + (WIN_H - 1), 0, _NAT_DR - 1), h]
                    ok_a = ((kr >= lo) & (kr < lo + WIN_H)).astype(jnp.int32)
                    ok_b = ((kr + 1 >= lo) & (kr + 1 < lo + WIN_H)).astype(jnp.int32)
                    ok = jnp.where(lane_b < GRID_W, ok_a, ok_b) == 1
                    bias_scr[qr * GRID_W:(qr + 1) * GRID_W, kp * LANES:(kp + 1) * LANES] = jnp.where(ok, tile, neg)
            mine = (lane_q >> 6) == half
            qm = jnp.where(mine, qt, 0.0).astype(BF16)
            s_win = _dot_nt(qm, kw) * scale + bias_scr[...]
            s_ctx = _dot_nt(qm, ck) * scale
            out = jnp.where(mine, _softmax_pv([s_win, s_ctx], [vw, cv]), out)
        o_ref[:, sl] = out


def _nat_attention(qn, kn, vn, cache_k, cache_v, rpb_l, layer):
    lat0 = N_CTX // _NAT_Q
    half = jnp.pad(rpb_l.reshape(H_NAT * _NAT_DR, _NAT_DC), ((0, 0), (0, GRID_W - _NAT_DC)))
    rpb_rows = jnp.concatenate([half, half], axis=1)
    return pl.pallas_call(
        _nat_kernel,
        grid=(DEC_BATCH, _NAT_GROUPS),
        in_specs=[
            pl.BlockSpec((H_NAT * _NAT_DR, LANES), lambda b, g: (0, 0)),
            pl.BlockSpec((_NAT_Q, NAT_W), lambda b, g: (lat0 + b * _NAT_GROUPS + g, 0)),
            pl.BlockSpec((DEC_SEQ, NAT_W), lambda b, g: (N_CTX // DEC_SEQ + b, 0)),
            pl.BlockSpec((DEC_SEQ, NAT_W), lambda b, g: (N_CTX // DEC_SEQ + b, 0)),
            pl.BlockSpec((1, 1, PAST_LEN, NAT_W), lambda b, g: (b, layer, 0, 0)),
            pl.BlockSpec((1, 1, PAST_LEN, NAT_W), lambda b, g: (b, layer, 0, 0)),
        ],
        out_specs=pl.BlockSpec((_NAT_Q, NAT_W), lambda b, g: (b * _NAT_GROUPS + g, 0)),
        out_shape=jax.ShapeDtypeStruct((N_LAT, NAT_W), F32),
        scratch_shapes=[
            pltpu.VMEM((_NAT_DR, H_NAT, GRID_W, LANES), F32),
            pltpu.VMEM((_NAT_Q, _NAT_KEYS), F32),
        ],
        compiler_params=pltpu.CompilerParams(
            dimension_semantics=("arbitrary", "arbitrary"), vmem_limit_bytes=VMEM_LIMIT),
        name="nat_attention",
    )(rpb_rows, qn, kn, vn, cache_k, cache_v)


def _out_kernel(x_ref, o_ref, r_ref, on_ctx_ref, on_lat_ref, mod_ref, ng_ref, gng_ref, w_ref, y_ref):
    o_nat = _read_split(pl.program_id(0), on_ctx_ref, on_lat_ref)
    og = o_ref[0] + o_ref[1]
    parts = []
    for h in range(H_GLA):
        parts.append(_rms(og[:, h * DV_GLA:(h + 1) * DV_GLA], gng_ref[...]))
    merged = (jnp.concatenate(parts, axis=1) * _silu(r_ref[...])).astype(BF16)
    merged = _permute_chunks(_block_transpose_perm(), merged)
    y = _dot(merged, w_ref[0:V_W, :]) + _dot(o_nat.astype(BF16), w_ref[V_W:, :])
    y_ref[...] = x_ref[...] + mod_ref[0, 5:6, :] * _rms(y, ng_ref[3:4, :])


def _mixer_out(x, o_gla, r, o_ctx, o_lat, mod_l, ng_l, gng_l, w_out_bf):
    tile = lambda w: pl.BlockSpec((TM, w), lambda i: (i, 0))
    full = lambda a: pl.BlockSpec(a.shape, lambda i: (0,) * a.ndim)
    return pl.pallas_call(
        _out_kernel,
        grid=(N_ALL // TM,),
        in_specs=[
            tile(D_MODEL),
            pl.BlockSpec((2, TM, V_W), lambda i: (0, i, 0)),
            tile(V_W),
            pl.BlockSpec((TM, NAT_W), lambda i: (_ctx_tile(i), 0)),
            pl.BlockSpec((TM, NAT_W), lambda i: (_lat_tile(i), 0)),
            pl.BlockSpec((1, N_MOD, D_MODEL), lambda i: (_group_of_tile(i), 0, 0)),
            full(ng_l), full(gng_l), full(w_out_bf),
        ],
        out_specs=tile(D_MODEL),
        out_shape=jax.ShapeDtypeStruct((N_ALL, D_MODEL), F32),
        compiler_params=pltpu.CompilerParams(
            dimension_semantics=("arbitrary",), vmem_limit_bytes=VMEM_LIMIT),
        name="mixer_out",
    )(x, o_gla, r, o_ctx, o_lat, mod_l, ng_l, gng_l, w_out_bf)


def _rope_tables():
    quarter = DK_GLA // 4
    freqs = ROPE_BASE ** (-jnp.arange(quarter, dtype=F32) / quarter)
    t = jnp.arange(DEC_SEQ)
    ang_r = (t // GRID_W).astype(F32)[:, None] * freqs[None, :]
    ang_c = (t % GRID_W).astype(F32)[:, None] * freqs[None, :]
    cos_h = jnp.concatenate([jnp.cos(ang_r), jnp.cos(ang_r), jnp.cos(ang_c), jnp.cos(ang_c)], axis=1)
    sin_h = jnp.concatenate([-jnp.sin(ang_r), jnp.sin(ang_r), -jnp.sin(ang_c), jnp.sin(ang_c)], axis=1)
    pos_major = lambda a: a.reshape(-1, GLA_NB, GLA_CB, QK_W).transpose(0, 2, 1, 3).reshape(a.shape)
    cos_t = jnp.concatenate([pos_major(jnp.tile(cos_h, (1, H_GLA))), jnp.ones((TM, QK_W), F32)], axis=0)
    sin_t = jnp.concatenate([pos_major(jnp.tile(sin_h, (1, H_GLA))), jnp.zeros((TM, QK_W), F32)], axis=0)
    return cos_t, sin_t


def _swap_perm():
    quarter = DK_GLA // 4
    idx = np.arange(QK_W)
    blk = (idx // quarter) % 4
    return np.where(blk % 2 == 0, idx + quarter, idx - quarter)


def _pack_w_in(w_in_l, gla_wa2_l, gla_ba_l):
    sizes = [QK_W, QK_W, V_W, V_W, GLA_RANK, GLA_RANK, NAT_W, NAT_W, NAT_W]
    cuts = np.cumsum([0] + sizes)
    wq, wk, wv, wr, wlf, wlb, wqn, wkn, wvn = [w_in_l[:, cuts[i]:cuts[i + 1]] for i in range(9)]
    perm = _swap_perm()
    w_big = jnp.concatenate([wq, wk, wq[:, perm], wk[:, perm], wv, wr, wqn, wkn, wvn], axis=1).astype(BF16)
    w_lr = jnp.zeros((D_MODEL, LANES), F32).at[:, 0:GLA_RANK].set(wlf).at[:, GLA_RANK:2 * GLA_RANK].set(wlb)
    w_a = jnp.zeros((LANES, 2 * QK_W), F32)
    w_a = w_a.at[0:GLA_RANK, 0:QK_W].set(gla_wa2_l[0]).at[GLA_RANK:2 * GLA_RANK, QK_W:].set(gla_wa2_l[1])
    b_a = jnp.concatenate([gla_ba_l[0], gla_ba_l[1]])[None, :]
    return w_big, w_lr.astype(BF16), w_a.astype(BF16), b_a


def kernel(x_prompt, x_sample, cache_k, cache_v, state_gla, c, c_ctx, w_mod, b_mod, norm_g, ffn_w_in, ffn_w_out,
           w_in, gla_wa2, gla_ba, gla_norm_g, nat_rpb, w_out):
    cvecs = jnp.zeros((SUBLANES, D_MODEL), F32).at[0].set(c_ctx).at[1:1 + DEC_BATCH].set(c)
    mod = _modulation(cvecs, w_mod, b_mod)[:, :N_GROUPS].reshape(DEPTH, N_GROUPS, N_MOD, D_MODEL)

    ffn_w_in_bf = ffn_w_in.astype(BF16)
    ffn_w_out_bf = ffn_w_out.astype(BF16)
    w_out_bf = w_out.astype(BF16)
    cos_t, sin_t = _rope_tables()
    ck = cache_k.reshape(DEC_BATCH, DEPTH, PAST_LEN, NAT_W)
    cv = cache_v.reshape(DEC_BATCH, DEPTH, PAST_LEN, NAT_W)

    x = (x_prompt.reshape(N_CTX, D_MODEL), x_sample.reshape(N_LAT, D_MODEL))
    k_list, v_list, s_list = [], [], []
    for l in range(DEPTH):
        mod_l, ng_l = mod[l], norm_g[l]
        x = _ffn(x, mod_l, ng_l, ffn_w_in_bf, ffn_w_out_bf, layer=l, which=0)

        w_big, w_lr, w_a, b_a = _pack_w_in(w_in[l], gla_wa2[l], gla_ba[l])
        q, k, v, r, g, qn, kn, vn = _project(x, mod_l, ng_l, w_big, w_lr, w_a, b_a, cos_t, sin_t)

        s0 = jnp.concatenate([jnp.zeros((BATCH, 2, H_GLA, DK_GLA, DV_GLA), F32), state_gla[:, l]], axis=0)
        o_gla, s_fin = _gla(q, k, v, g, s0)
        o_ctx = _ctx_attention(qn, kn, vn)
        o_lat = _nat_attention(qn, kn, vn, ck, cv, nat_rpb[l], l)

        x = _mixer_out(x, o_gla, r, o_ctx, o_lat, mod_l, ng_l, gla_norm_g[l][None, :], w_out_bf[l])
        x = _ffn(x, mod_l, ng_l, ffn_w_in_bf, ffn_w_out_bf, layer=l, which=1, split_out=(l == DEPTH - 1))

        k_list.append(kn[:N_CTX].reshape(BATCH, SEQ, H_NAT, HD_NAT))
        v_list.append(vn[:N_CTX].reshape(BATCH, SEQ, H_NAT, HD_NAT))
        s_list.append(s_fin[:BATCH])

    y_prompt = x[0].reshape(BATCH, SEQ, D_MODEL)
    y_sample = x[1].reshape(DEC_BATCH, DEC_SEQ, D_MODEL)
    return (y_prompt, y_sample, jnp.stack(k_list, axis=1), jnp.stack(v_list, axis=1), jnp.stack(s_list, axis=1))
```

```python
import functools

import numpy as np
import jax
import jax.numpy as jnp
from jax import lax
from jax.experimental import pallas as pl
from jax.experimental.pallas import tpu as pltpu

D_MODEL = 1024
BATCH = 16
SEQ = 256
DEPTH = 2
DEC_BATCH = 2
DEC_SEQ = 1024
PAST_LEN = 512
GRID_W = 64
H_GLA = 4
DK_GLA = 64
DV_GLA = 128
GLA_RANK = 16
GATE_NORM = 16.0
H_NAT = 8
HD_NAT = 64
WIN_H = 8
WIN_W = 16
D_FF = 2816
N_MOD = 9
ROPE_BASE = 10000.0
EPS = 1e-6
NEG_INF = -1e30

F32 = jnp.float32
BF16 = jnp.bfloat16

N_CTX = BATCH * SEQ
N_LAT = DEC_BATCH * DEC_SEQ
N_ALL = N_CTX + N_LAT
N_GROUPS = 1 + DEC_BATCH
QK_W = H_GLA * DK_GLA
V_W = H_GLA * DV_GLA
NAT_W = H_NAT * HD_NAT
GRID_ROWS = DEC_SEQ // GRID_W

LANES = 128
SUBLANES = 8

TM = 512
TF = 1408
MOD_TN = 1152
GLA_T = 256
GLA_CB = 16
GLA_NB = GLA_T // GLA_CB
VMEM_LIMIT = 56 * 1024 * 1024

assert N_CTX % TM == 0 and DEC_SEQ % TM == 0 and D_FF % TF == 0 and TF % LANES == 0
assert SEQ == GLA_T and DEC_SEQ % GLA_T == 0


def _group_of_tile(i):
    return jnp.where(i < N_CTX // TM, 0, 1 + (i - N_CTX // TM) // (DEC_SEQ // TM))


def _dot(a, b):
    return jnp.dot(a, b, preferred_element_type=F32)


def _dot_nt(a, b):
    return lax.dot_general(a, b, (((1,), (1,)), ((), ())), preferred_element_type=F32)


def _dot_tn(a, b):
    return lax.dot_general(a, b, (((0,), (0,)), ((), ())), preferred_element_type=F32)


def _rms(x, g):
    ms = jnp.mean(x * x, axis=-1, keepdims=True)
    return x * lax.rsqrt(ms + EPS) * g


def _silu(x):
    return x * jax.nn.sigmoid(x)


def _mod_kernel(c_ref, w_ref, b_ref, o_ref):
    s = _silu(c_ref[...]).astype(BF16)
    o_ref[0] = _dot(s, w_ref[0].astype(BF16)) + b_ref[0]


def _modulation(cvecs, w_mod, b_mod):
    n_out = N_MOD * D_MODEL
    return pl.pallas_call(
        _mod_kernel,
        grid=(DEPTH, n_out // MOD_TN),
        in_specs=[
            pl.BlockSpec((SUBLANES, D_MODEL), lambda l, j: (0, 0)),
            pl.BlockSpec((1, D_MODEL, MOD_TN), lambda l, j: (l, 0, j)),
            pl.BlockSpec((1, 1, MOD_TN), lambda l, j: (l, 0, j)),
        ],
        out_specs=pl.BlockSpec((1, SUBLANES, MOD_TN), lambda l, j: (l, 0, j)),
        out_shape=jax.ShapeDtypeStruct((DEPTH, SUBLANES, n_out), F32),
        compiler_params=pltpu.CompilerParams(
            dimension_semantics=("arbitrary", "arbitrary"), vmem_limit_bytes=VMEM_LIMIT),
        name="modulation",
    )(cvecs, w_mod, b_mod.reshape(DEPTH, 1, n_out))


_CTX_TILES = N_CTX // TM
_LAT_TILES = N_LAT // TM


def _ctx_tile(i):
    return jnp.minimum(i, _CTX_TILES - 1)


def _lat_tile(i):
    return jnp.maximum(i - _CTX_TILES, 0)


def _read_split(i, ctx_ref, lat_ref):
    return jnp.where(i < _CTX_TILES, ctx_ref[...], lat_ref[...])


def _ffn_kernel(*refs, m0, n0, split_in, split_out):
    n_x = 2 if split_in else 1
    x_refs, (mod_ref, ng_ref, wg_ref, wu_ref, wo_ref) = refs[:n_x], refs[n_x:n_x + 5]
    n_o = 2 if split_out else 1
    o_refs = refs[n_x + 5:n_x + 5 + n_o]
    h_scr, acc_scr, x_scr = refs[n_x + 5 + n_o:]
    i = pl.program_id(0)
    j = pl.program_id(1)

    @pl.when(j == 0)
    def _():
        x = _read_split(i, *x_refs) if split_in else x_refs[0][...]
        x_scr[...] = x
        gain = ng_ref[n0:n0 + 1, :] * (1.0 + mod_ref[0, m0 + 1:m0 + 2, :])
        h_scr[...] = (_rms(x, gain) + mod_ref[0, m0:m0 + 1, :]).astype(BF16)
        acc_scr[...] = jnp.zeros_like(acc_scr)

    h = h_scr[...]
    gate = _dot(h, wg_ref[0, 0])
    up = _dot(h, wu_ref[0, 0])
    act = (_silu(gate) * up).astype(BF16)
    acc_scr[...] += _dot(act, wo_ref[0, 0])

    @pl.when(j == pl.num_programs(1) - 1)
    def _():
        gain = 0.5 * mod_ref[0, m0 + 2:m0 + 3, :] * ng_ref[n0 + 1:n0 + 2, :]
        out = x_scr[...] + _rms(acc_scr[...], gain)
        if split_out:
            @pl.when(i < _CTX_TILES)
            def _():
                o_refs[0][...] = out

            @pl.when(i >= _CTX_TILES)
            def _():
                o_refs[1][...] = out
        else:
            o_refs[0][...] = out


def _ffn(x, mod_l, ng_l, w_in_bf, w_out_bf, *, layer, which, split_out=False):
    nf = D_FF // TF
    m0, n0 = (0, 0) if which == 0 else (6, 4)
    split_in = isinstance(x, tuple)
    tile = pl.BlockSpec((TM, D_MODEL), lambda i, j: (i, 0))
    ctx_tile = pl.BlockSpec((TM, D_MODEL), lambda i, j: (_ctx_tile(i), 0))
    lat_tile = pl.BlockSpec((TM, D_MODEL), lambda i, j: (_lat_tile(i), 0))
    if split_out:
        out_specs = [ctx_tile, lat_tile]
        out_shape = [jax.ShapeDtypeStruct((N_CTX, D_MODEL), F32), jax.ShapeDtypeStruct((N_LAT, D_MODEL), F32)]
    else:
        out_specs = tile
        out_shape = jax.ShapeDtypeStruct((N_ALL, D_MODEL), F32)
    return pl.pallas_call(
        functools.partial(_ffn_kernel, m0=m0, n0=n0, split_in=split_in, split_out=split_out),
        grid=(N_ALL // TM, nf),
        in_specs=([ctx_tile, lat_tile] if split_in else [tile]) + [
            pl.BlockSpec((1, N_MOD, D_MODEL), lambda i, j: (_group_of_tile(i), 0, 0)),
            pl.BlockSpec((6, D_MODEL), lambda i, j: (0, 0)),
            pl.BlockSpec((1, 1, D_MODEL, TF), lambda i, j: (layer, which, 0, j)),
            pl.BlockSpec((1, 1, D_MODEL, TF), lambda i, j: (layer, which, 0, j + nf)),
            pl.BlockSpec((1, 1, TF, D_MODEL), lambda i, j: (layer, which, j, 0)),
        ],
        out_specs=out_specs,
        out_shape=out_shape,
        scratch_shapes=[pltpu.VMEM((TM, D_MODEL), BF16), pltpu.VMEM((TM, D_MODEL), F32),
                        pltpu.VMEM((TM, D_MODEL), F32)],
        compiler_params=pltpu.CompilerParams(
            dimension_semantics=("arbitrary", "arbitrary"), vmem_limit_bytes=VMEM_LIMIT),
        name="ffn",
    )(*(x if split_in else (x,)), mod_l, ng_l, w_in_bf, w_in_bf, w_out_bf)


def _block_transpose_perm():
    r = lax.broadcasted_iota(jnp.int32, (GLA_T, GLA_T), 0)
    c = lax.broadcasted_iota(jnp.int32, (GLA_T, GLA_T), 1)
    return jnp.where(c == (r & (GLA_CB - 1)) * GLA_NB + (r >> 4), 1.0, 0.0).astype(BF16)


def _permute_chunks(perm, a):
    return jnp.concatenate([_dot(perm, a[c * GLA_T:(c + 1) * GLA_T, :]).astype(BF16)
                            for c in range(a.shape[0] // GLA_T)], axis=0)


def _proj_kernel(x_ref, mod_ref, ng_ref, wgla_ref, wsw_ref, wlr_ref, wnat_ref, wa_ref, ba_ref, cos_ref, sin_ref,
                 q_ref, k_ref, v_ref, r_ref, g_ref, qn_ref, kn_ctx_ref, vn_ctx_ref, kn_lat_ref, vn_lat_ref):
    i = pl.program_id(0)
    h = _rms(x_ref[...], ng_ref[2:3, :])
    h = (h * (1.0 + mod_ref[0, 4:5, :]) + mod_ref[0, 3:4, :]).astype(BF16)
    hp = _permute_chunks(_block_transpose_perm(), h)

    qk = _dot(hp, wgla_ref[0, :, 0:2 * QK_W])
    sw = _dot(hp, wsw_ref[0])
    cos = cos_ref[...]
    sin = sin_ref[...]
    q_ref[...] = (qk[:, 0:QK_W] * cos + sw[:, 0:QK_W] * sin) * (DK_GLA ** -0.5)
    k_ref[...] = qk[:, QK_W:2 * QK_W] * cos + sw[:, QK_W:2 * QK_W] * sin

    vr = _dot(hp, wgla_ref[0, :, 2 * QK_W:])
    v_ref[...] = vr[:, 0:V_W]
    r_ref[...] = vr[:, V_W:2 * V_W]

    nat = _dot(h, wnat_ref[0])
    qn_ref[...] = nat[:, 0:NAT_W]

    @pl.when(i < _CTX_TILES)
    def _():
        kn_ctx_ref[...] = nat[:, NAT_W:2 * NAT_W]
        vn_ctx_ref[...] = nat[:, 2 * NAT_W:3 * NAT_W]

    @pl.when(i >= _CTX_TILES)
    def _():
        kn_lat_ref[...] = nat[:, NAT_W:2 * NAT_W]
        vn_lat_ref[...] = nat[:, 2 * NAT_W:3 * NAT_W]

    lr = _dot(hp, wlr_ref[0])
    lane = lax.broadcasted_iota(jnp.int32, lr.shape, 1)
    lr = jnp.where(lane < 2 * GLA_RANK, lr, 0.0).astype(BF16)
    z = _dot(lr, wa_ref[...]) + ba_ref[...]
    g = (jnp.minimum(z, 0.0) - jnp.log1p(jnp.exp(-jnp.abs(z)))) * (1.0 / GATE_NORM)
    g_ref[0] = g[:, 0:QK_W]
    g_ref[1] = g[:, QK_W:2 * QK_W]


def _rope_table_block(i):
    lat_tiles = DEC_SEQ // TM
    return jnp.where(i < N_CTX // TM, lat_tiles, (i - N_CTX // TM) % lat_tiles)


_LR_COL0 = 2 * QK_W + 2 * V_W
_NAT_COL0 = _LR_COL0 + 2 * GLA_RANK
assert _LR_COL0 % LANES == 0


def _project(x, mod_l, ng_l, w_in_bf, w_sw, w_nat, w_a, b_a, cos_t, sin_t, *, layer):
    tile = lambda w: pl.BlockSpec((TM, w), lambda i: (i, 0))
    ctx_tile = pl.BlockSpec((TM, NAT_W), lambda i: (_ctx_tile(i), 0))
    lat_tile = pl.BlockSpec((TM, NAT_W), lambda i: (_lat_tile(i), 0))
    full = lambda a: pl.BlockSpec(a.shape, lambda i: (0,) * a.ndim)
    per_layer = lambda a: pl.BlockSpec((1,) + a.shape[1:], lambda i: (layer, 0, 0))
    return pl.pallas_call(
        _proj_kernel,
        grid=(N_ALL // TM,),
        in_specs=[
            tile(D_MODEL),
            pl.BlockSpec((1, N_MOD, D_MODEL), lambda i: (_group_of_tile(i), 0, 0)),
            full(ng_l),
            pl.BlockSpec((1, D_MODEL, _LR_COL0), lambda i: (layer, 0, 0)),
            per_layer(w_sw),
            pl.BlockSpec((1, D_MODEL, LANES), lambda i: (layer, 0, _LR_COL0 // LANES)),
            per_layer(w_nat),
            full(w_a), full(b_a),
            pl.BlockSpec((TM, QK_W), lambda i: (_rope_table_block(i), 0)),
            pl.BlockSpec((TM, QK_W), lambda i: (_rope_table_block(i), 0)),
        ],
        out_specs=[
            tile(QK_W), tile(QK_W), tile(V_W), tile(V_W),
            pl.BlockSpec((2, TM, QK_W), lambda i: (0, i, 0)),
            tile(NAT_W), ctx_tile, ctx_tile, lat_tile, lat_tile,
        ],
        out_shape=[
            jax.ShapeDtypeStruct((N_ALL, QK_W), F32), jax.ShapeDtypeStruct((N_ALL, QK_W), F32),
            jax.ShapeDtypeStruct((N_ALL, V_W), F32), jax.ShapeDtypeStruct((N_ALL, V_W), F32),
            jax.ShapeDtypeStruct((2, N_ALL, QK_W), F32),
            jax.ShapeDtypeStruct((N_ALL, NAT_W), F32),
            jax.ShapeDtypeStruct((N_CTX, NAT_W), F32), jax.ShapeDtypeStruct((N_CTX, NAT_W), F32),
            jax.ShapeDtypeStruct((N_LAT, NAT_W), F32), jax.ShapeDtypeStruct((N_LAT, NAT_W), F32),
        ],
        compiler_params=pltpu.CompilerParams(
            dimension_semantics=("arbitrary",), vmem_limit_bytes=VMEM_LIMIT),
        name="mixer_proj",
    )(x, mod_l, ng_l, w_in_bf, w_sw, w_in_bf, w_nat, w_a, b_a, cos_t, sin_t)


def _gla_tables():
    rows = []
    seq_specs = [(b * (SEQ // GLA_T), SEQ // GLA_T) for b in range(BATCH)]
    seq_specs += [(N_CTX // GLA_T + b * (DEC_SEQ // GLA_T), DEC_SEQ // GLA_T) for b in range(DEC_BATCH)]
    for sid, (blk0, nchunk) in enumerate(seq_specs):
        for direction in (0, 1):
            order = range(nchunk) if direction == 0 else range(nchunk - 1, -1, -1)
            for n, c in enumerate(order):
                rows.append((blk0 + c, direction, int(n == 0), sid))
    return np.asarray(rows, dtype=np.int32).T.copy()


_GLA_TAB = _gla_tables()
_GLA_ITEMS = _GLA_TAB.shape[1]
_GLA_NSEQ = BATCH + DEC_BATCH
_GLA_PAIR_ROWS = GLA_NB * GLA_CB * (GLA_CB + 1) // 2


def _gla_item(direction, q_ref, k_ref, v_ref, g_ref, o_ref, s_scr, cp, sstk, p_scr, w_scr):
    T, CB, NB = GLA_T, GLA_CB, GLA_NB
    fwd = direction == 0
    slab = lambda i: slice(i * NB, (i + 1) * NB)

    cum = None
    for i in (range(CB) if fwd else range(CB - 1, -1, -1)):
        gi = g_ref[0, slab(i), :]
        cum = gi if cum is None else cum + gi
        cp[slab(i), :] = cum
    total = cum
    cum_all = cp[...]
    qt = q_ref[...] * jnp.exp(cum_all)
    kh = k_ref[...] * jnp.exp(jnp.concatenate([total] * CB, axis=0) - cum_all)
    dec_t = jnp.concatenate([jnp.exp(total), jnp.zeros((LANES - NB, QK_W), F32)], axis=0).T

    erow = lax.broadcasted_iota(jnp.int32, (QK_W, QK_W), 0)
    ecol = lax.broadcasted_iota(jnp.int32, (QK_W, QK_W), 1)
    head_sum = jnp.where((erow >> 6) == (ecol >> 6), 1.0, 0.0).astype(BF16)

    key_positions = lambda i: range(i + 1) if fwd else range(i, CB)
    r0 = 0
    for i in range(CB):
        qi = q_ref[slab(i), :]
        ci = cp[slab(i), :]
        for j in key_positions(i):
            e = jnp.exp(ci - cp[slab(j), :])
            p_scr[r0:r0 + NB, :] = (qi * k_ref[slab(j), :] * e).astype(BF16)
            r0 += NB
    w_scr[...] = _dot(p_scr[...], head_sum)
    first_half = lax.broadcasted_iota(jnp.int32, (NB, LANES), 1) < DK_GLA
    r0 = 0
    for i in range(CB):
        acc = None
        for j in key_positions(i):
            spread = []
            for pair in range(H_GLA // 2):
                tile = w_scr[r0:r0 + NB, pair * LANES:(pair + 1) * LANES]
                other = pltpu.roll(tile, DK_GLA, axis=1)
                spread += [jnp.where(first_half, tile, other), jnp.where(first_half, other, tile)]
            term = jnp.concatenate(spread, axis=1) * v_ref[slab(j), :]
            acc = term if acc is None else acc + term
            r0 += NB
        o_ref[0, slab(i), :] = acc

    kht = kh.T.astype(BF16)
    v_bf = v_ref[...].astype(BF16)
    key_blk = lax.broadcasted_iota(jnp.int32, (DK_GLA, T), 1) & (NB - 1)
    row_blk = lax.broadcasted_iota(jnp.int32, (T, LANES), 0) & (NB - 1)
    lane_half = lax.broadcasted_iota(jnp.int32, (T, LANES), 1) >> 6
    order = range(NB) if fwd else range(NB - 1, -1, -1)
    for h in range(H_GLA):
        kh_h = kht[h * DK_GLA:(h + 1) * DK_GLA, :]
        kv = _dot(jnp.concatenate([jnp.where(key_blk == b, kh_h, 0) for b in range(NB)], axis=0),
                  v_bf[:, h * DV_GLA:(h + 1) * DV_GLA])
        s = s_scr[h]
        for b in order:
            sstk[h, b * DK_GLA:(b + 1) * DK_GLA, :] = s.astype(BF16)
            s = dec_t[h * DK_GLA:(h + 1) * DK_GLA, b:b + 1] * s + kv[b * DK_GLA:(b + 1) * DK_GLA, :]
        s_scr[h] = s

        pair_tile = qt[:, (h // 2) * LANES:(h // 2 + 1) * LANES]
        both = jnp.where(lane_half == h % 2, pair_tile, pltpu.roll(pair_tile, DK_GLA, axis=1))
        lhs = jnp.concatenate([jnp.where(row_blk == 2 * j + lane_half, both, 0.0).astype(BF16)
                               for j in range(NB // 2)], axis=1)
        o_ref[0, :, h * DV_GLA:(h + 1) * DV_GLA] += _dot(lhs, sstk[h])


def _gla_kernel(tab_ref, q_ref, k_ref, v_ref, g_ref, s0_ref, o_ref, so_ref, s_scr, cp, sstk, p_scr, w_scr):
    it = pl.program_id(0)

    @pl.when(tab_ref[2, it] == 1)
    def _():
        s_scr[...] = s0_ref[0, 0]

    for direction in (0, 1):
        @pl.when(tab_ref[1, it] == direction)
        def _(direction=direction):
            _gla_item(direction, q_ref, k_ref, v_ref, g_ref, o_ref, s_scr, cp, sstk, p_scr, w_scr)

    so_ref[0, 0] = s_scr[...]


def _gla(q, k, v, g, s0):
    tok = lambda w: pl.BlockSpec((GLA_T, w), lambda it, tab: (tab[0, it], 0))
    state = pl.BlockSpec((1, 1, H_GLA, DK_GLA, DV_GLA), lambda it, tab: (tab[3, it], tab[1, it], 0, 0, 0))
    grid_spec = pltpu.PrefetchScalarGridSpec(
        num_scalar_prefetch=1,
        grid=(_GLA_ITEMS,),
        in_specs=[
            tok(QK_W), tok(QK_W), tok(V_W),
            pl.BlockSpec((1, GLA_T, QK_W), lambda it, tab: (tab[1, it], tab[0, it], 0)),
            state,
        ],
        out_specs=[
            pl.BlockSpec((1, GLA_T, V_W), lambda it, tab: (tab[1, it], tab[0, it], 0)),
            state,
        ],
        scratch_shapes=[
            pltpu.VMEM((H_GLA, DK_GLA, DV_GLA), F32),
            pltpu.VMEM((GLA_T, QK_W), F32),
            pltpu.VMEM((H_GLA, GLA_NB * DK_GLA, DV_GLA), BF16),
            pltpu.VMEM((_GLA_PAIR_ROWS, QK_W), BF16),
            pltpu.VMEM((_GLA_PAIR_ROWS, QK_W), F32),
        ],
    )
    return pl.pallas_call(
        _gla_kernel,
        grid_spec=grid_spec,
        out_shape=[
            jax.ShapeDtypeStruct((2, N_ALL, V_W), F32),
            jax.ShapeDtypeStruct((_GLA_NSEQ, 2, H_GLA, DK_GLA, DV_GLA), F32),
        ],
        compiler_params=pltpu.CompilerParams(
            dimension_semantics=("arbitrary",), vmem_limit_bytes=VMEM_LIMIT),
        name="gla",
    )(jnp.asarray(_GLA_TAB), q, k, v, g, s0)


def _softmax_pv(s_list, v_list):
    m = s_list[0].max(axis=-1, keepdims=True)
    for s in s_list[1:]:
        m = jnp.maximum(m, s.max(axis=-1, keepdims=True))
    num = None
    den = None
    for s, vv in zip(s_list, v_list):
        e = jnp.exp(s - m)
        den = e.sum(axis=-1, keepdims=True) if den is None else den + e.sum(axis=-1, keepdims=True)
        pv = _dot(e.astype(BF16), vv)
        num = pv if num is None else num + pv
    return num / den


def _ctx_attn_kernel(q_ref, k_ref, v_ref, o_ref):
    scale = HD_NAT ** -0.5
    lane = lax.broadcasted_iota(jnp.int32, (SEQ, LANES), 1)
    for t in range(NAT_W // LANES):
        sl = slice(t * LANES, (t + 1) * LANES)
        qt = q_ref[:, sl]
        kt = k_ref[:, sl].astype(BF16)
        vt = v_ref[:, sl].astype(BF16)
        out = jnp.zeros((SEQ, LANES), F32)
        for half in range(LANES // HD_NAT):
            mine = (lane >> 6) == half
            s = _dot_nt(jnp.where(mine, qt, 0.0).astype(BF16), kt) * scale
            out = jnp.where(mine, _softmax_pv([s], [vt]), out)
        o_ref[:, sl] = out


def _ctx_attention(qn, kn, vn):
    spec = pl.BlockSpec((SEQ, NAT_W), lambda b: (b, 0))
    return pl.pallas_call(
        _ctx_attn_kernel,
        grid=(BATCH,),
        in_specs=[spec, spec, spec],
        out_specs=spec,
        out_shape=jax.ShapeDtypeStruct((N_CTX, NAT_W), F32),
        compiler_params=pltpu.CompilerParams(
            dimension_semantics=("arbitrary",), vmem_limit_bytes=VMEM_LIMIT),
        name="ctx_attention",
    )(qn, kn, vn)


_NAT_QROWS = 4
_NAT_GROUPS = GRID_ROWS // _NAT_QROWS
_NAT_KROWS = WIN_H + _NAT_QROWS
_NAT_Q = _NAT_QROWS * GRID_W
_NAT_KEYS = _NAT_KROWS * GRID_W
_NAT_DR = 2 * WIN_H - 1
_NAT_DC = 2 * WIN_W - 1


def _nat_key_row0(r):
    return jnp.clip(r - WIN_H // 2, 0, GRID_ROWS - WIN_H)


def _nat_build_bias(rpb_ref, tz_scr):
    c = lax.broadcasted_iota(jnp.int32, (GRID_W, LANES), 0)
    lane = lax.broadcasted_iota(jnp.int32, (GRID_W, LANES), 1)
    kc = lane & (GRID_W - 1)
    second = (lane >> 6) == 1
    win_start = jnp.clip(c - WIN_W // 2, 0, GRID_W - WIN_W)
    valid = (kc >= win_start) & (kc < win_start + WIN_W)

    def one_row(n, carry):
        dr = n >> 3
        h = n & (H_NAT - 1)
        src = jnp.broadcast_to(rpb_ref[pl.ds(h * _NAT_DR + dr, 1), :], (GRID_W, LANES))
        rolled = pltpu.roll(src, LANES - (WIN_W - 1), axis=1, stride=1, stride_axis=0)
        tz_scr[dr, h] = jnp.where(valid, rolled, NEG_INF)
        return carry

    lax.fori_loop(0, _NAT_DR * H_NAT, one_row, 0)

    def pair_rows(n, carry):
        dr = n >> 3
        h = n & (H_NAT - 1)
        tz_scr[dr, h] = jnp.where(second, tz_scr[dr + 1, h], tz_scr[dr, h])
        return carry

    lax.fori_loop(0, (_NAT_DR - 1) * H_NAT, pair_rows, 0)


def _nat_kernel(rpb_ref, q_ref, k_ref, v_ref, ck_ref, cv_ref, o_ref, tz_scr, bias_scr):
    grp = pl.program_id(1)

    @pl.when((pl.program_id(0) == 0) & (grp == 0))
    def _():
        _nat_build_bias(rpb_ref, tz_scr)

    scale = HD_NAT ** -0.5
    krow0 = jnp.clip(_NAT_QROWS * grp - WIN_H // 2, 0, GRID_ROWS - _NAT_KROWS)
    k0 = pl.multiple_of(krow0 * GRID_W, GRID_W)
    lane_q = lax.broadcasted_iota(jnp.int32, (_NAT_Q, LANES), 1)
    lane_b = lax.broadcasted_iota(jnp.int32, (GRID_W, LANES), 1)
    neg = jnp.full((GRID_W, LANES), NEG_INF, F32)
    for t in range(NAT_W // LANES):
        sl = slice(t * LANES, (t + 1) * LANES)
        qt = q_ref[:, sl]
        kw = k_ref[pl.ds(k0, _NAT_KEYS), sl].astype(BF16)
        vw = v_ref[pl.ds(k0, _NAT_KEYS), sl].astype(BF16)
        ck = ck_ref[0, 0, :, sl].astype(BF16)
        cv = cv_ref[0, 0, :, sl].astype(BF16)
        out = jnp.zeros((_NAT_Q, LANES), F32)
        for half in range(LANES // HD_NAT):
            h = 2 * t + half
            for qr in range(_NAT_QROWS):
                r = _NAT_QROWS * grp + qr
                lo = _nat_key_row0(r)
                for kp in range(_NAT_KROWS // 2):
                    kr = krow0 + 2 * kp
                    tile = tz_scr[jnp.clip(kr - r + (WIN_H - 1), 0, _NAT_DR - 1), h]
                    ok_a = ((kr >= lo) & (kr < lo + WIN_H)).astype(jnp.int32)
                    ok_b = ((kr + 1 >= lo) & (kr + 1 < lo + WIN_H)).astype(jnp.int32)
                    ok = jnp.where(lane_b < GRID_W, ok_a, ok_b) == 1
                    bias_scr[qr * GRID_W:(qr + 1) * GRID_W, kp * LANES:(kp + 1) * LANES] = jnp.where(ok, tile, neg)
            mine = (lane_q >> 6) == half
            qm = jnp.where(mine, qt, 0.0).astype(BF16)
            s_win = _dot_nt(qm, kw) * scale + bias_scr[...]
            s_ctx = _dot_nt(qm, ck) * scale
            out = jnp.where(mine, _softmax_pv([s_win, s_ctx], [vw, cv]), out)
        o_ref[:, sl] = out


def _nat_attention(qn, kn, vn, cache_k, cache_v, rpb_l, layer):
    lat0 = N_CTX // _NAT_Q
    half = jnp.pad(rpb_l.reshape(H_NAT * _NAT_DR, _NAT_DC), ((0, 0), (0, GRID_W - _NAT_DC)))
    rpb_rows = jnp.concatenate([half, half], axis=1)
    return pl.pallas_call(
        _nat_kernel,
        grid=(DEC_BATCH, _NAT_GROUPS),
        in_specs=[
            pl.BlockSpec((H_NAT * _NAT_DR, LANES), lambda b, g: (0, 0)),
            pl.BlockSpec((_NAT_Q, NAT_W), lambda b, g: (lat0 + b * _NAT_GROUPS + g, 0)),
            pl.BlockSpec((DEC_SEQ, NAT_W), lambda b, g: (b, 0)),
            pl.BlockSpec((DEC_SEQ, NAT_W), lambda b, g: (b, 0)),
            pl.BlockSpec((1, 1, PAST_LEN, NAT_W), lambda b, g: (b, layer, 0, 0)),
            pl.BlockSpec((1, 1, PAST_LEN, NAT_W), lambda b, g: (b, layer, 0, 0)),
        ],
        out_specs=pl.BlockSpec((_NAT_Q, NAT_W), lambda b, g: (b * _NAT_GROUPS + g, 0)),
        out_shape=jax.ShapeDtypeStruct((N_LAT, NAT_W), F32),
        scratch_shapes=[
            pltpu.VMEM((_NAT_DR, H_NAT, GRID_W, LANES), F32),
            pltpu.VMEM((_NAT_Q, _NAT_KEYS), F32),
        ],
        compiler_params=pltpu.CompilerParams(
            dimension_semantics=("arbitrary", "arbitrary"), vmem_limit_bytes=VMEM_LIMIT),
        name="nat_attention",
    )(rpb_rows, qn, kn, vn, cache_k, cache_v)


def _out_kernel(x_ref, o_ref, r_ref, on_ctx_ref, on_lat_ref, mod_ref, ng_ref, gng_ref, w_ref, y_ref):
    o_nat = _read_split(pl.program_id(0), on_ctx_ref, on_lat_ref)
    og = o_ref[0] + o_ref[1]
    parts = []
    for h in range(H_GLA):
        parts.append(_rms(og[:, h * DV_GLA:(h + 1) * DV_GLA], gng_ref[...]))
    merged = (jnp.concatenate(parts, axis=1) * _silu(r_ref[...])).astype(BF16)
    merged = _permute_chunks(_block_transpose_perm(), merged)
    y = _dot(merged, w_ref[0:V_W, :]) + _dot(o_nat.astype(BF16), w_ref[V_W:, :])
    y_ref[...] = x_ref[...] + mod_ref[0, 5:6, :] * _rms(y, ng_ref[3:4, :])


def _mixer_out(x, o_gla, r, o_ctx, o_lat, mod_l, ng_l, gng_l, w_out_bf):
    tile = lambda w: pl.BlockSpec((TM, w), lambda i: (i, 0))
    full = lambda a: pl.BlockSpec(a.shape, lambda i: (0,) * a.ndim)
    return pl.pallas_call(
        _out_kernel,
        grid=(N_ALL // TM,),
        in_specs=[
            tile(D_MODEL),
            pl.BlockSpec((2, TM, V_W), lambda i: (0, i, 0)),
            tile(V_W),
            pl.BlockSpec((TM, NAT_W), lambda i: (_ctx_tile(i), 0)),
            pl.BlockSpec((TM, NAT_W), lambda i: (_lat_tile(i), 0)),
            pl.BlockSpec((1, N_MOD, D_MODEL), lambda i: (_group_of_tile(i), 0, 0)),
            full(ng_l), full(gng_l), full(w_out_bf),
        ],
        out_specs=tile(D_MODEL),
        out_shape=jax.ShapeDtypeStruct((N_ALL, D_MODEL), F32),
        compiler_params=pltpu.CompilerParams(
            dimension_semantics=("arbitrary",), vmem_limit_bytes=VMEM_LIMIT),
        name="mixer_out",
    )(x, o_gla, r, o_ctx, o_lat, mod_l, ng_l, gng_l, w_out_bf)


def _rope_tables():
    quarter = DK_GLA // 4
    freqs = ROPE_BASE ** (-jnp.arange(quarter, dtype=F32) / quarter)
    t = jnp.arange(DEC_SEQ)
    ang_r = (t // GRID_W).astype(F32)[:, None] * freqs[None, :]
    ang_c = (t % GRID_W).astype(F32)[:, None] * freqs[None, :]
    cos_h = jnp.concatenate([jnp.cos(ang_r), jnp.cos(ang_r), jnp.cos(ang_c), jnp.cos(ang_c)], axis=1)
    sin_h = jnp.concatenate([-jnp.sin(ang_r), jnp.sin(ang_r), -jnp.sin(ang_c), jnp.sin(ang_c)], axis=1)
    pos_major = lambda a: a.reshape(-1, GLA_NB, GLA_CB, QK_W).transpose(0, 2, 1, 3).reshape(a.shape)
    cos_t = jnp.concatenate([pos_major(jnp.tile(cos_h, (1, H_GLA))), jnp.ones((TM, QK_W), F32)], axis=0)
    sin_t = jnp.concatenate([pos_major(jnp.tile(sin_h, (1, H_GLA))), jnp.zeros((TM, QK_W), F32)], axis=0)
    return cos_t, sin_t


def _swap_perm():
    quarter = DK_GLA // 4
    idx = np.arange(QK_W)
    blk = (idx // quarter) % 4
    return np.where(blk % 2 == 0, idx + quarter, idx - quarter)


def _gate_up_weights(gla_wa2_l, gla_ba_l):
    w_a = jnp.zeros((LANES, 2 * QK_W), F32)
    w_a = w_a.at[0:GLA_RANK, 0:QK_W].set(gla_wa2_l[0]).at[GLA_RANK:2 * GLA_RANK, QK_W:].set(gla_wa2_l[1])
    b_a = jnp.concatenate([gla_ba_l[0], gla_ba_l[1]])[None, :]
    return w_a.astype(BF16), b_a


def kernel(x_prompt, x_sample, cache_k, cache_v, state_gla, c, c_ctx, w_mod, b_mod, norm_g, ffn_w_in, ffn_w_out,
           w_in, gla_wa2, gla_ba, gla_norm_g, nat_rpb, w_out):
    cvecs = jnp.zeros((SUBLANES, D_MODEL), F32).at[0].set(c_ctx).at[1:1 + DEC_BATCH].set(c)
    mod = _modulation(cvecs, w_mod, b_mod)[:, :N_GROUPS].reshape(DEPTH, N_GROUPS, N_MOD, D_MODEL)

    ffn_w_in_bf = ffn_w_in.astype(BF16)
    ffn_w_out_bf = ffn_w_out.astype(BF16)
    w_out_bf = w_out.astype(BF16)
    cos_t, sin_t = _rope_tables()
    ck = cache_k.reshape(DEC_BATCH, DEPTH, PAST_LEN, NAT_W)
    cv = cache_v.reshape(DEC_BATCH, DEPTH, PAST_LEN, NAT_W)

    w_in_bf = w_in.astype(BF16)
    swap = _swap_perm()
    w_sw = jnp.concatenate([w_in_bf[:, :, swap], w_in_bf[:, :, QK_W + swap]], axis=2)
    w_nat = w_in_bf[:, :, _NAT_COL0:]

    x = (x_prompt.reshape(N_CTX, D_MODEL), x_sample.reshape(N_LAT, D_MODEL))
    k_list, v_list, s_list = [], [], []
    for l in range(DEPTH):
        mod_l, ng_l = mod[l], norm_g[l]
        x = _ffn(x, mod_l, ng_l, ffn_w_in_bf, ffn_w_out_bf, layer=l, which=0)

        w_a, b_a = _gate_up_weights(gla_wa2[l], gla_ba[l])
        q, k, v, r, g, qn, kn_ctx, vn_ctx, kn_lat, vn_lat = _project(
            x, mod_l, ng_l, w_in_bf, w_sw, w_nat, w_a, b_a, cos_t, sin_t, layer=l)

        s0 = jnp.concatenate([jnp.zeros((BATCH, 2, H_GLA, DK_GLA, DV_GLA), F32), state_gla[:, l]], axis=0)
        o_gla, s_fin = _gla(q, k, v, g, s0)
        o_ctx = _ctx_attention(qn, kn_ctx, vn_ctx)
        o_lat = _nat_attention(qn, kn_lat, vn_lat, ck, cv, nat_rpb[l], l)

        x = _mixer_out(x, o_gla, r, o_ctx, o_lat, mod_l, ng_l, gla_norm_g[l][None, :], w_out_bf[l])
        x = _ffn(x, mod_l, ng_l, ffn_w_in_bf, ffn_w_out_bf, layer=l, which=1, split_out=(l == DEPTH - 1))

        k_list.append(kn_ctx.reshape(BATCH, SEQ, H_NAT, HD_NAT))
        v_list.append(vn_ctx.reshape(BATCH, SEQ, H_NAT, HD_NAT))
        s_list.append(s_fin[:BATCH])

    y_prompt = x[0].reshape(BATCH, SEQ, D_MODEL)
    y_sample = x[1].reshape(DEC_BATCH, DEC_SEQ, D_MODEL)
    return (y_prompt, y_sample, jnp.stack(k_list, axis=1), jnp.stack(v_list, axis=1), jnp.stack(s_list, axis=1))
```

```python
import functools

import numpy as np
import jax
import jax.numpy as jnp
from jax import lax
from jax.experimental import pallas as pl
from jax.experimental.pallas import tpu as pltpu

D_MODEL = 1024
BATCH = 16
SEQ = 256
DEPTH = 2
DEC_BATCH = 2
DEC_SEQ = 1024
PAST_LEN = 512
GRID_W = 64
H_GLA = 4
DK_GLA = 64
DV_GLA = 128
GLA_RANK = 16
GATE_NORM = 16.0
H_NAT = 8
HD_NAT = 64
WIN_H = 8
WIN_W = 16
D_FF = 2816
N_MOD = 9
ROPE_BASE = 10000.0
EPS = 1e-6
NEG_INF = -1e30

F32 = jnp.float32
BF16 = jnp.bfloat16

N_CTX = BATCH * SEQ
N_LAT = DEC_BATCH * DEC_SEQ
N_ALL = N_CTX + N_LAT
N_GROUPS = 1 + DEC_BATCH
QK_W = H_GLA * DK_GLA
V_W = H_GLA * DV_GLA
NAT_W = H_NAT * HD_NAT
GRID_ROWS = DEC_SEQ // GRID_W

LANES = 128
SUBLANES = 8

TM = 512
TF = 1408
MOD_TN = 1152
GLA_T = 256
GLA_CB = 16
GLA_NB = GLA_T // GLA_CB
VMEM_LIMIT = 56 * 1024 * 1024

assert N_CTX % TM == 0 and DEC_SEQ % TM == 0 and D_FF % TF == 0 and TF % LANES == 0
assert SEQ == GLA_T and DEC_SEQ % GLA_T == 0


def _group_of_tile(i):
    return jnp.where(i < N_CTX // TM, 0, 1 + (i - N_CTX // TM) // (DEC_SEQ // TM))


def _dot(a, b):
    return jnp.dot(a, b, preferred_element_type=F32)


def _dot_nt(a, b):
    return lax.dot_general(a, b, (((1,), (1,)), ((), ())), preferred_element_type=F32)


def _dot_tn(a, b):
    return lax.dot_general(a, b, (((0,), (0,)), ((), ())), preferred_element_type=F32)


def _rms(x, g):
    ms = jnp.mean(x * x, axis=-1, keepdims=True)
    return x * lax.rsqrt(ms + EPS) * g


def _silu(x):
    return x * jax.nn.sigmoid(x)


def _mod_kernel(c_ref, w_ref, b_ref, o_ref):
    s = _silu(c_ref[...]).astype(BF16)
    o_ref[0] = _dot(s, w_ref[0].astype(BF16)) + b_ref[0]


def _modulation(cvecs, w_mod, b_mod):
    n_out = N_MOD * D_MODEL
    return pl.pallas_call(
        _mod_kernel,
        grid=(DEPTH, n_out // MOD_TN),
        in_specs=[
            pl.BlockSpec((SUBLANES, D_MODEL), lambda l, j: (0, 0)),
            pl.BlockSpec((1, D_MODEL, MOD_TN), lambda l, j: (l, 0, j)),
            pl.BlockSpec((1, 1, MOD_TN), lambda l, j: (l, 0, j)),
        ],
        out_specs=pl.BlockSpec((1, SUBLANES, MOD_TN), lambda l, j: (l, 0, j)),
        out_shape=jax.ShapeDtypeStruct((DEPTH, SUBLANES, n_out), F32),
        compiler_params=pltpu.CompilerParams(
            dimension_semantics=("arbitrary", "arbitrary"), vmem_limit_bytes=VMEM_LIMIT),
        name="modulation",
    )(cvecs, w_mod, b_mod.reshape(DEPTH, 1, n_out))


_CTX_TILES = N_CTX // TM
_LAT_TILES = N_LAT // TM


def _ctx_tile(i):
    return jnp.minimum(i, _CTX_TILES - 1)


def _lat_tile(i):
    return jnp.maximum(i - _CTX_TILES, 0)


def _read_split(i, ctx_ref, lat_ref):
    return jnp.where(i < _CTX_TILES, ctx_ref[...], lat_ref[...])


def _ffn_kernel(*refs, m0, n0, split_in, split_out):
    n_x = 2 if split_in else 1
    x_refs, (mod_ref, ng_ref, wg_ref, wu_ref, wo_ref) = refs[:n_x], refs[n_x:n_x + 5]
    n_o = 2 if split_out else 1
    o_refs = refs[n_x + 5:n_x + 5 + n_o]
    h_scr, acc_scr, x_scr = refs[n_x + 5 + n_o:]
    i = pl.program_id(0)
    j = pl.program_id(1)

    @pl.when(j == 0)
    def _():
        x = _read_split(i, *x_refs) if split_in else x_refs[0][...]
        x_scr[...] = x
        gain = ng_ref[n0:n0 + 1, :] * (1.0 + mod_ref[0, m0 + 1:m0 + 2, :])
        h_scr[...] = (_rms(x, gain) + mod_ref[0, m0:m0 + 1, :]).astype(BF16)
        acc_scr[...] = jnp.zeros_like(acc_scr)

    h = h_scr[...]
    gate = _dot(h, wg_ref[0, 0])
    up = _dot(h, wu_ref[0, 0])
    act = (_silu(gate) * up).astype(BF16)
    acc_scr[...] += _dot(act, wo_ref[0, 0])

    @pl.when(j == pl.num_programs(1) - 1)
    def _():
        gain = 0.5 * mod_ref[0, m0 + 2:m0 + 3, :] * ng_ref[n0 + 1:n0 + 2, :]
        out = x_scr[...] + _rms(acc_scr[...], gain)
        if split_out:
            @pl.when(i < _CTX_TILES)
            def _():
                o_refs[0][...] = out

            @pl.when(i >= _CTX_TILES)
            def _():
                o_refs[1][...] = out
        else:
            o_refs[0][...] = out


def _ffn(x, mod_l, ng_l, w_in_bf, w_out_bf, *, layer, which, split_out=False):
    nf = D_FF // TF
    m0, n0 = (0, 0) if which == 0 else (6, 4)
    split_in = isinstance(x, tuple)
    tile = pl.BlockSpec((TM, D_MODEL), lambda i, j: (i, 0))
    ctx_tile = pl.BlockSpec((TM, D_MODEL), lambda i, j: (_ctx_tile(i), 0))
    lat_tile = pl.BlockSpec((TM, D_MODEL), lambda i, j: (_lat_tile(i), 0))
    if split_out:
        out_specs = [ctx_tile, lat_tile]
        out_shape = [jax.ShapeDtypeStruct((N_CTX, D_MODEL), F32), jax.ShapeDtypeStruct((N_LAT, D_MODEL), F32)]
    else:
        out_specs = tile
        out_shape = jax.ShapeDtypeStruct((N_ALL, D_MODEL), F32)
    return pl.pallas_call(
        functools.partial(_ffn_kernel, m0=m0, n0=n0, split_in=split_in, split_out=split_out),
        grid=(N_ALL // TM, nf),
        in_specs=([ctx_tile, lat_tile] if split_in else [tile]) + [
            pl.BlockSpec((1, N_MOD, D_MODEL), lambda i, j: (_group_of_tile(i), 0, 0)),
            pl.BlockSpec((6, D_MODEL), lambda i, j: (0, 0)),
            pl.BlockSpec((1, 1, D_MODEL, TF), lambda i, j: (layer, which, 0, j)),
            pl.BlockSpec((1, 1, D_MODEL, TF), lambda i, j: (layer, which, 0, j + nf)),
            pl.BlockSpec((1, 1, TF, D_MODEL), lambda i, j: (layer, which, j, 0)),
        ],
        out_specs=out_specs,
        out_shape=out_shape,
        scratch_shapes=[pltpu.VMEM((TM, D_MODEL), BF16), pltpu.VMEM((TM, D_MODEL), F32),
                        pltpu.VMEM((TM, D_MODEL), F32)],
        compiler_params=pltpu.CompilerParams(
            dimension_semantics=("arbitrary", "arbitrary"), vmem_limit_bytes=VMEM_LIMIT),
        name="ffn",
    )(*(x if split_in else (x,)), mod_l, ng_l, w_in_bf, w_in_bf, w_out_bf)


def _block_transpose_perm():
    r = lax.broadcasted_iota(jnp.int32, (GLA_T, GLA_T), 0)
    c = lax.broadcasted_iota(jnp.int32, (GLA_T, GLA_T), 1)
    return jnp.where(c == (r & (GLA_CB - 1)) * GLA_NB + (r >> 4), 1.0, 0.0).astype(BF16)


def _permute_chunks(perm, a):
    return jnp.concatenate([_dot(perm, a[c * GLA_T:(c + 1) * GLA_T, :]).astype(BF16)
                            for c in range(a.shape[0] // GLA_T)], axis=0)


_P_QK = 0
_P_VR = 4 * QK_W
_P_NAT = _P_VR + 2 * V_W
_P_LR = _P_NAT + 3 * NAT_W
_P_END = _P_LR + LANES


def _proj_kernel(x_ref, mod_ref, ng_ref, w_ref, wa_ref, ba_ref, cos_ref, sin_ref,
                 q_ref, k_ref, v_ref, r_ref, g_ref, qn_ref, kn_ctx_ref, vn_ctx_ref, kn_lat_ref, vn_lat_ref):
    i = pl.program_id(0)
    h = _rms(x_ref[...], ng_ref[2:3, :])
    h = (h * (1.0 + mod_ref[0, 4:5, :]) + mod_ref[0, 3:4, :]).astype(BF16)
    hp = _permute_chunks(_block_transpose_perm(), h)

    qk = _dot(hp, w_ref[0, :, _P_QK:_P_VR])
    cos = cos_ref[...]
    sin = sin_ref[...]
    q_ref[...] = (qk[:, 0:QK_W] * cos + qk[:, 2 * QK_W:3 * QK_W] * sin) * (DK_GLA ** -0.5)
    k_ref[...] = qk[:, QK_W:2 * QK_W] * cos + qk[:, 3 * QK_W:4 * QK_W] * sin

    vr = _dot(hp, w_ref[0, :, _P_VR:_P_NAT])
    v_ref[...] = vr[:, 0:V_W]
    r_ref[...] = vr[:, V_W:2 * V_W]

    nat = _dot(h, w_ref[0, :, _P_NAT:_P_LR])
    qn_ref[...] = nat[:, 0:NAT_W]

    @pl.when(i < _CTX_TILES)
    def _():
        kn_ctx_ref[...] = nat[:, NAT_W:2 * NAT_W]
        vn_ctx_ref[...] = nat[:, 2 * NAT_W:3 * NAT_W]

    @pl.when(i >= _CTX_TILES)
    def _():
        kn_lat_ref[...] = nat[:, NAT_W:2 * NAT_W]
        vn_lat_ref[...] = nat[:, 2 * NAT_W:3 * NAT_W]

    lr = _dot(hp, w_ref[0, :, _P_LR:_P_END]).astype(BF16)
    z = _dot(lr, wa_ref[...]) + ba_ref[...]
    g = (jnp.minimum(z, 0.0) - jnp.log1p(jnp.exp(-jnp.abs(z)))) * (1.0 / GATE_NORM)
    g_ref[0] = g[:, 0:QK_W]
    g_ref[1] = g[:, QK_W:2 * QK_W]


def _rope_table_block(i):
    lat_tiles = DEC_SEQ // TM
    return jnp.where(i < N_CTX // TM, lat_tiles, (i - N_CTX // TM) % lat_tiles)


def _project(x, mod_l, ng_l, w_big, w_a, b_a, cos_t, sin_t, *, layer):
    tile = lambda w: pl.BlockSpec((TM, w), lambda i: (i, 0))
    ctx_tile = pl.BlockSpec((TM, NAT_W), lambda i: (_ctx_tile(i), 0))
    lat_tile = pl.BlockSpec((TM, NAT_W), lambda i: (_lat_tile(i), 0))
    full = lambda a: pl.BlockSpec(a.shape, lambda i: (0,) * a.ndim)
    return pl.pallas_call(
        _proj_kernel,
        grid=(N_ALL // TM,),
        in_specs=[
            tile(D_MODEL),
            pl.BlockSpec((1, N_MOD, D_MODEL), lambda i: (_group_of_tile(i), 0, 0)),
            full(ng_l),
            pl.BlockSpec((1, D_MODEL, _P_END), lambda i: (layer, 0, 0)),
            full(w_a), full(b_a),
            pl.BlockSpec((TM, QK_W), lambda i: (_rope_table_block(i), 0)),
            pl.BlockSpec((TM, QK_W), lambda i: (_rope_table_block(i), 0)),
        ],
        out_specs=[
            tile(QK_W), tile(QK_W), tile(V_W), tile(V_W),
            pl.BlockSpec((2, TM, QK_W), lambda i: (0, i, 0)),
            tile(NAT_W), ctx_tile, ctx_tile, lat_tile, lat_tile,
        ],
        out_shape=[
            jax.ShapeDtypeStruct((N_ALL, QK_W), F32), jax.ShapeDtypeStruct((N_ALL, QK_W), F32),
            jax.ShapeDtypeStruct((N_ALL, V_W), F32), jax.ShapeDtypeStruct((N_ALL, V_W), F32),
            jax.ShapeDtypeStruct((2, N_ALL, QK_W), F32),
            jax.ShapeDtypeStruct((N_ALL, NAT_W), F32),
            jax.ShapeDtypeStruct((N_CTX, NAT_W), F32), jax.ShapeDtypeStruct((N_CTX, NAT_W), F32),
            jax.ShapeDtypeStruct((N_LAT, NAT_W), F32), jax.ShapeDtypeStruct((N_LAT, NAT_W), F32),
        ],
        compiler_params=pltpu.CompilerParams(
            dimension_semantics=("arbitrary",), vmem_limit_bytes=VMEM_LIMIT),
        name="mixer_proj",
    )(x, mod_l, ng_l, w_big, w_a, b_a, cos_t, sin_t)


def _gla_tables():
    rows = []
    seq_specs = [(b * (SEQ // GLA_T), SEQ // GLA_T) for b in range(BATCH)]
    seq_specs += [(N_CTX // GLA_T + b * (DEC_SEQ // GLA_T), DEC_SEQ // GLA_T) for b in range(DEC_BATCH)]
    for sid, (blk0, nchunk) in enumerate(seq_specs):
        for direction in (0, 1):
            order = range(nchunk) if direction == 0 else range(nchunk - 1, -1, -1)
            for n, c in enumerate(order):
                rows.append((blk0 + c, direction, int(n == 0), sid))
    return np.asarray(rows, dtype=np.int32).T.copy()


_GLA_TAB = _gla_tables()
_GLA_ITEMS = _GLA_TAB.shape[1]
_GLA_NSEQ = BATCH + DEC_BATCH
_GLA_PAIR_ROWS = GLA_NB * GLA_CB * (GLA_CB + 1) // 2


def _gla_item(direction, q_ref, k_ref, v_ref, g_ref, o_ref, s_scr, cp, sstk, p_scr, w_scr):
    T, CB, NB = GLA_T, GLA_CB, GLA_NB
    fwd = direction == 0
    slab = lambda i: slice(i * NB, (i + 1) * NB)

    cum = None
    for i in (range(CB) if fwd else range(CB - 1, -1, -1)):
        gi = g_ref[0, slab(i), :]
        cum = gi if cum is None else cum + gi
        cp[slab(i), :] = cum
    total = cum
    cum_all = cp[...]
    qt = q_ref[...] * jnp.exp(cum_all)
    kh = k_ref[...] * jnp.exp(jnp.concatenate([total] * CB, axis=0) - cum_all)
    dec_t = jnp.concatenate([jnp.exp(total), jnp.zeros((LANES - NB, QK_W), F32)], axis=0).T

    erow = lax.broadcasted_iota(jnp.int32, (QK_W, QK_W), 0)
    ecol = lax.broadcasted_iota(jnp.int32, (QK_W, QK_W), 1)
    head_sum = jnp.where((erow >> 6) == (ecol >> 6), 1.0, 0.0).astype(BF16)

    key_positions = lambda i: range(i + 1) if fwd else range(i, CB)
    r0 = 0
    for i in range(CB):
        qi = q_ref[slab(i), :]
        ci = cp[slab(i), :]
        for j in key_positions(i):
            e = jnp.exp(ci - cp[slab(j), :])
            p_scr[r0:r0 + NB, :] = (qi * k_ref[slab(j), :] * e).astype(BF16)
            r0 += NB
    w_scr[...] = _dot(p_scr[...], head_sum)
    first_half = lax.broadcasted_iota(jnp.int32, (NB, LANES), 1) < DK_GLA
    r0 = 0
    for i in range(CB):
        acc = None
        for j in key_positions(i):
            spread = []
            for pair in range(H_GLA // 2):
                tile = w_scr[r0:r0 + NB, pair * LANES:(pair + 1) * LANES]
                other = pltpu.roll(tile, DK_GLA, axis=1)
                spread += [jnp.where(first_half, tile, other), jnp.where(first_half, other, tile)]
            term = jnp.concatenate(spread, axis=1) * v_ref[slab(j), :]
            acc = term if acc is None else acc + term
            r0 += NB
        o_ref[0, slab(i), :] = acc

    kht = kh.T.astype(BF16)
    v_bf = v_ref[...].astype(BF16)
    key_blk = lax.broadcasted_iota(jnp.int32, (DK_GLA, T), 1) & (NB - 1)
    row_blk = lax.broadcasted_iota(jnp.int32, (T, LANES), 0) & (NB - 1)
    lane_half = lax.broadcasted_iota(jnp.int32, (T, LANES), 1) >> 6
    order = range(NB) if fwd else range(NB - 1, -1, -1)
    for h in range(H_GLA):
        kh_h = kht[h * DK_GLA:(h + 1) * DK_GLA, :]
        kv = _dot(jnp.concatenate([jnp.where(key_blk == b, kh_h, 0) for b in range(NB)], axis=0),
                  v_bf[:, h * DV_GLA:(h + 1) * DV_GLA])
        s = s_scr[h]
        for b in order:
            sstk[h, b * DK_GLA:(b + 1) * DK_GLA, :] = s.astype(BF16)
            s = dec_t[h * DK_GLA:(h + 1) * DK_GLA, b:b + 1] * s + kv[b * DK_GLA:(b + 1) * DK_GLA, :]
        s_scr[h] = s

        pair_tile = qt[:, (h // 2) * LANES:(h // 2 + 1) * LANES]
        both = jnp.where(lane_half == h % 2, pair_tile, pltpu.roll(pair_tile, DK_GLA, axis=1))
        lhs = jnp.concatenate([jnp.where(row_blk == 2 * j + lane_half, both, 0.0).astype(BF16)
                               for j in range(NB // 2)], axis=1)
        o_ref[0, :, h * DV_GLA:(h + 1) * DV_GLA] += _dot(lhs, sstk[h])


def _gla_kernel(tab_ref, q_ref, k_ref, v_ref, g_ref, s0_ref, o_ref, so_ref, s_scr, cp, sstk, p_scr, w_scr):
    it = pl.program_id(0)

    @pl.when(tab_ref[2, it] == 1)
    def _():
        s_scr[...] = s0_ref[0, 0]

    for direction in (0, 1):
        @pl.when(tab_ref[1, it] == direction)
        def _(direction=direction):
            _gla_item(direction, q_ref, k_ref, v_ref, g_ref, o_ref, s_scr, cp, sstk, p_scr, w_scr)

    so_ref[0, 0] = s_scr[...]


def _gla(q, k, v, g, s0):
    tok = lambda w: pl.BlockSpec((GLA_T, w), lambda it, tab: (tab[0, it], 0))
    state = pl.BlockSpec((1, 1, H_GLA, DK_GLA, DV_GLA), lambda it, tab: (tab[3, it], tab[1, it], 0, 0, 0))
    grid_spec = pltpu.PrefetchScalarGridSpec(
        num_scalar_prefetch=1,
        grid=(_GLA_ITEMS,),
        in_specs=[
            tok(QK_W), tok(QK_W), tok(V_W),
            pl.BlockSpec((1, GLA_T, QK_W), lambda it, tab: (tab[1, it], tab[0, it], 0)),
            state,
        ],
        out_specs=[
            pl.BlockSpec((1, GLA_T, V_W), lambda it, tab: (tab[1, it], tab[0, it], 0)),
            state,
        ],
        scratch_shapes=[
            pltpu.VMEM((H_GLA, DK_GLA, DV_GLA), F32),
            pltpu.VMEM((GLA_T, QK_W), F32),
            pltpu.VMEM((H_GLA, GLA_NB * DK_GLA, DV_GLA), BF16),
            pltpu.VMEM((_GLA_PAIR_ROWS, QK_W), BF16),
            pltpu.VMEM((_GLA_PAIR_ROWS, QK_W), F32),
        ],
    )
    return pl.pallas_call(
        _gla_kernel,
        grid_spec=grid_spec,
        out_shape=[
            jax.ShapeDtypeStruct((2, N_ALL, V_W), F32),
            jax.ShapeDtypeStruct((_GLA_NSEQ, 2, H_GLA, DK_GLA, DV_GLA), F32),
        ],
        compiler_params=pltpu.CompilerParams(
            dimension_semantics=("arbitrary",), vmem_limit_bytes=VMEM_LIMIT),
        name="gla",
    )(jnp.asarray(_GLA_TAB), q, k, v, g, s0)


def _softmax_pv(s_list, v_list):
    m = s_list[0].max(axis=-1, keepdims=True)
    for s in s_list[1:]:
        m = jnp.maximum(m, s.max(axis=-1, keepdims=True))
    num = None
    den = None
    for s, vv in zip(s_list, v_list):
        e = jnp.exp(s - m)
        den = e.sum(axis=-1, keepdims=True) if den is None else den + e.sum(axis=-1, keepdims=True)
        pv = _dot(e.astype(BF16), vv)
        num = pv if num is None else num + pv
    return num / den


def _ctx_attn_kernel(q_ref, k_ref, v_ref, o_ref):
    scale = HD_NAT ** -0.5
    lane = lax.broadcasted_iota(jnp.int32, (SEQ, LANES), 1)
    for t in range(NAT_W // LANES):
        sl = slice(t * LANES, (t + 1) * LANES)
        qt = q_ref[:, sl]
        kt = k_ref[:, sl].astype(BF16)
        vt = v_ref[:, sl].astype(BF16)
        out = jnp.zeros((SEQ, LANES), F32)
        for half in range(LANES // HD_NAT):
            mine = (lane >> 6) == half
            s = _dot_nt(jnp.where(mine, qt, 0.0).astype(BF16), kt) * scale
            out = jnp.where(mine, _softmax_pv([s], [vt]), out)
        o_ref[:, sl] = out


def _ctx_attention(qn, kn, vn):
    spec = pl.BlockSpec((SEQ, NAT_W), lambda b: (b, 0))
    return pl.pallas_call(
        _ctx_attn_kernel,
        grid=(BATCH,),
        in_specs=[spec, spec, spec],
        out_specs=spec,
        out_shape=jax.ShapeDtypeStruct((N_CTX, NAT_W), F32),
        compiler_params=pltpu.CompilerParams(
            dimension_semantics=("arbitrary",), vmem_limit_bytes=VMEM_LIMIT),
        name="ctx_attention",
    )(qn, kn, vn)


_NAT_QROWS = 4
_NAT_GROUPS = GRID_ROWS // _NAT_QROWS
_NAT_KROWS = WIN_H + _NAT_QROWS
_NAT_Q = _NAT_QROWS * GRID_W
_NAT_KEYS = _NAT_KROWS * GRID_W
_NAT_DR = 2 * WIN_H - 1
_NAT_DC = 2 * WIN_W - 1


def _nat_key_row0(r):
    return jnp.clip(r - WIN_H // 2, 0, GRID_ROWS - WIN_H)


def _nat_build_bias(rpb_ref, tz_scr):
    c = lax.broadcasted_iota(jnp.int32, (GRID_W, LANES), 0)
    lane = lax.broadcasted_iota(jnp.int32, (GRID_W, LANES), 1)
    kc = lane & (GRID_W - 1)
    second = (lane >> 6) == 1
    win_start = jnp.clip(c - WIN_W // 2, 0, GRID_W - WIN_W)
    valid = (kc >= win_start) & (kc < win_start + WIN_W)

    def one_row(n, carry):
        dr = n >> 3
        h = n & (H_NAT - 1)
        src = jnp.broadcast_to(rpb_ref[pl.ds(h * _NAT_DR + dr, 1), :], (GRID_W, LANES))
        rolled = pltpu.roll(src, LANES - (WIN_W - 1), axis=1, stride=1, stride_axis=0)
        tz_scr[dr, h] = jnp.where(valid, rolled, NEG_INF)
        return carry

    lax.fori_loop(0, _NAT_DR * H_NAT, one_row, 0)

    def pair_rows(n, carry):
        dr = n >> 3
        h = n & (H_NAT - 1)
        tz_scr[dr, h] = jnp.where(second, tz_scr[dr + 1, h], tz_scr[dr, h])
        return carry

    lax.fori_loop(0, (_NAT_DR - 1) * H_NAT, pair_rows, 0)


def _nat_kernel(rpb_ref, q_ref, k_ref, v_ref, ck_ref, cv_ref, o_ref, tz_scr, bias_scr):
    grp = pl.program_id(1)

    @pl.when((pl.program_id(0) == 0) & (grp == 0))
    def _():
        _nat_build_bias(rpb_ref, tz_scr)

    scale = HD_NAT ** -0.5
    krow0 = jnp.clip(_NAT_QROWS * grp - WIN_H // 2, 0, GRID_ROWS - _NAT_KROWS)
    k0 = pl.multiple_of(krow0 * GRID_W, GRID_W)
    lane_q = lax.broadcasted_iota(jnp.int32, (_NAT_Q, LANES), 1)
    lane_b = lax.broadcasted_iota(jnp.int32, (GRID_W, LANES), 1)
    neg = jnp.full((GRID_W, LANES), NEG_INF, F32)
    for t in range(NAT_W // LANES):
        sl = slice(t * LANES, (t + 1) * LANES)
        qt = q_ref[:, sl]
        kw = k_ref[pl.ds(k0, _NAT_KEYS), sl].astype(BF16)
        vw = v_ref[pl.ds(k0, _NAT_KEYS), sl].astype(BF16)
        ck = ck_ref[0, 0, :, sl].astype(BF16)
        cv = cv_ref[0, 0, :, sl].astype(BF16)
        out = jnp.zeros((_NAT_Q, LANES), F32)
        for half in range(LANES // HD_NAT):
            h = 2 * t + half
            for qr in range(_NAT_QROWS):
                r = _NAT_QROWS * grp + qr
                lo = _nat_key_row0(r)
                for kp in range(_NAT_KROWS // 2):
                    kr = krow0 + 2 * kp
                    tile = tz_scr[jnp.clip(kr - r + (WIN_H - 1), 0, _NAT_DR - 1), h]
                    ok_a = ((kr >= lo) & (kr < lo + WIN_H)).astype(jnp.int32)
                    ok_b = ((kr + 1 >= lo) & (kr + 1 < lo + WIN_H)).astype(jnp.int32)
                    ok = jnp.where(lane_b < GRID_W, ok_a, ok_b) == 1
                    bias_scr[qr * GRID_W:(qr + 1) * GRID_W, kp * LANES:(kp + 1) * LANES] = jnp.where(ok, tile, neg)
            mine = (lane_q >> 6) == half
            qm = jnp.where(mine, qt, 0.0).astype(BF16)
            s_win = _dot_nt(qm, kw) * scale + bias_scr[...]
            s_ctx = _dot_nt(qm, ck) * scale
            out = jnp.where(mine, _softmax_pv([s_win, s_ctx], [vw, cv]), out)
        o_ref[:, sl] = out


def _nat_attention(qn, kn, vn, cache_k, cache_v, rpb_l, layer):
    lat0 = N_CTX // _NAT_Q
    half = jnp.pad(rpb_l.reshape(H_NAT * _NAT_DR, _NAT_DC), ((0, 0), (0, GRID_W - _NAT_DC)))
    rpb_rows = jnp.concatenate([half, half], axis=1)
    return pl.pallas_call(
        _nat_kernel,
        grid=(DEC_BATCH, _NAT_GROUPS),
        in_specs=[
            pl.BlockSpec((H_NAT * _NAT_DR, LANES), lambda b, g: (0, 0)),
            pl.BlockSpec((_NAT_Q, NAT_W), lambda b, g: (lat0 + b * _NAT_GROUPS + g, 0)),
            pl.BlockSpec((DEC_SEQ, NAT_W), lambda b, g: (b, 0)),
            pl.BlockSpec((DEC_SEQ, NAT_W), lambda b, g: (b, 0)),
            pl.BlockSpec((1, 1, PAST_LEN, NAT_W), lambda b, g: (b, layer, 0, 0)),
            pl.BlockSpec((1, 1, PAST_LEN, NAT_W), lambda b, g: (b, layer, 0, 0)),
        ],
        out_specs=pl.BlockSpec((_NAT_Q, NAT_W), lambda b, g: (b * _NAT_GROUPS + g, 0)),
        out_shape=jax.ShapeDtypeStruct((N_LAT, NAT_W), F32),
        scratch_shapes=[
            pltpu.VMEM((_NAT_DR, H_NAT, GRID_W, LANES), F32),
            pltpu.VMEM((_NAT_Q, _NAT_KEYS), F32),
        ],
        compiler_params=pltpu.CompilerParams(
            dimension_semantics=("arbitrary", "arbitrary"), vmem_limit_bytes=VMEM_LIMIT),
        name="nat_attention",
    )(rpb_rows, qn, kn, vn, cache_k, cache_v)


def _out_kernel(x_ref, o_ref, r_ref, on_ctx_ref, on_lat_ref, mod_ref, ng_ref, gng_ref, w_ref, y_ref):
    o_nat = _read_split(pl.program_id(0), on_ctx_ref, on_lat_ref)
    og = o_ref[0] + o_ref[1]
    parts = []
    for h in range(H_GLA):
        parts.append(_rms(og[:, h * DV_GLA:(h + 1) * DV_GLA], gng_ref[...]))
    merged = (jnp.concatenate(parts, axis=1) * _silu(r_ref[...])).astype(BF16)
    merged = _permute_chunks(_block_transpose_perm(), merged)
    y = _dot(merged, w_ref[0:V_W, :]) + _dot(o_nat.astype(BF16), w_ref[V_W:, :])
    y_ref[...] = x_ref[...] + mod_ref[0, 5:6, :] * _rms(y, ng_ref[3:4, :])


def _mixer_out(x, o_gla, r, o_ctx, o_lat, mod_l, ng_l, gng_l, w_out_bf):
    tile = lambda w: pl.BlockSpec((TM, w), lambda i: (i, 0))
    full = lambda a: pl.BlockSpec(a.shape, lambda i: (0,) * a.ndim)
    return pl.pallas_call(
        _out_kernel,
        grid=(N_ALL // TM,),
        in_specs=[
            tile(D_MODEL),
            pl.BlockSpec((2, TM, V_W), lambda i: (0, i, 0)),
            tile(V_W),
            pl.BlockSpec((TM, NAT_W), lambda i: (_ctx_tile(i), 0)),
            pl.BlockSpec((TM, NAT_W), lambda i: (_lat_tile(i), 0)),
            pl.BlockSpec((1, N_MOD, D_MODEL), lambda i: (_group_of_tile(i), 0, 0)),
            full(ng_l), full(gng_l), full(w_out_bf),
        ],
        out_specs=tile(D_MODEL),
        out_shape=jax.ShapeDtypeStruct((N_ALL, D_MODEL), F32),
        compiler_params=pltpu.CompilerParams(
            dimension_semantics=("arbitrary",), vmem_limit_bytes=VMEM_LIMIT),
        name="mixer_out",
    )(x, o_gla, r, o_ctx, o_lat, mod_l, ng_l, gng_l, w_out_bf)


def _rope_tables():
    quarter = DK_GLA // 4
    freqs = ROPE_BASE ** (-jnp.arange(quarter, dtype=F32) / quarter)
    t = jnp.arange(DEC_SEQ)
    ang_r = (t // GRID_W).astype(F32)[:, None] * freqs[None, :]
    ang_c = (t % GRID_W).astype(F32)[:, None] * freqs[None, :]
    cos_h = jnp.concatenate([jnp.cos(ang_r), jnp.cos(ang_r), jnp.cos(ang_c), jnp.cos(ang_c)], axis=1)
    sin_h = jnp.concatenate([-jnp.sin(ang_r), jnp.sin(ang_r), -jnp.sin(ang_c), jnp.sin(ang_c)], axis=1)
    pos_major = lambda a: a.reshape(-1, GLA_NB, GLA_CB, QK_W).transpose(0, 2, 1, 3).reshape(a.shape)
    cos_t = jnp.concatenate([pos_major(jnp.tile(cos_h, (1, H_GLA))), jnp.ones((TM, QK_W), F32)], axis=0)
    sin_t = jnp.concatenate([pos_major(jnp.tile(sin_h, (1, H_GLA))), jnp.zeros((TM, QK_W), F32)], axis=0)
    return cos_t, sin_t


def _swap_perm():
    quarter = DK_GLA // 4
    idx = np.arange(QK_W)
    blk = (idx // quarter) % 4
    return np.where(blk % 2 == 0, idx + quarter, idx - quarter)


def _pack_w_in(w_in):
    sizes = [QK_W, QK_W, V_W, V_W, 2 * GLA_RANK, 3 * NAT_W]
    cuts = np.cumsum([0] + sizes)
    wq, wk, wv, wr, wlr, wnat = [w_in[:, :, cuts[i]:cuts[i + 1]] for i in range(len(sizes))]
    swap = _swap_perm()
    pad = jnp.zeros((DEPTH, D_MODEL, LANES - 2 * GLA_RANK), w_in.dtype)
    return jnp.concatenate([wq, wk, wq[:, :, swap], wk[:, :, swap], wv, wr, wnat, wlr, pad], axis=2).astype(BF16)


def _gate_up_weights(gla_wa2_l, gla_ba_l):
    w_a = jnp.zeros((LANES, 2 * QK_W), F32)
    w_a = w_a.at[0:GLA_RANK, 0:QK_W].set(gla_wa2_l[0]).at[GLA_RANK:2 * GLA_RANK, QK_W:].set(gla_wa2_l[1])
    b_a = jnp.concatenate([gla_ba_l[0], gla_ba_l[1]])[None, :]
    return w_a.astype(BF16), b_a


def kernel(x_prompt, x_sample, cache_k, cache_v, state_gla, c, c_ctx, w_mod, b_mod, norm_g, ffn_w_in, ffn_w_out,
           w_in, gla_wa2, gla_ba, gla_norm_g, nat_rpb, w_out):
    cvecs = jnp.zeros((SUBLANES, D_MODEL), F32).at[0].set(c_ctx).at[1:1 + DEC_BATCH].set(c)
    mod = _modulation(cvecs, w_mod, b_mod)[:, :N_GROUPS].reshape(DEPTH, N_GROUPS, N_MOD, D_MODEL)

    ffn_w_in_bf = ffn_w_in.astype(BF16)
    ffn_w_out_bf = ffn_w_out.astype(BF16)
    w_out_bf = w_out.astype(BF16)
    cos_t, sin_t = _rope_tables()
    ck = cache_k.reshape(DEC_BATCH, DEPTH, PAST_LEN, NAT_W)
    cv = cache_v.reshape(DEC_BATCH, DEPTH, PAST_LEN, NAT_W)

    w_big = _pack_w_in(w_in)

    x = (x_prompt.reshape(N_CTX, D_MODEL), x_sample.reshape(N_LAT, D_MODEL))
    k_list, v_list, s_list = [], [], []
    for l in range(DEPTH):
        mod_l, ng_l = mod[l], norm_g[l]
        x = _ffn(x, mod_l, ng_l, ffn_w_in_bf, ffn_w_out_bf, layer=l, which=0)

        w_a, b_a = _gate_up_weights(gla_wa2[l], gla_ba[l])
        q, k, v, r, g, qn, kn_ctx, vn_ctx, kn_lat, vn_lat = _project(
            x, mod_l, ng_l, w_big, w_a, b_a, cos_t, sin_t, layer=l)

        s0 = jnp.concatenate([jnp.zeros((BATCH, 2, H_GLA, DK_GLA, DV_GLA), F32), state_gla[:, l]], axis=0)
        o_gla, s_fin = _gla(q, k, v, g, s0)
        o_ctx = _ctx_attention(qn, kn_ctx, vn_ctx)
        o_lat = _nat_attention(qn, kn_lat, vn_lat, ck, cv, nat_rpb[l], l)

        x = _mixer_out(x, o_gla, r, o_ctx, o_lat, mod_l, ng_l, gla_norm_g[l][None, :], w_out_bf[l])
        x = _ffn(x, mod_l, ng_l, ffn_w_in_bf, ffn_w_out_bf, layer=l, which=1, split_out=(l == DEPTH - 1))

        k_list.append(kn_ctx.reshape(BATCH, SEQ, NAT_W))
        v_list.append(vn_ctx.reshape(BATCH, SEQ, NAT_W))
        s_list.append(s_fin[:BATCH])

    y_prompt = x[0].reshape(BATCH, SEQ, D_MODEL)
    y_sample = x[1].reshape(DEC_BATCH, DEC_SEQ, D_MODEL)
    new_k = jnp.stack(k_list, axis=1).reshape(BATCH, DEPTH, SEQ, H_NAT, HD_NAT)
    new_v = jnp.stack(v_list, axis=1).reshape(BATCH, DEPTH, SEQ, H_NAT, HD_NAT)
    return (y_prompt, y_sample, new_k, new_v, jnp.stack(s_list, axis=1))
```

```python
import functools

import numpy as np
import jax
import jax.numpy as jnp
from jax import lax
from jax.experimental import pallas as pl
from jax.experimental.pallas import tpu as pltpu

D_MODEL = 1024
BATCH = 16
SEQ = 256
DEPTH = 2
DEC_BATCH = 2
DEC_SEQ = 1024
PAST_LEN = 512
GRID_W = 64
H_GLA = 4
DK_GLA = 64
DV_GLA = 128
GLA_RANK = 16
GATE_NORM = 16.0
H_NAT = 8
HD_NAT = 64
WIN_H = 8
WIN_W = 16
D_FF = 2816
N_MOD = 9
ROPE_BASE = 10000.0
EPS = 1e-6
NEG_INF = -1e30

F32 = jnp.float32
BF16 = jnp.bfloat16

N_CTX = BATCH * SEQ
N_LAT = DEC_BATCH * DEC_SEQ
N_ALL = N_CTX + N_LAT
N_GROUPS = 1 + DEC_BATCH
QK_W = H_GLA * DK_GLA
V_W = H_GLA * DV_GLA
NAT_W = H_NAT * HD_NAT
GRID_ROWS = DEC_SEQ // GRID_W

LANES = 128
SUBLANES = 8

TM = 512
FFN_TC = 256
FFN_NC = D_FF // FFN_TC
FFN_STAGES = 2
MOD_TN = 1152
GLA_T = 256
GLA_CB = 16
GLA_NB = GLA_T // GLA_CB
VMEM_LIMIT = 56 * 1024 * 1024

assert N_CTX % TM == 0 and DEC_SEQ % TM == 0 and D_FF % FFN_TC == 0 and FFN_TC % LANES == 0
assert SEQ == GLA_T and DEC_SEQ % GLA_T == 0


def _group_of_tile(i):
    return jnp.where(i < N_CTX // TM, 0, 1 + (i - N_CTX // TM) // (DEC_SEQ // TM))


def _dot(a, b):
    return jnp.dot(a, b, preferred_element_type=F32)


def _dot_nt(a, b):
    return lax.dot_general(a, b, (((1,), (1,)), ((), ())), preferred_element_type=F32)


def _dot_tn(a, b):
    return lax.dot_general(a, b, (((0,), (0,)), ((), ())), preferred_element_type=F32)


def _rms(x, g):
    ms = jnp.mean(x * x, axis=-1, keepdims=True)
    return x * lax.rsqrt(ms + EPS) * g


def _silu(x):
    return x * jax.nn.sigmoid(x)


def _mod_kernel(c_ref, w_ref, b_ref, o_ref):
    s = _silu(c_ref[...]).astype(BF16)
    o_ref[0] = _dot(s, w_ref[0].astype(BF16)) + b_ref[0]


def _modulation(cvecs, w_mod, b_mod):
    n_out = N_MOD * D_MODEL
    return pl.pallas_call(
        _mod_kernel,
        grid=(DEPTH, n_out // MOD_TN),
        in_specs=[
            pl.BlockSpec((SUBLANES, D_MODEL), lambda l, j: (0, 0)),
            pl.BlockSpec((1, D_MODEL, MOD_TN), lambda l, j: (l, 0, j)),
            pl.BlockSpec((1, 1, MOD_TN), lambda l, j: (l, 0, j)),
        ],
        out_specs=pl.BlockSpec((1, SUBLANES, MOD_TN), lambda l, j: (l, 0, j)),
        out_shape=jax.ShapeDtypeStruct((DEPTH, SUBLANES, n_out), F32),
        compiler_params=pltpu.CompilerParams(
            dimension_semantics=("arbitrary", "arbitrary"), vmem_limit_bytes=VMEM_LIMIT),
        name="modulation",
    )(cvecs, w_mod, b_mod.reshape(DEPTH, 1, n_out))


_CTX_TILES = N_CTX // TM
_LAT_TILES = N_LAT // TM


def _ctx_tile(i):
    return jnp.minimum(i, _CTX_TILES - 1)


def _lat_tile(i):
    return jnp.maximum(i - _CTX_TILES, 0)


def _read_split(i, ctx_ref, lat_ref):
    return jnp.where(i < _CTX_TILES, ctx_ref[...], lat_ref[...])


def _ffn_kernel(*refs, m0, n0, layer, which, split_in, split_out):
    n_x = 2 if split_in else 1
    x_refs, (mod_ref, ng_ref, win_hbm, wout_hbm) = refs[:n_x], refs[n_x:n_x + 4]
    n_o = 2 if split_out else 1
    o_refs = refs[n_x + 4:n_x + 4 + n_o]
    h_scr, acc_scr, x_scr, wg_bf, wu_bf, wo_bf, stage_g, stage_u, stage_o, sem = refs[n_x + 4 + n_o:]
    i = pl.program_id(0)

    def chunk_copies(c, slot):
        cols = pl.ds(c * FFN_TC, FFN_TC)
        up_cols = pl.ds(D_FF + c * FFN_TC, FFN_TC)
        return (
            pltpu.make_async_copy(win_hbm.at[layer, which, :, cols], stage_g.at[slot], sem.at[0, slot]),
            pltpu.make_async_copy(win_hbm.at[layer, which, :, up_cols], stage_u.at[slot], sem.at[1, slot]),
            pltpu.make_async_copy(wout_hbm.at[layer, which, cols, :], stage_o.at[slot], sem.at[2, slot]),
        )

    def accumulate(c):
        h = h_scr[...]
        gate = _dot(h, wg_bf[c])
        up = _dot(h, wu_bf[c])
        part = _dot((_silu(gate) * up).astype(BF16), wo_bf[c])
        if c == 0:
            acc_scr[...] = part
        else:
            acc_scr[...] += part

    x = _read_split(i, *x_refs) if split_in else x_refs[0][...]
    x_scr[...] = x
    gain = ng_ref[n0:n0 + 1, :] * (1.0 + mod_ref[0, m0 + 1:m0 + 2, :])
    h_scr[...] = (_rms(x, gain) + mod_ref[0, m0:m0 + 1, :]).astype(BF16)

    @pl.when(i == 0)
    def _():
        for c in range(min(FFN_STAGES, FFN_NC)):
            for cp in chunk_copies(c, c):
                cp.start()
        for c in range(FFN_NC):
            slot = c % FFN_STAGES
            for cp in chunk_copies(c, slot):
                cp.wait()
            wg_bf[c] = stage_g[slot].astype(BF16)
            wu_bf[c] = stage_u[slot].astype(BF16)
            wo_bf[c] = stage_o[slot].astype(BF16)
            if c + FFN_STAGES < FFN_NC:
                for cp in chunk_copies(c + FFN_STAGES, slot):
                    cp.start()
            accumulate(c)

    @pl.when(i > 0)
    def _():
        for c in range(FFN_NC):
            accumulate(c)

    gain = 0.5 * mod_ref[0, m0 + 2:m0 + 3, :] * ng_ref[n0 + 1:n0 + 2, :]
    out = x_scr[...] + _rms(acc_scr[...], gain)
    if split_out:
        @pl.when(i < _CTX_TILES)
        def _():
            o_refs[0][...] = out

        @pl.when(i >= _CTX_TILES)
        def _():
            o_refs[1][...] = out
    else:
        o_refs[0][...] = out


def _ffn(x, mod_l, ng_l, w_in, w_out, *, layer, which, split_out=False):
    m0, n0 = (0, 0) if which == 0 else (6, 4)
    split_in = isinstance(x, tuple)
    tile = pl.BlockSpec((TM, D_MODEL), lambda i: (i, 0))
    ctx_tile = pl.BlockSpec((TM, D_MODEL), lambda i: (_ctx_tile(i), 0))
    lat_tile = pl.BlockSpec((TM, D_MODEL), lambda i: (_lat_tile(i), 0))
    if split_out:
        out_specs = [ctx_tile, lat_tile]
        out_shape = [jax.ShapeDtypeStruct((N_CTX, D_MODEL), F32), jax.ShapeDtypeStruct((N_LAT, D_MODEL), F32)]
    else:
        out_specs = tile
        out_shape = jax.ShapeDtypeStruct((N_ALL, D_MODEL), F32)
    return pl.pallas_call(
        functools.partial(_ffn_kernel, m0=m0, n0=n0, layer=layer, which=which, split_in=split_in,
                          split_out=split_out),
        grid=(N_ALL // TM,),
        in_specs=([ctx_tile, lat_tile] if split_in else [tile]) + [
            pl.BlockSpec((1, N_MOD, D_MODEL), lambda i: (_group_of_tile(i), 0, 0)),
            pl.BlockSpec((6, D_MODEL), lambda i: (0, 0)),
            pl.BlockSpec(memory_space=pl.ANY),
            pl.BlockSpec(memory_space=pl.ANY),
        ],
        out_specs=out_specs,
        out_shape=out_shape,
        scratch_shapes=[
            pltpu.VMEM((TM, D_MODEL), BF16),
            pltpu.VMEM((TM, D_MODEL), F32),
            pltpu.VMEM((TM, D_MODEL), F32),
            pltpu.VMEM((FFN_NC, D_MODEL, FFN_TC), BF16),
            pltpu.VMEM((FFN_NC, D_MODEL, FFN_TC), BF16),
            pltpu.VMEM((FFN_NC, FFN_TC, D_MODEL), BF16),
            pltpu.VMEM((FFN_STAGES, D_MODEL, FFN_TC), F32),
            pltpu.VMEM((FFN_STAGES, D_MODEL, FFN_TC), F32),
            pltpu.VMEM((FFN_STAGES, FFN_TC, D_MODEL), F32),
            pltpu.SemaphoreType.DMA((3, FFN_STAGES)),
        ],
        compiler_params=pltpu.CompilerParams(
            dimension_semantics=("arbitrary",), vmem_limit_bytes=VMEM_LIMIT),
        name="ffn",
    )(*(x if split_in else (x,)), mod_l, ng_l, w_in, w_out)


def _block_transpose_perm():
    r = lax.broadcasted_iota(jnp.int32, (GLA_T, GLA_T), 0)
    c = lax.broadcasted_iota(jnp.int32, (GLA_T, GLA_T), 1)
    return jnp.where(c == (r & (GLA_CB - 1)) * GLA_NB + (r >> 4), 1.0, 0.0).astype(BF16)


def _permute_chunks(perm, a):
    return jnp.concatenate([_dot(perm, a[c * GLA_T:(c + 1) * GLA_T, :]).astype(BF16)
                            for c in range(a.shape[0] // GLA_T)], axis=0)


_P_QK = 0
_P_VR = 4 * QK_W
_P_NAT = _P_VR + 2 * V_W
_P_LR = _P_NAT + 3 * NAT_W
_P_END = _P_LR + LANES


def _proj_kernel(x_ref, mod_ref, ng_ref, w_ref, wa_ref, ba_ref, cos_ref, sin_ref,
                 q_ref, k_ref, v_ref, r_ref, g_ref, qn_ref, kn_ctx_ref, vn_ctx_ref, kn_lat_ref, vn_lat_ref):
    i = pl.program_id(0)
    h = _rms(x_ref[...], ng_ref[2:3, :])
    h = (h * (1.0 + mod_ref[0, 4:5, :]) + mod_ref[0, 3:4, :]).astype(BF16)
    hp = _permute_chunks(_block_transpose_perm(), h)

    qk = _dot(hp, w_ref[0, :, _P_QK:_P_VR])
    cos = cos_ref[...]
    sin = sin_ref[...]
    q_ref[...] = (qk[:, 0:QK_W] * cos + qk[:, 2 * QK_W:3 * QK_W] * sin) * (DK_GLA ** -0.5)
    k_ref[...] = qk[:, QK_W:2 * QK_W] * cos + qk[:, 3 * QK_W:4 * QK_W] * sin

    vr = _dot(hp, w_ref[0, :, _P_VR:_P_NAT])
    v_ref[...] = vr[:, 0:V_W]
    r_ref[...] = vr[:, V_W:2 * V_W]

    nat = _dot(h, w_ref[0, :, _P_NAT:_P_LR])
    qn_ref[...] = nat[:, 0:NAT_W]

    @pl.when(i < _CTX_TILES)
    def _():
        kn_ctx_ref[...] = nat[:, NAT_W:2 * NAT_W]
        vn_ctx_ref[...] = nat[:, 2 * NAT_W:3 * NAT_W]

    @pl.when(i >= _CTX_TILES)
    def _():
        kn_lat_ref[...] = nat[:, NAT_W:2 * NAT_W]
        vn_lat_ref[...] = nat[:, 2 * NAT_W:3 * NAT_W]

    lr = _dot(hp, w_ref[0, :, _P_LR:_P_END]).astype(BF16)
    z = _dot(lr, wa_ref[...]) + ba_ref[...]
    g = (jnp.minimum(z, 0.0) - jnp.log1p(jnp.exp(-jnp.abs(z)))) * (1.0 / GATE_NORM)
    g_ref[0] = g[:, 0:QK_W]
    g_ref[1] = g[:, QK_W:2 * QK_W]


def _rope_table_block(i):
    lat_tiles = DEC_SEQ // TM
    return jnp.where(i < N_CTX // TM, lat_tiles, (i - N_CTX // TM) % lat_tiles)


def _project(x, mod_l, ng_l, w_big, w_a, b_a, cos_t, sin_t, *, layer):
    tile = lambda w: pl.BlockSpec((TM, w), lambda i: (i, 0))
    ctx_tile = pl.BlockSpec((TM, NAT_W), lambda i: (_ctx_tile(i), 0))
    lat_tile = pl.BlockSpec((TM, NAT_W), lambda i: (_lat_tile(i), 0))
    full = lambda a: pl.BlockSpec(a.shape, lambda i: (0,) * a.ndim)
    return pl.pallas_call(
        _proj_kernel,
        grid=(N_ALL // TM,),
        in_specs=[
            tile(D_MODEL),
            pl.BlockSpec((1, N_MOD, D_MODEL), lambda i: (_group_of_tile(i), 0, 0)),
            full(ng_l),
            pl.BlockSpec((1, D_MODEL, _P_END), lambda i: (layer, 0, 0)),
            full(w_a), full(b_a),
            pl.BlockSpec((TM, QK_W), lambda i: (_rope_table_block(i), 0)),
            pl.BlockSpec((TM, QK_W), lambda i: (_rope_table_block(i), 0)),
        ],
        out_specs=[
            tile(QK_W), tile(QK_W), tile(V_W), tile(V_W),
            pl.BlockSpec((2, TM, QK_W), lambda i: (0, i, 0)),
            tile(NAT_W), ctx_tile, ctx_tile, lat_tile, lat_tile,
        ],
        out_shape=[
            jax.ShapeDtypeStruct((N_ALL, QK_W), F32), jax.ShapeDtypeStruct((N_ALL, QK_W), F32),
            jax.ShapeDtypeStruct((N_ALL, V_W), F32), jax.ShapeDtypeStruct((N_ALL, V_W), F32),
            jax.ShapeDtypeStruct((2, N_ALL, QK_W), F32),
            jax.ShapeDtypeStruct((N_ALL, NAT_W), F32),
            jax.ShapeDtypeStruct((N_CTX, NAT_W), F32), jax.ShapeDtypeStruct((N_CTX, NAT_W), F32),
            jax.ShapeDtypeStruct((N_LAT, NAT_W), F32), jax.ShapeDtypeStruct((N_LAT, NAT_W), F32),
        ],
        compiler_params=pltpu.CompilerParams(
            dimension_semantics=("arbitrary",), vmem_limit_bytes=VMEM_LIMIT),
        name="mixer_proj",
    )(x, mod_l, ng_l, w_big, w_a, b_a, cos_t, sin_t)


def _gla_tables():
    rows = []
    seq_specs = [(b * (SEQ // GLA_T), SEQ // GLA_T) for b in range(BATCH)]
    seq_specs += [(N_CTX // GLA_T + b * (DEC_SEQ // GLA_T), DEC_SEQ // GLA_T) for b in range(DEC_BATCH)]
    for sid, (blk0, nchunk) in enumerate(seq_specs):
        for direction in (0, 1):
            order = range(nchunk) if direction == 0 else range(nchunk - 1, -1, -1)
            for n, c in enumerate(order):
                rows.append((blk0 + c, direction, int(n == 0), sid))
    return np.asarray(rows, dtype=np.int32).T.copy()


_GLA_TAB = _gla_tables()
_GLA_ITEMS = _GLA_TAB.shape[1]
_GLA_NSEQ = BATCH + DEC_BATCH
_GLA_PAIR_ROWS = GLA_NB * GLA_CB * (GLA_CB + 1) // 2


def _gla_item(direction, q_ref, k_ref, v_ref, g_ref, o_ref, s_scr, cp, sstk, p_scr, w_scr):
    T, CB, NB = GLA_T, GLA_CB, GLA_NB
    fwd = direction == 0
    slab = lambda i: slice(i * NB, (i + 1) * NB)

    cum = None
    for i in (range(CB) if fwd else range(CB - 1, -1, -1)):
        gi = g_ref[0, slab(i), :]
        cum = gi if cum is None else cum + gi
        cp[slab(i), :] = cum
    total = cum
    cum_all = cp[...]
    qt = q_ref[...] * jnp.exp(cum_all)
    kh = k_ref[...] * jnp.exp(jnp.concatenate([total] * CB, axis=0) - cum_all)
    dec_t = jnp.concatenate([jnp.exp(total), jnp.zeros((LANES - NB, QK_W), F32)], axis=0).T

    erow = lax.broadcasted_iota(jnp.int32, (QK_W, QK_W), 0)
    ecol = lax.broadcasted_iota(jnp.int32, (QK_W, QK_W), 1)
    head_sum = jnp.where((erow >> 6) == (ecol >> 6), 1.0, 0.0).astype(BF16)

    key_positions = lambda i: range(i + 1) if fwd else range(i, CB)
    r0 = 0
    for i in range(CB):
        qi = q_ref[slab(i), :]
        ci = cp[slab(i), :]
        for j in key_positions(i):
            e = jnp.exp(ci - cp[slab(j), :])
            p_scr[r0:r0 + NB, :] = (qi * k_ref[slab(j), :] * e).astype(BF16)
            r0 += NB
    w_scr[...] = _dot(p_scr[...], head_sum)
    first_half = lax.broadcasted_iota(jnp.int32, (NB, LANES), 1) < DK_GLA
    r0 = 0
    for i in range(CB):
        acc = None
        for j in key_positions(i):
            spread = []
            for pair in range(H_GLA // 2):
                tile = w_scr[r0:r0 + NB, pair * LANES:(pair + 1) * LANES]
                other = pltpu.roll(tile, DK_GLA, axis=1)
                spread += [jnp.where(first_half, tile, other), jnp.where(first_half, other, tile)]
            term = jnp.concatenate(spread, axis=1) * v_ref[slab(j), :]
            acc = term if acc is None else acc + term
            r0 += NB
        o_ref[0, slab(i), :] = acc

    kht = kh.T.astype(BF16)
    v_bf = v_ref[...].astype(BF16)
    key_blk = lax.broadcasted_iota(jnp.int32, (DK_GLA, T), 1) & (NB - 1)
    row_blk = lax.broadcasted_iota(jnp.int32, (T, LANES), 0) & (NB - 1)
    lane_half = lax.broadcasted_iota(jnp.int32, (T, LANES), 1) >> 6
    order = range(NB) if fwd else range(NB - 1, -1, -1)
    for h in range(H_GLA):
        kh_h = kht[h * DK_GLA:(h + 1) * DK_GLA, :]
        kv = _dot(jnp.concatenate([jnp.where(key_blk == b, kh_h, 0) for b in range(NB)], axis=0),
                  v_bf[:, h * DV_GLA:(h + 1) * DV_GLA])
        s = s_scr[h]
        for b in order:
            sstk[h, b * DK_GLA:(b + 1) * DK_GLA, :] = s.astype(BF16)
            s = dec_t[h * DK_GLA:(h + 1) * DK_GLA, b:b + 1] * s + kv[b * DK_GLA:(b + 1) * DK_GLA, :]
        s_scr[h] = s

        pair_tile = qt[:, (h // 2) * LANES:(h // 2 + 1) * LANES]
        both = jnp.where(lane_half == h % 2, pair_tile, pltpu.roll(pair_tile, DK_GLA, axis=1))
        lhs = jnp.concatenate([jnp.where(row_blk == 2 * j + lane_half, both, 0.0).astype(BF16)
                               for j in range(NB // 2)], axis=1)
        o_ref[0, :, h * DV_GLA:(h + 1) * DV_GLA] += _dot(lhs, sstk[h])


def _gla_kernel(tab_ref, q_ref, k_ref, v_ref, g_ref, s0_ref, o_ref, so_ref, s_scr, cp, sstk, p_scr, w_scr):
    it = pl.program_id(0)

    @pl.when(tab_ref[2, it] == 1)
    def _():
        s_scr[...] = s0_ref[0, 0]

    for direction in (0, 1):
        @pl.when(tab_ref[1, it] == direction)
        def _(direction=direction):
            _gla_item(direction, q_ref, k_ref, v_ref, g_ref, o_ref, s_scr, cp, sstk, p_scr, w_scr)

    so_ref[0, 0] = s_scr[...]


def _gla(q, k, v, g, s0):
    tok = lambda w: pl.BlockSpec((GLA_T, w), lambda it, tab: (tab[0, it], 0))
    state = pl.BlockSpec((1, 1, H_GLA, DK_GLA, DV_GLA), lambda it, tab: (tab[3, it], tab[1, it], 0, 0, 0))
    grid_spec = pltpu.PrefetchScalarGridSpec(
        num_scalar_prefetch=1,
        grid=(_GLA_ITEMS,),
        in_specs=[
            tok(QK_W), tok(QK_W), tok(V_W),
            pl.BlockSpec((1, GLA_T, QK_W), lambda it, tab: (tab[1, it], tab[0, it], 0)),
            state,
        ],
        out_specs=[
            pl.BlockSpec((1, GLA_T, V_W), lambda it, tab: (tab[1, it], tab[0, it], 0)),
            state,
        ],
        scratch_shapes=[
            pltpu.VMEM((H_GLA, DK_GLA, DV_GLA), F32),
            pltpu.VMEM((GLA_T, QK_W), F32),
            pltpu.VMEM((H_GLA, GLA_NB * DK_GLA, DV_GLA), BF16),
            pltpu.VMEM((_GLA_PAIR_ROWS, QK_W), BF16),
            pltpu.VMEM((_GLA_PAIR_ROWS, QK_W), F32),
        ],
    )
    return pl.pallas_call(
        _gla_kernel,
        grid_spec=grid_spec,
        out_shape=[
            jax.ShapeDtypeStruct((2, N_ALL, V_W), F32),
            jax.ShapeDtypeStruct((_GLA_NSEQ, 2, H_GLA, DK_GLA, DV_GLA), F32),
        ],
        compiler_params=pltpu.CompilerParams(
            dimension_semantics=("arbitrary",), vmem_limit_bytes=VMEM_LIMIT),
        name="gla",
    )(jnp.asarray(_GLA_TAB), q, k, v, g, s0)


def _softmax_pv(s_list, v_list):
    m = s_list[0].max(axis=-1, keepdims=True)
    for s in s_list[1:]:
        m = jnp.maximum(m, s.max(axis=-1, keepdims=True))
    num = None
    den = None
    for s, vv in zip(s_list, v_list):
        e = jnp.exp(s - m)
        den = e.sum(axis=-1, keepdims=True) if den is None else den + e.sum(axis=-1, keepdims=True)
        pv = _dot(e.astype(BF16), vv)
        num = pv if num is None else num + pv
    return num / den


def _ctx_attn_kernel(q_ref, k_ref, v_ref, o_ref):
    scale = HD_NAT ** -0.5
    lane = lax.broadcasted_iota(jnp.int32, (SEQ, LANES), 1)
    for t in range(NAT_W // LANES):
        sl = slice(t * LANES, (t + 1) * LANES)
        qt = q_ref[:, sl]
        kt = k_ref[:, sl].astype(BF16)
        vt = v_ref[:, sl].astype(BF16)
        out = jnp.zeros((SEQ, LANES), F32)
        for half in range(LANES // HD_NAT):
            mine = (lane >> 6) == half
            s = _dot_nt(jnp.where(mine, qt, 0.0).astype(BF16), kt) * scale
            out = jnp.where(mine, _softmax_pv([s], [vt]), out)
        o_ref[:, sl] = out


def _ctx_attention(qn, kn, vn):
    spec = pl.BlockSpec((SEQ, NAT_W), lambda b: (b, 0))
    return pl.pallas_call(
        _ctx_attn_kernel,
        grid=(BATCH,),
        in_specs=[spec, spec, spec],
        out_specs=spec,
        out_shape=jax.ShapeDtypeStruct((N_CTX, NAT_W), F32),
        compiler_params=pltpu.CompilerParams(
            dimension_semantics=("arbitrary",), vmem_limit_bytes=VMEM_LIMIT),
        name="ctx_attention",
    )(qn, kn, vn)


_NAT_QROWS = 4
_NAT_GROUPS = GRID_ROWS // _NAT_QROWS
_NAT_KROWS = WIN_H + _NAT_QROWS
_NAT_Q = _NAT_QROWS * GRID_W
_NAT_KEYS = _NAT_KROWS * GRID_W
_NAT_DR = 2 * WIN_H - 1
_NAT_DC = 2 * WIN_W - 1


def _nat_key_row0(r):
    return jnp.clip(r - WIN_H // 2, 0, GRID_ROWS - WIN_H)


def _nat_build_bias(rpb_ref, tz_scr):
    c = lax.broadcasted_iota(jnp.int32, (GRID_W, LANES), 0)
    lane = lax.broadcasted_iota(jnp.int32, (GRID_W, LANES), 1)
    kc = lane & (GRID_W - 1)
    second = (lane >> 6) == 1
    win_start = jnp.clip(c - WIN_W // 2, 0, GRID_W - WIN_W)
    valid = (kc >= win_start) & (kc < win_start + WIN_W)

    def one_row(n, carry):
        dr = n >> 3
        h = n & (H_NAT - 1)
        src = jnp.broadcast_to(rpb_ref[pl.ds(h * _NAT_DR + dr, 1), :], (GRID_W, LANES))
        rolled = pltpu.roll(src, LANES - (WIN_W - 1), axis=1, stride=1, stride_axis=0)
        tz_scr[dr, h] = jnp.where(valid, rolled, NEG_INF)
        return carry

    lax.fori_loop(0, _NAT_DR * H_NAT, one_row, 0)

    def pair_rows(n, carry):
        dr = n >> 3
        h = n & (H_NAT - 1)
        tz_scr[dr, h] = jnp.where(second, tz_scr[dr + 1, h], tz_scr[dr, h])
        return carry

    lax.fori_loop(0, (_NAT_DR - 1) * H_NAT, pair_rows, 0)


def _nat_kernel(rpb_ref, q_ref, k_ref, v_ref, ck_ref, cv_ref, o_ref, tz_scr, bias_scr):
    grp = pl.program_id(1)

    @pl.when((pl.program_id(0) == 0) & (grp == 0))
    def _():
        _nat_build_bias(rpb_ref, tz_scr)

    scale = HD_NAT ** -0.5
    krow0 = jnp.clip(_NAT_QROWS * grp - WIN_H // 2, 0, GRID_ROWS - _NAT_KROWS)
    k0 = pl.multiple_of(krow0 * GRID_W, GRID_W)
    lane_q = lax.broadcasted_iota(jnp.int32, (_NAT_Q, LANES), 1)
    lane_b = lax.broadcasted_iota(jnp.int32, (GRID_W, LANES), 1)
    neg = jnp.full((GRID_W, LANES), NEG_INF, F32)
    for t in range(NAT_W // LANES):
        sl = slice(t * LANES, (t + 1) * LANES)
        qt = q_ref[:, sl]
        kw = k_ref[pl.ds(k0, _NAT_KEYS), sl].astype(BF16)
        vw = v_ref[pl.ds(k0, _NAT_KEYS), sl].astype(BF16)
        ck = ck_ref[0, 0, :, sl].astype(BF16)
        cv = cv_ref[0, 0, :, sl].astype(BF16)
        out = jnp.zeros((_NAT_Q, LANES), F32)
        for half in range(LANES // HD_NAT):
            h = 2 * t + half
            for qr in range(_NAT_QROWS):
                r = _NAT_QROWS * grp + qr
                lo = _nat_key_row0(r)
                for kp in range(_NAT_KROWS // 2):
                    kr = krow0 + 2 * kp
                    tile = tz_scr[jnp.clip(kr - r + (WIN_H - 1), 0, _NAT_DR - 1), h]
                    ok_a = ((kr >= lo) & (kr < lo + WIN_H)).astype(jnp.int32)
                    ok_b = ((kr + 1 >= lo) & (kr + 1 < lo + WIN_H)).astype(jnp.int32)
                    ok = jnp.where(lane_b < GRID_W, ok_a, ok_b) == 1
                    bias_scr[qr * GRID_W:(qr + 1) * GRID_W, kp * LANES:(kp + 1) * LANES] = jnp.where(ok, tile, neg)
            mine = (lane_q >> 6) == half
            qm = jnp.where(mine, qt, 0.0).astype(BF16)
            s_win = _dot_nt(qm, kw) * scale + bias_scr[...]
            s_ctx = _dot_nt(qm, ck) * scale
            out = jnp.where(mine, _softmax_pv([s_win, s_ctx], [vw, cv]), out)
        o_ref[:, sl] = out


def _nat_attention(qn, kn, vn, cache_k, cache_v, rpb_l, layer):
    lat0 = N_CTX // _NAT_Q
    half = jnp.pad(rpb_l.reshape(H_NAT * _NAT_DR, _NAT_DC), ((0, 0), (0, GRID_W - _NAT_DC)))
    rpb_rows = jnp.concatenate([half, half], axis=1)
    return pl.pallas_call(
        _nat_kernel,
        grid=(DEC_BATCH, _NAT_GROUPS),
        in_specs=[
            pl.BlockSpec((H_NAT * _NAT_DR, LANES), lambda b, g: (0, 0)),
            pl.BlockSpec((_NAT_Q, NAT_W), lambda b, g: (lat0 + b * _NAT_GROUPS + g, 0)),
            pl.BlockSpec((DEC_SEQ, NAT_W), lambda b, g: (b, 0)),
            pl.BlockSpec((DEC_SEQ, NAT_W), lambda b, g: (b, 0)),
            pl.BlockSpec((1, 1, PAST_LEN, NAT_W), lambda b, g: (b, layer, 0, 0)),
            pl.BlockSpec((1, 1, PAST_LEN, NAT_W), lambda b, g: (b, layer, 0, 0)),
        ],
        out_specs=pl.BlockSpec((_NAT_Q, NAT_W), lambda b, g: (b * _NAT_GROUPS + g, 0)),
        out_shape=jax.ShapeDtypeStruct((N_LAT, NAT_W), F32),
        scratch_shapes=[
            pltpu.VMEM((_NAT_DR, H_NAT, GRID_W, LANES), F32),
            pltpu.VMEM((_NAT_Q, _NAT_KEYS), F32),
        ],
        compiler_params=pltpu.CompilerParams(
            dimension_semantics=("arbitrary", "arbitrary"), vmem_limit_bytes=VMEM_LIMIT),
        name="nat_attention",
    )(rpb_rows, qn, kn, vn, cache_k, cache_v)


def _out_kernel(x_ref, o_ref, r_ref, on_ctx_ref, on_lat_ref, mod_ref, ng_ref, gng_ref, w_ref, y_ref):
    o_nat = _read_split(pl.program_id(0), on_ctx_ref, on_lat_ref)
    og = o_ref[0] + o_ref[1]
    parts = []
    for h in range(H_GLA):
        parts.append(_rms(og[:, h * DV_GLA:(h + 1) * DV_GLA], gng_ref[...]))
    merged = (jnp.concatenate(parts, axis=1) * _silu(r_ref[...])).astype(BF16)
    merged = _permute_chunks(_block_transpose_perm(), merged)
    y = _dot(merged, w_ref[0:V_W, :]) + _dot(o_nat.astype(BF16), w_ref[V_W:, :])
    y_ref[...] = x_ref[...] + mod_ref[0, 5:6, :] * _rms(y, ng_ref[3:4, :])


def _mixer_out(x, o_gla, r, o_ctx, o_lat, mod_l, ng_l, gng_l, w_out_bf):
    tile = lambda w: pl.BlockSpec((TM, w), lambda i: (i, 0))
    full = lambda a: pl.BlockSpec(a.shape, lambda i: (0,) * a.ndim)
    return pl.pallas_call(
        _out_kernel,
        grid=(N_ALL // TM,),
        in_specs=[
            tile(D_MODEL),
            pl.BlockSpec((2, TM, V_W), lambda i: (0, i, 0)),
            tile(V_W),
            pl.BlockSpec((TM, NAT_W), lambda i: (_ctx_tile(i), 0)),
            pl.BlockSpec((TM, NAT_W), lambda i: (_lat_tile(i), 0)),
            pl.BlockSpec((1, N_MOD, D_MODEL), lambda i: (_group_of_tile(i), 0, 0)),
            full(ng_l), full(gng_l), full(w_out_bf),
        ],
        out_specs=tile(D_MODEL),
        out_shape=jax.ShapeDtypeStruct((N_ALL, D_MODEL), F32),
        compiler_params=pltpu.CompilerParams(
            dimension_semantics=("arbitrary",), vmem_limit_bytes=VMEM_LIMIT),
        name="mixer_out",
    )(x, o_gla, r, o_ctx, o_lat, mod_l, ng_l, gng_l, w_out_bf)


def _rope_tables():
    quarter = DK_GLA // 4
    freqs = ROPE_BASE ** (-jnp.arange(quarter, dtype=F32) / quarter)
    t = jnp.arange(DEC_SEQ)
    ang_r = (t // GRID_W).astype(F32)[:, None] * freqs[None, :]
    ang_c = (t % GRID_W).astype(F32)[:, None] * freqs[None, :]
    cos_h = jnp.concatenate([jnp.cos(ang_r), jnp.cos(ang_r), jnp.cos(ang_c), jnp.cos(ang_c)], axis=1)
    sin_h = jnp.concatenate([-jnp.sin(ang_r), jnp.sin(ang_r), -jnp.sin(ang_c), jnp.sin(ang_c)], axis=1)
    pos_major = lambda a: a.reshape(-1, GLA_NB, GLA_CB, QK_W).transpose(0, 2, 1, 3).reshape(a.shape)
    cos_t = jnp.concatenate([pos_major(jnp.tile(cos_h, (1, H_GLA))), jnp.ones((TM, QK_W), F32)], axis=0)
    sin_t = jnp.concatenate([pos_major(jnp.tile(sin_h, (1, H_GLA))), jnp.zeros((TM, QK_W), F32)], axis=0)
    return cos_t, sin_t


def _swap_perm():
    quarter = DK_GLA // 4
    idx = np.arange(QK_W)
    blk = (idx // quarter) % 4
    return np.where(blk % 2 == 0, idx + quarter, idx - quarter)


def _pack_w_in(w_in):
    sizes = [QK_W, QK_W, V_W, V_W, 2 * GLA_RANK, 3 * NAT_W]
    cuts = np.cumsum([0] + sizes)
    wq, wk, wv, wr, wlr, wnat = [w_in[:, :, cuts[i]:cuts[i + 1]] for i in range(len(sizes))]
    swap = _swap_perm()
    pad = jnp.zeros((DEPTH, D_MODEL, LANES - 2 * GLA_RANK), w_in.dtype)
    return jnp.concatenate([wq, wk, wq[:, :, swap], wk[:, :, swap], wv, wr, wnat, wlr, pad], axis=2).astype(BF16)


def _gate_up_weights(gla_wa2_l, gla_ba_l):
    w_a = jnp.zeros((LANES, 2 * QK_W), F32)
    w_a = w_a.at[0:GLA_RANK, 0:QK_W].set(gla_wa2_l[0]).at[GLA_RANK:2 * GLA_RANK, QK_W:].set(gla_wa2_l[1])
    b_a = jnp.concatenate([gla_ba_l[0], gla_ba_l[1]])[None, :]
    return w_a.astype(BF16), b_a


def kernel(x_prompt, x_sample, cache_k, cache_v, state_gla, c, c_ctx, w_mod, b_mod, norm_g, ffn_w_in, ffn_w_out,
           w_in, gla_wa2, gla_ba, gla_norm_g, nat_rpb, w_out):
    cvecs = jnp.zeros((SUBLANES, D_MODEL), F32).at[0].set(c_ctx).at[1:1 + DEC_BATCH].set(c)
    mod = _modulation(cvecs, w_mod, b_mod)[:, :N_GROUPS].reshape(DEPTH, N_GROUPS, N_MOD, D_MODEL)

    w_out_bf = w_out.astype(BF16)
    cos_t, sin_t = _rope_tables()
    ck = cache_k.reshape(DEC_BATCH, DEPTH, PAST_LEN, NAT_W)
    cv = cache_v.reshape(DEC_BATCH, DEPTH, PAST_LEN, NAT_W)

    w_big = _pack_w_in(w_in)

    x = (x_prompt.reshape(N_CTX, D_MODEL), x_sample.reshape(N_LAT, D_MODEL))
    k_list, v_list, s_list = [], [], []
    for l in range(DEPTH):
        mod_l, ng_l = mod[l], norm_g[l]
        x = _ffn(x, mod_l, ng_l, ffn_w_in, ffn_w_out, layer=l, which=0)

        w_a, b_a = _gate_up_weights(gla_wa2[l], gla_ba[l])
        q, k, v, r, g, qn, kn_ctx, vn_ctx, kn_lat, vn_lat = _project(
            x, mod_l, ng_l, w_big, w_a, b_a, cos_t, sin_t, layer=l)

        s0 = jnp.concatenate([jnp.zeros((BATCH, 2, H_GLA, DK_GLA, DV_GLA), F32), state_gla[:, l]], axis=0)
        o_gla, s_fin = _gla(q, k, v, g, s0)
        o_ctx = _ctx_attention(qn, kn_ctx, vn_ctx)
        o_lat = _nat_attention(qn, kn_lat, vn_lat, ck, cv, nat_rpb[l], l)

        x = _mixer_out(x, o_gla, r, o_ctx, o_lat, mod_l, ng_l, gla_norm_g[l][None, :], w_out_bf[l])
        x = _ffn(x, mod_l, ng_l, ffn_w_in, ffn_w_out, layer=l, which=1, split_out=(l == DEPTH - 1))

        k_list.append(kn_ctx.reshape(BATCH, SEQ, NAT_W))
        v_list.append(vn_ctx.reshape(BATCH, SEQ, NAT_W))
        s_list.append(s_fin[:BATCH])

    y_prompt = x[0].reshape(BATCH, SEQ, D_MODEL)
    y_sample = x[1].reshape(DEC_BATCH, DEC_SEQ, D_MODEL)
    new_k = jnp.stack(k_list, axis=1).reshape(BATCH, DEPTH, SEQ, H_NAT, HD_NAT)
    new_v = jnp.stack(v_list, axis=1).reshape(BATCH, DEPTH, SEQ, H_NAT, HD_NAT)
    return (y_prompt, y_sample, new_k, new_v, jnp.stack(s_list, axis=1))
```

```python
import functools

import numpy as np
import jax
import jax.numpy as jnp
from jax import lax
from jax.experimental import pallas as pl
from jax.experimental.pallas import tpu as pltpu

D_MODEL = 1024
BATCH = 16
SEQ = 256
DEPTH = 2
DEC_BATCH = 2
DEC_SEQ = 1024
PAST_LEN = 512
GRID_W = 64
H_GLA = 4
DK_GLA = 64
DV_GLA = 128
GLA_RANK = 16
GATE_NORM = 16.0
H_NAT = 8
HD_NAT = 64
WIN_H = 8
WIN_W = 16
D_FF = 2816
N_MOD = 9
ROPE_BASE = 10000.0
EPS = 1e-6
NEG_INF = -1e30

F32 = jnp.float32
BF16 = jnp.bfloat16

N_CTX = BATCH * SEQ
N_LAT = DEC_BATCH * DEC_SEQ
N_ALL = N_CTX + N_LAT
N_GROUPS = 1 + DEC_BATCH
QK_W = H_GLA * DK_GLA
V_W = H_GLA * DV_GLA
NAT_W = H_NAT * HD_NAT
GRID_ROWS = DEC_SEQ // GRID_W

LANES = 128
SUBLANES = 8

TM = 512
FFN_TC = 256
FFN_NC = D_FF // FFN_TC
FFN_STAGES = 2
MOD_TN = 1152
GLA_T = 256
GLA_CB = 16
GLA_NB = GLA_T // GLA_CB
VMEM_LIMIT = 56 * 1024 * 1024

assert N_CTX % TM == 0 and DEC_SEQ % TM == 0 and D_FF % FFN_TC == 0 and FFN_TC % LANES == 0
assert SEQ == GLA_T and DEC_SEQ % GLA_T == 0


def _group_of_tile(i):
    return jnp.where(i < N_CTX // TM, 0, 1 + (i - N_CTX // TM) // (DEC_SEQ // TM))


def _dot(a, b):
    return jnp.dot(a, b, preferred_element_type=F32)


def _dot_nt(a, b):
    return lax.dot_general(a, b, (((1,), (1,)), ((), ())), preferred_element_type=F32)


def _dot_tn(a, b):
    return lax.dot_general(a, b, (((0,), (0,)), ((), ())), preferred_element_type=F32)


def _rms(x, g):
    ms = jnp.mean(x * x, axis=-1, keepdims=True)
    return x * lax.rsqrt(ms + EPS) * g


def _silu(x):
    return x * jax.nn.sigmoid(x)


def _mod_kernel(c_ref, w_ref, b_ref, o_ref):
    s = _silu(c_ref[...]).astype(BF16)
    o_ref[0] = _dot(s, w_ref[0].astype(BF16)) + b_ref[0]


def _modulation(cvecs, w_mod, b_mod):
    n_out = N_MOD * D_MODEL
    return pl.pallas_call(
        _mod_kernel,
        grid=(DEPTH, n_out // MOD_TN),
        in_specs=[
            pl.BlockSpec((SUBLANES, D_MODEL), lambda l, j: (0, 0)),
            pl.BlockSpec((1, D_MODEL, MOD_TN), lambda l, j: (l, 0, j)),
            pl.BlockSpec((1, 1, MOD_TN), lambda l, j: (l, 0, j)),
        ],
        out_specs=pl.BlockSpec((1, SUBLANES, MOD_TN), lambda l, j: (l, 0, j)),
        out_shape=jax.ShapeDtypeStruct((DEPTH, SUBLANES, n_out), F32),
        compiler_params=pltpu.CompilerParams(
            dimension_semantics=("arbitrary", "arbitrary"), vmem_limit_bytes=VMEM_LIMIT),
        name="modulation",
    )(cvecs, w_mod, b_mod.reshape(DEPTH, 1, n_out))


_CTX_TILES = N_CTX // TM
_LAT_TILES = N_LAT // TM


def _ctx_tile(i):
    return jnp.minimum(i, _CTX_TILES - 1)


def _lat_tile(i):
    return jnp.maximum(i - _CTX_TILES, 0)


def _read_split(i, ctx_ref, lat_ref):
    return jnp.where(i < _CTX_TILES, ctx_ref[...], lat_ref[...])


def _ffn_kernel(*refs, m0, n0, layer, which, split_in, split_out):
    n_x = 2 if split_in else 1
    x_refs, (mod_ref, ng_ref, win_hbm, wout_hbm) = refs[:n_x], refs[n_x:n_x + 4]
    n_o = 2 if split_out else 1
    o_refs = refs[n_x + 4:n_x + 4 + n_o]
    h_scr, acc_scr, x_scr, wg_bf, wu_bf, wo_bf, stage_g, stage_u, stage_o, sem = refs[n_x + 4 + n_o:]
    i = pl.program_id(0)

    def chunk_copies(c, slot):
        cols = pl.ds(c * FFN_TC, FFN_TC)
        up_cols = pl.ds(D_FF + c * FFN_TC, FFN_TC)
        return (
            pltpu.make_async_copy(win_hbm.at[layer, which, :, cols], stage_g.at[slot], sem.at[0, slot]),
            pltpu.make_async_copy(win_hbm.at[layer, which, :, up_cols], stage_u.at[slot], sem.at[1, slot]),
            pltpu.make_async_copy(wout_hbm.at[layer, which, cols, :], stage_o.at[slot], sem.at[2, slot]),
        )

    def accumulate(c):
        h = h_scr[...]
        gate = _dot(h, wg_bf[c])
        up = _dot(h, wu_bf[c])
        part = _dot((_silu(gate) * up).astype(BF16), wo_bf[c])
        if c == 0:
            acc_scr[...] = part
        else:
            acc_scr[...] += part

    x = _read_split(i, *x_refs) if split_in else x_refs[0][...]
    x_scr[...] = x
    gain = ng_ref[n0:n0 + 1, :] * (1.0 + mod_ref[0, m0 + 1:m0 + 2, :])
    h_scr[...] = (_rms(x, gain) + mod_ref[0, m0:m0 + 1, :]).astype(BF16)

    @pl.when(i == 0)
    def _():
        for c in range(min(FFN_STAGES, FFN_NC)):
            for cp in chunk_copies(c, c):
                cp.start()
        for c in range(FFN_NC):
            slot = c % FFN_STAGES
            for cp in chunk_copies(c, slot):
                cp.wait()
            wg_bf[c] = stage_g[slot].astype(BF16)
            wu_bf[c] = stage_u[slot].astype(BF16)
            wo_bf[c] = stage_o[slot].astype(BF16)
            if c + FFN_STAGES < FFN_NC:
                for cp in chunk_copies(c + FFN_STAGES, slot):
                    cp.start()
            accumulate(c)

    @pl.when(i > 0)
    def _():
        for c in range(FFN_NC):
            accumulate(c)

    gain = 0.5 * mod_ref[0, m0 + 2:m0 + 3, :] * ng_ref[n0 + 1:n0 + 2, :]
    out = x_scr[...] + _rms(acc_scr[...], gain)
    if split_out:
        @pl.when(i < _CTX_TILES)
        def _():
            o_refs[0][...] = out

        @pl.when(i >= _CTX_TILES)
        def _():
            o_refs[1][...] = out
    else:
        o_refs[0][...] = out


def _ffn(x, mod_l, ng_l, w_in, w_out, *, layer, which, split_out=False):
    m0, n0 = (0, 0) if which == 0 else (6, 4)
    split_in = isinstance(x, tuple)
    tile = pl.BlockSpec((TM, D_MODEL), lambda i: (i, 0))
    ctx_tile = pl.BlockSpec((TM, D_MODEL), lambda i: (_ctx_tile(i), 0))
    lat_tile = pl.BlockSpec((TM, D_MODEL), lambda i: (_lat_tile(i), 0))
    if split_out:
        out_specs = [ctx_tile, lat_tile]
        out_shape = [jax.ShapeDtypeStruct((N_CTX, D_MODEL), F32), jax.ShapeDtypeStruct((N_LAT, D_MODEL), F32)]
    else:
        out_specs = tile
        out_shape = jax.ShapeDtypeStruct((N_ALL, D_MODEL), F32)
    return pl.pallas_call(
        functools.partial(_ffn_kernel, m0=m0, n0=n0, layer=layer, which=which, split_in=split_in,
                          split_out=split_out),
        grid=(N_ALL // TM,),
        in_specs=([ctx_tile, lat_tile] if split_in else [tile]) + [
            pl.BlockSpec((1, N_MOD, D_MODEL), lambda i: (_group_of_tile(i), 0, 0)),
            pl.BlockSpec((6, D_MODEL), lambda i: (0, 0)),
            pl.BlockSpec(memory_space=pl.ANY),
            pl.BlockSpec(memory_space=pl.ANY),
        ],
        out_specs=out_specs,
        out_shape=out_shape,
        scratch_shapes=[
            pltpu.VMEM((TM, D_MODEL), BF16),
            pltpu.VMEM((TM, D_MODEL), F32),
            pltpu.VMEM((TM, D_MODEL), F32),
            pltpu.VMEM((FFN_NC, D_MODEL, FFN_TC), BF16),
            pltpu.VMEM((FFN_NC, D_MODEL, FFN_TC), BF16),
            pltpu.VMEM((FFN_NC, FFN_TC, D_MODEL), BF16),
            pltpu.VMEM((FFN_STAGES, D_MODEL, FFN_TC), F32),
            pltpu.VMEM((FFN_STAGES, D_MODEL, FFN_TC), F32),
            pltpu.VMEM((FFN_STAGES, FFN_TC, D_MODEL), F32),
            pltpu.SemaphoreType.DMA((3, FFN_STAGES)),
        ],
        compiler_params=pltpu.CompilerParams(
            dimension_semantics=("arbitrary",), vmem_limit_bytes=VMEM_LIMIT),
        name="ffn",
    )(*(x if split_in else (x,)), mod_l, ng_l, w_in, w_out)


def _block_transpose_perm():
    r = lax.broadcasted_iota(jnp.int32, (GLA_T, GLA_T), 0)
    c = lax.broadcasted_iota(jnp.int32, (GLA_T, GLA_T), 1)
    return jnp.where(c == (r & (GLA_CB - 1)) * GLA_NB + (r >> 4), 1.0, 0.0).astype(BF16)


def _permute_chunks(perm, a):
    return jnp.concatenate([_dot(perm, a[c * GLA_T:(c + 1) * GLA_T, :]).astype(BF16)
                            for c in range(a.shape[0] // GLA_T)], axis=0)


_P_QK = 0
_P_VR = 4 * QK_W
_P_NAT = _P_VR + 2 * V_W
_P_LR = _P_NAT + 3 * NAT_W
_P_END = _P_LR + LANES


def _proj_kernel(x_ref, mod_ref, ng_ref, w_ref, wa_ref, ba_ref, cos_ref, sin_ref,
                 q_ref, k_ref, v_ref, r_ref, g_ref, qn_ref, kn_ctx_ref, vn_ctx_ref, kn_lat_ref, vn_lat_ref):
    i = pl.program_id(0)
    h = _rms(x_ref[...], ng_ref[2:3, :])
    h = (h * (1.0 + mod_ref[0, 4:5, :]) + mod_ref[0, 3:4, :]).astype(BF16)
    hp = _permute_chunks(_block_transpose_perm(), h)

    qk = _dot(hp, w_ref[0, :, _P_QK:_P_VR])
    cos = cos_ref[...]
    sin = sin_ref[...]
    q_ref[...] = (qk[:, 0:QK_W] * cos + qk[:, 2 * QK_W:3 * QK_W] * sin) * (DK_GLA ** -0.5)
    k_ref[...] = qk[:, QK_W:2 * QK_W] * cos + qk[:, 3 * QK_W:4 * QK_W] * sin

    vr = _dot(hp, w_ref[0, :, _P_VR:_P_NAT])
    v_ref[...] = vr[:, 0:V_W]
    r_ref[...] = vr[:, V_W:2 * V_W]

    nat = _dot(h, w_ref[0, :, _P_NAT:_P_LR])
    qn_ref[...] = nat[:, 0:NAT_W].astype(BF16)

    @pl.when(i < _CTX_TILES)
    def _():
        kn_ctx_ref[...] = nat[:, NAT_W:2 * NAT_W]
        vn_ctx_ref[...] = nat[:, 2 * NAT_W:3 * NAT_W]

    @pl.when(i >= _CTX_TILES)
    def _():
        kn_lat_ref[...] = nat[:, NAT_W:2 * NAT_W].astype(BF16)
        vn_lat_ref[...] = nat[:, 2 * NAT_W:3 * NAT_W].astype(BF16)

    lr = _dot(hp, w_ref[0, :, _P_LR:_P_END]).astype(BF16)
    z = _dot(lr, wa_ref[...]) + ba_ref[...]
    g = (jnp.minimum(z, 0.0) - jnp.log1p(jnp.exp(-jnp.abs(z)))) * (1.0 / GATE_NORM)
    g_ref[0] = g[:, 0:QK_W]
    g_ref[1] = g[:, QK_W:2 * QK_W]


def _rope_table_block(i):
    lat_tiles = DEC_SEQ // TM
    return jnp.where(i < N_CTX // TM, lat_tiles, (i - N_CTX // TM) % lat_tiles)


def _project(x, mod_l, ng_l, w_big, w_a, b_a, cos_t, sin_t, *, layer):
    tile = lambda w: pl.BlockSpec((TM, w), lambda i: (i, 0))
    ctx_tile = pl.BlockSpec((TM, NAT_W), lambda i: (_ctx_tile(i), 0))
    lat_tile = pl.BlockSpec((TM, NAT_W), lambda i: (_lat_tile(i), 0))
    full = lambda a: pl.BlockSpec(a.shape, lambda i: (0,) * a.ndim)
    return pl.pallas_call(
        _proj_kernel,
        grid=(N_ALL // TM,),
        in_specs=[
            tile(D_MODEL),
            pl.BlockSpec((1, N_MOD, D_MODEL), lambda i: (_group_of_tile(i), 0, 0)),
            full(ng_l),
            pl.BlockSpec((1, D_MODEL, _P_END), lambda i: (layer, 0, 0)),
            full(w_a), full(b_a),
            pl.BlockSpec((TM, QK_W), lambda i: (_rope_table_block(i), 0)),
            pl.BlockSpec((TM, QK_W), lambda i: (_rope_table_block(i), 0)),
        ],
        out_specs=[
            tile(QK_W), tile(QK_W), tile(V_W), tile(V_W),
            pl.BlockSpec((2, TM, QK_W), lambda i: (0, i, 0)),
            tile(NAT_W), ctx_tile, ctx_tile, lat_tile, lat_tile,
        ],
        out_shape=[
            jax.ShapeDtypeStruct((N_ALL, QK_W), F32), jax.ShapeDtypeStruct((N_ALL, QK_W), F32),
            jax.ShapeDtypeStruct((N_ALL, V_W), F32), jax.ShapeDtypeStruct((N_ALL, V_W), F32),
            jax.ShapeDtypeStruct((2, N_ALL, QK_W), F32),
            jax.ShapeDtypeStruct((N_ALL, NAT_W), BF16),
            jax.ShapeDtypeStruct((N_CTX, NAT_W), F32), jax.ShapeDtypeStruct((N_CTX, NAT_W), F32),
            jax.ShapeDtypeStruct((N_LAT, NAT_W), BF16), jax.ShapeDtypeStruct((N_LAT, NAT_W), BF16),
        ],
        compiler_params=pltpu.CompilerParams(
            dimension_semantics=("arbitrary",), vmem_limit_bytes=VMEM_LIMIT),
        name="mixer_proj",
    )(x, mod_l, ng_l, w_big, w_a, b_a, cos_t, sin_t)


def _gla_tables():
    rows = []
    seq_specs = [(b * (SEQ // GLA_T), SEQ // GLA_T) for b in range(BATCH)]
    seq_specs += [(N_CTX // GLA_T + b * (DEC_SEQ // GLA_T), DEC_SEQ // GLA_T) for b in range(DEC_BATCH)]
    for sid, (blk0, nchunk) in enumerate(seq_specs):
        for direction in (0, 1):
            order = range(nchunk) if direction == 0 else range(nchunk - 1, -1, -1)
            for n, c in enumerate(order):
                out_blk = blk0 + (c if direction == 1 or nchunk == 1 else nchunk - 1)
                rows.append((blk0 + c, direction, int(n == 0), sid, out_blk, c if nchunk > 1 else -1))
    return np.asarray(rows, dtype=np.int32).T.copy()


_GLA_TAB = _gla_tables()
_GLA_ITEMS = _GLA_TAB.shape[1]
_GLA_NSEQ = BATCH + DEC_BATCH
_GLA_PAIR_ROWS = GLA_NB * GLA_CB * (GLA_CB + 1) // 2


def _gla_item(direction, q_ref, k_ref, v_ref, g_ref, res, s_scr, cp, sstk, p_scr, w_scr):
    T, CB, NB = GLA_T, GLA_CB, GLA_NB
    fwd = direction == 0
    slab = lambda i: slice(i * NB, (i + 1) * NB)

    cum = None
    for i in (range(CB) if fwd else range(CB - 1, -1, -1)):
        gi = g_ref[0, slab(i), :]
        cum = gi if cum is None else cum + gi
        cp[slab(i), :] = cum
    total = cum
    cum_all = cp[...]
    qt = q_ref[...] * jnp.exp(cum_all)
    kh = k_ref[...] * jnp.exp(jnp.concatenate([total] * CB, axis=0) - cum_all)
    dec_t = jnp.concatenate([jnp.exp(total), jnp.zeros((LANES - NB, QK_W), F32)], axis=0).T

    erow = lax.broadcasted_iota(jnp.int32, (QK_W, QK_W), 0)
    ecol = lax.broadcasted_iota(jnp.int32, (QK_W, QK_W), 1)
    head_sum = jnp.where((erow >> 6) == (ecol >> 6), 1.0, 0.0).astype(BF16)

    key_positions = lambda i: range(i + 1) if fwd else range(i, CB)
    r0 = 0
    for i in range(CB):
        qi = q_ref[slab(i), :]
        ci = cp[slab(i), :]
        for j in key_positions(i):
            e = jnp.exp(ci - cp[slab(j), :])
            p_scr[r0:r0 + NB, :] = (qi * k_ref[slab(j), :] * e).astype(BF16)
            r0 += NB
    w_scr[...] = _dot(p_scr[...], head_sum)
    first_half = lax.broadcasted_iota(jnp.int32, (NB, LANES), 1) < DK_GLA
    r0 = 0
    for i in range(CB):
        acc = None
        for j in key_positions(i):
            spread = []
            for pair in range(H_GLA // 2):
                tile = w_scr[r0:r0 + NB, pair * LANES:(pair + 1) * LANES]
                other = pltpu.roll(tile, DK_GLA, axis=1)
                spread += [jnp.where(first_half, tile, other), jnp.where(first_half, other, tile)]
            term = jnp.concatenate(spread, axis=1) * v_ref[slab(j), :]
            acc = term if acc is None else acc + term
            r0 += NB
        res[slab(i), :] = acc

    kht = kh.T.astype(BF16)
    v_bf = v_ref[...].astype(BF16)
    key_blk = lax.broadcasted_iota(jnp.int32, (DK_GLA, T), 1) & (NB - 1)
    row_blk = lax.broadcasted_iota(jnp.int32, (T, LANES), 0) & (NB - 1)
    lane_half = lax.broadcasted_iota(jnp.int32, (T, LANES), 1) >> 6
    order = range(NB) if fwd else range(NB - 1, -1, -1)
    for h in range(H_GLA):
        kh_h = kht[h * DK_GLA:(h + 1) * DK_GLA, :]
        kv = _dot(jnp.concatenate([jnp.where(key_blk == b, kh_h, 0) for b in range(NB)], axis=0),
                  v_bf[:, h * DV_GLA:(h + 1) * DV_GLA])
        s = s_scr[h]
        for b in order:
            sstk[h, b * DK_GLA:(b + 1) * DK_GLA, :] = s.astype(BF16)
            s = dec_t[h * DK_GLA:(h + 1) * DK_GLA, b:b + 1] * s + kv[b * DK_GLA:(b + 1) * DK_GLA, :]
        s_scr[h] = s

        pair_tile = qt[:, (h // 2) * LANES:(h // 2 + 1) * LANES]
        both = jnp.where(lane_half == h % 2, pair_tile, pltpu.roll(pair_tile, DK_GLA, axis=1))
        lhs = jnp.concatenate([jnp.where(row_blk == 2 * j + lane_half, both, 0.0).astype(BF16)
                               for j in range(NB // 2)], axis=1)
        res[:, h * DV_GLA:(h + 1) * DV_GLA] += _dot(lhs, sstk[h])


def _gla_kernel(tab_ref, q_ref, k_ref, v_ref, g_ref, s0_ref, o_ref, so_ref,
                s_scr, cp, sstk, p_scr, w_scr, res, held):
    it = pl.program_id(0)
    direction = tab_ref[1, it]
    slot = tab_ref[5, it]

    @pl.when(tab_ref[2, it] == 1)
    def _():
        s_scr[...] = s0_ref[0, 0]

    for d in (0, 1):
        @pl.when(direction == d)
        def _(d=d):
            _gla_item(d, q_ref, k_ref, v_ref, g_ref, res, s_scr, cp, sstk, p_scr, w_scr)

    @pl.when((direction == 0) & (slot < 0))
    def _():
        o_ref[...] = res[...]

    @pl.when((direction == 1) & (slot < 0))
    def _():
        o_ref[...] += res[...]

    @pl.when((direction == 0) & (slot >= 0))
    def _():
        held[jnp.maximum(slot, 0)] = res[...]

    @pl.when((direction == 1) & (slot >= 0))
    def _():
        o_ref[...] = held[jnp.maximum(slot, 0)] + res[...]

    so_ref[0, 0] = s_scr[...]


def _gla(q, k, v, g, s0):
    tok = lambda w: pl.BlockSpec((GLA_T, w), lambda it, tab: (tab[0, it], 0))
    state = pl.BlockSpec((1, 1, H_GLA, DK_GLA, DV_GLA), lambda it, tab: (tab[3, it], tab[1, it], 0, 0, 0))
    grid_spec = pltpu.PrefetchScalarGridSpec(
        num_scalar_prefetch=1,
        grid=(_GLA_ITEMS,),
        in_specs=[
            tok(QK_W), tok(QK_W), tok(V_W),
            pl.BlockSpec((1, GLA_T, QK_W), lambda it, tab: (tab[1, it], tab[0, it], 0)),
            state,
        ],
        out_specs=[
            pl.BlockSpec((GLA_T, V_W), lambda it, tab: (tab[4, it], 0)),
            state,
        ],
        scratch_shapes=[
            pltpu.VMEM((H_GLA, DK_GLA, DV_GLA), F32),
            pltpu.VMEM((GLA_T, QK_W), F32),
            pltpu.VMEM((H_GLA, GLA_NB * DK_GLA, DV_GLA), BF16),
            pltpu.VMEM((_GLA_PAIR_ROWS, QK_W), BF16),
            pltpu.VMEM((_GLA_PAIR_ROWS, QK_W), F32),
            pltpu.VMEM((GLA_T, V_W), F32),
            pltpu.VMEM((DEC_SEQ // GLA_T, GLA_T, V_W), F32),
        ],
    )
    return pl.pallas_call(
        _gla_kernel,
        grid_spec=grid_spec,
        out_shape=[
            jax.ShapeDtypeStruct((N_ALL, V_W), F32),
            jax.ShapeDtypeStruct((_GLA_NSEQ, 2, H_GLA, DK_GLA, DV_GLA), F32),
        ],
        compiler_params=pltpu.CompilerParams(
            dimension_semantics=("arbitrary",), vmem_limit_bytes=VMEM_LIMIT),
        name="gla",
    )(jnp.asarray(_GLA_TAB), q, k, v, g, s0)


def _softmax_pv(s_list, v_list):
    m = s_list[0].max(axis=-1, keepdims=True)
    for s in s_list[1:]:
        m = jnp.maximum(m, s.max(axis=-1, keepdims=True))
    num = None
    den = None
    for s, vv in zip(s_list, v_list):
        e = jnp.exp(s - m)
        den = e.sum(axis=-1, keepdims=True) if den is None else den + e.sum(axis=-1, keepdims=True)
        pv = _dot(e.astype(BF16), vv)
        num = pv if num is None else num + pv
    return num / den


def _ctx_attn_kernel(q_ref, k_ref, v_ref, o_ref):
    scale = HD_NAT ** -0.5
    lane = lax.broadcasted_iota(jnp.int32, (SEQ, LANES), 1)
    for t in range(NAT_W // LANES):
        sl = slice(t * LANES, (t + 1) * LANES)
        qt = q_ref[:, sl]
        kt = k_ref[:, sl].astype(BF16)
        vt = v_ref[:, sl].astype(BF16)
        out = jnp.zeros((SEQ, LANES), F32)
        for half in range(LANES // HD_NAT):
            mine = (lane >> 6) == half
            s = _dot_nt(jnp.where(mine, qt, 0).astype(BF16), kt) * scale
            out = jnp.where(mine, _softmax_pv([s], [vt]), out)
        o_ref[:, sl] = out.astype(o_ref.dtype)


def _ctx_attention(qn, kn, vn):
    spec = pl.BlockSpec((SEQ, NAT_W), lambda b: (b, 0))
    return pl.pallas_call(
        _ctx_attn_kernel,
        grid=(BATCH,),
        in_specs=[spec, spec, spec],
        out_specs=spec,
        out_shape=jax.ShapeDtypeStruct((N_CTX, NAT_W), BF16),
        compiler_params=pltpu.CompilerParams(
            dimension_semantics=("arbitrary",), vmem_limit_bytes=VMEM_LIMIT),
        name="ctx_attention",
    )(qn, kn, vn)


_NAT_QROWS = 4
_NAT_GROUPS = GRID_ROWS // _NAT_QROWS
_NAT_KROWS = WIN_H + _NAT_QROWS
_NAT_Q = _NAT_QROWS * GRID_W
_NAT_KEYS = _NAT_KROWS * GRID_W
_NAT_DR = 2 * WIN_H - 1
_NAT_DC = 2 * WIN_W - 1


def _nat_key_row0(r):
    return jnp.clip(r - WIN_H // 2, 0, GRID_ROWS - WIN_H)


def _nat_build_bias(rpb_ref, tz_scr):
    c = lax.broadcasted_iota(jnp.int32, (GRID_W, LANES), 0)
    lane = lax.broadcasted_iota(jnp.int32, (GRID_W, LANES), 1)
    kc = lane & (GRID_W - 1)
    second = (lane >> 6) == 1
    win_start = jnp.clip(c - WIN_W // 2, 0, GRID_W - WIN_W)
    valid = (kc >= win_start) & (kc < win_start + WIN_W)

    def one_row(n, carry):
        dr = n >> 3
        h = n & (H_NAT - 1)
        src = jnp.broadcast_to(rpb_ref[pl.ds(h * _NAT_DR + dr, 1), :], (GRID_W, LANES))
        rolled = pltpu.roll(src, LANES - (WIN_W - 1), axis=1, stride=1, stride_axis=0)
        tz_scr[dr, h] = jnp.where(valid, rolled, NEG_INF)
        return carry

    lax.fori_loop(0, _NAT_DR * H_NAT, one_row, 0)

    def pair_rows(n, carry):
        dr = n >> 3
        h = n & (H_NAT - 1)
        tz_scr[dr, h] = jnp.where(second, tz_scr[dr + 1, h], tz_scr[dr, h])
        return carry

    lax.fori_loop(0, (_NAT_DR - 1) * H_NAT, pair_rows, 0)


def _nat_kernel(rpb_ref, q_ref, k_ref, v_ref, ck_ref, cv_ref, o_ref, tz_scr, bias_scr):
    grp = pl.program_id(1)

    @pl.when((pl.program_id(0) == 0) & (grp == 0))
    def _():
        _nat_build_bias(rpb_ref, tz_scr)

    scale = HD_NAT ** -0.5
    krow0 = jnp.clip(_NAT_QROWS * grp - WIN_H // 2, 0, GRID_ROWS - _NAT_KROWS)
    k0 = pl.multiple_of(krow0 * GRID_W, GRID_W)
    lane_q = lax.broadcasted_iota(jnp.int32, (_NAT_Q, LANES), 1)
    lane_b = lax.broadcasted_iota(jnp.int32, (GRID_W, LANES), 1)
    neg = jnp.full((GRID_W, LANES), NEG_INF, F32)
    for t in range(NAT_W // LANES):
        sl = slice(t * LANES, (t + 1) * LANES)
        qt = q_ref[:, sl]
        kw = k_ref[pl.ds(k0, _NAT_KEYS), sl].astype(BF16)
        vw = v_ref[pl.ds(k0, _NAT_KEYS), sl].astype(BF16)
        ck = ck_ref[0, 0, :, sl].astype(BF16)
        cv = cv_ref[0, 0, :, sl].astype(BF16)
        out = jnp.zeros((_NAT_Q, LANES), F32)
        for half in range(LANES // HD_NAT):
            h = 2 * t + half
            for qr in range(_NAT_QROWS):
                r = _NAT_QROWS * grp + qr
                lo = _nat_key_row0(r)
                for kp in range(_NAT_KROWS // 2):
                    kr = krow0 + 2 * kp
                    tile = tz_scr[jnp.clip(kr - r + (WIN_H - 1), 0, _NAT_DR - 1), h]
                    ok_a = ((kr >= lo) & (kr < lo + WIN_H)).astype(jnp.int32)
                    ok_b = ((kr + 1 >= lo) & (kr + 1 < lo + WIN_H)).astype(jnp.int32)
                    ok = jnp.where(lane_b < GRID_W, ok_a, ok_b) == 1
                    bias_scr[qr * GRID_W:(qr + 1) * GRID_W, kp * LANES:(kp + 1) * LANES] = jnp.where(ok, tile, neg)
            mine = (lane_q >> 6) == half
            qm = jnp.where(mine, qt, 0).astype(BF16)
            s_win = _dot_nt(qm, kw) * scale + bias_scr[...]
            s_ctx = _dot_nt(qm, ck) * scale
            out = jnp.where(mine, _softmax_pv([s_win, s_ctx], [vw, cv]), out)
        o_ref[:, sl] = out.astype(o_ref.dtype)


def _nat_attention(qn, kn, vn, cache_k, cache_v, rpb_l, layer):
    lat0 = N_CTX // _NAT_Q
    half = jnp.pad(rpb_l.reshape(H_NAT * _NAT_DR, _NAT_DC), ((0, 0), (0, GRID_W - _NAT_DC)))
    rpb_rows = jnp.concatenate([half, half], axis=1)
    return pl.pallas_call(
        _nat_kernel,
        grid=(DEC_BATCH, _NAT_GROUPS),
        in_specs=[
            pl.BlockSpec((H_NAT * _NAT_DR, LANES), lambda b, g: (0, 0)),
            pl.BlockSpec((_NAT_Q, NAT_W), lambda b, g: (lat0 + b * _NAT_GROUPS + g, 0)),
            pl.BlockSpec((DEC_SEQ, NAT_W), lambda b, g: (b, 0)),
            pl.BlockSpec((DEC_SEQ, NAT_W), lambda b, g: (b, 0)),
            pl.BlockSpec((1, 1, PAST_LEN, NAT_W), lambda b, g: (b, layer, 0, 0)),
            pl.BlockSpec((1, 1, PAST_LEN, NAT_W), lambda b, g: (b, layer, 0, 0)),
        ],
        out_specs=pl.BlockSpec((_NAT_Q, NAT_W), lambda b, g: (b * _NAT_GROUPS + g, 0)),
        out_shape=jax.ShapeDtypeStruct((N_LAT, NAT_W), BF16),
        scratch_shapes=[
            pltpu.VMEM((_NAT_DR, H_NAT, GRID_W, LANES), F32),
            pltpu.VMEM((_NAT_Q, _NAT_KEYS), F32),
        ],
        compiler_params=pltpu.CompilerParams(
            dimension_semantics=("arbitrary", "arbitrary"), vmem_limit_bytes=VMEM_LIMIT),
        name="nat_attention",
    )(rpb_rows, qn, kn, vn, cache_k, cache_v)


def _out_kernel(x_ref, o_ref, r_ref, on_ctx_ref, on_lat_ref, mod_ref, ng_ref, gng_ref, w_ref, y_ref):
    o_nat = _read_split(pl.program_id(0), on_ctx_ref, on_lat_ref)
    og = o_ref[...]
    parts = []
    for h in range(H_GLA):
        parts.append(_rms(og[:, h * DV_GLA:(h + 1) * DV_GLA], gng_ref[...]))
    merged = (jnp.concatenate(parts, axis=1) * _silu(r_ref[...])).astype(BF16)
    merged = _permute_chunks(_block_transpose_perm(), merged)
    y = _dot(merged, w_ref[0:V_W, :]) + _dot(o_nat.astype(BF16), w_ref[V_W:, :])
    y_ref[...] = x_ref[...] + mod_ref[0, 5:6, :] * _rms(y, ng_ref[3:4, :])


def _mixer_out(x, o_gla, r, o_ctx, o_lat, mod_l, ng_l, gng_l, w_out_bf):
    tile = lambda w: pl.BlockSpec((TM, w), lambda i: (i, 0))
    full = lambda a: pl.BlockSpec(a.shape, lambda i: (0,) * a.ndim)
    return pl.pallas_call(
        _out_kernel,
        grid=(N_ALL // TM,),
        in_specs=[
            tile(D_MODEL),
            tile(V_W),
            tile(V_W),
            pl.BlockSpec((TM, NAT_W), lambda i: (_ctx_tile(i), 0)),
            pl.BlockSpec((TM, NAT_W), lambda i: (_lat_tile(i), 0)),
            pl.BlockSpec((1, N_MOD, D_MODEL), lambda i: (_group_of_tile(i), 0, 0)),
            full(ng_l), full(gng_l), full(w_out_bf),
        ],
        out_specs=tile(D_MODEL),
        out_shape=jax.ShapeDtypeStruct((N_ALL, D_MODEL), F32),
        compiler_params=pltpu.CompilerParams(
            dimension_semantics=("arbitrary",), vmem_limit_bytes=VMEM_LIMIT),
        name="mixer_out",
    )(x, o_gla, r, o_ctx, o_lat, mod_l, ng_l, gng_l, w_out_bf)


def _rope_tables():
    quarter = DK_GLA // 4
    freqs = ROPE_BASE ** (-jnp.arange(quarter, dtype=F32) / quarter)
    t = jnp.arange(DEC_SEQ)
    ang_r = (t // GRID_W).astype(F32)[:, None] * freqs[None, :]
    ang_c = (t % GRID_W).astype(F32)[:, None] * freqs[None, :]
    cos_h = jnp.concatenate([jnp.cos(ang_r), jnp.cos(ang_r), jnp.cos(ang_c), jnp.cos(ang_c)], axis=1)
    sin_h = jnp.concatenate([-jnp.sin(ang_r), jnp.sin(ang_r), -jnp.sin(ang_c), jnp.sin(ang_c)], axis=1)
    pos_major = lambda a: a.reshape(-1, GLA_NB, GLA_CB, QK_W).transpose(0, 2, 1, 3).reshape(a.shape)
    cos_t = jnp.concatenate([pos_major(jnp.tile(cos_h, (1, H_GLA))), jnp.ones((TM, QK_W), F32)], axis=0)
    sin_t = jnp.concatenate([pos_major(jnp.tile(sin_h, (1, H_GLA))), jnp.zeros((TM, QK_W), F32)], axis=0)
    return cos_t, sin_t


def _swap_perm():
    quarter = DK_GLA // 4
    idx = np.arange(QK_W)
    blk = (idx // quarter) % 4
    return np.where(blk % 2 == 0, idx + quarter, idx - quarter)


def _pack_w_in(w_in):
    sizes = [QK_W, QK_W, V_W, V_W, 2 * GLA_RANK, 3 * NAT_W]
    cuts = np.cumsum([0] + sizes)
    wq, wk, wv, wr, wlr, wnat = [w_in[:, :, cuts[i]:cuts[i + 1]] for i in range(len(sizes))]
    swap = _swap_perm()
    pad = jnp.zeros((DEPTH, D_MODEL, LANES - 2 * GLA_RANK), w_in.dtype)
    return jnp.concatenate([wq, wk, wq[:, :, swap], wk[:, :, swap], wv, wr, wnat, wlr, pad], axis=2).astype(BF16)


def _gate_up_weights(gla_wa2_l, gla_ba_l):
    w_a = jnp.zeros((LANES, 2 * QK_W), F32)
    w_a = w_a.at[0:GLA_RANK, 0:QK_W].set(gla_wa2_l[0]).at[GLA_RANK:2 * GLA_RANK, QK_W:].set(gla_wa2_l[1])
    b_a = jnp.concatenate([gla_ba_l[0], gla_ba_l[1]])[None, :]
    return w_a.astype(BF16), b_a


def kernel(x_prompt, x_sample, cache_k, cache_v, state_gla, c, c_ctx, w_mod, b_mod, norm_g, ffn_w_in, ffn_w_out,
           w_in, gla_wa2, gla_ba, gla_norm_g, nat_rpb, w_out):
    cvecs = jnp.zeros((SUBLANES, D_MODEL), F32).at[0].set(c_ctx).at[1:1 + DEC_BATCH].set(c)
    mod = _modulation(cvecs, w_mod, b_mod)[:, :N_GROUPS].reshape(DEPTH, N_GROUPS, N_MOD, D_MODEL)

    w_out_bf = w_out.astype(BF16)
    cos_t, sin_t = _rope_tables()
    ck = cache_k.reshape(DEC_BATCH, DEPTH, PAST_LEN, NAT_W)
    cv = cache_v.reshape(DEC_BATCH, DEPTH, PAST_LEN, NAT_W)

    w_big = _pack_w_in(w_in)

    x = (x_prompt.reshape(N_CTX, D_MODEL), x_sample.reshape(N_LAT, D_MODEL))
    k_list, v_list, s_list = [], [], []
    for l in range(DEPTH):
        mod_l, ng_l = mod[l], norm_g[l]
        x = _ffn(x, mod_l, ng_l, ffn_w_in, ffn_w_out, layer=l, which=0)

        w_a, b_a = _gate_up_weights(gla_wa2[l], gla_ba[l])
        q, k, v, r, g, qn, kn_ctx, vn_ctx, kn_lat, vn_lat = _project(
            x, mod_l, ng_l, w_big, w_a, b_a, cos_t, sin_t, layer=l)

        s0 = jnp.concatenate([jnp.zeros((BATCH, 2, H_GLA, DK_GLA, DV_GLA), F32), state_gla[:, l]], axis=0)
        o_gla, s_fin = _gla(q, k, v, g, s0)
        o_ctx = _ctx_attention(qn, kn_ctx, vn_ctx)
        o_lat = _nat_attention(qn, kn_lat, vn_lat, ck, cv, nat_rpb[l], l)

        x = _mixer_out(x, o_gla, r, o_ctx, o_lat, mod_l, ng_l, gla_norm_g[l][None, :], w_out_bf[l])
        x = _ffn(x, mod_l, ng_l, ffn_w_in, ffn_w_out, layer=l, which=1, split_out=(l == DEPTH - 1))

        k_list.append(kn_ctx.reshape(BATCH, SEQ, NAT_W))
        v_list.append(vn_ctx.reshape(BATCH, SEQ, NAT_W))
        s_list.append(s_fin[:BATCH])

    y_prompt = x[0].reshape(BATCH, SEQ, D_MODEL)
    y_sample = x[1].reshape(DEC_BATCH, DEC_SEQ, D_MODEL)
    new_k = jnp.stack(k_list, axis=1).reshape(BATCH, DEPTH, SEQ, H_NAT, HD_NAT)
    new_v = jnp.stack(v_list, axis=1).reshape(BATCH, DEPTH, SEQ, H_NAT, HD_NAT)
    return (y_prompt, y_sample, new_k, new_v, jnp.stack(s_list, axis=1))
```

```python
import functools

import numpy as np
import jax
import jax.numpy as jnp
from jax import lax
from jax.experimental import pallas as pl
from jax.experimental.pallas import tpu as pltpu

D_MODEL = 1024
BATCH = 16
SEQ = 256
DEPTH = 2
DEC_BATCH = 2
DEC_SEQ = 1024
PAST_LEN = 512
GRID_W = 64
H_GLA = 4
DK_GLA = 64
DV_GLA = 128
GLA_RANK = 16
GATE_NORM = 16.0
H_NAT = 8
HD_NAT = 64
WIN_H = 8
WIN_W = 16
D_FF = 2816
N_MOD = 9
ROPE_BASE = 10000.0
EPS = 1e-6
NEG_INF = -1e30

F32 = jnp.float32
BF16 = jnp.bfloat16

N_CTX = BATCH * SEQ
N_LAT = DEC_BATCH * DEC_SEQ
N_ALL = N_CTX + N_LAT
N_GROUPS = 1 + DEC_BATCH
QK_W = H_GLA * DK_GLA
V_W = H_GLA * DV_GLA
NAT_W = H_NAT * HD_NAT
GRID_ROWS = DEC_SEQ // GRID_W

LANES = 128
SUBLANES = 8

TM = 512
FFN_TC = 256
FFN_NC = D_FF // FFN_TC
FFN_STAGES = 2
MOD_TN = 1152
GLA_T = 256
GLA_CB = 16
GLA_NB = GLA_T // GLA_CB
VMEM_LIMIT = 56 * 1024 * 1024

assert N_CTX % TM == 0 and DEC_SEQ % TM == 0 and D_FF % FFN_TC == 0 and FFN_TC % LANES == 0
assert SEQ == GLA_T and DEC_SEQ % GLA_T == 0


def _group_of_tile(i):
    return jnp.where(i < N_CTX // TM, 0, 1 + (i - N_CTX // TM) // (DEC_SEQ // TM))


def _dot(a, b):
    return jnp.dot(a, b, preferred_element_type=F32)


def _dot_nt(a, b):
    return lax.dot_general(a, b, (((1,), (1,)), ((), ())), preferred_element_type=F32)


def _dot_tn(a, b):
    return lax.dot_general(a, b, (((0,), (0,)), ((), ())), preferred_element_type=F32)


def _rms(x, g):
    ms = jnp.mean(x * x, axis=-1, keepdims=True)
    return x * lax.rsqrt(ms + EPS) * g


def _silu(x):
    return x * jax.nn.sigmoid(x)


def _mod_kernel(c_ref, w_ref, b_ref, o_ref):
    s = _silu(c_ref[...]).astype(BF16)
    o_ref[0] = _dot(s, w_ref[0].astype(BF16)) + b_ref[0]


def _modulation(cvecs, w_mod, b_mod):
    n_out = N_MOD * D_MODEL
    return pl.pallas_call(
        _mod_kernel,
        grid=(DEPTH, n_out // MOD_TN),
        in_specs=[
            pl.BlockSpec((SUBLANES, D_MODEL), lambda l, j: (0, 0)),
            pl.BlockSpec((1, D_MODEL, MOD_TN), lambda l, j: (l, 0, j)),
            pl.BlockSpec((1, 1, MOD_TN), lambda l, j: (l, 0, j)),
        ],
        out_specs=pl.BlockSpec((1, SUBLANES, MOD_TN), lambda l, j: (l, 0, j)),
        out_shape=jax.ShapeDtypeStruct((DEPTH, SUBLANES, n_out), F32),
        compiler_params=pltpu.CompilerParams(
            dimension_semantics=("arbitrary", "arbitrary"), vmem_limit_bytes=VMEM_LIMIT),
        name="modulation",
    )(cvecs, w_mod, b_mod.reshape(DEPTH, 1, n_out))


_CTX_TILES = N_CTX // TM
_LAT_TILES = N_LAT // TM


def _ctx_tile(i):
    return jnp.minimum(i, _CTX_TILES - 1)


def _lat_tile(i):
    return jnp.maximum(i - _CTX_TILES, 0)


def _read_split(i, ctx_ref, lat_ref):
    return jnp.where(i < _CTX_TILES, ctx_ref[...], lat_ref[...])


def _ffn_kernel(*refs, m0, n0, layer, which, split_in, split_out):
    n_x = 2 if split_in else 1
    x_refs, (mod_ref, ng_ref, win_hbm, wout_hbm) = refs[:n_x], refs[n_x:n_x + 4]
    n_o = 2 if split_out else 1
    o_refs = refs[n_x + 4:n_x + 4 + n_o]
    h_scr, acc_scr, x_scr, wg_bf, wu_bf, wo_bf, stage_g, stage_u, stage_o, sem = refs[n_x + 4 + n_o:]
    i = pl.program_id(0)

    def chunk_copies(c, slot):
        cols = pl.ds(c * FFN_TC, FFN_TC)
        up_cols = pl.ds(D_FF + c * FFN_TC, FFN_TC)
        return (
            pltpu.make_async_copy(win_hbm.at[layer, which, :, cols], stage_g.at[slot], sem.at[0, slot]),
            pltpu.make_async_copy(win_hbm.at[layer, which, :, up_cols], stage_u.at[slot], sem.at[1, slot]),
            pltpu.make_async_copy(wout_hbm.at[layer, which, cols, :], stage_o.at[slot], sem.at[2, slot]),
        )

    def accumulate(c):
        h = h_scr[...]
        gate = _dot(h, wg_bf[c])
        up = _dot(h, wu_bf[c])
        part = _dot((_silu(gate) * up).astype(BF16), wo_bf[c])
        if c == 0:
            acc_scr[...] = part
        else:
            acc_scr[...] += part

    x = _read_split(i, *x_refs) if split_in else x_refs[0][...]
    x_scr[...] = x
    gain = ng_ref[n0:n0 + 1, :] * (1.0 + mod_ref[0, m0 + 1:m0 + 2, :])
    h_scr[...] = (_rms(x, gain) + mod_ref[0, m0:m0 + 1, :]).astype(BF16)

    @pl.when(i == 0)
    def _():
        for c in range(min(FFN_STAGES, FFN_NC)):
            for cp in chunk_copies(c, c):
                cp.start()
        for c in range(FFN_NC):
            slot = c % FFN_STAGES
            for cp in chunk_copies(c, slot):
                cp.wait()
            wg_bf[c] = stage_g[slot].astype(BF16)
            wu_bf[c] = stage_u[slot].astype(BF16)
            wo_bf[c] = stage_o[slot].astype(BF16)
            if c + FFN_STAGES < FFN_NC:
                for cp in chunk_copies(c + FFN_STAGES, slot):
                    cp.start()
            accumulate(c)

    @pl.when(i > 0)
    def _():
        for c in range(FFN_NC):
            accumulate(c)

    gain = 0.5 * mod_ref[0, m0 + 2:m0 + 3, :] * ng_ref[n0 + 1:n0 + 2, :]
    out = x_scr[...] + _rms(acc_scr[...], gain)
    if split_out:
        @pl.when(i < _CTX_TILES)
        def _():
            o_refs[0][...] = out

        @pl.when(i >= _CTX_TILES)
        def _():
            o_refs[1][...] = out
    else:
        o_refs[0][...] = out


def _ffn(x, mod_l, ng_l, w_in, w_out, *, layer, which, split_out=False):
    m0, n0 = (0, 0) if which == 0 else (6, 4)
    split_in = isinstance(x, tuple)
    tile = pl.BlockSpec((TM, D_MODEL), lambda i: (i, 0))
    ctx_tile = pl.BlockSpec((TM, D_MODEL), lambda i: (_ctx_tile(i), 0))
    lat_tile = pl.BlockSpec((TM, D_MODEL), lambda i: (_lat_tile(i), 0))
    if split_out:
        out_specs = [ctx_tile, lat_tile]
        out_shape = [jax.ShapeDtypeStruct((N_CTX, D_MODEL), F32), jax.ShapeDtypeStruct((N_LAT, D_MODEL), F32)]
    else:
        out_specs = tile
        out_shape = jax.ShapeDtypeStruct((N_ALL, D_MODEL), F32)
    return pl.pallas_call(
        functools.partial(_ffn_kernel, m0=m0, n0=n0, layer=layer, which=which, split_in=split_in,
                          split_out=split_out),
        grid=(N_ALL // TM,),
        in_specs=([ctx_tile, lat_tile] if split_in else [tile]) + [
            pl.BlockSpec((1, N_MOD, D_MODEL), lambda i: (_group_of_tile(i), 0, 0)),
            pl.BlockSpec((6, D_MODEL), lambda i: (0, 0)),
            pl.BlockSpec(memory_space=pl.ANY),
            pl.BlockSpec(memory_space=pl.ANY),
        ],
        out_specs=out_specs,
        out_shape=out_shape,
        scratch_shapes=[
            pltpu.VMEM((TM, D_MODEL), BF16),
            pltpu.VMEM((TM, D_MODEL), F32),
            pltpu.VMEM((TM, D_MODEL), F32),
            pltpu.VMEM((FFN_NC, D_MODEL, FFN_TC), BF16),
            pltpu.VMEM((FFN_NC, D_MODEL, FFN_TC), BF16),
            pltpu.VMEM((FFN_NC, FFN_TC, D_MODEL), BF16),
            pltpu.VMEM((FFN_STAGES, D_MODEL, FFN_TC), F32),
            pltpu.VMEM((FFN_STAGES, D_MODEL, FFN_TC), F32),
            pltpu.VMEM((FFN_STAGES, FFN_TC, D_MODEL), F32),
            pltpu.SemaphoreType.DMA((3, FFN_STAGES)),
        ],
        compiler_params=pltpu.CompilerParams(
            dimension_semantics=("arbitrary",), vmem_limit_bytes=VMEM_LIMIT),
        name="ffn",
    )(*(x if split_in else (x,)), mod_l, ng_l, w_in, w_out)


def _block_transpose_perm():
    r = lax.broadcasted_iota(jnp.int32, (GLA_T, GLA_T), 0)
    c = lax.broadcasted_iota(jnp.int32, (GLA_T, GLA_T), 1)
    return jnp.where(c == (r & (GLA_CB - 1)) * GLA_NB + (r >> 4), 1.0, 0.0).astype(BF16)


def _permute_chunks(perm, a):
    return jnp.concatenate([_dot(perm, a[c * GLA_T:(c + 1) * GLA_T, :]).astype(BF16)
                            for c in range(a.shape[0] // GLA_T)], axis=0)


_P_QK = 0
_P_VR = 4 * QK_W
_P_NAT = _P_VR + 2 * V_W
_P_LR = _P_NAT + 3 * NAT_W
_P_END = _P_LR + LANES


def _proj_kernel(x_ref, mod_ref, ng_ref, w_ref, wa_ref, ba_ref, cos_ref, sin_ref,
                 q_ref, k_ref, v_ref, r_ref, g_ref, qn_ref, kn_ctx_ref, vn_ctx_ref, kn_ref, vn_ref):
    i = pl.program_id(0)
    h = _rms(x_ref[...], ng_ref[2:3, :])
    h = (h * (1.0 + mod_ref[0, 4:5, :]) + mod_ref[0, 3:4, :]).astype(BF16)
    hp = _permute_chunks(_block_transpose_perm(), h)

    lr = _dot(hp, w_ref[0, :, _P_LR:_P_END]).astype(BF16)
    z = _dot(lr, wa_ref[...]) + ba_ref[...]
    g = (jnp.minimum(z, 0.0) - jnp.log1p(jnp.exp(-jnp.abs(z)))) * (1.0 / GATE_NORM)
    g_ref[0] = g[:, 0:QK_W]
    g_ref[1] = g[:, QK_W:2 * QK_W]

    qk = _dot(hp, w_ref[0, :, _P_QK:_P_VR])
    cos = cos_ref[...]
    sin = sin_ref[...]
    q_ref[...] = (qk[:, 0:QK_W] * cos + qk[:, 2 * QK_W:3 * QK_W] * sin) * (DK_GLA ** -0.5)
    k_ref[...] = qk[:, QK_W:2 * QK_W] * cos + qk[:, 3 * QK_W:4 * QK_W] * sin

    vr = _dot(hp, w_ref[0, :, _P_VR:_P_NAT])
    v_ref[...] = vr[:, 0:V_W]
    r_ref[...] = vr[:, V_W:2 * V_W]

    nat = _dot(h, w_ref[0, :, _P_NAT:_P_LR])
    qn_ref[...] = nat[:, 0:NAT_W].astype(BF16)
    kn_ref[...] = nat[:, NAT_W:2 * NAT_W].astype(BF16)
    vn_ref[...] = nat[:, 2 * NAT_W:3 * NAT_W].astype(BF16)

    @pl.when(i < _CTX_TILES)
    def _():
        kn_ctx_ref[...] = nat[:, NAT_W:2 * NAT_W].reshape(kn_ctx_ref.shape)
        vn_ctx_ref[...] = nat[:, 2 * NAT_W:3 * NAT_W].reshape(vn_ctx_ref.shape)


def _rope_table_block(i):
    lat_tiles = DEC_SEQ // TM
    return jnp.where(i < N_CTX // TM, lat_tiles, (i - N_CTX // TM) % lat_tiles)


def _project(x, mod_l, ng_l, w_big, w_a, b_a, cos_t, sin_t, *, layer):
    tile = lambda w: pl.BlockSpec((TM, w), lambda i: (i, 0))
    ctx_heads = pl.BlockSpec((TM // SEQ, SEQ, H_NAT, HD_NAT), lambda i: (_ctx_tile(i), 0, 0, 0))
    full = lambda a: pl.BlockSpec(a.shape, lambda i: (0,) * a.ndim)
    return pl.pallas_call(
        _proj_kernel,
        grid=(N_ALL // TM,),
        in_specs=[
            tile(D_MODEL),
            pl.BlockSpec((1, N_MOD, D_MODEL), lambda i: (_group_of_tile(i), 0, 0)),
            full(ng_l),
            pl.BlockSpec((1, D_MODEL, _P_END), lambda i: (layer, 0, 0)),
            full(w_a), full(b_a),
            pl.BlockSpec((TM, QK_W), lambda i: (_rope_table_block(i), 0)),
            pl.BlockSpec((TM, QK_W), lambda i: (_rope_table_block(i), 0)),
        ],
        out_specs=[
            tile(QK_W), tile(QK_W), tile(V_W), tile(V_W),
            pl.BlockSpec((2, TM, QK_W), lambda i: (0, i, 0)),
            tile(NAT_W), ctx_heads, ctx_heads, tile(NAT_W), tile(NAT_W),
        ],
        out_shape=[
            jax.ShapeDtypeStruct((N_ALL, QK_W), F32), jax.ShapeDtypeStruct((N_ALL, QK_W), F32),
            jax.ShapeDtypeStruct((N_ALL, V_W), F32), jax.ShapeDtypeStruct((N_ALL, V_W), F32),
            jax.ShapeDtypeStruct((2, N_ALL, QK_W), F32),
            jax.ShapeDtypeStruct((N_ALL, NAT_W), BF16),
            jax.ShapeDtypeStruct((BATCH, SEQ, H_NAT, HD_NAT), F32),
            jax.ShapeDtypeStruct((BATCH, SEQ, H_NAT, HD_NAT), F32),
            jax.ShapeDtypeStruct((N_ALL, NAT_W), BF16), jax.ShapeDtypeStruct((N_ALL, NAT_W), BF16),
        ],
        compiler_params=pltpu.CompilerParams(
            dimension_semantics=("arbitrary",), vmem_limit_bytes=VMEM_LIMIT),
        name="mixer_proj",
    )(x, mod_l, ng_l, w_big, w_a, b_a, cos_t, sin_t)


def _gla_tables():
    rows = []
    seq_specs = [(b * (SEQ // GLA_T), SEQ // GLA_T) for b in range(BATCH)]
    seq_specs += [(N_CTX // GLA_T + b * (DEC_SEQ // GLA_T), DEC_SEQ // GLA_T) for b in range(DEC_BATCH)]
    for sid, (blk0, nchunk) in enumerate(seq_specs):
        for direction in (0, 1):
            order = range(nchunk) if direction == 0 else range(nchunk - 1, -1, -1)
            for n, c in enumerate(order):
                out_blk = blk0 + (c if direction == 1 or nchunk == 1 else nchunk - 1)
                rows.append((blk0 + c, direction, int(n == 0), sid, out_blk, c if nchunk > 1 else -1))
    return np.asarray(rows, dtype=np.int32).T.copy()


_GLA_TAB = _gla_tables()
_GLA_ITEMS = _GLA_TAB.shape[1]
_GLA_NSEQ = BATCH + DEC_BATCH
_GLA_PAIR_ROWS = GLA_NB * GLA_CB * (GLA_CB + 1) // 2


def _gla_item(direction, q_ref, k_ref, v_ref, g_ref, res, s_scr, cp, sstk, p_scr, w_scr):
    T, CB, NB = GLA_T, GLA_CB, GLA_NB
    fwd = direction == 0
    slab = lambda i: slice(i * NB, (i + 1) * NB)

    cum = None
    for i in (range(CB) if fwd else range(CB - 1, -1, -1)):
        gi = g_ref[0, slab(i), :]
        cum = gi if cum is None else cum + gi
        cp[slab(i), :] = cum
    total = cum
    cum_all = cp[...]
    qt = q_ref[...] * jnp.exp(cum_all)
    kh = k_ref[...] * jnp.exp(jnp.concatenate([total] * CB, axis=0) - cum_all)
    dec_t = jnp.concatenate([jnp.exp(total), jnp.zeros((LANES - NB, QK_W), F32)], axis=0).T

    erow = lax.broadcasted_iota(jnp.int32, (QK_W, QK_W), 0)
    ecol = lax.broadcasted_iota(jnp.int32, (QK_W, QK_W), 1)
    head_sum = jnp.where((erow >> 6) == (ecol >> 6), 1.0, 0.0).astype(BF16)

    key_positions = lambda i: range(i + 1) if fwd else range(i, CB)
    r0 = 0
    for i in range(CB):
        qi = q_ref[slab(i), :]
        ci = cp[slab(i), :]
        for j in key_positions(i):
            e = jnp.exp(ci - cp[slab(j), :])
            p_scr[r0:r0 + NB, :] = (qi * k_ref[slab(j), :] * e).astype(BF16)
            r0 += NB
    w_scr[...] = _dot(p_scr[...], head_sum)
    first_half = lax.broadcasted_iota(jnp.int32, (NB, LANES), 1) < DK_GLA
    r0 = 0
    for i in range(CB):
        acc = None
        for j in key_positions(i):
            spread = []
            for pair in range(H_GLA // 2):
                tile = w_scr[r0:r0 + NB, pair * LANES:(pair + 1) * LANES]
                other = pltpu.roll(tile, DK_GLA, axis=1)
                spread += [jnp.where(first_half, tile, other), jnp.where(first_half, other, tile)]
            term = jnp.concatenate(spread, axis=1) * v_ref[slab(j), :]
            acc = term if acc is None else acc + term
            r0 += NB
        res[slab(i), :] = acc

    kht = kh.T.astype(BF16)
    v_bf = v_ref[...].astype(BF16)
    key_blk = lax.broadcasted_iota(jnp.int32, (DK_GLA, T), 1) & (NB - 1)
    row_blk = lax.broadcasted_iota(jnp.int32, (T, LANES), 0) & (NB - 1)
    lane_half = lax.broadcasted_iota(jnp.int32, (T, LANES), 1) >> 6
    order = range(NB) if fwd else range(NB - 1, -1, -1)
    for h in range(H_GLA):
        kh_h = kht[h * DK_GLA:(h + 1) * DK_GLA, :]
        kv = _dot(jnp.concatenate([jnp.where(key_blk == b, kh_h, 0) for b in range(NB)], axis=0),
                  v_bf[:, h * DV_GLA:(h + 1) * DV_GLA])
        s = s_scr[h]
        for b in order:
            sstk[h, b * DK_GLA:(b + 1) * DK_GLA, :] = s.astype(BF16)
            s = dec_t[h * DK_GLA:(h + 1) * DK_GLA, b:b + 1] * s + kv[b * DK_GLA:(b + 1) * DK_GLA, :]
        s_scr[h] = s

        pair_tile = qt[:, (h // 2) * LANES:(h // 2 + 1) * LANES]
        both = jnp.where(lane_half == h % 2, pair_tile, pltpu.roll(pair_tile, DK_GLA, axis=1))
        lhs = jnp.concatenate([jnp.where(row_blk == 2 * j + lane_half, both, 0.0).astype(BF16)
                               for j in range(NB // 2)], axis=1)
        res[:, h * DV_GLA:(h + 1) * DV_GLA] += _dot(lhs, sstk[h])


def _gla_kernel(tab_ref, q_ref, k_ref, v_ref, g_ref, s0_ref, o_ref, so_ref,
                s_scr, cp, sstk, p_scr, w_scr, res, held):
    it = pl.program_id(0)
    direction = tab_ref[1, it]
    slot = tab_ref[5, it]

    @pl.when(tab_ref[2, it] == 1)
    def _():
        s_scr[...] = s0_ref[0, 0]

    for d in (0, 1):
        @pl.when(direction == d)
        def _(d=d):
            _gla_item(d, q_ref, k_ref, v_ref, g_ref, res, s_scr, cp, sstk, p_scr, w_scr)

    @pl.when((direction == 0) & (slot < 0))
    def _():
        o_ref[...] = res[...]

    @pl.when((direction == 1) & (slot < 0))
    def _():
        o_ref[...] += res[...]

    @pl.when((direction == 0) & (slot >= 0))
    def _():
        held[jnp.maximum(slot, 0)] = res[...]

    @pl.when((direction == 1) & (slot >= 0))
    def _():
        o_ref[...] = held[jnp.maximum(slot, 0)] + res[...]

    so_ref[0, 0] = s_scr[...]


def _gla(q, k, v, g, s0):
    tok = lambda w: pl.BlockSpec((GLA_T, w), lambda it, tab: (tab[0, it], 0))
    state = pl.BlockSpec((1, 1, H_GLA, DK_GLA, DV_GLA), lambda it, tab: (tab[3, it], tab[1, it], 0, 0, 0))
    grid_spec = pltpu.PrefetchScalarGridSpec(
        num_scalar_prefetch=1,
        grid=(_GLA_ITEMS,),
        in_specs=[
            tok(QK_W), tok(QK_W), tok(V_W),
            pl.BlockSpec((1, GLA_T, QK_W), lambda it, tab: (tab[1, it], tab[0, it], 0)),
            state,
        ],
        out_specs=[
            pl.BlockSpec((GLA_T, V_W), lambda it, tab: (tab[4, it], 0)),
            state,
        ],
        scratch_shapes=[
            pltpu.VMEM((H_GLA, DK_GLA, DV_GLA), F32),
            pltpu.VMEM((GLA_T, QK_W), F32),
            pltpu.VMEM((H_GLA, GLA_NB * DK_GLA, DV_GLA), BF16),
            pltpu.VMEM((_GLA_PAIR_ROWS, QK_W), BF16),
            pltpu.VMEM((_GLA_PAIR_ROWS, QK_W), F32),
            pltpu.VMEM((GLA_T, V_W), F32),
            pltpu.VMEM((DEC_SEQ // GLA_T, GLA_T, V_W), F32),
        ],
    )
    return pl.pallas_call(
        _gla_kernel,
        grid_spec=grid_spec,
        out_shape=[
            jax.ShapeDtypeStruct((N_ALL, V_W), F32),
            jax.ShapeDtypeStruct((_GLA_NSEQ, 2, H_GLA, DK_GLA, DV_GLA), F32),
        ],
        compiler_params=pltpu.CompilerParams(
            dimension_semantics=("arbitrary",), vmem_limit_bytes=VMEM_LIMIT),
        name="gla",
    )(jnp.asarray(_GLA_TAB), q, k, v, g, s0)


def _softmax_pv(s_list, v_list):
    m = s_list[0].max(axis=-1, keepdims=True)
    for s in s_list[1:]:
        m = jnp.maximum(m, s.max(axis=-1, keepdims=True))
    num = None
    den = None
    for s, vv in zip(s_list, v_list):
        e = jnp.exp(s - m)
        den = e.sum(axis=-1, keepdims=True) if den is None else den + e.sum(axis=-1, keepdims=True)
        pv = _dot(e.astype(BF16), vv)
        num = pv if num is None else num + pv
    return num / den


def _ctx_attn_kernel(q_ref, k_ref, v_ref, o_ref):
    scale = HD_NAT ** -0.5
    lane = lax.broadcasted_iota(jnp.int32, (SEQ, LANES), 1)
    for t in range(NAT_W // LANES):
        sl = slice(t * LANES, (t + 1) * LANES)
        qt = q_ref[:, sl]
        kt = k_ref[:, sl].astype(BF16)
        vt = v_ref[:, sl].astype(BF16)
        out = jnp.zeros((SEQ, LANES), F32)
        for half in range(LANES // HD_NAT):
            mine = (lane >> 6) == half
            s = _dot_nt(jnp.where(mine, qt, 0).astype(BF16), kt) * scale
            out = jnp.where(mine, _softmax_pv([s], [vt]), out)
        o_ref[:, sl] = out.astype(o_ref.dtype)


def _ctx_attention(qn, kn, vn):
    spec = pl.BlockSpec((SEQ, NAT_W), lambda b: (b, 0))
    return pl.pallas_call(
        _ctx_attn_kernel,
        grid=(BATCH,),
        in_specs=[spec, spec, spec],
        out_specs=spec,
        out_shape=jax.ShapeDtypeStruct((N_CTX, NAT_W), BF16),
        compiler_params=pltpu.CompilerParams(
            dimension_semantics=("arbitrary",), vmem_limit_bytes=VMEM_LIMIT),
        name="ctx_attention",
    )(qn, kn, vn)


_NAT_QROWS = 4
_NAT_GROUPS = GRID_ROWS // _NAT_QROWS
_NAT_KROWS = WIN_H + _NAT_QROWS
_NAT_Q = _NAT_QROWS * GRID_W
_NAT_KEYS = _NAT_KROWS * GRID_W
_NAT_DR = 2 * WIN_H - 1
_NAT_DC = 2 * WIN_W - 1


def _nat_key_row0(r):
    return jnp.clip(r - WIN_H // 2, 0, GRID_ROWS - WIN_H)


def _nat_build_bias(rpb_ref, tz_scr):
    c = lax.broadcasted_iota(jnp.int32, (GRID_W, LANES), 0)
    lane = lax.broadcasted_iota(jnp.int32, (GRID_W, LANES), 1)
    kc = lane & (GRID_W - 1)
    second = (lane >> 6) == 1
    win_start = jnp.clip(c - WIN_W // 2, 0, GRID_W - WIN_W)
    valid = (kc >= win_start) & (kc < win_start + WIN_W)

    def one_row(n, carry):
        dr = n >> 3
        h = n & (H_NAT - 1)
        src = jnp.broadcast_to(rpb_ref[pl.ds(h * _NAT_DR + dr, 1), :], (GRID_W, LANES))
        rolled = pltpu.roll(src, LANES - (WIN_W - 1), axis=1, stride=1, stride_axis=0)
        tz_scr[dr, h] = jnp.where(valid, rolled, NEG_INF)
        return carry

    lax.fori_loop(0, _NAT_DR * H_NAT, one_row, 0)

    def pair_rows(n, carry):
        dr = n >> 3
        h = n & (H_NAT - 1)
        tz_scr[dr, h] = jnp.where(second, tz_scr[dr + 1, h], tz_scr[dr, h])
        return carry

    lax.fori_loop(0, (_NAT_DR - 1) * H_NAT, pair_rows, 0)


def _nat_kernel(rpb_ref, q_ref, k_ref, v_ref, ck_ref, cv_ref, o_ref, tz_scr, bias_scr):
    grp = pl.program_id(1)

    @pl.when((pl.program_id(0) == 0) & (grp == 0))
    def _():
        _nat_build_bias(rpb_ref, tz_scr)

    scale = HD_NAT ** -0.5
    krow0 = jnp.clip(_NAT_QROWS * grp - WIN_H // 2, 0, GRID_ROWS - _NAT_KROWS)
    k0 = pl.multiple_of(krow0 * GRID_W, GRID_W)
    lane_q = lax.broadcasted_iota(jnp.int32, (_NAT_Q, LANES), 1)
    lane_b = lax.broadcasted_iota(jnp.int32, (GRID_W, LANES), 1)
    neg = jnp.full((GRID_W, LANES), NEG_INF, F32)
    for t in range(NAT_W // LANES):
        sl = slice(t * LANES, (t + 1) * LANES)
        qt = q_ref[:, sl]
        kw = k_ref[pl.ds(k0, _NAT_KEYS), sl].astype(BF16)
        vw = v_ref[pl.ds(k0, _NAT_KEYS), sl].astype(BF16)
        ck = ck_ref[0, 0, :, sl].astype(BF16)
        cv = cv_ref[0, 0, :, sl].astype(BF16)
        out = jnp.zeros((_NAT_Q, LANES), F32)
        for half in range(LANES // HD_NAT):
            h = 2 * t + half
            for qr in range(_NAT_QROWS):
                r = _NAT_QROWS * grp + qr
                lo = _nat_key_row0(r)
                for kp in range(_NAT_KROWS // 2):
                    kr = krow0 + 2 * kp
                    tile = tz_scr[jnp.clip(kr - r + (WIN_H - 1), 0, _NAT_DR - 1), h]
                    ok_a = ((kr >= lo) & (kr < lo + WIN_H)).astype(jnp.int32)
                    ok_b = ((kr + 1 >= lo) & (kr + 1 < lo + WIN_H)).astype(jnp.int32)
                    ok = jnp.where(lane_b < GRID_W, ok_a, ok_b) == 1
                    bias_scr[qr * GRID_W:(qr + 1) * GRID_W, kp * LANES:(kp + 1) * LANES] = jnp.where(ok, tile, neg)
            mine = (lane_q >> 6) == half
            qm = jnp.where(mine, qt, 0).astype(BF16)
            s_win = _dot_nt(qm, kw) * scale + bias_scr[...]
            s_ctx = _dot_nt(qm, ck) * scale
            out = jnp.where(mine, _softmax_pv([s_win, s_ctx], [vw, cv]), out)
        o_ref[:, sl] = out.astype(o_ref.dtype)


def _nat_attention(qn, kn, vn, cache_k, cache_v, rpb_l, layer):
    lat0 = N_CTX // _NAT_Q
    half = jnp.pad(rpb_l.reshape(H_NAT * _NAT_DR, _NAT_DC), ((0, 0), (0, GRID_W - _NAT_DC)))
    rpb_rows = jnp.concatenate([half, half], axis=1)
    return pl.pallas_call(
        _nat_kernel,
        grid=(DEC_BATCH, _NAT_GROUPS),
        in_specs=[
            pl.BlockSpec((H_NAT * _NAT_DR, LANES), lambda b, g: (0, 0)),
            pl.BlockSpec((_NAT_Q, NAT_W), lambda b, g: (lat0 + b * _NAT_GROUPS + g, 0)),
            pl.BlockSpec((DEC_SEQ, NAT_W), lambda b, g: (N_CTX // DEC_SEQ + b, 0)),
            pl.BlockSpec((DEC_SEQ, NAT_W), lambda b, g: (N_CTX // DEC_SEQ + b, 0)),
            pl.BlockSpec((1, 1, PAST_LEN, NAT_W), lambda b, g: (b, layer, 0, 0)),
            pl.BlockSpec((1, 1, PAST_LEN, NAT_W), lambda b, g: (b, layer, 0, 0)),
        ],
        out_specs=pl.BlockSpec((_NAT_Q, NAT_W), lambda b, g: (b * _NAT_GROUPS + g, 0)),
        out_shape=jax.ShapeDtypeStruct((N_LAT, NAT_W), BF16),
        scratch_shapes=[
            pltpu.VMEM((_NAT_DR, H_NAT, GRID_W, LANES), F32),
            pltpu.VMEM((_NAT_Q, _NAT_KEYS), F32),
        ],
        compiler_params=pltpu.CompilerParams(
            dimension_semantics=("arbitrary", "arbitrary"), vmem_limit_bytes=VMEM_LIMIT),
        name="nat_attention",
    )(rpb_rows, qn, kn, vn, cache_k, cache_v)


def _out_kernel(x_ref, o_ref, r_ref, on_ctx_ref, on_lat_ref, mod_ref, ng_ref, gng_ref, w_ref, y_ref):
    o_nat = _read_split(pl.program_id(0), on_ctx_ref, on_lat_ref)
    og = o_ref[...]
    parts = []
    for h in range(H_GLA):
        parts.append(_rms(og[:, h * DV_GLA:(h + 1) * DV_GLA], gng_ref[...]))
    merged = (jnp.concatenate(parts, axis=1) * _silu(r_ref[...])).astype(BF16)
    merged = _permute_chunks(_block_transpose_perm(), merged)
    y = _dot(merged, w_ref[0:V_W, :]) + _dot(o_nat.astype(BF16), w_ref[V_W:, :])
    y_ref[...] = x_ref[...] + mod_ref[0, 5:6, :] * _rms(y, ng_ref[3:4, :])


def _mixer_out(x, o_gla, r, o_ctx, o_lat, mod_l, ng_l, gng_l, w_out_bf):
    tile = lambda w: pl.BlockSpec((TM, w), lambda i: (i, 0))
    full = lambda a: pl.BlockSpec(a.shape, lambda i: (0,) * a.ndim)
    return pl.pallas_call(
        _out_kernel,
        grid=(N_ALL // TM,),
        in_specs=[
            tile(D_MODEL),
            tile(V_W),
            tile(V_W),
            pl.BlockSpec((TM, NAT_W), lambda i: (_ctx_tile(i), 0)),
            pl.BlockSpec((TM, NAT_W), lambda i: (_lat_tile(i), 0)),
            pl.BlockSpec((1, N_MOD, D_MODEL), lambda i: (_group_of_tile(i), 0, 0)),
            full(ng_l), full(gng_l), full(w_out_bf),
        ],
        out_specs=tile(D_MODEL),
        out_shape=jax.ShapeDtypeStruct((N_ALL, D_MODEL), F32),
        compiler_params=pltpu.CompilerParams(
            dimension_semantics=("arbitrary",), vmem_limit_bytes=VMEM_LIMIT),
        name="mixer_out",
    )(x, o_gla, r, o_ctx, o_lat, mod_l, ng_l, gng_l, w_out_bf)


def _rope_tables():
    quarter = DK_GLA // 4
    freqs = ROPE_BASE ** (-jnp.arange(quarter, dtype=F32) / quarter)
    t = jnp.arange(DEC_SEQ)
    ang_r = (t // GRID_W).astype(F32)[:, None] * freqs[None, :]
    ang_c = (t % GRID_W).astype(F32)[:, None] * freqs[None, :]
    cos_h = jnp.concatenate([jnp.cos(ang_r), jnp.cos(ang_r), jnp.cos(ang_c), jnp.cos(ang_c)], axis=1)
    sin_h = jnp.concatenate([-jnp.sin(ang_r), jnp.sin(ang_r), -jnp.sin(ang_c), jnp.sin(ang_c)], axis=1)
    pos_major = lambda a: a.reshape(-1, GLA_NB, GLA_CB, QK_W).transpose(0, 2, 1, 3).reshape(a.shape)
    cos_t = jnp.concatenate([pos_major(jnp.tile(cos_h, (1, H_GLA))), jnp.ones((TM, QK_W), F32)], axis=0)
    sin_t = jnp.concatenate([pos_major(jnp.tile(sin_h, (1, H_GLA))), jnp.zeros((TM, QK_W), F32)], axis=0)
    return cos_t, sin_t


def _swap_perm():
    quarter = DK_GLA // 4
    idx = np.arange(QK_W)
    blk = (idx // quarter) % 4
    return np.where(blk % 2 == 0, idx + quarter, idx - quarter)


def _pack_w_in(w_in):
    sizes = [QK_W, QK_W, V_W, V_W, 2 * GLA_RANK, 3 * NAT_W]
    cuts = np.cumsum([0] + sizes)
    wq, wk, wv, wr, wlr, wnat = [w_in[:, :, cuts[i]:cuts[i + 1]] for i in range(len(sizes))]
    swap = _swap_perm()
    pad = jnp.zeros((DEPTH, D_MODEL, LANES - 2 * GLA_RANK), w_in.dtype)
    return jnp.concatenate([wq, wk, wq[:, :, swap], wk[:, :, swap], wv, wr, wnat, wlr, pad], axis=2).astype(BF16)


def _gate_up_weights(gla_wa2_l, gla_ba_l):
    w_a = jnp.zeros((LANES, 2 * QK_W), F32)
    w_a = w_a.at[0:GLA_RANK, 0:QK_W].set(gla_wa2_l[0]).at[GLA_RANK:2 * GLA_RANK, QK_W:].set(gla_wa2_l[1])
    b_a = jnp.concatenate([gla_ba_l[0], gla_ba_l[1]])[None, :]
    return w_a.astype(BF16), b_a


def kernel(x_prompt, x_sample, cache_k, cache_v, state_gla, c, c_ctx, w_mod, b_mod, norm_g, ffn_w_in, ffn_w_out,
           w_in, gla_wa2, gla_ba, gla_norm_g, nat_rpb, w_out):
    cvecs = jnp.zeros((SUBLANES, D_MODEL), F32).at[0].set(c_ctx).at[1:1 + DEC_BATCH].set(c)
    mod = _modulation(cvecs, w_mod, b_mod)[:, :N_GROUPS].reshape(DEPTH, N_GROUPS, N_MOD, D_MODEL)

    w_out_bf = w_out.astype(BF16)
    cos_t, sin_t = _rope_tables()
    ck = cache_k.reshape(DEC_BATCH, DEPTH, PAST_LEN, NAT_W)
    cv = cache_v.reshape(DEC_BATCH, DEPTH, PAST_LEN, NAT_W)

    w_big = _pack_w_in(w_in)

    x = (x_prompt.reshape(N_CTX, D_MODEL), x_sample.reshape(N_LAT, D_MODEL))
    k_list, v_list, s_list = [], [], []
    for l in range(DEPTH):
        mod_l, ng_l = mod[l], norm_g[l]
        x = _ffn(x, mod_l, ng_l, ffn_w_in, ffn_w_out, layer=l, which=0)

        w_a, b_a = _gate_up_weights(gla_wa2[l], gla_ba[l])
        q, k, v, r, g, qn, kn_ctx, vn_ctx, kn, vn = _project(
            x, mod_l, ng_l, w_big, w_a, b_a, cos_t, sin_t, layer=l)

        s0 = jnp.concatenate([jnp.zeros((BATCH, 2, H_GLA, DK_GLA, DV_GLA), F32), state_gla[:, l]], axis=0)
        o_gla, s_fin = _gla(q, k, v, g, s0)
        o_ctx = _ctx_attention(qn, kn, vn)
        o_lat = _nat_attention(qn, kn, vn, ck, cv, nat_rpb[l], l)

        x = _mixer_out(x, o_gla, r, o_ctx, o_lat, mod_l, ng_l, gla_norm_g[l][None, :], w_out_bf[l])
        x = _ffn(x, mod_l, ng_l, ffn_w_in, ffn_w_out, layer=l, which=1, split_out=(l == DEPTH - 1))

        k_list.append(kn_ctx)
        v_list.append(vn_ctx)
        s_list.append(s_fin[:BATCH])

    y_prompt = x[0].reshape(BATCH, SEQ, D_MODEL)
    y_sample = x[1].reshape(DEC_BATCH, DEC_SEQ, D_MODEL)
    return (y_prompt, y_sample, jnp.stack(k_list, axis=1), jnp.stack(v_list, axis=1), jnp.stack(s_list, axis=1))
```

```python
import functools

import numpy as np
import jax
import jax.numpy as jnp
from jax import lax
from jax.experimental import pallas as pl
from jax.experimental.pallas import tpu as pltpu

D_MODEL = 1024
BATCH = 16
SEQ = 256
DEPTH = 2
DEC_BATCH = 2
DEC_SEQ = 1024
PAST_LEN = 512
GRID_W = 64
H_GLA = 4
DK_GLA = 64
DV_GLA = 128
GLA_RANK = 16
GATE_NORM = 16.0
H_NAT = 8
HD_NAT = 64
WIN_H = 8
WIN_W = 16
D_FF = 2816
N_MOD = 9
ROPE_BASE = 10000.0
EPS = 1e-6
NEG_INF = -1e30

F32 = jnp.float32
BF16 = jnp.bfloat16

N_CTX = BATCH * SEQ
N_LAT = DEC_BATCH * DEC_SEQ
N_ALL = N_CTX + N_LAT
N_GROUPS = 1 + DEC_BATCH
QK_W = H_GLA * DK_GLA
V_W = H_GLA * DV_GLA
NAT_W = H_NAT * HD_NAT
GRID_ROWS = DEC_SEQ // GRID_W

LANES = 128
SUBLANES = 8

TM = 512
FFN_TC = 256
FFN_NC = D_FF // FFN_TC
FFN_STAGES = 2
MOD_TN = 1152
GLA_T = 256
GLA_CB = 16
GLA_NB = GLA_T // GLA_CB
VMEM_LIMIT = 56 * 1024 * 1024

assert N_CTX % TM == 0 and DEC_SEQ % TM == 0 and D_FF % FFN_TC == 0 and FFN_TC % LANES == 0
assert SEQ == GLA_T and DEC_SEQ % GLA_T == 0


def _group_of_tile(i):
    return jnp.where(i < N_CTX // TM, 0, 1 + (i - N_CTX // TM) // (DEC_SEQ // TM))


def _dot(a, b):
    return jnp.dot(a, b, preferred_element_type=F32)


def _dot_nt(a, b):
    return lax.dot_general(a, b, (((1,), (1,)), ((), ())), preferred_element_type=F32)


def _dot_tn(a, b):
    return lax.dot_general(a, b, (((0,), (0,)), ((), ())), preferred_element_type=F32)


def _rms(x, g):
    ms = jnp.mean(x * x, axis=-1, keepdims=True)
    return x * lax.rsqrt(ms + EPS) * g


def _silu(x):
    return x * jax.nn.sigmoid(x)


def _mod_kernel(c_ref, w_ref, b_ref, o_ref):
    s = _silu(c_ref[...]).astype(BF16)
    o_ref[0] = _dot(s, w_ref[0].astype(BF16)) + b_ref[0]


def _modulation(cvecs, w_mod, b_mod):
    n_out = N_MOD * D_MODEL
    return pl.pallas_call(
        _mod_kernel,
        grid=(DEPTH, n_out // MOD_TN),
        in_specs=[
            pl.BlockSpec((SUBLANES, D_MODEL), lambda l, j: (0, 0)),
            pl.BlockSpec((1, D_MODEL, MOD_TN), lambda l, j: (l, 0, j)),
            pl.BlockSpec((1, 1, MOD_TN), lambda l, j: (l, 0, j)),
        ],
        out_specs=pl.BlockSpec((1, SUBLANES, MOD_TN), lambda l, j: (l, 0, j)),
        out_shape=jax.ShapeDtypeStruct((DEPTH, SUBLANES, n_out), F32),
        compiler_params=pltpu.CompilerParams(
            dimension_semantics=("arbitrary", "arbitrary"), vmem_limit_bytes=VMEM_LIMIT),
        name="modulation",
    )(cvecs, w_mod, b_mod.reshape(DEPTH, 1, n_out))


_CTX_TILES = N_CTX // TM
_LAT_TILES = N_LAT // TM


def _ctx_tile(i):
    return jnp.minimum(i, _CTX_TILES - 1)


def _lat_tile(i):
    return jnp.maximum(i - _CTX_TILES, 0)


def _read_split(i, ctx_ref, lat_ref):
    return jnp.where(i < _CTX_TILES, ctx_ref[...], lat_ref[...])


def _ffn_kernel(*refs, m0, n0, layer, which, split_in, split_out):
    n_x = 2 if split_in else 1
    x_refs, (mod_ref, ng_ref, win_hbm, wout_hbm) = refs[:n_x], refs[n_x:n_x + 4]
    n_o = 2 if split_out else 1
    o_refs = refs[n_x + 4:n_x + 4 + n_o]
    h_scr, acc_scr, x_scr, wg_bf, wu_bf, wo_bf, stage_g, stage_u, stage_o, sem = refs[n_x + 4 + n_o:]
    i = pl.program_id(0)

    def chunk_copies(c, slot):
        cols = pl.ds(c * FFN_TC, FFN_TC)
        up_cols = pl.ds(D_FF + c * FFN_TC, FFN_TC)
        return (
            pltpu.make_async_copy(win_hbm.at[layer, which, :, cols], stage_g.at[slot], sem.at[0, slot]),
            pltpu.make_async_copy(win_hbm.at[layer, which, :, up_cols], stage_u.at[slot], sem.at[1, slot]),
            pltpu.make_async_copy(wout_hbm.at[layer, which, cols, :], stage_o.at[slot], sem.at[2, slot]),
        )

    def accumulate(c):
        h = h_scr[...]
        gate = _dot(h, wg_bf[c])
        up = _dot(h, wu_bf[c])
        part = _dot((_silu(gate) * up).astype(BF16), wo_bf[c])
        if c == 0:
            acc_scr[...] = part
        else:
            acc_scr[...] += part

    x = _read_split(i, *x_refs) if split_in else x_refs[0][...]
    x_scr[...] = x
    gain = ng_ref[n0:n0 + 1, :] * (1.0 + mod_ref[0, m0 + 1:m0 + 2, :])
    h_scr[...] = (_rms(x, gain) + mod_ref[0, m0:m0 + 1, :]).astype(BF16)

    @pl.when(i == 0)
    def _():
        for c in range(min(FFN_STAGES, FFN_NC)):
            for cp in chunk_copies(c, c):
                cp.start()
        for c in range(FFN_NC):
            slot = c % FFN_STAGES
            for cp in chunk_copies(c, slot):
                cp.wait()
            wg_bf[c] = stage_g[slot].astype(BF16)
            wu_bf[c] = stage_u[slot].astype(BF16)
            wo_bf[c] = stage_o[slot].astype(BF16)
            if c + FFN_STAGES < FFN_NC:
                for cp in chunk_copies(c + FFN_STAGES, slot):
                    cp.start()
            accumulate(c)

    @pl.when(i > 0)
    def _():
        for c in range(FFN_NC):
            accumulate(c)

    gain = 0.5 * mod_ref[0, m0 + 2:m0 + 3, :] * ng_ref[n0 + 1:n0 + 2, :]
    out = x_scr[...] + _rms(acc_scr[...], gain)
    if split_out:
        @pl.when(i < _CTX_TILES)
        def _():
            o_refs[0][...] = out

        @pl.when(i >= _CTX_TILES)
        def _():
            o_refs[1][...] = out
    else:
        o_refs[0][...] = out


def _ffn(x, mod_l, ng_l, w_in, w_out, *, layer, which, split_out=False):
    m0, n0 = (0, 0) if which == 0 else (6, 4)
    split_in = isinstance(x, tuple)
    tile = pl.BlockSpec((TM, D_MODEL), lambda i: (i, 0))
    ctx_tile = pl.BlockSpec((TM, D_MODEL), lambda i: (_ctx_tile(i), 0))
    lat_tile = pl.BlockSpec((TM, D_MODEL), lambda i: (_lat_tile(i), 0))
    if split_out:
        out_specs = [ctx_tile, lat_tile]
        out_shape = [jax.ShapeDtypeStruct((N_CTX, D_MODEL), F32), jax.ShapeDtypeStruct((N_LAT, D_MODEL), F32)]
    else:
        out_specs = tile
        out_shape = jax.ShapeDtypeStruct((N_ALL, D_MODEL), F32)
    return pl.pallas_call(
        functools.partial(_ffn_kernel, m0=m0, n0=n0, layer=layer, which=which, split_in=split_in,
                          split_out=split_out),
        grid=(N_ALL // TM,),
        in_specs=([ctx_tile, lat_tile] if split_in else [tile]) + [
            pl.BlockSpec((1, N_MOD, D_MODEL), lambda i: (_group_of_tile(i), 0, 0)),
            pl.BlockSpec((6, D_MODEL), lambda i: (0, 0)),
            pl.BlockSpec(memory_space=pl.ANY),
            pl.BlockSpec(memory_space=pl.ANY),
        ],
        out_specs=out_specs,
        out_shape=out_shape,
        scratch_shapes=[
            pltpu.VMEM((TM, D_MODEL), BF16),
            pltpu.VMEM((TM, D_MODEL), F32),
            pltpu.VMEM((TM, D_MODEL), F32),
            pltpu.VMEM((FFN_NC, D_MODEL, FFN_TC), BF16),
            pltpu.VMEM((FFN_NC, D_MODEL, FFN_TC), BF16),
            pltpu.VMEM((FFN_NC, FFN_TC, D_MODEL), BF16),
            pltpu.VMEM((FFN_STAGES, D_MODEL, FFN_TC), F32),
            pltpu.VMEM((FFN_STAGES, D_MODEL, FFN_TC), F32),
            pltpu.VMEM((FFN_STAGES, FFN_TC, D_MODEL), F32),
            pltpu.SemaphoreType.DMA((3, FFN_STAGES)),
        ],
        compiler_params=pltpu.CompilerParams(
            dimension_semantics=("arbitrary",), vmem_limit_bytes=VMEM_LIMIT),
        name="ffn",
    )(*(x if split_in else (x,)), mod_l, ng_l, w_in, w_out)


def _block_transpose_perm():
    r = lax.broadcasted_iota(jnp.int32, (GLA_T, GLA_T), 0)
    c = lax.broadcasted_iota(jnp.int32, (GLA_T, GLA_T), 1)
    return jnp.where(c == (r & (GLA_CB - 1)) * GLA_NB + (r >> 4), 1.0, 0.0).astype(BF16)


def _permute_chunks(perm, a):
    return jnp.concatenate([_dot(perm, a[c * GLA_T:(c + 1) * GLA_T, :]).astype(BF16)
                            for c in range(a.shape[0] // GLA_T)], axis=0)


_P_QK = 0
_P_VR = 4 * QK_W
_P_NAT = _P_VR + 2 * V_W
_P_LR = _P_NAT + 3 * NAT_W
_P_END = _P_LR + LANES


def _proj_kernel(x_ref, mod_ref, ng_ref, w_ref, wa_ref, ba_ref, cos_ref, sin_ref, *rest):
    q_ref, k_ref, v_ref, r_ref, g_ref, qn_ref, kn_ctx_ref, vn_ctx_ref, kn_ref, vn_ref = rest[-10:]
    i = pl.program_id(0)
    h = _rms(x_ref[...], ng_ref[2:3, :])
    h = (h * (1.0 + mod_ref[0, 4:5, :]) + mod_ref[0, 3:4, :]).astype(BF16)
    hp = _permute_chunks(_block_transpose_perm(), h)

    lr = _dot(hp, w_ref[0, :, _P_LR:_P_END]).astype(BF16)
    z = _dot(lr, wa_ref[...]) + ba_ref[...]
    g = (jnp.minimum(z, 0.0) - jnp.log1p(jnp.exp(-jnp.abs(z)))) * (1.0 / GATE_NORM)
    g_ref[0] = g[:, 0:QK_W]
    g_ref[1] = g[:, QK_W:2 * QK_W]

    qk = _dot(hp, w_ref[0, :, _P_QK:_P_VR])
    cos = cos_ref[...]
    sin = sin_ref[...]
    q_ref[...] = (qk[:, 0:QK_W] * cos + qk[:, 2 * QK_W:3 * QK_W] * sin) * (DK_GLA ** -0.5)
    k_ref[...] = qk[:, QK_W:2 * QK_W] * cos + qk[:, 3 * QK_W:4 * QK_W] * sin

    vr = _dot(hp, w_ref[0, :, _P_VR:_P_NAT])
    v_ref[...] = vr[:, 0:V_W]
    r_ref[...] = vr[:, V_W:2 * V_W]

    nat = _dot(h, w_ref[0, :, _P_NAT:_P_LR])
    qn_ref[...] = nat[:, 0:NAT_W].astype(BF16)
    kn_ref[...] = nat[:, NAT_W:2 * NAT_W].astype(BF16)
    vn_ref[...] = nat[:, 2 * NAT_W:3 * NAT_W].astype(BF16)

    @pl.when(i < _CTX_TILES)
    def _():
        heads = (TM // SEQ, SEQ, H_NAT, HD_NAT)
        kn_ctx_ref[:, 0] = nat[:, NAT_W:2 * NAT_W].reshape(heads)
        vn_ctx_ref[:, 0] = nat[:, 2 * NAT_W:3 * NAT_W].reshape(heads)
        for later in range(1, kn_ctx_ref.shape[1]):
            kn_ctx_ref[:, later] = jnp.zeros(heads, F32)
            vn_ctx_ref[:, later] = jnp.zeros(heads, F32)


def _rope_table_block(i):
    lat_tiles = DEC_SEQ // TM
    return jnp.where(i < N_CTX // TM, lat_tiles, (i - N_CTX // TM) % lat_tiles)


def _project(x, mod_l, ng_l, w_big, w_a, b_a, cos_t, sin_t, new_cache, *, layer):
    tile = lambda w: pl.BlockSpec((TM, w), lambda i: (i, 0))
    if new_cache:
        ctx_heads = pl.BlockSpec((TM // SEQ, 1, SEQ, H_NAT, HD_NAT), lambda i: (_ctx_tile(i), layer, 0, 0, 0))
    else:
        ctx_heads = pl.BlockSpec((TM // SEQ, DEPTH, SEQ, H_NAT, HD_NAT), lambda i: (_ctx_tile(i), 0, 0, 0, 0))
    full = lambda a: pl.BlockSpec(a.shape, lambda i: (0,) * a.ndim)
    n_in = 8
    return pl.pallas_call(
        _proj_kernel,
        grid=(N_ALL // TM,),
        in_specs=[
            tile(D_MODEL),
            pl.BlockSpec((1, N_MOD, D_MODEL), lambda i: (_group_of_tile(i), 0, 0)),
            full(ng_l),
            pl.BlockSpec((1, D_MODEL, _P_END), lambda i: (layer, 0, 0)),
            full(w_a), full(b_a),
            pl.BlockSpec((TM, QK_W), lambda i: (_rope_table_block(i), 0)),
            pl.BlockSpec((TM, QK_W), lambda i: (_rope_table_block(i), 0)),
        ] + [pl.BlockSpec(memory_space=pl.ANY)] * len(new_cache),
        input_output_aliases={n_in + n: 6 + n for n in range(len(new_cache))},
        out_specs=[
            tile(QK_W), tile(QK_W), tile(V_W), tile(V_W),
            pl.BlockSpec((2, TM, QK_W), lambda i: (0, i, 0)),
            tile(NAT_W), ctx_heads, ctx_heads, tile(NAT_W), tile(NAT_W),
        ],
        out_shape=[
            jax.ShapeDtypeStruct((N_ALL, QK_W), F32), jax.ShapeDtypeStruct((N_ALL, QK_W), F32),
            jax.ShapeDtypeStruct((N_ALL, V_W), F32), jax.ShapeDtypeStruct((N_ALL, V_W), F32),
            jax.ShapeDtypeStruct((2, N_ALL, QK_W), F32),
            jax.ShapeDtypeStruct((N_ALL, NAT_W), BF16),
            jax.ShapeDtypeStruct((BATCH, DEPTH, SEQ, H_NAT, HD_NAT), F32),
            jax.ShapeDtypeStruct((BATCH, DEPTH, SEQ, H_NAT, HD_NAT), F32),
            jax.ShapeDtypeStruct((N_ALL, NAT_W), BF16), jax.ShapeDtypeStruct((N_ALL, NAT_W), BF16),
        ],
        compiler_params=pltpu.CompilerParams(
            dimension_semantics=("arbitrary",), vmem_limit_bytes=VMEM_LIMIT),
        name="mixer_proj",
    )(x, mod_l, ng_l, w_big, w_a, b_a, cos_t, sin_t, *new_cache)


def _gla_tables():
    rows = []
    seq_specs = [(b * (SEQ // GLA_T), SEQ // GLA_T) for b in range(BATCH)]
    seq_specs += [(N_CTX // GLA_T + b * (DEC_SEQ // GLA_T), DEC_SEQ // GLA_T) for b in range(DEC_BATCH)]
    for sid, (blk0, nchunk) in enumerate(seq_specs):
        for direction in (0, 1):
            order = range(nchunk) if direction == 0 else range(nchunk - 1, -1, -1)
            for n, c in enumerate(order):
                out_blk = blk0 + (c if direction == 1 or nchunk == 1 else nchunk - 1)
                rows.append((blk0 + c, direction, int(n == 0), sid, out_blk, c if nchunk > 1 else -1))
    return np.asarray(rows, dtype=np.int32).T.copy()


_GLA_TAB = _gla_tables()
_GLA_ITEMS = _GLA_TAB.shape[1]
_GLA_NSEQ = BATCH + DEC_BATCH
_GLA_PAIR_ROWS = GLA_NB * GLA_CB * (GLA_CB + 1) // 2


def _gla_item(direction, q_ref, k_ref, v_ref, g_ref, res, s_scr, cp, sstk, p_scr, w_scr):
    T, CB, NB = GLA_T, GLA_CB, GLA_NB
    fwd = direction == 0
    slab = lambda i: slice(i * NB, (i + 1) * NB)

    cum = None
    for i in (range(CB) if fwd else range(CB - 1, -1, -1)):
        gi = g_ref[0, slab(i), :]
        cum = gi if cum is None else cum + gi
        cp[slab(i), :] = cum
    total = cum
    cum_all = cp[...]
    qt = q_ref[...] * jnp.exp(cum_all)
    kh = k_ref[...] * jnp.exp(jnp.concatenate([total] * CB, axis=0) - cum_all)
    dec_t = jnp.concatenate([jnp.exp(total), jnp.zeros((LANES - NB, QK_W), F32)], axis=0).T

    erow = lax.broadcasted_iota(jnp.int32, (QK_W, QK_W), 0)
    ecol = lax.broadcasted_iota(jnp.int32, (QK_W, QK_W), 1)
    head_sum = jnp.where((erow >> 6) == (ecol >> 6), 1.0, 0.0).astype(BF16)

    key_positions = lambda i: range(i + 1) if fwd else range(i, CB)
    r0 = 0
    for i in range(CB):
        qi = q_ref[slab(i), :]
        ci = cp[slab(i), :]
        for j in key_positions(i):
            e = jnp.exp(ci - cp[slab(j), :])
            p_scr[r0:r0 + NB, :] = (qi * k_ref[slab(j), :] * e).astype(BF16)
            r0 += NB
    w_scr[...] = _dot(p_scr[...], head_sum)
    first_half = lax.broadcasted_iota(jnp.int32, (NB, LANES), 1) < DK_GLA
    r0 = 0
    for i in range(CB):
        acc = None
        for j in key_positions(i):
            spread = []
            for pair in range(H_GLA // 2):
                tile = w_scr[r0:r0 + NB, pair * LANES:(pair + 1) * LANES]
                other = pltpu.roll(tile, DK_GLA, axis=1)
                spread += [jnp.where(first_half, tile, other), jnp.where(first_half, other, tile)]
            term = jnp.concatenate(spread, axis=1) * v_ref[slab(j), :]
            acc = term if acc is None else acc + term
            r0 += NB
        res[slab(i), :] = acc

    kht = kh.T.astype(BF16)
    v_bf = v_ref[...].astype(BF16)
    key_blk = lax.broadcasted_iota(jnp.int32, (DK_GLA, T), 1) & (NB - 1)
    row_blk = lax.broadcasted_iota(jnp.int32, (T, LANES), 0) & (NB - 1)
    lane_half = lax.broadcasted_iota(jnp.int32, (T, LANES), 1) >> 6
    order = range(NB) if fwd else range(NB - 1, -1, -1)
    for h in range(H_GLA):
        kh_h = kht[h * DK_GLA:(h + 1) * DK_GLA, :]
        kv = _dot(jnp.concatenate([jnp.where(key_blk == b, kh_h, 0) for b in range(NB)], axis=0),
                  v_bf[:, h * DV_GLA:(h + 1) * DV_GLA])
        s = s_scr[h]
        for b in order:
            sstk[h, b * DK_GLA:(b + 1) * DK_GLA, :] = s.astype(BF16)
            s = dec_t[h * DK_GLA:(h + 1) * DK_GLA, b:b + 1] * s + kv[b * DK_GLA:(b + 1) * DK_GLA, :]
        s_scr[h] = s

        pair_tile = qt[:, (h // 2) * LANES:(h // 2 + 1) * LANES]
        both = jnp.where(lane_half == h % 2, pair_tile, pltpu.roll(pair_tile, DK_GLA, axis=1))
        lhs = jnp.concatenate([jnp.where(row_blk == 2 * j + lane_half, both, 0.0).astype(BF16)
                               for j in range(NB // 2)], axis=1)
        res[:, h * DV_GLA:(h + 1) * DV_GLA] += _dot(lhs, sstk[h])


def _gla_kernel(tab_ref, q_ref, k_ref, v_ref, g_ref, s0_ref, o_ref, so_ref,
                s_scr, cp, sstk, p_scr, w_scr, res, held):
    it = pl.program_id(0)
    direction = tab_ref[1, it]
    slot = tab_ref[5, it]

    @pl.when(tab_ref[2, it] == 1)
    def _():
        s_scr[...] = s0_ref[0, 0]

    for d in (0, 1):
        @pl.when(direction == d)
        def _(d=d):
            _gla_item(d, q_ref, k_ref, v_ref, g_ref, res, s_scr, cp, sstk, p_scr, w_scr)

    @pl.when((direction == 0) & (slot < 0))
    def _():
        o_ref[...] = res[...]

    @pl.when((direction == 1) & (slot < 0))
    def _():
        o_ref[...] += res[...]

    @pl.when((direction == 0) & (slot >= 0))
    def _():
        held[jnp.maximum(slot, 0)] = res[...]

    @pl.when((direction == 1) & (slot >= 0))
    def _():
        o_ref[...] = held[jnp.maximum(slot, 0)] + res[...]

    so_ref[0, 0] = s_scr[...]


def _gla(q, k, v, g, s0):
    tok = lambda w: pl.BlockSpec((GLA_T, w), lambda it, tab: (tab[0, it], 0))
    state = pl.BlockSpec((1, 1, H_GLA, DK_GLA, DV_GLA), lambda it, tab: (tab[3, it], tab[1, it], 0, 0, 0))
    grid_spec = pltpu.PrefetchScalarGridSpec(
        num_scalar_prefetch=1,
        grid=(_GLA_ITEMS,),
        in_specs=[
            tok(QK_W), tok(QK_W), tok(V_W),
            pl.BlockSpec((1, GLA_T, QK_W), lambda it, tab: (tab[1, it], tab[0, it], 0)),
            state,
        ],
        out_specs=[
            pl.BlockSpec((GLA_T, V_W), lambda it, tab: (tab[4, it], 0)),
            state,
        ],
        scratch_shapes=[
            pltpu.VMEM((H_GLA, DK_GLA, DV_GLA), F32),
            pltpu.VMEM((GLA_T, QK_W), F32),
            pltpu.VMEM((H_GLA, GLA_NB * DK_GLA, DV_GLA), BF16),
            pltpu.VMEM((_GLA_PAIR_ROWS, QK_W), BF16),
            pltpu.VMEM((_GLA_PAIR_ROWS, QK_W), F32),
            pltpu.VMEM((GLA_T, V_W), F32),
            pltpu.VMEM((DEC_SEQ // GLA_T, GLA_T, V_W), F32),
        ],
    )
    return pl.pallas_call(
        _gla_kernel,
        grid_spec=grid_spec,
        out_shape=[
            jax.ShapeDtypeStruct((N_ALL, V_W), F32),
            jax.ShapeDtypeStruct((_GLA_NSEQ, 2, H_GLA, DK_GLA, DV_GLA), F32),
        ],
        compiler_params=pltpu.CompilerParams(
            dimension_semantics=("arbitrary",), vmem_limit_bytes=VMEM_LIMIT),
        name="gla",
    )(jnp.asarray(_GLA_TAB), q, k, v, g, s0)


def _softmax_pv(s_list, v_list):
    m = s_list[0].max(axis=-1, keepdims=True)
    for s in s_list[1:]:
        m = jnp.maximum(m, s.max(axis=-1, keepdims=True))
    num = None
    den = None
    for s, vv in zip(s_list, v_list):
        e = jnp.exp(s - m)
        den = e.sum(axis=-1, keepdims=True) if den is None else den + e.sum(axis=-1, keepdims=True)
        pv = _dot(e.astype(BF16), vv)
        num = pv if num is None else num + pv
    return num / den


def _ctx_attn_kernel(q_ref, k_ref, v_ref, o_ref):
    scale = HD_NAT ** -0.5
    lane = lax.broadcasted_iota(jnp.int32, (SEQ, LANES), 1)
    for t in range(NAT_W // LANES):
        sl = slice(t * LANES, (t + 1) * LANES)
        qt = q_ref[:, sl]
        kt = k_ref[:, sl].astype(BF16)
        vt = v_ref[:, sl].astype(BF16)
        out = jnp.zeros((SEQ, LANES), F32)
        for half in range(LANES // HD_NAT):
            mine = (lane >> 6) == half
            s = _dot_nt(jnp.where(mine, qt, 0).astype(BF16), kt) * scale
            out = jnp.where(mine, _softmax_pv([s], [vt]), out)
        o_ref[:, sl] = out.astype(o_ref.dtype)


def _ctx_attention(qn, kn, vn):
    spec = pl.BlockSpec((SEQ, NAT_W), lambda b: (b, 0))
    return pl.pallas_call(
        _ctx_attn_kernel,
        grid=(BATCH,),
        in_specs=[spec, spec, spec],
        out_specs=spec,
        out_shape=jax.ShapeDtypeStruct((N_CTX, NAT_W), BF16),
        compiler_params=pltpu.CompilerParams(
            dimension_semantics=("arbitrary",), vmem_limit_bytes=VMEM_LIMIT),
        name="ctx_attention",
    )(qn, kn, vn)


_NAT_QROWS = 4
_NAT_GROUPS = GRID_ROWS // _NAT_QROWS
_NAT_KROWS = WIN_H + _NAT_QROWS
_NAT_Q = _NAT_QROWS * GRID_W
_NAT_KEYS = _NAT_KROWS * GRID_W
_NAT_DR = 2 * WIN_H - 1
_NAT_DC = 2 * WIN_W - 1


def _nat_key_row0(r):
    return jnp.clip(r - WIN_H // 2, 0, GRID_ROWS - WIN_H)


def _nat_build_bias(rpb_ref, tz_scr):
    c = lax.broadcasted_iota(jnp.int32, (GRID_W, LANES), 0)
    lane = lax.broadcasted_iota(jnp.int32, (GRID_W, LANES), 1)
    kc = lane & (GRID_W - 1)
    second = (lane >> 6) == 1
    win_start = jnp.clip(c - WIN_W // 2, 0, GRID_W - WIN_W)
    valid = (kc >= win_start) & (kc < win_start + WIN_W)

    def one_row(n, carry):
        dr = n >> 3
        h = n & (H_NAT - 1)
        src = jnp.broadcast_to(rpb_ref[pl.ds(h * _NAT_DR + dr, 1), :], (GRID_W, LANES))
        rolled = pltpu.roll(src, LANES - (WIN_W - 1), axis=1, stride=1, stride_axis=0)
        tz_scr[dr, h] = jnp.where(valid, rolled, NEG_INF)
        return carry

    lax.fori_loop(0, _NAT_DR * H_NAT, one_row, 0)

    def pair_rows(n, carry):
        dr = n >> 3
        h = n & (H_NAT - 1)
        tz_scr[dr, h] = jnp.where(second, tz_scr[dr + 1, h], tz_scr[dr, h])
        return carry

    lax.fori_loop(0, (_NAT_DR - 1) * H_NAT, pair_rows, 0)


def _nat_kernel(rpb_ref, q_ref, k_ref, v_ref, ck_ref, cv_ref, o_ref, tz_scr, bias_scr):
    grp = pl.program_id(1)

    @pl.when((pl.program_id(0) == 0) & (grp == 0))
    def _():
        _nat_build_bias(rpb_ref, tz_scr)

    scale = HD_NAT ** -0.5
    krow0 = jnp.clip(_NAT_QROWS * grp - WIN_H // 2, 0, GRID_ROWS - _NAT_KROWS)
    k0 = pl.multiple_of(krow0 * GRID_W, GRID_W)
    lane_q = lax.broadcasted_iota(jnp.int32, (_NAT_Q, LANES), 1)
    lane_b = lax.broadcasted_iota(jnp.int32, (GRID_W, LANES), 1)
    neg = jnp.full((GRID_W, LANES), NEG_INF, F32)
    for t in range(NAT_W // LANES):
        sl = slice(t * LANES, (t + 1) * LANES)
        qt = q_ref[:, sl]
        kw = k_ref[pl.ds(k0, _NAT_KEYS), sl].astype(BF16)
        vw = v_ref[pl.ds(k0, _NAT_KEYS), sl].astype(BF16)
        ck = ck_ref[0, 0, :, sl].astype(BF16)
        cv = cv_ref[0, 0, :, sl].astype(BF16)
        out = jnp.zeros((_NAT_Q, LANES), F32)
        for half in range(LANES // HD_NAT):
            h = 2 * t + half
            for qr in range(_NAT_QROWS):
                r = _NAT_QROWS * grp + qr
                lo = _nat_key_row0(r)
                for kp in range(_NAT_KROWS // 2):
                    kr = krow0 + 2 * kp
                    tile = tz_scr[jnp.clip(kr - r + (WIN_H - 1), 0, _NAT_DR - 1), h]
                    ok_a = ((kr >= lo) & (kr < lo + WIN_H)).astype(jnp.int32)
                    ok_b = ((kr + 1 >= lo) & (kr + 1 < lo + WIN_H)).astype(jnp.int32)
                    ok = jnp.where(lane_b < GRID_W, ok_a, ok_b) == 1
                    bias_scr[qr * GRID_W:(qr + 1) * GRID_W, kp * LANES:(kp + 1) * LANES] = jnp.where(ok, tile, neg)
            mine = (lane_q >> 6) == half
            qm = jnp.where(mine, qt, 0).astype(BF16)
            s_win = _dot_nt(qm, kw) * scale + bias_scr[...]
            s_ctx = _dot_nt(qm, ck) * scale
            out = jnp.where(mine, _softmax_pv([s_win, s_ctx], [vw, cv]), out)
        o_ref[:, sl] = out.astype(o_ref.dtype)


def _nat_attention(qn, kn, vn, cache_k, cache_v, rpb_l, layer):
    lat0 = N_CTX // _NAT_Q
    half = jnp.pad(rpb_l.reshape(H_NAT * _NAT_DR, _NAT_DC), ((0, 0), (0, GRID_W - _NAT_DC)))
    rpb_rows = jnp.concatenate([half, half], axis=1)
    return pl.pallas_call(
        _nat_kernel,
        grid=(DEC_BATCH, _NAT_GROUPS),
        in_specs=[
            pl.BlockSpec((H_NAT * _NAT_DR, LANES), lambda b, g: (0, 0)),
            pl.BlockSpec((_NAT_Q, NAT_W), lambda b, g: (lat0 + b * _NAT_GROUPS + g, 0)),
            pl.BlockSpec((DEC_SEQ, NAT_W), lambda b, g: (N_CTX // DEC_SEQ + b, 0)),
            pl.BlockSpec((DEC_SEQ, NAT_W), lambda b, g: (N_CTX // DEC_SEQ + b, 0)),
            pl.BlockSpec((1, 1, PAST_LEN, NAT_W), lambda b, g: (b, layer, 0, 0)),
            pl.BlockSpec((1, 1, PAST_LEN, NAT_W), lambda b, g: (b, layer, 0, 0)),
        ],
        out_specs=pl.BlockSpec((_NAT_Q, NAT_W), lambda b, g: (b * _NAT_GROUPS + g, 0)),
        out_shape=jax.ShapeDtypeStruct((N_LAT, NAT_W), BF16),
        scratch_shapes=[
            pltpu.VMEM((_NAT_DR, H_NAT, GRID_W, LANES), F32),
            pltpu.VMEM((_NAT_Q, _NAT_KEYS), F32),
        ],
        compiler_params=pltpu.CompilerParams(
            dimension_semantics=("arbitrary", "arbitrary"), vmem_limit_bytes=VMEM_LIMIT),
        name="nat_attention",
    )(rpb_rows, qn, kn, vn, cache_k, cache_v)


def _out_kernel(x_ref, o_ref, r_ref, on_ctx_ref, on_lat_ref, mod_ref, ng_ref, gng_ref, w_ref, y_ref):
    o_nat = _read_split(pl.program_id(0), on_ctx_ref, on_lat_ref)
    og = o_ref[...]
    parts = []
    for h in range(H_GLA):
        parts.append(_rms(og[:, h * DV_GLA:(h + 1) * DV_GLA], gng_ref[...]))
    merged = (jnp.concatenate(parts, axis=1) * _silu(r_ref[...])).astype(BF16)
    merged = _permute_chunks(_block_transpose_perm(), merged)
    y = _dot(merged, w_ref[0:V_W, :]) + _dot(o_nat.astype(BF16), w_ref[V_W:, :])
    y_ref[...] = x_ref[...] + mod_ref[0, 5:6, :] * _rms(y, ng_ref[3:4, :])


def _mixer_out(x, o_gla, r, o_ctx, o_lat, mod_l, ng_l, gng_l, w_out_bf):
    tile = lambda w: pl.BlockSpec((TM, w), lambda i: (i, 0))
    full = lambda a: pl.BlockSpec(a.shape, lambda i: (0,) * a.ndim)
    return pl.pallas_call(
        _out_kernel,
        grid=(N_ALL // TM,),
        in_specs=[
            tile(D_MODEL),
            tile(V_W),
            tile(V_W),
            pl.BlockSpec((TM, NAT_W), lambda i: (_ctx_tile(i), 0)),
            pl.BlockSpec((TM, NAT_W), lambda i: (_lat_tile(i), 0)),
            pl.BlockSpec((1, N_MOD, D_MODEL), lambda i: (_group_of_tile(i), 0, 0)),
            full(ng_l), full(gng_l), full(w_out_bf),
        ],
        out_specs=tile(D_MODEL),
        out_shape=jax.ShapeDtypeStruct((N_ALL, D_MODEL), F32),
        compiler_params=pltpu.CompilerParams(
            dimension_semantics=("arbitrary",), vmem_limit_bytes=VMEM_LIMIT),
        name="mixer_out",
    )(x, o_gla, r, o_ctx, o_lat, mod_l, ng_l, gng_l, w_out_bf)


def _rope_tables():
    quarter = DK_GLA // 4
    freqs = ROPE_BASE ** (-jnp.arange(quarter, dtype=F32) / quarter)
    t = jnp.arange(DEC_SEQ)
    ang_r = (t // GRID_W).astype(F32)[:, None] * freqs[None, :]
    ang_c = (t % GRID_W).astype(F32)[:, None] * freqs[None, :]
    cos_h = jnp.concatenate([jnp.cos(ang_r), jnp.cos(ang_r), jnp.cos(ang_c), jnp.cos(ang_c)], axis=1)
    sin_h = jnp.concatenate([-jnp.sin(ang_r), jnp.sin(ang_r), -jnp.sin(ang_c), jnp.sin(ang_c)], axis=1)
    pos_major = lambda a: a.reshape(-1, GLA_NB, GLA_CB, QK_W).transpose(0, 2, 1, 3).reshape(a.shape)
    cos_t = jnp.concatenate([pos_major(jnp.tile(cos_h, (1, H_GLA))), jnp.ones((TM, QK_W), F32)], axis=0)
    sin_t = jnp.concatenate([pos_major(jnp.tile(sin_h, (1, H_GLA))), jnp.zeros((TM, QK_W), F32)], axis=0)
    return cos_t, sin_t


def _swap_perm():
    quarter = DK_GLA // 4
    idx = np.arange(QK_W)
    blk = (idx // quarter) % 4
    return np.where(blk % 2 == 0, idx + quarter, idx - quarter)


def _pack_w_in(w_in):
    sizes = [QK_W, QK_W, V_W, V_W, 2 * GLA_RANK, 3 * NAT_W]
    cuts = np.cumsum([0] + sizes)
    wq, wk, wv, wr, wlr, wnat = [w_in[:, :, cuts[i]:cuts[i + 1]] for i in range(len(sizes))]
    swap = _swap_perm()
    pad = jnp.zeros((DEPTH, D_MODEL, LANES - 2 * GLA_RANK), w_in.dtype)
    return jnp.concatenate([wq, wk, wq[:, :, swap], wk[:, :, swap], wv, wr, wnat, wlr, pad], axis=2).astype(BF16)


def _gate_up_weights(gla_wa2_l, gla_ba_l):
    w_a = jnp.zeros((LANES, 2 * QK_W), F32)
    w_a = w_a.at[0:GLA_RANK, 0:QK_W].set(gla_wa2_l[0]).at[GLA_RANK:2 * GLA_RANK, QK_W:].set(gla_wa2_l[1])
    b_a = jnp.concatenate([gla_ba_l[0], gla_ba_l[1]])[None, :]
    return w_a.astype(BF16), b_a


def kernel(x_prompt, x_sample, cache_k, cache_v, state_gla, c, c_ctx, w_mod, b_mod, norm_g, ffn_w_in, ffn_w_out,
           w_in, gla_wa2, gla_ba, gla_norm_g, nat_rpb, w_out):
    cvecs = jnp.zeros((SUBLANES, D_MODEL), F32).at[0].set(c_ctx).at[1:1 + DEC_BATCH].set(c)
    mod = _modulation(cvecs, w_mod, b_mod)[:, :N_GROUPS].reshape(DEPTH, N_GROUPS, N_MOD, D_MODEL)

    w_out_bf = w_out.astype(BF16)
    cos_t, sin_t = _rope_tables()
    ck = cache_k.reshape(DEC_BATCH, DEPTH, PAST_LEN, NAT_W)
    cv = cache_v.reshape(DEC_BATCH, DEPTH, PAST_LEN, NAT_W)

    w_big = _pack_w_in(w_in)

    x = (x_prompt.reshape(N_CTX, D_MODEL), x_sample.reshape(N_LAT, D_MODEL))
    new_cache, s_list = (), []
    for l in range(DEPTH):
        mod_l, ng_l = mod[l], norm_g[l]
        x = _ffn(x, mod_l, ng_l, ffn_w_in, ffn_w_out, layer=l, which=0)

        w_a, b_a = _gate_up_weights(gla_wa2[l], gla_ba[l])
        q, k, v, r, g, qn, new_k, new_v, kn, vn = _project(
            x, mod_l, ng_l, w_big, w_a, b_a, cos_t, sin_t, new_cache, layer=l)
        new_cache = (new_k, new_v)

        s0 = jnp.concatenate([jnp.zeros((BATCH, 2, H_GLA, DK_GLA, DV_GLA), F32), state_gla[:, l]], axis=0)
        o_gla, s_fin = _gla(q, k, v, g, s0)
        o_ctx = _ctx_attention(qn, kn, vn)
        o_lat = _nat_attention(qn, kn, vn, ck, cv, nat_rpb[l], l)

        x = _mixer_out(x, o_gla, r, o_ctx, o_lat, mod_l, ng_l, gla_norm_g[l][None, :], w_out_bf[l])
        x = _ffn(x, mod_l, ng_l, ffn_w_in, ffn_w_out, layer=l, which=1, split_out=(l == DEPTH - 1))

        s_list.append(s_fin[:BATCH])

    y_prompt = x[0].reshape(BATCH, SEQ, D_MODEL)
    y_sample = x[1].reshape(DEC_BATCH, DEC_SEQ, D_MODEL)
    return (y_prompt, y_sample, new_cache[0], new_cache[1], jnp.stack(s_list, axis=1))
```

```python
import functools

import numpy as np
import jax
import jax.numpy as jnp
from jax import lax
from jax.experimental import pallas as pl
from jax.experimental.pallas import tpu as pltpu

D_MODEL = 1024
BATCH = 16
SEQ = 256
DEPTH = 2
DEC_BATCH = 2
DEC_SEQ = 1024
PAST_LEN = 512
GRID_W = 64
H_GLA = 4
DK_GLA = 64
DV_GLA = 128
GLA_RANK = 16
GATE_NORM = 16.0
H_NAT = 8
HD_NAT = 64
WIN_H = 8
WIN_W = 16
D_FF = 2816
N_MOD = 9
ROPE_BASE = 10000.0
EPS = 1e-6
NEG_INF = -1e30

F32 = jnp.float32
BF16 = jnp.bfloat16

N_CTX = BATCH * SEQ
N_LAT = DEC_BATCH * DEC_SEQ
N_ALL = N_CTX + N_LAT
N_GROUPS = 1 + DEC_BATCH
QK_W = H_GLA * DK_GLA
V_W = H_GLA * DV_GLA
NAT_W = H_NAT * HD_NAT
GRID_ROWS = DEC_SEQ // GRID_W

LANES = 128
SUBLANES = 8

TM = 512
FFN_TC = 256
FFN_NC = D_FF // FFN_TC
FFN_STAGES = 2
MOD_TN = 1152
GLA_T = 256
GLA_CB = 16
GLA_NB = GLA_T // GLA_CB
VMEM_LIMIT = 56 * 1024 * 1024

assert N_CTX % TM == 0 and DEC_SEQ % TM == 0 and D_FF % FFN_TC == 0 and FFN_TC % LANES == 0
assert SEQ == GLA_T and DEC_SEQ % GLA_T == 0


def _group_of_tile(i):
    return jnp.where(i < N_CTX // TM, 0, 1 + (i - N_CTX // TM) // (DEC_SEQ // TM))


def _dot(a, b):
    return jnp.dot(a, b, preferred_element_type=F32)


def _dot_nt(a, b):
    return lax.dot_general(a, b, (((1,), (1,)), ((), ())), preferred_element_type=F32)


def _dot_tn(a, b):
    return lax.dot_general(a, b, (((0,), (0,)), ((), ())), preferred_element_type=F32)


def _rms(x, g):
    ms = jnp.mean(x * x, axis=-1, keepdims=True)
    return x * lax.rsqrt(ms + EPS) * g


def _silu(x):
    return x * jax.nn.sigmoid(x)


def _mod_kernel(c_ref, w_ref, b_ref, o_ref):
    s = _silu(c_ref[...]).astype(BF16)
    o_ref[0] = _dot(s, w_ref[0].astype(BF16)) + b_ref[0]


def _modulation(cvecs, w_mod, b_mod):
    n_out = N_MOD * D_MODEL
    return pl.pallas_call(
        _mod_kernel,
        grid=(DEPTH, n_out // MOD_TN),
        in_specs=[
            pl.BlockSpec((SUBLANES, D_MODEL), lambda l, j: (0, 0)),
            pl.BlockSpec((1, D_MODEL, MOD_TN), lambda l, j: (l, 0, j)),
            pl.BlockSpec((1, 1, MOD_TN), lambda l, j: (l, 0, j)),
        ],
        out_specs=pl.BlockSpec((1, SUBLANES, MOD_TN), lambda l, j: (l, 0, j)),
        out_shape=jax.ShapeDtypeStruct((DEPTH, SUBLANES, n_out), F32),
        compiler_params=pltpu.CompilerParams(
            dimension_semantics=("arbitrary", "arbitrary"), vmem_limit_bytes=VMEM_LIMIT),
        name="modulation",
    )(cvecs, w_mod, b_mod.reshape(DEPTH, 1, n_out))


_CTX_TILES = N_CTX // TM
_LAT_TILES = N_LAT // TM


def _ctx_tile(i):
    return jnp.minimum(i, _CTX_TILES - 1)


def _lat_tile(i):
    return jnp.maximum(i - _CTX_TILES, 0)


def _read_split(i, ctx_ref, lat_ref):
    return jnp.where(i < _CTX_TILES, ctx_ref[...], lat_ref[...])


def _ffn_kernel(*refs, m0, n0, layer, which, split_in, split_out):
    n_x = 2 if split_in else 1
    x_refs, (mod_ref, ng_ref, win_hbm, wout_hbm) = refs[:n_x], refs[n_x:n_x + 4]
    n_o = 2 if split_out else 1
    o_refs = refs[n_x + 4:n_x + 4 + n_o]
    h_scr, acc_scr, x_scr, wg_bf, wu_bf, wo_bf, stage_g, stage_u, stage_o, sem = refs[n_x + 4 + n_o:]
    i = pl.program_id(0)

    def chunk_copies(c, slot):
        cols = pl.ds(c * FFN_TC, FFN_TC)
        up_cols = pl.ds(D_FF + c * FFN_TC, FFN_TC)
        return (
            pltpu.make_async_copy(win_hbm.at[layer, which, :, cols], stage_g.at[slot], sem.at[0, slot]),
            pltpu.make_async_copy(win_hbm.at[layer, which, :, up_cols], stage_u.at[slot], sem.at[1, slot]),
            pltpu.make_async_copy(wout_hbm.at[layer, which, cols, :], stage_o.at[slot], sem.at[2, slot]),
        )

    def accumulate(c):
        h = h_scr[...]
        gate = _dot(h, wg_bf[c])
        up = _dot(h, wu_bf[c])
        part = _dot((_silu(gate) * up).astype(BF16), wo_bf[c])
        if c == 0:
            acc_scr[...] = part
        else:
            acc_scr[...] += part

    x = _read_split(i, *x_refs) if split_in else x_refs[0][...]
    x_scr[...] = x
    gain = ng_ref[n0:n0 + 1, :] * (1.0 + mod_ref[0, m0 + 1:m0 + 2, :])
    h_scr[...] = (_rms(x, gain) + mod_ref[0, m0:m0 + 1, :]).astype(BF16)

    @pl.when(i == 0)
    def _():
        for c in range(min(FFN_STAGES, FFN_NC)):
            for cp in chunk_copies(c, c):
                cp.start()
        for c in range(FFN_NC):
            slot = c % FFN_STAGES
            for cp in chunk_copies(c, slot):
                cp.wait()
            wg_bf[c] = stage_g[slot].astype(BF16)
            wu_bf[c] = stage_u[slot].astype(BF16)
            wo_bf[c] = stage_o[slot].astype(BF16)
            if c + FFN_STAGES < FFN_NC:
                for cp in chunk_copies(c + FFN_STAGES, slot):
                    cp.start()
            accumulate(c)

    @pl.when(i > 0)
    def _():
        for c in range(FFN_NC):
            accumulate(c)

    gain = 0.5 * mod_ref[0, m0 + 2:m0 + 3, :] * ng_ref[n0 + 1:n0 + 2, :]
    out = x_scr[...] + _rms(acc_scr[...], gain)
    if split_out:
        @pl.when(i < _CTX_TILES)
        def _():
            o_refs[0][...] = out

        @pl.when(i >= _CTX_TILES)
        def _():
            o_refs[1][...] = out
    else:
        o_refs[0][...] = out


def _ffn(x, mod_l, ng_l, w_in, w_out, *, layer, which, split_out=False):
    m0, n0 = (0, 0) if which == 0 else (6, 4)
    split_in = isinstance(x, tuple)
    tile = pl.BlockSpec((TM, D_MODEL), lambda i: (i, 0))
    ctx_tile = pl.BlockSpec((TM, D_MODEL), lambda i: (_ctx_tile(i), 0))
    lat_tile = pl.BlockSpec((TM, D_MODEL), lambda i: (_lat_tile(i), 0))
    if split_out:
        out_specs = [ctx_tile, lat_tile]
        out_shape = [jax.ShapeDtypeStruct((N_CTX, D_MODEL), F32), jax.ShapeDtypeStruct((N_LAT, D_MODEL), F32)]
    else:
        out_specs = tile
        out_shape = jax.ShapeDtypeStruct((N_ALL, D_MODEL), F32)
    return pl.pallas_call(
        functools.partial(_ffn_kernel, m0=m0, n0=n0, layer=layer, which=which, split_in=split_in,
                          split_out=split_out),
        grid=(N_ALL // TM,),
        in_specs=([ctx_tile, lat_tile] if split_in else [tile]) + [
            pl.BlockSpec((1, N_MOD, D_MODEL), lambda i: (_group_of_tile(i), 0, 0)),
            pl.BlockSpec((6, D_MODEL), lambda i: (0, 0)),
            pl.BlockSpec(memory_space=pl.ANY),
            pl.BlockSpec(memory_space=pl.ANY),
        ],
        out_specs=out_specs,
        out_shape=out_shape,
        scratch_shapes=[
            pltpu.VMEM((TM, D_MODEL), BF16),
            pltpu.VMEM((TM, D_MODEL), F32),
            pltpu.VMEM((TM, D_MODEL), F32),
            pltpu.VMEM((FFN_NC, D_MODEL, FFN_TC), BF16),
            pltpu.VMEM((FFN_NC, D_MODEL, FFN_TC), BF16),
            pltpu.VMEM((FFN_NC, FFN_TC, D_MODEL), BF16),
            pltpu.VMEM((FFN_STAGES, D_MODEL, FFN_TC), F32),
            pltpu.VMEM((FFN_STAGES, D_MODEL, FFN_TC), F32),
            pltpu.VMEM((FFN_STAGES, FFN_TC, D_MODEL), F32),
            pltpu.SemaphoreType.DMA((3, FFN_STAGES)),
        ],
        compiler_params=pltpu.CompilerParams(
            dimension_semantics=("arbitrary",), vmem_limit_bytes=VMEM_LIMIT),
        name="ffn",
    )(*(x if split_in else (x,)), mod_l, ng_l, w_in, w_out)


def _block_transpose_perm():
    r = lax.broadcasted_iota(jnp.int32, (GLA_T, GLA_T), 0)
    c = lax.broadcasted_iota(jnp.int32, (GLA_T, GLA_T), 1)
    return jnp.where(c == (r & (GLA_CB - 1)) * GLA_NB + (r >> 4), 1.0, 0.0).astype(BF16)


def _permute_chunks(perm, a):
    return jnp.concatenate([_dot(perm, a[c * GLA_T:(c + 1) * GLA_T, :]).astype(BF16)
                            for c in range(a.shape[0] // GLA_T)], axis=0)


_P_QK = 0
_P_VR = 4 * QK_W
_P_NAT = _P_VR + 2 * V_W
_P_LR = _P_NAT + 3 * NAT_W
_P_END = _P_LR + LANES


_GLA_COLS = 2 * QK_W + 2 * V_W
_REST_COLS = 2 * GLA_RANK + 3 * NAT_W
_REST_TILES = -(-_REST_COLS // LANES)
_PREP_ROWS = 128
assert _GLA_COLS % LANES == 0 and _GLA_COLS + _REST_COLS == 3104 and D_MODEL % _PREP_ROWS == 0


def _proj_prepare_weights(win_hbm, tail_ref, layer, w_bf, stage, sem):
    lane = lax.broadcasted_iota(jnp.int32, (_PREP_ROWS, LANES), 1)
    first_of_pair = ((lane >> 4) & 1) == 0
    shift = 2 * GLA_RANK

    head = pltpu.make_async_copy(win_hbm.at[layer, :, pl.ds(0, _GLA_COLS)], stage.at[:, pl.ds(0, _GLA_COLS)],
                                 sem.at[0])
    head.start()
    head.wait()
    for r0 in range(0, D_MODEL, _PREP_ROWS):
        rows = slice(r0, r0 + _PREP_ROWS)
        w_bf[rows, _P_QK:_P_QK + 2 * QK_W] = stage[rows, 0:2 * QK_W].astype(BF16)
        for t in range(2 * QK_W // LANES):
            tile = stage[rows, t * LANES:(t + 1) * LANES]
            partner = jnp.where(first_of_pair, pltpu.roll(tile, LANES - 16, axis=1), pltpu.roll(tile, 16, axis=1))
            w_bf[rows, _P_QK + 2 * QK_W + t * LANES:_P_QK + 2 * QK_W + (t + 1) * LANES] = partner.astype(BF16)
        w_bf[rows, _P_VR:_P_NAT] = stage[rows, 2 * QK_W:_GLA_COLS].astype(BF16)

    whole = (_REST_TILES - 1) * LANES
    stage[:, whole:] = tail_ref[0]
    rest = pltpu.make_async_copy(win_hbm.at[layer, :, pl.ds(_GLA_COLS, whole)], stage.at[:, pl.ds(0, whole)],
                                 sem.at[1])
    rest.start()
    rest.wait()
    for r0 in range(0, D_MODEL, _PREP_ROWS):
        rows = slice(r0, r0 + _PREP_ROWS)
        w_bf[rows, _P_LR:_P_END] = jnp.where(lane < shift, stage[rows, 0:LANES], 0.0).astype(BF16)
        for t in range(3 * NAT_W // LANES):
            lo = pltpu.roll(stage[rows, t * LANES:(t + 1) * LANES], LANES - shift, axis=1)
            hi = pltpu.roll(stage[rows, (t + 1) * LANES:(t + 2) * LANES], LANES - shift, axis=1)
            w_bf[rows, _P_NAT + t * LANES:_P_NAT + (t + 1) * LANES] = jnp.where(lane < LANES - shift, lo,
                                                                                  hi).astype(BF16)


def _proj_kernel(x_ref, mod_ref, ng_ref, win_hbm, tail_ref, wa_ref, ba_ref, cos_ref, sin_ref, *rest, layer):
    q_ref, k_ref, v_ref, r_ref, g_ref, qn_ref, kn_ctx_ref, vn_ctx_ref, kn_ref, vn_ref = rest[-13:-3]
    w_ref, stage, sem = rest[-3:]
    i = pl.program_id(0)

    @pl.when(i == 0)
    def _():
        _proj_prepare_weights(win_hbm, tail_ref, layer, w_ref, stage, sem)

    h = _rms(x_ref[...], ng_ref[2:3, :])
    h = (h * (1.0 + mod_ref[0, 4:5, :]) + mod_ref[0, 3:4, :]).astype(BF16)
    hp = _permute_chunks(_block_transpose_perm(), h)

    lr = _dot(hp, w_ref[:, _P_LR:_P_END]).astype(BF16)
    z = _dot(lr, wa_ref[...]) + ba_ref[...]
    g = (jnp.minimum(z, 0.0) - jnp.log1p(jnp.exp(-jnp.abs(z)))) * (1.0 / GATE_NORM)
    g_ref[0] = g[:, 0:QK_W]
    g_ref[1] = g[:, QK_W:2 * QK_W]

    qk = _dot(hp, w_ref[:, _P_QK:_P_VR])
    cos = cos_ref[...]
    sin = sin_ref[...]
    q_ref[...] = (qk[:, 0:QK_W] * cos + qk[:, 2 * QK_W:3 * QK_W] * sin) * (DK_GLA ** -0.5)
    k_ref[...] = qk[:, QK_W:2 * QK_W] * cos + qk[:, 3 * QK_W:4 * QK_W] * sin

    vr = _dot(hp, w_ref[:, _P_VR:_P_NAT])
    v_ref[...] = vr[:, 0:V_W]
    r_ref[...] = vr[:, V_W:2 * V_W]

    nat = _dot(h, w_ref[:, _P_NAT:_P_LR])
    qn_ref[...] = nat[:, 0:NAT_W].astype(BF16)
    kn_ref[...] = nat[:, NAT_W:2 * NAT_W].astype(BF16)
    vn_ref[...] = nat[:, 2 * NAT_W:3 * NAT_W].astype(BF16)

    @pl.when(i < _CTX_TILES)
    def _():
        heads = (TM // SEQ, SEQ, H_NAT, HD_NAT)
        kn_ctx_ref[:, 0] = nat[:, NAT_W:2 * NAT_W].reshape(heads)
        vn_ctx_ref[:, 0] = nat[:, 2 * NAT_W:3 * NAT_W].reshape(heads)
        for later in range(1, kn_ctx_ref.shape[1]):
            kn_ctx_ref[:, later] = jnp.zeros(heads, F32)
            vn_ctx_ref[:, later] = jnp.zeros(heads, F32)


def _rope_table_block(i):
    lat_tiles = DEC_SEQ // TM
    return jnp.where(i < N_CTX // TM, lat_tiles, (i - N_CTX // TM) % lat_tiles)


def _project(x, mod_l, ng_l, w_in, w_a, b_a, cos_t, sin_t, new_cache, *, layer):
    tile = lambda w: pl.BlockSpec((TM, w), lambda i: (i, 0))
    if new_cache:
        ctx_heads = pl.BlockSpec((TM // SEQ, 1, SEQ, H_NAT, HD_NAT), lambda i: (_ctx_tile(i), layer, 0, 0, 0))
    else:
        ctx_heads = pl.BlockSpec((TM // SEQ, DEPTH, SEQ, H_NAT, HD_NAT), lambda i: (_ctx_tile(i), 0, 0, 0, 0))
    full = lambda a: pl.BlockSpec(a.shape, lambda i: (0,) * a.ndim)
    n_in = 9
    tail0 = _GLA_COLS + (_REST_TILES - 1) * LANES
    w_tail = jnp.pad(w_in[:, :, tail0:], ((0, 0), (0, 0), (0, LANES - (w_in.shape[2] - tail0))))
    return pl.pallas_call(
        functools.partial(_proj_kernel, layer=layer),
        grid=(N_ALL // TM,),
        in_specs=[
            tile(D_MODEL),
            pl.BlockSpec((1, N_MOD, D_MODEL), lambda i: (_group_of_tile(i), 0, 0)),
            full(ng_l),
            pl.BlockSpec(memory_space=pl.ANY),
            pl.BlockSpec((1, D_MODEL, LANES), lambda i: (layer, 0, 0)),
            full(w_a), full(b_a),
            pl.BlockSpec((TM, QK_W), lambda i: (_rope_table_block(i), 0)),
            pl.BlockSpec((TM, QK_W), lambda i: (_rope_table_block(i), 0)),
        ] + [pl.BlockSpec(memory_space=pl.ANY)] * len(new_cache),
        input_output_aliases={n_in + n: 6 + n for n in range(len(new_cache))},
        out_specs=[
            tile(QK_W), tile(QK_W), tile(V_W), tile(V_W),
            pl.BlockSpec((2, TM, QK_W), lambda i: (0, i, 0)),
            tile(NAT_W), ctx_heads, ctx_heads, tile(NAT_W), tile(NAT_W),
        ],
        out_shape=[
            jax.ShapeDtypeStruct((N_ALL, QK_W), F32), jax.ShapeDtypeStruct((N_ALL, QK_W), F32),
            jax.ShapeDtypeStruct((N_ALL, V_W), F32), jax.ShapeDtypeStruct((N_ALL, V_W), F32),
            jax.ShapeDtypeStruct((2, N_ALL, QK_W), F32),
            jax.ShapeDtypeStruct((N_ALL, NAT_W), BF16),
            jax.ShapeDtypeStruct((BATCH, DEPTH, SEQ, H_NAT, HD_NAT), F32),
            jax.ShapeDtypeStruct((BATCH, DEPTH, SEQ, H_NAT, HD_NAT), F32),
            jax.ShapeDtypeStruct((N_ALL, NAT_W), BF16), jax.ShapeDtypeStruct((N_ALL, NAT_W), BF16),
        ],
        scratch_shapes=[
            pltpu.VMEM((D_MODEL, _P_END), BF16),
            pltpu.VMEM((D_MODEL, _REST_TILES * LANES), F32),
            pltpu.SemaphoreType.DMA((2,)),
        ],
        compiler_params=pltpu.CompilerParams(
            dimension_semantics=("arbitrary",), vmem_limit_bytes=VMEM_LIMIT),
        name="mixer_proj",
    )(x, mod_l, ng_l, w_in, w_tail, w_a, b_a, cos_t, sin_t, *new_cache)


def _gla_tables():
    rows = []
    seq_specs = [(b * (SEQ // GLA_T), SEQ // GLA_T) for b in range(BATCH)]
    seq_specs += [(N_CTX // GLA_T + b * (DEC_SEQ // GLA_T), DEC_SEQ // GLA_T) for b in range(DEC_BATCH)]
    for sid, (blk0, nchunk) in enumerate(seq_specs):
        for direction in (0, 1):
            order = range(nchunk) if direction == 0 else range(nchunk - 1, -1, -1)
            for n, c in enumerate(order):
                out_blk = blk0 + (c if direction == 1 or nchunk == 1 else nchunk - 1)
                rows.append((blk0 + c, direction, int(n == 0), sid, out_blk, c if nchunk > 1 else -1))
    return np.asarray(rows, dtype=np.int32).T.copy()


_GLA_TAB = _gla_tables()
_GLA_ITEMS = _GLA_TAB.shape[1]
_GLA_NSEQ = BATCH + DEC_BATCH
_GLA_PAIR_ROWS = GLA_NB * GLA_CB * (GLA_CB + 1) // 2


def _gla_item(direction, q_ref, k_ref, v_ref, g_ref, res, s_scr, cp, sstk, p_scr, w_scr):
    T, CB, NB = GLA_T, GLA_CB, GLA_NB
    fwd = direction == 0
    slab = lambda i: slice(i * NB, (i + 1) * NB)

    cum = None
    for i in (range(CB) if fwd else range(CB - 1, -1, -1)):
        gi = g_ref[0, slab(i), :]
        cum = gi if cum is None else cum + gi
        cp[slab(i), :] = cum
    total = cum
    cum_all = cp[...]
    qt = q_ref[...] * jnp.exp(cum_all)
    kh = k_ref[...] * jnp.exp(jnp.concatenate([total] * CB, axis=0) - cum_all)
    dec_t = jnp.concatenate([jnp.exp(total), jnp.zeros((LANES - NB, QK_W), F32)], axis=0).T

    erow = lax.broadcasted_iota(jnp.int32, (QK_W, QK_W), 0)
    ecol = lax.broadcasted_iota(jnp.int32, (QK_W, QK_W), 1)
    head_sum = jnp.where((erow >> 6) == (ecol >> 6), 1.0, 0.0).astype(BF16)

    key_positions = lambda i: range(i + 1) if fwd else range(i, CB)
    r0 = 0
    for i in range(CB):
        qi = q_ref[slab(i), :]
        ci = cp[slab(i), :]
        for j in key_positions(i):
            e = jnp.exp(ci - cp[slab(j), :])
            p_scr[r0:r0 + NB, :] = (qi * k_ref[slab(j), :] * e).astype(BF16)
            r0 += NB
    w_scr[...] = _dot(p_scr[...], head_sum)
    first_half = lax.broadcasted_iota(jnp.int32, (NB, LANES), 1) < DK_GLA
    r0 = 0
    for i in range(CB):
        acc = None
        for j in key_positions(i):
            spread = []
            for pair in range(H_GLA // 2):
                tile = w_scr[r0:r0 + NB, pair * LANES:(pair + 1) * LANES]
                other = pltpu.roll(tile, DK_GLA, axis=1)
                spread += [jnp.where(first_half, tile, other), jnp.where(first_half, other, tile)]
            term = jnp.concatenate(spread, axis=1) * v_ref[slab(j), :]
            acc = term if acc is None else acc + term
            r0 += NB
        res[slab(i), :] = acc

    kht = kh.T.astype(BF16)
    v_bf = v_ref[...].astype(BF16)
    key_blk = lax.broadcasted_iota(jnp.int32, (DK_GLA, T), 1) & (NB - 1)
    row_blk = lax.broadcasted_iota(jnp.int32, (T, LANES), 0) & (NB - 1)
    lane_half = lax.broadcasted_iota(jnp.int32, (T, LANES), 1) >> 6
    order = range(NB) if fwd else range(NB - 1, -1, -1)
    for h in range(H_GLA):
        kh_h = kht[h * DK_GLA:(h + 1) * DK_GLA, :]
        kv = _dot(jnp.concatenate([jnp.where(key_blk == b, kh_h, 0) for b in range(NB)], axis=0),
                  v_bf[:, h * DV_GLA:(h + 1) * DV_GLA])
        s = s_scr[h]
        for b in order:
            sstk[h, b * DK_GLA:(b + 1) * DK_GLA, :] = s.astype(BF16)
            s = dec_t[h * DK_GLA:(h + 1) * DK_GLA, b:b + 1] * s + kv[b * DK_GLA:(b + 1) * DK_GLA, :]
        s_scr[h] = s

        pair_tile = qt[:, (h // 2) * LANES:(h // 2 + 1) * LANES]
        both = jnp.where(lane_half == h % 2, pair_tile, pltpu.roll(pair_tile, DK_GLA, axis=1))
        lhs = jnp.concatenate([jnp.where(row_blk == 2 * j + lane_half, both, 0.0).astype(BF16)
                               for j in range(NB // 2)], axis=1)
        res[:, h * DV_GLA:(h + 1) * DV_GLA] += _dot(lhs, sstk[h])


def _gla_kernel(tab_ref, q_ref, k_ref, v_ref, g_ref, s0_ref, o_ref, so_ref,
                s_scr, cp, sstk, p_scr, w_scr, res, held):
    it = pl.program_id(0)
    direction = tab_ref[1, it]
    slot = tab_ref[5, it]

    @pl.when(tab_ref[2, it] == 1)
    def _():
        s_scr[...] = s0_ref[0, 0]

    for d in (0, 1):
        @pl.when(direction == d)
        def _(d=d):
            _gla_item(d, q_ref, k_ref, v_ref, g_ref, res, s_scr, cp, sstk, p_scr, w_scr)

    @pl.when((direction == 0) & (slot < 0))
    def _():
        o_ref[...] = res[...]

    @pl.when((direction == 1) & (slot < 0))
    def _():
        o_ref[...] += res[...]

    @pl.when((direction == 0) & (slot >= 0))
    def _():
        held[jnp.maximum(slot, 0)] = res[...]

    @pl.when((direction == 1) & (slot >= 0))
    def _():
        o_ref[...] = held[jnp.maximum(slot, 0)] + res[...]

    so_ref[0, 0] = s_scr[...]


def _gla(q, k, v, g, s0):
    tok = lambda w: pl.BlockSpec((GLA_T, w), lambda it, tab: (tab[0, it], 0))
    state = pl.BlockSpec((1, 1, H_GLA, DK_GLA, DV_GLA), lambda it, tab: (tab[3, it], tab[1, it], 0, 0, 0))
    grid_spec = pltpu.PrefetchScalarGridSpec(
        num_scalar_prefetch=1,
        grid=(_GLA_ITEMS,),
        in_specs=[
            tok(QK_W), tok(QK_W), tok(V_W),
            pl.BlockSpec((1, GLA_T, QK_W), lambda it, tab: (tab[1, it], tab[0, it], 0)),
            state,
        ],
        out_specs=[
            pl.BlockSpec((GLA_T, V_W), lambda it, tab: (tab[4, it], 0)),
            state,
        ],
        scratch_shapes=[
            pltpu.VMEM((H_GLA, DK_GLA, DV_GLA), F32),
            pltpu.VMEM((GLA_T, QK_W), F32),
            pltpu.VMEM((H_GLA, GLA_NB * DK_GLA, DV_GLA), BF16),
            pltpu.VMEM((_GLA_PAIR_ROWS, QK_W), BF16),
            pltpu.VMEM((_GLA_PAIR_ROWS, QK_W), F32),
            pltpu.VMEM((GLA_T, V_W), F32),
            pltpu.VMEM((DEC_SEQ // GLA_T, GLA_T, V_W), F32),
        ],
    )
    return pl.pallas_call(
        _gla_kernel,
        grid_spec=grid_spec,
        out_shape=[
            jax.ShapeDtypeStruct((N_ALL, V_W), F32),
            jax.ShapeDtypeStruct((_GLA_NSEQ, 2, H_GLA, DK_GLA, DV_GLA), F32),
        ],
        compiler_params=pltpu.CompilerParams(
            dimension_semantics=("arbitrary",), vmem_limit_bytes=VMEM_LIMIT),
        name="gla",
    )(jnp.asarray(_GLA_TAB), q, k, v, g, s0)


def _softmax_pv(s_list, v_list):
    m = s_list[0].max(axis=-1, keepdims=True)
    for s in s_list[1:]:
        m = jnp.maximum(m, s.max(axis=-1, keepdims=True))
    num = None
    den = None
    for s, vv in zip(s_list, v_list):
        e = jnp.exp(s - m)
        den = e.sum(axis=-1, keepdims=True) if den is None else den + e.sum(axis=-1, keepdims=True)
        pv = _dot(e.astype(BF16), vv)
        num = pv if num is None else num + pv
    return num / den


def _ctx_attn_kernel(q_ref, k_ref, v_ref, o_ref):
    scale = HD_NAT ** -0.5
    lane = lax.broadcasted_iota(jnp.int32, (SEQ, LANES), 1)
    for t in range(NAT_W // LANES):
        sl = slice(t * LANES, (t + 1) * LANES)
        qt = q_ref[:, sl]
        kt = k_ref[:, sl].astype(BF16)
        vt = v_ref[:, sl].astype(BF16)
        out = jnp.zeros((SEQ, LANES), F32)
        for half in range(LANES // HD_NAT):
            mine = (lane >> 6) == half
            s = _dot_nt(jnp.where(mine, qt, 0).astype(BF16), kt) * scale
            out = jnp.where(mine, _softmax_pv([s], [vt]), out)
        o_ref[:, sl] = out.astype(o_ref.dtype)


def _ctx_attention(qn, kn, vn):
    spec = pl.BlockSpec((SEQ, NAT_W), lambda b: (b, 0))
    return pl.pallas_call(
        _ctx_attn_kernel,
        grid=(BATCH,),
        in_specs=[spec, spec, spec],
        out_specs=spec,
        out_shape=jax.ShapeDtypeStruct((N_CTX, NAT_W), BF16),
        compiler_params=pltpu.CompilerParams(
            dimension_semantics=("arbitrary",), vmem_limit_bytes=VMEM_LIMIT),
        name="ctx_attention",
    )(qn, kn, vn)


_NAT_QROWS = 4
_NAT_GROUPS = GRID_ROWS // _NAT_QROWS
_NAT_KROWS = WIN_H + _NAT_QROWS
_NAT_Q = _NAT_QROWS * GRID_W
_NAT_KEYS = _NAT_KROWS * GRID_W
_NAT_DR = 2 * WIN_H - 1
_NAT_DC = 2 * WIN_W - 1


def _nat_key_row0(r):
    return jnp.clip(r - WIN_H // 2, 0, GRID_ROWS - WIN_H)


def _nat_build_bias(rpb_ref, tz_scr):
    c = lax.broadcasted_iota(jnp.int32, (GRID_W, LANES), 0)
    lane = lax.broadcasted_iota(jnp.int32, (GRID_W, LANES), 1)
    kc = lane & (GRID_W - 1)
    second = (lane >> 6) == 1
    win_start = jnp.clip(c - WIN_W // 2, 0, GRID_W - WIN_W)
    valid = (kc >= win_start) & (kc < win_start + WIN_W)

    def one_row(n, carry):
        dr = n >> 3
        h = n & (H_NAT - 1)
        src = jnp.broadcast_to(rpb_ref[pl.ds(h * _NAT_DR + dr, 1), :], (GRID_W, LANES))
        rolled = pltpu.roll(src, LANES - (WIN_W - 1), axis=1, stride=1, stride_axis=0)
        tz_scr[dr, h] = jnp.where(valid, rolled, NEG_INF)
        return carry

    lax.fori_loop(0, _NAT_DR * H_NAT, one_row, 0)

    def pair_rows(n, carry):
        dr = n >> 3
        h = n & (H_NAT - 1)
        tz_scr[dr, h] = jnp.where(second, tz_scr[dr + 1, h], tz_scr[dr, h])
        return carry

    lax.fori_loop(0, (_NAT_DR - 1) * H_NAT, pair_rows, 0)


def _nat_kernel(rpb_ref, q_ref, k_ref, v_ref, ck_ref, cv_ref, o_ref, tz_scr, bias_scr):
    grp = pl.program_id(1)

    @pl.when((pl.program_id(0) == 0) & (grp == 0))
    def _():
        _nat_build_bias(rpb_ref, tz_scr)

    scale = HD_NAT ** -0.5
    krow0 = jnp.clip(_NAT_QROWS * grp - WIN_H // 2, 0, GRID_ROWS - _NAT_KROWS)
    k0 = pl.multiple_of(krow0 * GRID_W, GRID_W)
    lane_q = lax.broadcasted_iota(jnp.int32, (_NAT_Q, LANES), 1)
    lane_b = lax.broadcasted_iota(jnp.int32, (GRID_W, LANES), 1)
    neg = jnp.full((GRID_W, LANES), NEG_INF, F32)
    for t in range(NAT_W // LANES):
        sl = slice(t * LANES, (t + 1) * LANES)
        qt = q_ref[:, sl]
        kw = k_ref[pl.ds(k0, _NAT_KEYS), sl].astype(BF16)
        vw = v_ref[pl.ds(k0, _NAT_KEYS), sl].astype(BF16)
        ck = ck_ref[0, 0, :, sl].astype(BF16)
        cv = cv_ref[0, 0, :, sl].astype(BF16)
        out = jnp.zeros((_NAT_Q, LANES), F32)
        for half in range(LANES // HD_NAT):
            h = 2 * t + half
            for qr in range(_NAT_QROWS):
                r = _NAT_QROWS * grp + qr
                lo = _nat_key_row0(r)
                for kp in range(_NAT_KROWS // 2):
                    kr = krow0 + 2 * kp
                    tile = tz_scr[jnp.clip(kr - r + (WIN_H - 1), 0, _NAT_DR - 1), h]
                    ok_a = ((kr >= lo) & (kr < lo + WIN_H)).astype(jnp.int32)
                    ok_b = ((kr + 1 >= lo) & (kr + 1 < lo + WIN_H)).astype(jnp.int32)
                    ok = jnp.where(lane_b < GRID_W, ok_a, ok_b) == 1
                    bias_scr[qr * GRID_W:(qr + 1) * GRID_W, kp * LANES:(kp + 1) * LANES] = jnp.where(ok, tile, neg)
            mine = (lane_q >> 6) == half
            qm = jnp.where(mine, qt, 0).astype(BF16)
            s_win = _dot_nt(qm, kw) * scale + bias_scr[...]
            s_ctx = _dot_nt(qm, ck) * scale
            out = jnp.where(mine, _softmax_pv([s_win, s_ctx], [vw, cv]), out)
        o_ref[:, sl] = out.astype(o_ref.dtype)


def _nat_attention(qn, kn, vn, cache_k, cache_v, rpb_l, layer):
    lat0 = N_CTX // _NAT_Q
    half = jnp.pad(rpb_l.reshape(H_NAT * _NAT_DR, _NAT_DC), ((0, 0), (0, GRID_W - _NAT_DC)))
    rpb_rows = jnp.concatenate([half, half], axis=1)
    return pl.pallas_call(
        _nat_kernel,
        grid=(DEC_BATCH, _NAT_GROUPS),
        in_specs=[
            pl.BlockSpec((H_NAT * _NAT_DR, LANES), lambda b, g: (0, 0)),
            pl.BlockSpec((_NAT_Q, NAT_W), lambda b, g: (lat0 + b * _NAT_GROUPS + g, 0)),
            pl.BlockSpec((DEC_SEQ, NAT_W), lambda b, g: (N_CTX // DEC_SEQ + b, 0)),
            pl.BlockSpec((DEC_SEQ, NAT_W), lambda b, g: (N_CTX // DEC_SEQ + b, 0)),
            pl.BlockSpec((1, 1, PAST_LEN, NAT_W), lambda b, g: (b, layer, 0, 0)),
            pl.BlockSpec((1, 1, PAST_LEN, NAT_W), lambda b, g: (b, layer, 0, 0)),
        ],
        out_specs=pl.BlockSpec((_NAT_Q, NAT_W), lambda b, g: (b * _NAT_GROUPS + g, 0)),
        out_shape=jax.ShapeDtypeStruct((N_LAT, NAT_W), BF16),
        scratch_shapes=[
            pltpu.VMEM((_NAT_DR, H_NAT, GRID_W, LANES), F32),
            pltpu.VMEM((_NAT_Q, _NAT_KEYS), F32),
        ],
        compiler_params=pltpu.CompilerParams(
            dimension_semantics=("arbitrary", "arbitrary"), vmem_limit_bytes=VMEM_LIMIT),
        name="nat_attention",
    )(rpb_rows, qn, kn, vn, cache_k, cache_v)


def _out_kernel(x_ref, o_ref, r_ref, on_ctx_ref, on_lat_ref, mod_ref, ng_ref, gng_ref, w_ref, y_ref):
    o_nat = _read_split(pl.program_id(0), on_ctx_ref, on_lat_ref)
    og = o_ref[...]
    parts = []
    for h in range(H_GLA):
        parts.append(_rms(og[:, h * DV_GLA:(h + 1) * DV_GLA], gng_ref[...]))
    merged = (jnp.concatenate(parts, axis=1) * _silu(r_ref[...])).astype(BF16)
    merged = _permute_chunks(_block_transpose_perm(), merged)
    y = _dot(merged, w_ref[0:V_W, :]) + _dot(o_nat.astype(BF16), w_ref[V_W:, :])
    y_ref[...] = x_ref[...] + mod_ref[0, 5:6, :] * _rms(y, ng_ref[3:4, :])


def _mixer_out(x, o_gla, r, o_ctx, o_lat, mod_l, ng_l, gng_l, w_out_bf):
    tile = lambda w: pl.BlockSpec((TM, w), lambda i: (i, 0))
    full = lambda a: pl.BlockSpec(a.shape, lambda i: (0,) * a.ndim)
    return pl.pallas_call(
        _out_kernel,
        grid=(N_ALL // TM,),
        in_specs=[
            tile(D_MODEL),
            tile(V_W),
            tile(V_W),
            pl.BlockSpec((TM, NAT_W), lambda i: (_ctx_tile(i), 0)),
            pl.BlockSpec((TM, NAT_W), lambda i: (_lat_tile(i), 0)),
            pl.BlockSpec((1, N_MOD, D_MODEL), lambda i: (_group_of_tile(i), 0, 0)),
            full(ng_l), full(gng_l), full(w_out_bf),
        ],
        out_specs=tile(D_MODEL),
        out_shape=jax.ShapeDtypeStruct((N_ALL, D_MODEL), F32),
        compiler_params=pltpu.CompilerParams(
            dimension_semantics=("arbitrary",), vmem_limit_bytes=VMEM_LIMIT),
        name="mixer_out",
    )(x, o_gla, r, o_ctx, o_lat, mod_l, ng_l, gng_l, w_out_bf)


def _rope_tables():
    quarter = DK_GLA // 4
    freqs = ROPE_BASE ** (-jnp.arange(quarter, dtype=F32) / quarter)
    t = jnp.arange(DEC_SEQ)
    ang_r = (t // GRID_W).astype(F32)[:, None] * freqs[None, :]
    ang_c = (t % GRID_W).astype(F32)[:, None] * freqs[None, :]
    cos_h = jnp.concatenate([jnp.cos(ang_r), jnp.cos(ang_r), jnp.cos(ang_c), jnp.cos(ang_c)], axis=1)
    sin_h = jnp.concatenate([-jnp.sin(ang_r), jnp.sin(ang_r), -jnp.sin(ang_c), jnp.sin(ang_c)], axis=1)
    pos_major = lambda a: a.reshape(-1, GLA_NB, GLA_CB, QK_W).transpose(0, 2, 1, 3).reshape(a.shape)
    cos_t = jnp.concatenate([pos_major(jnp.tile(cos_h, (1, H_GLA))), jnp.ones((TM, QK_W), F32)], axis=0)
    sin_t = jnp.concatenate([pos_major(jnp.tile(sin_h, (1, H_GLA))), jnp.zeros((TM, QK_W), F32)], axis=0)
    return cos_t, sin_t


def _gate_up_weights(gla_wa2_l, gla_ba_l):
    w_a = jnp.zeros((LANES, 2 * QK_W), F32)
    w_a = w_a.at[0:GLA_RANK, 0:QK_W].set(gla_wa2_l[0]).at[GLA_RANK:2 * GLA_RANK, QK_W:].set(gla_wa2_l[1])
    b_a = jnp.concatenate([gla_ba_l[0], gla_ba_l[1]])[None, :]
    return w_a.astype(BF16), b_a


def kernel(x_prompt, x_sample, cache_k, cache_v, state_gla, c, c_ctx, w_mod, b_mod, norm_g, ffn_w_in, ffn_w_out,
           w_in, gla_wa2, gla_ba, gla_norm_g, nat_rpb, w_out):
    cvecs = jnp.zeros((SUBLANES, D_MODEL), F32).at[0].set(c_ctx).at[1:1 + DEC_BATCH].set(c)
    mod = _modulation(cvecs, w_mod, b_mod)[:, :N_GROUPS].reshape(DEPTH, N_GROUPS, N_MOD, D_MODEL)

    w_out_bf = w_out.astype(BF16)
    cos_t, sin_t = _rope_tables()
    ck = cache_k.reshape(DEC_BATCH, DEPTH, PAST_LEN, NAT_W)
    cv = cache_v.reshape(DEC_BATCH, DEPTH, PAST_LEN, NAT_W)

    x = (x_prompt.reshape(N_CTX, D_MODEL), x_sample.reshape(N_LAT, D_MODEL))
    new_cache, s_list = (), []
    for l in range(DEPTH):
        mod_l, ng_l = mod[l], norm_g[l]
        x = _ffn(x, mod_l, ng_l, ffn_w_in, ffn_w_out, layer=l, which=0)

        w_a, b_a = _gate_up_weights(gla_wa2[l], gla_ba[l])
        q, k, v, r, g, qn, new_k, new_v, kn, vn = _project(
            x, mod_l, ng_l, w_in, w_a, b_a, cos_t, sin_t, new_cache, layer=l)
        new_cache = (new_k, new_v)

        s0 = jnp.concatenate([jnp.zeros((BATCH, 2, H_GLA, DK_GLA, DV_GLA), F32), state_gla[:, l]], axis=0)
        o_gla, s_fin = _gla(q, k, v, g, s0)
        o_ctx = _ctx_attention(qn, kn, vn)
        o_lat = _nat_attention(qn, kn, vn, ck, cv, nat_rpb[l], l)

        x = _mixer_out(x, o_gla, r, o_ctx, o_lat, mod_l, ng_l, gla_norm_g[l][None, :], w_out_bf[l])
        x = _ffn(x, mod_l, ng_l, ffn_w_in, ffn_w_out, layer=l, which=1, split_out=(l == DEPTH - 1))

        s_list.append(s_fin[:BATCH])

    y_prompt = x[0].reshape(BATCH, SEQ, D_MODEL)
    y_sample = x[1].reshape(DEC_BATCH, DEC_SEQ, D_MODEL)
    return (y_prompt, y_sample, new_cache[0], new_cache[1], jnp.stack(s_list, axis=1))
```

```python
import functools

import numpy as np
import jax
import jax.numpy as jnp
from jax import lax
from jax.experimental import pallas as pl
from jax.experimental.pallas import tpu as pltpu

D_MODEL = 1024
BATCH = 16
SEQ = 256
DEPTH = 2
DEC_BATCH = 2
DEC_SEQ = 1024
PAST_LEN = 512
GRID_W = 64
H_GLA = 4
DK_GLA = 64
DV_GLA = 128
GLA_RANK = 16
GATE_NORM = 16.0
H_NAT = 8
HD_NAT = 64
WIN_H = 8
WIN_W = 16
D_FF = 2816
N_MOD = 9
ROPE_BASE = 10000.0
EPS = 1e-6
NEG_INF = -1e30

F32 = jnp.float32
BF16 = jnp.bfloat16

N_CTX = BATCH * SEQ
N_LAT = DEC_BATCH * DEC_SEQ
N_ALL = N_CTX + N_LAT
N_GROUPS = 1 + DEC_BATCH
QK_W = H_GLA * DK_GLA
V_W = H_GLA * DV_GLA
NAT_W = H_NAT * HD_NAT
GRID_ROWS = DEC_SEQ // GRID_W

LANES = 128
SUBLANES = 8

TM = 512
FFN_TC = 256
FFN_NC = D_FF // FFN_TC
FFN_STAGES = 2
MOD_TN = 1152
GLA_T = 256
GLA_CB = 16
GLA_NB = GLA_T // GLA_CB
VMEM_LIMIT = 56 * 1024 * 1024

assert N_CTX % TM == 0 and DEC_SEQ % TM == 0 and D_FF % FFN_TC == 0 and FFN_TC % LANES == 0
assert SEQ == GLA_T and DEC_SEQ % GLA_T == 0


def _group_of_tile(i):
    return jnp.where(i < N_CTX // TM, 0, 1 + (i - N_CTX // TM) // (DEC_SEQ // TM))


def _dot(a, b):
    return jnp.dot(a, b, preferred_element_type=F32)


def _dot_nt(a, b):
    return lax.dot_general(a, b, (((1,), (1,)), ((), ())), preferred_element_type=F32)


def _dot_tn(a, b):
    return lax.dot_general(a, b, (((0,), (0,)), ((), ())), preferred_element_type=F32)


def _rms(x, g):
    ms = jnp.mean(x * x, axis=-1, keepdims=True)
    return x * lax.rsqrt(ms + EPS) * g


def _silu(x):
    return x * jax.nn.sigmoid(x)


def _mod_kernel(c_ref, w_ref, b_ref, o_ref):
    s = _silu(c_ref[...]).astype(BF16)
    o_ref[0] = _dot(s, w_ref[0].astype(BF16)) + b_ref[0]


def _modulation(cvecs, w_mod, b_mod):
    n_out = N_MOD * D_MODEL
    return pl.pallas_call(
        _mod_kernel,
        grid=(DEPTH, n_out // MOD_TN),
        in_specs=[
            pl.BlockSpec((SUBLANES, D_MODEL), lambda l, j: (0, 0)),
            pl.BlockSpec((1, D_MODEL, MOD_TN), lambda l, j: (l, 0, j)),
            pl.BlockSpec((1, 1, MOD_TN), lambda l, j: (l, 0, j)),
        ],
        out_specs=pl.BlockSpec((1, SUBLANES, MOD_TN), lambda l, j: (l, 0, j)),
        out_shape=jax.ShapeDtypeStruct((DEPTH, SUBLANES, n_out), F32),
        compiler_params=pltpu.CompilerParams(
            dimension_semantics=("arbitrary", "arbitrary"), vmem_limit_bytes=VMEM_LIMIT),
        name="modulation",
    )(cvecs, w_mod, b_mod.reshape(DEPTH, 1, n_out))


_CTX_TILES = N_CTX // TM
_LAT_TILES = N_LAT // TM


def _ctx_tile(i):
    return jnp.minimum(i, _CTX_TILES - 1)


def _lat_tile(i):
    return jnp.maximum(i - _CTX_TILES, 0)


def _read_split(i, ctx_ref, lat_ref):
    return jnp.where(i < _CTX_TILES, ctx_ref[...], lat_ref[...])


def _ffn_kernel(*refs, m0, n0, layer, which, split_in, split_out):
    n_x = 2 if split_in else 1
    x_refs, (mod_ref, ng_ref, win_hbm, wout_hbm) = refs[:n_x], refs[n_x:n_x + 4]
    n_o = 2 if split_out else 1
    o_refs = refs[n_x + 4:n_x + 4 + n_o]
    h_scr, acc_scr, x_scr, wg_bf, wu_bf, wo_bf, stage_g, stage_u, stage_o, sem = refs[n_x + 4 + n_o:]
    i = pl.program_id(0)

    def chunk_copies(c, slot):
        cols = pl.ds(c * FFN_TC, FFN_TC)
        up_cols = pl.ds(D_FF + c * FFN_TC, FFN_TC)
        return (
            pltpu.make_async_copy(win_hbm.at[layer, which, :, cols], stage_g.at[slot], sem.at[0, slot]),
            pltpu.make_async_copy(win_hbm.at[layer, which, :, up_cols], stage_u.at[slot], sem.at[1, slot]),
            pltpu.make_async_copy(wout_hbm.at[layer, which, cols, :], stage_o.at[slot], sem.at[2, slot]),
        )

    def accumulate(c):
        h = h_scr[...]
        gate = _dot(h, wg_bf[c])
        up = _dot(h, wu_bf[c])
        part = _dot((_silu(gate) * up).astype(BF16), wo_bf[c])
        if c == 0:
            acc_scr[...] = part
        else:
            acc_scr[...] += part

    x = _read_split(i, *x_refs) if split_in else x_refs[0][...]
    x_scr[...] = x
    gain = ng_ref[n0:n0 + 1, :] * (1.0 + mod_ref[0, m0 + 1:m0 + 2, :])
    h_scr[...] = (_rms(x, gain) + mod_ref[0, m0:m0 + 1, :]).astype(BF16)

    @pl.when(i == 0)
    def _():
        for c in range(min(FFN_STAGES, FFN_NC)):
            for cp in chunk_copies(c, c):
                cp.start()
        for c in range(FFN_NC):
            slot = c % FFN_STAGES
            for cp in chunk_copies(c, slot):
                cp.wait()
            wg_bf[c] = stage_g[slot].astype(BF16)
            wu_bf[c] = stage_u[slot].astype(BF16)
            wo_bf[c] = stage_o[slot].astype(BF16)
            if c + FFN_STAGES < FFN_NC:
                for cp in chunk_copies(c + FFN_STAGES, slot):
                    cp.start()
            accumulate(c)

    @pl.when(i > 0)
    def _():
        for c in range(FFN_NC):
            accumulate(c)

    gain = 0.5 * mod_ref[0, m0 + 2:m0 + 3, :] * ng_ref[n0 + 1:n0 + 2, :]
    out = x_scr[...] + _rms(acc_scr[...], gain)
    if split_out:
        @pl.when(i < _CTX_TILES)
        def _():
            o_refs[0][...] = out

        @pl.when(i >= _CTX_TILES)
        def _():
            o_refs[1][...] = out
    else:
        o_refs[0][...] = out


def _ffn(x, mod_l, ng_l, w_in, w_out, *, layer, which, split_out=False):
    m0, n0 = (0, 0) if which == 0 else (6, 4)
    split_in = isinstance(x, tuple)
    tile = pl.BlockSpec((TM, D_MODEL), lambda i: (i, 0))
    ctx_tile = pl.BlockSpec((TM, D_MODEL), lambda i: (_ctx_tile(i), 0))
    lat_tile = pl.BlockSpec((TM, D_MODEL), lambda i: (_lat_tile(i), 0))
    if split_out:
        out_specs = [ctx_tile, lat_tile]
        out_shape = [jax.ShapeDtypeStruct((N_CTX, D_MODEL), F32), jax.ShapeDtypeStruct((N_LAT, D_MODEL), F32)]
    else:
        out_specs = tile
        out_shape = jax.ShapeDtypeStruct((N_ALL, D_MODEL), F32)
    return pl.pallas_call(
        functools.partial(_ffn_kernel, m0=m0, n0=n0, layer=layer, which=which, split_in=split_in,
                          split_out=split_out),
        grid=(N_ALL // TM,),
        in_specs=([ctx_tile, lat_tile] if split_in else [tile]) + [
            pl.BlockSpec((1, N_MOD, D_MODEL), lambda i: (_group_of_tile(i), 0, 0)),
            pl.BlockSpec((6, D_MODEL), lambda i: (0, 0)),
            pl.BlockSpec(memory_space=pl.ANY),
            pl.BlockSpec(memory_space=pl.ANY),
        ],
        out_specs=out_specs,
        out_shape=out_shape,
        scratch_shapes=[
            pltpu.VMEM((TM, D_MODEL), BF16),
            pltpu.VMEM((TM, D_MODEL), F32),
            pltpu.VMEM((TM, D_MODEL), F32),
            pltpu.VMEM((FFN_NC, D_MODEL, FFN_TC), BF16),
            pltpu.VMEM((FFN_NC, D_MODEL, FFN_TC), BF16),
            pltpu.VMEM((FFN_NC, FFN_TC, D_MODEL), BF16),
            pltpu.VMEM((FFN_STAGES, D_MODEL, FFN_TC), F32),
            pltpu.VMEM((FFN_STAGES, D_MODEL, FFN_TC), F32),
            pltpu.VMEM((FFN_STAGES, FFN_TC, D_MODEL), F32),
            pltpu.SemaphoreType.DMA((3, FFN_STAGES)),
        ],
        compiler_params=pltpu.CompilerParams(
            dimension_semantics=("arbitrary",), vmem_limit_bytes=VMEM_LIMIT),
        name="ffn",
    )(*(x if split_in else (x,)), mod_l, ng_l, w_in, w_out)


def _block_transpose_perm():
    r = lax.broadcasted_iota(jnp.int32, (GLA_T, GLA_T), 0)
    c = lax.broadcasted_iota(jnp.int32, (GLA_T, GLA_T), 1)
    return jnp.where(c == (r & (GLA_CB - 1)) * GLA_NB + (r >> 4), 1.0, 0.0).astype(BF16)


def _permute_chunks(perm, a):
    return jnp.concatenate([_dot(perm, a[c * GLA_T:(c + 1) * GLA_T, :]).astype(BF16)
                            for c in range(a.shape[0] // GLA_T)], axis=0)


_P_QK = 0
_P_VR = 4 * QK_W
_P_NAT = _P_VR + 2 * V_W
_P_LR = _P_NAT + 3 * NAT_W
_P_END = _P_LR + LANES


_NAT_SCALE = HD_NAT ** -0.5
assert _NAT_SCALE == 2.0 ** round(np.log2(_NAT_SCALE))
_GLA_COLS = 2 * QK_W + 2 * V_W
_REST_COLS = 2 * GLA_RANK + 3 * NAT_W
_REST_TILES = -(-_REST_COLS // LANES)
_PREP_ROWS = 128
assert _GLA_COLS % LANES == 0 and _GLA_COLS + _REST_COLS == 3104 and D_MODEL % _PREP_ROWS == 0


def _proj_prepare_weights(win_hbm, tail_ref, layer, w_bf, stage, sem):
    lane = lax.broadcasted_iota(jnp.int32, (_PREP_ROWS, LANES), 1)
    first_of_pair = ((lane >> 4) & 1) == 0
    shift = 2 * GLA_RANK

    head = pltpu.make_async_copy(win_hbm.at[layer, :, pl.ds(0, _GLA_COLS)], stage.at[:, pl.ds(0, _GLA_COLS)],
                                 sem.at[0])
    head.start()
    head.wait()
    for r0 in range(0, D_MODEL, _PREP_ROWS):
        rows = slice(r0, r0 + _PREP_ROWS)
        w_bf[rows, _P_QK:_P_QK + 2 * QK_W] = stage[rows, 0:2 * QK_W].astype(BF16)
        for t in range(2 * QK_W // LANES):
            tile = stage[rows, t * LANES:(t + 1) * LANES]
            partner = jnp.where(first_of_pair, pltpu.roll(tile, LANES - 16, axis=1), pltpu.roll(tile, 16, axis=1))
            w_bf[rows, _P_QK + 2 * QK_W + t * LANES:_P_QK + 2 * QK_W + (t + 1) * LANES] = partner.astype(BF16)
        w_bf[rows, _P_VR:_P_NAT] = stage[rows, 2 * QK_W:_GLA_COLS].astype(BF16)

    whole = (_REST_TILES - 1) * LANES
    stage[:, whole:] = tail_ref[0]
    rest = pltpu.make_async_copy(win_hbm.at[layer, :, pl.ds(_GLA_COLS, whole)], stage.at[:, pl.ds(0, whole)],
                                 sem.at[1])
    rest.start()
    rest.wait()
    for r0 in range(0, D_MODEL, _PREP_ROWS):
        rows = slice(r0, r0 + _PREP_ROWS)
        w_bf[rows, _P_LR:_P_END] = jnp.where(lane < shift, stage[rows, 0:LANES], 0.0).astype(BF16)
        for t in range(3 * NAT_W // LANES):
            lo = pltpu.roll(stage[rows, t * LANES:(t + 1) * LANES], LANES - shift, axis=1)
            hi = pltpu.roll(stage[rows, (t + 1) * LANES:(t + 2) * LANES], LANES - shift, axis=1)
            w_bf[rows, _P_NAT + t * LANES:_P_NAT + (t + 1) * LANES] = jnp.where(lane < LANES - shift, lo,
                                                                                  hi).astype(BF16)


def _proj_kernel(x_ref, mod_ref, ng_ref, win_hbm, tail_ref, wa_ref, ba_ref, cos_ref, sin_ref, *rest, layer):
    q_ref, k_ref, v_ref, r_ref, g_ref, qn_ref, kn_ctx_ref, vn_ctx_ref, kn_ref, vn_ref = rest[-13:-3]
    w_ref, stage, sem = rest[-3:]
    i = pl.program_id(0)

    @pl.when(i == 0)
    def _():
        _proj_prepare_weights(win_hbm, tail_ref, layer, w_ref, stage, sem)

    h = _rms(x_ref[...], ng_ref[2:3, :])
    h = (h * (1.0 + mod_ref[0, 4:5, :]) + mod_ref[0, 3:4, :]).astype(BF16)
    hp = _permute_chunks(_block_transpose_perm(), h)

    lr = _dot(hp, w_ref[:, _P_LR:_P_END]).astype(BF16)
    z = _dot(lr, wa_ref[...]) + ba_ref[...]
    g = (jnp.minimum(z, 0.0) - jnp.log1p(jnp.exp(-jnp.abs(z)))) * (1.0 / GATE_NORM)
    g_ref[0] = g[:, 0:QK_W]
    g_ref[1] = g[:, QK_W:2 * QK_W]

    qk = _dot(hp, w_ref[:, _P_QK:_P_VR])
    cos = cos_ref[...]
    sin = sin_ref[...]
    q_ref[...] = (qk[:, 0:QK_W] * cos + qk[:, 2 * QK_W:3 * QK_W] * sin) * (DK_GLA ** -0.5)
    k_ref[...] = qk[:, QK_W:2 * QK_W] * cos + qk[:, 3 * QK_W:4 * QK_W] * sin

    vr = _dot(hp, w_ref[:, _P_VR:_P_NAT])
    v_ref[...] = vr[:, 0:V_W]
    r_ref[...] = vr[:, V_W:2 * V_W]

    nat = _dot(h, w_ref[:, _P_NAT:_P_LR])
    qn_ref[...] = (nat[:, 0:NAT_W] * _NAT_SCALE).astype(BF16)
    kn_ref[...] = nat[:, NAT_W:2 * NAT_W].astype(BF16)
    vn_ref[...] = nat[:, 2 * NAT_W:3 * NAT_W].astype(BF16)

    @pl.when(i < _CTX_TILES)
    def _():
        heads = (TM // SEQ, SEQ, H_NAT, HD_NAT)
        kn_ctx_ref[:, 0] = nat[:, NAT_W:2 * NAT_W].reshape(heads)
        vn_ctx_ref[:, 0] = nat[:, 2 * NAT_W:3 * NAT_W].reshape(heads)
        for later in range(1, kn_ctx_ref.shape[1]):
            kn_ctx_ref[:, later] = jnp.zeros(heads, F32)
            vn_ctx_ref[:, later] = jnp.zeros(heads, F32)


def _rope_table_block(i):
    lat_tiles = DEC_SEQ // TM
    return jnp.where(i < N_CTX // TM, lat_tiles, (i - N_CTX // TM) % lat_tiles)


def _project(x, mod_l, ng_l, w_in, w_a, b_a, cos_t, sin_t, new_cache, *, layer):
    tile = lambda w: pl.BlockSpec((TM, w), lambda i: (i, 0))
    if new_cache:
        ctx_heads = pl.BlockSpec((TM // SEQ, 1, SEQ, H_NAT, HD_NAT), lambda i: (_ctx_tile(i), layer, 0, 0, 0))
    else:
        ctx_heads = pl.BlockSpec((TM // SEQ, DEPTH, SEQ, H_NAT, HD_NAT), lambda i: (_ctx_tile(i), 0, 0, 0, 0))
    full = lambda a: pl.BlockSpec(a.shape, lambda i: (0,) * a.ndim)
    n_in = 9
    tail0 = _GLA_COLS + (_REST_TILES - 1) * LANES
    w_tail = jnp.pad(w_in[:, :, tail0:], ((0, 0), (0, 0), (0, LANES - (w_in.shape[2] - tail0))))
    return pl.pallas_call(
        functools.partial(_proj_kernel, layer=layer),
        grid=(N_ALL // TM,),
        in_specs=[
            tile(D_MODEL),
            pl.BlockSpec((1, N_MOD, D_MODEL), lambda i: (_group_of_tile(i), 0, 0)),
            full(ng_l),
            pl.BlockSpec(memory_space=pl.ANY),
            pl.BlockSpec((1, D_MODEL, LANES), lambda i: (layer, 0, 0)),
            full(w_a), full(b_a),
            pl.BlockSpec((TM, QK_W), lambda i: (_rope_table_block(i), 0)),
            pl.BlockSpec((TM, QK_W), lambda i: (_rope_table_block(i), 0)),
        ] + [pl.BlockSpec(memory_space=pl.ANY)] * len(new_cache),
        input_output_aliases={n_in + n: 6 + n for n in range(len(new_cache))},
        out_specs=[
            tile(QK_W), tile(QK_W), tile(V_W), tile(V_W),
            pl.BlockSpec((2, TM, QK_W), lambda i: (0, i, 0)),
            tile(NAT_W), ctx_heads, ctx_heads, tile(NAT_W), tile(NAT_W),
        ],
        out_shape=[
            jax.ShapeDtypeStruct((N_ALL, QK_W), F32), jax.ShapeDtypeStruct((N_ALL, QK_W), F32),
            jax.ShapeDtypeStruct((N_ALL, V_W), F32), jax.ShapeDtypeStruct((N_ALL, V_W), F32),
            jax.ShapeDtypeStruct((2, N_ALL, QK_W), F32),
            jax.ShapeDtypeStruct((N_ALL, NAT_W), BF16),
            jax.ShapeDtypeStruct((BATCH, DEPTH, SEQ, H_NAT, HD_NAT), F32),
            jax.ShapeDtypeStruct((BATCH, DEPTH, SEQ, H_NAT, HD_NAT), F32),
            jax.ShapeDtypeStruct((N_ALL, NAT_W), BF16), jax.ShapeDtypeStruct((N_ALL, NAT_W), BF16),
        ],
        scratch_shapes=[
            pltpu.VMEM((D_MODEL, _P_END), BF16),
            pltpu.VMEM((D_MODEL, _REST_TILES * LANES), F32),
            pltpu.SemaphoreType.DMA((2,)),
        ],
        compiler_params=pltpu.CompilerParams(
            dimension_semantics=("arbitrary",), vmem_limit_bytes=VMEM_LIMIT),
        name="mixer_proj",
    )(x, mod_l, ng_l, w_in, w_tail, w_a, b_a, cos_t, sin_t, *new_cache)


def _gla_tables():
    rows = []
    seq_specs = [(b * (SEQ // GLA_T), SEQ // GLA_T) for b in range(BATCH)]
    seq_specs += [(N_CTX // GLA_T + b * (DEC_SEQ // GLA_T), DEC_SEQ // GLA_T) for b in range(DEC_BATCH)]
    for sid, (blk0, nchunk) in enumerate(seq_specs):
        for direction in (0, 1):
            order = range(nchunk) if direction == 0 else range(nchunk - 1, -1, -1)
            for n, c in enumerate(order):
                out_blk = blk0 + (c if direction == 1 or nchunk == 1 else nchunk - 1)
                rows.append((blk0 + c, direction, int(n == 0), sid, out_blk, c if nchunk > 1 else -1))
    return np.asarray(rows, dtype=np.int32).T.copy()


_GLA_TAB = _gla_tables()
_GLA_ITEMS = _GLA_TAB.shape[1]
_GLA_NSEQ = BATCH + DEC_BATCH
_GLA_PAIR_ROWS = GLA_NB * GLA_CB * (GLA_CB + 1) // 2


def _gla_item(direction, q_ref, k_ref, v_ref, g_ref, res, s_scr, cp, sstk, p_scr, w_scr):
    T, CB, NB = GLA_T, GLA_CB, GLA_NB
    fwd = direction == 0
    slab = lambda i: slice(i * NB, (i + 1) * NB)

    cum = None
    for i in (range(CB) if fwd else range(CB - 1, -1, -1)):
        gi = g_ref[0, slab(i), :]
        cum = gi if cum is None else cum + gi
        cp[slab(i), :] = cum
    total = cum
    cum_all = cp[...]
    qt = q_ref[...] * jnp.exp(cum_all)
    kh = k_ref[...] * jnp.exp(jnp.concatenate([total] * CB, axis=0) - cum_all)
    dec_t = jnp.concatenate([jnp.exp(total), jnp.zeros((LANES - NB, QK_W), F32)], axis=0).T

    erow = lax.broadcasted_iota(jnp.int32, (QK_W, QK_W), 0)
    ecol = lax.broadcasted_iota(jnp.int32, (QK_W, QK_W), 1)
    head_sum = jnp.where((erow >> 6) == (ecol >> 6), 1.0, 0.0).astype(BF16)

    key_positions = lambda i: range(i + 1) if fwd else range(i, CB)
    r0 = 0
    for i in range(CB):
        qi = q_ref[slab(i), :]
        ci = cp[slab(i), :]
        for j in key_positions(i):
            e = jnp.exp(ci - cp[slab(j), :])
            p_scr[r0:r0 + NB, :] = (qi * k_ref[slab(j), :] * e).astype(BF16)
            r0 += NB
    w_scr[...] = _dot(p_scr[...], head_sum)
    first_half = lax.broadcasted_iota(jnp.int32, (NB, LANES), 1) < DK_GLA
    r0 = 0
    for i in range(CB):
        acc = None
        for j in key_positions(i):
            spread = []
            for pair in range(H_GLA // 2):
                tile = w_scr[r0:r0 + NB, pair * LANES:(pair + 1) * LANES]
                other = pltpu.roll(tile, DK_GLA, axis=1)
                spread += [jnp.where(first_half, tile, other), jnp.where(first_half, other, tile)]
            term = jnp.concatenate(spread, axis=1) * v_ref[slab(j), :]
            acc = term if acc is None else acc + term
            r0 += NB
        res[slab(i), :] = acc

    kht = kh.T.astype(BF16)
    v_bf = v_ref[...].astype(BF16)
    key_blk = lax.broadcasted_iota(jnp.int32, (DK_GLA, T), 1) & (NB - 1)
    row_blk = lax.broadcasted_iota(jnp.int32, (T, LANES), 0) & (NB - 1)
    lane_half = lax.broadcasted_iota(jnp.int32, (T, LANES), 1) >> 6
    order = range(NB) if fwd else range(NB - 1, -1, -1)
    for h in range(H_GLA):
        kh_h = kht[h * DK_GLA:(h + 1) * DK_GLA, :]
        kv = _dot(jnp.concatenate([jnp.where(key_blk == b, kh_h, 0) for b in range(NB)], axis=0),
                  v_bf[:, h * DV_GLA:(h + 1) * DV_GLA])
        s = s_scr[h]
        for b in order:
            sstk[h, b * DK_GLA:(b + 1) * DK_GLA, :] = s.astype(BF16)
            s = dec_t[h * DK_GLA:(h + 1) * DK_GLA, b:b + 1] * s + kv[b * DK_GLA:(b + 1) * DK_GLA, :]
        s_scr[h] = s

        pair_tile = qt[:, (h // 2) * LANES:(h // 2 + 1) * LANES]
        both = jnp.where(lane_half == h % 2, pair_tile, pltpu.roll(pair_tile, DK_GLA, axis=1))
        lhs = jnp.concatenate([jnp.where(row_blk == 2 * j + lane_half, both, 0.0).astype(BF16)
                               for j in range(NB // 2)], axis=1)
        res[:, h * DV_GLA:(h + 1) * DV_GLA] += _dot(lhs, sstk[h])


def _gla_kernel(tab_ref, q_ref, k_ref, v_ref, g_ref, s0_ref, o_ref, so_ref,
                s_scr, cp, sstk, p_scr, w_scr, res, held):
    it = pl.program_id(0)
    direction = tab_ref[1, it]
    slot = tab_ref[5, it]

    @pl.when(tab_ref[2, it] == 1)
    def _():
        s_scr[...] = s0_ref[0, 0]

    for d in (0, 1):
        @pl.when(direction == d)
        def _(d=d):
            _gla_item(d, q_ref, k_ref, v_ref, g_ref, res, s_scr, cp, sstk, p_scr, w_scr)

    @pl.when((direction == 0) & (slot < 0))
    def _():
        o_ref[...] = res[...]

    @pl.when((direction == 1) & (slot < 0))
    def _():
        o_ref[...] += res[...]

    @pl.when((direction == 0) & (slot >= 0))
    def _():
        held[jnp.maximum(slot, 0)] = res[...]

    @pl.when((direction == 1) & (slot >= 0))
    def _():
        o_ref[...] = held[jnp.maximum(slot, 0)] + res[...]

    so_ref[0, 0] = s_scr[...]


def _gla(q, k, v, g, s0):
    tok = lambda w: pl.BlockSpec((GLA_T, w), lambda it, tab: (tab[0, it], 0))
    state = pl.BlockSpec((1, 1, H_GLA, DK_GLA, DV_GLA), lambda it, tab: (tab[3, it], tab[1, it], 0, 0, 0))
    grid_spec = pltpu.PrefetchScalarGridSpec(
        num_scalar_prefetch=1,
        grid=(_GLA_ITEMS,),
        in_specs=[
            tok(QK_W), tok(QK_W), tok(V_W),
            pl.BlockSpec((1, GLA_T, QK_W), lambda it, tab: (tab[1, it], tab[0, it], 0)),
            state,
        ],
        out_specs=[
            pl.BlockSpec((GLA_T, V_W), lambda it, tab: (tab[4, it], 0)),
            state,
        ],
        scratch_shapes=[
            pltpu.VMEM((H_GLA, DK_GLA, DV_GLA), F32),
            pltpu.VMEM((GLA_T, QK_W), F32),
            pltpu.VMEM((H_GLA, GLA_NB * DK_GLA, DV_GLA), BF16),
            pltpu.VMEM((_GLA_PAIR_ROWS, QK_W), BF16),
            pltpu.VMEM((_GLA_PAIR_ROWS, QK_W), F32),
            pltpu.VMEM((GLA_T, V_W), F32),
            pltpu.VMEM((DEC_SEQ // GLA_T, GLA_T, V_W), F32),
        ],
    )
    return pl.pallas_call(
        _gla_kernel,
        grid_spec=grid_spec,
        out_shape=[
            jax.ShapeDtypeStruct((N_ALL, V_W), F32),
            jax.ShapeDtypeStruct((_GLA_NSEQ, 2, H_GLA, DK_GLA, DV_GLA), F32),
        ],
        compiler_params=pltpu.CompilerParams(
            dimension_semantics=("arbitrary",), vmem_limit_bytes=VMEM_LIMIT),
        name="gla",
    )(jnp.asarray(_GLA_TAB), q, k, v, g, s0)


def _softmax_pv(s_list, v_list):
    m = s_list[0].max(axis=-1, keepdims=True)
    for s in s_list[1:]:
        m = jnp.maximum(m, s.max(axis=-1, keepdims=True))
    num = None
    den = None
    for s, vv in zip(s_list, v_list):
        e = jnp.exp(s - m)
        den = e.sum(axis=-1, keepdims=True) if den is None else den + e.sum(axis=-1, keepdims=True)
        pv = _dot(e.astype(BF16), vv)
        num = pv if num is None else num + pv
    return num / den


def _ctx_attn_kernel(q_ref, k_ref, v_ref, o_ref):
    lane =lax.broadcasted_iota(jnp.int32, (SEQ, LANES), 1)
    for t in range(NAT_W // LANES):
        sl = slice(t * LANES, (t + 1) * LANES)
        qt = q_ref[:, sl]
        kt = k_ref[:, sl].astype(BF16)
        vt = v_ref[:, sl].astype(BF16)
        out = jnp.zeros((SEQ, LANES), F32)
        for half in range(LANES // HD_NAT):
            mine = (lane >> 6) == half
            s = _dot_nt(jnp.where(mine, qt, 0).astype(BF16), kt)
            out = jnp.where(mine, _softmax_pv([s], [vt]), out)
        o_ref[:, sl] = out.astype(o_ref.dtype)


def _ctx_attention(qn, kn, vn):
    spec = pl.BlockSpec((SEQ, NAT_W), lambda b: (b, 0))
    return pl.pallas_call(
        _ctx_attn_kernel,
        grid=(BATCH,),
        in_specs=[spec, spec, spec],
        out_specs=spec,
        out_shape=jax.ShapeDtypeStruct((N_CTX, NAT_W), BF16),
        compiler_params=pltpu.CompilerParams(
            dimension_semantics=("arbitrary",), vmem_limit_bytes=VMEM_LIMIT),
        name="ctx_attention",
    )(qn, kn, vn)


_NAT_QROWS = 4
_NAT_GROUPS = GRID_ROWS // _NAT_QROWS
_NAT_KROWS = WIN_H + _NAT_QROWS
_NAT_Q = _NAT_QROWS * GRID_W
_NAT_KEYS = _NAT_KROWS * GRID_W
_NAT_DR = 2 * WIN_H - 1
_NAT_DC = 2 * WIN_W - 1


def _nat_key_row0(r):
    return jnp.clip(r - WIN_H // 2, 0, GRID_ROWS - WIN_H)


def _nat_build_bias(rpb_ref, tz_scr):
    c = lax.broadcasted_iota(jnp.int32, (GRID_W, LANES), 0)
    lane = lax.broadcasted_iota(jnp.int32, (GRID_W, LANES), 1)
    kc = lane & (GRID_W - 1)
    second = (lane >> 6) == 1
    win_start = jnp.clip(c - WIN_W // 2, 0, GRID_W - WIN_W)
    valid = (kc >= win_start) & (kc < win_start + WIN_W)

    def one_row(n, carry):
        dr = n >> 3
        h = n & (H_NAT - 1)
        src = jnp.broadcast_to(rpb_ref[pl.ds(h * _NAT_DR + dr, 1), :], (GRID_W, LANES))
        rolled = pltpu.roll(src, LANES - (WIN_W - 1), axis=1, stride=1, stride_axis=0)
        tz_scr[dr, h] = jnp.where(valid, rolled, NEG_INF)
        return carry

    lax.fori_loop(0, _NAT_DR * H_NAT, one_row, 0)

    def pair_rows(n, carry):
        dr = n >> 3
        h = n & (H_NAT - 1)
        tz_scr[dr, h] = jnp.where(second, tz_scr[dr + 1, h], tz_scr[dr, h])
        return carry

    lax.fori_loop(0, (_NAT_DR - 1) * H_NAT, pair_rows, 0)


def _nat_kernel(rpb_ref, q_ref, k_ref, v_ref, ck_ref, cv_ref, o_ref, tz_scr, bias_scr):
    grp = pl.program_id(1)

    @pl.when((pl.program_id(0) == 0) & (grp == 0))
    def _():
        _nat_build_bias(rpb_ref, tz_scr)

    krow0 = jnp.clip(_NAT_QROWS * grp - WIN_H // 2, 0, GRID_ROWS - _NAT_KROWS)
    k0 = pl.multiple_of(krow0 * GRID_W, GRID_W)
    lane_q = lax.broadcasted_iota(jnp.int32, (_NAT_Q, LANES), 1)
    lane_b = lax.broadcasted_iota(jnp.int32, (GRID_W, LANES), 1)
    neg = jnp.full((GRID_W, LANES), NEG_INF, F32)
    for t in range(NAT_W // LANES):
        sl = slice(t * LANES, (t + 1) * LANES)
        qt = q_ref[:, sl]
        kw = k_ref[pl.ds(k0, _NAT_KEYS), sl].astype(BF16)
        vw = v_ref[pl.ds(k0, _NAT_KEYS), sl].astype(BF16)
        ck = ck_ref[0, 0, :, sl].astype(BF16)
        cv = cv_ref[0, 0, :, sl].astype(BF16)
        out = jnp.zeros((_NAT_Q, LANES), F32)
        for half in range(LANES // HD_NAT):
            h = 2 * t + half
            for qr in range(_NAT_QROWS):
                r = _NAT_QROWS * grp + qr
                lo = _nat_key_row0(r)
                for kp in range(_NAT_KROWS // 2):
                    kr = krow0 + 2 * kp
                    tile = tz_scr[jnp.clip(kr - r + (WIN_H - 1), 0, _NAT_DR - 1), h]
                    ok_a = ((kr >= lo) & (kr < lo + WIN_H)).astype(jnp.int32)
                    ok_b = ((kr + 1 >= lo) & (kr + 1 < lo + WIN_H)).astype(jnp.int32)
                    ok = jnp.where(lane_b < GRID_W, ok_a, ok_b) == 1
                    bias_scr[qr * GRID_W:(qr + 1) * GRID_W, kp * LANES:(kp + 1) * LANES] = jnp.where(ok, tile, neg)
            mine = (lane_q >> 6) == half
            qm = jnp.where(mine, qt, 0).astype(BF16)
            s_win = _dot_nt(qm, kw) + bias_scr[...]
            s_ctx = _dot_nt(qm, ck)
            out = jnp.where(mine, _softmax_pv([s_win, s_ctx], [vw, cv]), out)
        o_ref[:, sl] = out.astype(o_ref.dtype)


def _nat_attention(qn, kn, vn, cache_k, cache_v, rpb_l, layer):
    lat0 = N_CTX // _NAT_Q
    half = jnp.pad(rpb_l.reshape(H_NAT * _NAT_DR, _NAT_DC), ((0, 0), (0, GRID_W - _NAT_DC)))
    rpb_rows = jnp.concatenate([half, half], axis=1)
    return pl.pallas_call(
        _nat_kernel,
        grid=(DEC_BATCH, _NAT_GROUPS),
        in_specs=[
            pl.BlockSpec((H_NAT * _NAT_DR, LANES), lambda b, g: (0, 0)),
            pl.BlockSpec((_NAT_Q, NAT_W), lambda b, g: (lat0 + b * _NAT_GROUPS + g, 0)),
            pl.BlockSpec((DEC_SEQ, NAT_W), lambda b, g: (N_CTX // DEC_SEQ + b, 0)),
            pl.BlockSpec((DEC_SEQ, NAT_W), lambda b, g: (N_CTX // DEC_SEQ + b, 0)),
            pl.BlockSpec((1, 1, PAST_LEN, NAT_W), lambda b, g: (b, layer, 0, 0)),
            pl.BlockSpec((1, 1, PAST_LEN, NAT_W), lambda b, g: (b, layer, 0, 0)),
        ],
        out_specs=pl.BlockSpec((_NAT_Q, NAT_W), lambda b, g: (b * _NAT_GROUPS + g, 0)),
        out_shape=jax.ShapeDtypeStruct((N_LAT, NAT_W), BF16),
        scratch_shapes=[
            pltpu.VMEM((_NAT_DR, H_NAT, GRID_W, LANES), F32),
            pltpu.VMEM((_NAT_Q, _NAT_KEYS), F32),
        ],
        compiler_params=pltpu.CompilerParams(
            dimension_semantics=("arbitrary", "arbitrary"), vmem_limit_bytes=VMEM_LIMIT),
        name="nat_attention",
    )(rpb_rows, qn, kn, vn, cache_k, cache_v)


def _out_kernel(x_ref, o_ref, r_ref, on_ctx_ref, on_lat_ref, mod_ref, ng_ref, gng_ref, w_ref, y_ref):
    o_nat = _read_split(pl.program_id(0), on_ctx_ref, on_lat_ref)
    og = o_ref[...]
    parts = []
    for h in range(H_GLA):
        parts.append(_rms(og[:, h * DV_GLA:(h + 1) * DV_GLA], gng_ref[...]))
    merged = (jnp.concatenate(parts, axis=1) * _silu(r_ref[...])).astype(BF16)
    merged = _permute_chunks(_block_transpose_perm(), merged)
    y = _dot(merged, w_ref[0:V_W, :]) + _dot(o_nat.astype(BF16), w_ref[V_W:, :])
    y_ref[...] = x_ref[...] + mod_ref[0, 5:6, :] * _rms(y, ng_ref[3:4, :])


def _mixer_out(x, o_gla, r, o_ctx, o_lat, mod_l, ng_l, gng_l, w_out_bf):
    tile = lambda w: pl.BlockSpec((TM, w), lambda i: (i, 0))
    full = lambda a: pl.BlockSpec(a.shape, lambda i: (0,) * a.ndim)
    return pl.pallas_call(
        _out_kernel,
        grid=(N_ALL // TM,),
        in_specs=[
            tile(D_MODEL),
            tile(V_W),
            tile(V_W),
            pl.BlockSpec((TM, NAT_W), lambda i: (_ctx_tile(i), 0)),
            pl.BlockSpec((TM, NAT_W), lambda i: (_lat_tile(i), 0)),
            pl.BlockSpec((1, N_MOD, D_MODEL), lambda i: (_group_of_tile(i), 0, 0)),
            full(ng_l), full(gng_l), full(w_out_bf),
        ],
        out_specs=tile(D_MODEL),
        out_shape=jax.ShapeDtypeStruct((N_ALL, D_MODEL), F32),
        compiler_params=pltpu.CompilerParams(
            dimension_semantics=("arbitrary",), vmem_limit_bytes=VMEM_LIMIT),
        name="mixer_out",
    )(x, o_gla, r, o_ctx, o_lat, mod_l, ng_l, gng_l, w_out_bf)


def _rope_tables():
    quarter = DK_GLA // 4
    freqs = ROPE_BASE ** (-jnp.arange(quarter, dtype=F32) / quarter)
    t = jnp.arange(DEC_SEQ)
    ang_r = (t // GRID_W).astype(F32)[:, None] * freqs[None, :]
    ang_c = (t % GRID_W).astype(F32)[:, None] * freqs[None, :]
    cos_h = jnp.concatenate([jnp.cos(ang_r), jnp.cos(ang_r), jnp.cos(ang_c), jnp.cos(ang_c)], axis=1)
    sin_h = jnp.concatenate([-jnp.sin(ang_r), jnp.sin(ang_r), -jnp.sin(ang_c), jnp.sin(ang_c)], axis=1)
    pos_major = lambda a: a.reshape(-1, GLA_NB, GLA_CB, QK_W).transpose(0, 2, 1, 3).reshape(a.shape)
    cos_t = jnp.concatenate([pos_major(jnp.tile(cos_h, (1, H_GLA))), jnp.ones((TM, QK_W), F32)], axis=0)
    sin_t = jnp.concatenate([pos_major(jnp.tile(sin_h, (1, H_GLA))), jnp.zeros((TM, QK_W), F32)], axis=0)
    return cos_t, sin_t


def _gate_up_weights(gla_wa2_l, gla_ba_l):
    w_a = jnp.zeros((LANES, 2 * QK_W), F32)
    w_a = w_a.at[0:GLA_RANK, 0:QK_W].set(gla_wa2_l[0]).at[GLA_RANK:2 * GLA_RANK, QK_W:].set(gla_wa2_l[1])
    b_a = jnp.concatenate([gla_ba_l[0], gla_ba_l[1]])[None, :]
    return w_a.astype(BF16), b_a


def kernel(x_prompt, x_sample, cache_k, cache_v, state_gla, c, c_ctx, w_mod, b_mod, norm_g, ffn_w_in, ffn_w_out,
           w_in, gla_wa2, gla_ba, gla_norm_g, nat_rpb, w_out):
    cvecs = jnp.zeros((SUBLANES, D_MODEL), F32).at[0].set(c_ctx).at[1:1 + DEC_BATCH].set(c)
    mod = _modulation(cvecs, w_mod, b_mod)[:, :N_GROUPS].reshape(DEPTH, N_GROUPS, N_MOD, D_MODEL)

    w_out_bf = w_out.astype(BF16)
    cos_t, sin_t = _rope_tables()
    ck = cache_k.reshape(DEC_BATCH, DEPTH, PAST_LEN, NAT_W)
    cv = cache_v.reshape(DEC_BATCH, DEPTH, PAST_LEN, NAT_W)

    x = (x_prompt.reshape(N_CTX, D_MODEL), x_sample.reshape(N_LAT, D_MODEL))
    new_cache, s_list = (), []
    for l in range(DEPTH):
        mod_l, ng_l = mod[l], norm_g[l]
        x = _ffn(x, mod_l, ng_l, ffn_w_in, ffn_w_out, layer=l, which=0)

        w_a, b_a = _gate_up_weights(gla_wa2[l], gla_ba[l])
        q, k, v, r, g, qn, new_k, new_v, kn, vn = _project(
            x, mod_l, ng_l, w_in, w_a, b_a, cos_t, sin_t, new_cache, layer=l)
        new_cache = (new_k, new_v)

        s0 = jnp.concatenate([jnp.zeros((BATCH, 2, H_GLA, DK_GLA, DV_GLA), F32), state_gla[:, l]], axis=0)
        o_gla, s_fin = _gla(q, k, v, g, s0)
        o_ctx = _ctx_attention(qn, kn, vn)
        o_lat = _nat_attention(qn, kn, vn, ck, cv, nat_rpb[l], l)

        x = _mixer_out(x, o_gla, r, o_ctx, o_lat, mod_l, ng_l, gla_norm_g[l][None, :], w_out_bf[l])
        x = _ffn(x, mod_l, ng_l, ffn_w_in, ffn_w_out, layer=l, which=1, split_out=(l == DEPTH - 1))

        s_list.append(s_fin[:BATCH])

    y_prompt = x[0].reshape(BATCH, SEQ, D_MODEL)
    y_sample = x[1].reshape(DEC_BATCH, DEC_SEQ, D_MODEL)
    return (y_prompt, y_sample, new_cache[0], new_cache[1], jnp.stack(s_list, axis=1))
```

```python
import functools

import numpy as np
import jax
import jax.numpy as jnp
from jax import lax
from jax.experimental import pallas as pl
from jax.experimental.pallas import tpu as pltpu

D_MODEL = 1024
BATCH = 16
SEQ = 256
DEPTH = 2
DEC_BATCH = 2
DEC_SEQ = 1024
PAST_LEN = 512
GRID_W = 64
H_GLA = 4
DK_GLA = 64
DV_GLA = 128
GLA_RANK = 16
GATE_NORM = 16.0
H_NAT = 8
HD_NAT = 64
WIN_H = 8
WIN_W = 16
D_FF = 2816
N_MOD = 9
ROPE_BASE = 10000.0
EPS = 1e-6
NEG_INF = -1e30

F32 = jnp.float32
BF16 = jnp.bfloat16

N_CTX = BATCH * SEQ
N_LAT = DEC_BATCH * DEC_SEQ
N_ALL = N_CTX + N_LAT
N_GROUPS = 1 + DEC_BATCH
QK_W = H_GLA * DK_GLA
V_W = H_GLA * DV_GLA
NAT_W = H_NAT * HD_NAT
GRID_ROWS = DEC_SEQ // GRID_W

LANES = 128
SUBLANES = 8

TM = 512
FFN_TC = 256
FFN_NC = D_FF // FFN_TC
FFN_STAGES = 2
MOD_TN = 1152
GLA_T = 256
GLA_CB = 16
GLA_NB = GLA_T // GLA_CB
VMEM_LIMIT = 56 * 1024 * 1024

assert N_CTX % TM == 0 and DEC_SEQ % TM == 0 and D_FF % FFN_TC == 0 and FFN_TC % LANES == 0
assert SEQ == GLA_T and DEC_SEQ % GLA_T == 0


def _group_of_tile(i):
    return jnp.where(i < N_CTX // TM, 0, 1 + (i - N_CTX // TM) // (DEC_SEQ // TM))


def _dot(a, b):
    return jnp.dot(a, b, preferred_element_type=F32)


def _dot_nt(a, b):
    return lax.dot_general(a, b, (((1,), (1,)), ((), ())), preferred_element_type=F32)


def _dot_tn(a, b):
    return lax.dot_general(a, b, (((0,), (0,)), ((), ())), preferred_element_type=F32)


def _rms(x, g):
    ms = jnp.mean(x * x, axis=-1, keepdims=True)
    return x * lax.rsqrt(ms + EPS) * g


def _silu(x):
    return x * jax.nn.sigmoid(x)


def _mod_kernel(c_ref, w_ref, b_ref, o_ref):
    s = _silu(c_ref[...]).astype(BF16)
    o_ref[0] = _dot(s, w_ref[0].astype(BF16)) + b_ref[0]


def _modulation(cvecs, w_mod, b_mod):
    n_out = N_MOD * D_MODEL
    return pl.pallas_call(
        _mod_kernel,
        grid=(DEPTH, n_out // MOD_TN),
        in_specs=[
            pl.BlockSpec((SUBLANES, D_MODEL), lambda l, j: (0, 0)),
            pl.BlockSpec((1, D_MODEL, MOD_TN), lambda l, j: (l, 0, j)),
            pl.BlockSpec((1, 1, MOD_TN), lambda l, j: (l, 0, j)),
        ],
        out_specs=pl.BlockSpec((1, SUBLANES, MOD_TN), lambda l, j: (l, 0, j)),
        out_shape=jax.ShapeDtypeStruct((DEPTH, SUBLANES, n_out), F32),
        compiler_params=pltpu.CompilerParams(
            dimension_semantics=("arbitrary", "arbitrary"), vmem_limit_bytes=VMEM_LIMIT),
        name="modulation",
    )(cvecs, w_mod, b_mod.reshape(DEPTH, 1, n_out))


_CTX_TILES = N_CTX // TM
_LAT_TILES = N_LAT // TM


def _ctx_tile(i):
    return jnp.minimum(i, _CTX_TILES - 1)


def _lat_tile(i):
    return jnp.maximum(i - _CTX_TILES, 0)


def _read_split(i, ctx_ref, lat_ref):
    return jnp.where(i < _CTX_TILES, ctx_ref[...], lat_ref[...])


def _ffn_kernel(*refs, m0, n0, layer, which, split_in, split_out):
    n_x = 2 if split_in else 1
    x_refs, (mod_ref, ng_ref, win_hbm, wout_hbm) = refs[:n_x], refs[n_x:n_x + 4]
    n_o = 2 if split_out else 1
    o_refs = refs[n_x + 4:n_x + 4 + n_o]
    h_scr, acc_scr, x_scr, wg_bf, wu_bf, wo_bf, stage_g, stage_u, stage_o, sem = refs[n_x + 4 + n_o:]
    i = pl.program_id(0)

    def chunk_copies(c, slot):
        cols = pl.ds(c * FFN_TC, FFN_TC)
        up_cols = pl.ds(D_FF + c * FFN_TC, FFN_TC)
        return (
            pltpu.make_async_copy(win_hbm.at[layer, which, :, cols], stage_g.at[slot], sem.at[0, slot]),
            pltpu.make_async_copy(win_hbm.at[layer, which, :, up_cols], stage_u.at[slot], sem.at[1, slot]),
            pltpu.make_async_copy(wout_hbm.at[layer, which, cols, :], stage_o.at[slot], sem.at[2, slot]),
        )

    def accumulate(c):
        h = h_scr[...]
        gate = _dot(h, wg_bf[c])
        up = _dot(h, wu_bf[c])
        part = _dot((_silu(gate) * up).astype(BF16), wo_bf[c])
        if c == 0:
            acc_scr[...] = part
        else:
            acc_scr[...] += part

    x = _read_split(i, *x_refs) if split_in else x_refs[0][...]
    x_scr[...] = x
    gain = ng_ref[n0:n0 + 1, :] * (1.0 + mod_ref[0, m0 + 1:m0 + 2, :])
    h_scr[...] = (_rms(x, gain) + mod_ref[0, m0:m0 + 1, :]).astype(BF16)

    @pl.when(i == 0)
    def _():
        for c in range(min(FFN_STAGES, FFN_NC)):
            for cp in chunk_copies(c, c):
                cp.start()
        for c in range(FFN_NC):
            slot = c % FFN_STAGES
            for cp in chunk_copies(c, slot):
                cp.wait()
            wg_bf[c] = stage_g[slot].astype(BF16)
            wu_bf[c] = stage_u[slot].astype(BF16)
            wo_bf[c] = stage_o[slot].astype(BF16)
            if c + FFN_STAGES < FFN_NC:
                for cp in chunk_copies(c + FFN_STAGES, slot):
                    cp.start()
            accumulate(c)

    @pl.when(i > 0)
    def _():
        for c in range(FFN_NC):
            accumulate(c)

    gain = 0.5 * mod_ref[0, m0 + 2:m0 + 3, :] * ng_ref[n0 + 1:n0 + 2, :]
    out = x_scr[...] + _rms(acc_scr[...], gain)
    if split_out:
        @pl.when(i < _CTX_TILES)
        def _():
            o_refs[0][...] = out

        @pl.when(i >= _CTX_TILES)
        def _():
            o_refs[1][...] = out
    else:
        o_refs[0][...] = out


def _ffn(x, mod_l, ng_l, w_in, w_out, *, layer, which, split_out=False):
    m0, n0 = (0, 0) if which == 0 else (6, 4)
    split_in = isinstance(x, tuple)
    tile = pl.BlockSpec((TM, D_MODEL), lambda i: (i, 0))
    ctx_tile = pl.BlockSpec((TM, D_MODEL), lambda i: (_ctx_tile(i), 0))
    lat_tile = pl.BlockSpec((TM, D_MODEL), lambda i: (_lat_tile(i), 0))
    if split_out:
        out_specs = [ctx_tile, lat_tile]
        out_shape = [jax.ShapeDtypeStruct((N_CTX, D_MODEL), F32), jax.ShapeDtypeStruct((N_LAT, D_MODEL), F32)]
    else:
        out_specs = tile
        out_shape = jax.ShapeDtypeStruct((N_ALL, D_MODEL), F32)
    return pl.pallas_call(
        functools.partial(_ffn_kernel, m0=m0, n0=n0, layer=layer, which=which, split_in=split_in,
                          split_out=split_out),
        grid=(N_ALL // TM,),
        in_specs=([ctx_tile, lat_tile] if split_in else [tile]) + [
            pl.BlockSpec((1, N_MOD, D_MODEL), lambda i: (_group_of_tile(i), 0, 0)),
            pl.BlockSpec((6, D_MODEL), lambda i: (0, 0)),
            pl.BlockSpec(memory_space=pl.ANY),
            pl.BlockSpec(memory_space=pl.ANY),
        ],
        out_specs=out_specs,
        out_shape=out_shape,
        scratch_shapes=[
            pltpu.VMEM((TM, D_MODEL), BF16),
            pltpu.VMEM((TM, D_MODEL), F32),
            pltpu.VMEM((TM, D_MODEL), F32),
            pltpu.VMEM((FFN_NC, D_MODEL, FFN_TC), BF16),
            pltpu.VMEM((FFN_NC, D_MODEL, FFN_TC), BF16),
            pltpu.VMEM((FFN_NC, FFN_TC, D_MODEL), BF16),
            pltpu.VMEM((FFN_STAGES, D_MODEL, FFN_TC), F32),
            pltpu.VMEM((FFN_STAGES, D_MODEL, FFN_TC), F32),
            pltpu.VMEM((FFN_STAGES, FFN_TC, D_MODEL), F32),
            pltpu.SemaphoreType.DMA((3, FFN_STAGES)),
        ],
        compiler_params=pltpu.CompilerParams(
            dimension_semantics=("arbitrary",), vmem_limit_bytes=VMEM_LIMIT),
        name="ffn",
    )(*(x if split_in else (x,)), mod_l, ng_l, w_in, w_out)


def _block_transpose_perm():
    r = lax.broadcasted_iota(jnp.int32, (GLA_T, GLA_T), 0)
    c = lax.broadcasted_iota(jnp.int32, (GLA_T, GLA_T), 1)
    return jnp.where(c == (r & (GLA_CB - 1)) * GLA_NB + (r >> 4), 1.0, 0.0).astype(BF16)


def _permute_chunks(perm, a):
    return jnp.concatenate([_dot(perm, a[c * GLA_T:(c + 1) * GLA_T, :]).astype(BF16)
                            for c in range(a.shape[0] // GLA_T)], axis=0)


_P_QK = 0
_P_VR = 4 * QK_W
_P_NAT = _P_VR + 2 * V_W
_P_LR = _P_NAT + 3 * NAT_W
_P_END = _P_LR + LANES


_NAT_SCALE = HD_NAT ** -0.5
assert _NAT_SCALE == 2.0 ** round(np.log2(_NAT_SCALE))
_GLA_COLS = 2 * QK_W + 2 * V_W
_REST_COLS = 2 * GLA_RANK + 3 * NAT_W
_REST_TILES = -(-_REST_COLS // LANES)
_PREP_ROWS = 128
assert _GLA_COLS % LANES == 0 and _GLA_COLS + _REST_COLS == 3104 and D_MODEL % _PREP_ROWS == 0


def _proj_prepare_weights(win_hbm, tail_ref, layer, w_bf, stage, sem):
    lane = lax.broadcasted_iota(jnp.int32, (_PREP_ROWS, LANES), 1)
    first_of_pair = ((lane >> 4) & 1) == 0
    shift = 2 * GLA_RANK

    head = pltpu.make_async_copy(win_hbm.at[layer, :, pl.ds(0, _GLA_COLS)], stage.at[:, pl.ds(0, _GLA_COLS)],
                                 sem.at[0])
    head.start()
    head.wait()
    for r0 in range(0, D_MODEL, _PREP_ROWS):
        rows = slice(r0, r0 + _PREP_ROWS)
        w_bf[rows, _P_QK:_P_QK + 2 * QK_W] = stage[rows, 0:2 * QK_W].astype(BF16)
        for t in range(2 * QK_W // LANES):
            tile = stage[rows, t * LANES:(t + 1) * LANES]
            partner = jnp.where(first_of_pair, pltpu.roll(tile, LANES - 16, axis=1), pltpu.roll(tile, 16, axis=1))
            w_bf[rows, _P_QK + 2 * QK_W + t * LANES:_P_QK + 2 * QK_W + (t + 1) * LANES] = partner.astype(BF16)
        w_bf[rows, _P_VR:_P_NAT] = stage[rows, 2 * QK_W:_GLA_COLS].astype(BF16)

    whole = (_REST_TILES - 1) * LANES
    stage[:, whole:] = tail_ref[0]
    rest = pltpu.make_async_copy(win_hbm.at[layer, :, pl.ds(_GLA_COLS, whole)], stage.at[:, pl.ds(0, whole)],
                                 sem.at[1])
    rest.start()
    rest.wait()
    for r0 in range(0, D_MODEL, _PREP_ROWS):
        rows = slice(r0, r0 + _PREP_ROWS)
        w_bf[rows, _P_LR:_P_END] = jnp.where(lane < shift, stage[rows, 0:LANES], 0.0).astype(BF16)
        for t in range(3 * NAT_W // LANES):
            lo = pltpu.roll(stage[rows, t * LANES:(t + 1) * LANES], LANES - shift, axis=1)
            hi = pltpu.roll(stage[rows, (t + 1) * LANES:(t + 2) * LANES], LANES - shift, axis=1)
            w_bf[rows, _P_NAT + t * LANES:_P_NAT + (t + 1) * LANES] = jnp.where(lane < LANES - shift, lo,
                                                                                  hi).astype(BF16)


def _proj_kernel(x_ref, mod_ref, ng_ref, win_hbm, tail_ref, wa_ref, ba_ref, cos_ref, sin_ref, *rest, layer):
    q_ref, k_ref, v_ref, r_ref, g_ref, qn_ref, kn_ctx_ref, vn_ctx_ref, kn_ref, vn_ref = rest[-13:-3]
    w_ref, stage, sem = rest[-3:]
    i = pl.program_id(0)

    @pl.when(i == 0)
    def _():
        _proj_prepare_weights(win_hbm, tail_ref, layer, w_ref, stage, sem)

    h = _rms(x_ref[...], ng_ref[2:3, :])
    h = (h * (1.0 + mod_ref[0, 4:5, :]) + mod_ref[0, 3:4, :]).astype(BF16)
    hp = _permute_chunks(_block_transpose_perm(), h)

    lr = _dot(hp, w_ref[:, _P_LR:_P_END]).astype(BF16)
    z = _dot(lr, wa_ref[...]) + ba_ref[...]
    g = (jnp.minimum(z, 0.0) - jnp.log1p(jnp.exp(-jnp.abs(z)))) * (1.0 / GATE_NORM)
    g_ref[0] = g[:, 0:QK_W]
    g_ref[1] = g[:, QK_W:2 * QK_W]

    qk = _dot(hp, w_ref[:, _P_QK:_P_VR])
    cos = cos_ref[...]
    sin = sin_ref[...]
    q_ref[...] = (qk[:, 0:QK_W] * cos + qk[:, 2 * QK_W:3 * QK_W] * sin) * (DK_GLA ** -0.5)
    k_ref[...] = qk[:, QK_W:2 * QK_W] * cos + qk[:, 3 * QK_W:4 * QK_W] * sin

    vr = _dot(hp, w_ref[:, _P_VR:_P_NAT])
    v_ref[...] = vr[:, 0:V_W]
    r_ref[...] = vr[:, V_W:2 * V_W]

    nat = _dot(h, w_ref[:, _P_NAT:_P_LR])
    qn_ref[...] = (nat[:, 0:NAT_W] * _NAT_SCALE).astype(BF16)
    kn_ref[...] = nat[:, NAT_W:2 * NAT_W].astype(BF16)
    vn_ref[...] = nat[:, 2 * NAT_W:3 * NAT_W].astype(BF16)

    @pl.when(i < _CTX_TILES)
    def _():
        heads = (TM // SEQ, SEQ, H_NAT, HD_NAT)
        kn_ctx_ref[:, 0] = nat[:, NAT_W:2 * NAT_W].reshape(heads)
        vn_ctx_ref[:, 0] = nat[:, 2 * NAT_W:3 * NAT_W].reshape(heads)
        for later in range(1, kn_ctx_ref.shape[1]):
            kn_ctx_ref[:, later] = jnp.zeros(heads, F32)
            vn_ctx_ref[:, later] = jnp.zeros(heads, F32)


def _rope_table_block(i):
    lat_tiles = DEC_SEQ // TM
    return jnp.where(i < N_CTX // TM, lat_tiles, (i - N_CTX // TM) % lat_tiles)


def _project(x, mod_l, ng_l, w_in, w_a, b_a, cos_t, sin_t, new_cache, *, layer):
    tile = lambda w: pl.BlockSpec((TM, w), lambda i: (i, 0))
    if new_cache:
        ctx_heads = pl.BlockSpec((TM // SEQ, 1, SEQ, H_NAT, HD_NAT), lambda i: (_ctx_tile(i), layer, 0, 0, 0))
    else:
        ctx_heads = pl.BlockSpec((TM // SEQ, DEPTH, SEQ, H_NAT, HD_NAT), lambda i: (_ctx_tile(i), 0, 0, 0, 0))
    full = lambda a: pl.BlockSpec(a.shape, lambda i: (0,) * a.ndim)
    n_in = 9
    tail0 = _GLA_COLS + (_REST_TILES - 1) * LANES
    w_tail = jnp.pad(w_in[:, :, tail0:], ((0, 0), (0, 0), (0, LANES - (w_in.shape[2] - tail0))))
    return pl.pallas_call(
        functools.partial(_proj_kernel, layer=layer),
        grid=(N_ALL // TM,),
        in_specs=[
            tile(D_MODEL),
            pl.BlockSpec((1, N_MOD, D_MODEL), lambda i: (_group_of_tile(i), 0, 0)),
            full(ng_l),
            pl.BlockSpec(memory_space=pl.ANY),
            pl.BlockSpec((1, D_MODEL, LANES), lambda i: (layer, 0, 0)),
            full(w_a), full(b_a),
            pl.BlockSpec((TM, QK_W), lambda i: (_rope_table_block(i), 0)),
            pl.BlockSpec((TM, QK_W), lambda i: (_rope_table_block(i), 0)),
        ] + [pl.BlockSpec(memory_space=pl.ANY)] * len(new_cache),
        input_output_aliases={n_in + n: 6 + n for n in range(len(new_cache))},
        out_specs=[
            tile(QK_W), tile(QK_W), tile(V_W), tile(V_W),
            pl.BlockSpec((2, TM, QK_W), lambda i: (0, i, 0)),
            tile(NAT_W), ctx_heads, ctx_heads, tile(NAT_W), tile(NAT_W),
        ],
        out_shape=[
            jax.ShapeDtypeStruct((N_ALL, QK_W), F32), jax.ShapeDtypeStruct((N_ALL, QK_W), F32),
            jax.ShapeDtypeStruct((N_ALL, V_W), F32), jax.ShapeDtypeStruct((N_ALL, V_W), F32),
            jax.ShapeDtypeStruct((2, N_ALL, QK_W), F32),
            jax.ShapeDtypeStruct((N_ALL, NAT_W), BF16),
            jax.ShapeDtypeStruct((BATCH, DEPTH, SEQ, H_NAT, HD_NAT), F32),
            jax.ShapeDtypeStruct((BATCH, DEPTH, SEQ, H_NAT, HD_NAT), F32),
            jax.ShapeDtypeStruct((N_ALL, NAT_W), BF16), jax.ShapeDtypeStruct((N_ALL, NAT_W), BF16),
        ],
        scratch_shapes=[
            pltpu.VMEM((D_MODEL, _P_END), BF16),
            pltpu.VMEM((D_MODEL, _REST_TILES * LANES), F32),
            pltpu.SemaphoreType.DMA((2,)),
        ],
        compiler_params=pltpu.CompilerParams(
            dimension_semantics=("arbitrary",), vmem_limit_bytes=VMEM_LIMIT),
        name="mixer_proj",
    )(x, mod_l, ng_l, w_in, w_tail, w_a, b_a, cos_t, sin_t, *new_cache)


def _gla_tables():
    rows = []
    seq_specs = [(b * (SEQ // GLA_T), SEQ // GLA_T) for b in range(BATCH)]
    seq_specs += [(N_CTX // GLA_T + b * (DEC_SEQ // GLA_T), DEC_SEQ // GLA_T) for b in range(DEC_BATCH)]
    for sid, (blk0, nchunk) in enumerate(seq_specs):
        if nchunk == 1:
            rows.append((blk0, _GLA_BOTH, 1, sid, blk0, 0))
            continue
        for direction in (0, 1):
            order = range(nchunk) if direction == 0 else range(nchunk - 1, -1, -1)
            for n, c in enumerate(order):
                out_blk = blk0 + (c if direction == 1 else nchunk - 1)
                rows.append((blk0 + c, direction, int(n == 0), sid, out_blk, c))
    return np.asarray(rows, dtype=np.int32).T.copy()


_GLA_BOTH = 2


_GLA_TAB = _gla_tables()
_GLA_ITEMS = _GLA_TAB.shape[1]
_GLA_NSEQ = BATCH + DEC_BATCH
_GLA_PAIR_ROWS = GLA_NB * GLA_CB * (GLA_CB + 1) // 2


def _gla_item(direction, q_ref, k_ref, v_ref, g_ref, res, s_scr, cp, sstk, p_scr, w_scr):
    T, CB, NB = GLA_T, GLA_CB, GLA_NB
    fwd = direction == 0
    slab = lambda i: slice(i * NB, (i + 1) * NB)

    cum = None
    for i in (range(CB) if fwd else range(CB - 1, -1, -1)):
        gi = g_ref[slab(i), :]
        cum = gi if cum is None else cum + gi
        cp[slab(i), :] = cum
    total = cum
    cum_all = cp[...]
    qt = q_ref[...] * jnp.exp(cum_all)
    kh = k_ref[...] * jnp.exp(jnp.concatenate([total] * CB, axis=0) - cum_all)
    dec_t = jnp.concatenate([jnp.exp(total), jnp.zeros((LANES - NB, QK_W), F32)], axis=0).T

    erow = lax.broadcasted_iota(jnp.int32, (QK_W, QK_W), 0)
    ecol = lax.broadcasted_iota(jnp.int32, (QK_W, QK_W), 1)
    head_sum = jnp.where((erow >> 6) == (ecol >> 6), 1.0, 0.0).astype(BF16)

    key_positions = lambda i: range(i + 1) if fwd else range(i, CB)
    r0 = 0
    for i in range(CB):
        qi = q_ref[slab(i), :]
        ci = cp[slab(i), :]
        for j in key_positions(i):
            e = jnp.exp(ci - cp[slab(j), :])
            p_scr[r0:r0 + NB, :] = (qi * k_ref[slab(j), :] * e).astype(BF16)
            r0 += NB
    w_scr[...] = _dot(p_scr[...], head_sum)
    first_half = lax.broadcasted_iota(jnp.int32, (NB, LANES), 1) < DK_GLA
    r0 = 0
    for i in range(CB):
        acc = None
        for j in key_positions(i):
            spread = []
            for pair in range(H_GLA // 2):
                tile = w_scr[r0:r0 + NB, pair * LANES:(pair + 1) * LANES]
                other = pltpu.roll(tile, DK_GLA, axis=1)
                spread += [jnp.where(first_half, tile, other), jnp.where(first_half, other, tile)]
            term = jnp.concatenate(spread, axis=1) * v_ref[slab(j), :]
            acc = term if acc is None else acc + term
            r0 += NB
        res[slab(i), :] = acc

    kht = kh.T.astype(BF16)
    v_bf = v_ref[...].astype(BF16)
    key_blk = lax.broadcasted_iota(jnp.int32, (DK_GLA, T), 1) & (NB - 1)
    row_blk = lax.broadcasted_iota(jnp.int32, (T, LANES), 0) & (NB - 1)
    lane_half = lax.broadcasted_iota(jnp.int32, (T, LANES), 1) >> 6
    order = range(NB) if fwd else range(NB - 1, -1, -1)
    for h in range(H_GLA):
        kh_h = kht[h * DK_GLA:(h + 1) * DK_GLA, :]
        kv = _dot(jnp.concatenate([jnp.where(key_blk == b, kh_h, 0) for b in range(NB)], axis=0),
                  v_bf[:, h * DV_GLA:(h + 1) * DV_GLA])
        s = s_scr[h]
        for b in order:
            sstk[h, b * DK_GLA:(b + 1) * DK_GLA, :] = s.astype(BF16)
            s = dec_t[h * DK_GLA:(h + 1) * DK_GLA, b:b + 1] * s + kv[b * DK_GLA:(b + 1) * DK_GLA, :]
        s_scr[h] = s

        pair_tile = qt[:, (h // 2) * LANES:(h // 2 + 1) * LANES]
        both = jnp.where(lane_half == h % 2, pair_tile, pltpu.roll(pair_tile, DK_GLA, axis=1))
        lhs = jnp.concatenate([jnp.where(row_blk == 2 * j + lane_half, both, 0.0).astype(BF16)
                               for j in range(NB // 2)], axis=1)
        res[:, h * DV_GLA:(h + 1) * DV_GLA] += _dot(lhs, sstk[h])


def _gla_kernel(tab_ref, q_ref, k_ref, v_ref, g_ref, s0_ref, o_ref, so_ref,
                s_scr, cp, sstk, p_scr, w_scr, res, held):
    it = pl.program_id(0)
    kind = tab_ref[1, it]
    slot = tab_ref[5, it]

    def run(d):
        _gla_item(d, q_ref, k_ref, v_ref, g_ref.at[d], res.at[d], s_scr.at[d], cp.at[d], sstk.at[d],
                  p_scr.at[d], w_scr.at[d])

    @pl.when(kind == _GLA_BOTH)
    def _():
        s_scr[...] = s0_ref[0]
        run(0)
        run(1)
        o_ref[...] = res[0] + res[1]
        so_ref[0] = s_scr[...]

    for d in (0, 1):
        @pl.when(kind == d)
        def _(d=d):
            @pl.when(tab_ref[2, it] == 1)
            def _():
                s_scr[d] = s0_ref[0, d]

            run(d)
            if d == 0:
                held[slot] = res[0]
            else:
                o_ref[...] = held[slot] + res[1]
            so_ref[0, d] = s_scr[d]


def _gla(q, k, v, g, s0):
    tok = lambda w: pl.BlockSpec((GLA_T, w), lambda it, tab: (tab[0, it], 0))
    state = pl.BlockSpec((1, 2, H_GLA, DK_GLA, DV_GLA), lambda it, tab: (tab[3, it], 0, 0, 0, 0))
    grid_spec = pltpu.PrefetchScalarGridSpec(
        num_scalar_prefetch=1,
        grid=(_GLA_ITEMS,),
        in_specs=[
            tok(QK_W), tok(QK_W), tok(V_W),
            pl.BlockSpec((2, GLA_T, QK_W), lambda it, tab: (0, tab[0, it], 0)),
            state,
        ],
        out_specs=[
            pl.BlockSpec((GLA_T, V_W), lambda it, tab: (tab[4, it], 0)),
            state,
        ],
        scratch_shapes=[
            pltpu.VMEM((2, H_GLA, DK_GLA, DV_GLA), F32),
            pltpu.VMEM((2, GLA_T, QK_W), F32),
            pltpu.VMEM((2, H_GLA, GLA_NB * DK_GLA, DV_GLA), BF16),
            pltpu.VMEM((2, _GLA_PAIR_ROWS, QK_W), BF16),
            pltpu.VMEM((2, _GLA_PAIR_ROWS, QK_W), F32),
            pltpu.VMEM((2, GLA_T, V_W), F32),
            pltpu.VMEM((DEC_SEQ // GLA_T, GLA_T, V_W), F32),
        ],
    )
    return pl.pallas_call(
        _gla_kernel,
        grid_spec=grid_spec,
        out_shape=[
            jax.ShapeDtypeStruct((N_ALL, V_W), F32),
            jax.ShapeDtypeStruct((_GLA_NSEQ, 2, H_GLA, DK_GLA, DV_GLA), F32),
        ],
        compiler_params=pltpu.CompilerParams(
            dimension_semantics=("arbitrary",), vmem_limit_bytes=VMEM_LIMIT),
        name="gla",
    )(jnp.asarray(_GLA_TAB), q, k, v, g, s0)


def _softmax_pv(s_list, v_list):
    m = s_list[0].max(axis=-1, keepdims=True)
    for s in s_list[1:]:
        m = jnp.maximum(m, s.max(axis=-1, keepdims=True))
    num = None
    den = None
    for s, vv in zip(s_list, v_list):
        e = jnp.exp(s - m)
        den = e.sum(axis=-1, keepdims=True) if den is None else den + e.sum(axis=-1, keepdims=True)
        pv = _dot(e.astype(BF16), vv)
        num = pv if num is None else num + pv
    return num / den


def _ctx_attn_kernel(q_ref, k_ref, v_ref, o_ref):
    lane =lax.broadcasted_iota(jnp.int32, (SEQ, LANES), 1)
    for t in range(NAT_W // LANES):
        sl = slice(t * LANES, (t + 1) * LANES)
        qt = q_ref[:, sl]
        kt = k_ref[:, sl].astype(BF16)
        vt = v_ref[:, sl].astype(BF16)
        out = jnp.zeros((SEQ, LANES), F32)
        for half in range(LANES // HD_NAT):
            mine = (lane >> 6) == half
            s = _dot_nt(jnp.where(mine, qt, 0).astype(BF16), kt)
            out = jnp.where(mine, _softmax_pv([s], [vt]), out)
        o_ref[:, sl] = out.astype(o_ref.dtype)


def _ctx_attention(qn, kn, vn):
    spec = pl.BlockSpec((SEQ, NAT_W), lambda b: (b, 0))
    return pl.pallas_call(
        _ctx_attn_kernel,
        grid=(BATCH,),
        in_specs=[spec, spec, spec],
        out_specs=spec,
        out_shape=jax.ShapeDtypeStruct((N_CTX, NAT_W), BF16),
        compiler_params=pltpu.CompilerParams(
            dimension_semantics=("arbitrary",), vmem_limit_bytes=VMEM_LIMIT),
        name="ctx_attention",
    )(qn, kn, vn)


_NAT_QROWS = 4
_NAT_GROUPS = GRID_ROWS // _NAT_QROWS
_NAT_KROWS = WIN_H + _NAT_QROWS
_NAT_Q = _NAT_QROWS * GRID_W
_NAT_KEYS = _NAT_KROWS * GRID_W
_NAT_DR = 2 * WIN_H - 1
_NAT_DC = 2 * WIN_W - 1


def _nat_key_row0(r):
    return jnp.clip(r - WIN_H // 2, 0, GRID_ROWS - WIN_H)


def _nat_build_bias(rpb_ref, tz_scr):
    c = lax.broadcasted_iota(jnp.int32, (GRID_W, LANES), 0)
    lane = lax.broadcasted_iota(jnp.int32, (GRID_W, LANES), 1)
    kc = lane & (GRID_W - 1)
    second = (lane >> 6) == 1
    win_start = jnp.clip(c - WIN_W // 2, 0, GRID_W - WIN_W)
    valid = (kc >= win_start) & (kc < win_start + WIN_W)

    def one_row(n, carry):
        dr = n >> 3
        h = n & (H_NAT - 1)
        src = jnp.broadcast_to(rpb_ref[pl.ds(h * _NAT_DR + dr, 1), :], (GRID_W, LANES))
        rolled = pltpu.roll(src, LANES - (WIN_W - 1), axis=1, stride=1, stride_axis=0)
        tz_scr[dr, h] = jnp.where(valid, rolled, NEG_INF)
        return carry

    lax.fori_loop(0, _NAT_DR * H_NAT, one_row, 0)

    def pair_rows(n, carry):
        dr = n >> 3
        h = n & (H_NAT - 1)
        tz_scr[dr, h] = jnp.where(second, tz_scr[dr + 1, h], tz_scr[dr, h])
        return carry

    lax.fori_loop(0, (_NAT_DR - 1) * H_NAT, pair_rows, 0)


def _nat_kernel(rpb_ref, q_ref, k_ref, v_ref, ck_ref, cv_ref, o_ref, tz_scr, bias_scr):
    grp = pl.program_id(1)

    @pl.when((pl.program_id(0) == 0) & (grp == 0))
    def _():
        _nat_build_bias(rpb_ref, tz_scr)

    krow0 = jnp.clip(_NAT_QROWS * grp - WIN_H // 2, 0, GRID_ROWS - _NAT_KROWS)
    k0 = pl.multiple_of(krow0 * GRID_W, GRID_W)
    lane_q = lax.broadcasted_iota(jnp.int32, (_NAT_Q, LANES), 1)
    lane_b = lax.broadcasted_iota(jnp.int32, (GRID_W, LANES), 1)
    neg = jnp.full((GRID_W, LANES), NEG_INF, F32)
    for t in range(NAT_W // LANES):
        sl = slice(t * LANES, (t + 1) * LANES)
        qt = q_ref[:, sl]
        kw = k_ref[pl.ds(k0, _NAT_KEYS), sl].astype(BF16)
        vw = v_ref[pl.ds(k0, _NAT_KEYS), sl].astype(BF16)
        ck = ck_ref[0, 0, :, sl].astype(BF16)
        cv = cv_ref[0, 0, :, sl].astype(BF16)
        out = jnp.zeros((_NAT_Q, LANES), F32)
        for half in range(LANES // HD_NAT):
            h = 2 * t + half
            for qr in range(_NAT_QROWS):
                r = _NAT_QROWS * grp + qr
                lo = _nat_key_row0(r)
                for kp in range(_NAT_KROWS // 2):
                    kr = krow0 + 2 * kp
                    tile = tz_scr[jnp.clip(kr - r + (WIN_H - 1), 0, _NAT_DR - 1), h]
                    ok_a = ((kr >= lo) & (kr < lo + WIN_H)).astype(jnp.int32)
                    ok_b = ((kr + 1 >= lo) & (kr + 1 < lo + WIN_H)).astype(jnp.int32)
                    ok = jnp.where(lane_b < GRID_W, ok_a, ok_b) == 1
                    bias_scr[qr * GRID_W:(qr + 1) * GRID_W, kp * LANES:(kp + 1) * LANES] = jnp.where(ok, tile, neg)
            mine = (lane_q >> 6) == half
            qm = jnp.where(mine, qt, 0).astype(BF16)
            s_win = _dot_nt(qm, kw) + bias_scr[...]
            s_ctx = _dot_nt(qm, ck)
            out = jnp.where(mine, _softmax_pv([s_win, s_ctx], [vw, cv]), out)
        o_ref[:, sl] = out.astype(o_ref.dtype)


def _nat_attention(qn, kn, vn, cache_k, cache_v, rpb_l, layer):
    lat0 = N_CTX // _NAT_Q
    half = jnp.pad(rpb_l.reshape(H_NAT * _NAT_DR, _NAT_DC), ((0, 0), (0, GRID_W - _NAT_DC)))
    rpb_rows = jnp.concatenate([half, half], axis=1)
    return pl.pallas_call(
        _nat_kernel,
        grid=(DEC_BATCH, _NAT_GROUPS),
        in_specs=[
            pl.BlockSpec((H_NAT * _NAT_DR, LANES), lambda b, g: (0, 0)),
            pl.BlockSpec((_NAT_Q, NAT_W), lambda b, g: (lat0 + b * _NAT_GROUPS + g, 0)),
            pl.BlockSpec((DEC_SEQ, NAT_W), lambda b, g: (N_CTX // DEC_SEQ + b, 0)),
            pl.BlockSpec((DEC_SEQ, NAT_W), lambda b, g: (N_CTX // DEC_SEQ + b, 0)),
            pl.BlockSpec((1, 1, PAST_LEN, NAT_W), lambda b, g: (b, layer, 0, 0)),
            pl.BlockSpec((1, 1, PAST_LEN, NAT_W), lambda b, g: (b, layer, 0, 0)),
        ],
        out_specs=pl.BlockSpec((_NAT_Q, NAT_W), lambda b, g: (b * _NAT_GROUPS + g, 0)),
        out_shape=jax.ShapeDtypeStruct((N_LAT, NAT_W), BF16),
        scratch_shapes=[
            pltpu.VMEM((_NAT_DR, H_NAT, GRID_W, LANES), F32),
            pltpu.VMEM((_NAT_Q, _NAT_KEYS), F32),
        ],
        compiler_params=pltpu.CompilerParams(
            dimension_semantics=("arbitrary", "arbitrary"), vmem_limit_bytes=VMEM_LIMIT),
        name="nat_attention",
    )(rpb_rows, qn, kn, vn, cache_k, cache_v)


def _out_kernel(x_ref, o_ref, r_ref, on_ctx_ref, on_lat_ref, mod_ref, ng_ref, gng_ref, w_ref, y_ref):
    o_nat = _read_split(pl.program_id(0), on_ctx_ref, on_lat_ref)
    og = o_ref[...]
    parts = []
    for h in range(H_GLA):
        parts.append(_rms(og[:, h * DV_GLA:(h + 1) * DV_GLA], gng_ref[...]))
    merged = (jnp.concatenate(parts, axis=1) * _silu(r_ref[...])).astype(BF16)
    merged = _permute_chunks(_block_transpose_perm(), merged)
    y = _dot(merged, w_ref[0:V_W, :]) + _dot(o_nat.astype(BF16), w_ref[V_W:, :])
    y_ref[...] = x_ref[...] + mod_ref[0, 5:6, :] * _rms(y, ng_ref[3:4, :])


def _mixer_out(x, o_gla, r, o_ctx, o_lat, mod_l, ng_l, gng_l, w_out_bf):
    tile = lambda w: pl.BlockSpec((TM, w), lambda i: (i, 0))
    full = lambda a: pl.BlockSpec(a.shape, lambda i: (0,) * a.ndim)
    return pl.pallas_call(
        _out_kernel,
        grid=(N_ALL // TM,),
        in_specs=[
            tile(D_MODEL),
            tile(V_W),
            tile(V_W),
            pl.BlockSpec((TM, NAT_W), lambda i: (_ctx_tile(i), 0)),
            pl.BlockSpec((TM, NAT_W), lambda i: (_lat_tile(i), 0)),
            pl.BlockSpec((1, N_MOD, D_MODEL), lambda i: (_group_of_tile(i), 0, 0)),
            full(ng_l), full(gng_l), full(w_out_bf),
        ],
        out_specs=tile(D_MODEL),
        out_shape=jax.ShapeDtypeStruct((N_ALL, D_MODEL), F32),
        compiler_params=pltpu.CompilerParams(
            dimension_semantics=("arbitrary",), vmem_limit_bytes=VMEM_LIMIT),
        name="mixer_out",
    )(x, o_gla, r, o_ctx, o_lat, mod_l, ng_l, gng_l, w_out_bf)


def _rope_tables():
    quarter = DK_GLA // 4
    freqs = ROPE_BASE ** (-jnp.arange(quarter, dtype=F32) / quarter)
    t = jnp.arange(DEC_SEQ)
    ang_r = (t // GRID_W).astype(F32)[:, None] * freqs[None, :]
    ang_c = (t % GRID_W).astype(F32)[:, None] * freqs[None, :]
    cos_h = jnp.concatenate([jnp.cos(ang_r), jnp.cos(ang_r), jnp.cos(ang_c), jnp.cos(ang_c)], axis=1)
    sin_h = jnp.concatenate([-jnp.sin(ang_r), jnp.sin(ang_r), -jnp.sin(ang_c), jnp.sin(ang_c)], axis=1)
    pos_major = lambda a: a.reshape(-1, GLA_NB, GLA_CB, QK_W).transpose(0, 2, 1, 3).reshape(a.shape)
    cos_t = jnp.concatenate([pos_major(jnp.tile(cos_h, (1, H_GLA))), jnp.ones((TM, QK_W), F32)], axis=0)
    sin_t = jnp.concatenate([pos_major(jnp.tile(sin_h, (1, H_GLA))), jnp.zeros((TM, QK_W), F32)], axis=0)
    return cos_t, sin_t


def _gate_up_weights(gla_wa2_l, gla_ba_l):
    w_a = jnp.zeros((LANES, 2 * QK_W), F32)
    w_a = w_a.at[0:GLA_RANK, 0:QK_W].set(gla_wa2_l[0]).at[GLA_RANK:2 * GLA_RANK, QK_W:].set(gla_wa2_l[1])
    b_a = jnp.concatenate([gla_ba_l[0], gla_ba_l[1]])[None, :]
    return w_a.astype(BF16), b_a


def kernel(x_prompt, x_sample, cache_k, cache_v, state_gla, c, c_ctx, w_mod, b_mod, norm_g, ffn_w_in, ffn_w_out,
           w_in, gla_wa2, gla_ba, gla_norm_g, nat_rpb, w_out):
    cvecs = jnp.zeros((SUBLANES, D_MODEL), F32).at[0].set(c_ctx).at[1:1 + DEC_BATCH].set(c)
    mod = _modulation(cvecs, w_mod, b_mod)[:, :N_GROUPS].reshape(DEPTH, N_GROUPS, N_MOD, D_MODEL)

    w_out_bf = w_out.astype(BF16)
    cos_t, sin_t = _rope_tables()
    ck = cache_k.reshape(DEC_BATCH, DEPTH, PAST_LEN, NAT_W)
    cv = cache_v.reshape(DEC_BATCH, DEPTH, PAST_LEN, NAT_W)

    x = (x_prompt.reshape(N_CTX, D_MODEL), x_sample.reshape(N_LAT, D_MODEL))
    new_cache, s_list = (), []
    for l in range(DEPTH):
        mod_l, ng_l = mod[l], norm_g[l]
        x = _ffn(x, mod_l, ng_l, ffn_w_in, ffn_w_out, layer=l, which=0)

        w_a, b_a = _gate_up_weights(gla_wa2[l], gla_ba[l])
        q, k, v, r, g, qn, new_k, new_v, kn, vn = _project(
            x, mod_l, ng_l, w_in, w_a, b_a, cos_t, sin_t, new_cache, layer=l)
        new_cache = (new_k, new_v)

        s0 = jnp.concatenate([jnp.zeros((BATCH, 2, H_GLA, DK_GLA, DV_GLA), F32), state_gla[:, l]], axis=0)
        o_gla, s_fin = _gla(q, k, v, g, s0)
        o_ctx = _ctx_attention(qn, kn, vn)
        o_lat = _nat_attention(qn, kn, vn, ck, cv, nat_rpb[l], l)

        x = _mixer_out(x, o_gla, r, o_ctx, o_lat, mod_l, ng_l, gla_norm_g[l][None, :], w_out_bf[l])
        x = _ffn(x, mod_l, ng_l, ffn_w_in, ffn_w_out, layer=l, which=1, split_out=(l == DEPTH - 1))

        s_list.append(s_fin[:BATCH])

    y_prompt = x[0].reshape(BATCH, SEQ, D_MODEL)
    y_sample = x[1].reshape(DEC_BATCH, DEC_SEQ, D_MODEL)
    return (y_prompt, y_sample, new_cache[0], new_cache[1], jnp.stack(s_list, axis=1))
```

```python
import functools

import numpy as np
import jax
import jax.numpy as jnp
from jax import lax
from jax.experimental import pallas as pl
from jax.experimental.pallas import tpu as pltpu

D_MODEL = 1024
BATCH = 16
SEQ = 256
DEPTH = 2
DEC_BATCH = 2
DEC_SEQ = 1024
PAST_LEN = 512
GRID_W = 64
H_GLA = 4
DK_GLA = 64
DV_GLA = 128
GLA_RANK = 16
GATE_NORM = 16.0
H_NAT = 8
HD_NAT = 64
WIN_H = 8
WIN_W = 16
D_FF = 2816
N_MOD = 9
ROPE_BASE = 10000.0
EPS = 1e-6
NEG_INF = -1e30

F32 = jnp.float32
BF16 = jnp.bfloat16

N_CTX = BATCH * SEQ
N_LAT = DEC_BATCH * DEC_SEQ
N_ALL = N_CTX + N_LAT
N_GROUPS = 1 + DEC_BATCH
QK_W = H_GLA * DK_GLA
V_W = H_GLA * DV_GLA
NAT_W = H_NAT * HD_NAT
GRID_ROWS = DEC_SEQ // GRID_W

LANES = 128
SUBLANES = 8

TM = 512
FFN_TC = 256
FFN_NC = D_FF // FFN_TC
FFN_STAGES = 2
MOD_TN = 1152
GLA_T = 256
GLA_CB = 16
GLA_NB = GLA_T // GLA_CB
VMEM_LIMIT = 56 * 1024 * 1024

assert N_CTX % TM == 0 and DEC_SEQ % TM == 0 and D_FF % FFN_TC == 0 and FFN_TC % LANES == 0
assert SEQ == GLA_T and DEC_SEQ % GLA_T == 0


def _group_of_tile(i):
    return jnp.where(i < N_CTX // TM, 0, 1 + (i - N_CTX // TM) // (DEC_SEQ // TM))


def _dot(a, b):
    return jnp.dot(a, b, preferred_element_type=F32)


def _dot_nt(a, b):
    return lax.dot_general(a, b, (((1,), (1,)), ((), ())), preferred_element_type=F32)


def _dot_tn(a, b):
    return lax.dot_general(a, b, (((0,), (0,)), ((), ())), preferred_element_type=F32)


def _rms(x, g):
    ms = jnp.mean(x * x, axis=-1, keepdims=True)
    return x * lax.rsqrt(ms + EPS) * g


def _silu(x):
    return x * jax.nn.sigmoid(x)


def _mod_kernel(c_ref, w_ref, b_ref, o_ref):
    s = _silu(c_ref[...]).astype(BF16)
    o_ref[0] = _dot(s, w_ref[0].astype(BF16)) + b_ref[0]


def _modulation(cvecs, w_mod, b_mod):
    n_out = N_MOD * D_MODEL
    return pl.pallas_call(
        _mod_kernel,
        grid=(DEPTH, n_out // MOD_TN),
        in_specs=[
            pl.BlockSpec((SUBLANES, D_MODEL), lambda l, j: (0, 0)),
            pl.BlockSpec((1, D_MODEL, MOD_TN), lambda l, j: (l, 0, j)),
            pl.BlockSpec((1, 1, MOD_TN), lambda l, j: (l, 0, j)),
        ],
        out_specs=pl.BlockSpec((1, SUBLANES, MOD_TN), lambda l, j: (l, 0, j)),
        out_shape=jax.ShapeDtypeStruct((DEPTH, SUBLANES, n_out), F32),
        compiler_params=pltpu.CompilerParams(
            dimension_semantics=("arbitrary", "arbitrary"), vmem_limit_bytes=VMEM_LIMIT),
        name="modulation",
    )(cvecs, w_mod, b_mod.reshape(DEPTH, 1, n_out))


_CTX_TILES = N_CTX // TM
_LAT_TILES = N_LAT // TM


def _ctx_tile(i):
    return jnp.minimum(i, _CTX_TILES - 1)


def _lat_tile(i):
    return jnp.maximum(i - _CTX_TILES, 0)


def _read_split(i, ctx_ref, lat_ref):
    return jnp.where(i < _CTX_TILES, ctx_ref[...], lat_ref[...])


def _ffn_kernel(*refs, m0, n0, layer, which, split_in, split_out):
    n_x = 2 if split_in else 1
    x_refs, (mod_ref, ng_ref, win_hbm, wout_hbm) = refs[:n_x], refs[n_x:n_x + 4]
    n_o = 2 if split_out else 1
    o_refs = refs[n_x + 4:n_x + 4 + n_o]
    h_scr, acc_scr, x_scr, wg_bf, wu_bf, wo_bf, stage_g, stage_u, stage_o, sem = refs[n_x + 4 + n_o:]
    i = pl.program_id(0)

    def chunk_copies(c, slot):
        cols = pl.ds(c * FFN_TC, FFN_TC)
        up_cols = pl.ds(D_FF + c * FFN_TC, FFN_TC)
        return (
            pltpu.make_async_copy(win_hbm.at[layer, which, :, cols], stage_g.at[slot], sem.at[0, slot]),
            pltpu.make_async_copy(win_hbm.at[layer, which, :, up_cols], stage_u.at[slot], sem.at[1, slot]),
            pltpu.make_async_copy(wout_hbm.at[layer, which, cols, :], stage_o.at[slot], sem.at[2, slot]),
        )

    def accumulate(c):
        h = h_scr[...]
        gate = _dot(h, wg_bf[c])
        up = _dot(h, wu_bf[c])
        part = _dot((_silu(gate) * up).astype(BF16), wo_bf[c])
        if c == 0:
            acc_scr[...] = part
        else:
            acc_scr[...] += part

    x = _read_split(i, *x_refs) if split_in else x_refs[0][...]
    x_scr[...] = x
    gain = ng_ref[n0:n0 + 1, :] * (1.0 + mod_ref[0, m0 + 1:m0 + 2, :])
    h_scr[...] = (_rms(x, gain) + mod_ref[0, m0:m0 + 1, :]).astype(BF16)

    @pl.when(i == 0)
    def _():
        for c in range(min(FFN_STAGES, FFN_NC)):
            for cp in chunk_copies(c, c):
                cp.start()
        for c in range(FFN_NC):
            slot = c % FFN_STAGES
            for cp in chunk_copies(c, slot):
                cp.wait()
            wg_bf[c] = stage_g[slot].astype(BF16)
            wu_bf[c] = stage_u[slot].astype(BF16)
            wo_bf[c] = stage_o[slot].astype(BF16)
            if c + FFN_STAGES < FFN_NC:
                for cp in chunk_copies(c + FFN_STAGES, slot):
                    cp.start()
            accumulate(c)

    @pl.when(i > 0)
    def _():
        for c in range(FFN_NC):
            accumulate(c)

    gain = 0.5 * mod_ref[0, m0 + 2:m0 + 3, :] * ng_ref[n0 + 1:n0 + 2, :]
    out = x_scr[...] + _rms(acc_scr[...], gain)
    if split_out:
        @pl.when(i < _CTX_TILES)
        def _():
            o_refs[0][...] = out

        @pl.when(i >= _CTX_TILES)
        def _():
            o_refs[1][...] = out
    else:
        o_refs[0][...] = out


def _ffn(x, mod_l, ng_l, w_in, w_out, *, layer, which, split_out=False):
    m0, n0 = (0, 0) if which == 0 else (6, 4)
    split_in = isinstance(x, tuple)
    tile = pl.BlockSpec((TM, D_MODEL), lambda i: (i, 0))
    ctx_tile = pl.BlockSpec((TM, D_MODEL), lambda i: (_ctx_tile(i), 0))
    lat_tile = pl.BlockSpec((TM, D_MODEL), lambda i: (_lat_tile(i), 0))
    if split_out:
        out_specs = [ctx_tile, lat_tile]
        out_shape = [jax.ShapeDtypeStruct((N_CTX, D_MODEL), F32), jax.ShapeDtypeStruct((N_LAT, D_MODEL), F32)]
    else:
        out_specs = tile
        out_shape = jax.ShapeDtypeStruct((N_ALL, D_MODEL), F32)
    return pl.pallas_call(
        functools.partial(_ffn_kernel, m0=m0, n0=n0, layer=layer, which=which, split_in=split_in,
                          split_out=split_out),
        grid=(N_ALL // TM,),
        in_specs=([ctx_tile, lat_tile] if split_in else [tile]) + [
            pl.BlockSpec((1, N_MOD, D_MODEL), lambda i: (_group_of_tile(i), 0, 0)),
            pl.BlockSpec((6, D_MODEL), lambda i: (0, 0)),
            pl.BlockSpec(memory_space=pl.ANY),
            pl.BlockSpec(memory_space=pl.ANY),
        ],
        out_specs=out_specs,
        out_shape=out_shape,
        scratch_shapes=[
            pltpu.VMEM((TM, D_MODEL), BF16),
            pltpu.VMEM((TM, D_MODEL), F32),
            pltpu.VMEM((TM, D_MODEL), F32),
            pltpu.VMEM((FFN_NC, D_MODEL, FFN_TC), BF16),
            pltpu.VMEM((FFN_NC, D_MODEL, FFN_TC), BF16),
            pltpu.VMEM((FFN_NC, FFN_TC, D_MODEL), BF16),
            pltpu.VMEM((FFN_STAGES, D_MODEL, FFN_TC), F32),
            pltpu.VMEM((FFN_STAGES, D_MODEL, FFN_TC), F32),
            pltpu.VMEM((FFN_STAGES, FFN_TC, D_MODEL), F32),
            pltpu.SemaphoreType.DMA((3, FFN_STAGES)),
        ],
        compiler_params=pltpu.CompilerParams(
            dimension_semantics=("arbitrary",), vmem_limit_bytes=VMEM_LIMIT),
        name="ffn",
    )(*(x if split_in else (x,)), mod_l, ng_l, w_in, w_out)


def _block_transpose_perm():
    r = lax.broadcasted_iota(jnp.int32, (GLA_T, GLA_T), 0)
    c = lax.broadcasted_iota(jnp.int32, (GLA_T, GLA_T), 1)
    return jnp.where(c == (r & (GLA_CB - 1)) * GLA_NB + (r >> 4), 1.0, 0.0).astype(BF16)


def _permute_chunks(perm, a):
    return jnp.concatenate([_dot(perm, a[c * GLA_T:(c + 1) * GLA_T, :]).astype(BF16)
                            for c in range(a.shape[0] // GLA_T)], axis=0)


_P_QK = 0
_P_VR = 4 * QK_W
_P_NAT = _P_VR + 2 * V_W
_P_LR = _P_NAT + 3 * NAT_W
_P_END = _P_LR + LANES


_NAT_SCALE = HD_NAT ** -0.5
assert _NAT_SCALE == 2.0 ** round(np.log2(_NAT_SCALE))
_GLA_COLS = 2 * QK_W + 2 * V_W
_REST_COLS = 2 * GLA_RANK + 3 * NAT_W
_REST_TILES = -(-_REST_COLS // LANES)
_PREP_ROWS = 128
assert _GLA_COLS % LANES == 0 and _GLA_COLS + _REST_COLS == 3104 and D_MODEL % _PREP_ROWS == 0


def _proj_prepare_weights(win_hbm, tail_ref, layer, w_bf, stage, sem):
    lane = lax.broadcasted_iota(jnp.int32, (_PREP_ROWS, LANES), 1)
    first_of_pair = ((lane >> 4) & 1) == 0
    shift = 2 * GLA_RANK

    head = pltpu.make_async_copy(win_hbm.at[layer, :, pl.ds(0, _GLA_COLS)], stage.at[:, pl.ds(0, _GLA_COLS)],
                                 sem.at[0])
    head.start()
    head.wait()
    for r0 in range(0, D_MODEL, _PREP_ROWS):
        rows = slice(r0, r0 + _PREP_ROWS)
        w_bf[rows, _P_QK:_P_QK + 2 * QK_W] = stage[rows, 0:2 * QK_W].astype(BF16)
        for t in range(2 * QK_W // LANES):
            tile = stage[rows, t * LANES:(t + 1) * LANES]
            partner = jnp.where(first_of_pair, pltpu.roll(tile, LANES - 16, axis=1), pltpu.roll(tile, 16, axis=1))
            w_bf[rows, _P_QK + 2 * QK_W + t * LANES:_P_QK + 2 * QK_W + (t + 1) * LANES] = partner.astype(BF16)
        w_bf[rows, _P_VR:_P_NAT] = stage[rows, 2 * QK_W:_GLA_COLS].astype(BF16)

    whole = (_REST_TILES - 1) * LANES
    stage[:, whole:] = tail_ref[0]
    rest = pltpu.make_async_copy(win_hbm.at[layer, :, pl.ds(_GLA_COLS, whole)], stage.at[:, pl.ds(0, whole)],
                                 sem.at[1])
    rest.start()
    rest.wait()
    for r0 in range(0, D_MODEL, _PREP_ROWS):
        rows = slice(r0, r0 + _PREP_ROWS)
        w_bf[rows, _P_LR:_P_END] = jnp.where(lane < shift, stage[rows, 0:LANES], 0.0).astype(BF16)
        for t in range(3 * NAT_W // LANES):
            lo = pltpu.roll(stage[rows, t * LANES:(t + 1) * LANES], LANES - shift, axis=1)
            hi = pltpu.roll(stage[rows, (t + 1) * LANES:(t + 2) * LANES], LANES - shift, axis=1)
            w_bf[rows, _P_NAT + t * LANES:_P_NAT + (t + 1) * LANES] = jnp.where(lane < LANES - shift, lo,
                                                                                  hi).astype(BF16)


def _proj_kernel(x_ref, mod_ref, ng_ref, win_hbm, tail_ref, wa_ref, ba_ref, cos_ref, sin_ref, *rest, layer):
    q_ref, k_ref, v_ref, r_ref, g_ref, qn_ref, kn_ctx_ref, vn_ctx_ref, kn_ref, vn_ref = rest[-13:-3]
    w_ref, stage, sem = rest[-3:]
    i = pl.program_id(0)

    @pl.when(i == 0)
    def _():
        _proj_prepare_weights(win_hbm, tail_ref, layer, w_ref, stage, sem)

    h = _rms(x_ref[...], ng_ref[2:3, :])
    h = (h * (1.0 + mod_ref[0, 4:5, :]) + mod_ref[0, 3:4, :]).astype(BF16)
    hp = _permute_chunks(_block_transpose_perm(), h)

    lr = _dot(hp, w_ref[:, _P_LR:_P_END]).astype(BF16)
    z = _dot(lr, wa_ref[...]) + ba_ref[...]
    g = (jnp.minimum(z, 0.0) - jnp.log1p(jnp.exp(-jnp.abs(z)))) * (1.0 / GATE_NORM)
    g_ref[0] = g[:, 0:QK_W]
    g_ref[1] = g[:, QK_W:2 * QK_W]

    qk = _dot(hp, w_ref[:, _P_QK:_P_VR])
    cos = cos_ref[...]
    sin = sin_ref[...]
    q_ref[...] = (qk[:, 0:QK_W] * cos + qk[:, 2 * QK_W:3 * QK_W] * sin) * (DK_GLA ** -0.5)
    k_ref[...] = qk[:, QK_W:2 * QK_W] * cos + qk[:, 3 * QK_W:4 * QK_W] * sin

    vr = _dot(hp, w_ref[:, _P_VR:_P_NAT])
    v_ref[...] = vr[:, 0:V_W]
    r_ref[...] = vr[:, V_W:2 * V_W]

    nat = _dot(h, w_ref[:, _P_NAT:_P_LR])
    qn_ref[...] = (nat[:, 0:NAT_W] * _NAT_SCALE).astype(BF16)
    kn_ref[...] = nat[:, NAT_W:2 * NAT_W].astype(BF16)
    vn_ref[...] = nat[:, 2 * NAT_W:3 * NAT_W].astype(BF16)

    @pl.when(i < _CTX_TILES)
    def _():
        heads = (TM // SEQ, SEQ, H_NAT, HD_NAT)
        kn_ctx_ref[:, 0] = nat[:, NAT_W:2 * NAT_W].reshape(heads)
        vn_ctx_ref[:, 0] = nat[:, 2 * NAT_W:3 * NAT_W].reshape(heads)
        for later in range(1, kn_ctx_ref.shape[1]):
            kn_ctx_ref[:, later] = jnp.zeros(heads, F32)
            vn_ctx_ref[:, later] = jnp.zeros(heads, F32)


def _rope_table_block(i):
    lat_tiles = DEC_SEQ // TM
    return jnp.where(i < N_CTX // TM, lat_tiles, (i - N_CTX // TM) % lat_tiles)


def _project(x, mod_l, ng_l, w_in, w_a, b_a, cos_t, sin_t, new_cache, *, layer):
    tile = lambda w: pl.BlockSpec((TM, w), lambda i: (i, 0))
    if new_cache:
        ctx_heads = pl.BlockSpec((TM // SEQ, 1, SEQ, H_NAT, HD_NAT), lambda i: (_ctx_tile(i), layer, 0, 0, 0))
    else:
        ctx_heads = pl.BlockSpec((TM // SEQ, DEPTH, SEQ, H_NAT, HD_NAT), lambda i: (_ctx_tile(i), 0, 0, 0, 0))
    full = lambda a: pl.BlockSpec(a.shape, lambda i: (0,) * a.ndim)
    n_in = 9
    tail0 = _GLA_COLS + (_REST_TILES - 1) * LANES
    w_tail = jnp.pad(w_in[:, :, tail0:], ((0, 0), (0, 0), (0, LANES - (w_in.shape[2] - tail0))))
    return pl.pallas_call(
        functools.partial(_proj_kernel, layer=layer),
        grid=(N_ALL // TM,),
        in_specs=[
            tile(D_MODEL),
            pl.BlockSpec((1, N_MOD, D_MODEL), lambda i: (_group_of_tile(i), 0, 0)),
            full(ng_l),
            pl.BlockSpec(memory_space=pl.ANY),
            pl.BlockSpec((1, D_MODEL, LANES), lambda i: (layer, 0, 0)),
            full(w_a), full(b_a),
            pl.BlockSpec((TM, QK_W), lambda i: (_rope_table_block(i), 0)),
            pl.BlockSpec((TM, QK_W), lambda i: (_rope_table_block(i), 0)),
        ] + [pl.BlockSpec(memory_space=pl.ANY)] * len(new_cache),
        input_output_aliases={n_in + n: 6 + n for n in range(len(new_cache))},
        out_specs=[
            tile(QK_W), tile(QK_W), tile(V_W), tile(V_W),
            pl.BlockSpec((2, TM, QK_W), lambda i: (0, i, 0)),
            tile(NAT_W), ctx_heads, ctx_heads, tile(NAT_W), tile(NAT_W),
        ],
        out_shape=[
            jax.ShapeDtypeStruct((N_ALL, QK_W), F32), jax.ShapeDtypeStruct((N_ALL, QK_W), F32),
            jax.ShapeDtypeStruct((N_ALL, V_W), F32), jax.ShapeDtypeStruct((N_ALL, V_W), F32),
            jax.ShapeDtypeStruct((2, N_ALL, QK_W), F32),
            jax.ShapeDtypeStruct((N_ALL, NAT_W), BF16),
            jax.ShapeDtypeStruct((BATCH, DEPTH, SEQ, H_NAT, HD_NAT), F32),
            jax.ShapeDtypeStruct((BATCH, DEPTH, SEQ, H_NAT, HD_NAT), F32),
            jax.ShapeDtypeStruct((N_ALL, NAT_W), BF16), jax.ShapeDtypeStruct((N_ALL, NAT_W), BF16),
        ],
        scratch_shapes=[
            pltpu.VMEM((D_MODEL, _P_END), BF16),
            pltpu.VMEM((D_MODEL, _REST_TILES * LANES), F32),
            pltpu.SemaphoreType.DMA((2,)),
        ],
        compiler_params=pltpu.CompilerParams(
            dimension_semantics=("arbitrary",), vmem_limit_bytes=VMEM_LIMIT),
        name="mixer_proj",
    )(x, mod_l, ng_l, w_in, w_tail, w_a, b_a, cos_t, sin_t, *new_cache)


def _gla_tables():
    rows = []
    seq_specs = [(b * (SEQ // GLA_T), SEQ // GLA_T, 0, 0, b, 1) for b in range(BATCH)]
    seq_specs += [(N_CTX // GLA_T + b * (DEC_SEQ // GLA_T), DEC_SEQ // GLA_T, b, 1, BATCH - 1, 0)
                  for b in range(DEC_BATCH)]
    for blk0, nchunk, s0_row, has_s0, fin_row, wants_fin in seq_specs:
        states = (s0_row, has_s0, fin_row, wants_fin)
        if nchunk == 1:
            rows.append((blk0, _GLA_BOTH, 1, blk0, 0) + states)
            continue
        for direction in (0, 1):
            order = range(nchunk) if direction == 0 else range(nchunk - 1, -1, -1)
            for n, c in enumerate(order):
                out_blk = blk0 + (c if direction == 1 else nchunk - 1)
                rows.append((blk0 + c, direction, int(n == 0), out_blk, c) + states)
    return np.asarray(rows, dtype=np.int32).T.copy()


_GLA_BOTH = 2


_GLA_TAB = _gla_tables()
_GLA_ITEMS = _GLA_TAB.shape[1]
_GLA_PAIR_ROWS = GLA_NB * GLA_CB * (GLA_CB + 1) // 2


def _gla_item(direction, q_ref, k_ref, v_ref, g_ref, res, s_scr, cp, sstk, p_scr, w_scr):
    T, CB, NB = GLA_T, GLA_CB, GLA_NB
    fwd = direction == 0
    slab = lambda i: slice(i * NB, (i + 1) * NB)

    cum = None
    for i in (range(CB) if fwd else range(CB - 1, -1, -1)):
        gi = g_ref[slab(i), :]
        cum = gi if cum is None else cum + gi
        cp[slab(i), :] = cum
    total = cum
    cum_all = cp[...]
    qt = q_ref[...] * jnp.exp(cum_all)
    kh = k_ref[...] * jnp.exp(jnp.concatenate([total] * CB, axis=0) - cum_all)
    dec_t = jnp.concatenate([jnp.exp(total), jnp.zeros((LANES - NB, QK_W), F32)], axis=0).T

    erow = lax.broadcasted_iota(jnp.int32, (QK_W, QK_W), 0)
    ecol = lax.broadcasted_iota(jnp.int32, (QK_W, QK_W), 1)
    head_sum = jnp.where((erow >> 6) == (ecol >> 6), 1.0, 0.0).astype(BF16)

    key_positions = lambda i: range(i + 1) if fwd else range(i, CB)
    r0 = 0
    for i in range(CB):
        qi = q_ref[slab(i), :]
        ci = cp[slab(i), :]
        for j in key_positions(i):
            e = jnp.exp(ci - cp[slab(j), :])
            p_scr[r0:r0 + NB, :] = (qi * k_ref[slab(j), :] * e).astype(BF16)
            r0 += NB
    w_scr[...] = _dot(p_scr[...], head_sum)
    first_half = lax.broadcasted_iota(jnp.int32, (NB, LANES), 1) < DK_GLA
    r0 = 0
    for i in range(CB):
        acc = None
        for j in key_positions(i):
            spread = []
            for pair in range(H_GLA // 2):
                tile = w_scr[r0:r0 + NB, pair * LANES:(pair + 1) * LANES]
                other = pltpu.roll(tile, DK_GLA, axis=1)
                spread += [jnp.where(first_half, tile, other), jnp.where(first_half, other, tile)]
            term = jnp.concatenate(spread, axis=1) * v_ref[slab(j), :]
            acc = term if acc is None else acc + term
            r0 += NB
        res[slab(i), :] = acc

    kht = kh.T.astype(BF16)
    v_bf = v_ref[...].astype(BF16)
    key_blk = lax.broadcasted_iota(jnp.int32, (DK_GLA, T), 1) & (NB - 1)
    row_blk = lax.broadcasted_iota(jnp.int32, (T, LANES), 0) & (NB - 1)
    lane_half = lax.broadcasted_iota(jnp.int32, (T, LANES), 1) >> 6
    order = range(NB) if fwd else range(NB - 1, -1, -1)
    for h in range(H_GLA):
        kh_h = kht[h * DK_GLA:(h + 1) * DK_GLA, :]
        kv = _dot(jnp.concatenate([jnp.where(key_blk == b, kh_h, 0) for b in range(NB)], axis=0),
                  v_bf[:, h * DV_GLA:(h + 1) * DV_GLA])
        s = s_scr[h]
        for b in order:
            sstk[h, b * DK_GLA:(b + 1) * DK_GLA, :] = s.astype(BF16)
            s = dec_t[h * DK_GLA:(h + 1) * DK_GLA, b:b + 1] * s + kv[b * DK_GLA:(b + 1) * DK_GLA, :]
        s_scr[h] = s

        pair_tile = qt[:, (h // 2) * LANES:(h // 2 + 1) * LANES]
        both = jnp.where(lane_half == h % 2, pair_tile, pltpu.roll(pair_tile, DK_GLA, axis=1))
        lhs = jnp.concatenate([jnp.where(row_blk == 2 * j + lane_half, both, 0.0).astype(BF16)
                               for j in range(NB // 2)], axis=1)
        res[:, h * DV_GLA:(h + 1) * DV_GLA] += _dot(lhs, sstk[h])


def _gla_kernel(tab_ref, q_ref, k_ref, v_ref, g_ref, s0_ref, *rest):
    o_ref, so_ref, s_scr, cp, sstk, p_scr, w_scr, res, held = rest[-9:]
    it = pl.program_id(0)
    kind = tab_ref[1, it]
    slot = tab_ref[4, it]
    has_s0 = tab_ref[6, it] == 1
    wants_final = tab_ref[8, it] == 1

    def run(d):
        _gla_item(d, q_ref, k_ref, v_ref, g_ref.at[d], res.at[d], s_scr.at[d], cp.at[d], sstk.at[d],
                  p_scr.at[d], w_scr.at[d])

    def clear_later_layers():
        for later in range(1, so_ref.shape[1]):
            so_ref[0, later] = jnp.zeros(so_ref.shape[2:], F32)

    @pl.when(kind == _GLA_BOTH)
    def _():
        s_scr[...] = jnp.where(has_s0, s0_ref[0, 0], 0.0)
        run(0)
        run(1)
        o_ref[...] = res[0] + res[1]

        @pl.when(wants_final)
        def _():
            so_ref[0, 0] = s_scr[...]
            clear_later_layers()

    for d in (0, 1):
        @pl.when(kind == d)
        def _(d=d):
            @pl.when(tab_ref[2, it] == 1)
            def _():
                s_scr[d] = jnp.where(has_s0, s0_ref[0, 0, d], 0.0)

            run(d)
            if d == 0:
                held[slot] = res[0]
            else:
                o_ref[...] = held[slot] + res[1]

            @pl.when(wants_final)
            def _():
                so_ref[0, 0, d] = s_scr[d]
                if d == 0:
                    clear_later_layers()


def _gla(q, k, v, g, state_gla, new_state, *, layer):
    tok = lambda w: pl.BlockSpec((GLA_T, w), lambda it, tab: (tab[0, it], 0))
    state_blk = (2, H_GLA, DK_GLA, DV_GLA)
    if new_state:
        final_spec = pl.BlockSpec((1, 1) + state_blk, lambda it, tab: (tab[7, it], layer, 0, 0, 0, 0))
    else:
        final_spec = pl.BlockSpec((1, DEPTH) + state_blk, lambda it, tab: (tab[7, it], 0, 0, 0, 0, 0))
    grid_spec = pltpu.PrefetchScalarGridSpec(
        num_scalar_prefetch=1,
        grid=(_GLA_ITEMS,),
        in_specs=[
            tok(QK_W), tok(QK_W), tok(V_W),
            pl.BlockSpec((2, GLA_T, QK_W), lambda it, tab: (0, tab[0, it], 0)),
            pl.BlockSpec((1, 1) + state_blk, lambda it, tab: (tab[5, it], layer, 0, 0, 0, 0)),
        ] + [pl.BlockSpec(memory_space=pl.ANY)] * len(new_state),
        out_specs=[
            pl.BlockSpec((GLA_T, V_W), lambda it, tab: (tab[3, it], 0)),
            final_spec,
        ],
        scratch_shapes=[
            pltpu.VMEM((2, H_GLA, DK_GLA, DV_GLA), F32),
            pltpu.VMEM((2, GLA_T, QK_W), F32),
            pltpu.VMEM((2, H_GLA, GLA_NB * DK_GLA, DV_GLA), BF16),
            pltpu.VMEM((2, _GLA_PAIR_ROWS, QK_W), BF16),
            pltpu.VMEM((2, _GLA_PAIR_ROWS, QK_W), F32),
            pltpu.VMEM((2, GLA_T, V_W), F32),
            pltpu.VMEM((DEC_SEQ // GLA_T, GLA_T, V_W), F32),
        ],
    )
    return pl.pallas_call(
        _gla_kernel,
        grid_spec=grid_spec,
        out_shape=[
            jax.ShapeDtypeStruct((N_ALL, V_W), F32),
            jax.ShapeDtypeStruct((BATCH, DEPTH) + state_blk, F32),
        ],
        input_output_aliases={6 + n: 1 + n for n in range(len(new_state))},
        compiler_params=pltpu.CompilerParams(
            dimension_semantics=("arbitrary",), vmem_limit_bytes=VMEM_LIMIT),
        name="gla",
    )(jnp.asarray(_GLA_TAB), q, k, v, g, state_gla, *new_state)


def _softmax_pv(s_list, v_list):
    m = s_list[0].max(axis=-1, keepdims=True)
    for s in s_list[1:]:
        m = jnp.maximum(m, s.max(axis=-1, keepdims=True))
    num = None
    den = None
    for s, vv in zip(s_list, v_list):
        e = jnp.exp(s - m)
        den = e.sum(axis=-1, keepdims=True) if den is None else den + e.sum(axis=-1, keepdims=True)
        pv = _dot(e.astype(BF16), vv)
        num = pv if num is None else num + pv
    return num / den


def _ctx_attn_kernel(q_ref, k_ref, v_ref, o_ref):
    lane =lax.broadcasted_iota(jnp.int32, (SEQ, LANES), 1)
    for t in range(NAT_W // LANES):
        sl = slice(t * LANES, (t + 1) * LANES)
        qt = q_ref[:, sl]
        kt = k_ref[:, sl].astype(BF16)
        vt = v_ref[:, sl].astype(BF16)
        out = jnp.zeros((SEQ, LANES), F32)
        for half in range(LANES // HD_NAT):
            mine = (lane >> 6) == half
            s = _dot_nt(jnp.where(mine, qt, 0).astype(BF16), kt)
            out = jnp.where(mine, _softmax_pv([s], [vt]), out)
        o_ref[:, sl] = out.astype(o_ref.dtype)


def _ctx_attention(qn, kn, vn):
    spec = pl.BlockSpec((SEQ, NAT_W), lambda b: (b, 0))
    return pl.pallas_call(
        _ctx_attn_kernel,
        grid=(BATCH,),
        in_specs=[spec, spec, spec],
        out_specs=spec,
        out_shape=jax.ShapeDtypeStruct((N_CTX, NAT_W), BF16),
        compiler_params=pltpu.CompilerParams(
            dimension_semantics=("arbitrary",), vmem_limit_bytes=VMEM_LIMIT),
        name="ctx_attention",
    )(qn, kn, vn)


_NAT_QROWS = 4
_NAT_GROUPS = GRID_ROWS // _NAT_QROWS
_NAT_KROWS = WIN_H + _NAT_QROWS
_NAT_Q = _NAT_QROWS * GRID_W
_NAT_KEYS = _NAT_KROWS * GRID_W
_NAT_DR = 2 * WIN_H - 1
_NAT_DC = 2 * WIN_W - 1


def _nat_key_row0(r):
    return jnp.clip(r - WIN_H // 2, 0, GRID_ROWS - WIN_H)


def _nat_build_bias(rpb_ref, tz_scr):
    c = lax.broadcasted_iota(jnp.int32, (GRID_W, LANES), 0)
    lane = lax.broadcasted_iota(jnp.int32, (GRID_W, LANES), 1)
    kc = lane & (GRID_W - 1)
    second = (lane >> 6) == 1
    win_start = jnp.clip(c - WIN_W // 2, 0, GRID_W - WIN_W)
    valid = (kc >= win_start) & (kc < win_start + WIN_W)

    def one_row(n, carry):
        dr = n >> 3
        h = n & (H_NAT - 1)
        src = jnp.broadcast_to(rpb_ref[pl.ds(h * _NAT_DR + dr, 1), :], (GRID_W, LANES))
        rolled = pltpu.roll(src, LANES - (WIN_W - 1), axis=1, stride=1, stride_axis=0)
        tz_scr[dr, h] = jnp.where(valid, rolled, NEG_INF)
        return carry

    lax.fori_loop(0, _NAT_DR * H_NAT, one_row, 0)

    def pair_rows(n, carry):
        dr = n >> 3
        h = n & (H_NAT - 1)
        tz_scr[dr, h] = jnp.where(second, tz_scr[dr + 1, h], tz_scr[dr, h])
        return carry

    lax.fori_loop(0, (_NAT_DR - 1) * H_NAT, pair_rows, 0)


def _nat_kernel(rpb_ref, q_ref, k_ref, v_ref, ck_ref, cv_ref, o_ref, tz_scr, bias_scr):
    grp = pl.program_id(1)

    @pl.when((pl.program_id(0) == 0) & (grp == 0))
    def _():
        _nat_build_bias(rpb_ref, tz_scr)

    krow0 = jnp.clip(_NAT_QROWS * grp - WIN_H // 2, 0, GRID_ROWS - _NAT_KROWS)
    k0 = pl.multiple_of(krow0 * GRID_W, GRID_W)
    lane_q = lax.broadcasted_iota(jnp.int32, (_NAT_Q, LANES), 1)
    lane_b = lax.broadcasted_iota(jnp.int32, (GRID_W, LANES), 1)
    neg = jnp.full((GRID_W, LANES), NEG_INF, F32)
    for t in range(NAT_W // LANES):
        sl = slice(t * LANES, (t + 1) * LANES)
        qt = q_ref[:, sl]
        kw = k_ref[pl.ds(k0, _NAT_KEYS), sl].astype(BF16)
        vw = v_ref[pl.ds(k0, _NAT_KEYS), sl].astype(BF16)
        ck = ck_ref[0, 0, :, sl].astype(BF16)
        cv = cv_ref[0, 0, :, sl].astype(BF16)
        out = jnp.zeros((_NAT_Q, LANES), F32)
        for half in range(LANES // HD_NAT):
            h = 2 * t + half
            for qr in range(_NAT_QROWS):
                r = _NAT_QROWS * grp + qr
                lo = _nat_key_row0(r)
                for kp in range(_NAT_KROWS // 2):
                    kr = krow0 + 2 * kp
                    tile = tz_scr[jnp.clip(kr - r + (WIN_H - 1), 0, _NAT_DR - 1), h]
                    ok_a = ((kr >= lo) & (kr < lo + WIN_H)).astype(jnp.int32)
                    ok_b = ((kr + 1 >= lo) & (kr + 1 < lo + WIN_H)).astype(jnp.int32)
                    ok = jnp.where(lane_b < GRID_W, ok_a, ok_b) == 1
                    bias_scr[qr * GRID_W:(qr + 1) * GRID_W, kp * LANES:(kp + 1) * LANES] = jnp.where(ok, tile, neg)
            mine = (lane_q >> 6) == half
            qm = jnp.where(mine, qt, 0).astype(BF16)
            s_win = _dot_nt(qm, kw) + bias_scr[...]
            s_ctx = _dot_nt(qm, ck)
            out = jnp.where(mine, _softmax_pv([s_win, s_ctx], [vw, cv]), out)
        o_ref[:, sl] = out.astype(o_ref.dtype)


def _nat_attention(qn, kn, vn, cache_k, cache_v, rpb_l, layer):
    lat0 = N_CTX // _NAT_Q
    half = jnp.pad(rpb_l.reshape(H_NAT * _NAT_DR, _NAT_DC), ((0, 0), (0, GRID_W - _NAT_DC)))
    rpb_rows = jnp.concatenate([half, half], axis=1)
    return pl.pallas_call(
        _nat_kernel,
        grid=(DEC_BATCH, _NAT_GROUPS),
        in_specs=[
            pl.BlockSpec((H_NAT * _NAT_DR, LANES), lambda b, g: (0, 0)),
            pl.BlockSpec((_NAT_Q, NAT_W), lambda b, g: (lat0 + b * _NAT_GROUPS + g, 0)),
            pl.BlockSpec((DEC_SEQ, NAT_W), lambda b, g: (N_CTX // DEC_SEQ + b, 0)),
            pl.BlockSpec((DEC_SEQ, NAT_W), lambda b, g: (N_CTX // DEC_SEQ + b, 0)),
            pl.BlockSpec((1, 1, PAST_LEN, NAT_W), lambda b, g: (b, layer, 0, 0)),
            pl.BlockSpec((1, 1, PAST_LEN, NAT_W), lambda b, g: (b, layer, 0, 0)),
        ],
        out_specs=pl.BlockSpec((_NAT_Q, NAT_W), lambda b, g: (b * _NAT_GROUPS + g, 0)),
        out_shape=jax.ShapeDtypeStruct((N_LAT, NAT_W), BF16),
        scratch_shapes=[
            pltpu.VMEM((_NAT_DR, H_NAT, GRID_W, LANES), F32),
            pltpu.VMEM((_NAT_Q, _NAT_KEYS), F32),
        ],
        compiler_params=pltpu.CompilerParams(
            dimension_semantics=("arbitrary", "arbitrary"), vmem_limit_bytes=VMEM_LIMIT),
        name="nat_attention",
    )(rpb_rows, qn, kn, vn, cache_k, cache_v)


def _out_kernel(x_ref, o_ref, r_ref, on_ctx_ref, on_lat_ref, mod_ref, ng_ref, gng_ref, w32_ref, y_ref, w_ref):
    @pl.when(pl.program_id(0) == 0)
    def _():
        w_ref[...] = w32_ref[0].astype(BF16)

    o_nat = _read_split(pl.program_id(0), on_ctx_ref, on_lat_ref)
    og = o_ref[...]
    parts = []
    for h in range(H_GLA):
        parts.append(_rms(og[:, h * DV_GLA:(h + 1) * DV_GLA], gng_ref[...]))
    merged = (jnp.concatenate(parts, axis=1) * _silu(r_ref[...])).astype(BF16)
    merged = _permute_chunks(_block_transpose_perm(), merged)
    y = _dot(merged, w_ref[0:V_W, :]) + _dot(o_nat.astype(BF16), w_ref[V_W:, :])
    y_ref[...] = x_ref[...] + mod_ref[0, 5:6, :] * _rms(y, ng_ref[3:4, :])


def _mixer_out(x, o_gla, r, o_ctx, o_lat, mod_l, ng_l, gng_l, w_out, *, layer):
    tile = lambda w: pl.BlockSpec((TM, w), lambda i: (i, 0))
    full = lambda a: pl.BlockSpec(a.shape, lambda i: (0,) * a.ndim)
    return pl.pallas_call(
        _out_kernel,
        grid=(N_ALL // TM,),
        in_specs=[
            tile(D_MODEL),
            tile(V_W),
            tile(V_W),
            pl.BlockSpec((TM, NAT_W), lambda i: (_ctx_tile(i), 0)),
            pl.BlockSpec((TM, NAT_W), lambda i: (_lat_tile(i), 0)),
            pl.BlockSpec((1, N_MOD, D_MODEL), lambda i: (_group_of_tile(i), 0, 0)),
            full(ng_l), full(gng_l),
            pl.BlockSpec((1,) + w_out.shape[1:], lambda i: (layer, 0, 0)),
        ],
        out_specs=tile(D_MODEL),
        out_shape=jax.ShapeDtypeStruct((N_ALL, D_MODEL), F32),
        scratch_shapes=[pltpu.VMEM(w_out.shape[1:], BF16)],
        compiler_params=pltpu.CompilerParams(
            dimension_semantics=("arbitrary",), vmem_limit_bytes=VMEM_LIMIT),
        name="mixer_out",
    )(x, o_gla, r, o_ctx, o_lat, mod_l, ng_l, gng_l, w_out)


def _rope_tables():
    quarter = DK_GLA // 4
    freqs = ROPE_BASE ** (-jnp.arange(quarter, dtype=F32) / quarter)
    t = jnp.arange(DEC_SEQ)
    ang_r = (t // GRID_W).astype(F32)[:, None] * freqs[None, :]
    ang_c = (t % GRID_W).astype(F32)[:, None] * freqs[None, :]
    cos_h = jnp.concatenate([jnp.cos(ang_r), jnp.cos(ang_r), jnp.cos(ang_c), jnp.cos(ang_c)], axis=1)
    sin_h = jnp.concatenate([-jnp.sin(ang_r), jnp.sin(ang_r), -jnp.sin(ang_c), jnp.sin(ang_c)], axis=1)
    pos_major = lambda a: a.reshape(-1, GLA_NB, GLA_CB, QK_W).transpose(0, 2, 1, 3).reshape(a.shape)
    cos_t = jnp.concatenate([pos_major(jnp.tile(cos_h, (1, H_GLA))), jnp.ones((TM, QK_W), F32)], axis=0)
    sin_t = jnp.concatenate([pos_major(jnp.tile(sin_h, (1, H_GLA))), jnp.zeros((TM, QK_W), F32)], axis=0)
    return cos_t, sin_t


def _gate_up_weights(gla_wa2_l, gla_ba_l):
    w_a = jnp.zeros((LANES, 2 * QK_W), F32)
    w_a = w_a.at[0:GLA_RANK, 0:QK_W].set(gla_wa2_l[0]).at[GLA_RANK:2 * GLA_RANK, QK_W:].set(gla_wa2_l[1])
    b_a = jnp.concatenate([gla_ba_l[0], gla_ba_l[1]])[None, :]
    return w_a.astype(BF16), b_a


def kernel(x_prompt, x_sample, cache_k, cache_v, state_gla, c, c_ctx, w_mod, b_mod, norm_g, ffn_w_in, ffn_w_out,
           w_in, gla_wa2, gla_ba, gla_norm_g, nat_rpb, w_out):
    cvecs = jnp.zeros((SUBLANES, D_MODEL), F32).at[0].set(c_ctx).at[1:1 + DEC_BATCH].set(c)
    mod = _modulation(cvecs, w_mod, b_mod)[:, :N_GROUPS].reshape(DEPTH, N_GROUPS, N_MOD, D_MODEL)

    cos_t, sin_t = _rope_tables()
    ck = cache_k.reshape(DEC_BATCH, DEPTH, PAST_LEN, NAT_W)
    cv = cache_v.reshape(DEC_BATCH, DEPTH, PAST_LEN, NAT_W)

    x = (x_prompt.reshape(N_CTX, D_MODEL), x_sample.reshape(N_LAT, D_MODEL))
    new_cache, new_state = (), ()
    for l in range(DEPTH):
        mod_l, ng_l = mod[l], norm_g[l]
        x = _ffn(x, mod_l, ng_l, ffn_w_in, ffn_w_out, layer=l, which=0)

        w_a, b_a = _gate_up_weights(gla_wa2[l], gla_ba[l])
        q, k, v, r, g, qn, new_k, new_v, kn, vn = _project(
            x, mod_l, ng_l, w_in, w_a, b_a, cos_t, sin_t, new_cache, layer=l)
        new_cache = (new_k, new_v)

        o_gla, new_s = _gla(q, k, v, g, state_gla, new_state, layer=l)
        new_state = (new_s,)
        o_ctx = _ctx_attention(qn, kn, vn)
        o_lat = _nat_attention(qn, kn, vn, ck, cv, nat_rpb[l], l)

        x = _mixer_out(x, o_gla, r, o_ctx, o_lat, mod_l, ng_l, gla_norm_g[l][None, :], w_out, layer=l)
        x = _ffn(x, mod_l, ng_l, ffn_w_in, ffn_w_out, layer=l, which=1, split_out=(l == DEPTH - 1))

    y_prompt = x[0].reshape(BATCH, SEQ, D_MODEL)
    y_sample = x[1].reshape(DEC_BATCH, DEC_SEQ, D_MODEL)
    return (y_prompt, y_sample, new_cache[0], new_cache[1], new_state[0])
```

```python
import functools

import numpy as np
import jax
import jax.numpy as jnp
from jax import lax
from jax.experimental import pallas as pl
from jax.experimental.pallas import tpu as pltpu

D_MODEL = 1024
BATCH = 16
SEQ = 256
DEPTH = 2
DEC_BATCH = 2
DEC_SEQ = 1024
PAST_LEN = 512
GRID_W = 64
H_GLA = 4
DK_GLA = 64
DV_GLA = 128
GLA_RANK = 16
GATE_NORM = 16.0
H_NAT = 8
HD_NAT = 64
WIN_H = 8
WIN_W = 16
D_FF = 2816
N_MOD = 9
ROPE_BASE = 10000.0
EPS = 1e-6
NEG_INF = -1e30

F32 = jnp.float32
BF16 = jnp.bfloat16

N_CTX = BATCH * SEQ
N_LAT = DEC_BATCH * DEC_SEQ
N_ALL = N_CTX + N_LAT
N_GROUPS = 1 + DEC_BATCH
QK_W = H_GLA * DK_GLA
V_W = H_GLA * DV_GLA
NAT_W = H_NAT * HD_NAT
GRID_ROWS = DEC_SEQ // GRID_W

LANES = 128
SUBLANES = 8

TM = 512
FFN_TC = 256
FFN_NC = D_FF // FFN_TC
FFN_STAGES = 2
MOD_TN = 1152
GLA_T = 256
GLA_CB = 16
GLA_NB = GLA_T // GLA_CB
VMEM_LIMIT = 56 * 1024 * 1024

assert N_CTX % TM == 0 and DEC_SEQ % TM == 0 and D_FF % FFN_TC == 0 and FFN_TC % LANES == 0
assert SEQ == GLA_T and DEC_SEQ % GLA_T == 0


def _group_of_tile(i):
    return jnp.where(i < N_CTX // TM, 0, 1 + (i - N_CTX // TM) // (DEC_SEQ // TM))


def _dot(a, b):
    return jnp.dot(a, b, preferred_element_type=F32)


def _dot_nt(a, b):
    return lax.dot_general(a, b, (((1,), (1,)), ((), ())), preferred_element_type=F32)


def _dot_tn(a, b):
    return lax.dot_general(a, b, (((0,), (0,)), ((), ())), preferred_element_type=F32)


def _rms(x, g):
    ms = jnp.mean(x * x, axis=-1, keepdims=True)
    return x * lax.rsqrt(ms + EPS) * g


def _silu(x):
    return x * jax.nn.sigmoid(x)


def _mod_kernel(c_ref, w_ref, b_ref, o_ref):
    s = _silu(c_ref[...]).astype(BF16)
    o_ref[0] = _dot(s, w_ref[0].astype(BF16)) + b_ref[0]


def _modulation(cvecs, w_mod, b_mod):
    n_out = N_MOD * D_MODEL
    return pl.pallas_call(
        _mod_kernel,
        grid=(DEPTH, n_out // MOD_TN),
        in_specs=[
            pl.BlockSpec((SUBLANES, D_MODEL), lambda l, j: (0, 0)),
            pl.BlockSpec((1, D_MODEL, MOD_TN), lambda l, j: (l, 0, j)),
            pl.BlockSpec((1, 1, MOD_TN), lambda l, j: (l, 0, j)),
        ],
        out_specs=pl.BlockSpec((1, SUBLANES, MOD_TN), lambda l, j: (l, 0, j)),
        out_shape=jax.ShapeDtypeStruct((DEPTH, SUBLANES, n_out), F32),
        compiler_params=pltpu.CompilerParams(
            dimension_semantics=("arbitrary", "arbitrary"), vmem_limit_bytes=VMEM_LIMIT),
        name="modulation",
    )(cvecs, w_mod, b_mod.reshape(DEPTH, 1, n_out))


_CTX_TILES = N_CTX // TM
_LAT_TILES = N_LAT // TM


def _ctx_tile(i):
    return jnp.minimum(i, _CTX_TILES - 1)


def _lat_tile(i):
    return jnp.maximum(i - _CTX_TILES, 0)


def _read_split(i, ctx_ref, lat_ref):
    return jnp.where(i < _CTX_TILES, ctx_ref[...], lat_ref[...])


def _ffn_kernel(*refs, m0, n0, layer, which, split_in, split_out):
    n_x = 2 if split_in else 1
    x_refs, (mod_ref, ng_ref, win_hbm, wout_hbm) = refs[:n_x], refs[n_x:n_x + 4]
    n_o = 2 if split_out else 1
    o_refs = refs[n_x + 4:n_x + 4 + n_o]
    h_scr, acc_scr, x_scr, wg_bf, wu_bf, wo_bf, stage_g, stage_u, stage_o, sem = refs[n_x + 4 + n_o:]
    i = pl.program_id(0)

    def chunk_copies(c, slot):
        cols = pl.ds(c * FFN_TC, FFN_TC)
        up_cols = pl.ds(D_FF + c * FFN_TC, FFN_TC)
        return (
            pltpu.make_async_copy(win_hbm.at[layer, which, :, cols], stage_g.at[slot], sem.at[0, slot]),
            pltpu.make_async_copy(win_hbm.at[layer, which, :, up_cols], stage_u.at[slot], sem.at[1, slot]),
            pltpu.make_async_copy(wout_hbm.at[layer, which, cols, :], stage_o.at[slot], sem.at[2, slot]),
        )

    def accumulate(c):
        h = h_scr[...]
        gate = _dot(h, wg_bf[c])
        up = _dot(h, wu_bf[c])
        part = _dot((_silu(gate) * up).astype(BF16), wo_bf[c])
        if c == 0:
            acc_scr[...] = part
        else:
            acc_scr[...] += part

    x = _read_split(i, *x_refs) if split_in else x_refs[0][...]
    x_scr[...] = x
    gain = ng_ref[n0:n0 + 1, :] * (1.0 + mod_ref[0, m0 + 1:m0 + 2, :])
    h_scr[...] = (_rms(x, gain) + mod_ref[0, m0:m0 + 1, :]).astype(BF16)

    @pl.when(i == 0)
    def _():
        for c in range(min(FFN_STAGES, FFN_NC)):
            for cp in chunk_copies(c, c):
                cp.start()
        for c in range(FFN_NC):
            slot = c % FFN_STAGES
            for cp in chunk_copies(c, slot):
                cp.wait()
            wg_bf[c] = stage_g[slot].astype(BF16)
            wu_bf[c] = stage_u[slot].astype(BF16)
            wo_bf[c] = stage_o[slot].astype(BF16)
            if c + FFN_STAGES < FFN_NC:
                for cp in chunk_copies(c + FFN_STAGES, slot):
                    cp.start()
            accumulate(c)

    @pl.when(i > 0)
    def _():
        for c in range(FFN_NC):
            accumulate(c)

    gain = 0.5 * mod_ref[0, m0 + 2:m0 + 3, :] * ng_ref[n0 + 1:n0 + 2, :]
    out = x_scr[...] + _rms(acc_scr[...], gain)
    if split_out:
        @pl.when(i < _CTX_TILES)
        def _():
            o_refs[0][...] = out

        @pl.when(i >= _CTX_TILES)
        def _():
            o_refs[1][...] = out
    else:
        o_refs[0][...] = out


def _ffn(x, mod_l, ng_l, w_in, w_out, *, layer, which, split_out=False):
    m0, n0 = (0, 0) if which == 0 else (6, 4)
    split_in = isinstance(x, tuple)
    tile = pl.BlockSpec((TM, D_MODEL), lambda i: (i, 0))
    ctx_tile = pl.BlockSpec((TM, D_MODEL), lambda i: (_ctx_tile(i), 0))
    lat_tile = pl.BlockSpec((TM, D_MODEL), lambda i: (_lat_tile(i), 0))
    if split_out:
        out_specs = [ctx_tile, lat_tile]
        out_shape = [jax.ShapeDtypeStruct((N_CTX, D_MODEL), F32), jax.ShapeDtypeStruct((N_LAT, D_MODEL), F32)]
    else:
        out_specs = tile
        out_shape = jax.ShapeDtypeStruct((N_ALL, D_MODEL), F32)
    return pl.pallas_call(
        functools.partial(_ffn_kernel, m0=m0, n0=n0, layer=layer, which=which, split_in=split_in,
                          split_out=split_out),
        grid=(N_ALL // TM,),
        in_specs=([ctx_tile, lat_tile] if split_in else [tile]) + [
            pl.BlockSpec((1, N_MOD, D_MODEL), lambda i: (_group_of_tile(i), 0, 0)),
            pl.BlockSpec((6, D_MODEL), lambda i: (0, 0)),
            pl.BlockSpec(memory_space=pl.ANY),
            pl.BlockSpec(memory_space=pl.ANY),
        ],
        out_specs=out_specs,
        out_shape=out_shape,
        scratch_shapes=[
            pltpu.VMEM((TM, D_MODEL), BF16),
            pltpu.VMEM((TM, D_MODEL), F32),
            pltpu.VMEM((TM, D_MODEL), F32),
            pltpu.VMEM((FFN_NC, D_MODEL, FFN_TC), BF16),
            pltpu.VMEM((FFN_NC, D_MODEL, FFN_TC), BF16),
            pltpu.VMEM((FFN_NC, FFN_TC, D_MODEL), BF16),
            pltpu.VMEM((FFN_STAGES, D_MODEL, FFN_TC), F32),
            pltpu.VMEM((FFN_STAGES, D_MODEL, FFN_TC), F32),
            pltpu.VMEM((FFN_STAGES, FFN_TC, D_MODEL), F32),
            pltpu.SemaphoreType.DMA((3, FFN_STAGES)),
        ],
        compiler_params=pltpu.CompilerParams(
            dimension_semantics=("arbitrary",), vmem_limit_bytes=VMEM_LIMIT),
        name="ffn",
    )(*(x if split_in else (x,)), mod_l, ng_l, w_in, w_out)


def _block_transpose_perm():
    r = lax.broadcasted_iota(jnp.int32, (GLA_T, GLA_T), 0)
    c = lax.broadcasted_iota(jnp.int32, (GLA_T, GLA_T), 1)
    return jnp.where(c == (r & (GLA_CB - 1)) * GLA_NB + (r >> 4), 1.0, 0.0).astype(BF16)


def _permute_chunks(perm, a):
    return jnp.concatenate([_dot(perm, a[c * GLA_T:(c + 1) * GLA_T, :]).astype(BF16)
                            for c in range(a.shape[0] // GLA_T)], axis=0)


_P_QK = 0
_P_VR = 4 * QK_W
_P_NAT = _P_VR + 2 * V_W
_P_LR = _P_NAT + 3 * NAT_W
_P_END = _P_LR + LANES


_NAT_SCALE = HD_NAT ** -0.5
assert _NAT_SCALE == 2.0 ** round(np.log2(_NAT_SCALE))
_GLA_COLS = 2 * QK_W + 2 * V_W
_REST_COLS = 2 * GLA_RANK + 3 * NAT_W
_REST_TILES = -(-_REST_COLS // LANES)
_PREP_ROWS = 128
assert _GLA_COLS % LANES == 0 and _GLA_COLS + _REST_COLS == 3104 and D_MODEL % _PREP_ROWS == 0


def _proj_prepare_weights(win_hbm, tail_ref, layer, w_bf, stage, sem):
    lane = lax.broadcasted_iota(jnp.int32, (_PREP_ROWS, LANES), 1)
    first_of_pair = ((lane >> 4) & 1) == 0
    shift = 2 * GLA_RANK

    head = pltpu.make_async_copy(win_hbm.at[layer, :, pl.ds(0, _GLA_COLS)], stage.at[:, pl.ds(0, _GLA_COLS)],
                                 sem.at[0])
    head.start()
    head.wait()
    for r0 in range(0, D_MODEL, _PREP_ROWS):
        rows = slice(r0, r0 + _PREP_ROWS)
        w_bf[rows, _P_QK:_P_QK + 2 * QK_W] = stage[rows, 0:2 * QK_W].astype(BF16)
        for t in range(2 * QK_W // LANES):
            tile = stage[rows, t * LANES:(t + 1) * LANES]
            partner = jnp.where(first_of_pair, pltpu.roll(tile, LANES - 16, axis=1), pltpu.roll(tile, 16, axis=1))
            w_bf[rows, _P_QK + 2 * QK_W + t * LANES:_P_QK + 2 * QK_W + (t + 1) * LANES] = partner.astype(BF16)
        w_bf[rows, _P_VR:_P_NAT] = stage[rows, 2 * QK_W:_GLA_COLS].astype(BF16)

    whole = (_REST_TILES - 1) * LANES
    stage[:, whole:] = tail_ref[0]
    rest = pltpu.make_async_copy(win_hbm.at[layer, :, pl.ds(_GLA_COLS, whole)], stage.at[:, pl.ds(0, whole)],
                                 sem.at[1])
    rest.start()
    rest.wait()
    for r0 in range(0, D_MODEL, _PREP_ROWS):
        rows = slice(r0, r0 + _PREP_ROWS)
        w_bf[rows, _P_LR:_P_END] = jnp.where(lane < shift, stage[rows, 0:LANES], 0.0).astype(BF16)
        for t in range(3 * NAT_W // LANES):
            lo = pltpu.roll(stage[rows, t * LANES:(t + 1) * LANES], LANES - shift, axis=1)
            hi = pltpu.roll(stage[rows, (t + 1) * LANES:(t + 2) * LANES], LANES - shift, axis=1)
            w_bf[rows, _P_NAT + t * LANES:_P_NAT + (t + 1) * LANES] = jnp.where(lane < LANES - shift, lo,
                                                                                  hi).astype(BF16)


def _proj_kernel(x_ref, mod_ref, ng_ref, win_hbm, tail_ref, wa_ref, ba_ref, cos_ref, sin_ref, *rest, layer):
    q_ref, k_ref, v_ref, r_ref, g_ref, qn_ref, kn_ctx_ref, vn_ctx_ref, kn_ref, vn_ref = rest[-13:-3]
    w_ref, stage, sem = rest[-3:]
    i = pl.program_id(0)

    @pl.when(i == 0)
    def _():
        _proj_prepare_weights(win_hbm, tail_ref, layer, w_ref, stage, sem)

    h = _rms(x_ref[...], ng_ref[2:3, :])
    h = (h * (1.0 + mod_ref[0, 4:5, :]) + mod_ref[0, 3:4, :]).astype(BF16)
    hp = _permute_chunks(_block_transpose_perm(), h)

    lr = _dot(hp, w_ref[:, _P_LR:_P_END]).astype(BF16)
    z = _dot(lr, wa_ref[...]) + ba_ref[...]
    g = (jnp.minimum(z, 0.0) - jnp.log1p(jnp.exp(-jnp.abs(z)))) * (1.0 / GATE_NORM)
    g_ref[0] = g[:, 0:QK_W]
    g_ref[1] = g[:, QK_W:2 * QK_W]

    qk = _dot(hp, w_ref[:, _P_QK:_P_VR])
    cos = cos_ref[...]
    sin = sin_ref[...]
    q_ref[...] = (qk[:, 0:QK_W] * cos + qk[:, 2 * QK_W:3 * QK_W] * sin) * (DK_GLA ** -0.5)
    k_ref[...] = qk[:, QK_W:2 * QK_W] * cos + qk[:, 3 * QK_W:4 * QK_W] * sin

    vr = _dot(hp, w_ref[:, _P_VR:_P_NAT])
    v_ref[...] = vr[:, 0:V_W]
    r_ref[...] = vr[:, V_W:2 * V_W]

    nat = _dot(h, w_ref[:, _P_NAT:_P_LR])
    qn_ref[...] = (nat[:, 0:NAT_W] * _NAT_SCALE).astype(BF16)
    kn_ref[...] = nat[:, NAT_W:2 * NAT_W].astype(BF16)
    vn_ref[...] = nat[:, 2 * NAT_W:3 * NAT_W].astype(BF16)

    @pl.when(i < _CTX_TILES)
    def _():
        heads = (TM // SEQ, SEQ, H_NAT, HD_NAT)
        kn_ctx_ref[:, 0] = nat[:, NAT_W:2 * NAT_W].reshape(heads)
        vn_ctx_ref[:, 0] = nat[:, 2 * NAT_W:3 * NAT_W].reshape(heads)
        for later in range(1, kn_ctx_ref.shape[1]):
            kn_ctx_ref[:, later] = jnp.zeros(heads, F32)
            vn_ctx_ref[:, later] = jnp.zeros(heads, F32)


def _rope_table_block(i):
    lat_tiles = DEC_SEQ // TM
    return jnp.where(i < N_CTX // TM, lat_tiles, (i - N_CTX // TM) % lat_tiles)


def _project(x, mod_l, ng_l, w_in, w_a, b_a, cos_t, sin_t, new_cache, *, layer):
    tile = lambda w: pl.BlockSpec((TM, w), lambda i: (i, 0))
    if new_cache:
        ctx_heads = pl.BlockSpec((TM // SEQ, 1, SEQ, H_NAT, HD_NAT), lambda i: (_ctx_tile(i), layer, 0, 0, 0))
    else:
        ctx_heads = pl.BlockSpec((TM // SEQ, DEPTH, SEQ, H_NAT, HD_NAT), lambda i: (_ctx_tile(i), 0, 0, 0, 0))
    full = lambda a: pl.BlockSpec(a.shape, lambda i: (0,) * a.ndim)
    n_in = 9
    tail0 = _GLA_COLS + (_REST_TILES - 1) * LANES
    w_tail = jnp.pad(w_in[:, :, tail0:], ((0, 0), (0, 0), (0, LANES - (w_in.shape[2] - tail0))))
    return pl.pallas_call(
        functools.partial(_proj_kernel, layer=layer),
        grid=(N_ALL // TM,),
        in_specs=[
            tile(D_MODEL),
            pl.BlockSpec((1, N_MOD, D_MODEL), lambda i: (_group_of_tile(i), 0, 0)),
            full(ng_l),
            pl.BlockSpec(memory_space=pl.ANY),
            pl.BlockSpec((1, D_MODEL, LANES), lambda i: (layer, 0, 0)),
            full(w_a), full(b_a),
            pl.BlockSpec((TM, QK_W), lambda i: (_rope_table_block(i), 0)),
            pl.BlockSpec((TM, QK_W), lambda i: (_rope_table_block(i), 0)),
        ] + [pl.BlockSpec(memory_space=pl.ANY)] * len(new_cache),
        input_output_aliases={n_in + n: 6 + n for n in range(len(new_cache))},
        out_specs=[
            tile(QK_W), tile(QK_W), tile(V_W), tile(V_W),
            pl.BlockSpec((2, TM, QK_W), lambda i: (0, i, 0)),
            tile(NAT_W), ctx_heads, ctx_heads, tile(NAT_W), tile(NAT_W),
        ],
        out_shape=[
            jax.ShapeDtypeStruct((N_ALL, QK_W), F32), jax.ShapeDtypeStruct((N_ALL, QK_W), F32),
            jax.ShapeDtypeStruct((N_ALL, V_W), F32), jax.ShapeDtypeStruct((N_ALL, V_W), F32),
            jax.ShapeDtypeStruct((2, N_ALL, QK_W), F32),
            jax.ShapeDtypeStruct((N_ALL, NAT_W), BF16),
            jax.ShapeDtypeStruct((BATCH, DEPTH, SEQ, H_NAT, HD_NAT), F32),
            jax.ShapeDtypeStruct((BATCH, DEPTH, SEQ, H_NAT, HD_NAT), F32),
            jax.ShapeDtypeStruct((N_ALL, NAT_W), BF16), jax.ShapeDtypeStruct((N_ALL, NAT_W), BF16),
        ],
        scratch_shapes=[
            pltpu.VMEM((D_MODEL, _P_END), BF16),
            pltpu.VMEM((D_MODEL, _REST_TILES * LANES), F32),
            pltpu.SemaphoreType.DMA((2,)),
        ],
        compiler_params=pltpu.CompilerParams(
            dimension_semantics=("arbitrary",), vmem_limit_bytes=VMEM_LIMIT),
        name="mixer_proj",
    )(x, mod_l, ng_l, w_in, w_tail, w_a, b_a, cos_t, sin_t, *new_cache)


def _gla_tables():
    rows = []
    seq_specs = [(b * (SEQ // GLA_T), SEQ // GLA_T, 0, 0, b, 1) for b in range(BATCH)]
    seq_specs += [(N_CTX // GLA_T + b * (DEC_SEQ // GLA_T), DEC_SEQ // GLA_T, b, 1, BATCH - 1, 0)
                  for b in range(DEC_BATCH)]
    for blk0, nchunk, s0_row, has_s0, fin_row, wants_fin in seq_specs:
        states = (s0_row, has_s0, fin_row, wants_fin)
        if nchunk == 1:
            rows.append((blk0, _GLA_BOTH, 1, blk0, 0) + states)
            continue
        for direction in (0, 1):
            order = range(nchunk) if direction == 0 else range(nchunk - 1, -1, -1)
            for n, c in enumerate(order):
                out_blk = blk0 + (c if direction == 1 else nchunk - 1)
                rows.append((blk0 + c, direction, int(n == 0), out_blk, c) + states)
    return np.asarray(rows, dtype=np.int32).T.copy()


_GLA_BOTH = 2


_GLA_TAB = _gla_tables()
_GLA_ITEMS = _GLA_TAB.shape[1]
_GLA_PAIR_ROWS = GLA_NB * GLA_CB * (GLA_CB + 1) // 2


def _gla_item(direction, q_ref, k_ref, v_ref, g_ref, res, s_scr, cp, sstk, p_scr, w_scr):
    T, CB, NB = GLA_T, GLA_CB, GLA_NB
    fwd = direction == 0
    slab = lambda i: slice(i * NB, (i + 1) * NB)

    cum = None
    for i in (range(CB) if fwd else range(CB - 1, -1, -1)):
        gi = g_ref[slab(i), :]
        cum = gi if cum is None else cum + gi
        cp[slab(i), :] = cum
    total = cum
    cum_all = cp[...]
    qt = q_ref[...] * jnp.exp(cum_all)
    kh = k_ref[...] * jnp.exp(jnp.concatenate([total] * CB, axis=0) - cum_all)
    dec_t = jnp.concatenate([jnp.exp(total), jnp.zeros((LANES - NB, QK_W), F32)], axis=0).T

    erow = lax.broadcasted_iota(jnp.int32, (QK_W, QK_W), 0)
    ecol = lax.broadcasted_iota(jnp.int32, (QK_W, QK_W), 1)
    head_sum = jnp.where((erow >> 6) == (ecol >> 6), 1.0, 0.0).astype(BF16)

    key_positions = lambda i: range(i + 1) if fwd else range(i, CB)
    r0 = 0
    for i in range(CB):
        qi = q_ref[slab(i), :]
        ci = cp[slab(i), :]
        for j in key_positions(i):
            e = jnp.exp(ci - cp[slab(j), :])
            p_scr[r0:r0 + NB, :] = (qi * k_ref[slab(j), :] * e).astype(BF16)
            r0 += NB
    w_scr[...] = _dot(p_scr[...], head_sum)
    first_half = lax.broadcasted_iota(jnp.int32, (NB, LANES), 1) < DK_GLA
    r0 = 0
    for i in range(CB):
        acc = None
        for j in key_positions(i):
            spread = []
            for pair in range(H_GLA // 2):
                tile = w_scr[r0:r0 + NB, pair * LANES:(pair + 1) * LANES]
                other = pltpu.roll(tile, DK_GLA, axis=1)
                spread += [jnp.where(first_half, tile, other), jnp.where(first_half, other, tile)]
            term = jnp.concatenate(spread, axis=1) * v_ref[slab(j), :]
            acc = term if acc is None else acc + term
            r0 += NB
        res[slab(i), :] = acc

    kht = kh.T.astype(BF16)
    v_bf = v_ref[...].astype(BF16)
    key_blk = lax.broadcasted_iota(jnp.int32, (DK_GLA, T), 1) & (NB - 1)
    row_blk = lax.broadcasted_iota(jnp.int32, (T, LANES), 0) & (NB - 1)
    lane_half = lax.broadcasted_iota(jnp.int32, (T, LANES), 1) >> 6
    order = range(NB) if fwd else range(NB - 1, -1, -1)
    for h in range(H_GLA):
        kh_h = kht[h * DK_GLA:(h + 1) * DK_GLA, :]
        kv = _dot(jnp.concatenate([jnp.where(key_blk == b, kh_h, 0) for b in range(NB)], axis=0),
                  v_bf[:, h * DV_GLA:(h + 1) * DV_GLA])
        s = s_scr[h]
        for b in order:
            sstk[h, b * DK_GLA:(b + 1) * DK_GLA, :] = s.astype(BF16)
            s = dec_t[h * DK_GLA:(h + 1) * DK_GLA, b:b + 1] * s + kv[b * DK_GLA:(b + 1) * DK_GLA, :]
        s_scr[h] = s

        pair_tile = qt[:, (h // 2) * LANES:(h // 2 + 1) * LANES]
        both = jnp.where(lane_half == h % 2, pair_tile, pltpu.roll(pair_tile, DK_GLA, axis=1))
        lhs = jnp.concatenate([jnp.where(row_blk == 2 * j + lane_half, both, 0.0).astype(BF16)
                               for j in range(NB // 2)], axis=1)
        res[:, h * DV_GLA:(h + 1) * DV_GLA] += _dot(lhs, sstk[h])


def _gla_kernel(tab_ref, q_ref, k_ref, v_ref, g_ref, s0_ref, *rest):
    o_ref, so_ref, s_scr, cp, sstk, p_scr, w_scr, res, held = rest[-9:]
    it = pl.program_id(0)
    kind = tab_ref[1, it]
    slot = tab_ref[4, it]
    has_s0 = tab_ref[6, it] == 1
    wants_final = tab_ref[8, it] == 1

    def run(d):
        _gla_item(d, q_ref, k_ref, v_ref, g_ref.at[d], res.at[d], s_scr.at[d], cp.at[d], sstk.at[d],
                  p_scr.at[d], w_scr.at[d])

    def clear_later_layers():
        for later in range(1, so_ref.shape[1]):
            so_ref[0, later] = jnp.zeros(so_ref.shape[2:], F32)

    @pl.when(kind == _GLA_BOTH)
    def _():
        s_scr[...] = jnp.where(has_s0, s0_ref[0, 0], 0.0)
        run(0)
        run(1)
        o_ref[...] = res[0] + res[1]

        @pl.when(wants_final)
        def _():
            so_ref[0, 0] = s_scr[...]
            clear_later_layers()

    for d in (0, 1):
        @pl.when(kind == d)
        def _(d=d):
            @pl.when(tab_ref[2, it] == 1)
            def _():
                s_scr[d] = jnp.where(has_s0, s0_ref[0, 0, d], 0.0)

            run(d)
            if d == 0:
                held[slot] = res[0]
            else:
                o_ref[...] = held[slot] + res[1]

            @pl.when(wants_final)
            def _():
                so_ref[0, 0, d] = s_scr[d]
                if d == 0:
                    clear_later_layers()


def _gla(q, k, v, g, state_gla, new_state, *, layer):
    tok = lambda w: pl.BlockSpec((GLA_T, w), lambda it, tab: (tab[0, it], 0))
    state_blk = (2, H_GLA, DK_GLA, DV_GLA)
    if new_state:
        final_spec = pl.BlockSpec((1, 1) + state_blk, lambda it, tab: (tab[7, it], layer, 0, 0, 0, 0))
    else:
        final_spec = pl.BlockSpec((1, DEPTH) + state_blk, lambda it, tab: (tab[7, it], 0, 0, 0, 0, 0))
    grid_spec = pltpu.PrefetchScalarGridSpec(
        num_scalar_prefetch=1,
        grid=(_GLA_ITEMS,),
        in_specs=[
            tok(QK_W), tok(QK_W), tok(V_W),
            pl.BlockSpec((2, GLA_T, QK_W), lambda it, tab: (0, tab[0, it], 0)),
            pl.BlockSpec((1, 1) + state_blk, lambda it, tab: (tab[5, it], layer, 0, 0, 0, 0)),
        ] + [pl.BlockSpec(memory_space=pl.ANY)] * len(new_state),
        out_specs=[
            pl.BlockSpec((GLA_T, V_W), lambda it, tab: (tab[3, it], 0)),
            final_spec,
        ],
        scratch_shapes=[
            pltpu.VMEM((2, H_GLA, DK_GLA, DV_GLA), F32),
            pltpu.VMEM((2, GLA_T, QK_W), F32),
            pltpu.VMEM((2, H_GLA, GLA_NB * DK_GLA, DV_GLA), BF16),
            pltpu.VMEM((2, _GLA_PAIR_ROWS, QK_W), BF16),
            pltpu.VMEM((2, _GLA_PAIR_ROWS, QK_W), F32),
            pltpu.VMEM((2, GLA_T, V_W), F32),
            pltpu.VMEM((DEC_SEQ // GLA_T, GLA_T, V_W), F32),
        ],
    )
    return pl.pallas_call(
        _gla_kernel,
        grid_spec=grid_spec,
        out_shape=[
            jax.ShapeDtypeStruct((N_ALL, V_W), F32),
            jax.ShapeDtypeStruct((BATCH, DEPTH) + state_blk, F32),
        ],
        input_output_aliases={6 + n: 1 + n for n in range(len(new_state))},
        compiler_params=pltpu.CompilerParams(
            dimension_semantics=("arbitrary",), vmem_limit_bytes=VMEM_LIMIT),
        name="gla",
    )(jnp.asarray(_GLA_TAB), q, k, v, g, state_gla, *new_state)


def _with_ones(v):
    return jnp.concatenate([v, jnp.ones_like(v)], axis=1)


def _softmax_pv(s_list, v1_list):
    m = s_list[0].max(axis=-1, keepdims=True)
    for s in s_list[1:]:
        m = jnp.maximum(m, s.max(axis=-1, keepdims=True))
    acc = None
    for s, v1 in zip(s_list, v1_list):
        pv = _dot(jnp.exp(s - m).astype(BF16), v1)
        acc = pv if acc is None else acc + pv
    width = acc.shape[1] // 2
    return acc[:, :width] / acc[:, width:]


def _ctx_attn_kernel(q_ref, k_ref, v_ref, o_ref):
    lane =lax.broadcasted_iota(jnp.int32, (SEQ, LANES), 1)
    for t in range(NAT_W // LANES):
        sl = slice(t * LANES, (t + 1) * LANES)
        qt = q_ref[:, sl]
        kt = k_ref[:, sl].astype(BF16)
        vt = _with_ones(v_ref[:, sl].astype(BF16))
        out = jnp.zeros((SEQ, LANES), F32)
        for half in range(LANES // HD_NAT):
            mine = (lane >> 6) == half
            s = _dot_nt(jnp.where(mine, qt, 0).astype(BF16), kt)
            out = jnp.where(mine, _softmax_pv([s], [vt]), out)
        o_ref[:, sl] = out.astype(o_ref.dtype)


def _ctx_attention(qn, kn, vn):
    spec = pl.BlockSpec((SEQ, NAT_W), lambda b: (b, 0))
    return pl.pallas_call(
        _ctx_attn_kernel,
        grid=(BATCH,),
        in_specs=[spec, spec, spec],
        out_specs=spec,
        out_shape=jax.ShapeDtypeStruct((N_CTX, NAT_W), BF16),
        compiler_params=pltpu.CompilerParams(
            dimension_semantics=("arbitrary",), vmem_limit_bytes=VMEM_LIMIT),
        name="ctx_attention",
    )(qn, kn, vn)


_NAT_QROWS = 4
_NAT_GROUPS = GRID_ROWS // _NAT_QROWS
_NAT_KROWS = WIN_H + _NAT_QROWS
_NAT_Q = _NAT_QROWS * GRID_W
_NAT_KEYS = _NAT_KROWS * GRID_W
_NAT_DR = 2 * WIN_H - 1
_NAT_DC = 2 * WIN_W - 1


def _nat_key_row0(r):
    return jnp.clip(r - WIN_H // 2, 0, GRID_ROWS - WIN_H)


def _nat_build_bias(rpb_ref, tz_scr):
    c = lax.broadcasted_iota(jnp.int32, (GRID_W, LANES), 0)
    lane = lax.broadcasted_iota(jnp.int32, (GRID_W, LANES), 1)
    kc = lane & (GRID_W - 1)
    second = (lane >> 6) == 1
    win_start = jnp.clip(c - WIN_W // 2, 0, GRID_W - WIN_W)
    valid = (kc >= win_start) & (kc < win_start + WIN_W)

    def one_row(n, carry):
        dr = n >> 3
        h = n & (H_NAT - 1)
        src = jnp.broadcast_to(rpb_ref[pl.ds(h * _NAT_DR + dr, 1), :], (GRID_W, LANES))
        rolled = pltpu.roll(src, LANES - (WIN_W - 1), axis=1, stride=1, stride_axis=0)
        tz_scr[dr, h] = jnp.where(valid, rolled, NEG_INF)
        return carry

    lax.fori_loop(0, _NAT_DR * H_NAT, one_row, 0)

    def pair_rows(n, carry):
        dr = n >> 3
        h = n & (H_NAT - 1)
        tz_scr[dr, h] = jnp.where(second, tz_scr[dr + 1, h], tz_scr[dr, h])
        return carry

    lax.fori_loop(0, (_NAT_DR - 1) * H_NAT, pair_rows, 0)


def _nat_kernel(rpb_ref, q_ref, k_ref, v_ref, ck_ref, cv_ref, o_ref, tz_scr, bias_scr):
    grp = pl.program_id(1)

    @pl.when((pl.program_id(0) == 0) & (grp == 0))
    def _():
        _nat_build_bias(rpb_ref, tz_scr)

    krow0 = jnp.clip(_NAT_QROWS * grp - WIN_H // 2, 0, GRID_ROWS - _NAT_KROWS)
    k0 = pl.multiple_of(krow0 * GRID_W, GRID_W)
    lane_q = lax.broadcasted_iota(jnp.int32, (_NAT_Q, LANES), 1)
    lane_b = lax.broadcasted_iota(jnp.int32, (GRID_W, LANES), 1)
    neg = jnp.full((GRID_W, LANES), NEG_INF, F32)
    for t in range(NAT_W // LANES):
        sl = slice(t * LANES, (t + 1) * LANES)
        qt = q_ref[:, sl]
        kw = k_ref[pl.ds(k0, _NAT_KEYS), sl].astype(BF16)
        vw = _with_ones(v_ref[pl.ds(k0, _NAT_KEYS), sl].astype(BF16))
        ck = ck_ref[0, 0, :, sl].astype(BF16)
        cv = _with_ones(cv_ref[0, 0, :, sl].astype(BF16))
        out = jnp.zeros((_NAT_Q, LANES), F32)
        for half in range(LANES // HD_NAT):
            h = 2 * t + half
            for qr in range(_NAT_QROWS):
                r = _NAT_QROWS * grp + qr
                lo = _nat_key_row0(r)
                for kp in range(_NAT_KROWS // 2):
                    kr = krow0 + 2 * kp
                    tile = tz_scr[jnp.clip(kr - r + (WIN_H - 1), 0, _NAT_DR - 1), h]
                    ok_a = ((kr >= lo) & (kr < lo + WIN_H)).astype(jnp.int32)
                    ok_b = ((kr + 1 >= lo) & (kr + 1 < lo + WIN_H)).astype(jnp.int32)
                    ok = jnp.where(lane_b < GRID_W, ok_a, ok_b) == 1
                    bias_scr[qr * GRID_W:(qr + 1) * GRID_W, kp * LANES:(kp + 1) * LANES] = jnp.where(ok, tile, neg)
            mine = (lane_q >> 6) == half
            qm = jnp.where(mine, qt, 0).astype(BF16)
            s_win = _dot_nt(qm, kw) + bias_scr[...]
            s_ctx = _dot_nt(qm, ck)
            out = jnp.where(mine, _softmax_pv([s_win, s_ctx], [vw, cv]), out)
        o_ref[:, sl] = out.astype(o_ref.dtype)


def _nat_attention(qn, kn, vn, cache_k, cache_v, rpb_l, layer):
    lat0 = N_CTX // _NAT_Q
    half = jnp.pad(rpb_l.reshape(H_NAT * _NAT_DR, _NAT_DC), ((0, 0), (0, GRID_W - _NAT_DC)))
    rpb_rows = jnp.concatenate([half, half], axis=1)
    return pl.pallas_call(
        _nat_kernel,
        grid=(DEC_BATCH, _NAT_GROUPS),
        in_specs=[
            pl.BlockSpec((H_NAT * _NAT_DR, LANES), lambda b, g: (0, 0)),
            pl.BlockSpec((_NAT_Q, NAT_W), lambda b, g: (lat0 + b * _NAT_GROUPS + g, 0)),
            pl.BlockSpec((DEC_SEQ, NAT_W), lambda b, g: (N_CTX // DEC_SEQ + b, 0)),
            pl.BlockSpec((DEC_SEQ, NAT_W), lambda b, g: (N_CTX // DEC_SEQ + b, 0)),
            pl.BlockSpec((1, 1, PAST_LEN, NAT_W), lambda b, g: (b, layer, 0, 0)),
            pl.BlockSpec((1, 1, PAST_LEN, NAT_W), lambda b, g: (b, layer, 0, 0)),
        ],
        out_specs=pl.BlockSpec((_NAT_Q, NAT_W), lambda b, g: (b * _NAT_GROUPS + g, 0)),
        out_shape=jax.ShapeDtypeStruct((N_LAT, NAT_W), BF16),
        scratch_shapes=[
            pltpu.VMEM((_NAT_DR, H_NAT, GRID_W, LANES), F32),
            pltpu.VMEM((_NAT_Q, _NAT_KEYS), F32),
        ],
        compiler_params=pltpu.CompilerParams(
            dimension_semantics=("arbitrary", "arbitrary"), vmem_limit_bytes=VMEM_LIMIT),
        name="nat_attention",
    )(rpb_rows, qn, kn, vn, cache_k, cache_v)


def _out_kernel(x_ref, o_ref, r_ref, on_ctx_ref, on_lat_ref, mod_ref, ng_ref, gng_ref, w32_ref, y_ref, w_ref):
    @pl.when(pl.program_id(0) == 0)
    def _():
        w_ref[...] = w32_ref[0].astype(BF16)

    o_nat = _read_split(pl.program_id(0), on_ctx_ref, on_lat_ref)
    og = o_ref[...]
    parts = []
    for h in range(H_GLA):
        parts.append(_rms(og[:, h * DV_GLA:(h + 1) * DV_GLA], gng_ref[...]))
    merged = (jnp.concatenate(parts, axis=1) * _silu(r_ref[...])).astype(BF16)
    merged = _permute_chunks(_block_transpose_perm(), merged)
    y = _dot(merged, w_ref[0:V_W, :]) + _dot(o_nat.astype(BF16), w_ref[V_W:, :])
    y_ref[...] = x_ref[...] + mod_ref[0, 5:6, :] * _rms(y, ng_ref[3:4, :])


def _mixer_out(x, o_gla, r, o_ctx, o_lat, mod_l, ng_l, gng_l, w_out, *, layer):
    tile = lambda w: pl.BlockSpec((TM, w), lambda i: (i, 0))
    full = lambda a: pl.BlockSpec(a.shape, lambda i: (0,) * a.ndim)
    return pl.pallas_call(
        _out_kernel,
        grid=(N_ALL // TM,),
        in_specs=[
            tile(D_MODEL),
            tile(V_W),
            tile(V_W),
            pl.BlockSpec((TM, NAT_W), lambda i: (_ctx_tile(i), 0)),
            pl.BlockSpec((TM, NAT_W), lambda i: (_lat_tile(i), 0)),
            pl.BlockSpec((1, N_MOD, D_MODEL), lambda i: (_group_of_tile(i), 0, 0)),
            full(ng_l), full(gng_l),
            pl.BlockSpec((1,) + w_out.shape[1:], lambda i: (layer, 0, 0)),
        ],
        out_specs=tile(D_MODEL),
        out_shape=jax.ShapeDtypeStruct((N_ALL, D_MODEL), F32),
        scratch_shapes=[pltpu.VMEM(w_out.shape[1:], BF16)],
        compiler_params=pltpu.CompilerParams(
            dimension_semantics=("arbitrary",), vmem_limit_bytes=VMEM_LIMIT),
        name="mixer_out",
    )(x, o_gla, r, o_ctx, o_lat, mod_l, ng_l, gng_l, w_out)


def _rope_tables():
    quarter = DK_GLA // 4
    freqs = ROPE_BASE ** (-jnp.arange(quarter, dtype=F32) / quarter)
    t = jnp.arange(DEC_SEQ)
    ang_r = (t // GRID_W).astype(F32)[:, None] * freqs[None, :]
    ang_c = (t % GRID_W).astype(F32)[:, None] * freqs[None, :]
    cos_h = jnp.concatenate([jnp.cos(ang_r), jnp.cos(ang_r), jnp.cos(ang_c), jnp.cos(ang_c)], axis=1)
    sin_h = jnp.concatenate([-jnp.sin(ang_r), jnp.sin(ang_r), -jnp.sin(ang_c), jnp.sin(ang_c)], axis=1)
    pos_major = lambda a: a.reshape(-1, GLA_NB, GLA_CB, QK_W).transpose(0, 2, 1, 3).reshape(a.shape)
    cos_t = jnp.concatenate([pos_major(jnp.tile(cos_h, (1, H_GLA))), jnp.ones((TM, QK_W), F32)], axis=0)
    sin_t = jnp.concatenate([pos_major(jnp.tile(sin_h, (1, H_GLA))), jnp.zeros((TM, QK_W), F32)], axis=0)
    return cos_t, sin_t


def _gate_up_weights(gla_wa2_l, gla_ba_l):
    w_a = jnp.zeros((LANES, 2 * QK_W), F32)
    w_a = w_a.at[0:GLA_RANK, 0:QK_W].set(gla_wa2_l[0]).at[GLA_RANK:2 * GLA_RANK, QK_W:].set(gla_wa2_l[1])
    b_a = jnp.concatenate([gla_ba_l[0], gla_ba_l[1]])[None, :]
    return w_a.astype(BF16), b_a


def kernel(x_prompt, x_sample, cache_k, cache_v, state_gla, c, c_ctx, w_mod, b_mod, norm_g, ffn_w_in, ffn_w_out,
           w_in, gla_wa2, gla_ba, gla_norm_g, nat_rpb, w_out):
    cvecs = jnp.zeros((SUBLANES, D_MODEL), F32).at[0].set(c_ctx).at[1:1 + DEC_BATCH].set(c)
    mod = _modulation(cvecs, w_mod, b_mod)[:, :N_GROUPS].reshape(DEPTH, N_GROUPS, N_MOD, D_MODEL)

    cos_t, sin_t = _rope_tables()
    ck = cache_k.reshape(DEC_BATCH, DEPTH, PAST_LEN, NAT_W)
    cv = cache_v.reshape(DEC_BATCH, DEPTH, PAST_LEN, NAT_W)

    x = (x_prompt.reshape(N_CTX, D_MODEL), x_sample.reshape(N_LAT, D_MODEL))
    new_cache, new_state = (), ()
    for l in range(DEPTH):
        mod_l, ng_l = mod[l], norm_g[l]
        x = _ffn(x, mod_l, ng_l, ffn_w_in, ffn_w_out, layer=l, which=0)

        w_a, b_a = _gate_up_weights(gla_wa2[l], gla_ba[l])
        q, k, v, r, g, qn, new_k, new_v, kn, vn = _project(
            x, mod_l, ng_l, w_in, w_a, b_a, cos_t, sin_t, new_cache, layer=l)
        new_cache = (new_k, new_v)

        o_gla, new_s = _gla(q, k, v, g, state_gla, new_state, layer=l)
        new_state = (new_s,)
        o_ctx = _ctx_attention(qn, kn, vn)
        o_lat = _nat_attention(qn, kn, vn, ck, cv, nat_rpb[l], l)

        x = _mixer_out(x, o_gla, r, o_ctx, o_lat, mod_l, ng_l, gla_norm_g[l][None, :], w_out, layer=l)
        x = _ffn(x, mod_l, ng_l, ffn_w_in, ffn_w_out, layer=l, which=1, split_out=(l == DEPTH - 1))

    y_prompt = x[0].reshape(BATCH, SEQ, D_MODEL)
    y_sample = x[1].reshape(DEC_BATCH, DEC_SEQ, D_MODEL)
    return (y_prompt, y_sample, new_cache[0], new_cache[1], new_state[0])
```

```python
import functools

import numpy as np
import jax
import jax.numpy as jnp
from jax import lax
from jax.experimental import pallas as pl
from jax.experimental.pallas import tpu as pltpu

D_MODEL = 1024
BATCH = 16
SEQ = 256
DEPTH = 2
DEC_BATCH = 2
DEC_SEQ = 1024
PAST_LEN = 512
GRID_W = 64
H_GLA = 4
DK_GLA = 64
DV_GLA = 128
GLA_RANK = 16
GATE_NORM = 16.0
H_NAT = 8
HD_NAT = 64
WIN_H = 8
WIN_W = 16
D_FF = 2816
N_MOD = 9
ROPE_BASE = 10000.0
EPS = 1e-6
NEG_INF = -1e30

F32 = jnp.float32
BF16 = jnp.bfloat16

N_CTX = BATCH * SEQ
N_LAT = DEC_BATCH * DEC_SEQ
N_ALL = N_CTX + N_LAT
N_GROUPS = 1 + DEC_BATCH
QK_W = H_GLA * DK_GLA
V_W = H_GLA * DV_GLA
NAT_W = H_NAT * HD_NAT
GRID_ROWS = DEC_SEQ // GRID_W

LANES = 128
SUBLANES = 8

TM = 512
FFN_TC = 256
FFN_NC = D_FF // FFN_TC
FFN_STAGES = 2
MOD_TN = 1152
GLA_T = 256
GLA_CB = 16
GLA_NB = GLA_T // GLA_CB
VMEM_LIMIT = 56 * 1024 * 1024

assert N_CTX % TM == 0 and DEC_SEQ % TM == 0 and D_FF % FFN_TC == 0 and FFN_TC % LANES == 0
assert SEQ == GLA_T and DEC_SEQ % GLA_T == 0


def _group_of_tile(i):
    return jnp.where(i < N_CTX // TM, 0, 1 + (i - N_CTX // TM) // (DEC_SEQ // TM))


def _dot(a, b):
    return jnp.dot(a, b, preferred_element_type=F32)


def _dot_nt(a, b):
    return lax.dot_general(a, b, (((1,), (1,)), ((), ())), preferred_element_type=F32)


def _dot_tn(a, b):
    return lax.dot_general(a, b, (((0,), (0,)), ((), ())), preferred_element_type=F32)


def _rms(x, g):
    ms = jnp.mean(x * x, axis=-1, keepdims=True)
    return x * lax.rsqrt(ms + EPS) * g


def _silu(x):
    return x * jax.nn.sigmoid(x)


def _mod_kernel(c_ref, w_ref, b_ref, o_ref):
    s = _silu(c_ref[...]).astype(BF16)
    o_ref[0] = _dot(s, w_ref[0].astype(BF16)) + b_ref[0]


def _modulation(cvecs, w_mod, b_mod):
    n_out = N_MOD * D_MODEL
    return pl.pallas_call(
        _mod_kernel,
        grid=(DEPTH, n_out // MOD_TN),
        in_specs=[
            pl.BlockSpec((SUBLANES, D_MODEL), lambda l, j: (0, 0)),
            pl.BlockSpec((1, D_MODEL, MOD_TN), lambda l, j: (l, 0, j)),
            pl.BlockSpec((1, 1, MOD_TN), lambda l, j: (l, 0, j)),
        ],
        out_specs=pl.BlockSpec((1, SUBLANES, MOD_TN), lambda l, j: (l, 0, j)),
        out_shape=jax.ShapeDtypeStruct((DEPTH, SUBLANES, n_out), F32),
        compiler_params=pltpu.CompilerParams(
            dimension_semantics=("arbitrary", "arbitrary"), vmem_limit_bytes=VMEM_LIMIT),
        name="modulation",
    )(cvecs, w_mod, b_mod.reshape(DEPTH, 1, n_out))


_CTX_TILES = N_CTX // TM
_LAT_TILES = N_LAT // TM


def _ctx_tile(i):
    return jnp.minimum(i, _CTX_TILES - 1)


def _lat_tile(i):
    return jnp.maximum(i - _CTX_TILES, 0)


def _read_split(i, ctx_ref, lat_ref):
    return jnp.where(i < _CTX_TILES, ctx_ref[...], lat_ref[...])


def _ffn_kernel(*refs, m0, n0, layer, which, split_in, split_out):
    n_x = 2 if split_in else 1
    x_refs, (mod_ref, ng_ref, win_hbm, wout_hbm) = refs[:n_x], refs[n_x:n_x + 4]
    n_o = 2 if split_out else 1
    o_refs = refs[n_x + 4:n_x + 4 + n_o]
    h_scr, acc_scr, x_scr, wg_bf, wu_bf, wo_bf, stage_g, stage_u, stage_o, sem = refs[n_x + 4 + n_o:]
    i = pl.program_id(0)

    def chunk_copies(c, slot):
        cols = pl.ds(c * FFN_TC, FFN_TC)
        up_cols = pl.ds(D_FF + c * FFN_TC, FFN_TC)
        return (
            pltpu.make_async_copy(win_hbm.at[layer, which, :, cols], stage_g.at[slot], sem.at[0, slot]),
            pltpu.make_async_copy(win_hbm.at[layer, which, :, up_cols], stage_u.at[slot], sem.at[1, slot]),
            pltpu.make_async_copy(wout_hbm.at[layer, which, cols, :], stage_o.at[slot], sem.at[2, slot]),
        )

    def accumulate(c):
        h = h_scr[...]
        gate = _dot(h, wg_bf[c])
        up = _dot(h, wu_bf[c])
        part = _dot((_silu(gate) * up).astype(BF16), wo_bf[c])
        if c == 0:
            acc_scr[...] = part
        else:
            acc_scr[...] += part

    x = _read_split(i, *x_refs) if split_in else x_refs[0][...]
    x_scr[...] = x
    gain = ng_ref[n0:n0 + 1, :] * (1.0 + mod_ref[0, m0 + 1:m0 + 2, :])
    h_scr[...] = (_rms(x, gain) + mod_ref[0, m0:m0 + 1, :]).astype(BF16)

    @pl.when(i == 0)
    def _():
        for c in range(min(FFN_STAGES, FFN_NC)):
            for cp in chunk_copies(c, c):
                cp.start()
        for c in range(FFN_NC):
            slot = c % FFN_STAGES
            for cp in chunk_copies(c, slot):
                cp.wait()
            wg_bf[c] = stage_g[slot].astype(BF16)
            wu_bf[c] = stage_u[slot].astype(BF16)
            wo_bf[c] = stage_o[slot].astype(BF16)
            if c + FFN_STAGES < FFN_NC:
                for cp in chunk_copies(c + FFN_STAGES, slot):
                    cp.start()
            accumulate(c)

    @pl.when(i > 0)
    def _():
        for c in range(FFN_NC):
            accumulate(c)

    gain = 0.5 * mod_ref[0, m0 + 2:m0 + 3, :] * ng_ref[n0 + 1:n0 + 2, :]
    out = x_scr[...] + _rms(acc_scr[...], gain)
    if split_out:
        @pl.when(i < _CTX_TILES)
        def _():
            o_refs[0][...] = out

        @pl.when(i >= _CTX_TILES)
        def _():
            o_refs[1][...] = out
    else:
        o_refs[0][...] = out


def _ffn(x, mod_l, ng_l, w_in, w_out, *, layer, which, split_out=False):
    m0, n0 = (0, 0) if which == 0 else (6, 4)
    split_in = isinstance(x, tuple)
    tile = pl.BlockSpec((TM, D_MODEL), lambda i: (i, 0))
    ctx_tile = pl.BlockSpec((TM, D_MODEL), lambda i: (_ctx_tile(i), 0))
    lat_tile = pl.BlockSpec((TM, D_MODEL), lambda i: (_lat_tile(i), 0))
    if split_out:
        out_specs = [ctx_tile, lat_tile]
        out_shape = [jax.ShapeDtypeStruct((N_CTX, D_MODEL), F32), jax.ShapeDtypeStruct((N_LAT, D_MODEL), F32)]
    else:
        out_specs = tile
        out_shape = jax.ShapeDtypeStruct((N_ALL, D_MODEL), F32)
    return pl.pallas_call(
        functools.partial(_ffn_kernel, m0=m0, n0=n0, layer=layer, which=which, split_in=split_in,
                          split_out=split_out),
        grid=(N_ALL // TM,),
        in_specs=([ctx_tile, lat_tile] if split_in else [tile]) + [
            pl.BlockSpec((1, N_MOD, D_MODEL), lambda i: (_group_of_tile(i), 0, 0)),
            pl.BlockSpec((6, D_MODEL), lambda i: (0, 0)),
            pl.BlockSpec(memory_space=pl.ANY),
            pl.BlockSpec(memory_space=pl.ANY),
        ],
        out_specs=out_specs,
        out_shape=out_shape,
        scratch_shapes=[
            pltpu.VMEM((TM, D_MODEL), BF16),
            pltpu.VMEM((TM, D_MODEL), F32),
            pltpu.VMEM((TM, D_MODEL), F32),
            pltpu.VMEM((FFN_NC, D_MODEL, FFN_TC), BF16),
            pltpu.VMEM((FFN_NC, D_MODEL, FFN_TC), BF16),
            pltpu.VMEM((FFN_NC, FFN_TC, D_MODEL), BF16),
            pltpu.VMEM((FFN_STAGES, D_MODEL, FFN_TC), F32),
            pltpu.VMEM((FFN_STAGES, D_MODEL, FFN_TC), F32),
            pltpu.VMEM((FFN_STAGES, FFN_TC, D_MODEL), F32),
            pltpu.SemaphoreType.DMA((3, FFN_STAGES)),
        ],
        compiler_params=pltpu.CompilerParams(
            dimension_semantics=("arbitrary",), vmem_limit_bytes=VMEM_LIMIT),
        name="ffn",
    )(*(x if split_in else (x,)), mod_l, ng_l, w_in, w_out)


def _block_transpose_perm():
    r = lax.broadcasted_iota(jnp.int32, (GLA_T, GLA_T), 0)
    c = lax.broadcasted_iota(jnp.int32, (GLA_T, GLA_T), 1)
    return jnp.where(c == (r & (GLA_CB - 1)) * GLA_NB + (r >> 4), 1.0, 0.0).astype(BF16)


def _permute_chunks(perm, a):
    return jnp.concatenate([_dot(perm, a[c * GLA_T:(c + 1) * GLA_T, :]).astype(BF16)
                            for c in range(a.shape[0] // GLA_T)], axis=0)


_P_QK = 0
_P_VR = 2 * QK_W
_P_NAT = _P_VR + 2 * V_W
_P_LR = _P_NAT + 3 * NAT_W
_P_END = _P_LR + LANES


_NAT_SCALE = HD_NAT ** -0.5
assert _NAT_SCALE == 2.0 ** round(np.log2(_NAT_SCALE))
_GLA_COLS = 2 * QK_W + 2 * V_W
_REST_COLS = 2 * GLA_RANK + 3 * NAT_W
_REST_TILES = -(-_REST_COLS // LANES)
_PREP_ROWS = 128
assert _GLA_COLS % LANES == 0 and _GLA_COLS + _REST_COLS == 3104 and D_MODEL % _PREP_ROWS == 0


def _proj_prepare_weights(win_hbm, tail_ref, layer, w_bf, stage, sem):
    lane = lax.broadcasted_iota(jnp.int32, (_PREP_ROWS, LANES), 1)
    shift = 2 * GLA_RANK

    head = pltpu.make_async_copy(win_hbm.at[layer, :, pl.ds(0, _GLA_COLS)], stage.at[:, pl.ds(0, _GLA_COLS)],
                                 sem.at[0])
    head.start()
    head.wait()
    for r0 in range(0, D_MODEL, _PREP_ROWS):
        rows = slice(r0, r0 + _PREP_ROWS)
        w_bf[rows, _P_QK:_P_NAT] = stage[rows, 0:_GLA_COLS].astype(BF16)

    whole = (_REST_TILES - 1) * LANES
    stage[:, whole:] = tail_ref[0]
    rest = pltpu.make_async_copy(win_hbm.at[layer, :, pl.ds(_GLA_COLS, whole)], stage.at[:, pl.ds(0, whole)],
                                 sem.at[1])
    rest.start()
    rest.wait()
    for r0 in range(0, D_MODEL, _PREP_ROWS):
        rows = slice(r0, r0 + _PREP_ROWS)
        w_bf[rows, _P_LR:_P_END] = jnp.where(lane < shift, stage[rows, 0:LANES], 0.0).astype(BF16)
        for t in range(3 * NAT_W // LANES):
            lo = pltpu.roll(stage[rows, t * LANES:(t + 1) * LANES], LANES - shift, axis=1)
            hi = pltpu.roll(stage[rows, (t + 1) * LANES:(t + 2) * LANES], LANES - shift, axis=1)
            w_bf[rows, _P_NAT + t * LANES:_P_NAT + (t + 1) * LANES] = jnp.where(lane < LANES - shift, lo,
                                                                                  hi).astype(BF16)


def _rotary_partner(x):
    lane = lax.broadcasted_iota(jnp.int32, (x.shape[0], LANES), 1)
    first_of_pair = ((lane >> 4) & 1) == 0
    half = DK_GLA // 4
    tiles = []
    for t in range(x.shape[1] // LANES):
        tile = x[:, t * LANES:(t + 1) * LANES]
        tiles.append(jnp.where(first_of_pair, pltpu.roll(tile, LANES - half, axis=1), pltpu.roll(tile, half, axis=1)))
    return jnp.concatenate(tiles, axis=1)


def _proj_kernel(x_ref, mod_ref, ng_ref, win_hbm, tail_ref, wa_ref, ba_ref, cos_ref, sin_ref, *rest, layer):
    q_ref, k_ref, v_ref, r_ref, g_ref, qn_ref, kn_ctx_ref, vn_ctx_ref, kn_ref, vn_ref = rest[-13:-3]
    w_ref, stage, sem = rest[-3:]
    i = pl.program_id(0)

    @pl.when(i == 0)
    def _():
        _proj_prepare_weights(win_hbm, tail_ref, layer, w_ref, stage, sem)

    h = _rms(x_ref[...], ng_ref[2:3, :])
    h = (h * (1.0 + mod_ref[0, 4:5, :]) + mod_ref[0, 3:4, :]).astype(BF16)
    hp = _permute_chunks(_block_transpose_perm(), h)

    lr = _dot(hp, w_ref[:, _P_LR:_P_END]).astype(BF16)
    z = _dot(lr, wa_ref[...]) + ba_ref[...]
    g = (jnp.minimum(z, 0.0) - jnp.log1p(jnp.exp(-jnp.abs(z)))) * (1.0 / GATE_NORM)
    g_ref[0] = g[:, 0:QK_W]
    g_ref[1] = g[:, QK_W:2 * QK_W]

    qk = _dot(hp, w_ref[:, _P_QK:_P_VR])
    cos = cos_ref[...]
    sin = sin_ref[...]
    q_ref[...] = (qk[:, 0:QK_W] * cos + _rotary_partner(qk[:, 0:QK_W]) * sin) * (DK_GLA ** -0.5)
    k_ref[...] = qk[:, QK_W:2 * QK_W] * cos + _rotary_partner(qk[:, QK_W:2 * QK_W]) * sin

    vr = _dot(hp, w_ref[:, _P_VR:_P_NAT])
    v_ref[...] = vr[:, 0:V_W]
    r_ref[...] = vr[:, V_W:2 * V_W]

    nat = _dot(h, w_ref[:, _P_NAT:_P_LR])
    qn_ref[...] = (nat[:, 0:NAT_W] * _NAT_SCALE).astype(BF16)
    kn_ref[...] = nat[:, NAT_W:2 * NAT_W].astype(BF16)
    vn_ref[...] = nat[:, 2 * NAT_W:3 * NAT_W].astype(BF16)

    @pl.when(i < _CTX_TILES)
    def _():
        heads = (TM // SEQ, SEQ, H_NAT, HD_NAT)
        kn_ctx_ref[:, 0] = nat[:, NAT_W:2 * NAT_W].reshape(heads)
        vn_ctx_ref[:, 0] = nat[:, 2 * NAT_W:3 * NAT_W].reshape(heads)
        for later in range(1, kn_ctx_ref.shape[1]):
            kn_ctx_ref[:, later] = jnp.zeros(heads, F32)
            vn_ctx_ref[:, later] = jnp.zeros(heads, F32)


def _rope_table_block(i):
    lat_tiles = DEC_SEQ // TM
    return jnp.where(i < N_CTX // TM, lat_tiles, (i - N_CTX // TM) % lat_tiles)


def _project(x, mod_l, ng_l, w_in, w_a, b_a, cos_t, sin_t, new_cache, *, layer):
    tile = lambda w: pl.BlockSpec((TM, w), lambda i: (i, 0))
    if new_cache:
        ctx_heads = pl.BlockSpec((TM // SEQ, 1, SEQ, H_NAT, HD_NAT), lambda i: (_ctx_tile(i), layer, 0, 0, 0))
    else:
        ctx_heads = pl.BlockSpec((TM // SEQ, DEPTH, SEQ, H_NAT, HD_NAT), lambda i: (_ctx_tile(i), 0, 0, 0, 0))
    full = lambda a: pl.BlockSpec(a.shape, lambda i: (0,) * a.ndim)
    n_in = 9
    tail0 = _GLA_COLS + (_REST_TILES - 1) * LANES
    w_tail = jnp.pad(w_in[:, :, tail0:], ((0, 0), (0, 0), (0, LANES - (w_in.shape[2] - tail0))))
    return pl.pallas_call(
        functools.partial(_proj_kernel, layer=layer),
        grid=(N_ALL // TM,),
        in_specs=[
            tile(D_MODEL),
            pl.BlockSpec((1, N_MOD, D_MODEL), lambda i: (_group_of_tile(i), 0, 0)),
            full(ng_l),
            pl.BlockSpec(memory_space=pl.ANY),
            pl.BlockSpec((1, D_MODEL, LANES), lambda i: (layer, 0, 0)),
            full(w_a), full(b_a),
            pl.BlockSpec((TM, QK_W), lambda i: (_rope_table_block(i), 0)),
            pl.BlockSpec((TM, QK_W), lambda i: (_rope_table_block(i), 0)),
        ] + [pl.BlockSpec(memory_space=pl.ANY)] * len(new_cache),
        input_output_aliases={n_in + n: 6 + n for n in range(len(new_cache))},
        out_specs=[
            tile(QK_W), tile(QK_W), tile(V_W), tile(V_W),
            pl.BlockSpec((2, TM, QK_W), lambda i: (0, i, 0)),
            tile(NAT_W), ctx_heads, ctx_heads, tile(NAT_W), tile(NAT_W),
        ],
        out_shape=[
            jax.ShapeDtypeStruct((N_ALL, QK_W), F32), jax.ShapeDtypeStruct((N_ALL, QK_W), F32),
            jax.ShapeDtypeStruct((N_ALL, V_W), F32), jax.ShapeDtypeStruct((N_ALL, V_W), F32),
            jax.ShapeDtypeStruct((2, N_ALL, QK_W), F32),
            jax.ShapeDtypeStruct((N_ALL, NAT_W), BF16),
            jax.ShapeDtypeStruct((BATCH, DEPTH, SEQ, H_NAT, HD_NAT), F32),
            jax.ShapeDtypeStruct((BATCH, DEPTH, SEQ, H_NAT, HD_NAT), F32),
            jax.ShapeDtypeStruct((N_ALL, NAT_W), BF16), jax.ShapeDtypeStruct((N_ALL, NAT_W), BF16),
        ],
        scratch_shapes=[
            pltpu.VMEM((D_MODEL, _P_END), BF16),
            pltpu.VMEM((D_MODEL, _REST_TILES * LANES), F32),
            pltpu.SemaphoreType.DMA((2,)),
        ],
        compiler_params=pltpu.CompilerParams(
            dimension_semantics=("arbitrary",), vmem_limit_bytes=VMEM_LIMIT),
        name="mixer_proj",
    )(x, mod_l, ng_l, w_in, w_tail, w_a, b_a, cos_t, sin_t, *new_cache)


def _gla_tables():
    rows = []
    seq_specs = [(b * (SEQ // GLA_T), SEQ // GLA_T, 0, 0, b, 1) for b in range(BATCH)]
    seq_specs += [(N_CTX // GLA_T + b * (DEC_SEQ // GLA_T), DEC_SEQ // GLA_T, b, 1, BATCH - 1, 0)
                  for b in range(DEC_BATCH)]
    for blk0, nchunk, s0_row, has_s0, fin_row, wants_fin in seq_specs:
        states = (s0_row, has_s0, fin_row, wants_fin)
        if nchunk == 1:
            rows.append((blk0, _GLA_BOTH, 1, blk0, 0) + states)
            continue
        for direction in (0, 1):
            order = range(nchunk) if direction == 0 else range(nchunk - 1, -1, -1)
            for n, c in enumerate(order):
                out_blk = blk0 + (c if direction == 1 else nchunk - 1)
                rows.append((blk0 + c, direction, int(n == 0), out_blk, c) + states)
    return np.asarray(rows, dtype=np.int32).T.copy()


_GLA_BOTH = 2


_GLA_TAB = _gla_tables()
_GLA_ITEMS = _GLA_TAB.shape[1]
_GLA_PAIR_ROWS = GLA_NB * GLA_CB * (GLA_CB + 1) // 2


def _gla_item(direction, q_ref, k_ref, v_ref, g_ref, res, s_scr, cp, sstk, p_scr, w_scr):
    T, CB, NB = GLA_T, GLA_CB, GLA_NB
    fwd = direction == 0
    slab = lambda i: slice(i * NB, (i + 1) * NB)

    cum = None
    for i in (range(CB) if fwd else range(CB - 1, -1, -1)):
        gi = g_ref[slab(i), :]
        cum = gi if cum is None else cum + gi
        cp[slab(i), :] = cum
    total = cum
    cum_all = cp[...]
    qt = q_ref[...] * jnp.exp(cum_all)
    kh = k_ref[...] * jnp.exp(jnp.concatenate([total] * CB, axis=0) - cum_all)
    dec_t = jnp.concatenate([jnp.exp(total), jnp.zeros((LANES - NB, QK_W), F32)], axis=0).T

    erow = lax.broadcasted_iota(jnp.int32, (QK_W, QK_W), 0)
    ecol = lax.broadcasted_iota(jnp.int32, (QK_W, QK_W), 1)
    head_sum = jnp.where((erow >> 6) == (ecol >> 6), 1.0, 0.0).astype(BF16)

    key_positions = lambda i: range(i + 1) if fwd else range(i, CB)
    r0 = 0
    for i in range(CB):
        qi = q_ref[slab(i), :]
        ci = cp[slab(i), :]
        for j in key_positions(i):
            e = jnp.exp(ci - cp[slab(j), :])
            p_scr[r0:r0 + NB, :] = (qi * k_ref[slab(j), :] * e).astype(BF16)
            r0 += NB
    w_scr[...] = _dot(p_scr[...], head_sum)
    first_half = lax.broadcasted_iota(jnp.int32, (NB, LANES), 1) < DK_GLA
    r0 = 0
    for i in range(CB):
        acc = None
        for j in key_positions(i):
            spread = []
            for pair in range(H_GLA // 2):
                tile = w_scr[r0:r0 + NB, pair * LANES:(pair + 1) * LANES]
                other = pltpu.roll(tile, DK_GLA, axis=1)
                spread += [jnp.where(first_half, tile, other), jnp.where(first_half, other, tile)]
            term = jnp.concatenate(spread, axis=1) * v_ref[slab(j), :]
            acc = term if acc is None else acc + term
            r0 += NB
        res[slab(i), :] = acc

    kht = kh.T.astype(BF16)
    v_bf = v_ref[...].astype(BF16)
    key_blk = lax.broadcasted_iota(jnp.int32, (DK_GLA, T), 1) & (NB - 1)
    row_blk = lax.broadcasted_iota(jnp.int32, (T, LANES), 0) & (NB - 1)
    lane_half = lax.broadcasted_iota(jnp.int32, (T, LANES), 1) >> 6
    order = range(NB) if fwd else range(NB - 1, -1, -1)
    for h in range(H_GLA):
        kh_h = kht[h * DK_GLA:(h + 1) * DK_GLA, :]
        kv = _dot(jnp.concatenate([jnp.where(key_blk == b, kh_h, 0) for b in range(NB)], axis=0),
                  v_bf[:, h * DV_GLA:(h + 1) * DV_GLA])
        s = s_scr[h]
        for b in order:
            sstk[h, b * DK_GLA:(b + 1) * DK_GLA, :] = s.astype(BF16)
            s = dec_t[h * DK_GLA:(h + 1) * DK_GLA, b:b + 1] * s + kv[b * DK_GLA:(b + 1) * DK_GLA, :]
        s_scr[h] = s

        pair_tile = qt[:, (h // 2) * LANES:(h // 2 + 1) * LANES]
        both = jnp.where(lane_half == h % 2, pair_tile, pltpu.roll(pair_tile, DK_GLA, axis=1))
        lhs = jnp.concatenate([jnp.where(row_blk == 2 * j + lane_half, both, 0.0).astype(BF16)
                               for j in range(NB // 2)], axis=1)
        res[:, h * DV_GLA:(h + 1) * DV_GLA] += _dot(lhs, sstk[h])


def _gla_kernel(tab_ref, q_ref, k_ref, v_ref, g_ref, s0_ref, *rest):
    o_ref, so_ref, s_scr, cp, sstk, p_scr, w_scr, res, held = rest[-9:]
    it = pl.program_id(0)
    kind = tab_ref[1, it]
    slot = tab_ref[4, it]
    has_s0 = tab_ref[6, it] == 1
    wants_final = tab_ref[8, it] == 1

    def run(d):
        _gla_item(d, q_ref, k_ref, v_ref, g_ref.at[d], res.at[d], s_scr.at[d], cp.at[d], sstk.at[d],
                  p_scr.at[d], w_scr.at[d])

    def clear_later_layers():
        for later in range(1, so_ref.shape[1]):
            so_ref[0, later] = jnp.zeros(so_ref.shape[2:], F32)

    @pl.when(kind == _GLA_BOTH)
    def _():
        s_scr[...] = jnp.where(has_s0, s0_ref[0, 0], 0.0)
        run(0)
        run(1)
        o_ref[...] = res[0] + res[1]

        @pl.when(wants_final)
        def _():
            so_ref[0, 0] = s_scr[...]
            clear_later_layers()

    for d in (0, 1):
        @pl.when(kind == d)
        def _(d=d):
            @pl.when(tab_ref[2, it] == 1)
            def _():
                s_scr[d] = jnp.where(has_s0, s0_ref[0, 0, d], 0.0)

            run(d)
            if d == 0:
                held[slot] = res[0]
            else:
                o_ref[...] = held[slot] + res[1]

            @pl.when(wants_final)
            def _():
                so_ref[0, 0, d] = s_scr[d]
                if d == 0:
                    clear_later_layers()


def _gla(q, k, v, g, state_gla, new_state, *, layer):
    tok = lambda w: pl.BlockSpec((GLA_T, w), lambda it, tab: (tab[0, it], 0))
    state_blk = (2, H_GLA, DK_GLA, DV_GLA)
    if new_state:
        final_spec = pl.BlockSpec((1, 1) + state_blk, lambda it, tab: (tab[7, it], layer, 0, 0, 0, 0))
    else:
        final_spec = pl.BlockSpec((1, DEPTH) + state_blk, lambda it, tab: (tab[7, it], 0, 0, 0, 0, 0))
    grid_spec = pltpu.PrefetchScalarGridSpec(
        num_scalar_prefetch=1,
        grid=(_GLA_ITEMS,),
        in_specs=[
            tok(QK_W), tok(QK_W), tok(V_W),
            pl.BlockSpec((2, GLA_T, QK_W), lambda it, tab: (0, tab[0, it], 0)),
            pl.BlockSpec((1, 1) + state_blk, lambda it, tab: (tab[5, it], layer, 0, 0, 0, 0)),
        ] + [pl.BlockSpec(memory_space=pl.ANY)] * len(new_state),
        out_specs=[
            pl.BlockSpec((GLA_T, V_W), lambda it, tab: (tab[3, it], 0)),
            final_spec,
        ],
        scratch_shapes=[
            pltpu.VMEM((2, H_GLA, DK_GLA, DV_GLA), F32),
            pltpu.VMEM((2, GLA_T, QK_W), F32),
            pltpu.VMEM((2, H_GLA, GLA_NB * DK_GLA, DV_GLA), BF16),
            pltpu.VMEM((2, _GLA_PAIR_ROWS, QK_W), BF16),
            pltpu.VMEM((2, _GLA_PAIR_ROWS, QK_W), F32),
            pltpu.VMEM((2, GLA_T, V_W), F32),
            pltpu.VMEM((DEC_SEQ // GLA_T, GLA_T, V_W), F32),
        ],
    )
    return pl.pallas_call(
        _gla_kernel,
        grid_spec=grid_spec,
        out_shape=[
            jax.ShapeDtypeStruct((N_ALL, V_W), F32),
            jax.ShapeDtypeStruct((BATCH, DEPTH) + state_blk, F32),
        ],
        input_output_aliases={6 + n: 1 + n for n in range(len(new_state))},
        compiler_params=pltpu.CompilerParams(
            dimension_semantics=("arbitrary",), vmem_limit_bytes=VMEM_LIMIT),
        name="gla",
    )(jnp.asarray(_GLA_TAB), q, k, v, g, state_gla, *new_state)


def _with_ones(v):
    return jnp.concatenate([v, jnp.ones_like(v)], axis=1)


def _softmax_pv(s_list, v1_list):
    m = s_list[0].max(axis=-1, keepdims=True)
    for s in s_list[1:]:
        m = jnp.maximum(m, s.max(axis=-1, keepdims=True))
    acc = None
    for s, v1 in zip(s_list, v1_list):
        pv = _dot(jnp.exp(s - m).astype(BF16), v1)
        acc = pv if acc is None else acc + pv
    width = acc.shape[1] // 2
    return acc[:, :width] / acc[:, width:]


def _ctx_attn_kernel(q_ref, k_ref, v_ref, o_ref):
    lane =lax.broadcasted_iota(jnp.int32, (SEQ, LANES), 1)
    for t in range(NAT_W // LANES):
        sl = slice(t * LANES, (t + 1) * LANES)
        qt = q_ref[:, sl]
        kt = k_ref[:, sl].astype(BF16)
        vt = _with_ones(v_ref[:, sl].astype(BF16))
        out = jnp.zeros((SEQ, LANES), F32)
        for half in range(LANES // HD_NAT):
            mine = (lane >> 6) == half
            s = _dot_nt(jnp.where(mine, qt, 0).astype(BF16), kt)
            out = jnp.where(mine, _softmax_pv([s], [vt]), out)
        o_ref[:, sl] = out.astype(o_ref.dtype)


def _ctx_attention(qn, kn, vn):
    spec = pl.BlockSpec((SEQ, NAT_W), lambda b: (b, 0))
    return pl.pallas_call(
        _ctx_attn_kernel,
        grid=(BATCH,),
        in_specs=[spec, spec, spec],
        out_specs=spec,
        out_shape=jax.ShapeDtypeStruct((N_CTX, NAT_W), BF16),
        compiler_params=pltpu.CompilerParams(
            dimension_semantics=("arbitrary",), vmem_limit_bytes=VMEM_LIMIT),
        name="ctx_attention",
    )(qn, kn, vn)


_NAT_QROWS = 4
_NAT_GROUPS = GRID_ROWS // _NAT_QROWS
_NAT_KROWS = WIN_H + _NAT_QROWS
_NAT_Q = _NAT_QROWS * GRID_W
_NAT_KEYS = _NAT_KROWS * GRID_W
_NAT_DR = 2 * WIN_H - 1
_NAT_DC = 2 * WIN_W - 1


def _nat_key_row0(r):
    return jnp.clip(r - WIN_H // 2, 0, GRID_ROWS - WIN_H)


def _nat_build_bias(rpb_ref, tz_scr):
    c = lax.broadcasted_iota(jnp.int32, (GRID_W, LANES), 0)
    lane = lax.broadcasted_iota(jnp.int32, (GRID_W, LANES), 1)
    kc = lane & (GRID_W - 1)
    second = (lane >> 6) == 1
    win_start = jnp.clip(c - WIN_W // 2, 0, GRID_W - WIN_W)
    valid = (kc >= win_start) & (kc < win_start + WIN_W)

    def one_row(n, carry):
        dr = n >> 3
        h = n & (H_NAT - 1)
        src = jnp.broadcast_to(rpb_ref[pl.ds(h * _NAT_DR + dr, 1), :], (GRID_W, LANES))
        rolled = pltpu.roll(src, LANES - (WIN_W - 1), axis=1, stride=1, stride_axis=0)
        tz_scr[dr, h] = jnp.where(valid, rolled, NEG_INF)
        return carry

    lax.fori_loop(0, _NAT_DR * H_NAT, one_row, 0)

    def pair_rows(n, carry):
        dr = n >> 3
        h = n & (H_NAT - 1)
        tz_scr[dr, h] = jnp.where(second, tz_scr[dr + 1, h], tz_scr[dr, h])
        return carry

    lax.fori_loop(0, (_NAT_DR - 1) * H_NAT, pair_rows, 0)


def _nat_kernel(rpb_ref, q_ref, k_ref, v_ref, ck_ref, cv_ref, o_ref, tz_scr, bias_scr):
    grp = pl.program_id(1)

    @pl.when((pl.program_id(0) == 0) & (grp == 0))
    def _():
        _nat_build_bias(rpb_ref, tz_scr)

    krow0 = jnp.clip(_NAT_QROWS * grp - WIN_H // 2, 0, GRID_ROWS - _NAT_KROWS)
    k0 = pl.multiple_of(krow0 * GRID_W, GRID_W)
    lane_q = lax.broadcasted_iota(jnp.int32, (_NAT_Q, LANES), 1)
    lane_b = lax.broadcasted_iota(jnp.int32, (GRID_W, LANES), 1)
    neg = jnp.full((GRID_W, LANES), NEG_INF, F32)
    for t in range(NAT_W // LANES):
        sl = slice(t * LANES, (t + 1) * LANES)
        qt = q_ref[:, sl]
        kw = k_ref[pl.ds(k0, _NAT_KEYS), sl].astype(BF16)
        vw = _with_ones(v_ref[pl.ds(k0, _NAT_KEYS), sl].astype(BF16))
        ck = ck_ref[0, 0, :, sl].astype(BF16)
        cv = _with_ones(cv_ref[0, 0, :, sl].astype(BF16))
        out = jnp.zeros((_NAT_Q, LANES), F32)
        for half in range(LANES // HD_NAT):
            h = 2 * t + half
            for qr in range(_NAT_QROWS):
                r = _NAT_QROWS * grp + qr
                lo = _nat_key_row0(r)
                for kp in range(_NAT_KROWS // 2):
                    kr = krow0 + 2 * kp
                    tile = tz_scr[jnp.clip(kr - r + (WIN_H - 1), 0, _NAT_DR - 1), h]
                    ok_a = ((kr >= lo) & (kr < lo + WIN_H)).astype(jnp.int32)
                    ok_b = ((kr + 1 >= lo) & (kr + 1 < lo + WIN_H)).astype(jnp.int32)
                    ok = jnp.where(lane_b < GRID_W, ok_a, ok_b) == 1
                    bias_scr[qr * GRID_W:(qr + 1) * GRID_W, kp * LANES:(kp + 1) * LANES] = jnp.where(ok, tile, neg)
            mine = (lane_q >> 6) == half
            qm = jnp.where(mine, qt, 0).astype(BF16)
            s_win = _dot_nt(qm, kw) + bias_scr[...]
            s_ctx = _dot_nt(qm, ck)
            out = jnp.where(mine, _softmax_pv([s_win, s_ctx], [vw, cv]), out)
        o_ref[:, sl] = out.astype(o_ref.dtype)


def _nat_attention(qn, kn, vn, cache_k, cache_v, rpb_l, layer):
    lat0 = N_CTX // _NAT_Q
    half = jnp.pad(rpb_l.reshape(H_NAT * _NAT_DR, _NAT_DC), ((0, 0), (0, GRID_W - _NAT_DC)))
    rpb_rows = jnp.concatenate([half, half], axis=1)
    return pl.pallas_call(
        _nat_kernel,
        grid=(DEC_BATCH, _NAT_GROUPS),
        in_specs=[
            pl.BlockSpec((H_NAT * _NAT_DR, LANES), lambda b, g: (0, 0)),
            pl.BlockSpec((_NAT_Q, NAT_W), lambda b, g: (lat0 + b * _NAT_GROUPS + g, 0)),
            pl.BlockSpec((DEC_SEQ, NAT_W), lambda b, g: (N_CTX // DEC_SEQ + b, 0)),
            pl.BlockSpec((DEC_SEQ, NAT_W), lambda b, g: (N_CTX // DEC_SEQ + b, 0)),
            pl.BlockSpec((1, 1, PAST_LEN, NAT_W), lambda b, g: (b, layer, 0, 0)),
            pl.BlockSpec((1, 1, PAST_LEN, NAT_W), lambda b, g: (b, layer, 0, 0)),
        ],
        out_specs=pl.BlockSpec((_NAT_Q, NAT_W), lambda b, g: (b * _NAT_GROUPS + g, 0)),
        out_shape=jax.ShapeDtypeStruct((N_LAT, NAT_W), BF16),
        scratch_shapes=[
            pltpu.VMEM((_NAT_DR, H_NAT, GRID_W, LANES), F32),
            pltpu.VMEM((_NAT_Q, _NAT_KEYS), F32),
        ],
        compiler_params=pltpu.CompilerParams(
            dimension_semantics=("arbitrary", "arbitrary"), vmem_limit_bytes=VMEM_LIMIT),
        name="nat_attention",
    )(rpb_rows, qn, kn, vn, cache_k, cache_v)


def _out_kernel(x_ref, o_ref, r_ref, on_ctx_ref, on_lat_ref, mod_ref, ng_ref, gng_ref, w32_ref, y_ref, w_ref):
    @pl.when(pl.program_id(0) == 0)
    def _():
        w_ref[...] = w32_ref[0].astype(BF16)

    o_nat = _read_split(pl.program_id(0), on_ctx_ref, on_lat_ref)
    og = o_ref[...]
    parts = []
    for h in range(H_GLA):
        parts.append(_rms(og[:, h * DV_GLA:(h + 1) * DV_GLA], gng_ref[...]))
    merged = (jnp.concatenate(parts, axis=1) * _silu(r_ref[...])).astype(BF16)
    merged = _permute_chunks(_block_transpose_perm(), merged)
    y = _dot(merged, w_ref[0:V_W, :]) + _dot(o_nat.astype(BF16), w_ref[V_W:, :])
    y_ref[...] = x_ref[...] + mod_ref[0, 5:6, :] * _rms(y, ng_ref[3:4, :])


def _mixer_out(x, o_gla, r, o_ctx, o_lat, mod_l, ng_l, gng_l, w_out, *, layer):
    tile = lambda w: pl.BlockSpec((TM, w), lambda i: (i, 0))
    full = lambda a: pl.BlockSpec(a.shape, lambda i: (0,) * a.ndim)
    return pl.pallas_call(
        _out_kernel,
        grid=(N_ALL // TM,),
        in_specs=[
            tile(D_MODEL),
            tile(V_W),
            tile(V_W),
            pl.BlockSpec((TM, NAT_W), lambda i: (_ctx_tile(i), 0)),
            pl.BlockSpec((TM, NAT_W), lambda i: (_lat_tile(i), 0)),
            pl.BlockSpec((1, N_MOD, D_MODEL), lambda i: (_group_of_tile(i), 0, 0)),
            full(ng_l), full(gng_l),
            pl.BlockSpec((1,) + w_out.shape[1:], lambda i: (layer, 0, 0)),
        ],
        out_specs=tile(D_MODEL),
        out_shape=jax.ShapeDtypeStruct((N_ALL, D_MODEL), F32),
        scratch_shapes=[pltpu.VMEM(w_out.shape[1:], BF16)],
        compiler_params=pltpu.CompilerParams(
            dimension_semantics=("arbitrary",), vmem_limit_bytes=VMEM_LIMIT),
        name="mixer_out",
    )(x, o_gla, r, o_ctx, o_lat, mod_l, ng_l, gng_l, w_out)


def _rope_tables():
    quarter = DK_GLA // 4
    freqs = ROPE_BASE ** (-jnp.arange(quarter, dtype=F32) / quarter)
    t = jnp.arange(DEC_SEQ)
    ang_r = (t // GRID_W).astype(F32)[:, None] * freqs[None, :]
    ang_c = (t % GRID_W).astype(F32)[:, None] * freqs[None, :]
    cos_h = jnp.concatenate([jnp.cos(ang_r), jnp.cos(ang_r), jnp.cos(ang_c), jnp.cos(ang_c)], axis=1)
    sin_h = jnp.concatenate([-jnp.sin(ang_r), jnp.sin(ang_r), -jnp.sin(ang_c), jnp.sin(ang_c)], axis=1)
    pos_major = lambda a: a.reshape(-1, GLA_NB, GLA_CB, QK_W).transpose(0, 2, 1, 3).reshape(a.shape)
    cos_t = jnp.concatenate([pos_major(jnp.tile(cos_h, (1, H_GLA))), jnp.ones((TM, QK_W), F32)], axis=0)
    sin_t = jnp.concatenate([pos_major(jnp.tile(sin_h, (1, H_GLA))), jnp.zeros((TM, QK_W), F32)], axis=0)
    return cos_t, sin_t


def _gate_up_weights(gla_wa2_l, gla_ba_l):
    w_a = jnp.zeros((LANES, 2 * QK_W), F32)
    w_a = w_a.at[0:GLA_RANK, 0:QK_W].set(gla_wa2_l[0]).at[GLA_RANK:2 * GLA_RANK, QK_W:].set(gla_wa2_l[1])
    b_a = jnp.concatenate([gla_ba_l[0], gla_ba_l[1]])[None, :]
    return w_a.astype(BF16), b_a


def kernel(x_prompt, x_sample, cache_k, cache_v, state_gla, c, c_ctx, w_mod, b_mod, norm_g, ffn_w_in, ffn_w_out,
           w_in, gla_wa2, gla_ba, gla_norm_g, nat_rpb, w_out):
    cvecs = jnp.zeros((SUBLANES, D_MODEL), F32).at[0].set(c_ctx).at[1:1 + DEC_BATCH].set(c)
    mod = _modulation(cvecs, w_mod, b_mod)[:, :N_GROUPS].reshape(DEPTH, N_GROUPS, N_MOD, D_MODEL)

    cos_t, sin_t = _rope_tables()
    ck = cache_k.reshape(DEC_BATCH, DEPTH, PAST_LEN, NAT_W)
    cv = cache_v.reshape(DEC_BATCH, DEPTH, PAST_LEN, NAT_W)

    x = (x_prompt.reshape(N_CTX, D_MODEL), x_sample.reshape(N_LAT, D_MODEL))
    new_cache, new_state = (), ()
    for l in range(DEPTH):
        mod_l, ng_l = mod[l], norm_g[l]
        x = _ffn(x, mod_l, ng_l, ffn_w_in, ffn_w_out, layer=l, which=0)

        w_a, b_a = _gate_up_weights(gla_wa2[l], gla_ba[l])
        q, k, v, r, g, qn, new_k, new_v, kn, vn = _project(
            x, mod_l, ng_l, w_in, w_a, b_a, cos_t, sin_t, new_cache, layer=l)
        new_cache = (new_k, new_v)

        o_gla, new_s = _gla(q, k, v, g, state_gla, new_state, layer=l)
        new_state = (new_s,)
        o_ctx = _ctx_attention(qn, kn, vn)
        o_lat = _nat_attention(qn, kn, vn, ck, cv, nat_rpb[l], l)

        x = _mixer_out(x, o_gla, r, o_ctx, o_lat, mod_l, ng_l, gla_norm_g[l][None, :], w_out, layer=l)
        x = _ffn(x, mod_l, ng_l, ffn_w_in, ffn_w_out, layer=l, which=1, split_out=(l == DEPTH - 1))

    y_prompt = x[0].reshape(BATCH, SEQ, D_MODEL)
    y_sample = x[1].reshape(DEC_BATCH, DEC_SEQ, D_MODEL)
    return (y_prompt, y_sample, new_cache[0], new_cache[1], new_state[0])
```

```python
import functools

import numpy as np
import jax
import jax.numpy as jnp
from jax import lax
from jax.experimental import pallas as pl
from jax.experimental.pallas import tpu as pltpu

D_MODEL = 1024
BATCH = 16
SEQ = 256
DEPTH = 2
DEC_BATCH = 2
DEC_SEQ = 1024
PAST_LEN = 512
GRID_W = 64
H_GLA = 4
DK_GLA = 64
DV_GLA = 128
GLA_RANK = 16
GATE_NORM = 16.0
H_NAT = 8
HD_NAT = 64
WIN_H = 8
WIN_W = 16
D_FF = 2816
N_MOD = 9
ROPE_BASE = 10000.0
EPS = 1e-6
NEG_INF = -1e30

F32 = jnp.float32
BF16 = jnp.bfloat16

N_CTX = BATCH * SEQ
N_LAT = DEC_BATCH * DEC_SEQ
N_ALL = N_CTX + N_LAT
N_GROUPS = 1 + DEC_BATCH
QK_W = H_GLA * DK_GLA
V_W = H_GLA * DV_GLA
NAT_W = H_NAT * HD_NAT
GRID_ROWS = DEC_SEQ // GRID_W

LANES = 128
SUBLANES = 8

TM = 512
FFN_TC = 256
FFN_NC = D_FF // FFN_TC
FFN_STAGES = 2
MOD_TN = 1152
GLA_T = 256
GLA_CB = 16
GLA_NB = GLA_T // GLA_CB
VMEM_LIMIT = 56 * 1024 * 1024

assert N_CTX % TM == 0 and DEC_SEQ % TM == 0 and D_FF % FFN_TC == 0 and FFN_TC % LANES == 0
assert SEQ == GLA_T and DEC_SEQ % GLA_T == 0


def _group_of_tile(i):
    return jnp.where(i < N_CTX // TM, 0, 1 + (i - N_CTX // TM) // (DEC_SEQ // TM))


def _dot(a, b):
    return jnp.dot(a, b, preferred_element_type=F32)


def _dot_nt(a, b):
    return lax.dot_general(a, b, (((1,), (1,)), ((), ())), preferred_element_type=F32)


def _dot_tn(a, b):
    return lax.dot_general(a, b, (((0,), (0,)), ((), ())), preferred_element_type=F32)


def _rms(x, g):
    ms = jnp.mean(x * x, axis=-1, keepdims=True)
    return x * lax.rsqrt(ms + EPS) * g


def _silu(x):
    return x * jax.nn.sigmoid(x)


def _mod_kernel(c_ref, w_ref, b_ref, o_ref):
    s = _silu(c_ref[...]).astype(BF16)
    o_ref[0] = _dot(s, w_ref[0].astype(BF16)) + b_ref[0]


def _modulation(cvecs, w_mod, b_mod):
    n_out = N_MOD * D_MODEL
    return pl.pallas_call(
        _mod_kernel,
        grid=(DEPTH, n_out // MOD_TN),
        in_specs=[
            pl.BlockSpec((SUBLANES, D_MODEL), lambda l, j: (0, 0)),
            pl.BlockSpec((1, D_MODEL, MOD_TN), lambda l, j: (l, 0, j)),
            pl.BlockSpec((1, 1, MOD_TN), lambda l, j: (l, 0, j)),
        ],
        out_specs=pl.BlockSpec((1, SUBLANES, MOD_TN), lambda l, j: (l, 0, j)),
        out_shape=jax.ShapeDtypeStruct((DEPTH, SUBLANES, n_out), F32),
        compiler_params=pltpu.CompilerParams(
            dimension_semantics=("arbitrary", "arbitrary"), vmem_limit_bytes=VMEM_LIMIT),
        name="modulation",
    )(cvecs, w_mod, b_mod.reshape(DEPTH, 1, n_out))


_CTX_TILES = N_CTX // TM
_LAT_TILES = N_LAT // TM


def _ctx_tile(i):
    return jnp.minimum(i, _CTX_TILES - 1)


def _lat_tile(i):
    return jnp.maximum(i - _CTX_TILES, 0)


def _read_split(i, ctx_ref, lat_ref):
    return jnp.where(i < _CTX_TILES, ctx_ref[...], lat_ref[...])


def _ffn_kernel(*refs, m0, n0, layer, which, split_in, split_out):
    n_x = 2 if split_in else 1
    x_refs, (mod_ref, ng_ref, win_hbm, wout_hbm) = refs[:n_x], refs[n_x:n_x + 4]
    n_o = 2 if split_out else 1
    o_refs = refs[n_x + 4:n_x + 4 + n_o]
    h_scr, acc_scr, x_scr, wg_bf, wu_bf, wo_bf, stage_g, stage_u, stage_o, sem = refs[n_x + 4 + n_o:]
    i = pl.program_id(0)

    def chunk_copies(c, slot):
        cols = pl.ds(c * FFN_TC, FFN_TC)
        up_cols = pl.ds(D_FF + c * FFN_TC, FFN_TC)
        return (
            pltpu.make_async_copy(win_hbm.at[layer, which, :, cols], stage_g.at[slot], sem.at[0, slot]),
            pltpu.make_async_copy(win_hbm.at[layer, which, :, up_cols], stage_u.at[slot], sem.at[1, slot]),
            pltpu.make_async_copy(wout_hbm.at[layer, which, cols, :], stage_o.at[slot], sem.at[2, slot]),
        )

    def accumulate(c):
        h = h_scr[...]
        gate = _dot(h, wg_bf[c])
        up = _dot(h, wu_bf[c])
        part = _dot((_silu(gate) * up).astype(BF16), wo_bf[c])
        if c == 0:
            acc_scr[...] = part
        else:
            acc_scr[...] += part

    x = _read_split(i, *x_refs) if split_in else x_refs[0][...]
    x_scr[...] = x
    gain = ng_ref[n0:n0 + 1, :] * (1.0 + mod_ref[0, m0 + 1:m0 + 2, :])
    h_scr[...] = (_rms(x, gain) + mod_ref[0, m0:m0 + 1, :]).astype(BF16)

    @pl.when(i == 0)
    def _():
        for c in range(min(FFN_STAGES, FFN_NC)):
            for cp in chunk_copies(c, c):
                cp.start()
        for c in range(FFN_NC):
            slot = c % FFN_STAGES
            for cp in chunk_copies(c, slot):
                cp.wait()
            wg_bf[c] = stage_g[slot].astype(BF16)
            wu_bf[c] = stage_u[slot].astype(BF16)
            wo_bf[c] = stage_o[slot].astype(BF16)
            if c + FFN_STAGES < FFN_NC:
                for cp in chunk_copies(c + FFN_STAGES, slot):
                    cp.start()
            accumulate(c)

    @pl.when(i > 0)
    def _():
        for c in range(FFN_NC):
            accumulate(c)

    gain = 0.5 * mod_ref[0, m0 + 2:m0 + 3, :] * ng_ref[n0 + 1:n0 + 2, :]
    out = x_scr[...] + _rms(acc_scr[...], gain)
    if split_out:
        @pl.when(i < _CTX_TILES)
        def _():
            o_refs[0][...] = out

        @pl.when(i >= _CTX_TILES)
        def _():
            o_refs[1][...] = out
    else:
        o_refs[0][...] = out


def _ffn(x, mod_l, ng_l, w_in, w_out, *, layer, which, split_out=False):
    m0, n0 = (0, 0) if which == 0 else (6, 4)
    split_in = isinstance(x, tuple)
    tile = pl.BlockSpec((TM, D_MODEL), lambda i: (i, 0))
    ctx_tile = pl.BlockSpec((TM, D_MODEL), lambda i: (_ctx_tile(i), 0))
    lat_tile = pl.BlockSpec((TM, D_MODEL), lambda i: (_lat_tile(i), 0))
    if split_out:
        out_specs = [ctx_tile, lat_tile]
        out_shape = [jax.ShapeDtypeStruct((N_CTX, D_MODEL), F32), jax.ShapeDtypeStruct((N_LAT, D_MODEL), F32)]
    else:
        out_specs = tile
        out_shape = jax.ShapeDtypeStruct((N_ALL, D_MODEL), F32)
    return pl.pallas_call(
        functools.partial(_ffn_kernel, m0=m0, n0=n0, layer=layer, which=which, split_in=split_in,
                          split_out=split_out),
        grid=(N_ALL // TM,),
        in_specs=([ctx_tile, lat_tile] if split_in else [tile]) + [
            pl.BlockSpec((1, N_MOD, D_MODEL), lambda i: (_group_of_tile(i), 0, 0)),
            pl.BlockSpec((6, D_MODEL), lambda i: (0, 0)),
            pl.BlockSpec(memory_space=pl.ANY),
            pl.BlockSpec(memory_space=pl.ANY),
        ],
        out_specs=out_specs,
        out_shape=out_shape,
        scratch_shapes=[
            pltpu.VMEM((TM, D_MODEL), BF16),
            pltpu.VMEM((TM, D_MODEL), F32),
            pltpu.VMEM((TM, D_MODEL), F32),
            pltpu.VMEM((FFN_NC, D_MODEL, FFN_TC), BF16),
            pltpu.VMEM((FFN_NC, D_MODEL, FFN_TC), BF16),
            pltpu.VMEM((FFN_NC, FFN_TC, D_MODEL), BF16),
            pltpu.VMEM((FFN_STAGES, D_MODEL, FFN_TC), F32),
            pltpu.VMEM((FFN_STAGES, D_MODEL, FFN_TC), F32),
            pltpu.VMEM((FFN_STAGES, FFN_TC, D_MODEL), F32),
            pltpu.SemaphoreType.DMA((3, FFN_STAGES)),
        ],
        compiler_params=pltpu.CompilerParams(
            dimension_semantics=("arbitrary",), vmem_limit_bytes=VMEM_LIMIT),
        name="ffn",
    )(*(x if split_in else (x,)), mod_l, ng_l, w_in, w_out)


def _block_transpose_perm():
    r = lax.broadcasted_iota(jnp.int32, (GLA_T, GLA_T), 0)
    c = lax.broadcasted_iota(jnp.int32, (GLA_T, GLA_T), 1)
    return jnp.where(c == (r & (GLA_CB - 1)) * GLA_NB + (r >> 4), 1.0, 0.0).astype(BF16)


def _permute_chunks(perm, a):
    return jnp.concatenate([_dot(perm, a[c * GLA_T:(c + 1) * GLA_T, :]).astype(BF16)
                            for c in range(a.shape[0] // GLA_T)], axis=0)


_P_QK = 0
_P_VR = 2 * QK_W
_P_NAT = _P_VR + 2 * V_W
_P_LR = _P_NAT + 3 * NAT_W
_P_END = _P_LR + LANES


_NAT_SCALE = HD_NAT ** -0.5
assert _NAT_SCALE == 2.0 ** round(np.log2(_NAT_SCALE))
_GLA_COLS = 2 * QK_W + 2 * V_W
_REST_COLS = 2 * GLA_RANK + 3 * NAT_W
_REST_TILES = -(-_REST_COLS // LANES)
_PREP_ROWS = 128
assert _GLA_COLS % LANES == 0 and _GLA_COLS + _REST_COLS == 3104 and D_MODEL % _PREP_ROWS == 0


def _proj_prepare_weights(win_hbm, tail_ref, layer, w_bf, stage, sem):
    lane = lax.broadcasted_iota(jnp.int32, (_PREP_ROWS, LANES), 1)
    shift = 2 * GLA_RANK

    head = pltpu.make_async_copy(win_hbm.at[layer, :, pl.ds(0, _GLA_COLS)], stage.at[:, pl.ds(0, _GLA_COLS)],
                                 sem.at[0])
    head.start()
    head.wait()
    for r0 in range(0, D_MODEL, _PREP_ROWS):
        rows = slice(r0, r0 + _PREP_ROWS)
        w_bf[rows, _P_QK:_P_NAT] = stage[rows, 0:_GLA_COLS].astype(BF16)

    whole = (_REST_TILES - 1) * LANES
    stage[:, whole:] = tail_ref[0]
    rest = pltpu.make_async_copy(win_hbm.at[layer, :, pl.ds(_GLA_COLS, whole)], stage.at[:, pl.ds(0, whole)],
                                 sem.at[1])
    rest.start()
    rest.wait()
    for r0 in range(0, D_MODEL, _PREP_ROWS):
        rows = slice(r0, r0 + _PREP_ROWS)
        w_bf[rows, _P_LR:_P_END] = jnp.where(lane < shift, stage[rows, 0:LANES], 0.0).astype(BF16)
        for t in range(3 * NAT_W // LANES):
            lo = pltpu.roll(stage[rows, t * LANES:(t + 1) * LANES], LANES - shift, axis=1)
            hi = pltpu.roll(stage[rows, (t + 1) * LANES:(t + 2) * LANES], LANES - shift, axis=1)
            w_bf[rows, _P_NAT + t * LANES:_P_NAT + (t + 1) * LANES] = jnp.where(lane < LANES - shift, lo,
                                                                                  hi).astype(BF16)


def _rotary_partner(x):
    lane = lax.broadcasted_iota(jnp.int32, (x.shape[0], LANES), 1)
    first_of_pair = ((lane >> 4) & 1) == 0
    half = DK_GLA // 4
    tiles = []
    for t in range(x.shape[1] // LANES):
        tile = x[:, t * LANES:(t + 1) * LANES]
        tiles.append(jnp.where(first_of_pair, pltpu.roll(tile, LANES - half, axis=1), pltpu.roll(tile, half, axis=1)))
    return jnp.concatenate(tiles, axis=1)


def _proj_kernel(x_ref, mod_ref, ng_ref, win_hbm, tail_ref, wa_ref, ba_ref, cos_ref, sin_ref, *rest, layer):
    q_ref, k_ref, v_ref, r_ref, g_ref, qn_ref, kn_ctx_ref, vn_ctx_ref, kn_ref, vn_ref = rest[-13:-3]
    w_ref, stage, sem = rest[-3:]
    i = pl.program_id(0)

    @pl.when(i == 0)
    def _():
        _proj_prepare_weights(win_hbm, tail_ref, layer, w_ref, stage, sem)

    h = _rms(x_ref[...], ng_ref[2:3, :])
    h = (h * (1.0 + mod_ref[0, 4:5, :]) + mod_ref[0, 3:4, :]).astype(BF16)
    hp = _permute_chunks(_block_transpose_perm(), h)

    lr = _dot(hp, w_ref[:, _P_LR:_P_END]).astype(BF16)
    z = _dot(lr, wa_ref[...]) + ba_ref[...]
    g = (jnp.minimum(z, 0.0) - jnp.log1p(jnp.exp(-jnp.abs(z)))) * (1.0 / GATE_NORM)
    g_ref[0] = g[:, 0:QK_W]
    g_ref[1] = g[:, QK_W:2 * QK_W]

    qk = _dot(hp, w_ref[:, _P_QK:_P_VR])
    cos = cos_ref[...]
    sin = sin_ref[...]
    q_ref[...] = (qk[:, 0:QK_W] * cos + _rotary_partner(qk[:, 0:QK_W]) * sin) * (DK_GLA ** -0.5)
    k_ref[...] = qk[:, QK_W:2 * QK_W] * cos + _rotary_partner(qk[:, QK_W:2 * QK_W]) * sin

    vr = _dot(hp, w_ref[:, _P_VR:_P_NAT])
    v_ref[...] = vr[:, 0:V_W].astype(BF16)
    r_ref[...] = vr[:, V_W:2 * V_W].astype(BF16)

    nat = _dot(h, w_ref[:, _P_NAT:_P_LR])
    qn_ref[...] = (nat[:, 0:NAT_W] * _NAT_SCALE).astype(BF16)
    kn_ref[...] = nat[:, NAT_W:2 * NAT_W].astype(BF16)
    vn_ref[...] = nat[:, 2 * NAT_W:3 * NAT_W].astype(BF16)

    @pl.when(i < _CTX_TILES)
    def _():
        heads = (TM // SEQ, SEQ, H_NAT, HD_NAT)
        kn_ctx_ref[:, 0] = nat[:, NAT_W:2 * NAT_W].reshape(heads)
        vn_ctx_ref[:, 0] = nat[:, 2 * NAT_W:3 * NAT_W].reshape(heads)
        for later in range(1, kn_ctx_ref.shape[1]):
            kn_ctx_ref[:, later] = jnp.zeros(heads, F32)
            vn_ctx_ref[:, later] = jnp.zeros(heads, F32)


def _rope_table_block(i):
    lat_tiles = DEC_SEQ // TM
    return jnp.where(i < N_CTX // TM, lat_tiles, (i - N_CTX // TM) % lat_tiles)


def _project(x, mod_l, ng_l, w_in, w_a, b_a, cos_t, sin_t, new_cache, *, layer):
    tile = lambda w: pl.BlockSpec((TM, w), lambda i: (i, 0))
    if new_cache:
        ctx_heads = pl.BlockSpec((TM // SEQ, 1, SEQ, H_NAT, HD_NAT), lambda i: (_ctx_tile(i), layer, 0, 0, 0))
    else:
        ctx_heads = pl.BlockSpec((TM // SEQ, DEPTH, SEQ, H_NAT, HD_NAT), lambda i: (_ctx_tile(i), 0, 0, 0, 0))
    full = lambda a: pl.BlockSpec(a.shape, lambda i: (0,) * a.ndim)
    n_in = 9
    tail0 = _GLA_COLS + (_REST_TILES - 1) * LANES
    w_tail = jnp.pad(w_in[:, :, tail0:], ((0, 0), (0, 0), (0, LANES - (w_in.shape[2] - tail0))))
    return pl.pallas_call(
        functools.partial(_proj_kernel, layer=layer),
        grid=(N_ALL // TM,),
        in_specs=[
            tile(D_MODEL),
            pl.BlockSpec((1, N_MOD, D_MODEL), lambda i: (_group_of_tile(i), 0, 0)),
            full(ng_l),
            pl.BlockSpec(memory_space=pl.ANY),
            pl.BlockSpec((1, D_MODEL, LANES), lambda i: (layer, 0, 0)),
            full(w_a), full(b_a),
            pl.BlockSpec((TM, QK_W), lambda i: (_rope_table_block(i), 0)),
            pl.BlockSpec((TM, QK_W), lambda i: (_rope_table_block(i), 0)),
        ] + [pl.BlockSpec(memory_space=pl.ANY)] * len(new_cache),
        input_output_aliases={n_in + n: 6 + n for n in range(len(new_cache))},
        out_specs=[
            tile(QK_W), tile(QK_W), tile(V_W), tile(V_W),
            pl.BlockSpec((2, TM, QK_W), lambda i: (0, i, 0)),
            tile(NAT_W), ctx_heads, ctx_heads, tile(NAT_W), tile(NAT_W),
        ],
        out_shape=[
            jax.ShapeDtypeStruct((N_ALL, QK_W), F32), jax.ShapeDtypeStruct((N_ALL, QK_W), F32),
            jax.ShapeDtypeStruct((N_ALL, V_W), BF16), jax.ShapeDtypeStruct((N_ALL, V_W), BF16),
            jax.ShapeDtypeStruct((2, N_ALL, QK_W), F32),
            jax.ShapeDtypeStruct((N_ALL, NAT_W), BF16),
            jax.ShapeDtypeStruct((BATCH, DEPTH, SEQ, H_NAT, HD_NAT), F32),
            jax.ShapeDtypeStruct((BATCH, DEPTH, SEQ, H_NAT, HD_NAT), F32),
            jax.ShapeDtypeStruct((N_ALL, NAT_W), BF16), jax.ShapeDtypeStruct((N_ALL, NAT_W), BF16),
        ],
        scratch_shapes=[
            pltpu.VMEM((D_MODEL, _P_END), BF16),
            pltpu.VMEM((D_MODEL, _REST_TILES * LANES), F32),
            pltpu.SemaphoreType.DMA((2,)),
        ],
        compiler_params=pltpu.CompilerParams(
            dimension_semantics=("arbitrary",), vmem_limit_bytes=VMEM_LIMIT),
        name="mixer_proj",
    )(x, mod_l, ng_l, w_in, w_tail, w_a, b_a, cos_t, sin_t, *new_cache)


def _gla_tables():
    rows = []
    seq_specs = [(b * (SEQ // GLA_T), SEQ // GLA_T, 0, 0, b, 1) for b in range(BATCH)]
    seq_specs += [(N_CTX // GLA_T + b * (DEC_SEQ // GLA_T), DEC_SEQ // GLA_T, b, 1, BATCH - 1, 0)
                  for b in range(DEC_BATCH)]
    for blk0, nchunk, s0_row, has_s0, fin_row, wants_fin in seq_specs:
        states = (s0_row, has_s0, fin_row, wants_fin)
        if nchunk == 1:
            rows.append((blk0, _GLA_BOTH, 1, blk0, 0) + states)
            continue
        for direction in (0, 1):
            order = range(nchunk) if direction == 0 else range(nchunk - 1, -1, -1)
            for n, c in enumerate(order):
                out_blk = blk0 + (c if direction == 1 else nchunk - 1)
                rows.append((blk0 + c, direction, int(n == 0), out_blk, c) + states)
    return np.asarray(rows, dtype=np.int32).T.copy()


_GLA_BOTH = 2


_GLA_TAB = _gla_tables()
_GLA_ITEMS = _GLA_TAB.shape[1]
_GLA_PAIR_ROWS = GLA_NB * GLA_CB * (GLA_CB + 1) // 2


def _gla_item(direction, q_ref, k_ref, v_ref, g_ref, res, s_scr, cp, sstk, p_scr, w_scr):
    T, CB, NB = GLA_T, GLA_CB, GLA_NB
    fwd = direction == 0
    slab = lambda i: slice(i * NB, (i + 1) * NB)

    cum = None
    for i in (range(CB) if fwd else range(CB - 1, -1, -1)):
        gi = g_ref[slab(i), :]
        cum = gi if cum is None else cum + gi
        cp[slab(i), :] = cum
    total = cum
    cum_all = cp[...]
    qt = q_ref[...] * jnp.exp(cum_all)
    kh = k_ref[...] * jnp.exp(jnp.concatenate([total] * CB, axis=0) - cum_all)
    dec_t = jnp.concatenate([jnp.exp(total), jnp.zeros((LANES - NB, QK_W), F32)], axis=0).T

    erow = lax.broadcasted_iota(jnp.int32, (QK_W, QK_W), 0)
    ecol = lax.broadcasted_iota(jnp.int32, (QK_W, QK_W), 1)
    head_sum = jnp.where((erow >> 6) == (ecol >> 6), 1.0, 0.0).astype(BF16)

    key_positions = lambda i: range(i + 1) if fwd else range(i, CB)
    r0 = 0
    for i in range(CB):
        qi = q_ref[slab(i), :]
        ci = cp[slab(i), :]
        for j in key_positions(i):
            e = jnp.exp(ci - cp[slab(j), :])
            p_scr[r0:r0 + NB, :] = (qi * k_ref[slab(j), :] * e).astype(BF16)
            r0 += NB
    w_scr[...] = _dot(p_scr[...], head_sum)
    first_half = lax.broadcasted_iota(jnp.int32, (NB, LANES), 1) < DK_GLA
    r0 = 0
    for i in range(CB):
        acc = None
        for j in key_positions(i):
            spread = []
            for pair in range(H_GLA // 2):
                tile = w_scr[r0:r0 + NB, pair * LANES:(pair + 1) * LANES]
                other = pltpu.roll(tile, DK_GLA, axis=1)
                spread += [jnp.where(first_half, tile, other), jnp.where(first_half, other, tile)]
            term = jnp.concatenate(spread, axis=1) * v_ref[slab(j), :].astype(F32)
            acc = term if acc is None else acc + term
            r0 += NB
        res[slab(i), :] = acc

    kht = kh.T.astype(BF16)
    v_bf = v_ref[...]
    key_blk = lax.broadcasted_iota(jnp.int32, (DK_GLA, T), 1) & (NB - 1)
    row_blk = lax.broadcasted_iota(jnp.int32, (T, LANES), 0) & (NB - 1)
    lane_half = lax.broadcasted_iota(jnp.int32, (T, LANES), 1) >> 6
    order = range(NB) if fwd else range(NB - 1, -1, -1)
    for h in range(H_GLA):
        kh_h = kht[h * DK_GLA:(h + 1) * DK_GLA, :]
        kv = _dot(jnp.concatenate([jnp.where(key_blk == b, kh_h, 0) for b in range(NB)], axis=0),
                  v_bf[:, h * DV_GLA:(h + 1) * DV_GLA])
        s = s_scr[h]
        for b in order:
            sstk[h, b * DK_GLA:(b + 1) * DK_GLA, :] = s.astype(BF16)
            s = dec_t[h * DK_GLA:(h + 1) * DK_GLA, b:b + 1] * s + kv[b * DK_GLA:(b + 1) * DK_GLA, :]
        s_scr[h] = s

        pair_tile = qt[:, (h // 2) * LANES:(h // 2 + 1) * LANES]
        both = jnp.where(lane_half == h % 2, pair_tile, pltpu.roll(pair_tile, DK_GLA, axis=1))
        lhs = jnp.concatenate([jnp.where(row_blk == 2 * j + lane_half, both, 0.0).astype(BF16)
                               for j in range(NB // 2)], axis=1)
        res[:, h * DV_GLA:(h + 1) * DV_GLA] += _dot(lhs, sstk[h])


def _gla_kernel(tab_ref, q_ref, k_ref, v_ref, g_ref, s0_ref, *rest):
    o_ref, so_ref, s_scr, cp, sstk, p_scr, w_scr, res, held = rest[-9:]
    it = pl.program_id(0)
    kind = tab_ref[1, it]
    slot = tab_ref[4, it]
    has_s0 = tab_ref[6, it] == 1
    wants_final = tab_ref[8, it] == 1

    def run(d):
        _gla_item(d, q_ref, k_ref, v_ref, g_ref.at[d], res.at[d], s_scr.at[d], cp.at[d], sstk.at[d],
                  p_scr.at[d], w_scr.at[d])

    def clear_later_layers():
        for later in range(1, so_ref.shape[1]):
            so_ref[0, later] = jnp.zeros(so_ref.shape[2:], F32)

    @pl.when(kind == _GLA_BOTH)
    def _():
        s_scr[...] = jnp.where(has_s0, s0_ref[0, 0], 0.0)
        run(0)
        run(1)
        o_ref[...] = res[0] + res[1]

        @pl.when(wants_final)
        def _():
            so_ref[0, 0] = s_scr[...]
            clear_later_layers()

    for d in (0, 1):
        @pl.when(kind == d)
        def _(d=d):
            @pl.when(tab_ref[2, it] == 1)
            def _():
                s_scr[d] = jnp.where(has_s0, s0_ref[0, 0, d], 0.0)

            run(d)
            if d == 0:
                held[slot] = res[0]
            else:
                o_ref[...] = held[slot] + res[1]

            @pl.when(wants_final)
            def _():
                so_ref[0, 0, d] = s_scr[d]
                if d == 0:
                    clear_later_layers()


def _gla(q, k, v, g, state_gla, new_state, *, layer):
    tok = lambda w: pl.BlockSpec((GLA_T, w), lambda it, tab: (tab[0, it], 0))
    state_blk = (2, H_GLA, DK_GLA, DV_GLA)
    if new_state:
        final_spec = pl.BlockSpec((1, 1) + state_blk, lambda it, tab: (tab[7, it], layer, 0, 0, 0, 0))
    else:
        final_spec = pl.BlockSpec((1, DEPTH) + state_blk, lambda it, tab: (tab[7, it], 0, 0, 0, 0, 0))
    grid_spec = pltpu.PrefetchScalarGridSpec(
        num_scalar_prefetch=1,
        grid=(_GLA_ITEMS,),
        in_specs=[
            tok(QK_W), tok(QK_W), tok(V_W),
            pl.BlockSpec((2, GLA_T, QK_W), lambda it, tab: (0, tab[0, it], 0)),
            pl.BlockSpec((1, 1) + state_blk, lambda it, tab: (tab[5, it], layer, 0, 0, 0, 0)),
        ] + [pl.BlockSpec(memory_space=pl.ANY)] * len(new_state),
        out_specs=[
            pl.BlockSpec((GLA_T, V_W), lambda it, tab: (tab[3, it], 0)),
            final_spec,
        ],
        scratch_shapes=[
            pltpu.VMEM((2, H_GLA, DK_GLA, DV_GLA), F32),
            pltpu.VMEM((2, GLA_T, QK_W), F32),
            pltpu.VMEM((2, H_GLA, GLA_NB * DK_GLA, DV_GLA), BF16),
            pltpu.VMEM((2, _GLA_PAIR_ROWS, QK_W), BF16),
            pltpu.VMEM((2, _GLA_PAIR_ROWS, QK_W), F32),
            pltpu.VMEM((2, GLA_T, V_W), F32),
            pltpu.VMEM((DEC_SEQ // GLA_T, GLA_T, V_W), F32),
        ],
    )
    return pl.pallas_call(
        _gla_kernel,
        grid_spec=grid_spec,
        out_shape=[
            jax.ShapeDtypeStruct((N_ALL, V_W), F32),
            jax.ShapeDtypeStruct((BATCH, DEPTH) + state_blk, F32),
        ],
        input_output_aliases={6 + n: 1 + n for n in range(len(new_state))},
        compiler_params=pltpu.CompilerParams(
            dimension_semantics=("arbitrary",), vmem_limit_bytes=VMEM_LIMIT),
        name="gla",
    )(jnp.asarray(_GLA_TAB), q, k, v, g, state_gla, *new_state)


def _with_ones(v):
    return jnp.concatenate([v, jnp.ones_like(v)], axis=1)


def _softmax_pv(s_list, v1_list):
    m = s_list[0].max(axis=-1, keepdims=True)
    for s in s_list[1:]:
        m = jnp.maximum(m, s.max(axis=-1, keepdims=True))
    acc = None
    for s, v1 in zip(s_list, v1_list):
        pv = _dot(jnp.exp(s - m).astype(BF16), v1)
        acc = pv if acc is None else acc + pv
    width = acc.shape[1] // 2
    return acc[:, :width] / acc[:, width:]


def _ctx_attn_kernel(q_ref, k_ref, v_ref, o_ref):
    lane =lax.broadcasted_iota(jnp.int32, (SEQ, LANES), 1)
    for t in range(NAT_W // LANES):
        sl = slice(t * LANES, (t + 1) * LANES)
        qt = q_ref[:, sl]
        kt = k_ref[:, sl].astype(BF16)
        vt = _with_ones(v_ref[:, sl].astype(BF16))
        out = jnp.zeros((SEQ, LANES), F32)
        for half in range(LANES // HD_NAT):
            mine = (lane >> 6) == half
            s = _dot_nt(jnp.where(mine, qt, 0).astype(BF16), kt)
            out = jnp.where(mine, _softmax_pv([s], [vt]), out)
        o_ref[:, sl] = out.astype(o_ref.dtype)


def _ctx_attention(qn, kn, vn):
    spec = pl.BlockSpec((SEQ, NAT_W), lambda b: (b, 0))
    return pl.pallas_call(
        _ctx_attn_kernel,
        grid=(BATCH,),
        in_specs=[spec, spec, spec],
        out_specs=spec,
        out_shape=jax.ShapeDtypeStruct((N_CTX, NAT_W), BF16),
        compiler_params=pltpu.CompilerParams(
            dimension_semantics=("arbitrary",), vmem_limit_bytes=VMEM_LIMIT),
        name="ctx_attention",
    )(qn, kn, vn)


_NAT_QROWS = 4
_NAT_GROUPS = GRID_ROWS // _NAT_QROWS
_NAT_KROWS = WIN_H + _NAT_QROWS
_NAT_Q = _NAT_QROWS * GRID_W
_NAT_KEYS = _NAT_KROWS * GRID_W
_NAT_DR = 2 * WIN_H - 1
_NAT_DC = 2 * WIN_W - 1


def _nat_key_row0(r):
    return jnp.clip(r - WIN_H // 2, 0, GRID_ROWS - WIN_H)


def _nat_build_bias(rpb_ref, tz_scr):
    c = lax.broadcasted_iota(jnp.int32, (GRID_W, LANES), 0)
    lane = lax.broadcasted_iota(jnp.int32, (GRID_W, LANES), 1)
    kc = lane & (GRID_W - 1)
    second = (lane >> 6) == 1
    win_start = jnp.clip(c - WIN_W // 2, 0, GRID_W - WIN_W)
    valid = (kc >= win_start) & (kc < win_start + WIN_W)

    def one_row(n, carry):
        dr = n >> 3
        h = n & (H_NAT - 1)
        src = jnp.broadcast_to(rpb_ref[pl.ds(h * _NAT_DR + dr, 1), :], (GRID_W, LANES))
        rolled = pltpu.roll(src, LANES - (WIN_W - 1), axis=1, stride=1, stride_axis=0)
        tz_scr[dr, h] = jnp.where(valid, rolled, NEG_INF)
        return carry

    lax.fori_loop(0, _NAT_DR * H_NAT, one_row, 0)

    def pair_rows(n, carry):
        dr = n >> 3
        h = n & (H_NAT - 1)
        tz_scr[dr, h] = jnp.where(second, tz_scr[dr + 1, h], tz_scr[dr, h])
        return carry

    lax.fori_loop(0, (_NAT_DR - 1) * H_NAT, pair_rows, 0)


def _nat_kernel(rpb_ref, q_ref, k_ref, v_ref, ck_ref, cv_ref, o_ref, tz_scr, bias_scr):
    grp = pl.program_id(1)

    @pl.when((pl.program_id(0) == 0) & (grp == 0))
    def _():
        _nat_build_bias(rpb_ref, tz_scr)

    krow0 = jnp.clip(_NAT_QROWS * grp - WIN_H // 2, 0, GRID_ROWS - _NAT_KROWS)
    k0 = pl.multiple_of(krow0 * GRID_W, GRID_W)
    lane_q = lax.broadcasted_iota(jnp.int32, (_NAT_Q, LANES), 1)
    lane_b = lax.broadcasted_iota(jnp.int32, (GRID_W, LANES), 1)
    neg = jnp.full((GRID_W, LANES), NEG_INF, F32)
    for t in range(NAT_W // LANES):
        sl = slice(t * LANES, (t + 1) * LANES)
        qt = q_ref[:, sl]
        kw = k_ref[pl.ds(k0, _NAT_KEYS), sl].astype(BF16)
        vw = _with_ones(v_ref[pl.ds(k0, _NAT_KEYS), sl].astype(BF16))
        ck = ck_ref[0, 0, :, sl].astype(BF16)
        cv = _with_ones(cv_ref[0, 0, :, sl].astype(BF16))
        out = jnp.zeros((_NAT_Q, LANES), F32)
        for half in range(LANES // HD_NAT):
            h = 2 * t + half
            for qr in range(_NAT_QROWS):
                r = _NAT_QROWS * grp + qr
                lo = _nat_key_row0(r)
                for kp in range(_NAT_KROWS // 2):
                    kr = krow0 + 2 * kp
                    tile = tz_scr[jnp.clip(kr - r + (WIN_H - 1), 0, _NAT_DR - 1), h]
                    ok_a = ((kr >= lo) & (kr < lo + WIN_H)).astype(jnp.int32)
                    ok_b = ((kr + 1 >= lo) & (kr + 1 < lo + WIN_H)).astype(jnp.int32)
                    ok = jnp.where(lane_b < GRID_W, ok_a, ok_b) == 1
                    bias_scr[qr * GRID_W:(qr + 1) * GRID_W, kp * LANES:(kp + 1) * LANES] = jnp.where(ok, tile, neg)
            mine = (lane_q >> 6) == half
            qm = jnp.where(mine, qt, 0).astype(BF16)
            s_win = _dot_nt(qm, kw) + bias_scr[...]
            s_ctx = _dot_nt(qm, ck)
            out = jnp.where(mine, _softmax_pv([s_win, s_ctx], [vw, cv]), out)
        o_ref[:, sl] = out.astype(o_ref.dtype)


def _nat_attention(qn, kn, vn, cache_k, cache_v, rpb_l, layer):
    lat0 = N_CTX // _NAT_Q
    half = jnp.pad(rpb_l.reshape(H_NAT * _NAT_DR, _NAT_DC), ((0, 0), (0, GRID_W - _NAT_DC)))
    rpb_rows = jnp.concatenate([half, half], axis=1)
    return pl.pallas_call(
        _nat_kernel,
        grid=(DEC_BATCH, _NAT_GROUPS),
        in_specs=[
            pl.BlockSpec((H_NAT * _NAT_DR, LANES), lambda b, g: (0, 0)),
            pl.BlockSpec((_NAT_Q, NAT_W), lambda b, g: (lat0 + b * _NAT_GROUPS + g, 0)),
            pl.BlockSpec((DEC_SEQ, NAT_W), lambda b, g: (N_CTX // DEC_SEQ + b, 0)),
            pl.BlockSpec((DEC_SEQ, NAT_W), lambda b, g: (N_CTX // DEC_SEQ + b, 0)),
            pl.BlockSpec((1, 1, PAST_LEN, NAT_W), lambda b, g: (b, layer, 0, 0)),
            pl.BlockSpec((1, 1, PAST_LEN, NAT_W), lambda b, g: (b, layer, 0, 0)),
        ],
        out_specs=pl.BlockSpec((_NAT_Q, NAT_W), lambda b, g: (b * _NAT_GROUPS + g, 0)),
        out_shape=jax.ShapeDtypeStruct((N_LAT, NAT_W), BF16),
        scratch_shapes=[
            pltpu.VMEM((_NAT_DR, H_NAT, GRID_W, LANES), F32),
            pltpu.VMEM((_NAT_Q, _NAT_KEYS), F32),
        ],
        compiler_params=pltpu.CompilerParams(
            dimension_semantics=("arbitrary", "arbitrary"), vmem_limit_bytes=VMEM_LIMIT),
        name="nat_attention",
    )(rpb_rows, qn, kn, vn, cache_k, cache_v)


def _out_kernel(x_ref, o_ref, r_ref, on_ctx_ref, on_lat_ref, mod_ref, ng_ref, gng_ref, w32_ref, y_ref, w_ref):
    @pl.when(pl.program_id(0) == 0)
    def _():
        w_ref[...] = w32_ref[0].astype(BF16)

    o_nat = _read_split(pl.program_id(0), on_ctx_ref, on_lat_ref)
    og = o_ref[...]
    parts = []
    for h in range(H_GLA):
        parts.append(_rms(og[:, h * DV_GLA:(h + 1) * DV_GLA], gng_ref[...]))
    merged = (jnp.concatenate(parts, axis=1) * _silu(r_ref[...].astype(F32))).astype(BF16)
    merged = _permute_chunks(_block_transpose_perm(), merged)
    y = _dot(merged, w_ref[0:V_W, :]) + _dot(o_nat.astype(BF16), w_ref[V_W:, :])
    y_ref[...] = x_ref[...] + mod_ref[0, 5:6, :] * _rms(y, ng_ref[3:4, :])


def _mixer_out(x, o_gla, r, o_ctx, o_lat, mod_l, ng_l, gng_l, w_out, *, layer):
    tile = lambda w: pl.BlockSpec((TM, w), lambda i: (i, 0))
    full = lambda a: pl.BlockSpec(a.shape, lambda i: (0,) * a.ndim)
    return pl.pallas_call(
        _out_kernel,
        grid=(N_ALL // TM,),
        in_specs=[
            tile(D_MODEL),
            tile(V_W),
            tile(V_W),
            pl.BlockSpec((TM, NAT_W), lambda i: (_ctx_tile(i), 0)),
            pl.BlockSpec((TM, NAT_W), lambda i: (_lat_tile(i), 0)),
            pl.BlockSpec((1, N_MOD, D_MODEL), lambda i: (_group_of_tile(i), 0, 0)),
            full(ng_l), full(gng_l),
            pl.BlockSpec((1,) + w_out.shape[1:], lambda i: (layer, 0, 0)),
        ],
        out_specs=tile(D_MODEL),
        out_shape=jax.ShapeDtypeStruct((N_ALL, D_MODEL), F32),
        scratch_shapes=[pltpu.VMEM(w_out.shape[1:], BF16)],
        compiler_params=pltpu.CompilerParams(
            dimension_semantics=("arbitrary",), vmem_limit_bytes=VMEM_LIMIT),
        name="mixer_out",
    )(x, o_gla, r, o_ctx, o_lat, mod_l, ng_l, gng_l, w_out)


def _rope_tables():
    quarter = DK_GLA // 4
    freqs = ROPE_BASE ** (-jnp.arange(quarter, dtype=F32) / quarter)
    t = jnp.arange(DEC_SEQ)
    ang_r = (t // GRID_W).astype(F32)[:, None] * freqs[None, :]
    ang_c = (t % GRID_W).astype(F32)[:, None] * freqs[None, :]
    cos_h = jnp.concatenate([jnp.cos(ang_r), jnp.cos(ang_r), jnp.cos(ang_c), jnp.cos(ang_c)], axis=1)
    sin_h = jnp.concatenate([-jnp.sin(ang_r), jnp.sin(ang_r), -jnp.sin(ang_c), jnp.sin(ang_c)], axis=1)
    pos_major = lambda a: a.reshape(-1, GLA_NB, GLA_CB, QK_W).transpose(0, 2, 1, 3).reshape(a.shape)
    cos_t = jnp.concatenate([pos_major(jnp.tile(cos_h, (1, H_GLA))), jnp.ones((TM, QK_W), F32)], axis=0)
    sin_t = jnp.concatenate([pos_major(jnp.tile(sin_h, (1, H_GLA))), jnp.zeros((TM, QK_W), F32)], axis=0)
    return cos_t, sin_t


def _gate_up_weights(gla_wa2_l, gla_ba_l):
    w_a = jnp.zeros((LANES, 2 * QK_W), F32)
    w_a = w_a.at[0:GLA_RANK, 0:QK_W].set(gla_wa2_l[0]).at[GLA_RANK:2 * GLA_RANK, QK_W:].set(gla_wa2_l[1])
    b_a = jnp.concatenate([gla_ba_l[0], gla_ba_l[1]])[None, :]
    return w_a.astype(BF16), b_a


def kernel(x_prompt, x_sample, cache_k, cache_v, state_gla, c, c_ctx, w_mod, b_mod, norm_g, ffn_w_in, ffn_w_out,
           w_in, gla_wa2, gla_ba, gla_norm_g, nat_rpb, w_out):
    cvecs = jnp.zeros((SUBLANES, D_MODEL), F32).at[0].set(c_ctx).at[1:1 + DEC_BATCH].set(c)
    mod = _modulation(cvecs, w_mod, b_mod)[:, :N_GROUPS].reshape(DEPTH, N_GROUPS, N_MOD, D_MODEL)

    cos_t, sin_t = _rope_tables()
    ck = cache_k.reshape(DEC_BATCH, DEPTH, PAST_LEN, NAT_W)
    cv = cache_v.reshape(DEC_BATCH, DEPTH, PAST_LEN, NAT_W)

    x = (x_prompt.reshape(N_CTX, D_MODEL), x_sample.reshape(N_LAT, D_MODEL))
    new_cache, new_state = (), ()
    for l in range(DEPTH):
        mod_l, ng_l = mod[l], norm_g[l]
        x = _ffn(x, mod_l, ng_l, ffn_w_in, ffn_w_out, layer=l, which=0)

        w_a, b_a = _gate_up_weights(gla_wa2[l], gla_ba[l])
        q, k, v, r, g, qn, new_k, new_v, kn, vn = _project(
            x, mod_l, ng_l, w_in, w_a, b_a, cos_t, sin_t, new_cache, layer=l)
        new_cache = (new_k, new_v)

        o_gla, new_s = _gla(q, k, v, g, state_gla, new_state, layer=l)
        new_state = (new_s,)
        o_ctx = _ctx_attention(qn, kn, vn)
        o_lat = _nat_attention(qn, kn, vn, ck, cv, nat_rpb[l], l)

        x = _mixer_out(x, o_gla, r, o_ctx, o_lat, mod_l, ng_l, gla_norm_g[l][None, :], w_out, layer=l)
        x = _ffn(x, mod_l, ng_l, ffn_w_in, ffn_w_out, layer=l, which=1, split_out=(l == DEPTH - 1))

    y_prompt = x[0].reshape(BATCH, SEQ, D_MODEL)
    y_sample = x[1].reshape(DEC_BATCH, DEC_SEQ, D_MODEL)
    return (y_prompt, y_sample, new_cache[0], new_cache[1], new_state[0])
```

```python
import functools

import numpy as np
import jax
import jax.numpy as jnp
from jax import lax
from jax.experimental import pallas as pl
from jax.experimental.pallas import tpu as pltpu

D_MODEL = 1024
BATCH = 16
SEQ = 256
DEPTH = 2
DEC_BATCH = 2
DEC_SEQ = 1024
PAST_LEN = 512
GRID_W = 64
H_GLA = 4
DK_GLA = 64
DV_GLA = 128
GLA_RANK = 16
GATE_NORM = 16.0
H_NAT = 8
HD_NAT = 64
WIN_H = 8
WIN_W = 16
D_FF = 2816
N_MOD = 9
ROPE_BASE = 10000.0
EPS = 1e-6
NEG_INF = -1e30

F32 = jnp.float32
BF16 = jnp.bfloat16

N_CTX = BATCH * SEQ
N_LAT = DEC_BATCH * DEC_SEQ
N_ALL = N_CTX + N_LAT
N_GROUPS = 1 + DEC_BATCH
QK_W = H_GLA * DK_GLA
V_W = H_GLA * DV_GLA
NAT_W = H_NAT * HD_NAT
GRID_ROWS = DEC_SEQ // GRID_W

LANES = 128
SUBLANES = 8

TM = 512
FFN_TC = 256
FFN_NC = D_FF // FFN_TC
FFN_STAGES = 2
MOD_TN = 1152
GLA_T = 256
GLA_CB = 16
GLA_NB = GLA_T // GLA_CB
VMEM_LIMIT = 56 * 1024 * 1024

assert N_CTX % TM == 0 and DEC_SEQ % TM == 0 and D_FF % FFN_TC == 0 and FFN_TC % LANES == 0
assert SEQ == GLA_T and DEC_SEQ % GLA_T == 0


def _group_of_tile(i):
    return jnp.where(i < N_CTX // TM, 0, 1 + (i - N_CTX // TM) // (DEC_SEQ // TM))


def _dot(a, b):
    return jnp.dot(a, b, preferred_element_type=F32)


def _dot_nt(a, b):
    return lax.dot_general(a, b, (((1,), (1,)), ((), ())), preferred_element_type=F32)


def _dot_tn(a, b):
    return lax.dot_general(a, b, (((0,), (0,)), ((), ())), preferred_element_type=F32)


def _rms(x, g):
    ms = jnp.mean(x * x, axis=-1, keepdims=True)
    return x * lax.rsqrt(ms + EPS) * g


def _silu(x):
    return x * jax.nn.sigmoid(x)


def _mod_kernel(c_ref, w_ref, b_ref, o_ref):
    s = _silu(c_ref[...]).astype(BF16)
    o_ref[0] = _dot(s, w_ref[0].astype(BF16)) + b_ref[0]


def _modulation(cvecs, w_mod, b_mod):
    n_out = N_MOD * D_MODEL
    return pl.pallas_call(
        _mod_kernel,
        grid=(DEPTH, n_out // MOD_TN),
        in_specs=[
            pl.BlockSpec((SUBLANES, D_MODEL), lambda l, j: (0, 0)),
            pl.BlockSpec((1, D_MODEL, MOD_TN), lambda l, j: (l, 0, j)),
            pl.BlockSpec((1, 1, MOD_TN), lambda l, j: (l, 0, j)),
        ],
        out_specs=pl.BlockSpec((1, SUBLANES, MOD_TN), lambda l, j: (l, 0, j)),
        out_shape=jax.ShapeDtypeStruct((DEPTH, SUBLANES, n_out), F32),
        compiler_params=pltpu.CompilerParams(
            dimension_semantics=("arbitrary", "arbitrary"), vmem_limit_bytes=VMEM_LIMIT),
        name="modulation",
    )(cvecs, w_mod, b_mod.reshape(DEPTH, 1, n_out))


_CTX_TILES = N_CTX // TM
_LAT_TILES = N_LAT // TM


def _ctx_tile(i):
    return jnp.minimum(i, _CTX_TILES - 1)


def _lat_tile(i):
    return jnp.maximum(i - _CTX_TILES, 0)


def _read_split(i, ctx_ref, lat_ref):
    return jnp.where(i < _CTX_TILES, ctx_ref[...], lat_ref[...])


def _ffn_kernel(*refs, m0, n0, layer, which, split_in, split_out):
    n_x = 2 if split_in else 1
    x_refs, (mod_ref, ng_ref, win_hbm, wout_hbm) = refs[:n_x], refs[n_x:n_x + 4]
    n_o = 2 if split_out else 1
    o_refs = refs[n_x + 4:n_x + 4 + n_o]
    h_scr, acc_scr, x_scr, wg_bf, wu_bf, wo_bf, stage_g, stage_u, stage_o, sem = refs[n_x + 4 + n_o:]
    i = pl.program_id(0)

    def chunk_copies(c, slot):
        cols = pl.ds(c * FFN_TC, FFN_TC)
        up_cols = pl.ds(D_FF + c * FFN_TC, FFN_TC)
        return (
            pltpu.make_async_copy(win_hbm.at[layer, which, :, cols], stage_g.at[slot], sem.at[0, slot]),
            pltpu.make_async_copy(win_hbm.at[layer, which, :, up_cols], stage_u.at[slot], sem.at[1, slot]),
            pltpu.make_async_copy(wout_hbm.at[layer, which, cols, :], stage_o.at[slot], sem.at[2, slot]),
        )

    def accumulate(c):
        h = h_scr[...]
        gate = _dot(h, wg_bf[c])
        up = _dot(h, wu_bf[c])
        part = _dot((_silu(gate) * up).astype(BF16), wo_bf[c])
        if c == 0:
            acc_scr[...] = part
        else:
            acc_scr[...] += part

    x = _read_split(i, *x_refs) if split_in else x_refs[0][...]
    x_scr[...] = x
    gain = ng_ref[n0:n0 + 1, :] * (1.0 + mod_ref[0, m0 + 1:m0 + 2, :])
    h_scr[...] = (_rms(x, gain) + mod_ref[0, m0:m0 + 1, :]).astype(BF16)

    @pl.when(i == 0)
    def _():
        for c in range(min(FFN_STAGES, FFN_NC)):
            for cp in chunk_copies(c, c):
                cp.start()
        for c in range(FFN_NC):
            slot = c % FFN_STAGES
            for cp in chunk_copies(c, slot):
                cp.wait()
            wg_bf[c] = stage_g[slot].astype(BF16)
            wu_bf[c] = stage_u[slot].astype(BF16)
            wo_bf[c] = stage_o[slot].astype(BF16)
            if c + FFN_STAGES < FFN_NC:
                for cp in chunk_copies(c + FFN_STAGES, slot):
                    cp.start()
            accumulate(c)

    @pl.when(i > 0)
    def _():
        for c in range(FFN_NC):
            accumulate(c)

    gain = 0.5 * mod_ref[0, m0 + 2:m0 + 3, :] * ng_ref[n0 + 1:n0 + 2, :]
    out = x_scr[...] + _rms(acc_scr[...], gain)
    if split_out:
        @pl.when(i < _CTX_TILES)
        def _():
            o_refs[0][...] = out

        @pl.when(i >= _CTX_TILES)
        def _():
            o_refs[1][...] = out
    else:
        o_refs[0][...] = out


def _ffn(x, mod_l, ng_l, w_in, w_out, *, layer, which, split_out=False):
    m0, n0 = (0, 0) if which == 0 else (6, 4)
    split_in = isinstance(x, tuple)
    tile = pl.BlockSpec((TM, D_MODEL), lambda i: (i, 0))
    ctx_tile = pl.BlockSpec((TM, D_MODEL), lambda i: (_ctx_tile(i), 0))
    lat_tile = pl.BlockSpec((TM, D_MODEL), lambda i: (_lat_tile(i), 0))
    if split_out:
        out_specs = [ctx_tile, lat_tile]
        out_shape = [jax.ShapeDtypeStruct((N_CTX, D_MODEL), F32), jax.ShapeDtypeStruct((N_LAT, D_MODEL), F32)]
    else:
        out_specs = tile
        out_shape = jax.ShapeDtypeStruct((N_ALL, D_MODEL), F32)
    return pl.pallas_call(
        functools.partial(_ffn_kernel, m0=m0, n0=n0, layer=layer, which=which, split_in=split_in,
                          split_out=split_out),
        grid=(N_ALL // TM,),
        in_specs=([ctx_tile, lat_tile] if split_in else [tile]) + [
            pl.BlockSpec((1, N_MOD, D_MODEL), lambda i: (_group_of_tile(i), 0, 0)),
            pl.BlockSpec((6, D_MODEL), lambda i: (0, 0)),
            pl.BlockSpec(memory_space=pl.ANY),
            pl.BlockSpec(memory_space=pl.ANY),
        ],
        out_specs=out_specs,
        out_shape=out_shape,
        scratch_shapes=[
            pltpu.VMEM((TM, D_MODEL), BF16),
            pltpu.VMEM((TM, D_MODEL), F32),
            pltpu.VMEM((TM, D_MODEL), F32),
            pltpu.VMEM((FFN_NC, D_MODEL, FFN_TC), BF16),
            pltpu.VMEM((FFN_NC, D_MODEL, FFN_TC), BF16),
            pltpu.VMEM((FFN_NC, FFN_TC, D_MODEL), BF16),
            pltpu.VMEM((FFN_STAGES, D_MODEL, FFN_TC), F32),
            pltpu.VMEM((FFN_STAGES, D_MODEL, FFN_TC), F32),
            pltpu.VMEM((FFN_STAGES, FFN_TC, D_MODEL), F32),
            pltpu.SemaphoreType.DMA((3, FFN_STAGES)),
        ],
        compiler_params=pltpu.CompilerParams(
            dimension_semantics=("arbitrary",), vmem_limit_bytes=VMEM_LIMIT),
        name="ffn",
    )(*(x if split_in else (x,)), mod_l, ng_l, w_in, w_out)


def _block_transpose_perm():
    r = lax.broadcasted_iota(jnp.int32, (GLA_T, GLA_T), 0)
    c = lax.broadcasted_iota(jnp.int32, (GLA_T, GLA_T), 1)
    return jnp.where(c == (r & (GLA_CB - 1)) * GLA_NB + (r >> 4), 1.0, 0.0).astype(BF16)


def _permute_chunks(perm, a):
    return jnp.concatenate([_dot(perm, a[c * GLA_T:(c + 1) * GLA_T, :]).astype(BF16)
                            for c in range(a.shape[0] // GLA_T)], axis=0)


_P_QK = 0
_P_VR = 2 * QK_W
_P_NAT = _P_VR + 2 * V_W
_P_LR = _P_NAT + 3 * NAT_W
_P_END = _P_LR + LANES


_NAT_SCALE = HD_NAT ** -0.5
assert _NAT_SCALE == 2.0 ** round(np.log2(_NAT_SCALE))
_GLA_COLS = 2 * QK_W + 2 * V_W
_REST_COLS = 2 * GLA_RANK + 3 * NAT_W
_REST_TILES = -(-_REST_COLS // LANES)
_PREP_ROWS = 128
assert _GLA_COLS % LANES == 0 and _GLA_COLS + _REST_COLS == 3104 and D_MODEL % _PREP_ROWS == 0


def _proj_prepare_weights(win_hbm, tail_ref, layer, w_bf, stage, sem):
    lane = lax.broadcasted_iota(jnp.int32, (_PREP_ROWS, LANES), 1)
    shift = 2 * GLA_RANK

    head = pltpu.make_async_copy(win_hbm.at[layer, :, pl.ds(0, _GLA_COLS)], stage.at[:, pl.ds(0, _GLA_COLS)],
                                 sem.at[0])
    head.start()
    head.wait()
    for r0 in range(0, D_MODEL, _PREP_ROWS):
        rows = slice(r0, r0 + _PREP_ROWS)
        w_bf[rows, _P_QK:_P_NAT] = stage[rows, 0:_GLA_COLS].astype(BF16)

    whole = (_REST_TILES - 1) * LANES
    stage[:, whole:] = tail_ref[0]
    rest = pltpu.make_async_copy(win_hbm.at[layer, :, pl.ds(_GLA_COLS, whole)], stage.at[:, pl.ds(0, whole)],
                                 sem.at[1])
    rest.start()
    rest.wait()
    for r0 in range(0, D_MODEL, _PREP_ROWS):
        rows = slice(r0, r0 + _PREP_ROWS)
        w_bf[rows, _P_LR:_P_END] = jnp.where(lane < shift, stage[rows, 0:LANES], 0.0).astype(BF16)
        for t in range(3 * NAT_W // LANES):
            lo = pltpu.roll(stage[rows, t * LANES:(t + 1) * LANES], LANES - shift, axis=1)
            hi = pltpu.roll(stage[rows, (t + 1) * LANES:(t + 2) * LANES], LANES - shift, axis=1)
            w_bf[rows, _P_NAT + t * LANES:_P_NAT + (t + 1) * LANES] = jnp.where(lane < LANES - shift, lo,
                                                                                  hi).astype(BF16)


def _rotary_partner(x):
    lane = lax.broadcasted_iota(jnp.int32, (x.shape[0], LANES), 1)
    first_of_pair = ((lane >> 4) & 1) == 0
    half = DK_GLA // 4
    tiles = []
    for t in range(x.shape[1] // LANES):
        tile = x[:, t * LANES:(t + 1) * LANES]
        tiles.append(jnp.where(first_of_pair, pltpu.roll(tile, LANES - half, axis=1), pltpu.roll(tile, half, axis=1)))
    return jnp.concatenate(tiles, axis=1)


def _proj_kernel(x_ref, mod_ref, ng_ref, win_hbm, tail_ref, wa_ref, ba_ref, cos_ref, sin_ref, *rest, layer):
    q_ref, k_ref, v_ref, r_ref, g_ref, qn_ref, kn_ctx_ref, vn_ctx_ref, kn_ref, vn_ref = rest[-13:-3]
    w_ref, stage, sem = rest[-3:]
    i = pl.program_id(0)

    @pl.when(i == 0)
    def _():
        _proj_prepare_weights(win_hbm, tail_ref, layer, w_ref, stage, sem)

    h = _rms(x_ref[...], ng_ref[2:3, :])
    h = (h * (1.0 + mod_ref[0, 4:5, :]) + mod_ref[0, 3:4, :]).astype(BF16)
    hp = _permute_chunks(_block_transpose_perm(), h)

    lr = _dot(hp, w_ref[:, _P_LR:_P_END]).astype(BF16)
    z = _dot(lr, wa_ref[...]) + ba_ref[...]
    g = (jnp.minimum(z, 0.0) - jnp.log1p(jnp.exp(-jnp.abs(z)))) * (1.0 / GATE_NORM)
    g_ref[0] = g[:, 0:QK_W]
    g_ref[1] = g[:, QK_W:2 * QK_W]

    qk = _dot(hp, w_ref[:, _P_QK:_P_VR])
    cos = cos_ref[...]
    sin = sin_ref[...]
    q_ref[...] = (qk[:, 0:QK_W] * cos + _rotary_partner(qk[:, 0:QK_W]) * sin) * (DK_GLA ** -0.5)
    k_ref[...] = qk[:, QK_W:2 * QK_W] * cos + _rotary_partner(qk[:, QK_W:2 * QK_W]) * sin

    vr = _dot(hp, w_ref[:, _P_VR:_P_NAT])
    v_ref[...] = vr[:, 0:V_W]
    r_ref[...] = vr[:, V_W:2 * V_W]

    nat = _dot(h, w_ref[:, _P_NAT:_P_LR])
    qn_ref[...] = (nat[:, 0:NAT_W] * _NAT_SCALE).astype(BF16)
    kn_ref[...] = nat[:, NAT_W:2 * NAT_W].astype(BF16)
    vn_ref[...] = nat[:, 2 * NAT_W:3 * NAT_W].astype(BF16)

    @pl.when(i < _CTX_TILES)
    def _():
        heads = (TM // SEQ, SEQ, H_NAT, HD_NAT)
        kn_ctx_ref[:, 0] = nat[:, NAT_W:2 * NAT_W].reshape(heads)
        vn_ctx_ref[:, 0] = nat[:, 2 * NAT_W:3 * NAT_W].reshape(heads)
        for later in range(1, kn_ctx_ref.shape[1]):
            kn_ctx_ref[:, later] = jnp.zeros(heads, F32)
            vn_ctx_ref[:, later] = jnp.zeros(heads, F32)


def _rope_table_block(i):
    lat_tiles = DEC_SEQ // TM
    return jnp.where(i < N_CTX // TM, lat_tiles, (i - N_CTX // TM) % lat_tiles)


def _project(x, mod_l, ng_l, w_in, w_a, b_a, cos_t, sin_t, new_cache, *, layer):
    tile = lambda w: pl.BlockSpec((TM, w), lambda i: (i, 0))
    if new_cache:
        ctx_heads = pl.BlockSpec((TM // SEQ, 1, SEQ, H_NAT, HD_NAT), lambda i: (_ctx_tile(i), layer, 0, 0, 0))
    else:
        ctx_heads = pl.BlockSpec((TM // SEQ, DEPTH, SEQ, H_NAT, HD_NAT), lambda i: (_ctx_tile(i), 0, 0, 0, 0))
    full = lambda a: pl.BlockSpec(a.shape, lambda i: (0,) * a.ndim)
    n_in = 9
    tail0 = _GLA_COLS + (_REST_TILES - 1) * LANES
    w_tail = jnp.pad(w_in[:, :, tail0:], ((0, 0), (0, 0), (0, LANES - (w_in.shape[2] - tail0))))
    return pl.pallas_call(
        functools.partial(_proj_kernel, layer=layer),
        grid=(N_ALL // TM,),
        in_specs=[
            tile(D_MODEL),
            pl.BlockSpec((1, N_MOD, D_MODEL), lambda i: (_group_of_tile(i), 0, 0)),
            full(ng_l),
            pl.BlockSpec(memory_space=pl.ANY),
            pl.BlockSpec((1, D_MODEL, LANES), lambda i: (layer, 0, 0)),
            full(w_a), full(b_a),
            pl.BlockSpec((TM, QK_W), lambda i: (_rope_table_block(i), 0)),
            pl.BlockSpec((TM, QK_W), lambda i: (_rope_table_block(i), 0)),
        ] + [pl.BlockSpec(memory_space=pl.ANY)] * len(new_cache),
        input_output_aliases={n_in + n: 6 + n for n in range(len(new_cache))},
        out_specs=[
            tile(QK_W), tile(QK_W), tile(V_W), tile(V_W),
            pl.BlockSpec((2, TM, QK_W), lambda i: (0, i, 0)),
            tile(NAT_W), ctx_heads, ctx_heads, tile(NAT_W), tile(NAT_W),
        ],
        out_shape=[
            jax.ShapeDtypeStruct((N_ALL, QK_W), F32), jax.ShapeDtypeStruct((N_ALL, QK_W), F32),
            jax.ShapeDtypeStruct((N_ALL, V_W), F32), jax.ShapeDtypeStruct((N_ALL, V_W), F32),
            jax.ShapeDtypeStruct((2, N_ALL, QK_W), F32),
            jax.ShapeDtypeStruct((N_ALL, NAT_W), BF16),
            jax.ShapeDtypeStruct((BATCH, DEPTH, SEQ, H_NAT, HD_NAT), F32),
            jax.ShapeDtypeStruct((BATCH, DEPTH, SEQ, H_NAT, HD_NAT), F32),
            jax.ShapeDtypeStruct((N_ALL, NAT_W), BF16), jax.ShapeDtypeStruct((N_ALL, NAT_W), BF16),
        ],
        scratch_shapes=[
            pltpu.VMEM((D_MODEL, _P_END), BF16),
            pltpu.VMEM((D_MODEL, _REST_TILES * LANES), F32),
            pltpu.SemaphoreType.DMA((2,)),
        ],
        compiler_params=pltpu.CompilerParams(
            dimension_semantics=("arbitrary",), vmem_limit_bytes=VMEM_LIMIT),
        name="mixer_proj",
    )(x, mod_l, ng_l, w_in, w_tail, w_a, b_a, cos_t, sin_t, *new_cache)


def _gla_tables():
    rows = []
    seq_specs = [(b * (SEQ // GLA_T), SEQ // GLA_T, 0, 0, b, 1) for b in range(BATCH)]
    seq_specs += [(N_CTX // GLA_T + b * (DEC_SEQ // GLA_T), DEC_SEQ // GLA_T, b, 1, BATCH - 1, 0)
                  for b in range(DEC_BATCH)]
    for blk0, nchunk, s0_row, has_s0, fin_row, wants_fin in seq_specs:
        states = (s0_row, has_s0, fin_row, wants_fin)
        if nchunk == 1:
            rows.append((blk0, _GLA_BOTH, 1, blk0, 0) + states)
            continue
        for direction in (0, 1):
            order = range(nchunk) if direction == 0 else range(nchunk - 1, -1, -1)
            for n, c in enumerate(order):
                out_blk = blk0 + (c if direction == 1 else nchunk - 1)
                rows.append((blk0 + c, direction, int(n == 0), out_blk, c) + states)
    return np.asarray(rows, dtype=np.int32).T.copy()


_GLA_BOTH = 2


_GLA_TAB = _gla_tables()
_GLA_ITEMS = _GLA_TAB.shape[1]
_GLA_PAIR_ROWS = GLA_NB * GLA_CB * (GLA_CB + 1) // 2


def _gla_item(direction, q_ref, k_ref, v_ref, g_ref, res, s_scr, cp, sstk, p_scr, w_scr):
    T, CB, NB = GLA_T, GLA_CB, GLA_NB
    fwd = direction == 0
    slab = lambda i: slice(i * NB, (i + 1) * NB)

    cum = None
    for i in (range(CB) if fwd else range(CB - 1, -1, -1)):
        gi = g_ref[slab(i), :]
        cum = gi if cum is None else cum + gi
        cp[slab(i), :] = cum
    total = cum
    cum_all = cp[...]
    qt = q_ref[...] * jnp.exp(cum_all)
    kh = k_ref[...] * jnp.exp(jnp.concatenate([total] * CB, axis=0) - cum_all)
    dec_t = jnp.concatenate([jnp.exp(total), jnp.zeros((LANES - NB, QK_W), F32)], axis=0).T

    erow = lax.broadcasted_iota(jnp.int32, (QK_W, QK_W), 0)
    ecol = lax.broadcasted_iota(jnp.int32, (QK_W, QK_W), 1)
    head_sum = jnp.where((erow >> 6) == (ecol >> 6), 1.0, 0.0).astype(BF16)

    key_positions = lambda i: range(i + 1) if fwd else range(i, CB)
    r0 = 0
    for i in range(CB):
        qi = q_ref[slab(i), :]
        ci = cp[slab(i), :]
        for j in key_positions(i):
            e = jnp.exp(ci - cp[slab(j), :])
            p_scr[r0:r0 + NB, :] = (qi * k_ref[slab(j), :] * e).astype(BF16)
            r0 += NB
    w_scr[...] = _dot(p_scr[...], head_sum)
    first_half = lax.broadcasted_iota(jnp.int32, (NB, LANES), 1) < DK_GLA
    r0 = 0
    for i in range(CB):
        acc = None
        for j in key_positions(i):
            spread = []
            for pair in range(H_GLA // 2):
                tile = w_scr[r0:r0 + NB, pair * LANES:(pair + 1) * LANES]
                other = pltpu.roll(tile, DK_GLA, axis=1)
                spread += [jnp.where(first_half, tile, other), jnp.where(first_half, other, tile)]
            term = jnp.concatenate(spread, axis=1) * v_ref[slab(j), :]
            acc = term if acc is None else acc + term
            r0 += NB
        res[slab(i), :] = acc

    kht = kh.T.astype(BF16)
    v_bf = v_ref[...].astype(BF16)
    key_blk = lax.broadcasted_iota(jnp.int32, (DK_GLA, T), 1) & (NB - 1)
    row_blk = lax.broadcasted_iota(jnp.int32, (T, LANES), 0) & (NB - 1)
    lane_half = lax.broadcasted_iota(jnp.int32, (T, LANES), 1) >> 6
    order = range(NB) if fwd else range(NB - 1, -1, -1)
    for h in range(H_GLA):
        kh_h = kht[h * DK_GLA:(h + 1) * DK_GLA, :]
        kv = _dot(jnp.concatenate([jnp.where(key_blk == b, kh_h, 0) for b in range(NB)], axis=0),
                  v_bf[:, h * DV_GLA:(h + 1) * DV_GLA])
        s = s_scr[h]
        for b in order:
            sstk[h, b * DK_GLA:(b + 1) * DK_GLA, :] = s.astype(BF16)
            s = dec_t[h * DK_GLA:(h + 1) * DK_GLA, b:b + 1] * s + kv[b * DK_GLA:(b + 1) * DK_GLA, :]
        s_scr[h] = s

        pair_tile = qt[:, (h // 2) * LANES:(h // 2 + 1) * LANES]
        both = jnp.where(lane_half == h % 2, pair_tile, pltpu.roll(pair_tile, DK_GLA, axis=1))
        lhs = jnp.concatenate([jnp.where(row_blk == 2 * j + lane_half, both, 0.0).astype(BF16)
                               for j in range(NB // 2)], axis=1)
        res[:, h * DV_GLA:(h + 1) * DV_GLA] += _dot(lhs, sstk[h])


def _gla_kernel(tab_ref, q_ref, k_ref, v_ref, g_ref, s0_ref, *rest):
    o_ref, so_ref, s_scr, cp, sstk, p_scr, w_scr, res, held = rest[-9:]
    it = pl.program_id(0)
    kind = tab_ref[1, it]
    slot = tab_ref[4, it]
    has_s0 = tab_ref[6, it] == 1
    wants_final = tab_ref[8, it] == 1

    def run(d):
        _gla_item(d, q_ref, k_ref, v_ref, g_ref.at[d], res.at[d], s_scr.at[d], cp.at[d], sstk.at[d],
                  p_scr.at[d], w_scr.at[d])

    def clear_later_layers():
        for later in range(1, so_ref.shape[1]):
            so_ref[0, later] = jnp.zeros(so_ref.shape[2:], F32)

    @pl.when(kind == _GLA_BOTH)
    def _():
        s_scr[...] = jnp.where(has_s0, s0_ref[0, 0], 0.0)
        run(0)
        run(1)
        o_ref[...] = res[0] + res[1]

        @pl.when(wants_final)
        def _():
            so_ref[0, 0] = s_scr[...]
            clear_later_layers()

    for d in (0, 1):
        @pl.when(kind == d)
        def _(d=d):
            @pl.when(tab_ref[2, it] == 1)
            def _():
                s_scr[d] = jnp.where(has_s0, s0_ref[0, 0, d], 0.0)

            run(d)
            if d == 0:
                held[slot] = res[0]
            else:
                o_ref[...] = held[slot] + res[1]

            @pl.when(wants_final)
            def _():
                so_ref[0, 0, d] = s_scr[d]
                if d == 0:
                    clear_later_layers()


def _gla(q, k, v, g, state_gla, new_state, *, layer):
    tok = lambda w: pl.BlockSpec((GLA_T, w), lambda it, tab: (tab[0, it], 0))
    state_blk = (2, H_GLA, DK_GLA, DV_GLA)
    if new_state:
        final_spec = pl.BlockSpec((1, 1) + state_blk, lambda it, tab: (tab[7, it], layer, 0, 0, 0, 0))
    else:
        final_spec = pl.BlockSpec((1, DEPTH) + state_blk, lambda it, tab: (tab[7, it], 0, 0, 0, 0, 0))
    grid_spec = pltpu.PrefetchScalarGridSpec(
        num_scalar_prefetch=1,
        grid=(_GLA_ITEMS,),
        in_specs=[
            tok(QK_W), tok(QK_W), tok(V_W),
            pl.BlockSpec((2, GLA_T, QK_W), lambda it, tab: (0, tab[0, it], 0)),
            pl.BlockSpec((1, 1) + state_blk, lambda it, tab: (tab[5, it], layer, 0, 0, 0, 0)),
        ] + [pl.BlockSpec(memory_space=pl.ANY)] * len(new_state),
        out_specs=[
            pl.BlockSpec((GLA_T, V_W), lambda it, tab: (tab[3, it], 0)),
            final_spec,
        ],
        scratch_shapes=[
            pltpu.VMEM((2, H_GLA, DK_GLA, DV_GLA), F32),
            pltpu.VMEM((2, GLA_T, QK_W), F32),
            pltpu.VMEM((2, H_GLA, GLA_NB * DK_GLA, DV_GLA), BF16),
            pltpu.VMEM((2, _GLA_PAIR_ROWS, QK_W), BF16),
            pltpu.VMEM((2, _GLA_PAIR_ROWS, QK_W), F32),
            pltpu.VMEM((2, GLA_T, V_W), F32),
            pltpu.VMEM((DEC_SEQ // GLA_T, GLA_T, V_W), F32),
        ],
    )
    return pl.pallas_call(
        _gla_kernel,
        grid_spec=grid_spec,
        out_shape=[
            jax.ShapeDtypeStruct((N_ALL, V_W), F32),
            jax.ShapeDtypeStruct((BATCH, DEPTH) + state_blk, F32),
        ],
        input_output_aliases={6 + n: 1 + n for n in range(len(new_state))},
        compiler_params=pltpu.CompilerParams(
            dimension_semantics=("arbitrary",), vmem_limit_bytes=VMEM_LIMIT),
        name="gla",
    )(jnp.asarray(_GLA_TAB), q, k, v, g, state_gla, *new_state)


def _with_ones(v):
    return jnp.concatenate([v, jnp.ones_like(v)], axis=1)


def _softmax_pv(s_list, v1_list):
    m = s_list[0].max(axis=-1, keepdims=True)
    for s in s_list[1:]:
        m = jnp.maximum(m, s.max(axis=-1, keepdims=True))
    acc = None
    for s, v1 in zip(s_list, v1_list):
        pv = _dot(jnp.exp(s - m).astype(BF16), v1)
        acc = pv if acc is None else acc + pv
    width = acc.shape[1] // 2
    return acc[:, :width] / acc[:, width:]


def _ctx_attn_kernel(q_ref, k_ref, v_ref, o_ref):
    lane =lax.broadcasted_iota(jnp.int32, (SEQ, LANES), 1)
    for t in range(NAT_W // LANES):
        sl = slice(t * LANES, (t + 1) * LANES)
        qt = q_ref[:, sl]
        kt = k_ref[:, sl].astype(BF16)
        vt = _with_ones(v_ref[:, sl].astype(BF16))
        out = jnp.zeros((SEQ, LANES), F32)
        for half in range(LANES // HD_NAT):
            mine = (lane >> 6) == half
            s = _dot_nt(jnp.where(mine, qt, 0).astype(BF16), kt)
            out = jnp.where(mine, _softmax_pv([s], [vt]), out)
        o_ref[:, sl] = out.astype(o_ref.dtype)


def _ctx_attention(qn, kn, vn):
    spec = pl.BlockSpec((SEQ, NAT_W), lambda b: (b, 0))
    return pl.pallas_call(
        _ctx_attn_kernel,
        grid=(BATCH,),
        in_specs=[spec, spec, spec],
        out_specs=spec,
        out_shape=jax.ShapeDtypeStruct((N_CTX, NAT_W), BF16),
        compiler_params=pltpu.CompilerParams(
            dimension_semantics=("arbitrary",), vmem_limit_bytes=VMEM_LIMIT),
        name="ctx_attention",
    )(qn, kn, vn)


_NAT_QROWS = 4
_NAT_GROUPS = GRID_ROWS // _NAT_QROWS
_NAT_KROWS = WIN_H + _NAT_QROWS
_NAT_Q = _NAT_QROWS * GRID_W
_NAT_KEYS = _NAT_KROWS * GRID_W
_NAT_DR = 2 * WIN_H - 1
_NAT_DC = 2 * WIN_W - 1


def _nat_build_bias(rpb_ref, tz_scr):
    c = lax.broadcasted_iota(jnp.int32, (GRID_W, LANES), 0)
    lane = lax.broadcasted_iota(jnp.int32, (GRID_W, LANES), 1)
    kc = lane & (GRID_W - 1)
    second = (lane >> 6) == 1
    win_start = jnp.clip(c - WIN_W // 2, 0, GRID_W - WIN_W)
    valid = (kc >= win_start) & (kc < win_start + WIN_W)

    def one_row(n, carry):
        dr = n >> 3
        h = n & (H_NAT - 1)
        src = jnp.broadcast_to(rpb_ref[pl.ds(h * _NAT_DR + dr, 1), :], (GRID_W, LANES))
        rolled = pltpu.roll(src, LANES - (WIN_W - 1), axis=1, stride=1, stride_axis=0)
        tz_scr[dr, h] = jnp.where(valid, rolled, NEG_INF)
        return carry

    lax.fori_loop(0, _NAT_DR * H_NAT, one_row, 0)

    def pair_rows(n, carry):
        dr = n >> 3
        h = n & (H_NAT - 1)
        tz_scr[dr, h] = jnp.where(second, tz_scr[dr + 1, h], tz_scr[dr, h])
        return carry

    lax.fori_loop(0, (_NAT_DR - 1) * H_NAT, pair_rows, 0)


def _nat_kernel(rpb_ref, q_ref, k_ref, v_ref, ck_ref, cv_ref, o_ref, tz_scr, bias_scr):
    grp = pl.program_id(1)

    @pl.when((pl.program_id(0) == 0) & (grp == 0))
    def _():
        _nat_build_bias(rpb_ref, tz_scr)

    for g in range(_NAT_GROUPS):
        @pl.when(grp == g)
        def _(g=g):
            _nat_group(g, q_ref, k_ref, v_ref, ck_ref, cv_ref, o_ref, tz_scr, bias_scr)


def _nat_group(g, q_ref, k_ref, v_ref, ck_ref, cv_ref, o_ref, tz_scr, bias_scr):
    q_rows = range(_NAT_QROWS * g, _NAT_QROWS * (g + 1))
    first_key_row = [min(max(r - WIN_H // 2, 0), GRID_ROWS - WIN_H) for r in q_rows]
    krow0 = min(first_key_row)
    n_krows = max(first_key_row) + WIN_H - krow0
    n_krows += n_krows % 2
    assert krow0 + n_krows <= GRID_ROWS and n_krows <= _NAT_KROWS
    keys = slice(krow0 * GRID_W, (krow0 + n_krows) * GRID_W)

    lane_q = lax.broadcasted_iota(jnp.int32, (_NAT_Q, LANES), 1)
    second = lax.broadcasted_iota(jnp.int32, (GRID_W, LANES), 1) >= GRID_W
    neg = jnp.full((GRID_W, LANES), NEG_INF, F32)
    for t in range(NAT_W // LANES):
        sl = slice(t * LANES, (t + 1) * LANES)
        qt = q_ref[:, sl]
        kw = k_ref[keys, sl].astype(BF16)
        vw = _with_ones(v_ref[keys, sl].astype(BF16))
        ck = ck_ref[0, 0, :, sl].astype(BF16)
        cv = _with_ones(cv_ref[0, 0, :, sl].astype(BF16))
        out = jnp.zeros((_NAT_Q, LANES), F32)
        for half in range(LANES // HD_NAT):
            h = 2 * t + half
            for qr, (r, lo) in enumerate(zip(q_rows, first_key_row)):
                for kp in range(n_krows // 2):
                    kr = krow0 + 2 * kp
                    ok_a, ok_b = lo <= kr < lo + WIN_H, lo <= kr + 1 < lo + WIN_H
                    if ok_a or ok_b:
                        tile = tz_scr[min(max(kr - r + (WIN_H - 1), 0), _NAT_DR - 1), h]
                        if not ok_a:
                            tile = jnp.where(second, tile, neg)
                        if not ok_b:
                            tile = jnp.where(second, neg, tile)
                    else:
                        tile = neg
                    bias_scr[qr * GRID_W:(qr + 1) * GRID_W, kp * LANES:(kp + 1) * LANES] = tile
            mine = (lane_q >> 6) == half
            qm = jnp.where(mine, qt, 0).astype(BF16)
            s_win = _dot_nt(qm, kw) + bias_scr[:, 0:n_krows * GRID_W]
            s_ctx = _dot_nt(qm, ck)
            out = jnp.where(mine, _softmax_pv([s_win, s_ctx], [vw, cv]), out)
        o_ref[:, sl] = out.astype(o_ref.dtype)


def _nat_attention(qn, kn, vn, cache_k, cache_v, rpb_l, layer):
    lat0 = N_CTX // _NAT_Q
    half = jnp.pad(rpb_l.reshape(H_NAT * _NAT_DR, _NAT_DC), ((0, 0), (0, GRID_W - _NAT_DC)))
    rpb_rows = jnp.concatenate([half, half], axis=1)
    return pl.pallas_call(
        _nat_kernel,
        grid=(DEC_BATCH, _NAT_GROUPS),
        in_specs=[
            pl.BlockSpec((H_NAT * _NAT_DR, LANES), lambda b, g: (0, 0)),
            pl.BlockSpec((_NAT_Q, NAT_W), lambda b, g: (lat0 + b * _NAT_GROUPS + g, 0)),
            pl.BlockSpec((DEC_SEQ, NAT_W), lambda b, g: (N_CTX // DEC_SEQ + b, 0)),
            pl.BlockSpec((DEC_SEQ, NAT_W), lambda b, g: (N_CTX // DEC_SEQ + b, 0)),
            pl.BlockSpec((1, 1, PAST_LEN, NAT_W), lambda b, g: (b, layer, 0, 0)),
            pl.BlockSpec((1, 1, PAST_LEN, NAT_W), lambda b, g: (b, layer, 0, 0)),
        ],
        out_specs=pl.BlockSpec((_NAT_Q, NAT_W), lambda b, g: (b * _NAT_GROUPS + g, 0)),
        out_shape=jax.ShapeDtypeStruct((N_LAT, NAT_W), BF16),
        scratch_shapes=[
            pltpu.VMEM((_NAT_DR, H_NAT, GRID_W, LANES), F32),
            pltpu.VMEM((_NAT_Q, _NAT_KEYS), F32),
        ],
        compiler_params=pltpu.CompilerParams(
            dimension_semantics=("arbitrary", "arbitrary"), vmem_limit_bytes=VMEM_LIMIT),
        name="nat_attention",
    )(rpb_rows, qn, kn, vn, cache_k, cache_v)


def _out_kernel(x_ref, o_ref, r_ref, on_ctx_ref, on_lat_ref, mod_ref, ng_ref, gng_ref, w32_ref, y_ref, w_ref):
    @pl.when(pl.program_id(0) == 0)
    def _():
        w_ref[...] = w32_ref[0].astype(BF16)

    o_nat = _read_split(pl.program_id(0), on_ctx_ref, on_lat_ref)
    og = o_ref[...]
    parts = []
    for h in range(H_GLA):
        parts.append(_rms(og[:, h * DV_GLA:(h + 1) * DV_GLA], gng_ref[...]))
    merged = (jnp.concatenate(parts, axis=1) * _silu(r_ref[...])).astype(BF16)
    merged = _permute_chunks(_block_transpose_perm(), merged)
    y = _dot(merged, w_ref[0:V_W, :]) + _dot(o_nat.astype(BF16), w_ref[V_W:, :])
    y_ref[...] = x_ref[...] + mod_ref[0, 5:6, :] * _rms(y, ng_ref[3:4, :])


def _mixer_out(x, o_gla, r, o_ctx, o_lat, mod_l, ng_l, gng_l, w_out, *, layer):
    tile = lambda w: pl.BlockSpec((TM, w), lambda i: (i, 0))
    full = lambda a: pl.BlockSpec(a.shape, lambda i: (0,) * a.ndim)
    return pl.pallas_call(
        _out_kernel,
        grid=(N_ALL // TM,),
        in_specs=[
            tile(D_MODEL),
            tile(V_W),
            tile(V_W),
            pl.BlockSpec((TM, NAT_W), lambda i: (_ctx_tile(i), 0)),
            pl.BlockSpec((TM, NAT_W), lambda i: (_lat_tile(i), 0)),
            pl.BlockSpec((1, N_MOD, D_MODEL), lambda i: (_group_of_tile(i), 0, 0)),
            full(ng_l), full(gng_l),
            pl.BlockSpec((1,) + w_out.shape[1:], lambda i: (layer, 0, 0)),
        ],
        out_specs=tile(D_MODEL),
        out_shape=jax.ShapeDtypeStruct((N_ALL, D_MODEL), F32),
        scratch_shapes=[pltpu.VMEM(w_out.shape[1:], BF16)],
        compiler_params=pltpu.CompilerParams(
            dimension_semantics=("arbitrary",), vmem_limit_bytes=VMEM_LIMIT),
        name="mixer_out",
    )(x, o_gla, r, o_ctx, o_lat, mod_l, ng_l, gng_l, w_out)


def _rope_tables():
    quarter = DK_GLA // 4
    freqs = ROPE_BASE ** (-jnp.arange(quarter, dtype=F32) / quarter)
    t = jnp.arange(DEC_SEQ)
    ang_r = (t // GRID_W).astype(F32)[:, None] * freqs[None, :]
    ang_c = (t % GRID_W).astype(F32)[:, None] * freqs[None, :]
    cos_h = jnp.concatenate([jnp.cos(ang_r), jnp.cos(ang_r), jnp.cos(ang_c), jnp.cos(ang_c)], axis=1)
    sin_h = jnp.concatenate([-jnp.sin(ang_r), jnp.sin(ang_r), -jnp.sin(ang_c), jnp.sin(ang_c)], axis=1)
    pos_major = lambda a: a.reshape(-1, GLA_NB, GLA_CB, QK_W).transpose(0, 2, 1, 3).reshape(a.shape)
    cos_t = jnp.concatenate([pos_major(jnp.tile(cos_h, (1, H_GLA))), jnp.ones((TM, QK_W), F32)], axis=0)
    sin_t = jnp.concatenate([pos_major(jnp.tile(sin_h, (1, H_GLA))), jnp.zeros((TM, QK_W), F32)], axis=0)
    return cos_t, sin_t


def _gate_up_weights(gla_wa2_l, gla_ba_l):
    w_a = jnp.zeros((LANES, 2 * QK_W), F32)
    w_a = w_a.at[0:GLA_RANK, 0:QK_W].set(gla_wa2_l[0]).at[GLA_RANK:2 * GLA_RANK, QK_W:].set(gla_wa2_l[1])
    b_a = jnp.concatenate([gla_ba_l[0], gla_ba_l[1]])[None, :]
    return w_a.astype(BF16), b_a


def kernel(x_prompt, x_sample, cache_k, cache_v, state_gla, c, c_ctx, w_mod, b_mod, norm_g, ffn_w_in, ffn_w_out,
           w_in, gla_wa2, gla_ba, gla_norm_g, nat_rpb, w_out):
    cvecs = jnp.zeros((SUBLANES, D_MODEL), F32).at[0].set(c_ctx).at[1:1 + DEC_BATCH].set(c)
    mod = _modulation(cvecs, w_mod, b_mod)[:, :N_GROUPS].reshape(DEPTH, N_GROUPS, N_MOD, D_MODEL)

    cos_t, sin_t = _rope_tables()
    ck = cache_k.reshape(DEC_BATCH, DEPTH, PAST_LEN, NAT_W)
    cv = cache_v.reshape(DEC_BATCH, DEPTH, PAST_LEN, NAT_W)

    x = (x_prompt.reshape(N_CTX, D_MODEL), x_sample.reshape(N_LAT, D_MODEL))
    new_cache, new_state = (), ()
    for l in range(DEPTH):
        mod_l, ng_l = mod[l], norm_g[l]
        x = _ffn(x, mod_l, ng_l, ffn_w_in, ffn_w_out, layer=l, which=0)

        w_a, b_a = _gate_up_weights(gla_wa2[l], gla_ba[l])
        q, k, v, r, g, qn, new_k, new_v, kn, vn = _project(
            x, mod_l, ng_l, w_in, w_a, b_a, cos_t, sin_t, new_cache, layer=l)
        new_cache = (new_k, new_v)

        o_gla, new_s = _gla(q, k, v, g, state_gla, new_state, layer=l)
        new_state = (new_s,)
        o_ctx = _ctx_attention(qn, kn, vn)
        o_lat = _nat_attention(qn, kn, vn, ck, cv, nat_rpb[l], l)

        x = _mixer_out(x, o_gla, r, o_ctx, o_lat, mod_l, ng_l, gla_norm_g[l][None, :], w_out, layer=l)
        x = _ffn(x, mod_l, ng_l, ffn_w_in, ffn_w_out, layer=l, which=1, split_out=(l == DEPTH - 1))

    y_prompt = x[0].reshape(BATCH, SEQ, D_MODEL)
    y_sample = x[1].reshape(DEC_BATCH, DEC_SEQ, D_MODEL)
    return (y_prompt, y_sample, new_cache[0], new_cache[1], new_state[0])
```

```python
import functools

import numpy as np
import jax
import jax.numpy as jnp
from jax import lax
from jax.experimental import pallas as pl
from jax.experimental.pallas import tpu as pltpu

D_MODEL = 1024
BATCH = 16
SEQ = 256
DEPTH = 2
DEC_BATCH = 2
DEC_SEQ = 1024
PAST_LEN = 512
GRID_W = 64
H_GLA = 4
DK_GLA = 64
DV_GLA = 128
GLA_RANK = 16
GATE_NORM = 16.0
H_NAT = 8
HD_NAT = 64
WIN_H = 8
WIN_W = 16
D_FF = 2816
N_MOD = 9
ROPE_BASE = 10000.0
EPS = 1e-6
NEG_INF = -1e30

F32 = jnp.float32
BF16 = jnp.bfloat16

N_CTX = BATCH * SEQ
N_LAT = DEC_BATCH * DEC_SEQ
N_ALL = N_CTX + N_LAT
N_GROUPS = 1 + DEC_BATCH
QK_W = H_GLA * DK_GLA
V_W = H_GLA * DV_GLA
NAT_W = H_NAT * HD_NAT
GRID_ROWS = DEC_SEQ // GRID_W

LANES = 128
SUBLANES = 8

TM = 512
FFN_TC = 256
FFN_NC = D_FF // FFN_TC
FFN_STAGES = 2
MOD_TN = 1152
GLA_T = 256
GLA_CB = 16
GLA_NB = GLA_T // GLA_CB
VMEM_LIMIT = 56 * 1024 * 1024

assert N_CTX % TM == 0 and DEC_SEQ % TM == 0 and D_FF % FFN_TC == 0 and FFN_TC % LANES == 0
assert SEQ == GLA_T and DEC_SEQ % GLA_T == 0


def _group_of_tile(i):
    return jnp.where(i < N_CTX // TM, 0, 1 + (i - N_CTX // TM) // (DEC_SEQ // TM))


def _dot(a, b):
    return jnp.dot(a, b, preferred_element_type=F32)


def _dot_nt(a, b):
    return lax.dot_general(a, b, (((1,), (1,)), ((), ())), preferred_element_type=F32)


def _rms(x, g):
    ms = jnp.mean(x * x, axis=-1, keepdims=True)
    return x * lax.rsqrt(ms + EPS) * g


def _silu(x):
    return x * jax.nn.sigmoid(x)


def _mod_kernel(c_ref, w_ref, b_ref, o_ref):
    s = _silu(c_ref[...]).astype(BF16)
    o_ref[0] = _dot(s, w_ref[0].astype(BF16)) + b_ref[0]


def _modulation(cvecs, w_mod, b_mod):
    n_out = N_MOD * D_MODEL
    return pl.pallas_call(
        _mod_kernel,
        grid=(DEPTH, n_out // MOD_TN),
        in_specs=[
            pl.BlockSpec((SUBLANES, D_MODEL), lambda l, j: (0, 0)),
            pl.BlockSpec((1, D_MODEL, MOD_TN), lambda l, j: (l, 0, j)),
            pl.BlockSpec((1, 1, MOD_TN), lambda l, j: (l, 0, j)),
        ],
        out_specs=pl.BlockSpec((1, SUBLANES, MOD_TN), lambda l, j: (l, 0, j)),
        out_shape=jax.ShapeDtypeStruct((DEPTH, SUBLANES, n_out), F32),
        compiler_params=pltpu.CompilerParams(
            dimension_semantics=("arbitrary", "arbitrary"), vmem_limit_bytes=VMEM_LIMIT),
        name="modulation",
    )(cvecs, w_mod, b_mod.reshape(DEPTH, 1, n_out))


_CTX_TILES = N_CTX // TM


def _ctx_tile(i):
    return jnp.minimum(i, _CTX_TILES - 1)


def _lat_tile(i):
    return jnp.maximum(i - _CTX_TILES, 0)


def _read_split(i, ctx_ref, lat_ref):
    return jnp.where(i < _CTX_TILES, ctx_ref[...], lat_ref[...])


def _ffn_kernel(*refs, m0, n0, layer, which, split_in, split_out):
    n_x = 2 if split_in else 1
    x_refs, (mod_ref, ng_ref, win_hbm, wout_hbm) = refs[:n_x], refs[n_x:n_x + 4]
    n_o = 2 if split_out else 1
    o_refs = refs[n_x + 4:n_x + 4 + n_o]
    h_scr, acc_scr, x_scr, wg_bf, wu_bf, wo_bf, stage_g, stage_u, stage_o, sem = refs[n_x + 4 + n_o:]
    i = pl.program_id(0)

    def chunk_copies(c, slot):
        cols = pl.ds(c * FFN_TC, FFN_TC)
        up_cols = pl.ds(D_FF + c * FFN_TC, FFN_TC)
        return (
            pltpu.make_async_copy(win_hbm.at[layer, which, :, cols], stage_g.at[slot], sem.at[0, slot]),
            pltpu.make_async_copy(win_hbm.at[layer, which, :, up_cols], stage_u.at[slot], sem.at[1, slot]),
            pltpu.make_async_copy(wout_hbm.at[layer, which, cols, :], stage_o.at[slot], sem.at[2, slot]),
        )

    def accumulate(c):
        h = h_scr[...]
        gate = _dot(h, wg_bf[c])
        up = _dot(h, wu_bf[c])
        part = _dot((_silu(gate) * up).astype(BF16), wo_bf[c])
        if c == 0:
            acc_scr[...] = part
        else:
            acc_scr[...] += part

    x = _read_split(i, *x_refs) if split_in else x_refs[0][...]
    x_scr[...] = x
    gain = ng_ref[n0:n0 + 1, :] * (1.0 + mod_ref[0, m0 + 1:m0 + 2, :])
    h_scr[...] = (_rms(x, gain) + mod_ref[0, m0:m0 + 1, :]).astype(BF16)

    @pl.when(i == 0)
    def _():
        for c in range(min(FFN_STAGES, FFN_NC)):
            for cp in chunk_copies(c, c):
                cp.start()
        for c in range(FFN_NC):
            slot = c % FFN_STAGES
            for cp in chunk_copies(c, slot):
                cp.wait()
            wg_bf[c] = stage_g[slot].astype(BF16)
            wu_bf[c] = stage_u[slot].astype(BF16)
            wo_bf[c] = stage_o[slot].astype(BF16)
            if c + FFN_STAGES < FFN_NC:
                for cp in chunk_copies(c + FFN_STAGES, slot):
                    cp.start()
            accumulate(c)

    @pl.when(i > 0)
    def _():
        for c in range(FFN_NC):
            accumulate(c)

    gain = 0.5 * mod_ref[0, m0 + 2:m0 + 3, :] * ng_ref[n0 + 1:n0 + 2, :]
    out = x_scr[...] + _rms(acc_scr[...], gain)
    if split_out:
        @pl.when(i < _CTX_TILES)
        def _():
            o_refs[0][...] = out

        @pl.when(i >= _CTX_TILES)
        def _():
            o_refs[1][...] = out
    else:
        o_refs[0][...] = out


def _ffn(x, mod_l, ng_l, w_in, w_out, *, layer, which, split_out=False):
    m0, n0 = (0, 0) if which == 0 else (6, 4)
    split_in = isinstance(x, tuple)
    tile = pl.BlockSpec((TM, D_MODEL), lambda i: (i, 0))
    ctx_tile = pl.BlockSpec((TM, D_MODEL), lambda i: (_ctx_tile(i), 0))
    lat_tile = pl.BlockSpec((TM, D_MODEL), lambda i: (_lat_tile(i), 0))
    if split_out:
        out_specs = [ctx_tile, lat_tile]
        out_shape = [jax.ShapeDtypeStruct((N_CTX, D_MODEL), F32), jax.ShapeDtypeStruct((N_LAT, D_MODEL), F32)]
    else:
        out_specs = tile
        out_shape = jax.ShapeDtypeStruct((N_ALL, D_MODEL), F32)
    return pl.pallas_call(
        functools.partial(_ffn_kernel, m0=m0, n0=n0, layer=layer, which=which, split_in=split_in,
                          split_out=split_out),
        grid=(N_ALL // TM,),
        in_specs=([ctx_tile, lat_tile] if split_in else [tile]) + [
            pl.BlockSpec((1, N_MOD, D_MODEL), lambda i: (_group_of_tile(i), 0, 0)),
            pl.BlockSpec((6, D_MODEL), lambda i: (0, 0)),
            pl.BlockSpec(memory_space=pl.ANY),
            pl.BlockSpec(memory_space=pl.ANY),
        ],
        out_specs=out_specs,
        out_shape=out_shape,
        scratch_shapes=[
            pltpu.VMEM((TM, D_MODEL), BF16),
            pltpu.VMEM((TM, D_MODEL), F32),
            pltpu.VMEM((TM, D_MODEL), F32),
            pltpu.VMEM((FFN_NC, D_MODEL, FFN_TC), BF16),
            pltpu.VMEM((FFN_NC, D_MODEL, FFN_TC), BF16),
            pltpu.VMEM((FFN_NC, FFN_TC, D_MODEL), BF16),
            pltpu.VMEM((FFN_STAGES, D_MODEL, FFN_TC), F32),
            pltpu.VMEM((FFN_STAGES, D_MODEL, FFN_TC), F32),
            pltpu.VMEM((FFN_STAGES, FFN_TC, D_MODEL), F32),
            pltpu.SemaphoreType.DMA((3, FFN_STAGES)),
        ],
        compiler_params=pltpu.CompilerParams(
            dimension_semantics=("arbitrary",), vmem_limit_bytes=VMEM_LIMIT),
        name="ffn",
    )(*(x if split_in else (x,)), mod_l, ng_l, w_in, w_out)


def _block_transpose_perm():
    r = lax.broadcasted_iota(jnp.int32, (GLA_T, GLA_T), 0)
    c = lax.broadcasted_iota(jnp.int32, (GLA_T, GLA_T), 1)
    return jnp.where(c == (r & (GLA_CB - 1)) * GLA_NB + (r >> 4), 1.0, 0.0).astype(BF16)


def _permute_chunks(perm, a):
    return jnp.concatenate([_dot(perm, a[c * GLA_T:(c + 1) * GLA_T, :]).astype(BF16)
                            for c in range(a.shape[0] // GLA_T)], axis=0)


_P_QK = 0
_P_VR = 2 * QK_W
_P_NAT = _P_VR + 2 * V_W
_P_LR = _P_NAT + 3 * NAT_W
_P_END = _P_LR + LANES


_NAT_SCALE = HD_NAT ** -0.5
assert _NAT_SCALE == 2.0 ** round(np.log2(_NAT_SCALE))
_GLA_COLS = 2 * QK_W + 2 * V_W
_REST_COLS = 2 * GLA_RANK + 3 * NAT_W
_REST_TILES = -(-_REST_COLS // LANES)
_PREP_ROWS = 128
assert _GLA_COLS % LANES == 0 and _GLA_COLS + _REST_COLS == 3104 and D_MODEL % _PREP_ROWS == 0


def _proj_prepare_weights(win_hbm, tail_ref, layer, w_bf, stage, sem):
    lane = lax.broadcasted_iota(jnp.int32, (_PREP_ROWS, LANES), 1)
    shift = 2 * GLA_RANK

    head = pltpu.make_async_copy(win_hbm.at[layer, :, pl.ds(0, _GLA_COLS)], stage.at[:, pl.ds(0, _GLA_COLS)],
                                 sem.at[0])
    head.start()
    head.wait()
    for r0 in range(0, D_MODEL, _PREP_ROWS):
        rows = slice(r0, r0 + _PREP_ROWS)
        w_bf[rows, _P_QK:_P_NAT] = stage[rows, 0:_GLA_COLS].astype(BF16)

    whole = (_REST_TILES - 1) * LANES
    stage[:, whole:] = tail_ref[0]
    rest = pltpu.make_async_copy(win_hbm.at[layer, :, pl.ds(_GLA_COLS, whole)], stage.at[:, pl.ds(0, whole)],
                                 sem.at[1])
    rest.start()
    rest.wait()
    for r0 in range(0, D_MODEL, _PREP_ROWS):
        rows = slice(r0, r0 + _PREP_ROWS)
        w_bf[rows, _P_LR:_P_END] = jnp.where(lane < shift, stage[rows, 0:LANES], 0.0).astype(BF16)
        for t in range(3 * NAT_W // LANES):
            lo = pltpu.roll(stage[rows, t * LANES:(t + 1) * LANES], LANES - shift, axis=1)
            hi = pltpu.roll(stage[rows, (t + 1) * LANES:(t + 2) * LANES], LANES - shift, axis=1)
            w_bf[rows, _P_NAT + t * LANES:_P_NAT + (t + 1) * LANES] = jnp.where(lane < LANES - shift, lo,
                                                                                  hi).astype(BF16)


def _rotary_partner(x):
    lane = lax.broadcasted_iota(jnp.int32, (x.shape[0], LANES), 1)
    first_of_pair = ((lane >> 4) & 1) == 0
    half = DK_GLA // 4
    tiles = []
    for t in range(x.shape[1] // LANES):
        tile = x[:, t * LANES:(t + 1) * LANES]
        tiles.append(jnp.where(first_of_pair, pltpu.roll(tile, LANES - half, axis=1), pltpu.roll(tile, half, axis=1)))
    return jnp.concatenate(tiles, axis=1)


def _proj_kernel(x_ref, mod_ref, ng_ref, win_hbm, tail_ref, wa_ref, ba_ref, cos_ref, sin_ref, *rest, layer):
    q_ref, k_ref, v_ref, r_ref, g_ref, qn_ref, kn_ctx_ref, vn_ctx_ref, kn_ref, vn_ref = rest[-13:-3]
    w_ref, stage, sem = rest[-3:]
    i = pl.program_id(0)

    @pl.when(i == 0)
    def _():
        _proj_prepare_weights(win_hbm, tail_ref, layer, w_ref, stage, sem)

    h = _rms(x_ref[...], ng_ref[2:3, :])
    h = (h * (1.0 + mod_ref[0, 4:5, :]) + mod_ref[0, 3:4, :]).astype(BF16)
    hp = _permute_chunks(_block_transpose_perm(), h)

    lr = _dot(hp, w_ref[:, _P_LR:_P_END]).astype(BF16)
    z = _dot(lr, wa_ref[...]) + ba_ref[...]
    g = (jnp.minimum(z, 0.0) - jnp.log1p(jnp.exp(-jnp.abs(z)))) * (1.0 / GATE_NORM)
    g_ref[0] = g[:, 0:QK_W]
    g_ref[1] = g[:, QK_W:2 * QK_W]

    qk = _dot(hp, w_ref[:, _P_QK:_P_VR])
    cos = cos_ref[...]
    sin = sin_ref[...]
    q_ref[...] = (qk[:, 0:QK_W] * cos + _rotary_partner(qk[:, 0:QK_W]) * sin) * (DK_GLA ** -0.5)
    k_ref[...] = qk[:, QK_W:2 * QK_W] * cos + _rotary_partner(qk[:, QK_W:2 * QK_W]) * sin

    vr = _dot(hp, w_ref[:, _P_VR:_P_NAT])
    v_ref[...] = vr[:, 0:V_W]
    r_ref[...] = vr[:, V_W:2 * V_W]

    nat = _dot(h, w_ref[:, _P_NAT:_P_LR])
    qn_ref[...] = (nat[:, 0:NAT_W] * _NAT_SCALE).astype(BF16)
    kn_ref[...] = nat[:, NAT_W:2 * NAT_W].astype(BF16)
    vn_ref[...] = nat[:, 2 * NAT_W:3 * NAT_W].astype(BF16)

    @pl.when(i < _CTX_TILES)
    def _():
        heads = (TM // SEQ, SEQ, H_NAT, HD_NAT)
        kn_ctx_ref[:, 0] = nat[:, NAT_W:2 * NAT_W].reshape(heads)
        vn_ctx_ref[:, 0] = nat[:, 2 * NAT_W:3 * NAT_W].reshape(heads)
        for later in range(1, kn_ctx_ref.shape[1]):
            kn_ctx_ref[:, later] = jnp.zeros(heads, F32)
            vn_ctx_ref[:, later] = jnp.zeros(heads, F32)


def _rope_table_block(i):
    lat_tiles = DEC_SEQ // TM
    return jnp.where(i < N_CTX // TM, lat_tiles, (i - N_CTX // TM) % lat_tiles)


def _project(x, mod_l, ng_l, w_in, w_a, b_a, cos_t, sin_t, new_cache, *, layer):
    tile = lambda w: pl.BlockSpec((TM, w), lambda i: (i, 0))
    if new_cache:
        ctx_heads = pl.BlockSpec((TM // SEQ, 1, SEQ, H_NAT, HD_NAT), lambda i: (_ctx_tile(i), layer, 0, 0, 0))
    else:
        ctx_heads = pl.BlockSpec((TM // SEQ, DEPTH, SEQ, H_NAT, HD_NAT), lambda i: (_ctx_tile(i), 0, 0, 0, 0))
    full = lambda a: pl.BlockSpec(a.shape, lambda i: (0,) * a.ndim)
    n_in = 9
    tail0 = _GLA_COLS + (_REST_TILES - 1) * LANES
    w_tail = jnp.pad(w_in[:, :, tail0:], ((0, 0), (0, 0), (0, LANES - (w_in.shape[2] - tail0))))
    return pl.pallas_call(
        functools.partial(_proj_kernel, layer=layer),
        grid=(N_ALL // TM,),
        in_specs=[
            tile(D_MODEL),
            pl.BlockSpec((1, N_MOD, D_MODEL), lambda i: (_group_of_tile(i), 0, 0)),
            full(ng_l),
            pl.BlockSpec(memory_space=pl.ANY),
            pl.BlockSpec((1, D_MODEL, LANES), lambda i: (layer, 0, 0)),
            full(w_a), full(b_a),
            pl.BlockSpec((TM, QK_W), lambda i: (_rope_table_block(i), 0)),
            pl.BlockSpec((TM, QK_W), lambda i: (_rope_table_block(i), 0)),
        ] + [pl.BlockSpec(memory_space=pl.ANY)] * len(new_cache),
        input_output_aliases={n_in + n: 6 + n for n in range(len(new_cache))},
        out_specs=[
            tile(QK_W), tile(QK_W), tile(V_W), tile(V_W),
            pl.BlockSpec((2, TM, QK_W), lambda i: (0, i, 0)),
            tile(NAT_W), ctx_heads, ctx_heads, tile(NAT_W), tile(NAT_W),
        ],
        out_shape=[
            jax.ShapeDtypeStruct((N_ALL, QK_W), F32), jax.ShapeDtypeStruct((N_ALL, QK_W), F32),
            jax.ShapeDtypeStruct((N_ALL, V_W), F32), jax.ShapeDtypeStruct((N_ALL, V_W), F32),
            jax.ShapeDtypeStruct((2, N_ALL, QK_W), F32),
            jax.ShapeDtypeStruct((N_ALL, NAT_W), BF16),
            jax.ShapeDtypeStruct((BATCH, DEPTH, SEQ, H_NAT, HD_NAT), F32),
            jax.ShapeDtypeStruct((BATCH, DEPTH, SEQ, H_NAT, HD_NAT), F32),
            jax.ShapeDtypeStruct((N_ALL, NAT_W), BF16), jax.ShapeDtypeStruct((N_ALL, NAT_W), BF16),
        ],
        scratch_shapes=[
            pltpu.VMEM((D_MODEL, _P_END), BF16),
            pltpu.VMEM((D_MODEL, _REST_TILES * LANES), F32),
            pltpu.SemaphoreType.DMA((2,)),
        ],
        compiler_params=pltpu.CompilerParams(
            dimension_semantics=("arbitrary",), vmem_limit_bytes=VMEM_LIMIT),
        name="mixer_proj",
    )(x, mod_l, ng_l, w_in, w_tail, w_a, b_a, cos_t, sin_t, *new_cache)


def _gla_tables():
    rows = []
    seq_specs = [(b * (SEQ // GLA_T), SEQ // GLA_T, 0, 0, b, 1) for b in range(BATCH)]
    seq_specs += [(N_CTX // GLA_T + b * (DEC_SEQ // GLA_T), DEC_SEQ // GLA_T, b, 1, BATCH - 1, 0)
                  for b in range(DEC_BATCH)]
    for blk0, nchunk, s0_row, has_s0, fin_row, wants_fin in seq_specs:
        states = (s0_row, has_s0, fin_row, wants_fin)
        if nchunk == 1:
            rows.append((blk0, _GLA_BOTH, 1, blk0, 0) + states)
            continue
        for direction in (0, 1):
            order = range(nchunk) if direction == 0 else range(nchunk - 1, -1, -1)
            for n, c in enumerate(order):
                out_blk = blk0 + (c if direction == 1 else nchunk - 1)
                rows.append((blk0 + c, direction, int(n == 0), out_blk, c) + states)
    return np.asarray(rows, dtype=np.int32).T.copy()


_GLA_BOTH = 2


_GLA_TAB = _gla_tables()
_GLA_ITEMS = _GLA_TAB.shape[1]
_GLA_PAIR_ROWS = GLA_NB * GLA_CB * (GLA_CB + 1) // 2


def _gla_item(direction, q_ref, k_ref, v_ref, g_ref, res, s_scr, cp, sstk, p_scr, w_scr):
    T, CB, NB = GLA_T, GLA_CB, GLA_NB
    fwd = direction == 0
    slab = lambda i: slice(i * NB, (i + 1) * NB)

    cum = None
    for i in (range(CB) if fwd else range(CB - 1, -1, -1)):
        gi = g_ref[slab(i), :]
        cum = gi if cum is None else cum + gi
        cp[slab(i), :] = cum
    total = cum
    cum_all = cp[...]
    qt = q_ref[...] * jnp.exp(cum_all)
    kh = k_ref[...] * jnp.exp(jnp.concatenate([total] * CB, axis=0) - cum_all)
    dec_t = jnp.concatenate([jnp.exp(total), jnp.zeros((LANES - NB, QK_W), F32)], axis=0).T

    erow = lax.broadcasted_iota(jnp.int32, (QK_W, QK_W), 0)
    ecol = lax.broadcasted_iota(jnp.int32, (QK_W, QK_W), 1)
    head_sum = jnp.where((erow >> 6) == (ecol >> 6), 1.0, 0.0).astype(BF16)

    key_positions = lambda i: range(i + 1) if fwd else range(i, CB)
    r0 = 0
    for i in range(CB):
        qi = q_ref[slab(i), :]
        ci = cp[slab(i), :]
        for j in key_positions(i):
            e = jnp.exp(ci - cp[slab(j), :])
            p_scr[r0:r0 + NB, :] = (qi * k_ref[slab(j), :] * e).astype(BF16)
            r0 += NB
    w_scr[...] = _dot(p_scr[...], head_sum)
    first_half = lax.broadcasted_iota(jnp.int32, (NB, LANES), 1) < DK_GLA
    r0 = 0
    for i in range(CB):
        acc = None
        for j in key_positions(i):
            spread = []
            for pair in range(H_GLA // 2):
                tile = w_scr[r0:r0 + NB, pair * LANES:(pair + 1) * LANES]
                other = pltpu.roll(tile, DK_GLA, axis=1)
                spread += [jnp.where(first_half, tile, other), jnp.where(first_half, other, tile)]
            term = jnp.concatenate(spread, axis=1) * v_ref[slab(j), :]
            acc = term if acc is None else acc + term
            r0 += NB
        res[slab(i), :] = acc

    kht = kh.T.astype(BF16)
    v_bf = v_ref[...].astype(BF16)
    key_blk = lax.broadcasted_iota(jnp.int32, (DK_GLA, T), 1) & (NB - 1)
    row_blk = lax.broadcasted_iota(jnp.int32, (T, LANES), 0) & (NB - 1)
    lane_half = lax.broadcasted_iota(jnp.int32, (T, LANES), 1) >> 6
    order = range(NB) if fwd else range(NB - 1, -1, -1)
    for h in range(H_GLA):
        kh_h = kht[h * DK_GLA:(h + 1) * DK_GLA, :]
        kv = _dot(jnp.concatenate([jnp.where(key_blk == b, kh_h, 0) for b in range(NB)], axis=0),
                  v_bf[:, h * DV_GLA:(h + 1) * DV_GLA])
        s = s_scr[h]
        for b in order:
            sstk[h, b * DK_GLA:(b + 1) * DK_GLA, :] = s.astype(BF16)
            s = dec_t[h * DK_GLA:(h + 1) * DK_GLA, b:b + 1] * s + kv[b * DK_GLA:(b + 1) * DK_GLA, :]
        s_scr[h] = s

        pair_tile = qt[:, (h // 2) * LANES:(h // 2 + 1) * LANES]
        both = jnp.where(lane_half == h % 2, pair_tile, pltpu.roll(pair_tile, DK_GLA, axis=1))
        lhs = jnp.concatenate([jnp.where(row_blk == 2 * j + lane_half, both, 0.0).astype(BF16)
                               for j in range(NB // 2)], axis=1)
        res[:, h * DV_GLA:(h + 1) * DV_GLA] += _dot(lhs, sstk[h])


def _gla_kernel(tab_ref, q_ref, k_ref, v_ref, g_ref, s0_ref, *rest):
    o_ref, so_ref, s_scr, cp, sstk, p_scr, w_scr, res, held = rest[-9:]
    it = pl.program_id(0)
    kind = tab_ref[1, it]
    slot = tab_ref[4, it]
    has_s0 = tab_ref[6, it] == 1
    wants_final = tab_ref[8, it] == 1

    def run(d):
        _gla_item(d, q_ref, k_ref, v_ref, g_ref.at[d], res.at[d], s_scr.at[d], cp.at[d], sstk.at[d],
                  p_scr.at[d], w_scr.at[d])

    def clear_later_layers():
        for later in range(1, so_ref.shape[1]):
            so_ref[0, later] = jnp.zeros(so_ref.shape[2:], F32)

    @pl.when(kind == _GLA_BOTH)
    def _():
        s_scr[...] = jnp.where(has_s0, s0_ref[0, 0], 0.0)
        run(0)
        run(1)
        o_ref[...] = res[0] + res[1]

        @pl.when(wants_final)
        def _():
            so_ref[0, 0] = s_scr[...]
            clear_later_layers()

    for d in (0, 1):
        @pl.when(kind == d)
        def _(d=d):
            @pl.when(tab_ref[2, it] == 1)
            def _():
                s_scr[d] = jnp.where(has_s0, s0_ref[0, 0, d], 0.0)

            run(d)
            if d == 0:
                held[slot] = res[0]
            else:
                o_ref[...] = held[slot] + res[1]

            @pl.when(wants_final)
            def _():
                so_ref[0, 0, d] = s_scr[d]
                if d == 0:
                    clear_later_layers()


def _gla(q, k, v, g, state_gla, new_state, *, layer):
    tok = lambda w: pl.BlockSpec((GLA_T, w), lambda it, tab: (tab[0, it], 0))
    state_blk = (2, H_GLA, DK_GLA, DV_GLA)
    if new_state:
        final_spec = pl.BlockSpec((1, 1) + state_blk, lambda it, tab: (tab[7, it], layer, 0, 0, 0, 0))
    else:
        final_spec = pl.BlockSpec((1, DEPTH) + state_blk, lambda it, tab: (tab[7, it], 0, 0, 0, 0, 0))
    grid_spec = pltpu.PrefetchScalarGridSpec(
        num_scalar_prefetch=1,
        grid=(_GLA_ITEMS,),
        in_specs=[
            tok(QK_W), tok(QK_W), tok(V_W),
            pl.BlockSpec((2, GLA_T, QK_W), lambda it, tab: (0, tab[0, it], 0)),
            pl.BlockSpec((1, 1) + state_blk, lambda it, tab: (tab[5, it], layer, 0, 0, 0, 0)),
        ] + [pl.BlockSpec(memory_space=pl.ANY)] * len(new_state),
        out_specs=[
            pl.BlockSpec((GLA_T, V_W), lambda it, tab: (tab[3, it], 0)),
            final_spec,
        ],
        scratch_shapes=[
            pltpu.VMEM((2, H_GLA, DK_GLA, DV_GLA), F32),
            pltpu.VMEM((2, GLA_T, QK_W), F32),
            pltpu.VMEM((2, H_GLA, GLA_NB * DK_GLA, DV_GLA), BF16),
            pltpu.VMEM((2, _GLA_PAIR_ROWS, QK_W), BF16),
            pltpu.VMEM((2, _GLA_PAIR_ROWS, QK_W), F32),
            pltpu.VMEM((2, GLA_T, V_W), F32),
            pltpu.VMEM((DEC_SEQ // GLA_T, GLA_T, V_W), F32),
        ],
    )
    return pl.pallas_call(
        _gla_kernel,
        grid_spec=grid_spec,
        out_shape=[
            jax.ShapeDtypeStruct((N_ALL, V_W), F32),
            jax.ShapeDtypeStruct((BATCH, DEPTH) + state_blk, F32),
        ],
        input_output_aliases={6 + n: 1 + n for n in range(len(new_state))},
        compiler_params=pltpu.CompilerParams(
            dimension_semantics=("arbitrary",), vmem_limit_bytes=VMEM_LIMIT),
        name="gla",
    )(jnp.asarray(_GLA_TAB), q, k, v, g, state_gla, *new_state)


def _with_ones(v):
    return jnp.concatenate([v, jnp.ones_like(v)], axis=1)


def _softmax_pv(s_list, v1_list):
    m = s_list[0].max(axis=-1, keepdims=True)
    for s in s_list[1:]:
        m = jnp.maximum(m, s.max(axis=-1, keepdims=True))
    acc = None
    for s, v1 in zip(s_list, v1_list):
        pv = _dot(jnp.exp(s - m).astype(BF16), v1)
        acc = pv if acc is None else acc + pv
    width = acc.shape[1] // 2
    return acc[:, :width] / acc[:, width:]


def _ctx_attn_kernel(q_ref, k_ref, v_ref, o_ref):
    lane =lax.broadcasted_iota(jnp.int32, (SEQ, LANES), 1)
    for t in range(NAT_W // LANES):
        sl = slice(t * LANES, (t + 1) * LANES)
        qt = q_ref[:, sl]
        kt = k_ref[:, sl].astype(BF16)
        vt = _with_ones(v_ref[:, sl].astype(BF16))
        out = jnp.zeros((SEQ, LANES), F32)
        for half in range(LANES // HD_NAT):
            mine = (lane >> 6) == half
            s = _dot_nt(jnp.where(mine, qt, 0).astype(BF16), kt)
            out = jnp.where(mine, _softmax_pv([s], [vt]), out)
        o_ref[:, sl] = out.astype(o_ref.dtype)


def _ctx_attention(qn, kn, vn):
    spec = pl.BlockSpec((SEQ, NAT_W), lambda b: (b, 0))
    return pl.pallas_call(
        _ctx_attn_kernel,
        grid=(BATCH,),
        in_specs=[spec, spec, spec],
        out_specs=spec,
        out_shape=jax.ShapeDtypeStruct((N_CTX, NAT_W), BF16),
        compiler_params=pltpu.CompilerParams(
            dimension_semantics=("arbitrary",), vmem_limit_bytes=VMEM_LIMIT),
        name="ctx_attention",
    )(qn, kn, vn)


_NAT_QROWS = 4
_NAT_GROUPS = GRID_ROWS // _NAT_QROWS
_NAT_KROWS = WIN_H + _NAT_QROWS
_NAT_Q = _NAT_QROWS * GRID_W
_NAT_KEYS = _NAT_KROWS * GRID_W
_NAT_DR = 2 * WIN_H - 1
_NAT_DC = 2 * WIN_W - 1


def _nat_build_bias(rpb_ref, tz_scr):
    c = lax.broadcasted_iota(jnp.int32, (GRID_W, LANES), 0)
    lane = lax.broadcasted_iota(jnp.int32, (GRID_W, LANES), 1)
    kc = lane & (GRID_W - 1)
    second = (lane >> 6) == 1
    win_start = jnp.clip(c - WIN_W // 2, 0, GRID_W - WIN_W)
    valid = (kc >= win_start) & (kc < win_start + WIN_W)

    def one_row(n, carry):
        dr = n >> 3
        h = n & (H_NAT - 1)
        src = jnp.broadcast_to(rpb_ref[pl.ds(h * _NAT_DR + dr, 1), :], (GRID_W, LANES))
        rolled = pltpu.roll(src, LANES - (WIN_W - 1), axis=1, stride=1, stride_axis=0)
        tz_scr[dr, h] = jnp.where(valid, rolled, NEG_INF)
        return carry

    lax.fori_loop(0, _NAT_DR * H_NAT, one_row, 0)

    def pair_rows(n, carry):
        dr = n >> 3
        h = n & (H_NAT - 1)
        tz_scr[dr, h] = jnp.where(second, tz_scr[dr + 1, h], tz_scr[dr, h])
        return carry

    lax.fori_loop(0, (_NAT_DR - 1) * H_NAT, pair_rows, 0)


def _nat_kernel(rpb_ref, q_ref, k_ref, v_ref, ck_ref, cv_ref, o_ref, tz_scr, bias_scr):
    grp = pl.program_id(1)

    @pl.when((pl.program_id(0) == 0) & (grp == 0))
    def _():
        _nat_build_bias(rpb_ref, tz_scr)

    for g in range(_NAT_GROUPS):
        @pl.when(grp == g)
        def _(g=g):
            _nat_group(g, q_ref, k_ref, v_ref, ck_ref, cv_ref, o_ref, tz_scr, bias_scr)


def _nat_group(g, q_ref, k_ref, v_ref, ck_ref, cv_ref, o_ref, tz_scr, bias_scr):
    q_rows = range(_NAT_QROWS * g, _NAT_QROWS * (g + 1))
    first_key_row = [min(max(r - WIN_H // 2, 0), GRID_ROWS - WIN_H) for r in q_rows]
    krow0 = min(first_key_row)
    n_krows = max(first_key_row) + WIN_H - krow0
    n_krows += n_krows % 2
    assert krow0 + n_krows <= GRID_ROWS and n_krows <= _NAT_KROWS
    keys = slice(krow0 * GRID_W, (krow0 + n_krows) * GRID_W)

    lane_q = lax.broadcasted_iota(jnp.int32, (_NAT_Q, LANES), 1)
    second = lax.broadcasted_iota(jnp.int32, (GRID_W, LANES), 1) >= GRID_W
    neg = jnp.full((GRID_W, LANES), NEG_INF, F32)
    for t in range(NAT_W // LANES):
        sl = slice(t * LANES, (t + 1) * LANES)
        qt = q_ref[:, sl]
        kw = k_ref[keys, sl].astype(BF16)
        vw = _with_ones(v_ref[keys, sl].astype(BF16))
        ck = ck_ref[0, 0, :, sl].astype(BF16)
        cv = _with_ones(cv_ref[0, 0, :, sl].astype(BF16))
        out = jnp.zeros((_NAT_Q, LANES), F32)
        for half in range(LANES // HD_NAT):
            h = 2 * t + half
            for qr, (r, lo) in enumerate(zip(q_rows, first_key_row)):
                for kp in range(n_krows // 2):
                    kr = krow0 + 2 * kp
                    ok_a, ok_b = lo <= kr < lo + WIN_H, lo <= kr + 1 < lo + WIN_H
                    if ok_a or ok_b:
                        tile = tz_scr[min(max(kr - r + (WIN_H - 1), 0), _NAT_DR - 1), h]
                        if not ok_a:
                            tile = jnp.where(second, tile, neg)
                        if not ok_b:
                            tile = jnp.where(second, neg, tile)
                    else:
                        tile = neg
                    bias_scr[qr * GRID_W:(qr + 1) * GRID_W, kp * LANES:(kp + 1) * LANES] = tile
            mine = (lane_q >> 6) == half
            qm = jnp.where(mine, qt, 0).astype(BF16)
            s_win = _dot_nt(qm, kw) + bias_scr[:, 0:n_krows * GRID_W]
            s_ctx = _dot_nt(qm, ck)
            out = jnp.where(mine, _softmax_pv([s_win, s_ctx], [vw, cv]), out)
        o_ref[:, sl] = out.astype(o_ref.dtype)


def _nat_attention(qn, kn, vn, cache_k, cache_v, rpb_l, layer):
    lat0 = N_CTX // _NAT_Q
    half = jnp.pad(rpb_l.reshape(H_NAT * _NAT_DR, _NAT_DC), ((0, 0), (0, GRID_W - _NAT_DC)))
    rpb_rows = jnp.concatenate([half, half], axis=1)
    return pl.pallas_call(
        _nat_kernel,
        grid=(DEC_BATCH, _NAT_GROUPS),
        in_specs=[
            pl.BlockSpec((H_NAT * _NAT_DR, LANES), lambda b, g: (0, 0)),
            pl.BlockSpec((_NAT_Q, NAT_W), lambda b, g: (lat0 + b * _NAT_GROUPS + g, 0)),
            pl.BlockSpec((DEC_SEQ, NAT_W), lambda b, g: (N_CTX // DEC_SEQ + b, 0)),
            pl.BlockSpec((DEC_SEQ, NAT_W), lambda b, g: (N_CTX // DEC_SEQ + b, 0)),
            pl.BlockSpec((1, 1, PAST_LEN, NAT_W), lambda b, g: (b, layer, 0, 0)),
            pl.BlockSpec((1, 1, PAST_LEN, NAT_W), lambda b, g: (b, layer, 0, 0)),
        ],
        out_specs=pl.BlockSpec((_NAT_Q, NAT_W), lambda b, g: (b * _NAT_GROUPS + g, 0)),
        out_shape=jax.ShapeDtypeStruct((N_LAT, NAT_W), BF16),
        scratch_shapes=[
            pltpu.VMEM((_NAT_DR, H_NAT, GRID_W, LANES), F32),
            pltpu.VMEM((_NAT_Q, _NAT_KEYS), F32),
        ],
        compiler_params=pltpu.CompilerParams(
            dimension_semantics=("arbitrary", "arbitrary"), vmem_limit_bytes=VMEM_LIMIT),
        name="nat_attention",
    )(rpb_rows, qn, kn, vn, cache_k, cache_v)


def _out_kernel(x_ref, o_ref, r_ref, on_ctx_ref, on_lat_ref, mod_ref, ng_ref, gng_ref, w32_ref, y_ref, w_ref):
    @pl.when(pl.program_id(0) == 0)
    def _():
        w_ref[...] = w32_ref[0].astype(BF16)

    o_nat = _read_split(pl.program_id(0), on_ctx_ref, on_lat_ref)
    og = o_ref[...]
    parts = []
    for h in range(H_GLA):
        parts.append(_rms(og[:, h * DV_GLA:(h + 1) * DV_GLA], gng_ref[...]))
    merged = (jnp.concatenate(parts, axis=1) * _silu(r_ref[...])).astype(BF16)
    merged = _permute_chunks(_block_transpose_perm(), merged)
    y = _dot(merged, w_ref[0:V_W, :]) + _dot(o_nat.astype(BF16), w_ref[V_W:, :])
    y_ref[...] = x_ref[...] + mod_ref[0, 5:6, :] * _rms(y, ng_ref[3:4, :])


def _mixer_out(x, o_gla, r, o_ctx, o_lat, mod_l, ng_l, gng_l, w_out, *, layer):
    tile = lambda w: pl.BlockSpec((TM, w), lambda i: (i, 0))
    full = lambda a: pl.BlockSpec(a.shape, lambda i: (0,) * a.ndim)
    return pl.pallas_call(
        _out_kernel,
        grid=(N_ALL // TM,),
        in_specs=[
            tile(D_MODEL),
            tile(V_W),
            tile(V_W),
            pl.BlockSpec((TM, NAT_W), lambda i: (_ctx_tile(i), 0)),
            pl.BlockSpec((TM, NAT_W), lambda i: (_lat_tile(i), 0)),
            pl.BlockSpec((1, N_MOD, D_MODEL), lambda i: (_group_of_tile(i), 0, 0)),
            full(ng_l), full(gng_l),
            pl.BlockSpec((1,) + w_out.shape[1:], lambda i: (layer, 0, 0)),
        ],
        out_specs=tile(D_MODEL),
        out_shape=jax.ShapeDtypeStruct((N_ALL, D_MODEL), F32),
        scratch_shapes=[pltpu.VMEM(w_out.shape[1:], BF16)],
        compiler_params=pltpu.CompilerParams(
            dimension_semantics=("arbitrary",), vmem_limit_bytes=VMEM_LIMIT),
        name="mixer_out",
    )(x, o_gla, r, o_ctx, o_lat, mod_l, ng_l, gng_l, w_out)


def _rope_tables():
    quarter = DK_GLA // 4
    freqs = ROPE_BASE ** (-jnp.arange(quarter, dtype=F32) / quarter)
    t = jnp.arange(DEC_SEQ)
    ang_r = (t // GRID_W).astype(F32)[:, None] * freqs[None, :]
    ang_c = (t % GRID_W).astype(F32)[:, None] * freqs[None, :]
    cos_h = jnp.concatenate([jnp.cos(ang_r), jnp.cos(ang_r), jnp.cos(ang_c), jnp.cos(ang_c)], axis=1)
    sin_h = jnp.concatenate([-jnp.sin(ang_r), jnp.sin(ang_r), -jnp.sin(ang_c), jnp.sin(ang_c)], axis=1)
    pos_major = lambda a: a.reshape(-1, GLA_NB, GLA_CB, QK_W).transpose(0, 2, 1, 3).reshape(a.shape)
    cos_t = jnp.concatenate([pos_major(jnp.tile(cos_h, (1, H_GLA))), jnp.ones((TM, QK_W), F32)], axis=0)
    sin_t = jnp.concatenate([pos_major(jnp.tile(sin_h, (1, H_GLA))), jnp.zeros((TM, QK_W), F32)], axis=0)
    return cos_t, sin_t


def _gate_up_weights(gla_wa2_l, gla_ba_l):
    w_a = jnp.zeros((LANES, 2 * QK_W), F32)
    w_a = w_a.at[0:GLA_RANK, 0:QK_W].set(gla_wa2_l[0]).at[GLA_RANK:2 * GLA_RANK, QK_W:].set(gla_wa2_l[1])
    b_a = jnp.concatenate([gla_ba_l[0], gla_ba_l[1]])[None, :]
    return w_a.astype(BF16), b_a


def kernel(x_prompt, x_sample, cache_k, cache_v, state_gla, c, c_ctx, w_mod, b_mod, norm_g, ffn_w_in, ffn_w_out,
           w_in, gla_wa2, gla_ba, gla_norm_g, nat_rpb, w_out):
    cvecs = jnp.zeros((SUBLANES, D_MODEL), F32).at[0].set(c_ctx).at[1:1 + DEC_BATCH].set(c)
    mod = _modulation(cvecs, w_mod, b_mod)[:, :N_GROUPS].reshape(DEPTH, N_GROUPS, N_MOD, D_MODEL)

    cos_t, sin_t = _rope_tables()
    ck = cache_k.reshape(DEC_BATCH, DEPTH, PAST_LEN, NAT_W)
    cv = cache_v.reshape(DEC_BATCH, DEPTH, PAST_LEN, NAT_W)

    x = (x_prompt.reshape(N_CTX, D_MODEL), x_sample.reshape(N_LAT, D_MODEL))
    new_cache, new_state = (), ()
    for l in range(DEPTH):
        mod_l, ng_l = mod[l], norm_g[l]
        x = _ffn(x, mod_l, ng_l, ffn_w_in, ffn_w_out, layer=l, which=0)

        w_a, b_a = _gate_up_weights(gla_wa2[l], gla_ba[l])
        q, k, v, r, g, qn, new_k, new_v, kn, vn = _project(
            x, mod_l, ng_l, w_in, w_a, b_a, cos_t, sin_t, new_cache, layer=l)
        new_cache = (new_k, new_v)

        o_gla, new_s = _gla(q, k, v, g, state_gla, new_state, layer=l)
        new_state = (new_s,)
        o_ctx = _ctx_attention(qn, kn, vn)
        o_lat = _nat_attention(qn, kn, vn, ck, cv, nat_rpb[l], l)

        x = _mixer_out(x, o_gla, r, o_ctx, o_lat, mod_l, ng_l, gla_norm_g[l][None, :], w_out, layer=l)
        x = _ffn(x, mod_l, ng_l, ffn_w_in, ffn_w_out, layer=l, which=1, split_out=(l == DEPTH - 1))

    y_prompt = x[0].reshape(BATCH, SEQ, D_MODEL)
    y_sample = x[1].reshape(DEC_BATCH, DEC_SEQ, D_MODEL)
    return (y_prompt, y_sample, new_cache[0], new_cache[1], new_state[0])
```

```python
import functools

import numpy as np
import jax
import jax.numpy as jnp
from jax import lax
from jax.experimental import pallas as pl
from jax.experimental.pallas import tpu as pltpu

D_MODEL = 1024
BATCH = 16
SEQ = 256
DEPTH = 2
DEC_BATCH = 2
DEC_SEQ = 1024
PAST_LEN = 512
GRID_W = 64
H_GLA = 4
DK_GLA = 64
DV_GLA = 128
GLA_RANK = 16
GATE_NORM = 16.0
H_NAT = 8
HD_NAT = 64
WIN_H = 8
WIN_W = 16
D_FF = 2816
N_MOD = 9
ROPE_BASE = 10000.0
EPS = 1e-6
NEG_INF = -1e30

F32 = jnp.float32
BF16 = jnp.bfloat16

N_CTX = BATCH * SEQ
N_LAT = DEC_BATCH * DEC_SEQ
N_ALL = N_CTX + N_LAT
N_GROUPS = 1 + DEC_BATCH
QK_W = H_GLA * DK_GLA
V_W = H_GLA * DV_GLA
NAT_W = H_NAT * HD_NAT
GRID_ROWS = DEC_SEQ // GRID_W

LANES = 128
SUBLANES = 8

TM = 512
FFN_TC = 256
FFN_NC = D_FF // FFN_TC
FFN_STAGES = 3
MOD_TN = 1152
GLA_T = 256
GLA_CB = 16
GLA_NB = GLA_T // GLA_CB
VMEM_LIMIT = 56 * 1024 * 1024

assert N_CTX % TM == 0 and DEC_SEQ % TM == 0 and D_FF % FFN_TC == 0 and FFN_TC % LANES == 0
assert SEQ == GLA_T and DEC_SEQ % GLA_T == 0


def _group_of_tile(i):
    return jnp.where(i < N_CTX // TM, 0, 1 + (i - N_CTX // TM) // (DEC_SEQ // TM))


def _dot(a, b):
    return jnp.dot(a, b, preferred_element_type=F32)


def _dot_nt(a, b):
    return lax.dot_general(a, b, (((1,), (1,)), ((), ())), preferred_element_type=F32)


def _rms(x, g):
    ms = jnp.mean(x * x, axis=-1, keepdims=True)
    return x * lax.rsqrt(ms + EPS) * g


def _silu(x):
    return x * jax.nn.sigmoid(x)


def _mod_kernel(c_ref, w_ref, b_ref, o_ref):
    s = _silu(c_ref[...]).astype(BF16)
    o_ref[0] = _dot(s, w_ref[0].astype(BF16)) + b_ref[0]


def _modulation(cvecs, w_mod, b_mod):
    n_out = N_MOD * D_MODEL
    return pl.pallas_call(
        _mod_kernel,
        grid=(DEPTH, n_out // MOD_TN),
        in_specs=[
            pl.BlockSpec((SUBLANES, D_MODEL), lambda l, j: (0, 0)),
            pl.BlockSpec((1, D_MODEL, MOD_TN), lambda l, j: (l, 0, j)),
            pl.BlockSpec((1, 1, MOD_TN), lambda l, j: (l, 0, j)),
        ],
        out_specs=pl.BlockSpec((1, SUBLANES, MOD_TN), lambda l, j: (l, 0, j)),
        out_shape=jax.ShapeDtypeStruct((DEPTH, SUBLANES, n_out), F32),
        compiler_params=pltpu.CompilerParams(
            dimension_semantics=("arbitrary", "arbitrary"), vmem_limit_bytes=VMEM_LIMIT),
        name="modulation",
    )(cvecs, w_mod, b_mod.reshape(DEPTH, 1, n_out))


_CTX_TILES = N_CTX // TM


def _ctx_tile(i):
    return jnp.minimum(i, _CTX_TILES - 1)


def _lat_tile(i):
    return jnp.maximum(i - _CTX_TILES, 0)


def _read_split(i, ctx_ref, lat_ref):
    return jnp.where(i < _CTX_TILES, ctx_ref[...], lat_ref[...])


def _ffn_kernel(*refs, m0, n0, layer, which, split_in, split_out):
    n_x = 2 if split_in else 1
    x_refs, (mod_ref, ng_ref, win_hbm, wout_hbm) = refs[:n_x], refs[n_x:n_x + 4]
    n_o = 2 if split_out else 1
    o_refs = refs[n_x + 4:n_x + 4 + n_o]
    h_scr, acc_scr, x_scr, wg_bf, wu_bf, wo_bf, stage_g, stage_u, stage_o, sem = refs[n_x + 4 + n_o:]
    i = pl.program_id(0)

    def chunk_copies(c, slot):
        cols = pl.ds(c * FFN_TC, FFN_TC)
        up_cols = pl.ds(D_FF + c * FFN_TC, FFN_TC)
        return (
            pltpu.make_async_copy(win_hbm.at[layer, which, :, cols], stage_g.at[slot], sem.at[0, slot]),
            pltpu.make_async_copy(win_hbm.at[layer, which, :, up_cols], stage_u.at[slot], sem.at[1, slot]),
            pltpu.make_async_copy(wout_hbm.at[layer, which, cols, :], stage_o.at[slot], sem.at[2, slot]),
        )

    def accumulate(c):
        h = h_scr[...]
        gate = _dot(h, wg_bf[c])
        up = _dot(h, wu_bf[c])
        part = _dot((_silu(gate) * up).astype(BF16), wo_bf[c])
        if c == 0:
            acc_scr[...] = part
        else:
            acc_scr[...] += part

    x = _read_split(i, *x_refs) if split_in else x_refs[0][...]
    x_scr[...] = x
    gain = ng_ref[n0:n0 + 1, :] * (1.0 + mod_ref[0, m0 + 1:m0 + 2, :])
    h_scr[...] = (_rms(x, gain) + mod_ref[0, m0:m0 + 1, :]).astype(BF16)

    @pl.when(i == 0)
    def _():
        for c in range(min(FFN_STAGES, FFN_NC)):
            for cp in chunk_copies(c, c):
                cp.start()
        for c in range(FFN_NC):
            slot = c % FFN_STAGES
            for cp in chunk_copies(c, slot):
                cp.wait()
            wg_bf[c] = stage_g[slot].astype(BF16)
            wu_bf[c] = stage_u[slot].astype(BF16)
            wo_bf[c] = stage_o[slot].astype(BF16)
            if c + FFN_STAGES < FFN_NC:
                for cp in chunk_copies(c + FFN_STAGES, slot):
                    cp.start()
            accumulate(c)

    @pl.when(i > 0)
    def _():
        for c in range(FFN_NC):
            accumulate(c)

    gain = 0.5 * mod_ref[0, m0 + 2:m0 + 3, :] * ng_ref[n0 + 1:n0 + 2, :]
    out = x_scr[...] + _rms(acc_scr[...], gain)
    if split_out:
        @pl.when(i < _CTX_TILES)
        def _():
            o_refs[0][...] = out

        @pl.when(i >= _CTX_TILES)
        def _():
            o_refs[1][...] = out
    else:
        o_refs[0][...] = out


def _ffn(x, mod_l, ng_l, w_in, w_out, *, layer, which, split_out=False):
    m0, n0 = (0, 0) if which == 0 else (6, 4)
    split_in = isinstance(x, tuple)
    tile = pl.BlockSpec((TM, D_MODEL), lambda i: (i, 0))
    ctx_tile = pl.BlockSpec((TM, D_MODEL), lambda i: (_ctx_tile(i), 0))
    lat_tile = pl.BlockSpec((TM, D_MODEL), lambda i: (_lat_tile(i), 0))
    if split_out:
        out_specs = [ctx_tile, lat_tile]
        out_shape = [jax.ShapeDtypeStruct((N_CTX, D_MODEL), F32), jax.ShapeDtypeStruct((N_LAT, D_MODEL), F32)]
    else:
        out_specs = tile
        out_shape = jax.ShapeDtypeStruct((N_ALL, D_MODEL), F32)
    return pl.pallas_call(
        functools.partial(_ffn_kernel, m0=m0, n0=n0, layer=layer, which=which, split_in=split_in,
                          split_out=split_out),
        grid=(N_ALL // TM,),
        in_specs=([ctx_tile, lat_tile] if split_in else [tile]) + [
            pl.BlockSpec((1, N_MOD, D_MODEL), lambda i: (_group_of_tile(i), 0, 0)),
            pl.BlockSpec((6, D_MODEL), lambda i: (0, 0)),
            pl.BlockSpec(memory_space=pl.ANY),
            pl.BlockSpec(memory_space=pl.ANY),
        ],
        out_specs=out_specs,
        out_shape=out_shape,
        scratch_shapes=[
            pltpu.VMEM((TM, D_MODEL), BF16),
            pltpu.VMEM((TM, D_MODEL), F32),
            pltpu.VMEM((TM, D_MODEL), F32),
            pltpu.VMEM((FFN_NC, D_MODEL, FFN_TC), BF16),
            pltpu.VMEM((FFN_NC, D_MODEL, FFN_TC), BF16),
            pltpu.VMEM((FFN_NC, FFN_TC, D_MODEL), BF16),
            pltpu.VMEM((FFN_STAGES, D_MODEL, FFN_TC), F32),
            pltpu.VMEM((FFN_STAGES, D_MODEL, FFN_TC), F32),
            pltpu.VMEM((FFN_STAGES, FFN_TC, D_MODEL), F32),
            pltpu.SemaphoreType.DMA((3, FFN_STAGES)),
        ],
        compiler_params=pltpu.CompilerParams(
            dimension_semantics=("arbitrary",), vmem_limit_bytes=VMEM_LIMIT),
        name="ffn",
    )(*(x if split_in else (x,)), mod_l, ng_l, w_in, w_out)


def _block_transpose_perm():
    r = lax.broadcasted_iota(jnp.int32, (GLA_T, GLA_T), 0)
    c = lax.broadcasted_iota(jnp.int32, (GLA_T, GLA_T), 1)
    return jnp.where(c == (r & (GLA_CB - 1)) * GLA_NB + (r >> 4), 1.0, 0.0).astype(BF16)


def _permute_chunks(perm, a):
    return jnp.concatenate([_dot(perm, a[c * GLA_T:(c + 1) * GLA_T, :]).astype(BF16)
                            for c in range(a.shape[0] // GLA_T)], axis=0)


_P_QK = 0
_P_VR = 2 * QK_W
_P_NAT = _P_VR + 2 * V_W
_P_LR = _P_NAT + 3 * NAT_W
_P_END = _P_LR + LANES


_NAT_SCALE = HD_NAT ** -0.5
assert _NAT_SCALE == 2.0 ** round(np.log2(_NAT_SCALE))
_GLA_COLS = 2 * QK_W + 2 * V_W
_REST_COLS = 2 * GLA_RANK + 3 * NAT_W
_REST_TILES = -(-_REST_COLS // LANES)
_PREP_ROWS = 128
assert _GLA_COLS % LANES == 0 and _GLA_COLS + _REST_COLS == 3104 and D_MODEL % _PREP_ROWS == 0


def _proj_prepare_weights(win_hbm, tail_ref, layer, w_bf, stage, sem):
    lane = lax.broadcasted_iota(jnp.int32, (_PREP_ROWS, LANES), 1)
    shift = 2 * GLA_RANK

    head = pltpu.make_async_copy(win_hbm.at[layer, :, pl.ds(0, _GLA_COLS)], stage.at[:, pl.ds(0, _GLA_COLS)],
                                 sem.at[0])
    head.start()
    head.wait()
    for r0 in range(0, D_MODEL, _PREP_ROWS):
        rows = slice(r0, r0 + _PREP_ROWS)
        w_bf[rows, _P_QK:_P_NAT] = stage[rows, 0:_GLA_COLS].astype(BF16)

    whole = (_REST_TILES - 1) * LANES
    stage[:, whole:] = tail_ref[0]
    rest = pltpu.make_async_copy(win_hbm.at[layer, :, pl.ds(_GLA_COLS, whole)], stage.at[:, pl.ds(0, whole)],
                                 sem.at[1])
    rest.start()
    rest.wait()
    for r0 in range(0, D_MODEL, _PREP_ROWS):
        rows = slice(r0, r0 + _PREP_ROWS)
        w_bf[rows, _P_LR:_P_END] = jnp.where(lane < shift, stage[rows, 0:LANES], 0.0).astype(BF16)
        for t in range(3 * NAT_W // LANES):
            lo = pltpu.roll(stage[rows, t * LANES:(t + 1) * LANES], LANES - shift, axis=1)
            hi = pltpu.roll(stage[rows, (t + 1) * LANES:(t + 2) * LANES], LANES - shift, axis=1)
            w_bf[rows, _P_NAT + t * LANES:_P_NAT + (t + 1) * LANES] = jnp.where(lane < LANES - shift, lo,
                                                                                  hi).astype(BF16)


def _rotary_partner(x):
    lane = lax.broadcasted_iota(jnp.int32, (x.shape[0], LANES), 1)
    first_of_pair = ((lane >> 4) & 1) == 0
    half = DK_GLA // 4
    tiles = []
    for t in range(x.shape[1] // LANES):
        tile = x[:, t * LANES:(t + 1) * LANES]
        tiles.append(jnp.where(first_of_pair, pltpu.roll(tile, LANES - half, axis=1), pltpu.roll(tile, half, axis=1)))
    return jnp.concatenate(tiles, axis=1)


def _proj_kernel(x_ref, mod_ref, ng_ref, win_hbm, tail_ref, wa_ref, ba_ref, cos_ref, sin_ref, *rest, layer):
    q_ref, k_ref, v_ref, r_ref, g_ref, qn_ref, kn_ctx_ref, vn_ctx_ref, kn_ref, vn_ref = rest[-13:-3]
    w_ref, stage, sem = rest[-3:]
    i = pl.program_id(0)

    @pl.when(i == 0)
    def _():
        _proj_prepare_weights(win_hbm, tail_ref, layer, w_ref, stage, sem)

    h = _rms(x_ref[...], ng_ref[2:3, :])
    h = (h * (1.0 + mod_ref[0, 4:5, :]) + mod_ref[0, 3:4, :]).astype(BF16)
    hp = _permute_chunks(_block_transpose_perm(), h)

    lr = _dot(hp, w_ref[:, _P_LR:_P_END]).astype(BF16)
    z = _dot(lr, wa_ref[...]) + ba_ref[...]
    g = (jnp.minimum(z, 0.0) - jnp.log1p(jnp.exp(-jnp.abs(z)))) * (1.0 / GATE_NORM)
    g_ref[0] = g[:, 0:QK_W]
    g_ref[1] = g[:, QK_W:2 * QK_W]

    qk = _dot(hp, w_ref[:, _P_QK:_P_VR])
    cos = cos_ref[...]
    sin = sin_ref[...]
    q_ref[...] = (qk[:, 0:QK_W] * cos + _rotary_partner(qk[:, 0:QK_W]) * sin) * (DK_GLA ** -0.5)
    k_ref[...] = qk[:, QK_W:2 * QK_W] * cos + _rotary_partner(qk[:, QK_W:2 * QK_W]) * sin

    vr = _dot(hp, w_ref[:, _P_VR:_P_NAT])
    v_ref[...] = vr[:, 0:V_W]
    r_ref[...] = vr[:, V_W:2 * V_W]

    nat = _dot(h, w_ref[:, _P_NAT:_P_LR])
    qn_ref[...] = (nat[:, 0:NAT_W] * _NAT_SCALE).astype(BF16)
    kn_ref[...] = nat[:, NAT_W:2 * NAT_W].astype(BF16)
    vn_ref[...] = nat[:, 2 * NAT_W:3 * NAT_W].astype(BF16)

    @pl.when(i < _CTX_TILES)
    def _():
        heads = (TM // SEQ, SEQ, H_NAT, HD_NAT)
        kn_ctx_ref[:, 0] = nat[:, NAT_W:2 * NAT_W].reshape(heads)
        vn_ctx_ref[:, 0] = nat[:, 2 * NAT_W:3 * NAT_W].reshape(heads)
        for later in range(1, kn_ctx_ref.shape[1]):
            kn_ctx_ref[:, later] = jnp.zeros(heads, F32)
            vn_ctx_ref[:, later] = jnp.zeros(heads, F32)


def _rope_table_block(i):
    lat_tiles = DEC_SEQ // TM
    return jnp.where(i < N_CTX // TM, lat_tiles, (i - N_CTX // TM) % lat_tiles)


def _project(x, mod_l, ng_l, w_in, w_a, b_a, cos_t, sin_t, new_cache, *, layer):
    tile = lambda w: pl.BlockSpec((TM, w), lambda i: (i, 0))
    if new_cache:
        ctx_heads = pl.BlockSpec((TM // SEQ, 1, SEQ, H_NAT, HD_NAT), lambda i: (_ctx_tile(i), layer, 0, 0, 0))
    else:
        ctx_heads = pl.BlockSpec((TM // SEQ, DEPTH, SEQ, H_NAT, HD_NAT), lambda i: (_ctx_tile(i), 0, 0, 0, 0))
    full = lambda a: pl.BlockSpec(a.shape, lambda i: (0,) * a.ndim)
    n_in = 9
    tail0 = _GLA_COLS + (_REST_TILES - 1) * LANES
    w_tail = jnp.pad(w_in[:, :, tail0:], ((0, 0), (0, 0), (0, LANES - (w_in.shape[2] - tail0))))
    return pl.pallas_call(
        functools.partial(_proj_kernel, layer=layer),
        grid=(N_ALL // TM,),
        in_specs=[
            tile(D_MODEL),
            pl.BlockSpec((1, N_MOD, D_MODEL), lambda i: (_group_of_tile(i), 0, 0)),
            full(ng_l),
            pl.BlockSpec(memory_space=pl.ANY),
            pl.BlockSpec((1, D_MODEL, LANES), lambda i: (layer, 0, 0)),
            full(w_a), full(b_a),
            pl.BlockSpec((TM, QK_W), lambda i: (_rope_table_block(i), 0)),
            pl.BlockSpec((TM, QK_W), lambda i: (_rope_table_block(i), 0)),
        ] + [pl.BlockSpec(memory_space=pl.ANY)] * len(new_cache),
        input_output_aliases={n_in + n: 6 + n for n in range(len(new_cache))},
        out_specs=[
            tile(QK_W), tile(QK_W), tile(V_W), tile(V_W),
            pl.BlockSpec((2, TM, QK_W), lambda i: (0, i, 0)),
            tile(NAT_W), ctx_heads, ctx_heads, tile(NAT_W), tile(NAT_W),
        ],
        out_shape=[
            jax.ShapeDtypeStruct((N_ALL, QK_W), F32), jax.ShapeDtypeStruct((N_ALL, QK_W), F32),
            jax.ShapeDtypeStruct((N_ALL, V_W), F32), jax.ShapeDtypeStruct((N_ALL, V_W), F32),
            jax.ShapeDtypeStruct((2, N_ALL, QK_W), F32),
            jax.ShapeDtypeStruct((N_ALL, NAT_W), BF16),
            jax.ShapeDtypeStruct((BATCH, DEPTH, SEQ, H_NAT, HD_NAT), F32),
            jax.ShapeDtypeStruct((BATCH, DEPTH, SEQ, H_NAT, HD_NAT), F32),
            jax.ShapeDtypeStruct((N_ALL, NAT_W), BF16), jax.ShapeDtypeStruct((N_ALL, NAT_W), BF16),
        ],
        scratch_shapes=[
            pltpu.VMEM((D_MODEL, _P_END), BF16),
            pltpu.VMEM((D_MODEL, _REST_TILES * LANES), F32),
            pltpu.SemaphoreType.DMA((2,)),
        ],
        compiler_params=pltpu.CompilerParams(
            dimension_semantics=("arbitrary",), vmem_limit_bytes=VMEM_LIMIT),
        name="mixer_proj",
    )(x, mod_l, ng_l, w_in, w_tail, w_a, b_a, cos_t, sin_t, *new_cache)


def _gla_tables():
    rows = []
    seq_specs = [(b * (SEQ // GLA_T), SEQ // GLA_T, 0, 0, b, 1) for b in range(BATCH)]
    seq_specs += [(N_CTX // GLA_T + b * (DEC_SEQ // GLA_T), DEC_SEQ // GLA_T, b, 1, BATCH - 1, 0)
                  for b in range(DEC_BATCH)]
    for blk0, nchunk, s0_row, has_s0, fin_row, wants_fin in seq_specs:
        states = (s0_row, has_s0, fin_row, wants_fin)
        if nchunk == 1:
            rows.append((blk0, _GLA_BOTH, 1, blk0, 0) + states)
            continue
        for direction in (0, 1):
            order = range(nchunk) if direction == 0 else range(nchunk - 1, -1, -1)
            for n, c in enumerate(order):
                out_blk = blk0 + (c if direction == 1 else nchunk - 1)
                rows.append((blk0 + c, direction, int(n == 0), out_blk, c) + states)
    return np.asarray(rows, dtype=np.int32).T.copy()


_GLA_BOTH = 2


_GLA_TAB = _gla_tables()
_GLA_ITEMS = _GLA_TAB.shape[1]
_GLA_PAIR_ROWS = GLA_NB * GLA_CB * (GLA_CB + 1) // 2


def _gla_item(direction, q_ref, k_ref, v_ref, g_ref, res, s_scr, cp, sstk, p_scr, w_scr):
    T, CB, NB = GLA_T, GLA_CB, GLA_NB
    fwd = direction == 0
    slab = lambda i: slice(i * NB, (i + 1) * NB)

    cum = None
    for i in (range(CB) if fwd else range(CB - 1, -1, -1)):
        gi = g_ref[slab(i), :]
        cum = gi if cum is None else cum + gi
        cp[slab(i), :] = cum
    total = cum
    cum_all = cp[...]
    qt = q_ref[...] * jnp.exp(cum_all)
    kh = k_ref[...] * jnp.exp(jnp.concatenate([total] * CB, axis=0) - cum_all)
    dec_t = jnp.concatenate([jnp.exp(total), jnp.zeros((LANES - NB, QK_W), F32)], axis=0).T

    erow = lax.broadcasted_iota(jnp.int32, (QK_W, QK_W), 0)
    ecol = lax.broadcasted_iota(jnp.int32, (QK_W, QK_W), 1)
    head_sum = jnp.where((erow >> 6) == (ecol >> 6), 1.0, 0.0).astype(BF16)

    key_positions = lambda i: range(i + 1) if fwd else range(i, CB)
    r0 = 0
    for i in range(CB):
        qi = q_ref[slab(i), :]
        ci = cp[slab(i), :]
        for j in key_positions(i):
            e = jnp.exp(ci - cp[slab(j), :])
            p_scr[r0:r0 + NB, :] = (qi * k_ref[slab(j), :] * e).astype(BF16)
            r0 += NB
    w_scr[...] = _dot(p_scr[...], head_sum)
    first_half = lax.broadcasted_iota(jnp.int32, (NB, LANES), 1) < DK_GLA
    r0 = 0
    for i in range(CB):
        acc = None
        for j in key_positions(i):
            spread = []
            for pair in range(H_GLA // 2):
                tile = w_scr[r0:r0 + NB, pair * LANES:(pair + 1) * LANES]
                other = pltpu.roll(tile, DK_GLA, axis=1)
                spread += [jnp.where(first_half, tile, other), jnp.where(first_half, other, tile)]
            term = jnp.concatenate(spread, axis=1) * v_ref[slab(j), :]
            acc = term if acc is None else acc + term
            r0 += NB
        res[slab(i), :] = acc

    kht = kh.T.astype(BF16)
    v_bf = v_ref[...].astype(BF16)
    key_blk = lax.broadcasted_iota(jnp.int32, (DK_GLA, T), 1) & (NB - 1)
    row_blk = lax.broadcasted_iota(jnp.int32, (T, LANES), 0) & (NB - 1)
    lane_half = lax.broadcasted_iota(jnp.int32, (T, LANES), 1) >> 6
    order = range(NB) if fwd else range(NB - 1, -1, -1)
    for h in range(H_GLA):
        kh_h = kht[h * DK_GLA:(h + 1) * DK_GLA, :]
        kv = _dot(jnp.concatenate([jnp.where(key_blk == b, kh_h, 0) for b in range(NB)], axis=0),
                  v_bf[:, h * DV_GLA:(h + 1) * DV_GLA])
        s = s_scr[h]
        for b in order:
            sstk[h, b * DK_GLA:(b + 1) * DK_GLA, :] = s.astype(BF16)
            s = dec_t[h * DK_GLA:(h + 1) * DK_GLA, b:b + 1] * s + kv[b * DK_GLA:(b + 1) * DK_GLA, :]
        s_scr[h] = s

        pair_tile = qt[:, (h // 2) * LANES:(h // 2 + 1) * LANES]
        both = jnp.where(lane_half == h % 2, pair_tile, pltpu.roll(pair_tile, DK_GLA, axis=1))
        lhs = jnp.concatenate([jnp.where(row_blk == 2 * j + lane_half, both, 0.0).astype(BF16)
                               for j in range(NB // 2)], axis=1)
        res[:, h * DV_GLA:(h + 1) * DV_GLA] += _dot(lhs, sstk[h])


def _gla_kernel(tab_ref, q_ref, k_ref, v_ref, g_ref, s0_ref, *rest):
    o_ref, so_ref, s_scr, cp, sstk, p_scr, w_scr, res, held = rest[-9:]
    it = pl.program_id(0)
    kind = tab_ref[1, it]
    slot = tab_ref[4, it]
    has_s0 = tab_ref[6, it] == 1
    wants_final = tab_ref[8, it] == 1

    def run(d):
        _gla_item(d, q_ref, k_ref, v_ref, g_ref.at[d], res.at[d], s_scr.at[d], cp.at[d], sstk.at[d],
                  p_scr.at[d], w_scr.at[d])

    def clear_later_layers():
        for later in range(1, so_ref.shape[1]):
            so_ref[0, later] = jnp.zeros(so_ref.shape[2:], F32)

    @pl.when(kind == _GLA_BOTH)
    def _():
        s_scr[...] = jnp.where(has_s0, s0_ref[0, 0], 0.0)
        run(0)
        run(1)
        o_ref[...] = res[0] + res[1]

        @pl.when(wants_final)
        def _():
            so_ref[0, 0] = s_scr[...]
            clear_later_layers()

    for d in (0, 1):
        @pl.when(kind == d)
        def _(d=d):
            @pl.when(tab_ref[2, it] == 1)
            def _():
                s_scr[d] = jnp.where(has_s0, s0_ref[0, 0, d], 0.0)

            run(d)
            if d == 0:
                held[slot] = res[0]
            else:
                o_ref[...] = held[slot] + res[1]

            @pl.when(wants_final)
            def _():
                so_ref[0, 0, d] = s_scr[d]
                if d == 0:
                    clear_later_layers()


def _gla(q, k, v, g, state_gla, new_state, *, layer):
    tok = lambda w: pl.BlockSpec((GLA_T, w), lambda it, tab: (tab[0, it], 0))
    state_blk = (2, H_GLA, DK_GLA, DV_GLA)
    if new_state:
        final_spec = pl.BlockSpec((1, 1) + state_blk, lambda it, tab: (tab[7, it], layer, 0, 0, 0, 0))
    else:
        final_spec = pl.BlockSpec((1, DEPTH) + state_blk, lambda it, tab: (tab[7, it], 0, 0, 0, 0, 0))
    grid_spec = pltpu.PrefetchScalarGridSpec(
        num_scalar_prefetch=1,
        grid=(_GLA_ITEMS,),
        in_specs=[
            tok(QK_W), tok(QK_W), tok(V_W),
            pl.BlockSpec((2, GLA_T, QK_W), lambda it, tab: (0, tab[0, it], 0)),
            pl.BlockSpec((1, 1) + state_blk, lambda it, tab: (tab[5, it], layer, 0, 0, 0, 0)),
        ] + [pl.BlockSpec(memory_space=pl.ANY)] * len(new_state),
        out_specs=[
            pl.BlockSpec((GLA_T, V_W), lambda it, tab: (tab[3, it], 0)),
            final_spec,
        ],
        scratch_shapes=[
            pltpu.VMEM((2, H_GLA, DK_GLA, DV_GLA), F32),
            pltpu.VMEM((2, GLA_T, QK_W), F32),
            pltpu.VMEM((2, H_GLA, GLA_NB * DK_GLA, DV_GLA), BF16),
            pltpu.VMEM((2, _GLA_PAIR_ROWS, QK_W), BF16),
            pltpu.VMEM((2, _GLA_PAIR_ROWS, QK_W), F32),
            pltpu.VMEM((2, GLA_T, V_W), F32),
            pltpu.VMEM((DEC_SEQ // GLA_T, GLA_T, V_W), F32),
        ],
    )
    return pl.pallas_call(
        _gla_kernel,
        grid_spec=grid_spec,
        out_shape=[
            jax.ShapeDtypeStruct((N_ALL, V_W), F32),
            jax.ShapeDtypeStruct((BATCH, DEPTH) + state_blk, F32),
        ],
        input_output_aliases={6 + n: 1 + n for n in range(len(new_state))},
        compiler_params=pltpu.CompilerParams(
            dimension_semantics=("arbitrary",), vmem_limit_bytes=VMEM_LIMIT),
        name="gla",
    )(jnp.asarray(_GLA_TAB), q, k, v, g, state_gla, *new_state)


def _with_ones(v):
    return jnp.concatenate([v, jnp.ones_like(v)], axis=1)


def _softmax_pv(s_list, v1_list):
    m = s_list[0].max(axis=-1, keepdims=True)
    for s in s_list[1:]:
        m = jnp.maximum(m, s.max(axis=-1, keepdims=True))
    acc = None
    for s, v1 in zip(s_list, v1_list):
        pv = _dot(jnp.exp(s - m).astype(BF16), v1)
        acc = pv if acc is None else acc + pv
    width = acc.shape[1] // 2
    return acc[:, :width] / acc[:, width:]


def _ctx_attn_kernel(q_ref, k_ref, v_ref, o_ref):
    lane =lax.broadcasted_iota(jnp.int32, (SEQ, LANES), 1)
    for t in range(NAT_W // LANES):
        sl = slice(t * LANES, (t + 1) * LANES)
        qt = q_ref[:, sl]
        kt = k_ref[:, sl].astype(BF16)
        vt = _with_ones(v_ref[:, sl].astype(BF16))
        out = jnp.zeros((SEQ, LANES), F32)
        for half in range(LANES // HD_NAT):
            mine = (lane >> 6) == half
            s = _dot_nt(jnp.where(mine, qt, 0).astype(BF16), kt)
            out = jnp.where(mine, _softmax_pv([s], [vt]), out)
        o_ref[:, sl] = out.astype(o_ref.dtype)


def _ctx_attention(qn, kn, vn):
    spec = pl.BlockSpec((SEQ, NAT_W), lambda b: (b, 0))
    return pl.pallas_call(
        _ctx_attn_kernel,
        grid=(BATCH,),
        in_specs=[spec, spec, spec],
        out_specs=spec,
        out_shape=jax.ShapeDtypeStruct((N_CTX, NAT_W), BF16),
        compiler_params=pltpu.CompilerParams(
            dimension_semantics=("arbitrary",), vmem_limit_bytes=VMEM_LIMIT),
        name="ctx_attention",
    )(qn, kn, vn)


_NAT_QROWS = 4
_NAT_GROUPS = GRID_ROWS // _NAT_QROWS
_NAT_KROWS = WIN_H + _NAT_QROWS
_NAT_Q = _NAT_QROWS * GRID_W
_NAT_KEYS = _NAT_KROWS * GRID_W
_NAT_DR = 2 * WIN_H - 1
_NAT_DC = 2 * WIN_W - 1


def _nat_build_bias(rpb_ref, tz_scr):
    c = lax.broadcasted_iota(jnp.int32, (GRID_W, LANES), 0)
    lane = lax.broadcasted_iota(jnp.int32, (GRID_W, LANES), 1)
    kc = lane & (GRID_W - 1)
    second = (lane >> 6) == 1
    win_start = jnp.clip(c - WIN_W // 2, 0, GRID_W - WIN_W)
    valid = (kc >= win_start) & (kc < win_start + WIN_W)

    def one_row(n, carry):
        dr = n >> 3
        h = n & (H_NAT - 1)
        src = jnp.broadcast_to(rpb_ref[pl.ds(h * _NAT_DR + dr, 1), :], (GRID_W, LANES))
        rolled = pltpu.roll(src, LANES - (WIN_W - 1), axis=1, stride=1, stride_axis=0)
        tz_scr[dr, h] = jnp.where(valid, rolled, NEG_INF)
        return carry

    lax.fori_loop(0, _NAT_DR * H_NAT, one_row, 0)

    def pair_rows(n, carry):
        dr = n >> 3
        h = n & (H_NAT - 1)
        tz_scr[dr, h] = jnp.where(second, tz_scr[dr + 1, h], tz_scr[dr, h])
        return carry

    lax.fori_loop(0, (_NAT_DR - 1) * H_NAT, pair_rows, 0)


def _nat_kernel(rpb_ref, q_ref, k_ref, v_ref, ck_ref, cv_ref, o_ref, tz_scr, bias_scr):
    grp = pl.program_id(1)

    @pl.when((pl.program_id(0) == 0) & (grp == 0))
    def _():
        _nat_build_bias(rpb_ref, tz_scr)

    for g in range(_NAT_GROUPS):
        @pl.when(grp == g)
        def _(g=g):
            _nat_group(g, q_ref, k_ref, v_ref, ck_ref, cv_ref, o_ref, tz_scr, bias_scr)


def _nat_group(g, q_ref, k_ref, v_ref, ck_ref, cv_ref, o_ref, tz_scr, bias_scr):
    q_rows = range(_NAT_QROWS * g, _NAT_QROWS * (g + 1))
    first_key_row = [min(max(r - WIN_H // 2, 0), GRID_ROWS - WIN_H) for r in q_rows]
    krow0 = min(first_key_row)
    n_krows = max(first_key_row) + WIN_H - krow0
    n_krows += n_krows % 2
    assert krow0 + n_krows <= GRID_ROWS and n_krows <= _NAT_KROWS
    keys = slice(krow0 * GRID_W, (krow0 + n_krows) * GRID_W)

    lane_q = lax.broadcasted_iota(jnp.int32, (_NAT_Q, LANES), 1)
    second = lax.broadcasted_iota(jnp.int32, (GRID_W, LANES), 1) >= GRID_W
    neg = jnp.full((GRID_W, LANES), NEG_INF, F32)
    for t in range(NAT_W // LANES):
        sl = slice(t * LANES, (t + 1) * LANES)
        qt = q_ref[:, sl]
        kw = k_ref[keys, sl].astype(BF16)
        vw = _with_ones(v_ref[keys, sl].astype(BF16))
        ck = ck_ref[0, 0, :, sl].astype(BF16)
        cv = _with_ones(cv_ref[0, 0, :, sl].astype(BF16))
        out = jnp.zeros((_NAT_Q, LANES), F32)
        for half in range(LANES // HD_NAT):
            h = 2 * t + half
            for qr, (r, lo) in enumerate(zip(q_rows, first_key_row)):
                for kp in range(n_krows // 2):
                    kr = krow0 + 2 * kp
                    ok_a, ok_b = lo <= kr < lo + WIN_H, lo <= kr + 1 < lo + WIN_H
                    if ok_a or ok_b:
                        tile = tz_scr[min(max(kr - r + (WIN_H - 1), 0), _NAT_DR - 1), h]
                        if not ok_a:
                            tile = jnp.where(second, tile, neg)
                        if not ok_b:
                            tile = jnp.where(second, neg, tile)
                    else:
                        tile = neg
                    bias_scr[qr * GRID_W:(qr + 1) * GRID_W, kp * LANES:(kp + 1) * LANES] = tile
            mine = (lane_q >> 6) == half
            qm = jnp.where(mine, qt, 0).astype(BF16)
            s_win = _dot_nt(qm, kw) + bias_scr[:, 0:n_krows * GRID_W]
            s_ctx = _dot_nt(qm, ck)
            out = jnp.where(mine, _softmax_pv([s_win, s_ctx], [vw, cv]), out)
        o_ref[:, sl] = out.astype(o_ref.dtype)


def _nat_attention(qn, kn, vn, cache_k, cache_v, rpb_l, layer):
    lat0 = N_CTX // _NAT_Q
    half = jnp.pad(rpb_l.reshape(H_NAT * _NAT_DR, _NAT_DC), ((0, 0), (0, GRID_W - _NAT_DC)))
    rpb_rows = jnp.concatenate([half, half], axis=1)
    return pl.pallas_call(
        _nat_kernel,
        grid=(DEC_BATCH, _NAT_GROUPS),
        in_specs=[
            pl.BlockSpec((H_NAT * _NAT_DR, LANES), lambda b, g: (0, 0)),
            pl.BlockSpec((_NAT_Q, NAT_W), lambda b, g: (lat0 + b * _NAT_GROUPS + g, 0)),
            pl.BlockSpec((DEC_SEQ, NAT_W), lambda b, g: (N_CTX // DEC_SEQ + b, 0)),
            pl.BlockSpec((DEC_SEQ, NAT_W), lambda b, g: (N_CTX // DEC_SEQ + b, 0)),
            pl.BlockSpec((1, 1, PAST_LEN, NAT_W), lambda b, g: (b, layer, 0, 0)),
            pl.BlockSpec((1, 1, PAST_LEN, NAT_W), lambda b, g: (b, layer, 0, 0)),
        ],
        out_specs=pl.BlockSpec((_NAT_Q, NAT_W), lambda b, g: (b * _NAT_GROUPS + g, 0)),
        out_shape=jax.ShapeDtypeStruct((N_LAT, NAT_W), BF16),
        scratch_shapes=[
            pltpu.VMEM((_NAT_DR, H_NAT, GRID_W, LANES), F32),
            pltpu.VMEM((_NAT_Q, _NAT_KEYS), F32),
        ],
        compiler_params=pltpu.CompilerParams(
            dimension_semantics=("arbitrary", "arbitrary"), vmem_limit_bytes=VMEM_LIMIT),
        name="nat_attention",
    )(rpb_rows, qn, kn, vn, cache_k, cache_v)


def _out_kernel(x_ref, o_ref, r_ref, on_ctx_ref, on_lat_ref, mod_ref, ng_ref, gng_ref, w32_ref, y_ref, w_ref):
    @pl.when(pl.program_id(0) == 0)
    def _():
        w_ref[...] = w32_ref[0].astype(BF16)

    o_nat = _read_split(pl.program_id(0), on_ctx_ref, on_lat_ref)
    og = o_ref[...]
    parts = []
    for h in range(H_GLA):
        parts.append(_rms(og[:, h * DV_GLA:(h + 1) * DV_GLA], gng_ref[...]))
    merged = (jnp.concatenate(parts, axis=1) * _silu(r_ref[...])).astype(BF16)
    merged = _permute_chunks(_block_transpose_perm(), merged)
    y = _dot(merged, w_ref[0:V_W, :]) + _dot(o_nat.astype(BF16), w_ref[V_W:, :])
    y_ref[...] = x_ref[...] + mod_ref[0, 5:6, :] * _rms(y, ng_ref[3:4, :])


def _mixer_out(x, o_gla, r, o_ctx, o_lat, mod_l, ng_l, gng_l, w_out, *, layer):
    tile = lambda w: pl.BlockSpec((TM, w), lambda i: (i, 0))
    full = lambda a: pl.BlockSpec(a.shape, lambda i: (0,) * a.ndim)
    return pl.pallas_call(
        _out_kernel,
        grid=(N_ALL // TM,),
        in_specs=[
            tile(D_MODEL),
            tile(V_W),
            tile(V_W),
            pl.BlockSpec((TM, NAT_W), lambda i: (_ctx_tile(i), 0)),
            pl.BlockSpec((TM, NAT_W), lambda i: (_lat_tile(i), 0)),
            pl.BlockSpec((1, N_MOD, D_MODEL), lambda i: (_group_of_tile(i), 0, 0)),
            full(ng_l), full(gng_l),
            pl.BlockSpec((1,) + w_out.shape[1:], lambda i: (layer, 0, 0)),
        ],
        out_specs=tile(D_MODEL),
        out_shape=jax.ShapeDtypeStruct((N_ALL, D_MODEL), F32),
        scratch_shapes=[pltpu.VMEM(w_out.shape[1:], BF16)],
        compiler_params=pltpu.CompilerParams(
            dimension_semantics=("arbitrary",), vmem_limit_bytes=VMEM_LIMIT),
        name="mixer_out",
    )(x, o_gla, r, o_ctx, o_lat, mod_l, ng_l, gng_l, w_out)


def _rope_tables():
    quarter = DK_GLA // 4
    freqs = ROPE_BASE ** (-jnp.arange(quarter, dtype=F32) / quarter)
    t = jnp.arange(DEC_SEQ)
    ang_r = (t // GRID_W).astype(F32)[:, None] * freqs[None, :]
    ang_c = (t % GRID_W).astype(F32)[:, None] * freqs[None, :]
    cos_h = jnp.concatenate([jnp.cos(ang_r), jnp.cos(ang_r), jnp.cos(ang_c), jnp.cos(ang_c)], axis=1)
    sin_h = jnp.concatenate([-jnp.sin(ang_r), jnp.sin(ang_r), -jnp.sin(ang_c), jnp.sin(ang_c)], axis=1)
    pos_major = lambda a: a.reshape(-1, GLA_NB, GLA_CB, QK_W).transpose(0, 2, 1, 3).reshape(a.shape)
    cos_t = jnp.concatenate([pos_major(jnp.tile(cos_h, (1, H_GLA))), jnp.ones((TM, QK_W), F32)], axis=0)
    sin_t = jnp.concatenate([pos_major(jnp.tile(sin_h, (1, H_GLA))), jnp.zeros((TM, QK_W), F32)], axis=0)
    return cos_t, sin_t


def _gate_up_weights(gla_wa2_l, gla_ba_l):
    w_a = jnp.zeros((LANES, 2 * QK_W), F32)
    w_a = w_a.at[0:GLA_RANK, 0:QK_W].set(gla_wa2_l[0]).at[GLA_RANK:2 * GLA_RANK, QK_W:].set(gla_wa2_l[1])
    b_a = jnp.concatenate([gla_ba_l[0], gla_ba_l[1]])[None, :]
    return w_a.astype(BF16), b_a


def kernel(x_prompt, x_sample, cache_k, cache_v, state_gla, c, c_ctx, w_mod, b_mod, norm_g, ffn_w_in, ffn_w_out,
           w_in, gla_wa2, gla_ba, gla_norm_g, nat_rpb, w_out):
    cvecs = jnp.zeros((SUBLANES, D_MODEL), F32).at[0].set(c_ctx).at[1:1 + DEC_BATCH].set(c)
    mod = _modulation(cvecs, w_mod, b_mod)[:, :N_GROUPS].reshape(DEPTH, N_GROUPS, N_MOD, D_MODEL)

    cos_t, sin_t = _rope_tables()
    ck = cache_k.reshape(DEC_BATCH, DEPTH, PAST_LEN, NAT_W)
    cv = cache_v.reshape(DEC_BATCH, DEPTH, PAST_LEN, NAT_W)

    x = (x_prompt.reshape(N_CTX, D_MODEL), x_sample.reshape(N_LAT, D_MODEL))
    new_cache, new_state = (), ()
    for l in range(DEPTH):
        mod_l, ng_l = mod[l], norm_g[l]
        x = _ffn(x, mod_l, ng_l, ffn_w_in, ffn_w_out, layer=l, which=0)

        w_a, b_a = _gate_up_weights(gla_wa2[l], gla_ba[l])
        q, k, v, r, g, qn, new_k, new_v, kn, vn = _project(
            x, mod_l, ng_l, w_in, w_a, b_a, cos_t, sin_t, new_cache, layer=l)
        new_cache = (new_k, new_v)

        o_gla, new_s = _gla(q, k, v, g, state_gla, new_state, layer=l)
        new_state = (new_s,)
        o_ctx = _ctx_attention(qn, kn, vn)
        o_lat = _nat_attention(qn, kn, vn, ck, cv, nat_rpb[l], l)

        x = _mixer_out(x, o_gla, r, o_ctx, o_lat, mod_l, ng_l, gla_norm_g[l][None, :], w_out, layer=l)
        x = _ffn(x, mod_l, ng_l, ffn_w_in, ffn_w_out, layer=l, which=1, split_out=(l == DEPTH - 1))

    y_prompt = x[0].reshape(BATCH, SEQ, D_MODEL)
    y_sample = x[1].reshape(DEC_BATCH, DEC_SEQ, D_MODEL)
    return (y_prompt, y_sample, new_cache[0], new_cache[1], new_state[0])
```
